```python
import math
import jax
import jax.numpy as jnp
from jax import lax
import numpy as np

D_MODEL = 1024
BATCH = 8
SEQ = 4096
DEPTH = 2

GRID_W = 64
CTX_LEN = 256
HEAD_DIM = 64
ROPE_BASE = 10000.0
EPS = 1e-6
BLK = 128
A_HEADS = 8
A_KV_HEADS = 2
WINDOW = 128
SSM_HEADS = 16
SSM_HEADDIM = 64
SSM_INNER = SSM_HEADS * SSM_HEADDIM
SSM_GROUPS = 2
SSM_STATE = 128
SSM_CONV = 3
SSM_CHUNK = 128
SSM_BC = SSM_GROUPS * SSM_STATE
SSM_XBC = SSM_INNER + 2 * SSM_BC
C_HEADS = 8
C_KV_HEADS = 2
D_FF = 2816
FFN_CONV = 3
A_Q_W = A_HEADS * HEAD_DIM
A_KV_W = A_KV_HEADS * HEAD_DIM
C_Q_W = C_HEADS * HEAD_DIM
C_KV_W = C_KV_HEADS * HEAD_DIM
IN_WIDTH = (A_Q_W + 2 * A_KV_W) + (SSM_INNER + SSM_XBC + 2 * SSM_HEADS) + (C_Q_W + 2 * C_KV_W) + 3 * D_MODEL

kernel_name = 'hybrid_swa_ssd_gridattn_convffn_prefix'


def _in_spans():
    sizes = (('a_q', A_Q_W), ('a_k', A_KV_W), ('a_v', A_KV_W),
             ('b_z', SSM_INNER), ('b_xbc', SSM_XBC), ('b_dt', 2 * SSM_HEADS),
             ('c_q', C_Q_W), ('c_k', C_KV_W), ('c_v', C_KV_W),
             ('gates', 3 * D_MODEL))
    spans, start = {}, 0
    for name, n in sizes:
        spans[name] = (start, start + n)
        start += n
    return spans


def rms_norm(x, g):
    xf = x.astype(jnp.float32)
    y = xf * lax.rsqrt(jnp.mean(xf * xf, axis=-1, keepdims=True) + EPS)
    return (y * g.astype(jnp.float32)).astype(x.dtype)


def modulate(h, shift, scale):
    return h * (1 + scale) + shift


def grid_rope(rows):
    t_row = jnp.repeat(jnp.arange(rows), GRID_W).astype(jnp.float32)
    t_col = jnp.tile(jnp.arange(GRID_W), rows).astype(jnp.float32)
    n = HEAD_DIM // 4
    inv = ROPE_BASE ** (-jnp.arange(n, dtype=jnp.float32) / n)
    ang = jnp.concatenate([t_row[:, None] * inv, t_col[:, None] * inv], axis=-1)
    return jnp.cos(ang), jnp.sin(ang)


def apply_rope(x, cos, sin):
    b, L, h, dh = x.shape
    xr = x.astype(jnp.float32).reshape(b, L, h, dh // 2, 2)
    c = cos[None, :, None, :]
    s = sin[None, :, None, :]
    x1, x2 = xr[..., 0], xr[..., 1]
    out = jnp.stack([x1 * c - x2 * s, x1 * s + x2 * c], axis=-1)
    return out.reshape(b, L, h, dh).astype(x.dtype)


def dwconv(u, w, bias):
    k = w.shape[0]
    pad = k // 2
    y = lax.conv_general_dilated(u, w[:, None, :].astype(u.dtype), window_strides=(1,),
                                 padding=[(pad, pad)], dimension_numbers=('NWC', 'WIO', 'NWC'),
                                 feature_group_count=u.shape[-1])
    return y + bias.astype(u.dtype)


def dense_attention(q, k, v, sink):
    b, lq, hq, dh = q.shape
    hkv = k.shape[2]
    g = hq // hkv
    qg = q.reshape(b, lq, hkv, g, dh)
    s = jnp.einsum('bqhgd,bkhd->bhgqk', qg, k).astype(jnp.float32) * (dh ** -0.5)
    if sink is not None:
        s_sink = jnp.broadcast_to(sink.astype(jnp.float32).reshape(1, hkv, g, 1, 1), s.shape[:-1] + (1,))
        s = jnp.concatenate([s, s_sink], axis=-1)
    p = jax.nn.softmax(s, axis=-1)
    if sink is not None:
        p = p[..., :-1]
    o = jnp.einsum('bhgqk,bkhd->bqhgd', p.astype(v.dtype), v)
    return o.reshape(b, lq, hq * dh)


def window_attention(q, k, v, kc, vc, sink):
    b, L, hq, dh = q.shape
    hkv = k.shape[2]
    g = hq // hkv
    nb = L // BLK
    lc = kc.shape[1]
    qb = q.reshape(b, nb, BLK, hkv, g, dh)
    pad = ((0, 0), (BLK, BLK), (0, 0), (0, 0))
    kp = jnp.pad(k, pad).reshape(b, nb + 2, BLK, hkv, dh)
    vp = jnp.pad(v, pad).reshape(b, nb + 2, BLK, hkv, dh)
    kw = jnp.concatenate([kp[:, :-2], kp[:, 1:-1], kp[:, 2:]], axis=2)
    vw = jnp.concatenate([vp[:, :-2], vp[:, 1:-1], vp[:, 2:]], axis=2)
    scale = dh ** -0.5
    s_loc = jnp.einsum('bnqhgd,bnjhd->bnhgqj', qb, kw).astype(jnp.float32) * scale
    s_ctx = jnp.einsum('bnqhgd,bchd->bnhgqc', qb, kc).astype(jnp.float32) * scale
    q_pos = (jnp.arange(nb) * BLK)[:, None] + jnp.arange(BLK)[None, :]
    k_pos = (jnp.arange(nb) * BLK - BLK)[:, None] + jnp.arange(3 * BLK)[None, :]
    rel = q_pos[:, :, None] - k_pos[:, None, :]
    valid = (jnp.abs(rel) <= WINDOW) & (k_pos[:, None, :] >= 0) & (k_pos[:, None, :] < L)
    s_loc = jnp.where(valid[None, :, None, None], s_loc, -jnp.inf)
    s_sink = jnp.broadcast_to(sink.astype(jnp.float32).reshape(1, 1, hkv, g, 1, 1), s_loc.shape[:-1] + (1,))
    p = jax.nn.softmax(jnp.concatenate([s_loc, s_ctx, s_sink], axis=-1), axis=-1)
    nloc = 3 * BLK
    p_loc = p[..., :nloc].astype(v.dtype)
    p_ctx = p[..., nloc:nloc + lc].astype(v.dtype)
    o = jnp.einsum('bnhgqj,bnjhd->bnqhgd', p_loc, vw) + jnp.einsum('bnhgqc,bchd->bnqhgd', p_ctx, vc)
    return o.reshape(b, L, hq * dh)


def grid_attention(q, k, v, kc, vc):
    b, L, hq, dh = q.shape
    nb = L // BLK
    k_all = jnp.concatenate([k, kc], axis=1)
    v_all = jnp.concatenate([v, vc], axis=1)
    qb = jnp.moveaxis(q.reshape(b, nb, BLK, hq, dh), 1, 0)
    ob = lax.map(lambda qi: dense_attention(qi, k_all, v_all, None), qb)
    return jnp.moveaxis(ob, 0, 1).reshape(b, L, hq * dh)


def ssm_inputs(xbc_raw, dt_raw, conv_w, conv_b, dt_bias):
    b, L, _ = xbc_raw.shape
    xbc = jax.nn.silu(dwconv(xbc_raw, conv_w, conv_b))
    xs = xbc[..., :SSM_INNER].reshape(b, L, SSM_HEADS, SSM_HEADDIM)
    bm = xbc[..., SSM_INNER:SSM_INNER + SSM_BC].reshape(b, L, SSM_GROUPS, SSM_STATE)
    cm = xbc[..., SSM_INNER + SSM_BC:].reshape(b, L, SSM_GROUPS, SSM_STATE)
    dt = jax.nn.softplus(dt_raw.astype(jnp.float32).reshape(b, L, 2, SSM_HEADS) + dt_bias.astype(jnp.float32))
    return xs, bm, cm, dt


def ssd_scan(xh, dt, a_coef, bm, cm, h0, want_y):
    b, L, H, P = xh.shape
    G, N = bm.shape[2], bm.shape[3]
    hg = H // G
    q = SSM_CHUNK
    nc = L // q
    f32 = jnp.float32
    xdt = (xh.astype(f32) * dt[..., None]).reshape(b, nc, q, G, hg, P)
    bc = bm.astype(f32).reshape(b, nc, q, G, N)
    cc = cm.astype(f32).reshape(b, nc, q, G, N)
    acs = jnp.cumsum((dt * a_coef.astype(f32)).reshape(b, nc, q, G, hg), axis=2)
    decay_end = jnp.exp(acs[:, :, -1:] - acs)
    states = jnp.einsum('bcjgn,bcjghp->bcghpn', bc, xdt * decay_end[..., None])
    chunk_decay = jnp.exp(acs[:, :, -1])

    def step(h, inp):
        st, dec = inp
        return h * dec[..., None, None] + st, h

    h_last, h_in = lax.scan(step, h0.reshape(b, G, hg, P, N),
                            (jnp.moveaxis(states, 1, 0), jnp.moveaxis(chunk_decay, 1, 0)))
    h_last = h_last.reshape(b, H, P, N)
    if not want_y:
        return None, h_last
    h_in = jnp.moveaxis(h_in, 0, 1)
    seg = acs[:, :, :, None] - acs[:, :, None, :]
    tri = jnp.tril(jnp.ones((q, q), dtype=bool))
    lmat = jnp.exp(jnp.where(tri[:, :, None, None], seg, -jnp.inf))
    cb = jnp.einsum('bcign,bcjgn->bcijg', cc, bc)
    y_diag = jnp.einsum('bcijgh,bcjghp->bcighp', cb[..., None] * lmat, xdt)
    y_off = jnp.einsum('bcign,bcghpn->bcighp', cc, h_in) * jnp.exp(acs)[..., None]
    y = (y_diag + y_off).reshape(b, L, H, P).astype(xh.dtype)
    return y, h_last


def ssm_output(yf, yb, xs, z, d_skip, norm_g):
    b, L = xs.shape[0], xs.shape[1]
    y = yf + yb + xs * d_skip[:, None].astype(xs.dtype)
    y = y.reshape(b, L, SSM_INNER)
    return rms_norm(y * jax.nn.silu(z), norm_g)


def merge_branches(ya, yb, yc, g_raw, w_oa, w_ob, w_oc, w_out):
    g = jax.nn.sigmoid(g_raw.astype(jnp.float32)).astype(ya.dtype)
    ga, gb, gc = jnp.split(g, 3, axis=-1)
    m = ga * (ya @ w_oa) + gb * (yb @ w_ob) + gc * (yc @ w_oc)
    return m @ w_out


def conv_ffn(h, w_up, w_gate, conv_w, conv_b, w_down):
    up = h @ w_up
    gt = dwconv(h @ w_gate, conv_w, conv_b)
    return (jax.nn.silu(gt) * up) @ w_down


def hybrid_layer(x, xc, c_mod, cc_mod, cos, sin, w_in, norm1, norm2, a_sink, ssm_conv_w, ssm_conv_b,
                 ssm_a_log, ssm_dt_bias, ssm_d, ssm_norm, c_q_norm, c_k_norm, w_oa, w_ob, w_oc, w_out,
                 ffn_w_up, ffn_w_gate, ffn_conv_w, ffn_conv_b, ffn_w_down, need_ctx_out):
    b, L, _ = x.shape
    lc = xc.shape[1]
    sp = _in_spans()
    shift1, scale1, gate1, shift2, scale2, gate2 = jnp.split(c_mod[:, None, :], 6, axis=-1)
    h = modulate(rms_norm(x, norm1), shift1, scale1)
    hc = modulate(rms_norm(xc, norm1), cc_mod[:D_MODEL], cc_mod[D_MODEL:2 * D_MODEL])
    u = h @ w_in

    def lat(name):
        return u[..., sp[name][0]:sp[name][1]]

    def ctxp(name):
        return hc @ w_in[:, sp[name][0]:sp[name][1]]

    qa = apply_rope(lat('a_q').reshape(b, L, A_HEADS, HEAD_DIM), cos, sin)
    ka = apply_rope(lat('a_k').reshape(b, L, A_KV_HEADS, HEAD_DIM), cos, sin)
    va = lat('a_v').reshape(b, L, A_KV_HEADS, HEAD_DIM)
    kac = ctxp('a_k').reshape(b, lc, A_KV_HEADS, HEAD_DIM)
    vac = ctxp('a_v').reshape(b, lc, A_KV_HEADS, HEAD_DIM)
    ya = window_attention(qa, ka, va, kac, vac, a_sink)

    qg = apply_rope(rms_norm(lat('c_q').reshape(b, L, C_HEADS, HEAD_DIM), c_q_norm), cos, sin)
    kg = apply_rope(rms_norm(lat('c_k').reshape(b, L, C_KV_HEADS, HEAD_DIM), c_k_norm), cos, sin)
    vg = lat('c_v').reshape(b, L, C_KV_HEADS, HEAD_DIM)
    kgc = rms_norm(ctxp('c_k').reshape(b, lc, C_KV_HEADS, HEAD_DIM), c_k_norm)
    vgc = ctxp('c_v').reshape(b, lc, C_KV_HEADS, HEAD_DIM)
    yg = grid_attention(qg, kg, vg, kgc, vgc)

    a_coef = -jnp.exp(ssm_a_log.astype(jnp.float32))
    xs_c, bm_c, cm_c, dt_c = ssm_inputs(ctxp('b_xbc'), ctxp('b_dt'), ssm_conv_w, ssm_conv_b, ssm_dt_bias)
    h0 = jnp.zeros((b, SSM_HEADS, SSM_HEADDIM, SSM_STATE), jnp.float32)
    yf_c, hf_c = ssd_scan(xs_c, dt_c[:, :, 0], a_coef[0], bm_c, cm_c, h0, need_ctx_out)
    yb_c, hb_c = ssd_scan(jnp.flip(xs_c, 1), jnp.flip(dt_c[:, :, 1], 1), a_coef[1],
                          jnp.flip(bm_c, 1), jnp.flip(cm_c, 1), h0, need_ctx_out)
    xs, bm, cm, dt = ssm_inputs(lat('b_xbc'), lat('b_dt'), ssm_conv_w, ssm_conv_b, ssm_dt_bias)
    yf, _ = ssd_scan(xs, dt[:, :, 0], a_coef[0], bm, cm, hf_c, True)
    yb, _ = ssd_scan(jnp.flip(xs, 1), jnp.flip(dt[:, :, 1], 1), a_coef[1],
                     jnp.flip(bm, 1), jnp.flip(cm, 1), hb_c, True)
    ys = ssm_output(yf, jnp.flip(yb, 1), xs, lat('b_z'), ssm_d, ssm_norm)

    x = x + gate1 * merge_branches(ya, ys, yg, lat('gates'), w_oa, w_ob, w_oc, w_out)
    h2 = modulate(rms_norm(x, norm2), shift2, scale2)
    x = x + gate2 * conv_ffn(h2, ffn_w_up, ffn_w_gate, ffn_conv_w, ffn_conv_b, ffn_w_down)
    if not need_ctx_out:
        return x, None

    cgate1 = cc_mod[2 * D_MODEL:3 * D_MODEL]
    cshift2 = cc_mod[3 * D_MODEL:4 * D_MODEL]
    cscale2 = cc_mod[4 * D_MODEL:5 * D_MODEL]
    cgate2 = cc_mod[5 * D_MODEL:]
    yac = dense_attention(ctxp('a_q').reshape(b, lc, A_HEADS, HEAD_DIM), kac, vac, a_sink)
    ygc = dense_attention(rms_norm(ctxp('c_q').reshape(b, lc, C_HEADS, HEAD_DIM), c_q_norm), kgc, vgc, None)
    ysc = ssm_output(yf_c, jnp.flip(yb_c, 1), xs_c, ctxp('b_z'), ssm_d, ssm_norm)
    xc = xc + cgate1 * merge_branches(yac, ysc, ygc, ctxp('gates'), w_oa, w_ob, w_oc, w_out)
    h2c = modulate(rms_norm(xc, norm2), cshift2, cscale2)
    xc = xc + cgate2 * conv_ffn(h2c, ffn_w_up, ffn_w_gate, ffn_conv_w, ffn_conv_b, ffn_w_down)
    return x, xc


def _fwd_setup_inputs(seed: int = 0) -> dict:
    key = jax.random.key(seed)
    ks = jax.random.split(key, 32)
    f32 = jnp.float32

    def nrm(k, shape, scale):
        return jax.random.normal(k, shape, f32) * scale

    d = D_MODEL
    dt0 = jnp.exp(jax.random.uniform(ks[11], (DEPTH, 2, SSM_HEADS), f32, math.log(1e-3), math.log(1e-1)))
    return {
        'x': nrm(ks[0], (BATCH, SEQ, d), 1.0),
        'c': nrm(ks[1], (BATCH, d), 1.0),
        'ctx': nrm(ks[2], (BATCH, CTX_LEN, d), 1.0),
        'c_ctx': nrm(ks[3], (d,), 1.0),
        'w_mod': nrm(ks[4], (DEPTH, d, 6 * d), 0.5 * d ** -0.5),
        'b_mod': nrm(ks[5], (DEPTH, 6 * d), 0.02),
        'norm1': 1.0 + nrm(ks[6], (DEPTH, d), 0.05),
        'norm2': 1.0 + nrm(ks[7], (DEPTH, d), 0.05),
        'w_in': nrm(ks[8], (DEPTH, d, IN_WIDTH), d ** -0.5),
        'a_sink': nrm(ks[9], (DEPTH, A_HEADS), 0.5),
        'ssm_conv_w': nrm(ks[10], (DEPTH, SSM_CONV, SSM_XBC), SSM_CONV ** -0.5),
        'ssm_conv_b': nrm(ks[12], (DEPTH, SSM_XBC), 0.02),
        'ssm_A_log': jnp.log(jax.random.uniform(ks[13], (DEPTH, 2, SSM_HEADS), f32, 1.0, 16.0)),
        'ssm_dt_bias': dt0 + jnp.log(-jnp.expm1(-dt0)),
        'ssm_D': 1.0 + nrm(ks[14], (DEPTH, SSM_HEADS), 0.1),
        'ssm_norm': 1.0 + nrm(ks[15], (DEPTH, SSM_INNER), 0.05),
        'c_q_norm': 1.0 + nrm(ks[16], (DEPTH, HEAD_DIM), 0.05),
        'c_k_norm': 1.0 + nrm(ks[17], (DEPTH, HEAD_DIM), 0.05),
        'w_oa': nrm(ks[18], (DEPTH, A_Q_W, d), A_Q_W ** -0.5),
        'w_ob': nrm(ks[19], (DEPTH, SSM_INNER, d), SSM_INNER ** -0.5),
        'w_oc': nrm(ks[20], (DEPTH, C_Q_W, d), C_Q_W ** -0.5),
        'w_out': nrm(ks[21], (DEPTH, d, d), d ** -0.5),
        'ffn_w_up': nrm(ks[22], (DEPTH, d, D_FF), d ** -0.5),
        'ffn_w_gate': nrm(ks[23], (DEPTH, d, D_FF), d ** -0.5),
        'ffn_conv_w': nrm(ks[24], (DEPTH, FFN_CONV, D_FF), FFN_CONV ** -0.5),
        'ffn_conv_b': nrm(ks[25], (DEPTH, D_FF), 0.02),
        'ffn_w_down': nrm(ks[26], (DEPTH, D_FF, d), D_FF ** -0.5),
        'final_norm': 1.0 + nrm(ks[27], (d,), 0.05),
    }


def _fwd_reference(x, c, ctx, c_ctx, w_mod, b_mod, norm1, norm2, w_in, a_sink, ssm_conv_w, ssm_conv_b,
              ssm_A_log, ssm_dt_bias, ssm_D, ssm_norm, c_q_norm, c_k_norm, w_oa, w_ob, w_oc, w_out,
              ffn_w_up, ffn_w_gate, ffn_conv_w, ffn_conv_b, ffn_w_down, final_norm):
    rows = x.shape[1] // GRID_W
    cos, sin = grid_rope(rows)
    xc = ctx
    sc = jax.nn.silu(c)
    scc = jax.nn.silu(c_ctx)
    for l in range(DEPTH):
        need_ctx_out = l < DEPTH - 1
        n_cmod = 6 * D_MODEL if need_ctx_out else 2 * D_MODEL
        c_mod = sc @ w_mod[l] + b_mod[l]
        cc_mod = scc @ w_mod[l][:, :n_cmod] + b_mod[l][:n_cmod]
        x, xc = hybrid_layer(x, xc, c_mod, cc_mod, cos, sin, w_in[l], norm1[l], norm2[l], a_sink[l],
                             ssm_conv_w[l], ssm_conv_b[l], ssm_A_log[l], ssm_dt_bias[l], ssm_D[l],
                             ssm_norm[l], c_q_norm[l], c_k_norm[l], w_oa[l], w_ob[l], w_oc[l], w_out[l],
                             ffn_w_up[l], ffn_w_gate[l], ffn_conv_w[l], ffn_conv_b[l], ffn_w_down[l],
                             need_ctx_out)
    return rms_norm(x, final_norm)


import jax as _jax
import jax.numpy as _jnp

TWIN_FORMAT = 'train_step'
FWD_PARAMS = ['x', 'c', 'ctx', 'c_ctx', 'w_mod', 'b_mod', 'norm1', 'norm2', 'w_in', 'a_sink', 'ssm_conv_w', 'ssm_conv_b', 'ssm_A_log', 'ssm_dt_bias', 'ssm_D', 'ssm_norm', 'c_q_norm', 'c_k_norm', 'w_oa', 'w_ob', 'w_oc', 'w_out', 'ffn_w_up', 'ffn_w_gate', 'ffn_conv_w', 'ffn_conv_b', 'ffn_w_down', 'final_norm']
TWIN_WEIGHTS = ['c_ctx', 'w_mod', 'b_mod', 'norm1', 'norm2', 'w_in', 'a_sink', 'ssm_conv_w', 'ssm_conv_b', 'ssm_A_log', 'ssm_dt_bias', 'ssm_D', 'ssm_norm', 'c_q_norm', 'c_k_norm', 'w_oa', 'w_ob', 'w_oc', 'w_out', 'ffn_w_up', 'ffn_w_gate', 'ffn_conv_w', 'ffn_conv_b', 'ffn_w_down', 'final_norm']
TWIN_DIFF_INPUT = 'x'
TWIN_INPUTS = ['x', 'c', 'ctx', 'c_ctx', 'w_mod', 'b_mod', 'norm1', 'norm2', 'w_in', 'a_sink', 'ssm_conv_w', 'ssm_conv_b', 'ssm_A_log', 'ssm_dt_bias', 'ssm_D', 'ssm_norm', 'c_q_norm', 'c_k_norm', 'w_oa', 'w_ob', 'w_oc', 'w_out', 'ffn_w_up', 'ffn_w_gate', 'ffn_conv_w', 'ffn_conv_b', 'ffn_w_down', 'final_norm', 'loss_target', 'm_c_ctx', 'm_w_mod', 'm_b_mod', 'm_norm1', 'm_norm2', 'm_w_in', 'm_a_sink', 'm_ssm_conv_w', 'm_ssm_conv_b', 'm_ssm_A_log', 'm_ssm_dt_bias', 'm_ssm_D', 'm_ssm_norm', 'm_c_q_norm', 'm_c_k_norm', 'm_w_oa', 'm_w_ob', 'm_w_oc', 'm_w_out', 'm_ffn_w_up', 'm_ffn_w_gate', 'm_ffn_conv_w', 'm_ffn_conv_b', 'm_ffn_w_down', 'm_final_norm', 'v_c_ctx', 'v_w_mod', 'v_b_mod', 'v_norm1', 'v_norm2', 'v_w_in', 'v_a_sink', 'v_ssm_conv_w', 'v_ssm_conv_b', 'v_ssm_A_log', 'v_ssm_dt_bias', 'v_ssm_D', 'v_ssm_norm', 'v_c_q_norm', 'v_c_k_norm', 'v_w_oa', 'v_w_ob', 'v_w_oc', 'v_w_out', 'v_ffn_w_up', 'v_ffn_w_gate', 'v_ffn_conv_w', 'v_ffn_conv_b', 'v_ffn_w_down', 'v_final_norm']
TWIN_OUTPUTS = ['loss', 'grad_x', 'grad_c_ctx', 'grad_w_mod', 'grad_b_mod', 'grad_norm1', 'grad_norm2', 'grad_w_in', 'grad_a_sink', 'grad_ssm_conv_w', 'grad_ssm_conv_b', 'grad_ssm_A_log', 'grad_ssm_dt_bias', 'grad_ssm_D', 'grad_ssm_norm', 'grad_c_q_norm', 'grad_c_k_norm', 'grad_w_oa', 'grad_w_ob', 'grad_w_oc', 'grad_w_out', 'grad_ffn_w_up', 'grad_ffn_w_gate', 'grad_ffn_conv_w', 'grad_ffn_conv_b', 'grad_ffn_w_down', 'grad_final_norm', 'delta_c_ctx', 'delta_w_mod', 'delta_b_mod', 'delta_norm1', 'delta_norm2', 'delta_w_in', 'delta_a_sink', 'delta_ssm_conv_w', 'delta_ssm_conv_b', 'delta_ssm_A_log', 'delta_ssm_dt_bias', 'delta_ssm_D', 'delta_ssm_norm', 'delta_c_q_norm', 'delta_c_k_norm', 'delta_w_oa', 'delta_w_ob', 'delta_w_oc', 'delta_w_out', 'delta_ffn_w_up', 'delta_ffn_w_gate', 'delta_ffn_conv_w', 'delta_ffn_conv_b', 'delta_ffn_w_down', 'delta_final_norm', 'new_m_c_ctx', 'new_m_w_mod', 'new_m_b_mod', 'new_m_norm1', 'new_m_norm2', 'new_m_w_in', 'new_m_a_sink', 'new_m_ssm_conv_w', 'new_m_ssm_conv_b', 'new_m_ssm_A_log', 'new_m_ssm_dt_bias', 'new_m_ssm_D', 'new_m_ssm_norm', 'new_m_c_q_norm', 'new_m_c_k_norm', 'new_m_w_oa', 'new_m_w_ob', 'new_m_w_oc', 'new_m_w_out', 'new_m_ffn_w_up', 'new_m_ffn_w_gate', 'new_m_ffn_conv_w', 'new_m_ffn_conv_b', 'new_m_ffn_w_down', 'new_m_final_norm', 'new_v_c_ctx', 'new_v_w_mod', 'new_v_b_mod', 'new_v_norm1', 'new_v_norm2', 'new_v_w_in', 'new_v_a_sink', 'new_v_ssm_conv_w', 'new_v_ssm_conv_b', 'new_v_ssm_A_log', 'new_v_ssm_dt_bias', 'new_v_ssm_D', 'new_v_ssm_norm', 'new_v_c_q_norm', 'new_v_c_k_norm', 'new_v_w_oa', 'new_v_w_ob', 'new_v_w_oc', 'new_v_w_out', 'new_v_ffn_w_up', 'new_v_ffn_w_gate', 'new_v_ffn_conv_w', 'new_v_ffn_conv_b', 'new_v_ffn_w_down', 'new_v_final_norm']
TWIN_LEAF_KINDS = {'loss': 'loss', 'grad_x': 'grad_x', 'grad_c_ctx': 'grad_w', 'grad_w_mod': 'grad_w', 'grad_b_mod': 'grad_w', 'grad_norm1': 'grad_w', 'grad_norm2': 'grad_w', 'grad_w_in': 'grad_w', 'grad_a_sink': 'grad_w', 'grad_ssm_conv_w': 'grad_w', 'grad_ssm_conv_b': 'grad_w', 'grad_ssm_A_log': 'grad_w', 'grad_ssm_dt_bias': 'grad_w', 'grad_ssm_D': 'grad_w', 'grad_ssm_norm': 'grad_w', 'grad_c_q_norm': 'grad_w', 'grad_c_k_norm': 'grad_w', 'grad_w_oa': 'grad_w', 'grad_w_ob': 'grad_w', 'grad_w_oc': 'grad_w', 'grad_w_out': 'grad_w', 'grad_ffn_w_up': 'grad_w', 'grad_ffn_w_gate': 'grad_w', 'grad_ffn_conv_w': 'grad_w', 'grad_ffn_conv_b': 'grad_w', 'grad_ffn_w_down': 'grad_w', 'grad_final_norm': 'grad_w', 'delta_c_ctx': 'delta_w', 'delta_w_mod': 'delta_w', 'delta_b_mod': 'delta_w', 'delta_norm1': 'delta_w', 'delta_norm2': 'delta_w', 'delta_w_in': 'delta_w', 'delta_a_sink': 'delta_w', 'delta_ssm_conv_w': 'delta_w', 'delta_ssm_conv_b': 'delta_w', 'delta_ssm_A_log': 'delta_w', 'delta_ssm_dt_bias': 'delta_w', 'delta_ssm_D': 'delta_w', 'delta_ssm_norm': 'delta_w', 'delta_c_q_norm': 'delta_w', 'delta_c_k_norm': 'delta_w', 'delta_w_oa': 'delta_w', 'delta_w_ob': 'delta_w', 'delta_w_oc': 'delta_w', 'delta_w_out': 'delta_w', 'delta_ffn_w_up': 'delta_w', 'delta_ffn_w_gate': 'delta_w', 'delta_ffn_conv_w': 'delta_w', 'delta_ffn_conv_b': 'delta_w', 'delta_ffn_w_down': 'delta_w', 'delta_final_norm': 'delta_w', 'new_m_c_ctx': 'new_m', 'new_m_w_mod': 'new_m', 'new_m_b_mod': 'new_m', 'new_m_norm1': 'new_m', 'new_m_norm2': 'new_m', 'new_m_w_in': 'new_m', 'new_m_a_sink': 'new_m', 'new_m_ssm_conv_w': 'new_m', 'new_m_ssm_conv_b': 'new_m', 'new_m_ssm_A_log': 'new_m', 'new_m_ssm_dt_bias': 'new_m', 'new_m_ssm_D': 'new_m', 'new_m_ssm_norm': 'new_m', 'new_m_c_q_norm': 'new_m', 'new_m_c_k_norm': 'new_m', 'new_m_w_oa': 'new_m', 'new_m_w_ob': 'new_m', 'new_m_w_oc': 'new_m', 'new_m_w_out': 'new_m', 'new_m_ffn_w_up': 'new_m', 'new_m_ffn_w_gate': 'new_m', 'new_m_ffn_conv_w': 'new_m', 'new_m_ffn_conv_b': 'new_m', 'new_m_ffn_w_down': 'new_m', 'new_m_final_norm': 'new_m', 'new_v_c_ctx': 'new_v', 'new_v_w_mod': 'new_v', 'new_v_b_mod': 'new_v', 'new_v_norm1': 'new_v', 'new_v_norm2': 'new_v', 'new_v_w_in': 'new_v', 'new_v_a_sink': 'new_v', 'new_v_ssm_conv_w': 'new_v', 'new_v_ssm_conv_b': 'new_v', 'new_v_ssm_A_log': 'new_v', 'new_v_ssm_dt_bias': 'new_v', 'new_v_ssm_D': 'new_v', 'new_v_ssm_norm': 'new_v', 'new_v_c_q_norm': 'new_v', 'new_v_c_k_norm': 'new_v', 'new_v_w_oa': 'new_v', 'new_v_w_ob': 'new_v', 'new_v_w_oc': 'new_v', 'new_v_w_out': 'new_v', 'new_v_ffn_w_up': 'new_v', 'new_v_ffn_w_gate': 'new_v', 'new_v_ffn_conv_w': 'new_v', 'new_v_ffn_conv_b': 'new_v', 'new_v_ffn_w_down': 'new_v', 'new_v_final_norm': 'new_v'}


def _forward(args):
    return _fwd_reference(*[args[k] for k in FWD_PARAMS])


def _output_shape():
    def fwd():
        inp = _fwd_setup_inputs(0)
        return _fwd_reference(*[inp[k] for k in FWD_PARAMS])
    out = _jax.eval_shape(fwd)
    return out.shape, out.dtype

N_MICROBATCH = 1
ADAM_LR = 0.001
ADAM_B1 = 0.9
ADAM_B2 = 0.999
ADAM_EPS = 1e-08
ADAM_WD = 0.01
ADAM_STEP = 10
PER_EXAMPLE_BATCH_AXIS = {'x': 0, 'c': 0, 'ctx': 0, 'loss_target': 0}
SHARED_INPUTS = []
_WEIGHT_DTYPES = {'c_ctx': _jnp.float32, 'w_mod': _jnp.float32, 'b_mod': _jnp.float32, 'norm1': _jnp.float32, 'norm2': _jnp.float32, 'w_in': _jnp.float32, 'a_sink': _jnp.float32, 'ssm_conv_w': _jnp.float32, 'ssm_conv_b': _jnp.float32, 'ssm_A_log': _jnp.float32, 'ssm_dt_bias': _jnp.float32, 'ssm_D': _jnp.float32, 'ssm_norm': _jnp.float32, 'c_q_norm': _jnp.float32, 'c_k_norm': _jnp.float32, 'w_oa': _jnp.float32, 'w_ob': _jnp.float32, 'w_oc': _jnp.float32, 'w_out': _jnp.float32, 'ffn_w_up': _jnp.float32, 'ffn_w_gate': _jnp.float32, 'ffn_conv_w': _jnp.float32, 'ffn_conv_b': _jnp.float32, 'ffn_w_down': _jnp.float32, 'final_norm': _jnp.float32}
MOMENT_SCALE = {'c_ctx': 1.160728e-02, 'w_mod': 5.787884e-02, 'b_mod': 1.067634e-01, 'norm1': 4.342582e-02, 'norm2': 5.341786e-02, 'w_in': 1.797465e-02, 'a_sink': 1.666856e-04, 'ssm_conv_w': 2.640772e-02, 'ssm_conv_b': 3.451570e-02, 'ssm_A_log': 7.602101e-02, 'ssm_dt_bias': 6.086160e-02, 'ssm_D': 9.622283e-02, 'ssm_norm': 3.070481e-02, 'c_q_norm': 1.080371e-02, 'c_k_norm': 1.112896e-02, 'w_oa': 8.077760e-03, 'w_ob': 2.999616e-02, 'w_oc': 9.885335e-03, 'w_out': 3.265752e-02, 'ffn_w_up': 2.328026e-02, 'ffn_w_gate': 2.388682e-02, 'ffn_conv_w': 2.378112e-02, 'ffn_conv_b': 2.073281e-02, 'ffn_w_down': 3.869146e-02, 'final_norm': 3.204828e+01}


def _to_microbatches(a, axis):
    t = _jnp.moveaxis(a, axis, 0)
    t = t.reshape((N_MICROBATCH, t.shape[0] // N_MICROBATCH) + t.shape[1:])
    return _jnp.moveaxis(t, 1, axis + 1)


def setup_inputs(seed: int = 0) -> dict:
    inp = _fwd_setup_inputs(seed)
    key = _jax.random.fold_in(_jax.random.key(seed), 7919)
    shape, _ = _output_shape()
    out = dict(inp)
    out["loss_target"] = _jax.random.normal(_jax.random.fold_in(key, 0), shape, _jnp.float32)
    for i, name in enumerate(TWIN_WEIGHTS):
        w = inp[name].astype(_jnp.float32)
        if MOMENT_SCALE is None:
            s = _jnp.sqrt(_jnp.mean(_jnp.square(w)) + 1e-30)
        else:
            s = MOMENT_SCALE[name]
        km, kv = _jax.random.split(_jax.random.fold_in(key, i + 1))
        out[name] = w
        out["m_" + name] = s * _jax.random.normal(km, w.shape, _jnp.float32)
        out["v_" + name] = (s * s) * _jax.random.uniform(kv, w.shape, _jnp.float32, 0.5, 1.5)
    if N_MICROBATCH > 1:
        for name, axis in PER_EXAMPLE_BATCH_AXIS.items():
            out[name] = _to_microbatches(out[name], axis)
    return {'x': out['x'], 'c': out['c'], 'ctx': out['ctx'], 'c_ctx': out['c_ctx'], 'w_mod': out['w_mod'], 'b_mod': out['b_mod'], 'norm1': out['norm1'], 'norm2': out['norm2'], 'w_in': out['w_in'], 'a_sink': out['a_sink'], 'ssm_conv_w': out['ssm_conv_w'], 'ssm_conv_b': out['ssm_conv_b'], 'ssm_A_log': out['ssm_A_log'], 'ssm_dt_bias': out['ssm_dt_bias'], 'ssm_D': out['ssm_D'], 'ssm_norm': out['ssm_norm'], 'c_q_norm': out['c_q_norm'], 'c_k_norm': out['c_k_norm'], 'w_oa': out['w_oa'], 'w_ob': out['w_ob'], 'w_oc': out['w_oc'], 'w_out': out['w_out'], 'ffn_w_up': out['ffn_w_up'], 'ffn_w_gate': out['ffn_w_gate'], 'ffn_conv_w': out['ffn_conv_w'], 'ffn_conv_b': out['ffn_conv_b'], 'ffn_w_down': out['ffn_w_down'], 'final_norm': out['final_norm'], 'loss_target': out['loss_target'], 'm_c_ctx': out['m_c_ctx'], 'm_w_mod': out['m_w_mod'], 'm_b_mod': out['m_b_mod'], 'm_norm1': out['m_norm1'], 'm_norm2': out['m_norm2'], 'm_w_in': out['m_w_in'], 'm_a_sink': out['m_a_sink'], 'm_ssm_conv_w': out['m_ssm_conv_w'], 'm_ssm_conv_b': out['m_ssm_conv_b'], 'm_ssm_A_log': out['m_ssm_A_log'], 'm_ssm_dt_bias': out['m_ssm_dt_bias'], 'm_ssm_D': out['m_ssm_D'], 'm_ssm_norm': out['m_ssm_norm'], 'm_c_q_norm': out['m_c_q_norm'], 'm_c_k_norm': out['m_c_k_norm'], 'm_w_oa': out['m_w_oa'], 'm_w_ob': out['m_w_ob'], 'm_w_oc': out['m_w_oc'], 'm_w_out': out['m_w_out'], 'm_ffn_w_up': out['m_ffn_w_up'], 'm_ffn_w_gate': out['m_ffn_w_gate'], 'm_ffn_conv_w': out['m_ffn_conv_w'], 'm_ffn_conv_b': out['m_ffn_conv_b'], 'm_ffn_w_down': out['m_ffn_w_down'], 'm_final_norm': out['m_final_norm'], 'v_c_ctx': out['v_c_ctx'], 'v_w_mod': out['v_w_mod'], 'v_b_mod': out['v_b_mod'], 'v_norm1': out['v_norm1'], 'v_norm2': out['v_norm2'], 'v_w_in': out['v_w_in'], 'v_a_sink': out['v_a_sink'], 'v_ssm_conv_w': out['v_ssm_conv_w'], 'v_ssm_conv_b': out['v_ssm_conv_b'], 'v_ssm_A_log': out['v_ssm_A_log'], 'v_ssm_dt_bias': out['v_ssm_dt_bias'], 'v_ssm_D': out['v_ssm_D'], 'v_ssm_norm': out['v_ssm_norm'], 'v_c_q_norm': out['v_c_q_norm'], 'v_c_k_norm': out['v_c_k_norm'], 'v_w_oa': out['v_w_oa'], 'v_w_ob': out['v_w_ob'], 'v_w_oc': out['v_w_oc'], 'v_w_out': out['v_w_out'], 'v_ffn_w_up': out['v_ffn_w_up'], 'v_ffn_w_gate': out['v_ffn_w_gate'], 'v_ffn_conv_w': out['v_ffn_conv_w'], 'v_ffn_conv_b': out['v_ffn_conv_b'], 'v_ffn_w_down': out['v_ffn_w_down'], 'v_final_norm': out['v_final_norm']}


def _loss(weights, diff, rest, loss_target):
    with _jax.named_scope("forward"):
        args = {**rest, TWIN_DIFF_INPUT: diff, **{k: w.astype(_WEIGHT_DTYPES[k]) for k, w in weights.items()}}
        y = _forward(args)
    with _jax.named_scope("loss_head"):
        err = _jnp.square(y.astype(_jnp.float32) - loss_target)
        return 0.5 * _jnp.sum(_jnp.mean(err, axis=-1)) if err.ndim else 0.5 * err


def _adamw(w, g, m, v):
    m = ADAM_B1 * m + (1.0 - ADAM_B1) * g
    v = ADAM_B2 * v + (1.0 - ADAM_B2) * _jnp.square(g)
    m_hat = m / (1.0 - ADAM_B1 ** ADAM_STEP)
    v_hat = v / (1.0 - ADAM_B2 ** ADAM_STEP)
    delta = -ADAM_LR * (m_hat / (_jnp.sqrt(v_hat) + ADAM_EPS) + ADAM_WD * w)
    return delta, m, v


def reference(x, c, ctx, c_ctx, w_mod, b_mod, norm1, norm2, w_in, a_sink, ssm_conv_w, ssm_conv_b, ssm_A_log, ssm_dt_bias, ssm_D, ssm_norm, c_q_norm, c_k_norm, w_oa, w_ob, w_oc, w_out, ffn_w_up, ffn_w_gate, ffn_conv_w, ffn_conv_b, ffn_w_down, final_norm, loss_target, m_c_ctx, m_w_mod, m_b_mod, m_norm1, m_norm2, m_w_in, m_a_sink, m_ssm_conv_w, m_ssm_conv_b, m_ssm_A_log, m_ssm_dt_bias, m_ssm_D, m_ssm_norm, m_c_q_norm, m_c_k_norm, m_w_oa, m_w_ob, m_w_oc, m_w_out, m_ffn_w_up, m_ffn_w_gate, m_ffn_conv_w, m_ffn_conv_b, m_ffn_w_down, m_final_norm, v_c_ctx, v_w_mod, v_b_mod, v_norm1, v_norm2, v_w_in, v_a_sink, v_ssm_conv_w, v_ssm_conv_b, v_ssm_A_log, v_ssm_dt_bias, v_ssm_D, v_ssm_norm, v_c_q_norm, v_c_k_norm, v_w_oa, v_w_ob, v_w_oc, v_w_out, v_ffn_w_up, v_ffn_w_gate, v_ffn_conv_w, v_ffn_conv_b, v_ffn_w_down, v_final_norm):
    given = dict(x=x, c=c, ctx=ctx, c_ctx=c_ctx, w_mod=w_mod, b_mod=b_mod, norm1=norm1, norm2=norm2, w_in=w_in, a_sink=a_sink, ssm_conv_w=ssm_conv_w, ssm_conv_b=ssm_conv_b, ssm_A_log=ssm_A_log, ssm_dt_bias=ssm_dt_bias, ssm_D=ssm_D, ssm_norm=ssm_norm, c_q_norm=c_q_norm, c_k_norm=c_k_norm, w_oa=w_oa, w_ob=w_ob, w_oc=w_oc, w_out=w_out, ffn_w_up=ffn_w_up, ffn_w_gate=ffn_w_gate, ffn_conv_w=ffn_conv_w, ffn_conv_b=ffn_conv_b, ffn_w_down=ffn_w_down, final_norm=final_norm, loss_target=loss_target, m_c_ctx=m_c_ctx, m_w_mod=m_w_mod, m_b_mod=m_b_mod, m_norm1=m_norm1, m_norm2=m_norm2, m_w_in=m_w_in, m_a_sink=m_a_sink, m_ssm_conv_w=m_ssm_conv_w, m_ssm_conv_b=m_ssm_conv_b, m_ssm_A_log=m_ssm_A_log, m_ssm_dt_bias=m_ssm_dt_bias, m_ssm_D=m_ssm_D, m_ssm_norm=m_ssm_norm, m_c_q_norm=m_c_q_norm, m_c_k_norm=m_c_k_norm, m_w_oa=m_w_oa, m_w_ob=m_w_ob, m_w_oc=m_w_oc, m_w_out=m_w_out, m_ffn_w_up=m_ffn_w_up, m_ffn_w_gate=m_ffn_w_gate, m_ffn_conv_w=m_ffn_conv_w, m_ffn_conv_b=m_ffn_conv_b, m_ffn_w_down=m_ffn_w_down, m_final_norm=m_final_norm, v_c_ctx=v_c_ctx, v_w_mod=v_w_mod, v_b_mod=v_b_mod, v_norm1=v_norm1, v_norm2=v_norm2, v_w_in=v_w_in, v_a_sink=v_a_sink, v_ssm_conv_w=v_ssm_conv_w, v_ssm_conv_b=v_ssm_conv_b, v_ssm_A_log=v_ssm_A_log, v_ssm_dt_bias=v_ssm_dt_bias, v_ssm_D=v_ssm_D, v_ssm_norm=v_ssm_norm, v_c_q_norm=v_c_q_norm, v_c_k_norm=v_c_k_norm, v_w_oa=v_w_oa, v_w_ob=v_w_ob, v_w_oc=v_w_oc, v_w_out=v_w_out, v_ffn_w_up=v_ffn_w_up, v_ffn_w_gate=v_ffn_w_gate, v_ffn_conv_w=v_ffn_conv_w, v_ffn_conv_b=v_ffn_conv_b, v_ffn_w_down=v_ffn_w_down, v_final_norm=v_final_norm)
    weights = {n: given[n] for n in TWIN_WEIGHTS}
    shared = {n: given[n] for n in SHARED_INPUTS}
    per_example = {n: given[n] for n in ['x', 'c', 'ctx']}
    grad_fn = _jax.value_and_grad(_loss, argnums=(0, 1))

    def one_microbatch(ex, loss_target):
        ex = dict(ex)
        diff = ex.pop(TWIN_DIFF_INPUT)
        return grad_fn(weights, diff, {**shared, **ex}, loss_target)

    if N_MICROBATCH == 1:
        loss, (grad_w, grad_x) = one_microbatch(per_example, given["loss_target"])
    else:
        def body(carry, xs):
            loss_sum, grad_sum = carry
            l_k, (gw_k, gx_k) = one_microbatch(xs[0], xs[1])
            with _jax.named_scope("update"):
                return (loss_sum + l_k, _jax.tree.map(_jnp.add, grad_sum, gw_k)), gx_k

        init = (_jnp.zeros((), _jnp.float32), _jax.tree.map(_jnp.zeros_like, weights))
        (loss, grad_w), grad_x = _jax.lax.scan(body, init, (per_example, given["loss_target"]))
    with _jax.named_scope("update"):
        delta_w, new_m, new_v = {}, {}, {}
        for n in TWIN_WEIGHTS:
            delta_w[n], new_m[n], new_v[n] = _adamw(weights[n], grad_w[n], given["m_" + n], given["v_" + n])
    return (loss, grad_x, *[grad_w[n] for n in TWIN_WEIGHTS], *[delta_w[n] for n in TWIN_WEIGHTS],
            *[new_m[n] for n in TWIN_WEIGHTS], *[new_v[n] for n in TWIN_WEIGHTS])
```

```python
import functools

import jax
import jax.numpy as jnp
from jax import lax
from jax.experimental import pallas as pl
from jax.experimental.pallas import tpu as pltpu

f32 = jnp.float32
bf16 = jnp.bfloat16
MESH = pl.DeviceIdType.MESH
AXES = ("x", "y", "c")

GRID_W = 64
HEAD_DIM = 64
ROPE_BASE = 10000.0
EPS = 1e-6
WINDOW = 128
N_HEADS = 8
N_KV = 2
SSM_HEADS = 16
SSM_P = 64
SSM_G = 2
SSM_N = 128
SSM_INNER = SSM_HEADS * SSM_P
SSM_BC = SSM_G * SSM_N
SSM_Q = 128
Q_W = N_HEADS * HEAD_DIM
KV_W = N_KV * HEAD_DIM
DT_W = 2 * SSM_HEADS
DT_PAD = 128
ADAM_LR, ADAM_B1, ADAM_B2, ADAM_EPS, ADAM_WD, ADAM_STEP = 0.001, 0.9, 0.999, 1e-08, 0.01, 10

LANES = 128
ROW_TILE = 256
VMEM_BLOCK_BUDGET = 6 * 1024 * 1024
NEG = -1e30

IN_NAMES = ['x', 'c', 'ctx', 'c_ctx', 'w_mod', 'b_mod', 'norm1', 'norm2', 'w_in', 'a_sink', 'ssm_conv_w', 'ssm_conv_b', 'ssm_A_log', 'ssm_dt_bias', 'ssm_D', 'ssm_norm', 'c_q_norm', 'c_k_norm', 'w_oa', 'w_ob', 'w_oc', 'w_out', 'ffn_w_up', 'ffn_w_gate', 'ffn_conv_w', 'ffn_conv_b', 'ffn_w_down', 'final_norm', 'loss_target']
WEIGHTS = IN_NAMES[3:28]
BIG = {'w_mod': 1, 'w_in': 1, 'w_oa': 1, 'w_ob': 0, 'w_oc': 1, 'w_out': 0, 'ffn_w_up': 1, 'ffn_w_gate': 1, 'ffn_w_down': 0}
CONV_W = ('ssm_conv_w', 'ffn_conv_w')
REPL = [n for n in WEIGHTS if n not in BIG and n not in CONV_W]

NT = (((1,), (1,)), ((), ()))
TN = (((0,), (0,)), ((), ()))
NN = (((1,), (0,)), ((), ()))


def _cparams(*sem):
    return pltpu.CompilerParams(dimension_semantics=sem)


def _row_tile(m, row_bytes):
    ok = [d for d in range(min(m, 1088), 0, -1) if m % d == 0 and (d % 8 == 0 or d == m)]
    for d in ok:
        if d * row_bytes <= VMEM_BLOCK_BUDGET:
            return d
    return ok[-1]


def _col_tile(n, col_bytes):
    for t in (512, 256, 128):
        if n % t == 0 and t * col_bytes <= VMEM_BLOCK_BUDGET:
            return t
    return LANES if n % LANES == 0 else n


def _mm_call(a, b, mode, out_dtype, name):
    if mode == "nn":
        (m, k), n = a.shape, b.shape[1]
    elif mode == "nt":
        (m, k), n = a.shape, b.shape[0]
    else:
        (k, m), n = a.shape, b.shape[1]
    dims = {"nn": NN, "nt": NT, "tn": TN}[mode]
    if mode == "tn":
        tm = _col_tile(m, k * a.dtype.itemsize)
    else:
        tm = _row_tile(m, k * a.dtype.itemsize)
    tn = _col_tile(n, k * b.dtype.itemsize)

    def body(a_ref, b_ref, o_ref):
        o_ref[...] = lax.dot_general(a_ref[...].astype(bf16), b_ref[...].astype(bf16), dims,
                                     preferred_element_type=f32).astype(o_ref.dtype)

    a_spec = pl.BlockSpec((k, tm), lambda i, j: (0, i)) if mode == "tn" else pl.BlockSpec((tm, k), lambda i, j: (i, 0))
    b_spec = pl.BlockSpec((tn, k), lambda i, j: (j, 0)) if mode == "nt" else pl.BlockSpec((k, tn), lambda i, j: (0, j))
    return pl.pallas_call(
        body, grid=(m // tm, n // tn), in_specs=[a_spec, b_spec],
        out_specs=pl.BlockSpec((tm, tn), lambda i, j: (i, j)),
        out_shape=jax.ShapeDtypeStruct((m, n), out_dtype), name=name,
        compiler_params=_cparams("parallel", "parallel"))(a, b)


def mm(a, b, name):
    @jax.custom_vjp
    def op(a, b):
        return _mm_call(a, b, "nn", f32, name)

    def fwd(a, b):
        return op(a, b), (a, b)

    def bwd(res, g):
        a, b = res
        return _mm_call(g, b, "nt", a.dtype, name + "_da"), _mm_call(a, g, "tn", b.dtype, name + "_db")

    op.defvjp(fwd, bwd)
    return op(a, b)


def split_cols(u, widths):
    offs = [0]
    for w in widths:
        offs.append(offs[-1] + w)

    @jax.custom_vjp
    def op(u):
        return tuple(u[:, offs[i]:offs[i + 1]] for i in range(len(widths)))

    def fwd(u):
        return op(u), None

    def bwd(_, cts):
        return (jnp.concatenate(cts, axis=1),)

    op.defvjp(fwd, bwd)
    return op(u)


def rowwise(fn, rows, consts, pars, out_widths, out_dtypes, name):
    t = rows[0].shape[0]
    tm = ROW_TILE
    nb = t // tm
    nr, nc, npar = len(rows), len(consts), len(pars)

    def rspec(a):
        return pl.BlockSpec((tm, a.shape[1]), lambda i: (i, 0))

    def pspec(a):
        return pl.BlockSpec(a.shape, lambda i: (0,) * a.ndim)

    def call_fwd(rows, consts, pars):
        def body(*refs):
            blk = pl.program_id(0)
            ins = [r[...].astype(f32) for r in refs[:nr + nc]]
            ps = [r[...] for r in refs[nr + nc:nr + nc + npar]]
            outs = fn(blk, *ins, *ps)
            for o_ref, o in zip(refs[nr + nc + npar:], outs):
                o_ref[...] = o.astype(o_ref.dtype)

        return pl.pallas_call(
            body, grid=(nb,),
            in_specs=[rspec(a) for a in rows + consts] + [pspec(a) for a in pars],
            out_specs=[pl.BlockSpec((tm, w), lambda i: (i, 0)) for w in out_widths],
            out_shape=[jax.ShapeDtypeStruct((t, w), d) for w, d in zip(out_widths, out_dtypes)],
            name=name, compiler_params=_cparams("parallel"))(*rows, *consts, *pars)

    def call_bwd(rows, consts, pars, cts):
        nout = len(cts)

        def body(*refs):
            blk = pl.program_id(0)
            ins = [r[...].astype(f32) for r in refs[:nr]]
            cs = [r[...].astype(f32) for r in refs[nr:nr + nc]]
            ps = [r[...] for r in refs[nr + nc:nr + nc + npar]]
            dys = [r[...].astype(f32) for r in refs[nr + nc + npar:nr + nc + npar + nout]]
            d_refs = refs[nr + nc + npar + nout:]
            _, vjp = jax.vjp(lambda *a: tuple(fn(blk, *a[:nr], *cs, *a[nr:])), *ins, *ps)
            grads = vjp(tuple(dys))
            for d_ref, g in zip(d_refs[:nr], grads[:nr]):
                d_ref[...] = g.astype(d_ref.dtype)
            if npar:
                @pl.when(blk == 0)
                def _():
                    for d_ref in d_refs[nr:]:
                        d_ref[...] = jnp.zeros_like(d_ref)

                for d_ref, g in zip(d_refs[nr:], grads[nr:]):
                    d_ref[...] += g

        return pl.pallas_call(
            body, grid=(nb,),
            in_specs=[rspec(a) for a in rows + consts] + [pspec(a) for a in pars] + [rspec(a) for a in cts],
            out_specs=[rspec(a) for a in rows] + [pspec(a) for a in pars],
            out_shape=[jax.ShapeDtypeStruct(a.shape, a.dtype) for a in rows + pars],
            name=name + "_bwd", compiler_params=_cparams("arbitrary"))(*rows, *consts, *pars, *cts)

    @jax.custom_vjp
    def op(rows, consts, pars):
        return tuple(call_fwd(list(rows), list(consts), list(pars)))

    def fwd(rows, consts, pars):
        return op(rows, consts, pars), (rows, consts, pars)

    def bwd(res, cts):
        rows, consts, pars = res
        g = call_bwd(list(rows), list(consts), list(pars), list(cts))
        return tuple(g[:nr]), tuple(jnp.zeros_like(a) for a in consts), tuple(g[nr:])

    op.defvjp(fwd, bwd)
    return op(tuple(rows), tuple(consts), tuple(pars))


def colwise(fn, cols, pars, out_dtype, name):
    t, w = cols[0].shape
    tc = LANES
    nb = w // tc
    ncol, npar = len(cols), len(pars)

    def cspec(a):
        return pl.BlockSpec((a.shape[0], tc), lambda j: (0, j))

    def call_fwd(cols, pars):
        def body(*refs):
            ins = [r[...].astype(f32) for r in refs[:ncol]]
            ps = [r[...] for r in refs[ncol:ncol + npar]]
            refs[-1][...] = fn(*ins, *ps).astype(refs[-1].dtype)

        return pl.pallas_call(
            body, grid=(nb,), in_specs=[cspec(a) for a in cols + pars], out_specs=cspec(cols[0]),
            out_shape=jax.ShapeDtypeStruct((t, w), out_dtype), name=name, compiler_params=_cparams("parallel"))(*cols, *pars)

    def call_bwd(cols, pars, ct):
        def body(*refs):
            ins = [r[...].astype(f32) for r in refs[:ncol]]
            ps = [r[...] for r in refs[ncol:ncol + npar]]
            dy = refs[ncol + npar][...].astype(f32)
            d_refs = refs[ncol + npar + 1:]
            _, vjp = jax.vjp(fn, *ins, *ps)
            grads = vjp(dy)
            for d_ref, g in zip(d_refs, grads):
                d_ref[...] = g.astype(d_ref.dtype)

        return pl.pallas_call(
            body, grid=(nb,), in_specs=[cspec(a) for a in cols + pars + [ct]],
            out_specs=[cspec(a) for a in cols + pars],
            out_shape=[jax.ShapeDtypeStruct(a.shape, a.dtype) for a in cols + pars],
            name=name + "_bwd", compiler_params=_cparams("parallel"))(*cols, *pars, ct)

    @jax.custom_vjp
    def op(cols, pars):
        return call_fwd(list(cols), list(pars))

    def fwd(cols, pars):
        return op(cols, pars), (cols, pars)

    def bwd(res, ct):
        cols, pars = res
        g = call_bwd(list(cols), list(pars), ct)
        return tuple(g[:ncol]), tuple(g[ncol:])

    op.defvjp(fwd, bwd)
    return op(tuple(cols), tuple(pars))


def _sigmoid(x):
    return 1.0 / (1.0 + jnp.exp(-x))


def _silu(x):
    return x * _sigmoid(x)


def _rms(x, g):
    return x * lax.rsqrt(jnp.mean(x * x, axis=-1, keepdims=True) + EPS) * g


def _shift_rows(u, k, n_ctx):
    @jax.custom_vjp
    def op(u):
        t = u.shape[0]
        row = lax.broadcasted_iota(jnp.int32, u.shape, 0)
        edge = ((row == 0) | (row == n_ctx)) if k == 1 else ((row == n_ctx - 1) | (row == t - 1))
        return jnp.where(edge, 0.0, pltpu.roll(u, k % t, 0))

    op.defvjp(lambda u: (op(u), None), lambda _, g: (_shift_rows(g, -k, n_ctx),))
    return op(u)


def _dwconv(u, w0, w1, w2, b, n_ctx):
    return w0 * _shift_rows(u, 1, n_ctx) + w1 * u + w2 * _shift_rows(u, -1, n_ctx) + b


@jax.custom_vjp
def _rot_half(x):
    w = x.shape[1]
    lane = lax.broadcasted_iota(jnp.int32, x.shape, 1)
    return jnp.where((lane % HEAD_DIM) < HEAD_DIM // 2, pltpu.roll(x, w - HEAD_DIM // 2, 1), pltpu.roll(x, HEAD_DIM // 2, 1))


_rot_half.defvjp(lambda x: (_rot_half(x), None), lambda _, g: (_rot_half(g),))


def _head_rms(x, g):
    w = x.shape[1]
    same = (lax.broadcasted_iota(jnp.int32, (w, w), 0) // HEAD_DIM) == (lax.broadcasted_iota(jnp.int32, (w, w), 1) // HEAD_DIM)
    ms = jnp.dot(x * x, same.astype(f32), precision=lax.Precision.HIGHEST, preferred_element_type=f32) * (1.0 / HEAD_DIM)
    return x * lax.rsqrt(ms + EPS) * g


def _band_ok(i, j, shape, tq, tk):
    qpos = i * tq + lax.broadcasted_iota(jnp.int32, shape, 0) % tq
    kpos = j * tk + lax.broadcasted_iota(jnp.int32, shape, 1)
    return jnp.abs(qpos - kpos) <= WINDOW


def _kv_range(i, nb, window):
    is_ctx = i == 0
    if window:
        return jnp.where(is_ctx, 1, jnp.maximum(i - 1, 1)), jnp.where(is_ctx, 1, jnp.minimum(i + 2, nb))
    return 1, jnp.where(is_ctx, 1, nb)


def _sink_col(sink_ref, g, r, tq):
    return jnp.concatenate([jnp.full((tq, 1), sink_ref[g * r + h], f32) for h in range(r)], axis=0)


def _attn_fwd_call(q, k, v, sink, window, name):
    h, t, dh = q.shape
    r = h // k.shape[0]
    tq = tk = ROW_TILE
    nb = t // tq

    def body(sink_ref, q_ref, k_ref, v_ref, o_ref, lse_ref):
        g, i = pl.program_id(0), pl.program_id(1)
        qv = q_ref[...].reshape(r * tq, dh)

        def chunk(j, carry, masked):
            m, l, acc = carry
            start = pl.multiple_of(j * tk, tk)
            kj = k_ref[0, pl.ds(start, tk), :]
            vj = v_ref[0, pl.ds(start, tk), :]
            s = lax.dot_general(qv, kj, NT, preferred_element_type=f32)
            if masked:
                s = jnp.where(_band_ok(i, j, s.shape, tq, tk), s, NEG)
            m2 = jnp.maximum(m, jnp.max(s, axis=1, keepdims=True))
            p = jnp.exp(s - m2)
            a = jnp.exp(m - m2)
            return m2, a * l + jnp.sum(p, axis=1, keepdims=True), a * acc + jnp.dot(p.astype(bf16), vj, preferred_element_type=f32)

        init = (jnp.full((r * tq, 1), NEG, f32), jnp.zeros((r * tq, 1), f32), jnp.zeros((r * tq, dh), f32))
        carry = chunk(0, init, False)
        lo, hi = _kv_range(i, nb, window)
        m, l, acc = lax.fori_loop(lo, hi, lambda j, c: chunk(j, c, window), carry)
        if window:
            sk = _sink_col(sink_ref, g, r, tq)
            m2 = jnp.maximum(m, sk)
            a = jnp.exp(m - m2)
            l = a * l + jnp.exp(sk - m2)
            acc = a * acc
            m = m2
        o_ref[...] = (acc / l).reshape(r, tq, dh).astype(o_ref.dtype)
        lse_ref[...] = (m + jnp.log(l)).reshape(r, tq, 1)

    return pl.pallas_call(
        body, grid=(k.shape[0], nb),
        in_specs=[pl.BlockSpec(memory_space=pltpu.SMEM),
                  pl.BlockSpec((r, tq, dh), lambda g, i: (g, i, 0)),
                  pl.BlockSpec((1, t, dh), lambda g, i: (g, 0, 0)),
                  pl.BlockSpec((1, t, dh), lambda g, i: (g, 0, 0))],
        out_specs=[pl.BlockSpec((r, tq, dh), lambda g, i: (g, i, 0)), pl.BlockSpec((r, tq, 1), lambda g, i: (g, i, 0))],
        out_shape=[jax.ShapeDtypeStruct((h, t, dh), bf16), jax.ShapeDtypeStruct((h, t, 1), f32)],
        name=name, compiler_params=_cparams("parallel", "parallel"))(sink, q, k, v)


def _attn_dq_call(q, k, v, sink, o, lse, do, window, name):
    h, t, dh = q.shape
    r = h // k.shape[0]
    tq = tk = ROW_TILE
    nb = t // tq

    def body(sink_ref, q_ref, k_ref, v_ref, o_ref, lse_ref, do_ref, dq_ref, delta_ref, dsink_ref):
        g, i = pl.program_id(0), pl.program_id(1)
        qv = q_ref[...].reshape(r * tq, dh)
        dov = do_ref[...].reshape(r * tq, dh)
        lse_v = lse_ref[...].reshape(r * tq, 1)
        delta = jnp.sum(dov.astype(f32) * o_ref[...].reshape(r * tq, dh).astype(f32), axis=1, keepdims=True)

        def chunk(j, dq, masked):
            start = pl.multiple_of(j * tk, tk)
            kj = k_ref[0, pl.ds(start, tk), :]
            vj = v_ref[0, pl.ds(start, tk), :]
            s = lax.dot_general(qv, kj, NT, preferred_element_type=f32)
            if masked:
                s = jnp.where(_band_ok(i, j, s.shape, tq, tk), s, NEG)
            p = jnp.exp(s - lse_v)
            dp = lax.dot_general(dov, vj, NT, preferred_element_type=f32)
            ds = p * (dp - delta)
            return dq + jnp.dot(ds.astype(bf16), kj, preferred_element_type=f32)

        dq = chunk(0, jnp.zeros((r * tq, dh), f32), False)
        lo, hi = _kv_range(i, nb, window)
        dq = lax.fori_loop(lo, hi, lambda j, c: chunk(j, c, window), dq)
        dq_ref[...] = dq.reshape(r, tq, dh).astype(dq_ref.dtype)
        delta_ref[...] = delta.reshape(r, tq, 1)
        if window:
            dsink_ref[...] = (-jnp.exp(_sink_col(sink_ref, g, r, tq) - lse_v) * delta).reshape(r, tq, 1)
        else:
            dsink_ref[...] = jnp.zeros_like(dsink_ref)

    qspec = pl.BlockSpec((r, tq, dh), lambda g, i: (g, i, 0))
    cspec = pl.BlockSpec((r, tq, 1), lambda g, i: (g, i, 0))
    kspec = pl.BlockSpec((1, t, dh), lambda g, i: (g, 0, 0))
    return pl.pallas_call(
        body, grid=(k.shape[0], nb),
        in_specs=[pl.BlockSpec(memory_space=pltpu.SMEM), qspec, kspec, kspec, qspec, cspec, qspec],
        out_specs=[qspec, cspec, cspec],
        out_shape=[jax.ShapeDtypeStruct((h, t, dh), bf16), jax.ShapeDtypeStruct((h, t, 1), f32), jax.ShapeDtypeStruct((h, t, 1), f32)],
        name=name, compiler_params=_cparams("parallel", "parallel"))(sink, q, k, v, o, lse, do)


def _attn_dkv_call(q, k, v, lse, do, delta, window, name):
    h, t, dh = q.shape
    nkv = k.shape[0]
    r = h // nkv
    tq = tk = ROW_TILE
    nb = t // tq

    def q_chunk(j, c):
        if window:
            return jnp.where(j == 0, c, jnp.clip(c, jnp.maximum(j - 1, 1), jnp.minimum(j + 1, nb - 1)))
        return c

    def body(q_ref, k_ref, v_ref, lse_ref, do_ref, delta_ref, dk_ref, dv_ref, dk_acc, dv_acc):
        j, c = pl.program_id(1), pl.program_id(2)

        @pl.when(c == 0)
        def _():
            dk_acc[...] = jnp.zeros_like(dk_acc)
            dv_acc[...] = jnp.zeros_like(dv_acc)

        if window:
            active = (j == 0) | ((c >= 1) & (jnp.abs(c - j) <= 1))
        else:
            active = (j == 0) | (c >= 1)

        @pl.when(active)
        def _():
            qv = q_ref[...].reshape(r * tq, dh)
            dov = do_ref[...].reshape(r * tq, dh)
            kj, vj = k_ref[0], v_ref[0]
            s = lax.dot_general(qv, kj, NT, preferred_element_type=f32)
            if window:
                s = jnp.where((j == 0) | _band_ok(c, j, s.shape, tq, tk), s, NEG)
            p = jnp.exp(s - lse_ref[...].reshape(r * tq, 1))
            dv_acc[...] += lax.dot_general(p.astype(bf16), dov, TN, preferred_element_type=f32)
            dp = lax.dot_general(dov, vj, NT, preferred_element_type=f32)
            ds = p * (dp - delta_ref[...].reshape(r * tq, 1))
            dk_acc[...] += lax.dot_general(ds.astype(bf16), qv, TN, preferred_element_type=f32)

        @pl.when(c == nb - 1)
        def _():
            dk_ref[0] = dk_acc[...].astype(dk_ref.dtype)
            dv_ref[0] = dv_acc[...].astype(dv_ref.dtype)

    qspec = pl.BlockSpec((r, tq, dh), lambda g, j, c: (g, q_chunk(j, c), 0))
    cspec = pl.BlockSpec((r, tq, 1), lambda g, j, c: (g, q_chunk(j, c), 0))
    kspec = pl.BlockSpec((1, tk, dh), lambda g, j, c: (g, j, 0))
    return pl.pallas_call(
        body, grid=(nkv, nb, nb),
        in_specs=[qspec, kspec, kspec, cspec, qspec, cspec],
        out_specs=[kspec, kspec],
        out_shape=[jax.ShapeDtypeStruct(k.shape, bf16), jax.ShapeDtypeStruct(v.shape, bf16)],
        scratch_shapes=[pltpu.VMEM((tk, dh), f32), pltpu.VMEM((tk, dh), f32)],
        name=name, compiler_params=_cparams("parallel", "parallel", "arbitrary"))(q, k, v, lse, do, delta)


def attention(q, k, v, sink, window, name):
    @jax.custom_vjp
    def op(q, k, v, sink):
        return _attn_fwd_call(q, k, v, sink, window, name)[0]

    def fwd(q, k, v, sink):
        o, lse = _attn_fwd_call(q, k, v, sink, window, name)
        return o, (q, k, v, sink, o, lse)

    def bwd(res, do):
        q, k, v, sink, o, lse = res
        dq, delta, dsink_rows = _attn_dq_call(q, k, v, sink, o, lse, do, window, name + "_dq")
        dk, dv = _attn_dkv_call(q, k, v, lse, do, delta, window, name + "_dkv")
        return dq, dk, dv, jnp.sum(dsink_rows, axis=(1, 2))

    op.defvjp(fwd, bwd)
    return op(q, k, v, sink)


def _ssd_pair(xs_p, dtx_p, dtr1, dtr2, ac1, ac2, bg, cg, hin_p, rev):
    q = xs_p.shape[0]
    ii = lax.broadcasted_iota(jnp.int32, (q, q), 0)
    jj = lax.broadcasted_iota(jnp.int32, (q, q), 1)
    tri = (ii <= jj) if rev else (ii >= jj)
    lo = lax.broadcasted_iota(jnp.int32, (q, LANES), 1) < SSM_P
    lo_row = lax.broadcasted_iota(jnp.int32, (1, LANES), 1) < SSM_P

    def cums(dtr, ac):
        a = dtr * ac
        c = jnp.sum(jnp.where(tri, jnp.broadcast_to(a, (q, q)), 0.0), axis=1, keepdims=True)
        return c, jnp.sum(a, axis=1, keepdims=True)

    def lmat(c):
        cf = jnp.broadcast_to(c, (q, q))
        return jnp.where(tri, jnp.exp(jnp.minimum(cf - cf.T, 0.0)), 0.0)

    c1, t1 = cums(dtr1, ac1)
    c2, t2 = cums(dtr2, ac2)
    cb = lax.dot_general(cg.astype(bf16), bg.astype(bf16), NT, preferred_element_type=f32)
    m = jnp.concatenate([cb * lmat(c1), cb * lmat(c2)], axis=1)
    xdt = xs_p * dtx_p
    x2 = jnp.concatenate([jnp.where(lo, xdt, 0.0), jnp.where(lo, 0.0, xdt)], axis=0)
    y_diag = jnp.dot(m.astype(bf16), x2.astype(bf16), preferred_element_type=f32)
    csel = jnp.where(lo, jnp.broadcast_to(c1, (q, LANES)), jnp.broadcast_to(c2, (q, LANES)))
    tsel = jnp.where(lo_row, jnp.broadcast_to(t1, (1, LANES)), jnp.broadcast_to(t2, (1, LANES)))
    st = lax.dot_general(bg.astype(bf16), (xdt * jnp.exp(tsel - csel)).astype(bf16), TN, preferred_element_type=f32)
    y_off = jnp.dot(cg.astype(bf16), hin_p.astype(bf16), preferred_element_type=f32) * jnp.exp(csel)
    return y_diag + y_off, hin_p * jnp.exp(tsel) + st


def _ssd_order(s, nc, ncc, rev):
    if not rev:
        return s
    return jnp.where(s < ncc, ncc - 1 - s, nc - 1 - (s - ncc))


def _ssd_pair_slices(j):
    return slice(LANES * j, LANES * (j + 1)), 2 * j, 2 * j + 1, (2 * j) // (SSM_HEADS // SSM_G)


def _ssd_fwd_call(xs, dtx, dtr, bm, cm, acol, rev, n_ctx, name):
    t, w = xs.shape
    q = SSM_Q
    nc, ncc = t // q, n_ctx // q

    def body(xs_ref, dtx_ref, dtr_ref, b_ref, c_ref, a_ref, y_ref, hin_ref, h_scr):
        @pl.when(pl.program_id(0) == 0)
        def _():
            h_scr[...] = jnp.zeros_like(h_scr)

        hin_ref[0] = h_scr[...]
        for j in range(SSM_HEADS // 2):
            sl, h1, h2, g = _ssd_pair_slices(j)
            gs = slice(SSM_N * g, SSM_N * (g + 1))
            y, hout = _ssd_pair(xs_ref[:, sl], dtx_ref[:, sl], dtr_ref[h1:h1 + 1, :], dtr_ref[h2:h2 + 1, :],
                                a_ref[h1:h1 + 1, :], a_ref[h2:h2 + 1, :], b_ref[:, gs], c_ref[:, gs], h_scr[:, sl], rev)
            y_ref[:, sl] = y
            h_scr[:, sl] = hout

    def at(s):
        return _ssd_order(s, nc, ncc, rev)

    return pl.pallas_call(
        body, grid=(nc,),
        in_specs=[pl.BlockSpec((q, w), lambda s: (at(s), 0)), pl.BlockSpec((q, w), lambda s: (at(s), 0)),
                  pl.BlockSpec((SSM_HEADS, q), lambda s: (0, at(s))),
                  pl.BlockSpec((q, SSM_BC), lambda s: (at(s), 0)), pl.BlockSpec((q, SSM_BC), lambda s: (at(s), 0)),
                  pl.BlockSpec((SSM_HEADS, 1), lambda s: (0, 0))],
        out_specs=[pl.BlockSpec((q, w), lambda s: (at(s), 0)), pl.BlockSpec((1, SSM_N, w), lambda s: (s, 0, 0))],
        out_shape=[jax.ShapeDtypeStruct((t, w), f32), jax.ShapeDtypeStruct((nc, SSM_N, w), f32)],
        scratch_shapes=[pltpu.VMEM((SSM_N, w), f32)],
        name=name, compiler_params=_cparams("arbitrary"))(xs, dtx, dtr, bm, cm, acol)


def _ssd_bwd_call(xs, dtx, dtr, bm, cm, acol, hin, dy, rev, n_ctx, name):
    t, w = xs.shape
    q = SSM_Q
    nc, ncc = t // q, n_ctx // q

    def body(xs_ref, dtx_ref, dtr_ref, b_ref, c_ref, a_ref, hin_ref, dy_ref,
             dxs_ref, ddtx_ref, ddtr_ref, db_ref, dc_ref, da_ref, dh_scr):
        @pl.when(pl.program_id(0) == 0)
        def _():
            dh_scr[...] = jnp.zeros_like(dh_scr)
            da_ref[...] = jnp.zeros_like(da_ref)

        db = [None] * SSM_G
        dc = [None] * SSM_G
        for j in range(SSM_HEADS // 2):
            sl, h1, h2, g = _ssd_pair_slices(j)
            gs = slice(SSM_N * g, SSM_N * (g + 1))
            _, vjp = jax.vjp(
                functools.partial(_ssd_pair, rev=rev),
                xs_ref[:, sl], dtx_ref[:, sl], dtr_ref[h1:h1 + 1, :], dtr_ref[h2:h2 + 1, :],
                a_ref[h1:h1 + 1, :], a_ref[h2:h2 + 1, :], b_ref[:, gs], c_ref[:, gs], hin_ref[0, :, sl])
            gr = vjp((dy_ref[:, sl], dh_scr[:, sl]))
            dxs_ref[:, sl] = gr[0]
            ddtx_ref[:, sl] = gr[1]
            ddtr_ref[h1:h1 + 1, :] = gr[2]
            ddtr_ref[h2:h2 + 1, :] = gr[3]
            da_ref[h1:h1 + 1, :] += gr[4]
            da_ref[h2:h2 + 1, :] += gr[5]
            db[g] = gr[6] if db[g] is None else db[g] + gr[6]
            dc[g] = gr[7] if dc[g] is None else dc[g] + gr[7]
            dh_scr[:, sl] = gr[8]
        for g in range(SSM_G):
            gs = slice(SSM_N * g, SSM_N * (g + 1))
            db_ref[:, gs] = db[g]
            dc_ref[:, gs] = dc[g]

    def step(s):
        return nc - 1 - s

    def at(s):
        return _ssd_order(step(s), nc, ncc, rev)

    row = lambda wd: pl.BlockSpec((q, wd), lambda s: (at(s), 0))
    dtr_spec = pl.BlockSpec((SSM_HEADS, q), lambda s: (0, at(s)))
    a_spec = pl.BlockSpec((SSM_HEADS, 1), lambda s: (0, 0))
    return pl.pallas_call(
        body, grid=(nc,),
        in_specs=[row(w), row(w), dtr_spec, row(SSM_BC), row(SSM_BC), a_spec,
                  pl.BlockSpec((1, SSM_N, w), lambda s: (step(s), 0, 0)), row(w)],
        out_specs=[row(w), row(w), dtr_spec, row(SSM_BC), row(SSM_BC), a_spec],
        out_shape=[jax.ShapeDtypeStruct((t, w), f32), jax.ShapeDtypeStruct((t, w), f32), jax.ShapeDtypeStruct(dtr.shape, f32),
                   jax.ShapeDtypeStruct(bm.shape, f32), jax.ShapeDtypeStruct(cm.shape, f32), jax.ShapeDtypeStruct(acol.shape, f32)],
        scratch_shapes=[pltpu.VMEM((SSM_N, w), f32)],
        name=name, compiler_params=_cparams("arbitrary"))(xs, dtx, dtr, bm, cm, acol, hin, dy)


def ssd_scan(xs, dtx, dtr, bm, cm, acol, rev, n_ctx, name):
    @jax.custom_vjp
    def op(xs, dtx, dtr, bm, cm, acol):
        return _ssd_fwd_call(xs, dtx, dtr, bm, cm, acol, rev, n_ctx, name)[0]

    def fwd(xs, dtx, dtr, bm, cm, acol):
        y, hin = _ssd_fwd_call(xs, dtx, dtr, bm, cm, acol, rev, n_ctx, name)
        return y, (xs, dtx, dtr, bm, cm, acol, hin)

    def bwd(res, dy):
        return tuple(_ssd_bwd_call(*res, dy, rev, n_ctx, name + "_bwd"))

    op.defvjp(fwd, bwd)
    return op(xs, dtx, dtr, bm, cm, acol)


def _deinterleave(w, n_heads):
    lead = w.shape[:-1]
    return w.reshape(*lead, n_heads, HEAD_DIM // 2, 2).swapaxes(-1, -2).reshape(*lead, n_heads * HEAD_DIM)


def _interleave(w, n_heads):
    lead = w.shape[:-1]
    return w.reshape(*lead, n_heads, 2, HEAD_DIM // 2).swapaxes(-1, -2).reshape(*lead, n_heads * HEAD_DIM)


def _in_layout(d):
    sizes = [('a_q', Q_W, N_HEADS), ('a_k', KV_W, N_KV), ('a_v', KV_W, 0), ('b_z', SSM_INNER, 0),
             ('b_xbc', SSM_INNER + 2 * SSM_BC, 0), ('b_dt', DT_W, 0), ('c_q', Q_W, N_HEADS), ('c_k', KV_W, N_KV),
             ('c_v', KV_W, 0), ('g_a', d, 0), ('g_b', d, 0), ('g_c', d, 0)]
    out, start = [], 0
    for name, n, heads in sizes:
        out.append((name, start, n, heads, DT_PAD if name == 'b_dt' else n))
        start += n
    return out


@jax.custom_vjp
def _w_in_padded(w):
    d = w.shape[0]
    parts = []
    for _, s, n, heads, wp in _in_layout(d):
        p = w[:, s:s + n]
        if heads:
            p = _deinterleave(p, heads)
        if wp > n:
            p = jnp.concatenate([p, jnp.zeros((d, wp - n), w.dtype)], axis=1)
        parts.append(p)
    total = sum(p.shape[1] for p in parts)
    pad = -total % 512
    if pad:
        parts.append(jnp.zeros((d, pad), w.dtype))
    return jnp.concatenate(parts, axis=1)


def _w_in_unpadded(gp):
    d = gp.shape[0]
    parts, start = [], 0
    for _, _, n, heads, wp in _in_layout(d):
        p = gp[:, start:start + n]
        if heads:
            p = _interleave(p, heads)
        parts.append(p)
        start += wp
    return jnp.concatenate(parts, axis=1)


_w_in_padded.defvjp(lambda w: (_w_in_padded(w), None), lambda _, g: (_w_in_unpadded(g),))


def _rope_tables(n_ctx, n_lat):
    rows = n_lat // GRID_W
    t_row = jnp.repeat(jnp.arange(rows), GRID_W).astype(f32)
    t_col = jnp.tile(jnp.arange(GRID_W), rows).astype(f32)
    n = HEAD_DIM // 4
    inv = ROPE_BASE ** (-jnp.arange(n, dtype=f32) / n)
    ang = jnp.concatenate([t_row[:, None] * inv, t_col[:, None] * inv], axis=-1)
    cos = jnp.concatenate([jnp.ones((n_ctx, HEAD_DIM // 2), f32), jnp.cos(ang)], axis=0)
    sin = jnp.concatenate([jnp.zeros((n_ctx, HEAD_DIM // 2), f32), jnp.sin(ang)], axis=0)
    return jnp.concatenate([cos, cos], axis=1), jnp.concatenate([-sin, sin], axis=1)


def _heads_major(a, n_heads):
    return a.reshape(a.shape[0], n_heads, HEAD_DIM).transpose(1, 0, 2)


def _heads_minor(a):
    return a.transpose(1, 0, 2).reshape(a.shape[1], a.shape[0] * HEAD_DIM)


def _layer(xall, w, s, cm, tabs, n_ctx, li):
    t, d = xall.shape
    ncb = n_ctx // ROW_TILE
    nm = f"l{li}_"
    ctq, stq, ctk, stk = tabs
    mod = [(cm[0:1, i * d:(i + 1) * d], cm[1:2, i * d:(i + 1) * d]) for i in range(6)]

    def pick(blk, pair_c, pair_l):
        return jnp.where(blk < ncb, pair_c, pair_l)

    def norm_mod(blk, x, g, sh_c, sh_l, sc_c, sc_l):
        return (_rms(x, g) * (1.0 + pick(blk, sc_c, sc_l)) + pick(blk, sh_c, sh_l),)

    (h,) = rowwise(norm_mod, [xall], [], [s['norm1'][None], *mod[0], *mod[1]], [d], [f32], nm + "norm1")
    u = mm(h, _w_in_padded(w['w_in']), nm + "in")
    widths = [wp for *_, wp in _in_layout(d)]
    widths.append(u.shape[1] - sum(widths))
    a_q, a_k, a_v, b_z, b_xbc, b_dt, c_q, c_k, c_v, g_a, g_b, g_c, _ = split_cols(u, widths)

    def rope(blk, q, k, ct_q, st_q, ct_k, st_k):
        return q * ct_q + _rot_half(q) * st_q, k * ct_k + _rot_half(k) * st_k

    def norm_rope(blk, q, k, ct_q, st_q, ct_k, st_k, gq, gk):
        return rope(blk, _head_rms(q, gq), _head_rms(k, gk), ct_q, st_q, ct_k, st_k)

    qa, ka = rowwise(rope, [a_q, a_k], [ctq, stq, ctk, stk], [], [Q_W, KV_W], [bf16, bf16], nm + "ropeA")
    gq = jnp.tile(_deinterleave(s['c_q_norm'], 1), N_HEADS)[None]
    gk = jnp.tile(_deinterleave(s['c_k_norm'], 1), N_KV)[None]
    qc, kc = rowwise(norm_rope, [c_q, c_k], [ctq, stq, ctk, stk], [gq, gk], [Q_W, KV_W], [bf16, bf16], nm + "ropeC")
    ya = _heads_minor(attention(_heads_major(qa, N_HEADS), _heads_major(ka, N_KV), _heads_major(a_v.astype(bf16), N_KV),
                                s['a_sink'], True, nm + "attnA"))
    yc = _heads_minor(attention(_heads_major(qc, N_HEADS), _heads_major(kc, N_KV), _heads_major(c_v.astype(bf16), N_KV),
                                jnp.zeros((N_HEADS,), f32), False, nm + "attnC"))

    cw, cb = s['ssm_conv_w'], s['ssm_conv_b']
    conv_silu = lambda uu, w0, w1, w2, b: _silu(_dwconv(uu, w0, w1, w2, b, n_ctx))
    xbc = colwise(conv_silu, [b_xbc], [cw[0:1], cw[1:2], cw[2:3], cb[None]], f32, nm + "ssmconv")
    xs, bm, cmat = split_cols(xbc, [SSM_INNER, SSM_BC, SSM_BC])
    bias = jnp.concatenate([s['ssm_dt_bias'].reshape(1, DT_W), jnp.zeros((1, DT_PAD - DT_W), f32)], axis=1)

    def softplus(blk, r, b):
        z = r + b
        return (jnp.maximum(z, 0.0) + jnp.log(1.0 + jnp.exp(-jnp.abs(z))),)

    (dt_all,) = rowwise(softplus, [b_dt], [], [bias], [DT_PAD], [f32], nm + "dt")
    a_coef = -jnp.exp(s['ssm_A_log'])
    ys_dir = []
    for di, rev in enumerate((False, True)):
        dt = dt_all[:, di * SSM_HEADS:(di + 1) * SSM_HEADS]
        ys_dir.append(ssd_scan(xs, jnp.repeat(dt, SSM_P, axis=1), dt.T, bm, cmat, a_coef[di][:, None], rev, n_ctx,
                               nm + ("ssd_r" if rev else "ssd_f")))

    def ssm_out(blk, yf, yb, x, z, dskip, g):
        return (_rms((yf + yb + x * dskip) * _silu(z), g),)

    (ysn,) = rowwise(ssm_out, [ys_dir[0], ys_dir[1], xs, b_z], [], [jnp.repeat(s['ssm_D'], SSM_P)[None], s['ssm_norm'][None]],
                     [SSM_INNER], [f32], nm + "ssmout")

    pa, pb, pc = mm(ya, w['w_oa'], nm + "oa"), mm(ysn, w['w_ob'], nm + "ob"), mm(yc, w['w_oc'], nm + "oc")

    def merge(blk, ga, gb, gc, a, b, c):
        return (_sigmoid(ga) * a + _sigmoid(gb) * b + _sigmoid(gc) * c,)

    (mrg,) = rowwise(merge, [g_a, g_b, g_c, pa, pb, pc], [], [], [d], [f32], nm + "merge")
    o = mm(mrg, w['w_out'], nm + "out")

    def resid_norm_mod(blk, x, oo, g1_c, g1_l, g, sh_c, sh_l, sc_c, sc_l):
        x1 = x + pick(blk, g1_c, g1_l) * oo
        return x1, _rms(x1, g) * (1.0 + pick(blk, sc_c, sc_l)) + pick(blk, sh_c, sh_l)

    x1, h2 = rowwise(resid_norm_mod, [xall, o], [], [*mod[2], s['norm2'][None], *mod[3], *mod[4]], [d, d], [f32, f32], nm + "norm2")
    up, gt = mm(h2, w['ffn_w_up'], nm + "up"), mm(h2, w['ffn_w_gate'], nm + "gate")
    fw, fb = s['ffn_conv_w'], s['ffn_conv_b']
    ffn_act = lambda g_, u_, w0, w1, w2, b: _silu(_dwconv(g_, w0, w1, w2, b, n_ctx)) * u_
    act = colwise(ffn_act, [gt, up], [fw[0:1], fw[1:2], fw[2:3], fb[None]], f32, nm + "ffnact")
    f = mm(act, w['ffn_w_down'], nm + "down")

    def resid(blk, x, ff, g2_c, g2_l):
        return (x + pick(blk, g2_c, g2_l) * ff,)

    (x2,) = rowwise(resid, [x1, f], [], [*mod[5]], [d], [f32], nm + "resid")
    return x2


def _loss_fn(big, small, x, ctx, c, target, n_ctx):
    n_lat, d = x.shape
    depth = len(big)
    xall = jnp.concatenate([ctx, x], axis=0)
    ct, st = _rope_tables(n_ctx, n_lat)
    tabs = (jnp.tile(ct, (1, N_HEADS)) * HEAD_DIM ** -0.5, jnp.tile(st, (1, N_HEADS)) * HEAD_DIM ** -0.5,
            jnp.tile(ct, (1, N_KV)), jnp.tile(st, (1, N_KV)))
    srows = jnp.concatenate([_silu(small['c_ctx'])[None], _silu(c), jnp.zeros((14, d), f32)], axis=0)
    for li in range(depth):
        cm = mm(srows, big[li]['w_mod'], f"l{li}_mod")[0:2] + small['b_mod'][li][None]
        sl = {k: v[li] for k, v in small.items() if k not in ('c_ctx', 'final_norm')}
        xall = _layer(xall, big[li], sl, cm, tabs, n_ctx, li)
    ncb = n_ctx // ROW_TILE
    tgt = jnp.concatenate([jnp.zeros((n_ctx, d), f32), target], axis=0)

    def loss_rows(blk, xx, tg, g):
        e = _rms(xx, g) - tg
        return (jnp.where(blk < ncb, 0.0, 0.5) * jnp.mean(e * e, axis=-1, keepdims=True),)

    (rows,) = rowwise(loss_rows, [xall], [tgt], [small['final_norm'][None]], [1], [f32], "loss")
    return jnp.sum(rows)


def all_gather(shard, name):
    def body(x_ref, out_ref, send_sems, recv_sems, local_sem):
        x, y, c = lax.axis_index("x"), lax.axis_index("y"), lax.axis_index("c")
        me, sibling = (x, y, c), (x, y, 1 - c)
        chips = [(1 - x, y), (x, 1 - y), (1 - x, 1 - y)]

        def slot(px, py, pc):
            return out_ref.at[4 * px + 2 * py + pc]

        def copy(k, block, to, src=None):
            return pltpu.make_async_remote_copy(
                src_ref=slot(*block) if src is None else src, dst_ref=slot(*block),
                send_sem=send_sems.at[k], recv_sem=recv_sems.at[k], device_id=to, device_id_type=MESH)

        mine = pltpu.make_async_copy(x_ref, slot(*me), local_sem)
        mine.start()
        first = [copy(0, me, sibling, src=x_ref)]
        first += [copy(1 + j, me, (*chip, c), src=x_ref) for j, chip in enumerate(chips)]
        for cp in first:
            cp.start()
        passed = [copy(4 + j, (*chip, c), sibling) for j, chip in enumerate(chips)]
        for j, chip in enumerate(chips):
            copy(1 + j, (*chip, c), me).wait_recv()
            passed[j].start()
        copy(0, sibling, me).wait_recv()
        for j, chip in enumerate(chips):
            copy(4 + j, (*chip, 1 - c), me).wait_recv()
        for cp in first + passed:
            cp.wait_send()
        mine.wait()

    return pl.pallas_call(
        body, out_shape=jax.ShapeDtypeStruct((8,) + shard.shape, shard.dtype),
        in_specs=[pl.BlockSpec(memory_space=pl.ANY)], out_specs=pl.BlockSpec(memory_space=pl.ANY),
        scratch_shapes=[pltpu.SemaphoreType.DMA((7,)), pltpu.SemaphoreType.DMA((7,)), pltpu.SemaphoreType.DMA],
        name=name)(shard)


def rs_to_sibling(g):
    def body(g_ref, out_ref, send_sems, recv_sems):
        x, y, c = lax.axis_index("x"), lax.axis_index("y"), lax.axis_index("c")
        copies = [pltpu.make_async_remote_copy(
            src_ref=g_ref.at[2 * k + (1 - c)], dst_ref=out_ref.at[k], send_sem=send_sems.at[k], recv_sem=recv_sems.at[k],
            device_id=(x, y, 1 - c), device_id_type=MESH) for k in range(4)]
        for cp in copies:
            cp.start()
        for cp in copies:
            cp.wait()

    return pl.pallas_call(
        body, out_shape=jax.ShapeDtypeStruct((4,) + g.shape[1:], g.dtype),
        in_specs=[pl.BlockSpec(memory_space=pl.ANY)], out_specs=pl.BlockSpec(memory_space=pl.ANY),
        scratch_shapes=[pltpu.SemaphoreType.DMA((4,)), pltpu.SemaphoreType.DMA((4,))], name="rs_sibling")(g)


def rs_to_chips(s):
    flips = [(1, 0), (0, 1), (1, 1)]

    def body(s_ref, out_ref, send_sems, recv_sems):
        x, y, c = lax.axis_index("x"), lax.axis_index("y"), lax.axis_index("c")
        copies = []
        for k, (fx, fy) in enumerate(flips):
            px, py = (1 - x) if fx else x, (1 - y) if fy else y
            copies.append(pltpu.make_async_remote_copy(
                src_ref=s_ref.at[2 * px + py], dst_ref=out_ref.at[k], send_sem=send_sems.at[k], recv_sem=recv_sems.at[k],
                device_id=(px, py, c), device_id_type=MESH))
        for cp in copies:
            cp.start()
        for cp in copies:
            cp.wait()

    return pl.pallas_call(
        body, out_shape=jax.ShapeDtypeStruct((3,) + s.shape[1:], s.dtype),
        in_specs=[pl.BlockSpec(memory_space=pl.ANY)], out_specs=pl.BlockSpec(memory_space=pl.ANY),
        scratch_shapes=[pltpu.SemaphoreType.DMA((3,)), pltpu.SemaphoreType.DMA((3,))], name="rs_chips")(s)


def _flat_tile(rows):
    return _row_tile(rows, 16 * 1024)


def pair_sum(g, r1, my_c):
    _, rows, cols = g.shape
    tm = _flat_tile(rows)

    def body(c_ref, g_ref, r_ref, o_ref):
        o_ref[...] = (g_ref[...].astype(f32) + r_ref[...].astype(f32)).astype(o_ref.dtype)

    return pl.pallas_call(
        body, grid_spec=pltpu.PrefetchScalarGridSpec(
            num_scalar_prefetch=1, grid=(4, rows // tm),
            in_specs=[pl.BlockSpec((1, tm, cols), lambda k, i, c: (2 * k + c[0], i, 0)),
                      pl.BlockSpec((1, tm, cols), lambda k, i, c: (k, i, 0))],
            out_specs=pl.BlockSpec((1, tm, cols), lambda k, i, c: (k, i, 0))),
        out_shape=jax.ShapeDtypeStruct((4, rows, cols), g.dtype), name="rs_pair_sum",
        compiler_params=_cparams("parallel", "parallel"))(my_c, g, r1)


def _adam_math(w, g, m, v):
    m2 = ADAM_B1 * m + (1.0 - ADAM_B1) * g
    v2 = ADAM_B2 * v + (1.0 - ADAM_B2) * (g * g)
    m_hat = m2 / (1.0 - ADAM_B1 ** ADAM_STEP)
    v_hat = v2 / (1.0 - ADAM_B2 ** ADAM_STEP)
    return -ADAM_LR * (m_hat / (jnp.sqrt(v_hat) + ADAM_EPS) + ADAM_WD * w), m2, v2


def sum_adam(parts, w, m, v, name):
    rows, cols = w.shape
    tm = _flat_tile(rows)
    n = len(parts)
    flat = []
    scalars = [p[2] for p in parts if p[2] is not None]
    assert len(scalars) <= 1
    for arr, static_rows, dyn in parts:
        if dyn is not None:
            flat.append((arr, lambda i, s: (s[0], i, 0)))
        else:
            for k in static_rows:
                flat.append((arr, functools.partial(lambda i, s, k: (k, i, 0), k=k)))
    del n
    na = len(flat)

    def body(s_ref, *refs):
        g = refs[0][0].astype(f32)
        for r in refs[1:na]:
            g = g + r[0].astype(f32)
        w_ref, m_ref, v_ref = refs[na:na + 3]
        g_out, d_out, m_out, v_out = refs[na + 3:]
        d, m2, v2 = _adam_math(w_ref[...], g, m_ref[...], v_ref[...])
        g_out[...] = g
        d_out[...] = d
        m_out[...] = m2
        v_out[...] = v2

    blk = pl.BlockSpec((tm, cols), lambda i, s: (i, 0))
    scalar = scalars[0] if scalars else jnp.zeros((1,), jnp.int32)
    return pl.pallas_call(
        body, grid_spec=pltpu.PrefetchScalarGridSpec(
            num_scalar_prefetch=1, grid=(rows // tm,),
            in_specs=[pl.BlockSpec((1, tm, cols), im) for _, im in flat] + [blk, blk, blk],
            out_specs=[blk, blk, blk, blk]),
        out_shape=[jax.ShapeDtypeStruct((rows, cols), f32)] * 4, name=name,
        compiler_params=_cparams("parallel"))(scalar, *[a for a, _ in flat], w, m, v)


FLAT_COLS = 1024


def _to_flat(vec):
    n = vec.shape[0]
    total = -(-n // (8 * FLAT_COLS)) * 8 * FLAT_COLS
    return jnp.concatenate([vec, jnp.zeros((total - n,), vec.dtype)]).reshape(-1, FLAT_COLS)


def _pack(tree, names):
    return jnp.concatenate([tree[n].reshape(-1) for n in names])


def _unpack(vec, like, names):
    out, off = {}, 0
    for n in names:
        size = like[n].size
        out[n] = vec[off:off + size].reshape(like[n].shape)
        off += size
    return out


def kernel(x, c, ctx, c_ctx, w_mod, b_mod, norm1, norm2, w_in, a_sink, ssm_conv_w, ssm_conv_b, ssm_A_log, ssm_dt_bias, ssm_D, ssm_norm, c_q_norm, c_k_norm, w_oa, w_ob, w_oc, w_out, ffn_w_up, ffn_w_gate, ffn_conv_w, ffn_conv_b, ffn_w_down, final_norm, loss_target, m_c_ctx, m_w_mod, m_b_mod, m_norm1, m_norm2, m_w_in, m_a_sink, m_ssm_conv_w, m_ssm_conv_b, m_ssm_A_log, m_ssm_dt_bias, m_ssm_D, m_ssm_norm, m_c_q_norm, m_c_k_norm, m_w_oa, m_w_ob, m_w_oc, m_w_out, m_ffn_w_up, m_ffn_w_gate, m_ffn_conv_w, m_ffn_conv_b, m_ffn_w_down, m_final_norm, v_c_ctx, v_w_mod, v_b_mod, v_norm1, v_norm2, v_w_in, v_a_sink, v_ssm_conv_w, v_ssm_conv_b, v_ssm_A_log, v_ssm_dt_bias, v_ssm_D, v_ssm_norm, v_c_q_norm, v_c_k_norm, v_w_oa, v_w_ob, v_w_oc, v_w_out, v_ffn_w_up, v_ffn_w_gate, v_ffn_conv_w, v_ffn_conv_b, v_ffn_w_down, v_final_norm):
    args = (x, c, ctx, c_ctx, w_mod, b_mod, norm1, norm2, w_in, a_sink, ssm_conv_w, ssm_conv_b, ssm_A_log, ssm_dt_bias, ssm_D, ssm_norm, c_q_norm, c_k_norm, w_oa, w_ob, w_oc, w_out, ffn_w_up, ffn_w_gate, ffn_conv_w, ffn_conv_b, ffn_w_down, final_norm, loss_target)
    moms = (m_c_ctx, m_w_mod, m_b_mod, m_norm1, m_norm2, m_w_in, m_a_sink, m_ssm_conv_w, m_ssm_conv_b, m_ssm_A_log, m_ssm_dt_bias, m_ssm_D, m_ssm_norm, m_c_q_norm, m_c_k_norm, m_w_oa, m_w_ob, m_w_oc, m_w_out, m_ffn_w_up, m_ffn_w_gate, m_ffn_conv_w, m_ffn_conv_b, m_ffn_w_down, m_final_norm)
    vars_ = (v_c_ctx, v_w_mod, v_b_mod, v_norm1, v_norm2, v_w_in, v_a_sink, v_ssm_conv_w, v_ssm_conv_b, v_ssm_A_log, v_ssm_dt_bias, v_ssm_D, v_ssm_norm, v_c_q_norm, v_c_k_norm, v_w_oa, v_w_ob, v_w_oc, v_w_out, v_ffn_w_up, v_ffn_w_gate, v_ffn_conv_w, v_ffn_conv_b, v_ffn_w_down, v_final_norm)
    p = dict(zip(IN_NAMES, args))
    mom = dict(zip(WEIGHTS, moms))
    var = dict(zip(WEIGHTS, vars_))
    depth = w_in.shape[0]
    n_ctx = ctx.shape[1]
    xi, yi, ci = lax.axis_index("x"), lax.axis_index("y"), lax.axis_index("c")
    dev = 4 * xi + 2 * yi + ci
    big_names = list(BIG)

    n_big = sum(p[n].size for n in big_names)
    g_big = all_gather(_to_flat(_pack(p, big_names)).astype(bf16), "gather_big").reshape(8, -1)
    g_conv = all_gather(_to_flat(_pack(p, CONV_W)), "gather_conv").reshape(8, -1)

    def full_from(gathered, names, like):
        out, off = {}, 0
        for n in names:
            shp = like[n].shape
            seg = gathered[:, off:off + like[n].size].reshape(8, *shp)
            if BIG.get(n, 1) == 1:
                out[n] = jnp.moveaxis(seg, 0, -2).reshape(*shp[:-1], 8 * shp[-1])
            else:
                out[n] = jnp.moveaxis(seg, 0, 1).reshape(shp[0], 8 * shp[1], shp[2])
            off += like[n].size
        return out

    big_full = full_from(g_big, big_names, p)
    conv_full = full_from(g_conv, CONV_W, p)
    big_layers = [{n: big_full[n][li] for n in big_names} for li in range(depth)]
    small = {n: p[n] for n in REPL}
    small.update(conv_full)

    loss, (g_layers, g_small, g_x) = jax.value_and_grad(_loss_fn, argnums=(0, 1, 2))(
        big_layers, small, x[0], ctx[0], c, loss_target[0], n_ctx)
    loss = lax.psum(loss, AXES)

    def shard_rows(n):
        g = jnp.stack([g_layers[li][n] for li in range(depth)])
        shp = p[n].shape
        if BIG[n] == 1:
            return jnp.moveaxis(g.reshape(shp[0], shp[1], 8, shp[2]), 2, 0).reshape(8, -1)
        return jnp.moveaxis(g.reshape(shp[0], 8, shp[1], shp[2]), 1, 0).reshape(8, -1)

    send = jnp.concatenate([shard_rows(n) for n in big_names], axis=1)
    pad = -n_big % (8 * FLAT_COLS)
    send = jnp.concatenate([send, jnp.zeros((8, pad), send.dtype)], axis=1).reshape(8, -1, FLAT_COLS)
    from_sibling = rs_to_sibling(send)
    side_sum = pair_sum(send, from_sibling, ci.reshape(1).astype(jnp.int32))
    from_chips = rs_to_chips(side_sum)
    chip = (2 * xi + yi).reshape(1).astype(jnp.int32)
    outs_big = sum_adam([(side_sum, None, chip), (from_chips, (0, 1, 2), None)],
                        _to_flat(_pack(p, big_names)), _to_flat(_pack(mom, big_names)), _to_flat(_pack(var, big_names)), "adam_big")
    big_out = [_unpack(o.reshape(-1), p, big_names) for o in outs_big]

    sm_names = REPL + list(CONV_W)
    g_vec = _to_flat(_pack(g_small, sm_names))
    gathered = all_gather(g_vec, "gather_small_grads")
    n_repl = sum(p[n].size for n in REPL)
    zero = jnp.zeros_like(g_vec)

    def repl_flat(tree):
        return _to_flat(jnp.concatenate([_pack(tree, REPL), jnp.zeros((g_vec.size - n_repl,), f32)]))

    outs_small = sum_adam([(gathered, tuple(range(8)), None)], repl_flat(p), repl_flat(mom), repl_flat(var), "adam_small")
    del zero
    g_sum = outs_small[0].reshape(-1)
    small_out = [_unpack(o.reshape(-1), p, REPL) for o in outs_small]
    conv_g_full = _unpack(g_sum[n_repl:], conv_full, CONV_W)
    conv_g = {n: lax.dynamic_slice_in_dim(conv_g_full[n], dev * p[n].shape[-1], p[n].shape[-1], axis=2) for n in CONV_W}
    conv_gv = _to_flat(_pack(conv_g, CONV_W))
    outs_conv = sum_adam([(conv_gv[None], (0,), None)], _to_flat(_pack(p, CONV_W)), _to_flat(_pack(mom, CONV_W)),
                         _to_flat(_pack(var, CONV_W)), "adam_conv")
    conv_out = [_unpack(o.reshape(-1), p, CONV_W) for o in outs_conv]

    res = []
    for k in range(4):
        tree = {**big_out[k], **small_out[k], **conv_out[k]}
        res.append([tree[n] for n in WEIGHTS])
    return (loss, g_x[None], *res[0], *res[1], *res[2], *res[3])
```

```python
import functools

import jax
import jax.numpy as jnp
from jax import lax
from jax.experimental import pallas as pl
from jax.experimental.pallas import tpu as pltpu

f32 = jnp.float32
bf16 = jnp.bfloat16
MESH = pl.DeviceIdType.MESH
AXES = ("x", "y", "c")

GRID_W = 64
HEAD_DIM = 64
ROPE_BASE = 10000.0
EPS = 1e-6
WINDOW = 128
N_HEADS = 8
N_KV = 2
SSM_HEADS = 16
SSM_P = 64
SSM_G = 2
SSM_N = 128
SSM_INNER = SSM_HEADS * SSM_P
SSM_BC = SSM_G * SSM_N
SSM_Q = 128
Q_W = N_HEADS * HEAD_DIM
KV_W = N_KV * HEAD_DIM
DT_W = 2 * SSM_HEADS
DT_PAD = 128
ADAM_LR, ADAM_B1, ADAM_B2, ADAM_EPS, ADAM_WD, ADAM_STEP = 0.001, 0.9, 0.999, 1e-08, 0.01, 10

LANES = 128
ROW_TILE = 256
VMEM_BLOCK_BUDGET = 6 * 1024 * 1024
MM_ROW_CAP = 1088
MM_TILE_CAP = 1536
NEG = -1e30

IN_NAMES = ['x', 'c', 'ctx', 'c_ctx', 'w_mod', 'b_mod', 'norm1', 'norm2', 'w_in', 'a_sink', 'ssm_conv_w', 'ssm_conv_b', 'ssm_A_log', 'ssm_dt_bias', 'ssm_D', 'ssm_norm', 'c_q_norm', 'c_k_norm', 'w_oa', 'w_ob', 'w_oc', 'w_out', 'ffn_w_up', 'ffn_w_gate', 'ffn_conv_w', 'ffn_conv_b', 'ffn_w_down', 'final_norm', 'loss_target']
WEIGHTS = IN_NAMES[3:28]
BIG = {'w_mod': 1, 'w_in': 1, 'w_oa': 1, 'w_ob': 0, 'w_oc': 1, 'w_out': 0, 'ffn_w_up': 1, 'ffn_w_gate': 1, 'ffn_w_down': 0}
CONV_W = ('ssm_conv_w', 'ffn_conv_w')
REPL = [n for n in WEIGHTS if n not in BIG and n not in CONV_W]

NT = (((1,), (1,)), ((), ()))
TN = (((0,), (0,)), ((), ()))
NN = (((1,), (0,)), ((), ()))


def _cparams(*sem):
    return pltpu.CompilerParams(dimension_semantics=sem)


def _div_tile(n, unit, cap):
    for d in range(min(n, int(cap)), 0, -1):
        if n % d == 0 and d % unit == 0:
            return d
    return n


def _row_tile(m, row_bytes):
    return _div_tile(m, 16, max(16, VMEM_BLOCK_BUDGET // row_bytes))


def _mm_call(a, b, mode, out_dtype, name):
    if mode == "nn":
        (m, k), n = a.shape, b.shape[1]
    elif mode == "nt":
        (m, k), n = a.shape, b.shape[0]
    else:
        (k, m), n = a.shape, b.shape[1]
    dims = {"nn": NN, "nt": NT, "tn": TN}[mode]
    ia, ib = a.dtype.itemsize, b.dtype.itemsize
    tm = _div_tile(m, LANES, MM_TILE_CAP) if mode == "tn" else _div_tile(m, 16, MM_ROW_CAP)
    tn = _div_tile(n, LANES, min(MM_TILE_CAP, VMEM_BLOCK_BUDGET // (4 * tm)))
    tk = _div_tile(k, 16 if mode == "tn" else LANES,
                   min(MM_ROW_CAP if mode == "tn" else MM_TILE_CAP, VMEM_BLOCK_BUDGET // (tm * ia), VMEM_BLOCK_BUDGET // (tn * ib)))
    nk = k // tk

    def body(a_ref, b_ref, o_ref, *acc):
        part = lax.dot_general(a_ref[...].astype(bf16), b_ref[...].astype(bf16), dims, preferred_element_type=f32)
        if nk == 1:
            o_ref[...] = part.astype(o_ref.dtype)
            return
        kk = pl.program_id(2)

        @pl.when(kk == 0)
        def _():
            acc[0][...] = part

        @pl.when(kk > 0)
        def _():
            acc[0][...] += part

        @pl.when(kk == nk - 1)
        def _():
            o_ref[...] = acc[0][...].astype(o_ref.dtype)

    a_spec = pl.BlockSpec((tk, tm), lambda i, j, kk: (kk, i)) if mode == "tn" else pl.BlockSpec((tm, tk), lambda i, j, kk: (i, kk))
    b_spec = pl.BlockSpec((tn, tk), lambda i, j, kk: (j, kk)) if mode == "nt" else pl.BlockSpec((tk, tn), lambda i, j, kk: (kk, j))
    return pl.pallas_call(
        body, grid=(m // tm, n // tn, nk), in_specs=[a_spec, b_spec],
        out_specs=pl.BlockSpec((tm, tn), lambda i, j, kk: (i, j)),
        out_shape=jax.ShapeDtypeStruct((m, n), out_dtype),
        scratch_shapes=[pltpu.VMEM((tm, tn), f32)] if nk > 1 else [], name=name,
        compiler_params=_cparams("parallel", "parallel", "arbitrary"))(a, b)


def mm(a, b, name):
    @jax.custom_vjp
    def op(a, b):
        return _mm_call(a, b, "nn", f32, name)

    def fwd(a, b):
        return op(a, b), (a, b)

    def bwd(res, g):
        a, b = res
        return _mm_call(g, b, "nt", a.dtype, name + "_da"), _mm_call(a, g, "tn", b.dtype, name + "_db")

    op.defvjp(fwd, bwd)
    return op(a, b)


def split_cols(u, widths):
    offs = [0]
    for w in widths:
        offs.append(offs[-1] + w)

    @jax.custom_vjp
    def op(u):
        return tuple(u[:, offs[i]:offs[i + 1]] for i in range(len(widths)))

    def fwd(u):
        return op(u), None

    def bwd(_, cts):
        return (jnp.concatenate(cts, axis=1),)

    op.defvjp(fwd, bwd)
    return op(u)


def rowwise(fn, rows, consts, pars, out_widths, out_dtypes, name):
    t = rows[0].shape[0]
    tm = ROW_TILE
    nb = t // tm
    nr, nc, npar = len(rows), len(consts), len(pars)

    def rspec(a):
        return pl.BlockSpec((tm, a.shape[1]), lambda i: (i, 0))

    def pspec(a):
        return pl.BlockSpec(a.shape, lambda i: (0,) * a.ndim)

    def call_fwd(rows, consts, pars):
        def body(*refs):
            blk = pl.program_id(0)
            ins = [r[...].astype(f32) for r in refs[:nr + nc]]
            ps = [r[...] for r in refs[nr + nc:nr + nc + npar]]
            outs = fn(blk, *ins, *ps)
            for o_ref, o in zip(refs[nr + nc + npar:], outs):
                o_ref[...] = o.astype(o_ref.dtype)

        return pl.pallas_call(
            body, grid=(nb,),
            in_specs=[rspec(a) for a in rows + consts] + [pspec(a) for a in pars],
            out_specs=[pl.BlockSpec((tm, w), lambda i: (i, 0)) for w in out_widths],
            out_shape=[jax.ShapeDtypeStruct((t, w), d) for w, d in zip(out_widths, out_dtypes)],
            name=name, compiler_params=_cparams("parallel"))(*rows, *consts, *pars)

    def call_bwd(rows, consts, pars, cts):
        nout = len(cts)

        def body(*refs):
            blk = pl.program_id(0)
            ins = [r[...].astype(f32) for r in refs[:nr]]
            cs = [r[...].astype(f32) for r in refs[nr:nr + nc]]
            ps = [r[...] for r in refs[nr + nc:nr + nc + npar]]
            dys = [r[...].astype(f32) for r in refs[nr + nc + npar:nr + nc + npar + nout]]
            d_refs = refs[nr + nc + npar + nout:]
            _, vjp = jax.vjp(lambda *a: tuple(fn(blk, *a[:nr], *cs, *a[nr:])), *ins, *ps)
            grads = vjp(tuple(dys))
            for d_ref, g in zip(d_refs[:nr], grads[:nr]):
                d_ref[...] = g.astype(d_ref.dtype)
            if npar:
                @pl.when(blk == 0)
                def _():
                    for d_ref in d_refs[nr:]:
                        d_ref[...] = jnp.zeros_like(d_ref)

                for d_ref, g in zip(d_refs[nr:], grads[nr:]):
                    d_ref[...] += g

        return pl.pallas_call(
            body, grid=(nb,),
            in_specs=[rspec(a) for a in rows + consts] + [pspec(a) for a in pars] + [rspec(a) for a in cts],
            out_specs=[rspec(a) for a in rows] + [pspec(a) for a in pars],
            out_shape=[jax.ShapeDtypeStruct(a.shape, a.dtype) for a in rows + pars],
            name=name + "_bwd", compiler_params=_cparams("arbitrary"))(*rows, *consts, *pars, *cts)

    @jax.custom_vjp
    def op(rows, consts, pars):
        return tuple(call_fwd(list(rows), list(consts), list(pars)))

    def fwd(rows, consts, pars):
        return op(rows, consts, pars), (rows, consts, pars)

    def bwd(res, cts):
        rows, consts, pars = res
        g = call_bwd(list(rows), list(consts), list(pars), list(cts))
        return tuple(g[:nr]), tuple(jnp.zeros_like(a) for a in consts), tuple(g[nr:])

    op.defvjp(fwd, bwd)
    return op(tuple(rows), tuple(consts), tuple(pars))


def colwise(fn, cols, pars, out_dtype, name):
    t, w = cols[0].shape
    tc = LANES
    nb = w // tc
    ncol, npar = len(cols), len(pars)

    def cspec(a):
        return pl.BlockSpec((a.shape[0], tc), lambda j: (0, j))

    def call_fwd(cols, pars):
        def body(*refs):
            ins = [r[...].astype(f32) for r in refs[:ncol]]
            ps = [r[...] for r in refs[ncol:ncol + npar]]
            refs[-1][...] = fn(*ins, *ps).astype(refs[-1].dtype)

        return pl.pallas_call(
            body, grid=(nb,), in_specs=[cspec(a) for a in cols + pars], out_specs=cspec(cols[0]),
            out_shape=jax.ShapeDtypeStruct((t, w), out_dtype), name=name, compiler_params=_cparams("parallel"))(*cols, *pars)

    def call_bwd(cols, pars, ct):
        def body(*refs):
            ins = [r[...].astype(f32) for r in refs[:ncol]]
            ps = [r[...] for r in refs[ncol:ncol + npar]]
            dy = refs[ncol + npar][...].astype(f32)
            d_refs = refs[ncol + npar + 1:]
            _, vjp = jax.vjp(fn, *ins, *ps)
            grads = vjp(dy)
            for d_ref, g in zip(d_refs, grads):
                d_ref[...] = g.astype(d_ref.dtype)

        return pl.pallas_call(
            body, grid=(nb,), in_specs=[cspec(a) for a in cols + pars + [ct]],
            out_specs=[cspec(a) for a in cols + pars],
            out_shape=[jax.ShapeDtypeStruct(a.shape, a.dtype) for a in cols + pars],
            name=name + "_bwd", compiler_params=_cparams("parallel"))(*cols, *pars, ct)

    @jax.custom_vjp
    def op(cols, pars):
        return call_fwd(list(cols), list(pars))

    def fwd(cols, pars):
        return op(cols, pars), (cols, pars)

    def bwd(res, ct):
        cols, pars = res
        g = call_bwd(list(cols), list(pars), ct)
        return tuple(g[:ncol]), tuple(g[ncol:])

    op.defvjp(fwd, bwd)
    return op(tuple(cols), tuple(pars))


def _sigmoid(x):
    return 1.0 / (1.0 + jnp.exp(-x))


def _silu(x):
    return x * _sigmoid(x)


def _rms(x, g):
    return x * lax.rsqrt(jnp.mean(x * x, axis=-1, keepdims=True) + EPS) * g


def _shift_rows(u, k, n_ctx):
    @jax.custom_vjp
    def op(u):
        t = u.shape[0]
        row = lax.broadcasted_iota(jnp.int32, u.shape, 0)
        edge = ((row == 0) | (row == n_ctx)) if k == 1 else ((row == n_ctx - 1) | (row == t - 1))
        return jnp.where(edge, 0.0, pltpu.roll(u, k % t, 0))

    op.defvjp(lambda u: (op(u), None), lambda _, g: (_shift_rows(g, -k, n_ctx),))
    return op(u)


def _dwconv(u, w0, w1, w2, b, n_ctx):
    return w0 * _shift_rows(u, 1, n_ctx) + w1 * u + w2 * _shift_rows(u, -1, n_ctx) + b


@jax.custom_vjp
def _rot_half(x):
    w = x.shape[1]
    lane = lax.broadcasted_iota(jnp.int32, x.shape, 1)
    return jnp.where((lane % HEAD_DIM) < HEAD_DIM // 2, pltpu.roll(x, w - HEAD_DIM // 2, 1), pltpu.roll(x, HEAD_DIM // 2, 1))


_rot_half.defvjp(lambda x: (_rot_half(x), None), lambda _, g: (_rot_half(g),))


def _head_rms(x, g):
    w = x.shape[1]
    same = (lax.broadcasted_iota(jnp.int32, (w, w), 0) // HEAD_DIM) == (lax.broadcasted_iota(jnp.int32, (w, w), 1) // HEAD_DIM)
    ms = jnp.dot(x * x, same.astype(f32), precision=lax.Precision.HIGHEST, preferred_element_type=f32) * (1.0 / HEAD_DIM)
    return x * lax.rsqrt(ms + EPS) * g


def _band_ok(i, j, shape, tq, tk):
    qpos = i * tq + lax.broadcasted_iota(jnp.int32, shape, 0) % tq
    kpos = j * tk + lax.broadcasted_iota(jnp.int32, shape, 1)
    return jnp.abs(qpos - kpos) <= WINDOW


def _kv_range(i, nb, window):
    is_ctx = i == 0
    if window:
        return jnp.where(is_ctx, 1, jnp.maximum(i - 1, 1)), jnp.where(is_ctx, 1, jnp.minimum(i + 2, nb))
    return 1, jnp.where(is_ctx, 1, nb)


def _sink_col(sink_ref, g, r, tq):
    return jnp.concatenate([jnp.full((tq, 1), sink_ref[g * r + h], f32) for h in range(r)], axis=0)


def _attn_fwd_call(q, k, v, sink, window, name):
    h, t, dh = q.shape
    r = h // k.shape[0]
    tq = tk = ROW_TILE
    nb = t // tq

    def body(sink_ref, q_ref, k_ref, v_ref, o_ref, lse_ref):
        g, i = pl.program_id(0), pl.program_id(1)
        qv = q_ref[...].reshape(r * tq, dh)

        def chunk(j, carry, masked):
            m, l, acc = carry
            start = pl.multiple_of(j * tk, tk)
            kj = k_ref[0, pl.ds(start, tk), :]
            vj = v_ref[0, pl.ds(start, tk), :]
            s = lax.dot_general(qv, kj, NT, preferred_element_type=f32)
            if masked:
                s = jnp.where(_band_ok(i, j, s.shape, tq, tk), s, NEG)
            m2 = jnp.maximum(m, jnp.max(s, axis=1, keepdims=True))
            p = jnp.exp(s - m2)
            a = jnp.exp(m - m2)
            return m2, a * l + jnp.sum(p, axis=1, keepdims=True), a * acc + jnp.dot(p.astype(bf16), vj, preferred_element_type=f32)

        init = (jnp.full((r * tq, 1), NEG, f32), jnp.zeros((r * tq, 1), f32), jnp.zeros((r * tq, dh), f32))
        carry = chunk(0, init, False)
        lo, hi = _kv_range(i, nb, window)
        m, l, acc = lax.fori_loop(lo, hi, lambda j, c: chunk(j, c, window), carry)
        if window:
            sk = _sink_col(sink_ref, g, r, tq)
            m2 = jnp.maximum(m, sk)
            a = jnp.exp(m - m2)
            l = a * l + jnp.exp(sk - m2)
            acc = a * acc
            m = m2
        o_ref[...] = (acc / l).reshape(r, tq, dh).astype(o_ref.dtype)
        lse_ref[...] = (m + jnp.log(l)).reshape(r, tq, 1)

    return pl.pallas_call(
        body, grid=(k.shape[0], nb),
        in_specs=[pl.BlockSpec(memory_space=pltpu.SMEM),
                  pl.BlockSpec((r, tq, dh), lambda g, i: (g, i, 0)),
                  pl.BlockSpec((1, t, dh), lambda g, i: (g, 0, 0)),
                  pl.BlockSpec((1, t, dh), lambda g, i: (g, 0, 0))],
        out_specs=[pl.BlockSpec((r, tq, dh), lambda g, i: (g, i, 0)), pl.BlockSpec((r, tq, 1), lambda g, i: (g, i, 0))],
        out_shape=[jax.ShapeDtypeStruct((h, t, dh), bf16), jax.ShapeDtypeStruct((h, t, 1), f32)],
        name=name, compiler_params=_cparams("parallel", "parallel"))(sink, q, k, v)


def _attn_bwd_call(q, k, v, sink, o, lse, do, window, name):
    h, t, dh = q.shape
    r = h // k.shape[0]
    tq = tk = ROW_TILE
    nb = t // tq

    def body(sink_ref, q_ref, k_ref, v_ref, o_ref, lse_ref, do_ref, dq_ref, dk_ref, dv_ref, dsink_ref):
        g, i = pl.program_id(0), pl.program_id(1)

        @pl.when(i == 0)
        def _():
            dk_ref[...] = jnp.zeros_like(dk_ref)
            dv_ref[...] = jnp.zeros_like(dv_ref)

        qv = q_ref[...].reshape(r * tq, dh)
        dov = do_ref[...].reshape(r * tq, dh)
        lse_v = lse_ref[...].reshape(r * tq, 1)
        delta = jnp.sum(dov.astype(f32) * o_ref[...].reshape(r * tq, dh).astype(f32), axis=1, keepdims=True)

        def chunk(j, dq, masked):
            rows = pl.ds(pl.multiple_of(j * tk, tk), tk)
            kj = k_ref[0, rows, :]
            vj = v_ref[0, rows, :]
            s = lax.dot_general(qv, kj, NT, preferred_element_type=f32)
            if masked:
                s = jnp.where(_band_ok(i, j, s.shape, tq, tk), s, NEG)
            p = jnp.exp(s - lse_v)
            dv_ref[0, rows, :] += lax.dot_general(p.astype(bf16), dov, TN, preferred_element_type=f32)
            dp = lax.dot_general(dov, vj, NT, preferred_element_type=f32)
            ds = (p * (dp - delta)).astype(bf16)
            dk_ref[0, rows, :] += lax.dot_general(ds, qv, TN, preferred_element_type=f32)
            return dq + jnp.dot(ds, kj, preferred_element_type=f32)

        dq = chunk(0, jnp.zeros((r * tq, dh), f32), False)
        lo, hi = _kv_range(i, nb, window)
        dq = lax.fori_loop(lo, hi, lambda j, c: chunk(j, c, window), dq)
        dq_ref[...] = dq.reshape(r, tq, dh).astype(dq_ref.dtype)
        if window:
            dsink_ref[...] = (-jnp.exp(_sink_col(sink_ref, g, r, tq) - lse_v) * delta).reshape(r, tq, 1)
        else:
            dsink_ref[...] = jnp.zeros_like(dsink_ref)

    qspec = pl.BlockSpec((r, tq, dh), lambda g, i: (g, i, 0))
    cspec = pl.BlockSpec((r, tq, 1), lambda g, i: (g, i, 0))
    kspec = pl.BlockSpec((1, t, dh), lambda g, i: (g, 0, 0))
    return pl.pallas_call(
        body, grid=(k.shape[0], nb),
        in_specs=[pl.BlockSpec(memory_space=pltpu.SMEM), qspec, kspec, kspec, qspec, cspec, qspec],
        out_specs=[qspec, kspec, kspec, cspec],
        out_shape=[jax.ShapeDtypeStruct((h, t, dh), bf16), jax.ShapeDtypeStruct(k.shape, f32), jax.ShapeDtypeStruct(v.shape, f32),
                   jax.ShapeDtypeStruct((h, t, 1), f32)],
        name=name, compiler_params=_cparams("parallel", "arbitrary"))(sink, q, k, v, o, lse, do)


def attention(q, k, v, sink, window, name):
    @jax.custom_vjp
    def op(q, k, v, sink):
        return _attn_fwd_call(q, k, v, sink, window, name)[0]

    def fwd(q, k, v, sink):
        o, lse = _attn_fwd_call(q, k, v, sink, window, name)
        return o, (q, k, v, sink, o, lse)

    def bwd(res, do):
        q, k, v, sink, o, lse = res
        dq, dk, dv, dsink_rows = _attn_bwd_call(q, k, v, sink, o, lse, do, window, name + "_bwd")
        return dq, dk.astype(k.dtype), dv.astype(v.dtype), jnp.sum(dsink_rows, axis=(1, 2))

    op.defvjp(fwd, bwd)
    return op(q, k, v, sink)


def _ssd_pair(xs_p, dtx_p, dtr1, dtr2, ac1, ac2, bg, cg, hin_p, rev):
    q = xs_p.shape[0]
    ii = lax.broadcasted_iota(jnp.int32, (q, q), 0)
    jj = lax.broadcasted_iota(jnp.int32, (q, q), 1)
    tri = (ii <= jj) if rev else (ii >= jj)
    lo = lax.broadcasted_iota(jnp.int32, (q, LANES), 1) < SSM_P
    lo_row = lax.broadcasted_iota(jnp.int32, (1, LANES), 1) < SSM_P

    def cums(dtr, ac):
        a = dtr * ac
        c = jnp.sum(jnp.where(tri, jnp.broadcast_to(a, (q, q)), 0.0), axis=1, keepdims=True)
        return c, jnp.sum(a, axis=1, keepdims=True)

    def lmat(c):
        cf = jnp.broadcast_to(c, (q, q))
        return jnp.where(tri, jnp.exp(jnp.minimum(cf - cf.T, 0.0)), 0.0)

    c1, t1 = cums(dtr1, ac1)
    c2, t2 = cums(dtr2, ac2)
    cb = lax.dot_general(cg.astype(bf16), bg.astype(bf16), NT, preferred_element_type=f32)
    m = jnp.concatenate([cb * lmat(c1), cb * lmat(c2)], axis=1)
    xdt = xs_p * dtx_p
    x2 = jnp.concatenate([jnp.where(lo, xdt, 0.0), jnp.where(lo, 0.0, xdt)], axis=0)
    y_diag = jnp.dot(m.astype(bf16), x2.astype(bf16), preferred_element_type=f32)
    csel = jnp.where(lo, jnp.broadcast_to(c1, (q, LANES)), jnp.broadcast_to(c2, (q, LANES)))
    tsel = jnp.where(lo_row, jnp.broadcast_to(t1, (1, LANES)), jnp.broadcast_to(t2, (1, LANES)))
    st = lax.dot_general(bg.astype(bf16), (xdt * jnp.exp(tsel - csel)).astype(bf16), TN, preferred_element_type=f32)
    y_off = jnp.dot(cg.astype(bf16), hin_p.astype(bf16), preferred_element_type=f32) * jnp.exp(csel)
    return y_diag + y_off, hin_p * jnp.exp(tsel) + st


def _ssd_order(s, nc, ncc, rev):
    if not rev:
        return s
    return jnp.where(s < ncc, ncc - 1 - s, nc - 1 - (s - ncc))


def _ssd_pair_slices(j):
    return slice(LANES * j, LANES * (j + 1)), 2 * j, 2 * j + 1, (2 * j) // (SSM_HEADS // SSM_G)


def _ssd_fwd_call(xs, dtx, dtr, bm, cm, acol, rev, n_ctx, name):
    t, w = xs.shape
    q = SSM_Q
    nc, ncc = t // q, n_ctx // q

    def body(xs_ref, dtx_ref, dtr_ref, b_ref, c_ref, a_ref, y_ref, hin_ref, h_scr):
        @pl.when(pl.program_id(0) == 0)
        def _():
            h_scr[...] = jnp.zeros_like(h_scr)

        hin_ref[0] = h_scr[...]
        for j in range(SSM_HEADS // 2):
            sl, h1, h2, g = _ssd_pair_slices(j)
            gs = slice(SSM_N * g, SSM_N * (g + 1))
            y, hout = _ssd_pair(xs_ref[:, sl], dtx_ref[:, sl], dtr_ref[h1:h1 + 1, :], dtr_ref[h2:h2 + 1, :],
                                a_ref[h1:h1 + 1, :], a_ref[h2:h2 + 1, :], b_ref[:, gs], c_ref[:, gs], h_scr[:, sl], rev)
            y_ref[:, sl] = y
            h_scr[:, sl] = hout

    def at(s):
        return _ssd_order(s, nc, ncc, rev)

    return pl.pallas_call(
        body, grid=(nc,),
        in_specs=[pl.BlockSpec((q, w), lambda s: (at(s), 0)), pl.BlockSpec((q, w), lambda s: (at(s), 0)),
                  pl.BlockSpec((SSM_HEADS, q), lambda s: (0, at(s))),
                  pl.BlockSpec((q, SSM_BC), lambda s: (at(s), 0)), pl.BlockSpec((q, SSM_BC), lambda s: (at(s), 0)),
                  pl.BlockSpec((SSM_HEADS, 1), lambda s: (0, 0))],
        out_specs=[pl.BlockSpec((q, w), lambda s: (at(s), 0)), pl.BlockSpec((1, SSM_N, w), lambda s: (s, 0, 0))],
        out_shape=[jax.ShapeDtypeStruct((t, w), f32), jax.ShapeDtypeStruct((nc, SSM_N, w), f32)],
        scratch_shapes=[pltpu.VMEM((SSM_N, w), f32)],
        name=name, compiler_params=_cparams("arbitrary"))(xs, dtx, dtr, bm, cm, acol)


def _ssd_bwd_call(xs, dtx, dtr, bm, cm, acol, hin, dy, rev, n_ctx, name):
    t, w = xs.shape
    q = SSM_Q
    nc, ncc = t // q, n_ctx // q

    def body(xs_ref, dtx_ref, dtr_ref, b_ref, c_ref, a_ref, hin_ref, dy_ref,
             dxs_ref, ddtx_ref, ddtr_ref, db_ref, dc_ref, da_ref, dh_scr):
        @pl.when(pl.program_id(0) == 0)
        def _():
            dh_scr[...] = jnp.zeros_like(dh_scr)
            da_ref[...] = jnp.zeros_like(da_ref)

        db = [None] * SSM_G
        dc = [None] * SSM_G
        for j in range(SSM_HEADS // 2):
            sl, h1, h2, g = _ssd_pair_slices(j)
            gs = slice(SSM_N * g, SSM_N * (g + 1))
            _, vjp = jax.vjp(
                functools.partial(_ssd_pair, rev=rev),
                xs_ref[:, sl], dtx_ref[:, sl], dtr_ref[h1:h1 + 1, :], dtr_ref[h2:h2 + 1, :],
                a_ref[h1:h1 + 1, :], a_ref[h2:h2 + 1, :], b_ref[:, gs], c_ref[:, gs], hin_ref[0, :, sl])
            gr = vjp((dy_ref[:, sl], dh_scr[:, sl]))
            dxs_ref[:, sl] = gr[0]
            ddtx_ref[:, sl] = gr[1]
            ddtr_ref[h1:h1 + 1, :] = gr[2]
            ddtr_ref[h2:h2 + 1, :] = gr[3]
            da_ref[h1:h1 + 1, :] += gr[4]
            da_ref[h2:h2 + 1, :] += gr[5]
            db[g] = gr[6] if db[g] is None else db[g] + gr[6]
            dc[g] = gr[7] if dc[g] is None else dc[g] + gr[7]
            dh_scr[:, sl] = gr[8]
        for g in range(SSM_G):
            gs = slice(SSM_N * g, SSM_N * (g + 1))
            db_ref[:, gs] = db[g]
            dc_ref[:, gs] = dc[g]

    def step(s):
        return nc - 1 - s

    def at(s):
        return _ssd_order(step(s), nc, ncc, rev)

    row = lambda wd: pl.BlockSpec((q, wd), lambda s: (at(s), 0))
    dtr_spec = pl.BlockSpec((SSM_HEADS, q), lambda s: (0, at(s)))
    a_spec = pl.BlockSpec((SSM_HEADS, 1), lambda s: (0, 0))
    return pl.pallas_call(
        body, grid=(nc,),
        in_specs=[row(w), row(w), dtr_spec, row(SSM_BC), row(SSM_BC), a_spec,
                  pl.BlockSpec((1, SSM_N, w), lambda s: (step(s), 0, 0)), row(w)],
        out_specs=[row(w), row(w), dtr_spec, row(SSM_BC), row(SSM_BC), a_spec],
        out_shape=[jax.ShapeDtypeStruct((t, w), f32), jax.ShapeDtypeStruct((t, w), f32), jax.ShapeDtypeStruct(dtr.shape, f32),
                   jax.ShapeDtypeStruct(bm.shape, f32), jax.ShapeDtypeStruct(cm.shape, f32), jax.ShapeDtypeStruct(acol.shape, f32)],
        scratch_shapes=[pltpu.VMEM((SSM_N, w), f32)],
        name=name, compiler_params=_cparams("arbitrary"))(xs, dtx, dtr, bm, cm, acol, hin, dy)


def ssd_scan(xs, dtx, dtr, bm, cm, acol, rev, n_ctx, name):
    @jax.custom_vjp
    def op(xs, dtx, dtr, bm, cm, acol):
        return _ssd_fwd_call(xs, dtx, dtr, bm, cm, acol, rev, n_ctx, name)[0]

    def fwd(xs, dtx, dtr, bm, cm, acol):
        y, hin = _ssd_fwd_call(xs, dtx, dtr, bm, cm, acol, rev, n_ctx, name)
        return y, (xs, dtx, dtr, bm, cm, acol, hin)

    def bwd(res, dy):
        return tuple(_ssd_bwd_call(*res, dy, rev, n_ctx, name + "_bwd"))

    op.defvjp(fwd, bwd)
    return op(xs, dtx, dtr, bm, cm, acol)


def _deinterleave(w, n_heads):
    lead = w.shape[:-1]
    return w.reshape(*lead, n_heads, HEAD_DIM // 2, 2).swapaxes(-1, -2).reshape(*lead, n_heads * HEAD_DIM)


def _interleave(w, n_heads):
    lead = w.shape[:-1]
    return w.reshape(*lead, n_heads, 2, HEAD_DIM // 2).swapaxes(-1, -2).reshape(*lead, n_heads * HEAD_DIM)


def _in_layout(d):
    sizes = [('a_q', Q_W, N_HEADS), ('a_k', KV_W, N_KV), ('a_v', KV_W, 0), ('b_z', SSM_INNER, 0),
             ('b_xbc', SSM_INNER + 2 * SSM_BC, 0), ('b_dt', DT_W, 0), ('c_q', Q_W, N_HEADS), ('c_k', KV_W, N_KV),
             ('c_v', KV_W, 0), ('g_a', d, 0), ('g_b', d, 0), ('g_c', d, 0)]
    out, start = [], 0
    for name, n, heads in sizes:
        out.append((name, start, n, heads, DT_PAD if name == 'b_dt' else n))
        start += n
    return out


@jax.custom_vjp
def _w_in_padded(w):
    d = w.shape[0]
    parts = []
    for _, s, n, heads, wp in _in_layout(d):
        p = w[:, s:s + n]
        if heads:
            p = _deinterleave(p, heads)
        if wp > n:
            p = jnp.concatenate([p, jnp.zeros((d, wp - n), w.dtype)], axis=1)
        parts.append(p)
    total = sum(p.shape[1] for p in parts)
    pad = -total % 512
    if pad:
        parts.append(jnp.zeros((d, pad), w.dtype))
    return jnp.concatenate(parts, axis=1)


def _w_in_unpadded(gp):
    d = gp.shape[0]
    parts, start = [], 0
    for _, _, n, heads, wp in _in_layout(d):
        p = gp[:, start:start + n]
        if heads:
            p = _interleave(p, heads)
        parts.append(p)
        start += wp
    return jnp.concatenate(parts, axis=1)


_w_in_padded.defvjp(lambda w: (_w_in_padded(w), None), lambda _, g: (_w_in_unpadded(g),))


def _rope_tables(n_ctx, n_lat):
    rows = n_lat // GRID_W
    t_row = jnp.repeat(jnp.arange(rows), GRID_W).astype(f32)
    t_col = jnp.tile(jnp.arange(GRID_W), rows).astype(f32)
    n = HEAD_DIM // 4
    inv = ROPE_BASE ** (-jnp.arange(n, dtype=f32) / n)
    ang = jnp.concatenate([t_row[:, None] * inv, t_col[:, None] * inv], axis=-1)
    cos = jnp.concatenate([jnp.ones((n_ctx, HEAD_DIM // 2), f32), jnp.cos(ang)], axis=0)
    sin = jnp.concatenate([jnp.zeros((n_ctx, HEAD_DIM // 2), f32), jnp.sin(ang)], axis=0)
    return jnp.concatenate([cos, cos], axis=1), jnp.concatenate([-sin, sin], axis=1)


def _heads_major(a, n_heads):
    return a.reshape(a.shape[0], n_heads, HEAD_DIM).transpose(1, 0, 2)


def _heads_minor(a):
    return a.transpose(1, 0, 2).reshape(a.shape[1], a.shape[0] * HEAD_DIM)


def _layer(xall, w, s, cm, tabs, n_ctx, li):
    t, d = xall.shape
    ncb = n_ctx // ROW_TILE
    nm = f"l{li}_"
    ctq, stq, ctk, stk = tabs
    mod = [(cm[0:1, i * d:(i + 1) * d], cm[1:2, i * d:(i + 1) * d]) for i in range(6)]

    def pick(blk, pair_c, pair_l):
        return jnp.where(blk < ncb, pair_c, pair_l)

    def norm_mod(blk, x, g, sh_c, sh_l, sc_c, sc_l):
        return (_rms(x, g) * (1.0 + pick(blk, sc_c, sc_l)) + pick(blk, sh_c, sh_l),)

    (h,) = rowwise(norm_mod, [xall], [], [s['norm1'][None], *mod[0], *mod[1]], [d], [f32], nm + "norm1")
    u = mm(h, _w_in_padded(w['w_in']), nm + "in")
    widths = [wp for *_, wp in _in_layout(d)]
    widths.append(u.shape[1] - sum(widths))
    a_q, a_k, a_v, b_z, b_xbc, b_dt, c_q, c_k, c_v, g_a, g_b, g_c, _ = split_cols(u, widths)

    def rope(blk, q, k, ct_q, st_q, ct_k, st_k):
        return q * ct_q + _rot_half(q) * st_q, k * ct_k + _rot_half(k) * st_k

    def norm_rope(blk, q, k, ct_q, st_q, ct_k, st_k, gq, gk):
        return rope(blk, _head_rms(q, gq), _head_rms(k, gk), ct_q, st_q, ct_k, st_k)

    qa, ka = rowwise(rope, [a_q, a_k], [ctq, stq, ctk, stk], [], [Q_W, KV_W], [bf16, bf16], nm + "ropeA")
    gq = jnp.tile(_deinterleave(s['c_q_norm'], 1), N_HEADS)[None]
    gk = jnp.tile(_deinterleave(s['c_k_norm'], 1), N_KV)[None]
    qc, kc = rowwise(norm_rope, [c_q, c_k], [ctq, stq, ctk, stk], [gq, gk], [Q_W, KV_W], [bf16, bf16], nm + "ropeC")
    ya = _heads_minor(attention(_heads_major(qa, N_HEADS), _heads_major(ka, N_KV), _heads_major(a_v.astype(bf16), N_KV),
                                s['a_sink'], True, nm + "attnA"))
    yc = _heads_minor(attention(_heads_major(qc, N_HEADS), _heads_major(kc, N_KV), _heads_major(c_v.astype(bf16), N_KV),
                                jnp.zeros((N_HEADS,), f32), False, nm + "attnC"))

    cw, cb = s['ssm_conv_w'], s['ssm_conv_b']
    conv_silu = lambda uu, w0, w1, w2, b: _silu(_dwconv(uu, w0, w1, w2, b, n_ctx))
    xbc = colwise(conv_silu, [b_xbc], [cw[0:1], cw[1:2], cw[2:3], cb[None]], f32, nm + "ssmconv")
    xs, bm, cmat = split_cols(xbc, [SSM_INNER, SSM_BC, SSM_BC])
    bias = jnp.concatenate([s['ssm_dt_bias'].reshape(1, DT_W), jnp.zeros((1, DT_PAD - DT_W), f32)], axis=1)

    def softplus(blk, r, b):
        z = r + b
        return (jnp.maximum(z, 0.0) + jnp.log(1.0 + jnp.exp(-jnp.abs(z))),)

    (dt_all,) = rowwise(softplus, [b_dt], [], [bias], [DT_PAD], [f32], nm + "dt")
    a_coef = -jnp.exp(s['ssm_A_log'])
    ys_dir = []
    for di, rev in enumerate((False, True)):
        dt = dt_all[:, di * SSM_HEADS:(di + 1) * SSM_HEADS]
        ys_dir.append(ssd_scan(xs, jnp.repeat(dt, SSM_P, axis=1), dt.T, bm, cmat, a_coef[di][:, None], rev, n_ctx,
                               nm + ("ssd_r" if rev else "ssd_f")))

    def ssm_out(blk, yf, yb, x, z, dskip, g):
        return (_rms((yf + yb + x * dskip) * _silu(z), g),)

    (ysn,) = rowwise(ssm_out, [ys_dir[0], ys_dir[1], xs, b_z], [], [jnp.repeat(s['ssm_D'], SSM_P)[None], s['ssm_norm'][None]],
                     [SSM_INNER], [f32], nm + "ssmout")

    pa, pb, pc = mm(ya, w['w_oa'], nm + "oa"), mm(ysn, w['w_ob'], nm + "ob"), mm(yc, w['w_oc'], nm + "oc")

    def merge(blk, ga, gb, gc, a, b, c):
        return (_sigmoid(ga) * a + _sigmoid(gb) * b + _sigmoid(gc) * c,)

    (mrg,) = rowwise(merge, [g_a, g_b, g_c, pa, pb, pc], [], [], [d], [f32], nm + "merge")
    o = mm(mrg, w['w_out'], nm + "out")

    def resid_norm_mod(blk, x, oo, g1_c, g1_l, g, sh_c, sh_l, sc_c, sc_l):
        x1 = x + pick(blk, g1_c, g1_l) * oo
        return x1, _rms(x1, g) * (1.0 + pick(blk, sc_c, sc_l)) + pick(blk, sh_c, sh_l)

    x1, h2 = rowwise(resid_norm_mod, [xall, o], [], [*mod[2], s['norm2'][None], *mod[3], *mod[4]], [d, d], [f32, f32], nm + "norm2")
    up, gt = mm(h2, w['ffn_w_up'], nm + "up"), mm(h2, w['ffn_w_gate'], nm + "gate")
    fw, fb = s['ffn_conv_w'], s['ffn_conv_b']
    ffn_act = lambda g_, u_, w0, w1, w2, b: _silu(_dwconv(g_, w0, w1, w2, b, n_ctx)) * u_
    act = colwise(ffn_act, [gt, up], [fw[0:1], fw[1:2], fw[2:3], fb[None]], f32, nm + "ffnact")
    f = mm(act, w['ffn_w_down'], nm + "down")

    def resid(blk, x, ff, g2_c, g2_l):
        return (x + pick(blk, g2_c, g2_l) * ff,)

    (x2,) = rowwise(resid, [x1, f], [], [*mod[5]], [d], [f32], nm + "resid")
    return x2


def _loss_fn(big, small, x, ctx, c, target, n_ctx):
    n_lat, d = x.shape
    depth = len(big)
    xall = jnp.concatenate([ctx, x], axis=0)
    ct, st = _rope_tables(n_ctx, n_lat)
    tabs = (jnp.tile(ct, (1, N_HEADS)) * HEAD_DIM ** -0.5, jnp.tile(st, (1, N_HEADS)) * HEAD_DIM ** -0.5,
            jnp.tile(ct, (1, N_KV)), jnp.tile(st, (1, N_KV)))
    srows = jnp.concatenate([_silu(small['c_ctx'])[None], _silu(c), jnp.zeros((14, d), f32)], axis=0)
    for li in range(depth):
        cm = mm(srows, big[li]['w_mod'], f"l{li}_mod")[0:2] + small['b_mod'][li][None]
        sl = {k: v[li] for k, v in small.items() if k not in ('c_ctx', 'final_norm')}
        xall = _layer(xall, big[li], sl, cm, tabs, n_ctx, li)
    ncb = n_ctx // ROW_TILE
    tgt = jnp.concatenate([jnp.zeros((n_ctx, d), f32), target], axis=0)

    def loss_rows(blk, xx, tg, g):
        e = _rms(xx, g) - tg
        return (jnp.where(blk < ncb, 0.0, 0.5) * jnp.mean(e * e, axis=-1, keepdims=True),)

    (rows,) = rowwise(loss_rows, [xall], [tgt], [small['final_norm'][None]], [1], [f32], "loss")
    return jnp.sum(rows)


def _hbm_call(body, ins, out_shapes, n_sems, name):
    any_spec = pl.BlockSpec(memory_space=pl.ANY)
    return pl.pallas_call(
        body, out_shape=out_shapes, in_specs=[any_spec] * len(ins), out_specs=[any_spec] * len(out_shapes),
        scratch_shapes=[pltpu.SemaphoreType.DMA((n_sems,)), pltpu.SemaphoreType.DMA((n_sems,)), pltpu.SemaphoreType.DMA((len(ins),))],
        name=name)(*ins)


def all_gather(shards, name):
    n = len(shards)

    def body(*refs):
        x_refs, out_refs, (send_sems, recv_sems, local_sems) = refs[:n], refs[n:2 * n], refs[2 * n:]
        x, y, c = lax.axis_index("x"), lax.axis_index("y"), lax.axis_index("c")
        me, sibling = (x, y, c), (x, y, 1 - c)
        chips = [(1 - x, y), (x, 1 - y), (1 - x, 1 - y)]

        def copy(a, k, block, to, src=None):
            px, py, pc = block
            slot = out_refs[a].at[4 * px + 2 * py + pc]
            return pltpu.make_async_remote_copy(
                src_ref=slot if src is None else src, dst_ref=slot,
                send_sem=send_sems.at[7 * a + k], recv_sem=recv_sems.at[7 * a + k], device_id=to, device_id_type=MESH)

        mine = [pltpu.make_async_copy(x_refs[a], out_refs[a].at[4 * x + 2 * y + c], local_sems.at[a]) for a in range(n)]
        first = []
        for a in range(n):
            mine[a].start()
            first += [copy(a, 1 + j, me, (*chip, c), src=x_refs[a]) for j, chip in enumerate(chips)]
            first.append(copy(a, 0, me, sibling, src=x_refs[a]))
        for cp in first:
            cp.start()
        passed = []
        for a in range(n):
            for j, chip in enumerate(chips):
                copy(a, 1 + j, (*chip, c), me).wait_recv()
                passed.append(copy(a, 4 + j, (*chip, c), sibling))
                passed[-1].start()
        for a in range(n):
            copy(a, 0, sibling, me).wait_recv()
            for j, chip in enumerate(chips):
                copy(a, 4 + j, (*chip, 1 - c), me).wait_recv()
        for cp in first + passed:
            cp.wait_send()
        for cp in mine:
            cp.wait()

    return _hbm_call(body, shards, [jax.ShapeDtypeStruct((8,) + s.shape, s.dtype) for s in shards], 7 * n, name)


def rs_to_sibling(gs):
    n = len(gs)

    def body(*refs):
        g_refs, out_refs, (send_sems, recv_sems, _) = refs[:n], refs[n:2 * n], refs[2 * n:]
        x, y, c = lax.axis_index("x"), lax.axis_index("y"), lax.axis_index("c")
        copies = [pltpu.make_async_remote_copy(
            src_ref=g_refs[a].at[2 * k + (1 - c)], dst_ref=out_refs[a].at[k], send_sem=send_sems.at[4 * a + k],
            recv_sem=recv_sems.at[4 * a + k], device_id=(x, y, 1 - c), device_id_type=MESH) for a in range(n) for k in range(4)]
        for cp in copies:
            cp.start()
        for cp in copies:
            cp.wait()

    return _hbm_call(body, gs, [jax.ShapeDtypeStruct((4,) + g.shape[1:], g.dtype) for g in gs], 4 * n, "rs_sibling")


def rs_to_chips(ss):
    n = len(ss)
    flips = [(1, 0), (0, 1), (1, 1)]

    def body(*refs):
        s_refs, out_refs, (send_sems, recv_sems, _) = refs[:n], refs[n:2 * n], refs[2 * n:]
        x, y, c = lax.axis_index("x"), lax.axis_index("y"), lax.axis_index("c")
        copies = []
        for a in range(n):
            for k, (fx, fy) in enumerate(flips):
                px, py = (1 - x) if fx else x, (1 - y) if fy else y
                copies.append(pltpu.make_async_remote_copy(
                    src_ref=s_refs[a].at[2 * px + py], dst_ref=out_refs[a].at[k], send_sem=send_sems.at[3 * a + k],
                    recv_sem=recv_sems.at[3 * a + k], device_id=(px, py, c), device_id_type=MESH))
        for cp in copies:
            cp.start()
        for cp in copies:
            cp.wait()

    return _hbm_call(body, ss, [jax.ShapeDtypeStruct((3,) + s.shape[1:], s.dtype) for s in ss], 3 * n, "rs_chips")


def _flat_tile(rows, cols):
    return _row_tile(rows, 4 * 4 * cols)


def pair_sum(g, r1, my_c, name):
    _, rows, cols = g.shape
    tm = _flat_tile(rows, cols)

    def body(c_ref, g_ref, r_ref, o_ref):
        o_ref[...] = (g_ref[...].astype(f32) + r_ref[...].astype(f32)).astype(o_ref.dtype)

    return pl.pallas_call(
        body, grid_spec=pltpu.PrefetchScalarGridSpec(
            num_scalar_prefetch=1, grid=(4, rows // tm),
            in_specs=[pl.BlockSpec((1, tm, cols), lambda k, i, c: (2 * k + c[0], i, 0)),
                      pl.BlockSpec((1, tm, cols), lambda k, i, c: (k, i, 0))],
            out_specs=pl.BlockSpec((1, tm, cols), lambda k, i, c: (k, i, 0))),
        out_shape=jax.ShapeDtypeStruct((4, rows, cols), g.dtype), name=name,
        compiler_params=_cparams("parallel", "parallel"))(my_c, g, r1)


def _adam_math(w, g, m, v):
    m2 = ADAM_B1 * m + (1.0 - ADAM_B1) * g
    v2 = ADAM_B2 * v + (1.0 - ADAM_B2) * (g * g)
    m_hat = m2 / (1.0 - ADAM_B1 ** ADAM_STEP)
    v_hat = v2 / (1.0 - ADAM_B2 ** ADAM_STEP)
    return -ADAM_LR * (m_hat / (jnp.sqrt(v_hat) + ADAM_EPS) + ADAM_WD * w), m2, v2


def sum_adam(parts, w, m, v, name):
    rows, cols = w.shape
    tm = _flat_tile(rows, cols)
    flat = []
    scalars = [p[2] for p in parts if p[2] is not None]
    assert len(scalars) <= 1
    for arr, static_rows, dyn in parts:
        if dyn is not None:
            flat.append((arr, lambda i, s: (s[0], i, 0)))
        else:
            for k in static_rows:
                flat.append((arr, functools.partial(lambda i, s, k: (k, i, 0), k=k)))
    na = len(flat)

    def body(s_ref, *refs):
        g = refs[0][0].astype(f32)
        for r in refs[1:na]:
            g = g + r[0].astype(f32)
        w_ref, m_ref, v_ref = refs[na:na + 3]
        g_out, d_out, m_out, v_out = refs[na + 3:]
        d, m2, v2 = _adam_math(w_ref[...], g, m_ref[...], v_ref[...])
        g_out[...] = g
        d_out[...] = d
        m_out[...] = m2
        v_out[...] = v2

    blk = pl.BlockSpec((tm, cols), lambda i, s: (i, 0))
    scalar = scalars[0] if scalars else jnp.zeros((1,), jnp.int32)
    return pl.pallas_call(
        body, grid_spec=pltpu.PrefetchScalarGridSpec(
            num_scalar_prefetch=1, grid=(rows // tm,),
            in_specs=[pl.BlockSpec((1, tm, cols), im) for _, im in flat] + [blk, blk, blk],
            out_specs=[blk, blk, blk, blk]),
        out_shape=[jax.ShapeDtypeStruct((rows, cols), f32)] * 4, name=name,
        compiler_params=_cparams("parallel"))(scalar, *[a for a, _ in flat], w, m, v)


FLAT_COLS = 1024


def _to_flat(vec):
    n = vec.shape[0]
    total = -(-n // (8 * FLAT_COLS)) * 8 * FLAT_COLS
    return jnp.concatenate([vec, jnp.zeros((total - n,), vec.dtype)]).reshape(-1, FLAT_COLS)


def _pack(tree, names):
    return jnp.concatenate([tree[n].reshape(-1) for n in names])


def _unpack(vec, like, names):
    out, off = {}, 0
    for n in names:
        size = like[n].size
        out[n] = vec[off:off + size].reshape(like[n].shape)
        off += size
    return out


def kernel(x, c, ctx, c_ctx, w_mod, b_mod, norm1, norm2, w_in, a_sink, ssm_conv_w, ssm_conv_b, ssm_A_log, ssm_dt_bias, ssm_D, ssm_norm, c_q_norm, c_k_norm, w_oa, w_ob, w_oc, w_out, ffn_w_up, ffn_w_gate, ffn_conv_w, ffn_conv_b, ffn_w_down, final_norm, loss_target, m_c_ctx, m_w_mod, m_b_mod, m_norm1, m_norm2, m_w_in, m_a_sink, m_ssm_conv_w, m_ssm_conv_b, m_ssm_A_log, m_ssm_dt_bias, m_ssm_D, m_ssm_norm, m_c_q_norm, m_c_k_norm, m_w_oa, m_w_ob, m_w_oc, m_w_out, m_ffn_w_up, m_ffn_w_gate, m_ffn_conv_w, m_ffn_conv_b, m_ffn_w_down, m_final_norm, v_c_ctx, v_w_mod, v_b_mod, v_norm1, v_norm2, v_w_in, v_a_sink, v_ssm_conv_w, v_ssm_conv_b, v_ssm_A_log, v_ssm_dt_bias, v_ssm_D, v_ssm_norm, v_c_q_norm, v_c_k_norm, v_w_oa, v_w_ob, v_w_oc, v_w_out, v_ffn_w_up, v_ffn_w_gate, v_ffn_conv_w, v_ffn_conv_b, v_ffn_w_down, v_final_norm):
    args = (x, c, ctx, c_ctx, w_mod, b_mod, norm1, norm2, w_in, a_sink, ssm_conv_w, ssm_conv_b, ssm_A_log, ssm_dt_bias, ssm_D, ssm_norm, c_q_norm, c_k_norm, w_oa, w_ob, w_oc, w_out, ffn_w_up, ffn_w_gate, ffn_conv_w, ffn_conv_b, ffn_w_down, final_norm, loss_target)
    moms = (m_c_ctx, m_w_mod, m_b_mod, m_norm1, m_norm2, m_w_in, m_a_sink, m_ssm_conv_w, m_ssm_conv_b, m_ssm_A_log, m_ssm_dt_bias, m_ssm_D, m_ssm_norm, m_c_q_norm, m_c_k_norm, m_w_oa, m_w_ob, m_w_oc, m_w_out, m_ffn_w_up, m_ffn_w_gate, m_ffn_conv_w, m_ffn_conv_b, m_ffn_w_down, m_final_norm)
    vars_ = (v_c_ctx, v_w_mod, v_b_mod, v_norm1, v_norm2, v_w_in, v_a_sink, v_ssm_conv_w, v_ssm_conv_b, v_ssm_A_log, v_ssm_dt_bias, v_ssm_D, v_ssm_norm, v_c_q_norm, v_c_k_norm, v_w_oa, v_w_ob, v_w_oc, v_w_out, v_ffn_w_up, v_ffn_w_gate, v_ffn_conv_w, v_ffn_conv_b, v_ffn_w_down, v_final_norm)
    p = dict(zip(IN_NAMES, args))
    mom = dict(zip(WEIGHTS, moms))
    var = dict(zip(WEIGHTS, vars_))
    depth = w_in.shape[0]
    n_ctx = ctx.shape[1]
    xi, yi, ci = lax.axis_index("x"), lax.axis_index("y"), lax.axis_index("c")
    dev = 4 * xi + 2 * yi + ci
    big_names = list(BIG)

    def as2d(a):
        return a.reshape(-1, a.shape[-1])

    g_big = all_gather([as2d(p[n]).astype(bf16) for n in big_names], "gather_big")
    g_conv = all_gather([_to_flat(_pack(p, CONV_W))], "gather_conv")[0].reshape(8, -1)

    def full_big(n, gathered):
        shp = p[n].shape
        seg = gathered.reshape(8, *shp)
        if BIG[n] == 1:
            return jnp.concatenate([seg[j] for j in range(8)], axis=-1)
        return jnp.concatenate([seg[j] for j in range(8)], axis=1)

    big_full = {n: full_big(n, g) for n, g in zip(big_names, g_big)}
    conv_full, off = {}, 0
    for n in CONV_W:
        shp = p[n].shape
        seg = g_conv[:, off:off + p[n].size].reshape(8, *shp)
        conv_full[n] = jnp.moveaxis(seg, 0, -2).reshape(*shp[:-1], 8 * shp[-1])
        off += p[n].size
    big_layers = [{n: big_full[n][li] for n in big_names} for li in range(depth)]
    small = {n: p[n] for n in REPL}
    small.update(conv_full)

    loss, (g_layers, g_small, g_x) = jax.value_and_grad(_loss_fn, argnums=(0, 1, 2))(
        big_layers, small, x[0], ctx[0], c, loss_target[0], n_ctx)
    loss = lax.psum(loss, AXES)

    def send_rows(n):
        shp = p[n].shape
        b = shp[-1] if BIG[n] == 1 else shp[1]
        cut = (lambda g, j: g[:, b * j:b * (j + 1)]) if BIG[n] == 1 else (lambda g, j: g[b * j:b * (j + 1), :])
        return jnp.stack([jnp.concatenate([cut(g_layers[li][n], j) for li in range(depth)], axis=0) for j in range(8)])

    send = [send_rows(n) for n in big_names]
    from_sibling = rs_to_sibling(send)
    my_c = ci.reshape(1).astype(jnp.int32)
    side_sum = [pair_sum(s, r, my_c, "rs_pair_sum_" + n) for n, s, r in zip(big_names, send, from_sibling)]
    from_chips = rs_to_chips(side_sum)
    chip = (2 * xi + yi).reshape(1).astype(jnp.int32)
    big_out = [{}, {}, {}, {}]
    for n, s, r in zip(big_names, side_sum, from_chips):
        outs = sum_adam([(s, None, chip), (r, (0, 1, 2), None)], as2d(p[n]), as2d(mom[n]), as2d(var[n]), "adam_" + n)
        for k in range(4):
            big_out[k][n] = outs[k].reshape(p[n].shape)

    sm_names = REPL + list(CONV_W)
    g_vec = _to_flat(_pack(g_small, sm_names))
    gathered = all_gather([g_vec], "gather_small_grads")[0]
    n_repl = sum(p[n].size for n in REPL)

    def repl_flat(tree):
        return _to_flat(jnp.concatenate([_pack(tree, REPL), jnp.zeros((g_vec.size - n_repl,), f32)]))

    outs_small = sum_adam([(gathered, tuple(range(8)), None)], repl_flat(p), repl_flat(mom), repl_flat(var), "adam_small")
    g_sum = outs_small[0].reshape(-1)
    small_out = [_unpack(o.reshape(-1), p, REPL) for o in outs_small]
    conv_g_full = _unpack(g_sum[n_repl:], conv_full, CONV_W)
    conv_g = {n: lax.dynamic_slice_in_dim(conv_g_full[n], dev * p[n].shape[-1], p[n].shape[-1], axis=2) for n in CONV_W}
    conv_gv = _to_flat(_pack(conv_g, CONV_W))
    outs_conv = sum_adam([(conv_gv[None], (0,), None)], _to_flat(_pack(p, CONV_W)), _to_flat(_pack(mom, CONV_W)),
                         _to_flat(_pack(var, CONV_W)), "adam_conv")
    conv_out = [_unpack(o.reshape(-1), p, CONV_W) for o in outs_conv]

    res = []
    for k in range(4):
        tree = {**big_out[k], **small_out[k], **conv_out[k]}
        res.append([tree[n] for n in WEIGHTS])
    return (loss, g_x[None], *res[0], *res[1], *res[2], *res[3])
```

```python
import functools

import jax
import jax.numpy as jnp
from jax import lax
from jax.experimental import pallas as pl
from jax.experimental.pallas import tpu as pltpu

f32 = jnp.float32
bf16 = jnp.bfloat16
MESH = pl.DeviceIdType.MESH
AXES = ("x", "y", "c")

GRID_W = 64
HEAD_DIM = 64
ROPE_BASE = 10000.0
EPS = 1e-6
WINDOW = 128
N_HEADS = 8
N_KV = 2
SSM_HEADS = 16
SSM_P = 64
SSM_G = 2
SSM_N = 128
SSM_INNER = SSM_HEADS * SSM_P
SSM_BC = SSM_G * SSM_N
SSM_Q = 128
Q_W = N_HEADS * HEAD_DIM
KV_W = N_KV * HEAD_DIM
DT_W = 2 * SSM_HEADS
DT_PAD = 128
ADAM_LR, ADAM_B1, ADAM_B2, ADAM_EPS, ADAM_WD, ADAM_STEP = 0.001, 0.9, 0.999, 1e-08, 0.01, 10

LANES = 128
ROW_TILE = 256
VMEM_BLOCK_BUDGET = 6 * 1024 * 1024
ATTN_SLAB = 128
ATTN_PART = 1024
MM_ROW_CAP = 1088
MM_TILE_CAP = 1536
NEG = -1e30

IN_NAMES = ['x', 'c', 'ctx', 'c_ctx', 'w_mod', 'b_mod', 'norm1', 'norm2', 'w_in', 'a_sink', 'ssm_conv_w', 'ssm_conv_b', 'ssm_A_log', 'ssm_dt_bias', 'ssm_D', 'ssm_norm', 'c_q_norm', 'c_k_norm', 'w_oa', 'w_ob', 'w_oc', 'w_out', 'ffn_w_up', 'ffn_w_gate', 'ffn_conv_w', 'ffn_conv_b', 'ffn_w_down', 'final_norm', 'loss_target']
WEIGHTS = IN_NAMES[3:28]
BIG = {'w_mod': 1, 'w_in': 1, 'w_oa': 1, 'w_ob': 0, 'w_oc': 1, 'w_out': 0, 'ffn_w_up': 1, 'ffn_w_gate': 1, 'ffn_w_down': 0}
CONV_W = ('ssm_conv_w', 'ffn_conv_w')
REPL = [n for n in WEIGHTS if n not in BIG and n not in CONV_W]

NT = (((1,), (1,)), ((), ()))
TN = (((0,), (0,)), ((), ()))
NN = (((1,), (0,)), ((), ()))


def _cparams(*sem):
    return pltpu.CompilerParams(dimension_semantics=sem)


def _div_tile(n, unit, cap):
    for d in range(min(n, int(cap)), 0, -1):
        if n % d == 0 and d % unit == 0:
            return d
    return n


def _row_tile(m, row_bytes):
    return _div_tile(m, 16, max(16, VMEM_BLOCK_BUDGET // row_bytes))


def _mm_call(a, b, mode, out_dtype, name):
    if mode == "nn":
        (m, k), n = a.shape, b.shape[1]
    elif mode == "nt":
        (m, k), n = a.shape, b.shape[0]
    else:
        (k, m), n = a.shape, b.shape[1]
    dims = {"nn": NN, "nt": NT, "tn": TN}[mode]
    ia, ib = a.dtype.itemsize, b.dtype.itemsize
    tm = _div_tile(m, LANES, MM_TILE_CAP) if mode == "tn" else _div_tile(m, 16, MM_ROW_CAP)
    tn = _div_tile(n, LANES, min(MM_TILE_CAP, VMEM_BLOCK_BUDGET // (4 * tm)))
    tk = _div_tile(k, 16 if mode == "tn" else LANES,
                   min(MM_ROW_CAP if mode == "tn" else MM_TILE_CAP, VMEM_BLOCK_BUDGET // (tm * ia), VMEM_BLOCK_BUDGET // (tn * ib)))
    nk = k // tk

    def body(a_ref, b_ref, o_ref, *acc):
        part = lax.dot_general(a_ref[...].astype(bf16), b_ref[...].astype(bf16), dims, preferred_element_type=f32)
        if nk == 1:
            o_ref[...] = part.astype(o_ref.dtype)
            return
        kk = pl.program_id(2)

        @pl.when(kk == 0)
        def _():
            acc[0][...] = part

        @pl.when(kk > 0)
        def _():
            acc[0][...] += part

        @pl.when(kk == nk - 1)
        def _():
            o_ref[...] = acc[0][...].astype(o_ref.dtype)

    a_spec = pl.BlockSpec((tk, tm), lambda i, j, kk: (kk, i)) if mode == "tn" else pl.BlockSpec((tm, tk), lambda i, j, kk: (i, kk))
    b_spec = pl.BlockSpec((tn, tk), lambda i, j, kk: (j, kk)) if mode == "nt" else pl.BlockSpec((tk, tn), lambda i, j, kk: (kk, j))
    return pl.pallas_call(
        body, grid=(m // tm, n // tn, nk), in_specs=[a_spec, b_spec],
        out_specs=pl.BlockSpec((tm, tn), lambda i, j, kk: (i, j)),
        out_shape=jax.ShapeDtypeStruct((m, n), out_dtype),
        scratch_shapes=[pltpu.VMEM((tm, tn), f32)] if nk > 1 else [], name=name,
        compiler_params=_cparams("parallel", "parallel", "arbitrary"))(a, b)


def mm(a, b, name, out_dtype=None):
    @jax.custom_vjp
    def op(a, b):
        return _mm_call(a, b, "nn", out_dtype or bf16, name)

    def fwd(a, b):
        return op(a, b), (a, b)

    def bwd(res, g):
        a, b = res
        return _mm_call(g, b, "nt", a.dtype, name + "_da"), _mm_call(a, g, "tn", b.dtype, name + "_db")

    op.defvjp(fwd, bwd)
    return op(a, b)


def split_cols(u, widths):
    offs = [0]
    for w in widths:
        offs.append(offs[-1] + w)

    @jax.custom_vjp
    def op(u):
        return tuple(u[:, offs[i]:offs[i + 1]] for i in range(len(widths)))

    def fwd(u):
        return op(u), None

    def bwd(_, cts):
        return (jnp.concatenate(cts, axis=1),)

    op.defvjp(fwd, bwd)
    return op(u)


def rowwise(fn, rows, consts, pars, out_widths, out_dtypes, name):
    t = rows[0].shape[0]
    tm = ROW_TILE
    nb = t // tm
    nr, nc, npar = len(rows), len(consts), len(pars)

    def rspec(a):
        return pl.BlockSpec((tm, a.shape[1]), lambda i: (i, 0))

    def pspec(a):
        return pl.BlockSpec(a.shape, lambda i: (0,) * a.ndim)

    def call_fwd(rows, consts, pars):
        def body(*refs):
            blk = pl.program_id(0)
            ins = [r[...].astype(f32) for r in refs[:nr + nc]]
            ps = [r[...] for r in refs[nr + nc:nr + nc + npar]]
            outs = fn(blk, *ins, *ps)
            for o_ref, o in zip(refs[nr + nc + npar:], outs):
                o_ref[...] = o.astype(o_ref.dtype)

        return pl.pallas_call(
            body, grid=(nb,),
            in_specs=[rspec(a) for a in rows + consts] + [pspec(a) for a in pars],
            out_specs=[pl.BlockSpec((tm, w), lambda i: (i, 0)) for w in out_widths],
            out_shape=[jax.ShapeDtypeStruct((t, w), d) for w, d in zip(out_widths, out_dtypes)],
            name=name, compiler_params=_cparams("parallel"))(*rows, *consts, *pars)

    def call_bwd(rows, consts, pars, cts):
        nout = len(cts)

        def body(*refs):
            blk = pl.program_id(0)
            ins = [r[...].astype(f32) for r in refs[:nr]]
            cs = [r[...].astype(f32) for r in refs[nr:nr + nc]]
            ps = [r[...] for r in refs[nr + nc:nr + nc + npar]]
            dys = [r[...].astype(f32) for r in refs[nr + nc + npar:nr + nc + npar + nout]]
            d_refs = refs[nr + nc + npar + nout:]
            _, vjp = jax.vjp(lambda *a: tuple(fn(blk, *a[:nr], *cs, *a[nr:])), *ins, *ps)
            grads = vjp(tuple(dys))
            for d_ref, g in zip(d_refs[:nr], grads[:nr]):
                d_ref[...] = g.astype(d_ref.dtype)
            if npar:
                @pl.when(blk == 0)
                def _():
                    for d_ref in d_refs[nr:]:
                        d_ref[...] = jnp.zeros_like(d_ref)

                for d_ref, g in zip(d_refs[nr:], grads[nr:]):
                    d_ref[...] += g

        return pl.pallas_call(
            body, grid=(nb,),
            in_specs=[rspec(a) for a in rows + consts] + [pspec(a) for a in pars] + [rspec(a) for a in cts],
            out_specs=[rspec(a) for a in rows] + [pspec(a) for a in pars],
            out_shape=[jax.ShapeDtypeStruct(a.shape, a.dtype) for a in rows + pars],
            name=name + "_bwd", compiler_params=_cparams("arbitrary"))(*rows, *consts, *pars, *cts)

    @jax.custom_vjp
    def op(rows, consts, pars):
        return tuple(call_fwd(list(rows), list(consts), list(pars)))

    def fwd(rows, consts, pars):
        return op(rows, consts, pars), (rows, consts, pars)

    def bwd(res, cts):
        rows, consts, pars = res
        g = call_bwd(list(rows), list(consts), list(pars), list(cts))
        return tuple(g[:nr]), tuple(jnp.zeros_like(a) for a in consts), tuple(g[nr:])

    op.defvjp(fwd, bwd)
    return op(tuple(rows), tuple(consts), tuple(pars))


def colwise(fn, cols, pars, out_dtype, name):
    t, w = cols[0].shape
    tc = LANES
    nb = w // tc
    ncol, npar = len(cols), len(pars)

    def cspec(a):
        return pl.BlockSpec((a.shape[0], tc), lambda j: (0, j))

    def call_fwd(cols, pars):
        def body(*refs):
            ins = [r[...].astype(f32) for r in refs[:ncol]]
            ps = [r[...] for r in refs[ncol:ncol + npar]]
            refs[-1][...] = fn(*ins, *ps).astype(refs[-1].dtype)

        return pl.pallas_call(
            body, grid=(nb,), in_specs=[cspec(a) for a in cols + pars], out_specs=cspec(cols[0]),
            out_shape=jax.ShapeDtypeStruct((t, w), out_dtype), name=name, compiler_params=_cparams("parallel"))(*cols, *pars)

    def call_bwd(cols, pars, ct):
        def body(*refs):
            ins = [r[...].astype(f32) for r in refs[:ncol]]
            ps = [r[...] for r in refs[ncol:ncol + npar]]
            dy = refs[ncol + npar][...].astype(f32)
            d_refs = refs[ncol + npar + 1:]
            _, vjp = jax.vjp(fn, *ins, *ps)
            grads = vjp(dy)
            for d_ref, g in zip(d_refs, grads):
                d_ref[...] = g.astype(d_ref.dtype)

        return pl.pallas_call(
            body, grid=(nb,), in_specs=[cspec(a) for a in cols + pars + [ct]],
            out_specs=[cspec(a) for a in cols + pars],
            out_shape=[jax.ShapeDtypeStruct(a.shape, a.dtype) for a in cols + pars],
            name=name + "_bwd", compiler_params=_cparams("parallel"))(*cols, *pars, ct)

    @jax.custom_vjp
    def op(cols, pars):
        return call_fwd(list(cols), list(pars))

    def fwd(cols, pars):
        return op(cols, pars), (cols, pars)

    def bwd(res, ct):
        cols, pars = res
        g = call_bwd(list(cols), list(pars), ct)
        return tuple(g[:ncol]), tuple(g[ncol:])

    op.defvjp(fwd, bwd)
    return op(tuple(cols), tuple(pars))


def _sigmoid(x):
    return 1.0 / (1.0 + jnp.exp(-x))


def _silu(x):
    return x * _sigmoid(x)


def _rms(x, g):
    return x * lax.rsqrt(jnp.mean(x * x, axis=-1, keepdims=True) + EPS) * g


def _shift_rows(u, k, n_ctx):
    @jax.custom_vjp
    def op(u):
        t = u.shape[0]
        row = lax.broadcasted_iota(jnp.int32, u.shape, 0)
        edge = ((row == 0) | (row == n_ctx)) if k == 1 else ((row == n_ctx - 1) | (row == t - 1))
        return jnp.where(edge, 0.0, pltpu.roll(u, k % t, 0))

    op.defvjp(lambda u: (op(u), None), lambda _, g: (_shift_rows(g, -k, n_ctx),))
    return op(u)


def _dwconv(u, w0, w1, w2, b, n_ctx):
    return w0 * _shift_rows(u, 1, n_ctx) + w1 * u + w2 * _shift_rows(u, -1, n_ctx) + b


@jax.custom_vjp
def _rot_half(x):
    w = x.shape[1]
    lane = lax.broadcasted_iota(jnp.int32, x.shape, 1)
    return jnp.where((lane % HEAD_DIM) < HEAD_DIM // 2, pltpu.roll(x, w - HEAD_DIM // 2, 1), pltpu.roll(x, HEAD_DIM // 2, 1))


_rot_half.defvjp(lambda x: (_rot_half(x), None), lambda _, g: (_rot_half(g),))


def _head_rms(x, g):
    w = x.shape[1]
    same = (lax.broadcasted_iota(jnp.int32, (w, w), 0) // HEAD_DIM) == (lax.broadcasted_iota(jnp.int32, (w, w), 1) // HEAD_DIM)
    ms = jnp.dot(x * x, same.astype(f32), precision=lax.Precision.HIGHEST, preferred_element_type=f32) * (1.0 / HEAD_DIM)
    return x * lax.rsqrt(ms + EPS) * g


def _band_ok(i, j, c0, shape, tq, tk):
    kpos = j * tk + lax.broadcasted_iota(jnp.int32, shape, 0)
    qpos = i * tq + (c0 + lax.broadcasted_iota(jnp.int32, shape, 1)) % tq
    return jnp.abs(qpos - kpos) <= WINDOW


def _kv_range(i, nb, window):
    is_ctx = i == 0
    if window:
        return jnp.where(is_ctx, 1, jnp.maximum(i - 1, 1)), jnp.where(is_ctx, 1, jnp.minimum(i + 2, nb))
    return 1, jnp.where(is_ctx, 1, nb)


def _sink_row(sink_ref, g, r, tq):
    return jnp.concatenate([jnp.full((1, tq), sink_ref[g * r + h], f32) for h in range(r)], axis=1)


def _attn_fwd_call(q, k, v, sink, window, name):
    h, t, dh = q.shape
    nkv = k.shape[0]
    r = h // nkv
    tq = tk = ROW_TILE
    nb = t // tq
    rows = r * tq

    nparts = rows // ATTN_PART

    def body(sink_ref, q_ref, k_ref, v_ref, o_ref, lse_ref, m_scr, l_scr, a_scr, acc_scr, *sp_scr):
        s_scr, p_scr = sp_scr[:nparts], sp_scr[nparts:]
        g, i = pl.program_id(0), pl.program_id(1)
        qv = q_ref[...].reshape(rows, dh)
        m_scr[...] = jnp.full_like(m_scr, NEG)
        l_scr[...] = jnp.zeros_like(l_scr)
        acc_scr[...] = jnp.zeros_like(acc_scr)

        def chunk(j, masked):
            start = pl.multiple_of(j * tk, tk)
            kj = k_ref[0, pl.ds(start, tk), :]
            vj = v_ref[0, pl.ds(start, tk), :]
            for hb in range(nparts):
                s_scr[hb][...] = lax.dot_general(kj, qv[hb * ATTN_PART:(hb + 1) * ATTN_PART, :], NT, preferred_element_type=f32)
            for hb in range(nparts):
                hs = slice(hb * ATTN_PART, (hb + 1) * ATTN_PART)
                for cb in range(ATTN_PART // ATTN_SLAB):
                    ls = slice(cb * ATTN_SLAB, (cb + 1) * ATTN_SLAB)
                    c0 = hb * ATTN_PART + cb * ATTN_SLAB
                    cs = slice(c0, c0 + ATTN_SLAB)
                    s = s_scr[hb][:, ls]
                    if masked:
                        s = jnp.where(_band_ok(i, j, c0, s.shape, tq, tk), s, NEG)
                    m = m_scr[:, cs]
                    m2 = jnp.maximum(m, jnp.max(s, axis=0, keepdims=True))
                    p = jnp.exp(s - m2)
                    a = jnp.exp(m - m2)
                    l_scr[:, cs] = a * l_scr[:, cs] + jnp.sum(p, axis=0, keepdims=True)
                    m_scr[:, cs] = m2
                    a_scr[:, cs] = a
                    p_scr[hb][:, ls] = p.astype(bf16)
                acc_scr[:, hs] = a_scr[:, hs] * acc_scr[:, hs] + lax.dot_general(vj, p_scr[hb][...], TN, preferred_element_type=f32)

        chunk(0, False)
        lo, hi = _kv_range(i, nb, window)
        lax.fori_loop(lo, hi, lambda j, c: (chunk(j, window), c)[1], 0)
        m, l, acc = m_scr[...], l_scr[...], acc_scr[...]
        if window:
            sk = _sink_row(sink_ref, g, r, tq)
            m2 = jnp.maximum(m, sk)
            a = jnp.exp(m - m2)
            l = a * l + jnp.exp(sk - m2)
            acc = a * acc
            m = m2
        o_ref[...] = (acc / l).T.reshape(r, tq, dh).astype(o_ref.dtype)
        lse_ref[0] = m + jnp.log(l)

    qspec = pl.BlockSpec((r, tq, dh), lambda g, i: (g, i, 0))
    kspec = pl.BlockSpec((1, t, dh), lambda g, i: (g, 0, 0))
    return pl.pallas_call(
        body, grid=(nkv, nb),
        in_specs=[pl.BlockSpec(memory_space=pltpu.SMEM), qspec, kspec, kspec],
        out_specs=[qspec, pl.BlockSpec((1, 1, rows), lambda g, i: (g * nb + i, 0, 0))],
        out_shape=[jax.ShapeDtypeStruct((h, t, dh), bf16), jax.ShapeDtypeStruct((nkv * nb, 1, rows), f32)],
        scratch_shapes=[pltpu.VMEM((1, rows), f32), pltpu.VMEM((1, rows), f32), pltpu.VMEM((1, rows), f32), pltpu.VMEM((dh, rows), f32)]
        + [pltpu.VMEM((tk, ATTN_PART), f32)] * nparts + [pltpu.VMEM((tk, ATTN_PART), bf16)] * nparts,
        name=name, compiler_params=_cparams("parallel", "parallel"))(sink, q, k, v)


def _attn_bwd_call(q, k, v, sink, o, lse, do, window, name):
    h, t, dh = q.shape
    nkv = k.shape[0]
    r = h // nkv
    tq = tk = ROW_TILE
    nb = t // tq
    rows = r * tq

    def body(sink_ref, q_ref, k_ref, v_ref, o_ref, lse_ref, do_ref, dq_ref, dk_ref, dv_ref, dsink_ref,
             s_scr, dp_scr, p_scr, ds_scr, dq_scr):
        g, i = pl.program_id(0), pl.program_id(1)

        @pl.when(i == 0)
        def _():
            dk_ref[...] = jnp.zeros_like(dk_ref)
            dv_ref[...] = jnp.zeros_like(dv_ref)

        qv = q_ref[...].reshape(rows, dh)
        dov = do_ref[...].reshape(rows, dh)
        lse_t = lse_ref[0]
        delta_t = jnp.sum((dov.astype(f32) * o_ref[...].reshape(rows, dh).astype(f32)).T, axis=0, keepdims=True)
        dq_scr[...] = jnp.zeros_like(dq_scr)

        def chunk(j, masked):
            kv_rows = pl.ds(pl.multiple_of(j * tk, tk), tk)
            kj = k_ref[0, kv_rows, :]
            vj = v_ref[0, kv_rows, :]
            s_scr[...] = lax.dot_general(kj, qv, NT, preferred_element_type=f32)
            dp_scr[...] = lax.dot_general(vj, dov, NT, preferred_element_type=f32)
            for cb in range(rows // ATTN_SLAB):
                cs = slice(cb * ATTN_SLAB, (cb + 1) * ATTN_SLAB)
                s = s_scr[:, cs]
                if masked:
                    s = jnp.where(_band_ok(i, j, cb * ATTN_SLAB, s.shape, tq, tk), s, NEG)
                p = jnp.exp(s - lse_t[:, cs])
                p_scr[:, cs] = p.astype(bf16)
                ds_scr[:, cs] = (p * (dp_scr[:, cs] - delta_t[:, cs])).astype(bf16)
            dv_ref[0, kv_rows, :] += jnp.dot(p_scr[...], dov, preferred_element_type=f32)
            dk_ref[0, kv_rows, :] += jnp.dot(ds_scr[...], qv, preferred_element_type=f32)
            dq_scr[...] += lax.dot_general(kj, ds_scr[...], TN, preferred_element_type=f32)

        chunk(0, False)
        lo, hi = _kv_range(i, nb, window)
        lax.fori_loop(lo, hi, lambda j, c: (chunk(j, window), c)[1], 0)
        dq_ref[...] = dq_scr[...].T.reshape(r, tq, dh).astype(dq_ref.dtype)
        if window:
            dsink_ref[0] = -jnp.exp(_sink_row(sink_ref, g, r, tq) - lse_t) * delta_t
        else:
            dsink_ref[...] = jnp.zeros_like(dsink_ref)

    qspec = pl.BlockSpec((r, tq, dh), lambda g, i: (g, i, 0))
    cspec = pl.BlockSpec((1, 1, rows), lambda g, i: (g * nb + i, 0, 0))
    kspec = pl.BlockSpec((1, t, dh), lambda g, i: (g, 0, 0))
    return pl.pallas_call(
        body, grid=(nkv, nb),
        in_specs=[pl.BlockSpec(memory_space=pltpu.SMEM), qspec, kspec, kspec, qspec, cspec, qspec],
        out_specs=[qspec, kspec, kspec, cspec],
        out_shape=[jax.ShapeDtypeStruct((h, t, dh), bf16), jax.ShapeDtypeStruct(k.shape, f32), jax.ShapeDtypeStruct(v.shape, f32),
                   jax.ShapeDtypeStruct((nkv * nb, 1, rows), f32)],
        scratch_shapes=[pltpu.VMEM((tk, rows), f32), pltpu.VMEM((tk, rows), f32), pltpu.VMEM((tk, rows), bf16),
                        pltpu.VMEM((tk, rows), bf16), pltpu.VMEM((dh, rows), f32)],
        name=name, compiler_params=_cparams("parallel", "arbitrary"))(sink, q, k, v, o, lse, do)


def attention(q, k, v, sink, window, name):
    @jax.custom_vjp
    def op(q, k, v, sink):
        return _attn_fwd_call(q, k, v, sink, window, name)[0]

    def fwd(q, k, v, sink):
        o, lse = _attn_fwd_call(q, k, v, sink, window, name)
        return o, (q, k, v, sink, o, lse)

    def bwd(res, do):
        q, k, v, sink, o, lse = res
        dq, dk, dv, dsink_rows = _attn_bwd_call(q, k, v, sink, o, lse, do, window, name + "_bwd")
        nkv, r = k.shape[0], q.shape[0] // k.shape[0]
        dsink = jnp.sum(dsink_rows.reshape(nkv, -1, r, ROW_TILE), axis=(1, 3)).reshape(nkv * r)
        return dq, dk.astype(k.dtype), dv.astype(v.dtype), dsink

    op.defvjp(fwd, bwd)
    return op(q, k, v, sink)


def _ssd_pair(xs_p, dtx_p, dtr1, dtr2, ac1, ac2, bg, cg, hin_p, rev):
    q = xs_p.shape[0]
    ii = lax.broadcasted_iota(jnp.int32, (q, q), 0)
    jj = lax.broadcasted_iota(jnp.int32, (q, q), 1)
    tri = (ii <= jj) if rev else (ii >= jj)
    lo = lax.broadcasted_iota(jnp.int32, (q, LANES), 1) < SSM_P
    lo_row = lax.broadcasted_iota(jnp.int32, (1, LANES), 1) < SSM_P

    def cums(dtr, ac):
        a = dtr * ac
        c = jnp.sum(jnp.where(tri, jnp.broadcast_to(a, (q, q)), 0.0), axis=1, keepdims=True)
        return c, jnp.sum(a, axis=1, keepdims=True)

    def lmat(c):
        cf = jnp.broadcast_to(c, (q, q))
        return jnp.where(tri, jnp.exp(jnp.minimum(cf - cf.T, 0.0)), 0.0)

    c1, t1 = cums(dtr1, ac1)
    c2, t2 = cums(dtr2, ac2)
    cb = lax.dot_general(cg.astype(bf16), bg.astype(bf16), NT, preferred_element_type=f32)
    m = jnp.concatenate([cb * lmat(c1), cb * lmat(c2)], axis=1)
    xdt = xs_p * dtx_p
    x2 = jnp.concatenate([jnp.where(lo, xdt, 0.0), jnp.where(lo, 0.0, xdt)], axis=0)
    y_diag = jnp.dot(m.astype(bf16), x2.astype(bf16), preferred_element_type=f32)
    csel = jnp.where(lo, jnp.broadcast_to(c1, (q, LANES)), jnp.broadcast_to(c2, (q, LANES)))
    tsel = jnp.where(lo_row, jnp.broadcast_to(t1, (1, LANES)), jnp.broadcast_to(t2, (1, LANES)))
    st = lax.dot_general(bg.astype(bf16), (xdt * jnp.exp(tsel - csel)).astype(bf16), TN, preferred_element_type=f32)
    y_off = jnp.dot(cg.astype(bf16), hin_p.astype(bf16), preferred_element_type=f32) * jnp.exp(csel)
    return y_diag + y_off, hin_p * jnp.exp(tsel) + st


def _ssd_order(s, nc, ncc, rev):
    if not rev:
        return s
    return jnp.where(s < ncc, ncc - 1 - s, nc - 1 - (s - ncc))


def _ssd_pair_slices(j):
    return slice(LANES * j, LANES * (j + 1)), 2 * j, 2 * j + 1, (2 * j) // (SSM_HEADS // SSM_G)


def _ssd_fwd_call(xs, dtx, dtr, bm, cm, acol, rev, n_ctx, name):
    t, w = xs.shape
    q = SSM_Q
    nc, ncc = t // q, n_ctx // q

    def body(xs_ref, dtx_ref, dtr_ref, b_ref, c_ref, a_ref, y_ref, hin_ref, h_scr):
        @pl.when(pl.program_id(0) == 0)
        def _():
            h_scr[...] = jnp.zeros_like(h_scr)

        hin_ref[0] = h_scr[...]
        for j in range(SSM_HEADS // 2):
            sl, h1, h2, g = _ssd_pair_slices(j)
            gs = slice(SSM_N * g, SSM_N * (g + 1))
            y, hout = _ssd_pair(xs_ref[:, sl], dtx_ref[:, sl], dtr_ref[h1:h1 + 1, :], dtr_ref[h2:h2 + 1, :],
                                a_ref[h1:h1 + 1, :], a_ref[h2:h2 + 1, :], b_ref[:, gs], c_ref[:, gs], h_scr[:, sl], rev)
            y_ref[:, sl] = y.astype(y_ref.dtype)
            h_scr[:, sl] = hout

    def at(s):
        return _ssd_order(s, nc, ncc, rev)

    return pl.pallas_call(
        body, grid=(nc,),
        in_specs=[pl.BlockSpec((q, w), lambda s: (at(s), 0)), pl.BlockSpec((q, w), lambda s: (at(s), 0)),
                  pl.BlockSpec((SSM_HEADS, q), lambda s: (0, at(s))),
                  pl.BlockSpec((q, SSM_BC), lambda s: (at(s), 0)), pl.BlockSpec((q, SSM_BC), lambda s: (at(s), 0)),
                  pl.BlockSpec((SSM_HEADS, 1), lambda s: (0, 0))],
        out_specs=[pl.BlockSpec((q, w), lambda s: (at(s), 0)), pl.BlockSpec((1, SSM_N, w), lambda s: (s, 0, 0))],
        out_shape=[jax.ShapeDtypeStruct((t, w), xs.dtype), jax.ShapeDtypeStruct((nc, SSM_N, w), f32)],
        scratch_shapes=[pltpu.VMEM((SSM_N, w), f32)],
        name=name, compiler_params=_cparams("arbitrary"))(xs, dtx, dtr, bm, cm, acol)


def _ssd_bwd_call(xs, dtx, dtr, bm, cm, acol, hin, dy, rev, n_ctx, name):
    t, w = xs.shape
    q = SSM_Q
    nc, ncc = t // q, n_ctx // q

    def body(xs_ref, dtx_ref, dtr_ref, b_ref, c_ref, a_ref, hin_ref, dy_ref,
             dxs_ref, ddtx_ref, ddtr_ref, db_ref, dc_ref, da_ref, dh_scr):
        @pl.when(pl.program_id(0) == 0)
        def _():
            dh_scr[...] = jnp.zeros_like(dh_scr)
            da_ref[...] = jnp.zeros_like(da_ref)

        db = [None] * SSM_G
        dc = [None] * SSM_G
        for j in range(SSM_HEADS // 2):
            sl, h1, h2, g = _ssd_pair_slices(j)
            gs = slice(SSM_N * g, SSM_N * (g + 1))
            _, vjp = jax.vjp(
                functools.partial(_ssd_pair, rev=rev),
                xs_ref[:, sl].astype(f32), dtx_ref[:, sl], dtr_ref[h1:h1 + 1, :], dtr_ref[h2:h2 + 1, :],
                a_ref[h1:h1 + 1, :], a_ref[h2:h2 + 1, :], b_ref[:, gs].astype(f32), c_ref[:, gs].astype(f32), hin_ref[0, :, sl])
            gr = vjp((dy_ref[:, sl].astype(f32), dh_scr[:, sl]))
            dxs_ref[:, sl] = gr[0].astype(dxs_ref.dtype)
            ddtx_ref[:, sl] = gr[1]
            ddtr_ref[h1:h1 + 1, :] = gr[2]
            ddtr_ref[h2:h2 + 1, :] = gr[3]
            da_ref[h1:h1 + 1, :] += gr[4]
            da_ref[h2:h2 + 1, :] += gr[5]
            db[g] = gr[6] if db[g] is None else db[g] + gr[6]
            dc[g] = gr[7] if dc[g] is None else dc[g] + gr[7]
            dh_scr[:, sl] = gr[8]
        for g in range(SSM_G):
            gs = slice(SSM_N * g, SSM_N * (g + 1))
            db_ref[:, gs] = db[g].astype(db_ref.dtype)
            dc_ref[:, gs] = dc[g].astype(dc_ref.dtype)

    def step(s):
        return nc - 1 - s

    def at(s):
        return _ssd_order(step(s), nc, ncc, rev)

    row = lambda wd: pl.BlockSpec((q, wd), lambda s: (at(s), 0))
    dtr_spec = pl.BlockSpec((SSM_HEADS, q), lambda s: (0, at(s)))
    a_spec = pl.BlockSpec((SSM_HEADS, 1), lambda s: (0, 0))
    return pl.pallas_call(
        body, grid=(nc,),
        in_specs=[row(w), row(w), dtr_spec, row(SSM_BC), row(SSM_BC), a_spec,
                  pl.BlockSpec((1, SSM_N, w), lambda s: (step(s), 0, 0)), row(w)],
        out_specs=[row(w), row(w), dtr_spec, row(SSM_BC), row(SSM_BC), a_spec],
        out_shape=[jax.ShapeDtypeStruct((t, w), xs.dtype), jax.ShapeDtypeStruct((t, w), f32), jax.ShapeDtypeStruct(dtr.shape, f32),
                   jax.ShapeDtypeStruct(bm.shape, bm.dtype), jax.ShapeDtypeStruct(cm.shape, cm.dtype), jax.ShapeDtypeStruct(acol.shape, f32)],
        scratch_shapes=[pltpu.VMEM((SSM_N, w), f32)],
        name=name, compiler_params=_cparams("arbitrary"))(xs, dtx, dtr, bm, cm, acol, hin, dy)


def ssd_scan(xs, dtx, dtr, bm, cm, acol, rev, n_ctx, name):
    @jax.custom_vjp
    def op(xs, dtx, dtr, bm, cm, acol):
        return _ssd_fwd_call(xs, dtx, dtr, bm, cm, acol, rev, n_ctx, name)[0]

    def fwd(xs, dtx, dtr, bm, cm, acol):
        y, hin = _ssd_fwd_call(xs, dtx, dtr, bm, cm, acol, rev, n_ctx, name)
        return y, (xs, dtx, dtr, bm, cm, acol, hin)

    def bwd(res, dy):
        return tuple(_ssd_bwd_call(*res, dy, rev, n_ctx, name + "_bwd"))

    op.defvjp(fwd, bwd)
    return op(xs, dtx, dtr, bm, cm, acol)


def _deinterleave(w, n_heads):
    lead = w.shape[:-1]
    return w.reshape(*lead, n_heads, HEAD_DIM // 2, 2).swapaxes(-1, -2).reshape(*lead, n_heads * HEAD_DIM)


def _interleave(w, n_heads):
    lead = w.shape[:-1]
    return w.reshape(*lead, n_heads, 2, HEAD_DIM // 2).swapaxes(-1, -2).reshape(*lead, n_heads * HEAD_DIM)


def _in_layout(d):
    sizes = [('a_q', Q_W, N_HEADS), ('a_k', KV_W, N_KV), ('a_v', KV_W, 0), ('b_z', SSM_INNER, 0),
             ('b_xbc', SSM_INNER + 2 * SSM_BC, 0), ('b_dt', DT_W, 0), ('c_q', Q_W, N_HEADS), ('c_k', KV_W, N_KV),
             ('c_v', KV_W, 0), ('g_a', d, 0), ('g_b', d, 0), ('g_c', d, 0)]
    out, start = [], 0
    for name, n, heads in sizes:
        out.append((name, start, n, heads))
        start += n
    return out


@jax.custom_vjp
def _w_in_split(w):
    d = w.shape[0]
    parts, dt = [], None
    for name, s, n, heads in _in_layout(d):
        p = w[:, s:s + n]
        if heads:
            p = _deinterleave(p, heads)
        if name == 'b_dt':
            dt = jnp.concatenate([p, jnp.zeros((d, DT_PAD - n), w.dtype)], axis=1)
        else:
            parts.append(p)
    return jnp.concatenate(parts, axis=1), dt


def _w_in_join(g_main, g_dt):
    d = g_main.shape[0]
    parts, start = [], 0
    for name, _, n, heads in _in_layout(d):
        if name == 'b_dt':
            parts.append(g_dt[:, :n])
            continue
        p = g_main[:, start:start + n]
        parts.append(_interleave(p, heads) if heads else p)
        start += n
    return jnp.concatenate(parts, axis=1)


_w_in_split.defvjp(lambda w: (_w_in_split(w), None), lambda _, g: (_w_in_join(*g),))


def _rope_tables(n_ctx, n_lat):
    rows = n_lat // GRID_W
    t_row = jnp.repeat(jnp.arange(rows), GRID_W).astype(f32)
    t_col = jnp.tile(jnp.arange(GRID_W), rows).astype(f32)
    n = HEAD_DIM // 4
    inv = ROPE_BASE ** (-jnp.arange(n, dtype=f32) / n)
    ang = jnp.concatenate([t_row[:, None] * inv, t_col[:, None] * inv], axis=-1)
    cos = jnp.concatenate([jnp.ones((n_ctx, HEAD_DIM // 2), f32), jnp.cos(ang)], axis=0)
    sin = jnp.concatenate([jnp.zeros((n_ctx, HEAD_DIM // 2), f32), jnp.sin(ang)], axis=0)
    return jnp.concatenate([cos, cos], axis=1), jnp.concatenate([-sin, sin], axis=1)


def _heads_major(a, n_heads):
    return a.reshape(a.shape[0], n_heads, HEAD_DIM).transpose(1, 0, 2)


def _heads_minor(a):
    return a.transpose(1, 0, 2).reshape(a.shape[1], a.shape[0] * HEAD_DIM)


def _layer(xall, w, s, cm, tabs, n_ctx, li):
    t, d = xall.shape
    ncb = n_ctx // ROW_TILE
    nm = f"l{li}_"
    ctq, stq, ctk, stk = tabs
    mod = [(cm[0:1, i * d:(i + 1) * d], cm[1:2, i * d:(i + 1) * d]) for i in range(6)]

    def pick(blk, pair_c, pair_l):
        return jnp.where(blk < ncb, pair_c, pair_l)

    def norm_mod(blk, x, g, sh_c, sh_l, sc_c, sc_l):
        return (_rms(x, g) * (1.0 + pick(blk, sc_c, sc_l)) + pick(blk, sh_c, sh_l),)

    (h,) = rowwise(norm_mod, [xall], [], [s['norm1'][None], *mod[0], *mod[1]], [d], [bf16], nm + "norm1")
    w_main, w_dt = _w_in_split(w['w_in'])
    u = mm(h, w_main, nm + "in")
    b_dt = mm(h, w_dt, nm + "in_dt", f32)
    a_q, a_k, a_v, b_z, b_xbc, c_q, c_k, c_v, g_a, g_b, g_c = split_cols(u, [n for name, _, n, _ in _in_layout(d) if name != 'b_dt'])

    def rope(blk, q, k, v, ct_q, st_q, ct_k, st_k):
        return q * ct_q + _rot_half(q) * st_q, k * ct_k + _rot_half(k) * st_k, v

    def norm_rope(blk, q, k, v, ct_q, st_q, ct_k, st_k, gq, gk):
        return rope(blk, _head_rms(q, gq), _head_rms(k, gk), v, ct_q, st_q, ct_k, st_k)

    qkv_w, qkv_t = [Q_W, KV_W, KV_W], [bf16, bf16, bf16]
    qa, ka, va = rowwise(rope, [a_q, a_k, a_v], [ctq, stq, ctk, stk], [], qkv_w, qkv_t, nm + "ropeA")
    gq = jnp.tile(_deinterleave(s['c_q_norm'], 1), N_HEADS)[None]
    gk = jnp.tile(_deinterleave(s['c_k_norm'], 1), N_KV)[None]
    qc, kc, vc = rowwise(norm_rope, [c_q, c_k, c_v], [ctq, stq, ctk, stk], [gq, gk], qkv_w, qkv_t, nm + "ropeC")
    ya = _heads_minor(attention(_heads_major(qa, N_HEADS), _heads_major(ka, N_KV), _heads_major(va, N_KV),
                                s['a_sink'], True, nm + "attnA"))
    yc = _heads_minor(attention(_heads_major(qc, N_HEADS), _heads_major(kc, N_KV), _heads_major(vc, N_KV),
                                jnp.zeros((N_HEADS,), f32), False, nm + "attnC"))

    cw, cb = s['ssm_conv_w'], s['ssm_conv_b']
    conv_silu = lambda uu, w0, w1, w2, b: _silu(_dwconv(uu, w0, w1, w2, b, n_ctx))
    xbc = colwise(conv_silu, [b_xbc], [cw[0:1], cw[1:2], cw[2:3], cb[None]], bf16, nm + "ssmconv")
    xs, bm, cmat = split_cols(xbc, [SSM_INNER, SSM_BC, SSM_BC])
    bias = jnp.concatenate([s['ssm_dt_bias'].reshape(1, DT_W), jnp.zeros((1, DT_PAD - DT_W), f32)], axis=1)

    def softplus(blk, r, b):
        z = r + b
        return (jnp.maximum(z, 0.0) + jnp.log(1.0 + jnp.exp(-jnp.abs(z))),)

    (dt_all,) = rowwise(softplus, [b_dt], [], [bias], [DT_PAD], [f32], nm + "dt")
    a_coef = -jnp.exp(s['ssm_A_log'])
    ys_dir = []
    for di, rev in enumerate((False, True)):
        dt = dt_all[:, di * SSM_HEADS:(di + 1) * SSM_HEADS]
        ys_dir.append(ssd_scan(xs, jnp.repeat(dt, SSM_P, axis=1), dt.T, bm, cmat, a_coef[di][:, None], rev, n_ctx,
                               nm + ("ssd_r" if rev else "ssd_f")))

    def ssm_out(blk, yf, yb, x, z, dskip, g):
        return (_rms((yf + yb + x * dskip) * _silu(z), g),)

    (ysn,) = rowwise(ssm_out, [ys_dir[0], ys_dir[1], xs, b_z], [], [jnp.repeat(s['ssm_D'], SSM_P)[None], s['ssm_norm'][None]],
                     [SSM_INNER], [bf16], nm + "ssmout")

    pa, pb, pc = mm(ya, w['w_oa'], nm + "oa"), mm(ysn, w['w_ob'], nm + "ob"), mm(yc, w['w_oc'], nm + "oc")

    def merge(blk, ga, gb, gc, a, b, c):
        return (_sigmoid(ga) * a + _sigmoid(gb) * b + _sigmoid(gc) * c,)

    (mrg,) = rowwise(merge, [g_a, g_b, g_c, pa, pb, pc], [], [], [d], [bf16], nm + "merge")
    o = mm(mrg, w['w_out'], nm + "out")

    def resid_norm_mod(blk, x, oo, g1_c, g1_l, g, sh_c, sh_l, sc_c, sc_l):
        x1 = x + pick(blk, g1_c, g1_l) * oo
        return x1, _rms(x1, g) * (1.0 + pick(blk, sc_c, sc_l)) + pick(blk, sh_c, sh_l)

    x1, h2 = rowwise(resid_norm_mod, [xall, o], [], [*mod[2], s['norm2'][None], *mod[3], *mod[4]], [d, d], [f32, bf16], nm + "norm2")
    up, gt = mm(h2, w['ffn_w_up'], nm + "up"), mm(h2, w['ffn_w_gate'], nm + "gate")
    fw, fb = s['ffn_conv_w'], s['ffn_conv_b']
    ffn_act = lambda g_, u_, w0, w1, w2, b: _silu(_dwconv(g_, w0, w1, w2, b, n_ctx)) * u_
    act = colwise(ffn_act, [gt, up], [fw[0:1], fw[1:2], fw[2:3], fb[None]], bf16, nm + "ffnact")
    f = mm(act, w['ffn_w_down'], nm + "down")

    def resid(blk, x, ff, g2_c, g2_l):
        return (x + pick(blk, g2_c, g2_l) * ff,)

    (x2,) = rowwise(resid, [x1, f], [], [*mod[5]], [d], [f32], nm + "resid")
    return x2


def _loss_fn(big, small, x, ctx, c, target, n_ctx):
    n_lat, d = x.shape
    depth = len(big)
    xall = jnp.concatenate([ctx, x], axis=0)
    ct, st = _rope_tables(n_ctx, n_lat)
    tabs = (jnp.tile(ct, (1, N_HEADS)) * HEAD_DIM ** -0.5, jnp.tile(st, (1, N_HEADS)) * HEAD_DIM ** -0.5,
            jnp.tile(ct, (1, N_KV)), jnp.tile(st, (1, N_KV)))
    srows = jnp.concatenate([_silu(small['c_ctx'])[None], _silu(c), jnp.zeros((14, d), f32)], axis=0)
    for li in range(depth):
        cm = mm(srows, big[li]['w_mod'], f"l{li}_mod", f32)[0:2] + small['b_mod'][li][None]
        sl = {k: v[li] for k, v in small.items() if k not in ('c_ctx', 'final_norm')}
        xall = _layer(xall, big[li], sl, cm, tabs, n_ctx, li)
    ncb = n_ctx // ROW_TILE
    tgt = jnp.concatenate([jnp.zeros((n_ctx, d), f32), target], axis=0)

    def loss_rows(blk, xx, tg, g):
        e = _rms(xx, g) - tg
        return (jnp.where(blk < ncb, 0.0, 0.5) * jnp.mean(e * e, axis=-1, keepdims=True),)

    (rows,) = rowwise(loss_rows, [xall], [tgt], [small['final_norm'][None]], [1], [f32], "loss")
    return jnp.sum(rows)


def _hbm_call(body, ins, out_shapes, n_sems, name):
    any_spec = pl.BlockSpec(memory_space=pl.ANY)
    return pl.pallas_call(
        body, out_shape=out_shapes, in_specs=[any_spec] * len(ins), out_specs=[any_spec] * len(out_shapes),
        scratch_shapes=[pltpu.SemaphoreType.DMA((n_sems,)), pltpu.SemaphoreType.DMA((n_sems,)), pltpu.SemaphoreType.DMA((len(ins),))],
        name=name)(*ins)


def all_gather(shards, name):
    n = len(shards)

    def body(*refs):
        x_refs, out_refs, (send_sems, recv_sems, local_sems) = refs[:n], refs[n:2 * n], refs[2 * n:]
        x, y, c = lax.axis_index("x"), lax.axis_index("y"), lax.axis_index("c")
        me, sibling = (x, y, c), (x, y, 1 - c)
        chips = [(1 - x, y), (x, 1 - y), (1 - x, 1 - y)]

        def copy(a, k, block, to, src=None):
            px, py, pc = block
            slot = out_refs[a].at[4 * px + 2 * py + pc]
            return pltpu.make_async_remote_copy(
                src_ref=slot if src is None else src, dst_ref=slot,
                send_sem=send_sems.at[7 * a + k], recv_sem=recv_sems.at[7 * a + k], device_id=to, device_id_type=MESH)

        mine = [pltpu.make_async_copy(x_refs[a], out_refs[a].at[4 * x + 2 * y + c], local_sems.at[a]) for a in range(n)]
        first = []
        for a in range(n):
            mine[a].start()
            first += [copy(a, 1 + j, me, (*chip, c), src=x_refs[a]) for j, chip in enumerate(chips)]
            first.append(copy(a, 0, me, sibling, src=x_refs[a]))
        for cp in first:
            cp.start()
        passed = []
        for a in range(n):
            for j, chip in enumerate(chips):
                copy(a, 1 + j, (*chip, c), me).wait_recv()
                passed.append(copy(a, 4 + j, (*chip, c), sibling))
                passed[-1].start()
        for a in range(n):
            copy(a, 0, sibling, me).wait_recv()
            for j, chip in enumerate(chips):
                copy(a, 4 + j, (*chip, 1 - c), me).wait_recv()
        for cp in first + passed:
            cp.wait_send()
        for cp in mine:
            cp.wait()

    return _hbm_call(body, shards, [jax.ShapeDtypeStruct((8,) + s.shape, s.dtype) for s in shards], 7 * n, name)


def rs_to_sibling(gs):
    n = len(gs)

    def body(*refs):
        g_refs, out_refs, (send_sems, recv_sems, _) = refs[:n], refs[n:2 * n], refs[2 * n:]
        x, y, c = lax.axis_index("x"), lax.axis_index("y"), lax.axis_index("c")
        copies = [pltpu.make_async_remote_copy(
            src_ref=g_refs[a].at[2 * k + (1 - c)], dst_ref=out_refs[a].at[k], send_sem=send_sems.at[4 * a + k],
            recv_sem=recv_sems.at[4 * a + k], device_id=(x, y, 1 - c), device_id_type=MESH) for a in range(n) for k in range(4)]
        for cp in copies:
            cp.start()
        for cp in copies:
            cp.wait()

    return _hbm_call(body, gs, [jax.ShapeDtypeStruct((4,) + g.shape[1:], g.dtype) for g in gs], 4 * n, "rs_sibling")


def rs_to_chips(ss):
    n = len(ss)
    flips = [(1, 0), (0, 1), (1, 1)]

    def body(*refs):
        s_refs, out_refs, (send_sems, recv_sems, _) = refs[:n], refs[n:2 * n], refs[2 * n:]
        x, y, c = lax.axis_index("x"), lax.axis_index("y"), lax.axis_index("c")
        copies = []
        for a in range(n):
            for k, (fx, fy) in enumerate(flips):
                px, py = (1 - x) if fx else x, (1 - y) if fy else y
                copies.append(pltpu.make_async_remote_copy(
                    src_ref=s_refs[a].at[2 * px + py], dst_ref=out_refs[a].at[k], send_sem=send_sems.at[3 * a + k],
                    recv_sem=recv_sems.at[3 * a + k], device_id=(px, py, c), device_id_type=MESH))
        for cp in copies:
            cp.start()
        for cp in copies:
            cp.wait()

    return _hbm_call(body, ss, [jax.ShapeDtypeStruct((3,) + s.shape[1:], s.dtype) for s in ss], 3 * n, "rs_chips")


def _flat_tile(rows, cols):
    return _row_tile(rows, 4 * 4 * cols)


def pair_sum(g, r1, my_c, name):
    _, rows, cols = g.shape
    tm = _flat_tile(rows, cols)

    def body(c_ref, g_ref, r_ref, o_ref):
        o_ref[...] = (g_ref[...].astype(f32) + r_ref[...].astype(f32)).astype(o_ref.dtype)

    return pl.pallas_call(
        body, grid_spec=pltpu.PrefetchScalarGridSpec(
            num_scalar_prefetch=1, grid=(4, rows // tm),
            in_specs=[pl.BlockSpec((1, tm, cols), lambda k, i, c: (2 * k + c[0], i, 0)),
                      pl.BlockSpec((1, tm, cols), lambda k, i, c: (k, i, 0))],
            out_specs=pl.BlockSpec((1, tm, cols), lambda k, i, c: (k, i, 0))),
        out_shape=jax.ShapeDtypeStruct((4, rows, cols), g.dtype), name=name,
        compiler_params=_cparams("parallel", "parallel"))(my_c, g, r1)


def _adam_math(w, g, m, v):
    m2 = ADAM_B1 * m + (1.0 - ADAM_B1) * g
    v2 = ADAM_B2 * v + (1.0 - ADAM_B2) * (g * g)
    m_hat = m2 / (1.0 - ADAM_B1 ** ADAM_STEP)
    v_hat = v2 / (1.0 - ADAM_B2 ** ADAM_STEP)
    return -ADAM_LR * (m_hat / (jnp.sqrt(v_hat) + ADAM_EPS) + ADAM_WD * w), m2, v2


def sum_adam(parts, w, m, v, name):
    rows, cols = w.shape
    tm = _flat_tile(rows, cols)
    flat = []
    scalars = [p[2] for p in parts if p[2] is not None]
    assert len(scalars) <= 1
    for arr, static_rows, dyn in parts:
        if dyn is not None:
            flat.append((arr, lambda i, s: (s[0], i, 0)))
        else:
            for k in static_rows:
                flat.append((arr, functools.partial(lambda i, s, k: (k, i, 0), k=k)))
    na = len(flat)

    def body(s_ref, *refs):
        g = refs[0][0].astype(f32)
        for r in refs[1:na]:
            g = g + r[0].astype(f32)
        w_ref, m_ref, v_ref = refs[na:na + 3]
        g_out, d_out, m_out, v_out = refs[na + 3:]
        d, m2, v2 = _adam_math(w_ref[...], g, m_ref[...], v_ref[...])
        g_out[...] = g
        d_out[...] = d
        m_out[...] = m2
        v_out[...] = v2

    blk = pl.BlockSpec((tm, cols), lambda i, s: (i, 0))
    scalar = scalars[0] if scalars else jnp.zeros((1,), jnp.int32)
    return pl.pallas_call(
        body, grid_spec=pltpu.PrefetchScalarGridSpec(
            num_scalar_prefetch=1, grid=(rows // tm,),
            in_specs=[pl.BlockSpec((1, tm, cols), im) for _, im in flat] + [blk, blk, blk],
            out_specs=[blk, blk, blk, blk]),
        out_shape=[jax.ShapeDtypeStruct((rows, cols), f32)] * 4, name=name,
        compiler_params=_cparams("parallel"))(scalar, *[a for a, _ in flat], w, m, v)


FLAT_COLS = 1024


def _to_flat(vec):
    n = vec.shape[0]
    total = -(-n // (8 * FLAT_COLS)) * 8 * FLAT_COLS
    return jnp.concatenate([vec, jnp.zeros((total - n,), vec.dtype)]).reshape(-1, FLAT_COLS)


def _pack(tree, names):
    return jnp.concatenate([tree[n].reshape(-1) for n in names])


def _unpack(vec, like, names):
    out, off = {}, 0
    for n in names:
        size = like[n].size
        out[n] = vec[off:off + size].reshape(like[n].shape)
        off += size
    return out


def kernel(x, c, ctx, c_ctx, w_mod, b_mod, norm1, norm2, w_in, a_sink, ssm_conv_w, ssm_conv_b, ssm_A_log, ssm_dt_bias, ssm_D, ssm_norm, c_q_norm, c_k_norm, w_oa, w_ob, w_oc, w_out, ffn_w_up, ffn_w_gate, ffn_conv_w, ffn_conv_b, ffn_w_down, final_norm, loss_target, m_c_ctx, m_w_mod, m_b_mod, m_norm1, m_norm2, m_w_in, m_a_sink, m_ssm_conv_w, m_ssm_conv_b, m_ssm_A_log, m_ssm_dt_bias, m_ssm_D, m_ssm_norm, m_c_q_norm, m_c_k_norm, m_w_oa, m_w_ob, m_w_oc, m_w_out, m_ffn_w_up, m_ffn_w_gate, m_ffn_conv_w, m_ffn_conv_b, m_ffn_w_down, m_final_norm, v_c_ctx, v_w_mod, v_b_mod, v_norm1, v_norm2, v_w_in, v_a_sink, v_ssm_conv_w, v_ssm_conv_b, v_ssm_A_log, v_ssm_dt_bias, v_ssm_D, v_ssm_norm, v_c_q_norm, v_c_k_norm, v_w_oa, v_w_ob, v_w_oc, v_w_out, v_ffn_w_up, v_ffn_w_gate, v_ffn_conv_w, v_ffn_conv_b, v_ffn_w_down, v_final_norm):
    args = (x, c, ctx, c_ctx, w_mod, b_mod, norm1, norm2, w_in, a_sink, ssm_conv_w, ssm_conv_b, ssm_A_log, ssm_dt_bias, ssm_D, ssm_norm, c_q_norm, c_k_norm, w_oa, w_ob, w_oc, w_out, ffn_w_up, ffn_w_gate, ffn_conv_w, ffn_conv_b, ffn_w_down, final_norm, loss_target)
    moms = (m_c_ctx, m_w_mod, m_b_mod, m_norm1, m_norm2, m_w_in, m_a_sink, m_ssm_conv_w, m_ssm_conv_b, m_ssm_A_log, m_ssm_dt_bias, m_ssm_D, m_ssm_norm, m_c_q_norm, m_c_k_norm, m_w_oa, m_w_ob, m_w_oc, m_w_out, m_ffn_w_up, m_ffn_w_gate, m_ffn_conv_w, m_ffn_conv_b, m_ffn_w_down, m_final_norm)
    vars_ = (v_c_ctx, v_w_mod, v_b_mod, v_norm1, v_norm2, v_w_in, v_a_sink, v_ssm_conv_w, v_ssm_conv_b, v_ssm_A_log, v_ssm_dt_bias, v_ssm_D, v_ssm_norm, v_c_q_norm, v_c_k_norm, v_w_oa, v_w_ob, v_w_oc, v_w_out, v_ffn_w_up, v_ffn_w_gate, v_ffn_conv_w, v_ffn_conv_b, v_ffn_w_down, v_final_norm)
    p = dict(zip(IN_NAMES, args))
    mom = dict(zip(WEIGHTS, moms))
    var = dict(zip(WEIGHTS, vars_))
    depth = w_in.shape[0]
    n_ctx = ctx.shape[1]
    xi, yi, ci = lax.axis_index("x"), lax.axis_index("y"), lax.axis_index("c")
    dev = 4 * xi + 2 * yi + ci
    big_names = list(BIG)

    def as2d(a):
        return a.reshape(-1, a.shape[-1])

    g_big = all_gather([as2d(p[n]).astype(bf16) for n in big_names], "gather_big")
    g_conv = all_gather([_to_flat(_pack(p, CONV_W))], "gather_conv")[0].reshape(8, -1)

    def full_big(n, gathered):
        shp = p[n].shape
        seg = gathered.reshape(8, *shp)
        if BIG[n] == 1:
            return jnp.concatenate([seg[j] for j in range(8)], axis=-1)
        return jnp.concatenate([seg[j] for j in range(8)], axis=1)

    big_full = {n: full_big(n, g) for n, g in zip(big_names, g_big)}
    conv_full, off = {}, 0
    for n in CONV_W:
        shp = p[n].shape
        seg = g_conv[:, off:off + p[n].size].reshape(8, *shp)
        conv_full[n] = jnp.moveaxis(seg, 0, -2).reshape(*shp[:-1], 8 * shp[-1])
        off += p[n].size
    big_layers = [{n: big_full[n][li] for n in big_names} for li in range(depth)]
    small = {n: p[n] for n in REPL}
    small.update(conv_full)

    loss, (g_layers, g_small, g_x) = jax.value_and_grad(_loss_fn, argnums=(0, 1, 2))(
        big_layers, small, x[0], ctx[0], c, loss_target[0], n_ctx)
    loss = lax.psum(loss, AXES)

    def send_rows(n):
        shp = p[n].shape
        b = shp[-1] if BIG[n] == 1 else shp[1]
        cut = (lambda g, j: g[:, b * j:b * (j + 1)]) if BIG[n] == 1 else (lambda g, j: g[b * j:b * (j + 1), :])
        return jnp.stack([jnp.concatenate([cut(g_layers[li][n], j) for li in range(depth)], axis=0) for j in range(8)])

    send = [send_rows(n) for n in big_names]
    from_sibling = rs_to_sibling(send)
    my_c = ci.reshape(1).astype(jnp.int32)
    side_sum = [pair_sum(s, r, my_c, "rs_pair_sum_" + n) for n, s, r in zip(big_names, send, from_sibling)]
    from_chips = rs_to_chips(side_sum)
    chip = (2 * xi + yi).reshape(1).astype(jnp.int32)
    big_out = [{}, {}, {}, {}]
    for n, s, r in zip(big_names, side_sum, from_chips):
        outs = sum_adam([(s, None, chip), (r, (0, 1, 2), None)], as2d(p[n]), as2d(mom[n]), as2d(var[n]), "adam_" + n)
        for k in range(4):
            big_out[k][n] = outs[k].reshape(p[n].shape)

    sm_names = REPL + list(CONV_W)
    g_vec = _to_flat(_pack(g_small, sm_names))
    gathered = all_gather([g_vec], "gather_small_grads")[0]
    n_repl = sum(p[n].size for n in REPL)

    def repl_flat(tree):
        return _to_flat(jnp.concatenate([_pack(tree, REPL), jnp.zeros((g_vec.size - n_repl,), f32)]))

    outs_small = sum_adam([(gathered, tuple(range(8)), None)], repl_flat(p), repl_flat(mom), repl_flat(var), "adam_small")
    g_sum = outs_small[0].reshape(-1)
    small_out = [_unpack(o.reshape(-1), p, REPL) for o in outs_small]
    conv_g_full = _unpack(g_sum[n_repl:], conv_full, CONV_W)
    conv_g = {n: lax.dynamic_slice_in_dim(conv_g_full[n], dev * p[n].shape[-1], p[n].shape[-1], axis=2) for n in CONV_W}
    conv_gv = _to_flat(_pack(conv_g, CONV_W))
    outs_conv = sum_adam([(conv_gv[None], (0,), None)], _to_flat(_pack(p, CONV_W)), _to_flat(_pack(mom, CONV_W)),
                         _to_flat(_pack(var, CONV_W)), "adam_conv")
    conv_out = [_unpack(o.reshape(-1), p, CONV_W) for o in outs_conv]

    res = []
    for k in range(4):
        tree = {**big_out[k], **small_out[k], **conv_out[k]}
        res.append([tree[n] for n in WEIGHTS])
    return (loss, g_x[None], *res[0], *res[1], *res[2], *res[3])
```

```python
import functools

import jax
import jax.numpy as jnp
from jax import lax
from jax.experimental import pallas as pl
from jax.experimental.pallas import tpu as pltpu

f32 = jnp.float32
bf16 = jnp.bfloat16
MESH = pl.DeviceIdType.MESH
AXES = ("x", "y", "c")

GRID_W = 64
HEAD_DIM = 64
ROPE_BASE = 10000.0
EPS = 1e-6
WINDOW = 128
N_HEADS = 8
N_KV = 2
SSM_HEADS = 16
SSM_P = 64
SSM_G = 2
SSM_N = 128
SSM_INNER = SSM_HEADS * SSM_P
SSM_BC = SSM_G * SSM_N
SSM_Q = 128
Q_W = N_HEADS * HEAD_DIM
KV_W = N_KV * HEAD_DIM
DT_W = 2 * SSM_HEADS
DT_PAD = 128
ADAM_LR, ADAM_B1, ADAM_B2, ADAM_EPS, ADAM_WD, ADAM_STEP = 0.001, 0.9, 0.999, 1e-08, 0.01, 10

LANES = 128
ROW_TILE = 256
VMEM_BLOCK_BUDGET = 6 * 1024 * 1024
ATTN_SLAB = 128
MM_ROW_CAP = 1088
MM_TILE_CAP = 1536
NEG = -1e30

IN_NAMES = ['x', 'c', 'ctx', 'c_ctx', 'w_mod', 'b_mod', 'norm1', 'norm2', 'w_in', 'a_sink', 'ssm_conv_w', 'ssm_conv_b', 'ssm_A_log', 'ssm_dt_bias', 'ssm_D', 'ssm_norm', 'c_q_norm', 'c_k_norm', 'w_oa', 'w_ob', 'w_oc', 'w_out', 'ffn_w_up', 'ffn_w_gate', 'ffn_conv_w', 'ffn_conv_b', 'ffn_w_down', 'final_norm', 'loss_target']
WEIGHTS = IN_NAMES[3:28]
BIG = {'w_mod': 1, 'w_in': 1, 'w_oa': 1, 'w_ob': 0, 'w_oc': 1, 'w_out': 0, 'ffn_w_up': 1, 'ffn_w_gate': 1, 'ffn_w_down': 0}
CONV_W = ('ssm_conv_w', 'ffn_conv_w')
REPL = [n for n in WEIGHTS if n not in BIG and n not in CONV_W]

NT = (((1,), (1,)), ((), ()))
TN = (((0,), (0,)), ((), ()))
NN = (((1,), (0,)), ((), ()))


def _cparams(*sem):
    return pltpu.CompilerParams(dimension_semantics=sem)


def _div_tile(n, unit, cap):
    for d in range(min(n, int(cap)), 0, -1):
        if n % d == 0 and d % unit == 0:
            return d
    return n


def _row_tile(m, row_bytes):
    return _div_tile(m, 16, max(16, VMEM_BLOCK_BUDGET // row_bytes))


def _mm_call(a, b, mode, out_dtype, name):
    if mode == "nn":
        (m, k), n = a.shape, b.shape[1]
    elif mode == "nt":
        (m, k), n = a.shape, b.shape[0]
    else:
        (k, m), n = a.shape, b.shape[1]
    dims = {"nn": NN, "nt": NT, "tn": TN}[mode]
    ia, ib = a.dtype.itemsize, b.dtype.itemsize
    tm = _div_tile(m, LANES, MM_TILE_CAP) if mode == "tn" else _div_tile(m, 16, MM_ROW_CAP)
    tn = _div_tile(n, LANES, min(MM_TILE_CAP, VMEM_BLOCK_BUDGET // (4 * tm)))
    tk = _div_tile(k, 16 if mode == "tn" else LANES,
                   min(MM_ROW_CAP if mode == "tn" else MM_TILE_CAP, VMEM_BLOCK_BUDGET // (tm * ia), VMEM_BLOCK_BUDGET // (tn * ib)))
    nk = k // tk

    def body(a_ref, b_ref, o_ref, *acc):
        part = lax.dot_general(a_ref[...].astype(bf16), b_ref[...].astype(bf16), dims, preferred_element_type=f32)
        if nk == 1:
            o_ref[...] = part.astype(o_ref.dtype)
            return
        kk = pl.program_id(2)

        @pl.when(kk == 0)
        def _():
            acc[0][...] = part

        @pl.when(kk > 0)
        def _():
            acc[0][...] += part

        @pl.when(kk == nk - 1)
        def _():
            o_ref[...] = acc[0][...].astype(o_ref.dtype)

    a_spec = pl.BlockSpec((tk, tm), lambda i, j, kk: (kk, i)) if mode == "tn" else pl.BlockSpec((tm, tk), lambda i, j, kk: (i, kk))
    b_spec = pl.BlockSpec((tn, tk), lambda i, j, kk: (j, kk)) if mode == "nt" else pl.BlockSpec((tk, tn), lambda i, j, kk: (kk, j))
    return pl.pallas_call(
        body, grid=(m // tm, n // tn, nk), in_specs=[a_spec, b_spec],
        out_specs=pl.BlockSpec((tm, tn), lambda i, j, kk: (i, j)),
        out_shape=jax.ShapeDtypeStruct((m, n), out_dtype),
        scratch_shapes=[pltpu.VMEM((tm, tn), f32)] if nk > 1 else [], name=name,
        compiler_params=_cparams("parallel", "parallel", "arbitrary"))(a, b)


def mm(a, b, name, out_dtype=None):
    @jax.custom_vjp
    def op(a, b):
        return _mm_call(a, b, "nn", out_dtype or bf16, name)

    def fwd(a, b):
        return op(a, b), (a, b)

    def bwd(res, g):
        a, b = res
        return _mm_call(g, b, "nt", a.dtype, name + "_da"), _mm_call(a, g, "tn", b.dtype, name + "_db")

    op.defvjp(fwd, bwd)
    return op(a, b)


def split_cols(u, widths):
    offs = [0]
    for w in widths:
        offs.append(offs[-1] + w)

    @jax.custom_vjp
    def op(u):
        return tuple(u[:, offs[i]:offs[i + 1]] for i in range(len(widths)))

    def fwd(u):
        return op(u), None

    def bwd(_, cts):
        return (jnp.concatenate(cts, axis=1),)

    op.defvjp(fwd, bwd)
    return op(u)


def rowwise(fn, rows, consts, pars, out_widths, out_dtypes, name):
    t = rows[0].shape[0]
    tm = ROW_TILE
    nb = t // tm
    nr, nc, npar = len(rows), len(consts), len(pars)

    def rspec(a):
        return pl.BlockSpec((tm, a.shape[1]), lambda i: (i, 0))

    def pspec(a):
        return pl.BlockSpec(a.shape, lambda i: (0,) * a.ndim)

    def call_fwd(rows, consts, pars):
        def body(*refs):
            blk = pl.program_id(0)
            ins = [r[...].astype(f32) for r in refs[:nr + nc]]
            ps = [r[...] for r in refs[nr + nc:nr + nc + npar]]
            outs = fn(blk, *ins, *ps)
            for o_ref, o in zip(refs[nr + nc + npar:], outs):
                o_ref[...] = o.astype(o_ref.dtype)

        return pl.pallas_call(
            body, grid=(nb,),
            in_specs=[rspec(a) for a in rows + consts] + [pspec(a) for a in pars],
            out_specs=[pl.BlockSpec((tm, w), lambda i: (i, 0)) for w in out_widths],
            out_shape=[jax.ShapeDtypeStruct((t, w), d) for w, d in zip(out_widths, out_dtypes)],
            name=name, compiler_params=_cparams("parallel"))(*rows, *consts, *pars)

    def call_bwd(rows, consts, pars, cts):
        nout = len(cts)

        def body(*refs):
            blk = pl.program_id(0)
            ins = [r[...].astype(f32) for r in refs[:nr]]
            cs = [r[...].astype(f32) for r in refs[nr:nr + nc]]
            ps = [r[...] for r in refs[nr + nc:nr + nc + npar]]
            dys = [r[...].astype(f32) for r in refs[nr + nc + npar:nr + nc + npar + nout]]
            d_refs = refs[nr + nc + npar + nout:]
            _, vjp = jax.vjp(lambda *a: tuple(fn(blk, *a[:nr], *cs, *a[nr:])), *ins, *ps)
            grads = vjp(tuple(dys))
            for d_ref, g in zip(d_refs[:nr], grads[:nr]):
                d_ref[...] = g.astype(d_ref.dtype)
            if npar:
                @pl.when(blk == 0)
                def _():
                    for d_ref in d_refs[nr:]:
                        d_ref[...] = jnp.zeros_like(d_ref)

                for d_ref, g in zip(d_refs[nr:], grads[nr:]):
                    d_ref[...] += g

        return pl.pallas_call(
            body, grid=(nb,),
            in_specs=[rspec(a) for a in rows + consts] + [pspec(a) for a in pars] + [rspec(a) for a in cts],
            out_specs=[rspec(a) for a in rows] + [pspec(a) for a in pars],
            out_shape=[jax.ShapeDtypeStruct(a.shape, a.dtype) for a in rows + pars],
            name=name + "_bwd", compiler_params=_cparams("arbitrary"))(*rows, *consts, *pars, *cts)

    @jax.custom_vjp
    def op(rows, consts, pars):
        return tuple(call_fwd(list(rows), list(consts), list(pars)))

    def fwd(rows, consts, pars):
        return op(rows, consts, pars), (rows, consts, pars)

    def bwd(res, cts):
        rows, consts, pars = res
        g = call_bwd(list(rows), list(consts), list(pars), list(cts))
        return tuple(g[:nr]), tuple(jnp.zeros_like(a) for a in consts), tuple(g[nr:])

    op.defvjp(fwd, bwd)
    return op(tuple(rows), tuple(consts), tuple(pars))


def colwise(fn, cols, pars, out_dtype, name):
    t, w = cols[0].shape
    tc = LANES
    nb = w // tc
    ncol, npar = len(cols), len(pars)

    def cspec(a):
        return pl.BlockSpec((a.shape[0], tc), lambda j: (0, j))

    def call_fwd(cols, pars):
        def body(*refs):
            ins = [r[...].astype(f32) for r in refs[:ncol]]
            ps = [r[...] for r in refs[ncol:ncol + npar]]
            refs[-1][...] = fn(*ins, *ps).astype(refs[-1].dtype)

        return pl.pallas_call(
            body, grid=(nb,), in_specs=[cspec(a) for a in cols + pars], out_specs=cspec(cols[0]),
            out_shape=jax.ShapeDtypeStruct((t, w), out_dtype), name=name, compiler_params=_cparams("parallel"))(*cols, *pars)

    def call_bwd(cols, pars, ct):
        def body(*refs):
            ins = [r[...].astype(f32) for r in refs[:ncol]]
            ps = [r[...] for r in refs[ncol:ncol + npar]]
            dy = refs[ncol + npar][...].astype(f32)
            d_refs = refs[ncol + npar + 1:]
            _, vjp = jax.vjp(fn, *ins, *ps)
            grads = vjp(dy)
            for d_ref, g in zip(d_refs, grads):
                d_ref[...] = g.astype(d_ref.dtype)

        return pl.pallas_call(
            body, grid=(nb,), in_specs=[cspec(a) for a in cols + pars + [ct]],
            out_specs=[cspec(a) for a in cols + pars],
            out_shape=[jax.ShapeDtypeStruct(a.shape, a.dtype) for a in cols + pars],
            name=name + "_bwd", compiler_params=_cparams("parallel"))(*cols, *pars, ct)

    @jax.custom_vjp
    def op(cols, pars):
        return call_fwd(list(cols), list(pars))

    def fwd(cols, pars):
        return op(cols, pars), (cols, pars)

    def bwd(res, ct):
        cols, pars = res
        g = call_bwd(list(cols), list(pars), ct)
        return tuple(g[:ncol]), tuple(g[ncol:])

    op.defvjp(fwd, bwd)
    return op(tuple(cols), tuple(pars))


def _sigmoid(x):
    return 1.0 / (1.0 + jnp.exp(-x))


def _silu(x):
    return x * _sigmoid(x)


def _rms(x, g):
    return x * lax.rsqrt(jnp.mean(x * x, axis=-1, keepdims=True) + EPS) * g


def _shift_rows(u, k, n_ctx):
    @jax.custom_vjp
    def op(u):
        t = u.shape[0]
        row = lax.broadcasted_iota(jnp.int32, u.shape, 0)
        edge = ((row == 0) | (row == n_ctx)) if k == 1 else ((row == n_ctx - 1) | (row == t - 1))
        return jnp.where(edge, 0.0, pltpu.roll(u, k % t, 0))

    op.defvjp(lambda u: (op(u), None), lambda _, g: (_shift_rows(g, -k, n_ctx),))
    return op(u)


def _dwconv(u, w0, w1, w2, b, n_ctx):
    return w0 * _shift_rows(u, 1, n_ctx) + w1 * u + w2 * _shift_rows(u, -1, n_ctx) + b


@jax.custom_vjp
def _rot_half(x):
    w = x.shape[1]
    lane = lax.broadcasted_iota(jnp.int32, x.shape, 1)
    return jnp.where((lane % HEAD_DIM) < HEAD_DIM // 2, pltpu.roll(x, w - HEAD_DIM // 2, 1), pltpu.roll(x, HEAD_DIM // 2, 1))


_rot_half.defvjp(lambda x: (_rot_half(x), None), lambda _, g: (_rot_half(g),))


def _head_rms(x, g):
    w = x.shape[1]
    same = (lax.broadcasted_iota(jnp.int32, (w, w), 0) // HEAD_DIM) == (lax.broadcasted_iota(jnp.int32, (w, w), 1) // HEAD_DIM)
    ms = jnp.dot(x * x, same.astype(f32), precision=lax.Precision.HIGHEST, preferred_element_type=f32) * (1.0 / HEAD_DIM)
    return x * lax.rsqrt(ms + EPS) * g


def _band_ok(i, j, c0, shape, tq, tk):
    kpos = j * tk + lax.broadcasted_iota(jnp.int32, shape, 0)
    qpos = i * tq + (c0 + lax.broadcasted_iota(jnp.int32, shape, 1)) % tq
    return jnp.abs(qpos - kpos) <= WINDOW


def _kv_range(i, nb, window):
    is_ctx = i == 0
    if window:
        return jnp.where(is_ctx, 1, jnp.maximum(i - 1, 1)), jnp.where(is_ctx, 1, jnp.minimum(i + 2, nb))
    return 1, jnp.where(is_ctx, 1, nb)


def _sink_row(sink_ref, g, r, tq):
    return jnp.concatenate([jnp.full((1, tq), sink_ref[g * r + h], f32) for h in range(r)], axis=1)


def _attn_fwd_call(q, k, v, sink, window, name):
    h, t, dh = q.shape
    nkv = k.shape[0]
    r = h // nkv
    tq = tk = ROW_TILE
    nb = t // tq
    rows = r * tq

    assert window or nb % 2 == 1, "the dense schedule takes the kv chunks after the context chunk in pairs"

    def body(sink_ref, q_ref, k_ref, v_ref, o_ref, lse_ref, m_scr, l_scr, acc_scr, s_a, s_b, p_a, p_b, a_a, a_b):
        g, i = pl.program_id(0), pl.program_id(1)
        qv = q_ref[...].reshape(rows, dh)
        m_scr[...] = jnp.full_like(m_scr, NEG)
        l_scr[...] = jnp.zeros_like(l_scr)
        acc_scr[...] = jnp.zeros_like(acc_scr)

        def kv_rows(j):
            return pl.ds(pl.multiple_of(jnp.minimum(j, nb - 1) * tk, tk), tk)

        def scores(j, s_scr):
            s_scr[...] = lax.dot_general(k_ref[0, kv_rows(j), :], qv, NT, preferred_element_type=f32)

        def softmax(j, s_scr, p_scr, a_scr, masked):
            for cb in range(rows // ATTN_SLAB):
                cs = slice(cb * ATTN_SLAB, (cb + 1) * ATTN_SLAB)
                s = s_scr[:, cs]
                if masked:
                    s = jnp.where(_band_ok(i, j, cb * ATTN_SLAB, s.shape, tq, tk), s, NEG)
                m = m_scr[:, cs]
                m2 = jnp.maximum(m, jnp.max(s, axis=0, keepdims=True))
                p = jnp.exp(s - m2)
                a = jnp.exp(m - m2)
                l_scr[:, cs] = a * l_scr[:, cs] + jnp.sum(p, axis=0, keepdims=True)
                m_scr[:, cs] = m2
                a_scr[:, cs] = a
                p_scr[:, cs] = p.astype(bf16)

        def weighted_v(j, p_scr, a_scr):
            acc_scr[...] = a_scr[...] * acc_scr[...] + lax.dot_general(v_ref[0, kv_rows(j), :], p_scr[...], TN, preferred_element_type=f32)

        scores(0, s_a)
        softmax(0, s_a, p_a, a_a, False)
        if window:
            weighted_v(0, p_a, a_a)
            lo, hi = _kv_range(i, nb, window)

            def chunk(j, c):
                scores(j, s_a)
                softmax(j, s_a, p_a, a_a, True)
                weighted_v(j, p_a, a_a)
                return c

            lax.fori_loop(lo, hi, chunk, 0)
        else:
            scores(1, s_b)

            def pair(tt, c):
                j0 = 2 * tt + 1
                scores(j0 + 1, s_a)
                weighted_v(j0 - 1, p_a, a_a)
                softmax(j0, s_b, p_b, a_b, False)
                scores(j0 + 2, s_b)
                weighted_v(j0, p_b, a_b)
                softmax(j0 + 1, s_a, p_a, a_a, False)
                return c

            lax.fori_loop(0, jnp.where(i == 0, 0, (nb - 1) // 2), pair, 0)
            weighted_v(jnp.where(i == 0, 0, nb - 1), p_a, a_a)
        m, l, acc = m_scr[...], l_scr[...], acc_scr[...]
        if window:
            sk = _sink_row(sink_ref, g, r, tq)
            m2 = jnp.maximum(m, sk)
            a = jnp.exp(m - m2)
            l = a * l + jnp.exp(sk - m2)
            acc = a * acc
            m = m2
        o_ref[...] = (acc / l).T.reshape(r, tq, dh).astype(o_ref.dtype)
        lse_ref[0] = m + jnp.log(l)

    qspec = pl.BlockSpec((r, tq, dh), lambda g, i: (g, i, 0))
    kspec = pl.BlockSpec((1, t, dh), lambda g, i: (g, 0, 0))
    return pl.pallas_call(
        body, grid=(nkv, nb),
        in_specs=[pl.BlockSpec(memory_space=pltpu.SMEM), qspec, kspec, kspec],
        out_specs=[qspec, pl.BlockSpec((1, 1, rows), lambda g, i: (g * nb + i, 0, 0))],
        out_shape=[jax.ShapeDtypeStruct((h, t, dh), bf16), jax.ShapeDtypeStruct((nkv * nb, 1, rows), f32)],
        scratch_shapes=[pltpu.VMEM((1, rows), f32), pltpu.VMEM((1, rows), f32), pltpu.VMEM((dh, rows), f32),
                        pltpu.VMEM((tk, rows), f32), pltpu.VMEM((tk, rows), f32), pltpu.VMEM((tk, rows), bf16),
                        pltpu.VMEM((tk, rows), bf16), pltpu.VMEM((1, rows), f32), pltpu.VMEM((1, rows), f32)],
        name=name, compiler_params=_cparams("parallel", "parallel"))(sink, q, k, v)


def _attn_bwd_call(q, k, v, sink, o, lse, do, window, name):
    h, t, dh = q.shape
    nkv = k.shape[0]
    r = h // nkv
    tq = tk = ROW_TILE
    nb = t // tq
    rows = r * tq

    def body(sink_ref, q_ref, k_ref, v_ref, o_ref, lse_ref, do_ref, dq_ref, dk_ref, dv_ref, dsink_ref,
             s_scr, dp_scr, p_scr, ds_scr, dq_scr):
        g, i = pl.program_id(0), pl.program_id(1)

        @pl.when(i == 0)
        def _():
            dk_ref[...] = jnp.zeros_like(dk_ref)
            dv_ref[...] = jnp.zeros_like(dv_ref)

        qv = q_ref[...].reshape(rows, dh)
        dov = do_ref[...].reshape(rows, dh)
        lse_t = lse_ref[0]
        delta_t = jnp.sum((dov.astype(f32) * o_ref[...].reshape(rows, dh).astype(f32)).T, axis=0, keepdims=True)
        dq_scr[...] = jnp.zeros_like(dq_scr)

        def chunk(j, masked):
            kv_rows = pl.ds(pl.multiple_of(j * tk, tk), tk)
            kj = k_ref[0, kv_rows, :]
            vj = v_ref[0, kv_rows, :]
            s_scr[...] = lax.dot_general(kj, qv, NT, preferred_element_type=f32)
            dp_scr[...] = lax.dot_general(vj, dov, NT, preferred_element_type=f32)
            for cb in range(rows // ATTN_SLAB):
                cs = slice(cb * ATTN_SLAB, (cb + 1) * ATTN_SLAB)
                s = s_scr[:, cs]
                if masked:
                    s = jnp.where(_band_ok(i, j, cb * ATTN_SLAB, s.shape, tq, tk), s, NEG)
                p = jnp.exp(s - lse_t[:, cs])
                p_scr[:, cs] = p.astype(bf16)
                ds_scr[:, cs] = (p * (dp_scr[:, cs] - delta_t[:, cs])).astype(bf16)
            dv_ref[0, kv_rows, :] += jnp.dot(p_scr[...], dov, preferred_element_type=f32)
            dk_ref[0, kv_rows, :] += jnp.dot(ds_scr[...], qv, preferred_element_type=f32)
            dq_scr[...] += lax.dot_general(kj, ds_scr[...], TN, preferred_element_type=f32)

        chunk(0, False)
        lo, hi = _kv_range(i, nb, window)
        lax.fori_loop(lo, hi, lambda j, c: (chunk(j, window), c)[1], 0)
        dq_ref[...] = dq_scr[...].T.reshape(r, tq, dh).astype(dq_ref.dtype)
        if window:
            dsink_ref[0] = -jnp.exp(_sink_row(sink_ref, g, r, tq) - lse_t) * delta_t
        else:
            dsink_ref[...] = jnp.zeros_like(dsink_ref)

    qspec = pl.BlockSpec((r, tq, dh), lambda g, i: (g, i, 0))
    cspec = pl.BlockSpec((1, 1, rows), lambda g, i: (g * nb + i, 0, 0))
    kspec = pl.BlockSpec((1, t, dh), lambda g, i: (g, 0, 0))
    return pl.pallas_call(
        body, grid=(nkv, nb),
        in_specs=[pl.BlockSpec(memory_space=pltpu.SMEM), qspec, kspec, kspec, qspec, cspec, qspec],
        out_specs=[qspec, kspec, kspec, cspec],
        out_shape=[jax.ShapeDtypeStruct((h, t, dh), bf16), jax.ShapeDtypeStruct(k.shape, f32), jax.ShapeDtypeStruct(v.shape, f32),
                   jax.ShapeDtypeStruct((nkv * nb, 1, rows), f32)],
        scratch_shapes=[pltpu.VMEM((tk, rows), f32), pltpu.VMEM((tk, rows), f32), pltpu.VMEM((tk, rows), bf16),
                        pltpu.VMEM((tk, rows), bf16), pltpu.VMEM((dh, rows), f32)],
        name=name, compiler_params=_cparams("parallel", "arbitrary"))(sink, q, k, v, o, lse, do)


def attention(q, k, v, sink, window, name):
    @jax.custom_vjp
    def op(q, k, v, sink):
        return _attn_fwd_call(q, k, v, sink, window, name)[0]

    def fwd(q, k, v, sink):
        o, lse = _attn_fwd_call(q, k, v, sink, window, name)
        return o, (q, k, v, sink, o, lse)

    def bwd(res, do):
        q, k, v, sink, o, lse = res
        dq, dk, dv, dsink_rows = _attn_bwd_call(q, k, v, sink, o, lse, do, window, name + "_bwd")
        nkv, r = k.shape[0], q.shape[0] // k.shape[0]
        dsink = jnp.sum(dsink_rows.reshape(nkv, -1, r, ROW_TILE), axis=(1, 3)).reshape(nkv * r)
        return dq, dk.astype(k.dtype), dv.astype(v.dtype), dsink

    op.defvjp(fwd, bwd)
    return op(q, k, v, sink)


def _ssd_pair(xs_p, dtx_p, dtr1, dtr2, ac1, ac2, bg, cg, hin_p, rev):
    q = xs_p.shape[0]
    ii = lax.broadcasted_iota(jnp.int32, (q, q), 0)
    jj = lax.broadcasted_iota(jnp.int32, (q, q), 1)
    tri = (ii <= jj) if rev else (ii >= jj)
    lo = lax.broadcasted_iota(jnp.int32, (q, LANES), 1) < SSM_P
    lo_row = lax.broadcasted_iota(jnp.int32, (1, LANES), 1) < SSM_P

    def cums(dtr, ac):
        a = dtr * ac
        c = jnp.sum(jnp.where(tri, jnp.broadcast_to(a, (q, q)), 0.0), axis=1, keepdims=True)
        return c, jnp.sum(a, axis=1, keepdims=True)

    def lmat(c):
        cf = jnp.broadcast_to(c, (q, q))
        return jnp.where(tri, jnp.exp(jnp.minimum(cf - cf.T, 0.0)), 0.0)

    c1, t1 = cums(dtr1, ac1)
    c2, t2 = cums(dtr2, ac2)
    cb = lax.dot_general(cg.astype(bf16), bg.astype(bf16), NT, preferred_element_type=f32)
    m = jnp.concatenate([cb * lmat(c1), cb * lmat(c2)], axis=1)
    xdt = xs_p * dtx_p
    x2 = jnp.concatenate([jnp.where(lo, xdt, 0.0), jnp.where(lo, 0.0, xdt)], axis=0)
    y_diag = jnp.dot(m.astype(bf16), x2.astype(bf16), preferred_element_type=f32)
    csel = jnp.where(lo, jnp.broadcast_to(c1, (q, LANES)), jnp.broadcast_to(c2, (q, LANES)))
    tsel = jnp.where(lo_row, jnp.broadcast_to(t1, (1, LANES)), jnp.broadcast_to(t2, (1, LANES)))
    st = lax.dot_general(bg.astype(bf16), (xdt * jnp.exp(tsel - csel)).astype(bf16), TN, preferred_element_type=f32)
    y_off = jnp.dot(cg.astype(bf16), hin_p.astype(bf16), preferred_element_type=f32) * jnp.exp(csel)
    return y_diag + y_off, hin_p * jnp.exp(tsel) + st


def _ssd_order(s, nc, ncc, rev):
    if not rev:
        return s
    return jnp.where(s < ncc, ncc - 1 - s, nc - 1 - (s - ncc))


def _ssd_pair_slices(j):
    return slice(LANES * j, LANES * (j + 1)), 2 * j, 2 * j + 1, (2 * j) // (SSM_HEADS // SSM_G)


def _head_lanes(w, transpose=False):
    shape = (w, SSM_HEADS) if transpose else (SSM_HEADS, w)
    head = lax.broadcasted_iota(jnp.int32, shape, 1 if transpose else 0)
    lane = lax.broadcasted_iota(jnp.int32, shape, 0 if transpose else 1)
    return (lane // SSM_P == head).astype(f32)


def _ssd_fwd_call(xs, dt, dtr, bm, cm, acol, rev, n_ctx, name):
    t, w = xs.shape
    q = SSM_Q
    nc, ncc = t // q, n_ctx // q

    def body(xs_ref, dt_ref, dtr_ref, b_ref, c_ref, a_ref, y_ref, hin_ref, h_scr):
        @pl.when(pl.program_id(0) == 0)
        def _():
            h_scr[...] = jnp.zeros_like(h_scr)

        hin_ref[0] = h_scr[...]
        dtx = jnp.dot(dt_ref[...], _head_lanes(w), precision=lax.Precision.HIGHEST, preferred_element_type=f32)
        for j in range(SSM_HEADS // 2):
            sl, h1, h2, g = _ssd_pair_slices(j)
            gs = slice(SSM_N * g, SSM_N * (g + 1))
            y, hout = _ssd_pair(xs_ref[:, sl], dtx[:, sl], dtr_ref[h1:h1 + 1, :], dtr_ref[h2:h2 + 1, :],
                                a_ref[h1:h1 + 1, :], a_ref[h2:h2 + 1, :], b_ref[:, gs], c_ref[:, gs], h_scr[:, sl], rev)
            y_ref[:, sl] = y.astype(y_ref.dtype)
            h_scr[:, sl] = hout

    def at(s):
        return _ssd_order(s, nc, ncc, rev)

    return pl.pallas_call(
        body, grid=(nc,),
        in_specs=[pl.BlockSpec((q, w), lambda s: (at(s), 0)), pl.BlockSpec((q, SSM_HEADS), lambda s: (at(s), 0)),
                  pl.BlockSpec((SSM_HEADS, q), lambda s: (0, at(s))),
                  pl.BlockSpec((q, SSM_BC), lambda s: (at(s), 0)), pl.BlockSpec((q, SSM_BC), lambda s: (at(s), 0)),
                  pl.BlockSpec((SSM_HEADS, 1), lambda s: (0, 0))],
        out_specs=[pl.BlockSpec((q, w), lambda s: (at(s), 0)), pl.BlockSpec((1, SSM_N, w), lambda s: (s, 0, 0))],
        out_shape=[jax.ShapeDtypeStruct((t, w), xs.dtype), jax.ShapeDtypeStruct((nc, SSM_N, w), f32)],
        scratch_shapes=[pltpu.VMEM((SSM_N, w), f32)],
        name=name, compiler_params=_cparams("arbitrary"))(xs, dt, dtr, bm, cm, acol)


def _ssd_bwd_call(xs, dt, dtr, bm, cm, acol, hin, dy, rev, n_ctx, name):
    t, w = xs.shape
    q = SSM_Q
    nc, ncc = t // q, n_ctx // q

    def body(xs_ref, dt_ref, dtr_ref, b_ref, c_ref, a_ref, hin_ref, dy_ref,
             dxs_ref, ddt_ref, ddtr_ref, db_ref, dc_ref, da_ref, dh_scr):
        @pl.when(pl.program_id(0) == 0)
        def _():
            dh_scr[...] = jnp.zeros_like(dh_scr)
            da_ref[...] = jnp.zeros_like(da_ref)

        dtx = jnp.dot(dt_ref[...], _head_lanes(w), precision=lax.Precision.HIGHEST, preferred_element_type=f32)
        ddtx = []
        db = [None] * SSM_G
        dc = [None] * SSM_G
        for j in range(SSM_HEADS // 2):
            sl, h1, h2, g = _ssd_pair_slices(j)
            gs = slice(SSM_N * g, SSM_N * (g + 1))
            _, vjp = jax.vjp(
                functools.partial(_ssd_pair, rev=rev),
                xs_ref[:, sl].astype(f32), dtx[:, sl], dtr_ref[h1:h1 + 1, :], dtr_ref[h2:h2 + 1, :],
                a_ref[h1:h1 + 1, :], a_ref[h2:h2 + 1, :], b_ref[:, gs].astype(f32), c_ref[:, gs].astype(f32), hin_ref[0, :, sl])
            gr = vjp((dy_ref[:, sl].astype(f32), dh_scr[:, sl]))
            dxs_ref[:, sl] = gr[0].astype(dxs_ref.dtype)
            ddtx.append(gr[1])
            ddtr_ref[h1:h1 + 1, :] = gr[2]
            ddtr_ref[h2:h2 + 1, :] = gr[3]
            da_ref[h1:h1 + 1, :] += gr[4]
            da_ref[h2:h2 + 1, :] += gr[5]
            db[g] = gr[6] if db[g] is None else db[g] + gr[6]
            dc[g] = gr[7] if dc[g] is None else dc[g] + gr[7]
            dh_scr[:, sl] = gr[8]
        for g in range(SSM_G):
            gs = slice(SSM_N * g, SSM_N * (g + 1))
            db_ref[:, gs] = db[g].astype(db_ref.dtype)
            dc_ref[:, gs] = dc[g].astype(dc_ref.dtype)
        ddt_ref[...] = jnp.dot(jnp.concatenate(ddtx, axis=1), _head_lanes(w, transpose=True),
                               precision=lax.Precision.HIGHEST, preferred_element_type=f32)

    def step(s):
        return nc - 1 - s

    def at(s):
        return _ssd_order(step(s), nc, ncc, rev)

    row = lambda wd: pl.BlockSpec((q, wd), lambda s: (at(s), 0))
    dtr_spec = pl.BlockSpec((SSM_HEADS, q), lambda s: (0, at(s)))
    a_spec = pl.BlockSpec((SSM_HEADS, 1), lambda s: (0, 0))
    return pl.pallas_call(
        body, grid=(nc,),
        in_specs=[row(w), row(SSM_HEADS), dtr_spec, row(SSM_BC), row(SSM_BC), a_spec,
                  pl.BlockSpec((1, SSM_N, w), lambda s: (step(s), 0, 0)), row(w)],
        out_specs=[row(w), row(SSM_HEADS), dtr_spec, row(SSM_BC), row(SSM_BC), a_spec],
        out_shape=[jax.ShapeDtypeStruct((t, w), xs.dtype), jax.ShapeDtypeStruct(dt.shape, f32), jax.ShapeDtypeStruct(dtr.shape, f32),
                   jax.ShapeDtypeStruct(bm.shape, bm.dtype), jax.ShapeDtypeStruct(cm.shape, cm.dtype), jax.ShapeDtypeStruct(acol.shape, f32)],
        scratch_shapes=[pltpu.VMEM((SSM_N, w), f32)],
        name=name, compiler_params=_cparams("arbitrary"))(xs, dt, dtr, bm, cm, acol, hin, dy)


def ssd_scan(xs, dt, dtr, bm, cm, acol, rev, n_ctx, name):
    @jax.custom_vjp
    def op(xs, dt, dtr, bm, cm, acol):
        return _ssd_fwd_call(xs, dt, dtr, bm, cm, acol, rev, n_ctx, name)[0]

    def fwd(xs, dt, dtr, bm, cm, acol):
        y, hin = _ssd_fwd_call(xs, dt, dtr, bm, cm, acol, rev, n_ctx, name)
        return y, (xs, dt, dtr, bm, cm, acol, hin)

    def bwd(res, dy):
        return tuple(_ssd_bwd_call(*res, dy, rev, n_ctx, name + "_bwd"))

    op.defvjp(fwd, bwd)
    return op(xs, dt, dtr, bm, cm, acol)


def _deinterleave(w, n_heads):
    lead = w.shape[:-1]
    return w.reshape(*lead, n_heads, HEAD_DIM // 2, 2).swapaxes(-1, -2).reshape(*lead, n_heads * HEAD_DIM)


def _interleave(w, n_heads):
    lead = w.shape[:-1]
    return w.reshape(*lead, n_heads, 2, HEAD_DIM // 2).swapaxes(-1, -2).reshape(*lead, n_heads * HEAD_DIM)


def _in_layout(d):
    sizes = [('a_q', Q_W, N_HEADS), ('a_k', KV_W, N_KV), ('a_v', KV_W, 0), ('b_z', SSM_INNER, 0),
             ('b_xbc', SSM_INNER + 2 * SSM_BC, 0), ('b_dt', DT_W, 0), ('c_q', Q_W, N_HEADS), ('c_k', KV_W, N_KV),
             ('c_v', KV_W, 0), ('g_a', d, 0), ('g_b', d, 0), ('g_c', d, 0)]
    out, start = [], 0
    for name, n, heads in sizes:
        out.append((name, start, n, heads))
        start += n
    return out


@jax.custom_vjp
def _w_in_split(w):
    d = w.shape[0]
    parts, dt = [], None
    for name, s, n, heads in _in_layout(d):
        p = w[:, s:s + n]
        if heads:
            p = _deinterleave(p, heads)
        if name == 'b_dt':
            dt = jnp.concatenate([p, jnp.zeros((d, DT_PAD - n), w.dtype)], axis=1)
        else:
            parts.append(p)
    return jnp.concatenate(parts, axis=1), dt


def _w_in_join(g_main, g_dt):
    d = g_main.shape[0]
    parts, start = [], 0
    for name, _, n, heads in _in_layout(d):
        if name == 'b_dt':
            parts.append(g_dt[:, :n])
            continue
        p = g_main[:, start:start + n]
        parts.append(_interleave(p, heads) if heads else p)
        start += n
    return jnp.concatenate(parts, axis=1)


_w_in_split.defvjp(lambda w: (_w_in_split(w), None), lambda _, g: (_w_in_join(*g),))


def _rope_tables(n_ctx, n_lat):
    rows = n_lat // GRID_W
    t_row = jnp.repeat(jnp.arange(rows), GRID_W).astype(f32)
    t_col = jnp.tile(jnp.arange(GRID_W), rows).astype(f32)
    n = HEAD_DIM // 4
    inv = ROPE_BASE ** (-jnp.arange(n, dtype=f32) / n)
    ang = jnp.concatenate([t_row[:, None] * inv, t_col[:, None] * inv], axis=-1)
    cos = jnp.concatenate([jnp.ones((n_ctx, HEAD_DIM // 2), f32), jnp.cos(ang)], axis=0)
    sin = jnp.concatenate([jnp.zeros((n_ctx, HEAD_DIM // 2), f32), jnp.sin(ang)], axis=0)
    return jnp.concatenate([cos, cos], axis=1), jnp.concatenate([-sin, sin], axis=1)


def _heads_major(a, n_heads):
    return a.reshape(a.shape[0], n_heads, HEAD_DIM).transpose(1, 0, 2)


def _heads_minor(a):
    return a.transpose(1, 0, 2).reshape(a.shape[1], a.shape[0] * HEAD_DIM)


def _layer(xall, w, s, cm, tabs, n_ctx, li):
    t, d = xall.shape
    ncb = n_ctx // ROW_TILE
    nm = f"l{li}_"
    ctq, stq, ctk, stk = tabs
    mod = [(cm[0:1, i * d:(i + 1) * d], cm[1:2, i * d:(i + 1) * d]) for i in range(6)]

    def pick(blk, pair_c, pair_l):
        return jnp.where(blk < ncb, pair_c, pair_l)

    def norm_mod(blk, x, g, sh_c, sh_l, sc_c, sc_l):
        return (_rms(x, g) * (1.0 + pick(blk, sc_c, sc_l)) + pick(blk, sh_c, sh_l),)

    (h,) = rowwise(norm_mod, [xall], [], [s['norm1'][None], *mod[0], *mod[1]], [d], [bf16], nm + "norm1")
    w_main, w_dt = _w_in_split(w['w_in'])
    u = mm(h, w_main, nm + "in")
    b_dt = mm(h, w_dt, nm + "in_dt", f32)
    a_q, a_k, a_v, b_z, b_xbc, c_q, c_k, c_v, g_a, g_b, g_c = split_cols(u, [n for name, _, n, _ in _in_layout(d) if name != 'b_dt'])

    def rope(blk, q, k, v, ct_q, st_q, ct_k, st_k):
        return q * ct_q + _rot_half(q) * st_q, k * ct_k + _rot_half(k) * st_k, v

    def norm_rope(blk, q, k, v, ct_q, st_q, ct_k, st_k, gq, gk):
        return rope(blk, _head_rms(q, gq), _head_rms(k, gk), v, ct_q, st_q, ct_k, st_k)

    qkv_w, qkv_t = [Q_W, KV_W, KV_W], [bf16, bf16, bf16]
    qa, ka, va = rowwise(rope, [a_q, a_k, a_v], [ctq, stq, ctk, stk], [], qkv_w, qkv_t, nm + "ropeA")
    gq = jnp.tile(_deinterleave(s['c_q_norm'], 1), N_HEADS)[None]
    gk = jnp.tile(_deinterleave(s['c_k_norm'], 1), N_KV)[None]
    qc, kc, vc = rowwise(norm_rope, [c_q, c_k, c_v], [ctq, stq, ctk, stk], [gq, gk], qkv_w, qkv_t, nm + "ropeC")
    ya = _heads_minor(attention(_heads_major(qa, N_HEADS), _heads_major(ka, N_KV), _heads_major(va, N_KV),
                                s['a_sink'], True, nm + "attnA"))
    yc = _heads_minor(attention(_heads_major(qc, N_HEADS), _heads_major(kc, N_KV), _heads_major(vc, N_KV),
                                jnp.zeros((N_HEADS,), f32), False, nm + "attnC"))

    cw, cb = s['ssm_conv_w'], s['ssm_conv_b']
    conv_silu = lambda uu, w0, w1, w2, b: _silu(_dwconv(uu, w0, w1, w2, b, n_ctx))
    xbc = colwise(conv_silu, [b_xbc], [cw[0:1], cw[1:2], cw[2:3], cb[None]], bf16, nm + "ssmconv")
    xs, bm, cmat = split_cols(xbc, [SSM_INNER, SSM_BC, SSM_BC])
    bias = jnp.concatenate([s['ssm_dt_bias'].reshape(1, DT_W), jnp.zeros((1, DT_PAD - DT_W), f32)], axis=1)

    def softplus(blk, r, b):
        z = r + b
        return (jnp.maximum(z, 0.0) + jnp.log(1.0 + jnp.exp(-jnp.abs(z))),)

    (dt_all,) = rowwise(softplus, [b_dt], [], [bias], [DT_PAD], [f32], nm + "dt")
    a_coef = -jnp.exp(s['ssm_A_log'])
    ys_dir = []
    for di, rev in enumerate((False, True)):
        dt = dt_all[:, di * SSM_HEADS:(di + 1) * SSM_HEADS]
        ys_dir.append(ssd_scan(xs, dt, dt.T, bm, cmat, a_coef[di][:, None], rev, n_ctx,
                               nm + ("ssd_r" if rev else "ssd_f")))

    def ssm_out(blk, yf, yb, x, z, dskip, g):
        return (_rms((yf + yb + x * dskip) * _silu(z), g),)

    (ysn,) = rowwise(ssm_out, [ys_dir[0], ys_dir[1], xs, b_z], [], [jnp.repeat(s['ssm_D'], SSM_P)[None], s['ssm_norm'][None]],
                     [SSM_INNER], [bf16], nm + "ssmout")

    pa, pb, pc = mm(ya, w['w_oa'], nm + "oa"), mm(ysn, w['w_ob'], nm + "ob"), mm(yc, w['w_oc'], nm + "oc")

    def merge(blk, ga, gb, gc, a, b, c):
        return (_sigmoid(ga) * a + _sigmoid(gb) * b + _sigmoid(gc) * c,)

    (mrg,) = rowwise(merge, [g_a, g_b, g_c, pa, pb, pc], [], [], [d], [bf16], nm + "merge")
    o = mm(mrg, w['w_out'], nm + "out")

    def resid_norm_mod(blk, x, oo, g1_c, g1_l, g, sh_c, sh_l, sc_c, sc_l):
        x1 = x + pick(blk, g1_c, g1_l) * oo
        return x1, _rms(x1, g) * (1.0 + pick(blk, sc_c, sc_l)) + pick(blk, sh_c, sh_l)

    x1, h2 = rowwise(resid_norm_mod, [xall, o], [], [*mod[2], s['norm2'][None], *mod[3], *mod[4]], [d, d], [f32, bf16], nm + "norm2")
    up, gt = mm(h2, w['ffn_w_up'], nm + "up"), mm(h2, w['ffn_w_gate'], nm + "gate")
    fw, fb = s['ffn_conv_w'], s['ffn_conv_b']
    ffn_act = lambda g_, u_, w0, w1, w2, b: _silu(_dwconv(g_, w0, w1, w2, b, n_ctx)) * u_
    act = colwise(ffn_act, [gt, up], [fw[0:1], fw[1:2], fw[2:3], fb[None]], bf16, nm + "ffnact")
    f = mm(act, w['ffn_w_down'], nm + "down")

    def resid(blk, x, ff, g2_c, g2_l):
        return (x + pick(blk, g2_c, g2_l) * ff,)

    (x2,) = rowwise(resid, [x1, f], [], [*mod[5]], [d], [f32], nm + "resid")
    return x2


def _loss_fn(big, small, x, ctx, c, target, n_ctx):
    n_lat, d = x.shape
    depth = len(big)
    xall = jnp.concatenate([ctx, x], axis=0)
    ct, st = _rope_tables(n_ctx, n_lat)
    tabs = (jnp.tile(ct, (1, N_HEADS)) * HEAD_DIM ** -0.5, jnp.tile(st, (1, N_HEADS)) * HEAD_DIM ** -0.5,
            jnp.tile(ct, (1, N_KV)), jnp.tile(st, (1, N_KV)))
    srows = jnp.concatenate([_silu(small['c_ctx'])[None], _silu(c), jnp.zeros((14, d), f32)], axis=0)
    for li in range(depth):
        cm = mm(srows, big[li]['w_mod'], f"l{li}_mod", f32)[0:2] + small['b_mod'][li][None]
        sl = {k: v[li] for k, v in small.items() if k not in ('c_ctx', 'final_norm')}
        xall = _layer(xall, big[li], sl, cm, tabs, n_ctx, li)
    ncb = n_ctx // ROW_TILE
    tgt = jnp.concatenate([jnp.zeros((n_ctx, d), f32), target], axis=0)

    def loss_rows(blk, xx, tg, g):
        e = _rms(xx, g) - tg
        return (jnp.where(blk < ncb, 0.0, 0.5) * jnp.mean(e * e, axis=-1, keepdims=True),)

    (rows,) = rowwise(loss_rows, [xall], [tgt], [small['final_norm'][None]], [1], [f32], "loss")
    return jnp.sum(rows)


def _hbm_call(body, ins, out_shapes, n_sems, name):
    any_spec = pl.BlockSpec(memory_space=pl.ANY)
    return pl.pallas_call(
        body, out_shape=out_shapes, in_specs=[any_spec] * len(ins), out_specs=[any_spec] * len(out_shapes),
        scratch_shapes=[pltpu.SemaphoreType.DMA((n_sems,)), pltpu.SemaphoreType.DMA((n_sems,)), pltpu.SemaphoreType.DMA((len(ins),))],
        name=name)(*ins)


def all_gather(shards, name):
    n = len(shards)

    def body(*refs):
        x_refs, out_refs, (send_sems, recv_sems, local_sems) = refs[:n], refs[n:2 * n], refs[2 * n:]
        x, y, c = lax.axis_index("x"), lax.axis_index("y"), lax.axis_index("c")
        me, sibling = (x, y, c), (x, y, 1 - c)
        chips = [(1 - x, y), (x, 1 - y), (1 - x, 1 - y)]

        def copy(a, k, block, to, src=None):
            px, py, pc = block
            slot = out_refs[a].at[4 * px + 2 * py + pc]
            return pltpu.make_async_remote_copy(
                src_ref=slot if src is None else src, dst_ref=slot,
                send_sem=send_sems.at[7 * a + k], recv_sem=recv_sems.at[7 * a + k], device_id=to, device_id_type=MESH)

        mine = [pltpu.make_async_copy(x_refs[a], out_refs[a].at[4 * x + 2 * y + c], local_sems.at[a]) for a in range(n)]
        first = []
        for a in range(n):
            mine[a].start()
            first += [copy(a, 1 + j, me, (*chip, c), src=x_refs[a]) for j, chip in enumerate(chips)]
            first.append(copy(a, 0, me, sibling, src=x_refs[a]))
        for cp in first:
            cp.start()
        passed = []
        for a in range(n):
            for j, chip in enumerate(chips):
                copy(a, 1 + j, (*chip, c), me).wait_recv()
                passed.append(copy(a, 4 + j, (*chip, c), sibling))
                passed[-1].start()
        for a in range(n):
            copy(a, 0, sibling, me).wait_recv()
            for j, chip in enumerate(chips):
                copy(a, 4 + j, (*chip, 1 - c), me).wait_recv()
        for cp in first + passed:
            cp.wait_send()
        for cp in mine:
            cp.wait()

    return _hbm_call(body, shards, [jax.ShapeDtypeStruct((8,) + s.shape, s.dtype) for s in shards], 7 * n, name)


def rs_to_sibling(gs):
    n = len(gs)

    def body(*refs):
        g_refs, out_refs, (send_sems, recv_sems, _) = refs[:n], refs[n:2 * n], refs[2 * n:]
        x, y, c = lax.axis_index("x"), lax.axis_index("y"), lax.axis_index("c")
        copies = [pltpu.make_async_remote_copy(
            src_ref=g_refs[a].at[2 * k + (1 - c)], dst_ref=out_refs[a].at[k], send_sem=send_sems.at[4 * a + k],
            recv_sem=recv_sems.at[4 * a + k], device_id=(x, y, 1 - c), device_id_type=MESH) for a in range(n) for k in range(4)]
        for cp in copies:
            cp.start()
        for cp in copies:
            cp.wait()

    return _hbm_call(body, gs, [jax.ShapeDtypeStruct((4,) + g.shape[1:], g.dtype) for g in gs], 4 * n, "rs_sibling")


def rs_to_chips(ss):
    n = len(ss)
    flips = [(1, 0), (0, 1), (1, 1)]

    def body(*refs):
        s_refs, out_refs, (send_sems, recv_sems, _) = refs[:n], refs[n:2 * n], refs[2 * n:]
        x, y, c = lax.axis_index("x"), lax.axis_index("y"), lax.axis_index("c")
        copies = []
        for a in range(n):
            for k, (fx, fy) in enumerate(flips):
                px, py = (1 - x) if fx else x, (1 - y) if fy else y
                copies.append(pltpu.make_async_remote_copy(
                    src_ref=s_refs[a].at[2 * px + py], dst_ref=out_refs[a].at[k], send_sem=send_sems.at[3 * a + k],
                    recv_sem=recv_sems.at[3 * a + k], device_id=(px, py, c), device_id_type=MESH))
        for cp in copies:
            cp.start()
        for cp in copies:
            cp.wait()

    return _hbm_call(body, ss, [jax.ShapeDtypeStruct((3,) + s.shape[1:], s.dtype) for s in ss], 3 * n, "rs_chips")


def _flat_tile(rows, cols):
    return _row_tile(rows, 4 * 4 * cols)


def pair_sum(g, r1, my_c, name):
    _, rows, cols = g.shape
    tm = _flat_tile(rows, cols)

    def body(c_ref, g_ref, r_ref, o_ref):
        o_ref[...] = (g_ref[...].astype(f32) + r_ref[...].astype(f32)).astype(o_ref.dtype)

    return pl.pallas_call(
        body, grid_spec=pltpu.PrefetchScalarGridSpec(
            num_scalar_prefetch=1, grid=(4, rows // tm),
            in_specs=[pl.BlockSpec((1, tm, cols), lambda k, i, c: (2 * k + c[0], i, 0)),
                      pl.BlockSpec((1, tm, cols), lambda k, i, c: (k, i, 0))],
            out_specs=pl.BlockSpec((1, tm, cols), lambda k, i, c: (k, i, 0))),
        out_shape=jax.ShapeDtypeStruct((4, rows, cols), g.dtype), name=name,
        compiler_params=_cparams("parallel", "parallel"))(my_c, g, r1)


def _adam_math(w, g, m, v):
    m2 = ADAM_B1 * m + (1.0 - ADAM_B1) * g
    v2 = ADAM_B2 * v + (1.0 - ADAM_B2) * (g * g)
    m_hat = m2 / (1.0 - ADAM_B1 ** ADAM_STEP)
    v_hat = v2 / (1.0 - ADAM_B2 ** ADAM_STEP)
    return -ADAM_LR * (m_hat / (jnp.sqrt(v_hat) + ADAM_EPS) + ADAM_WD * w), m2, v2


def sum_adam(parts, w, m, v, name):
    rows, cols = w.shape
    tm = _flat_tile(rows, cols)
    flat = []
    scalars = [p[2] for p in parts if p[2] is not None]
    assert len(scalars) <= 1
    for arr, static_rows, dyn in parts:
        if dyn is not None:
            flat.append((arr, lambda i, s: (s[0], i, 0)))
        else:
            for k in static_rows:
                flat.append((arr, functools.partial(lambda i, s, k: (k, i, 0), k=k)))
    na = len(flat)

    def body(s_ref, *refs):
        g = refs[0][0].astype(f32)
        for r in refs[1:na]:
            g = g + r[0].astype(f32)
        w_ref, m_ref, v_ref = refs[na:na + 3]
        g_out, d_out, m_out, v_out = refs[na + 3:]
        d, m2, v2 = _adam_math(w_ref[...], g, m_ref[...], v_ref[...])
        g_out[...] = g
        d_out[...] = d
        m_out[...] = m2
        v_out[...] = v2

    blk = pl.BlockSpec((tm, cols), lambda i, s: (i, 0))
    scalar = scalars[0] if scalars else jnp.zeros((1,), jnp.int32)
    return pl.pallas_call(
        body, grid_spec=pltpu.PrefetchScalarGridSpec(
            num_scalar_prefetch=1, grid=(rows // tm,),
            in_specs=[pl.BlockSpec((1, tm, cols), im) for _, im in flat] + [blk, blk, blk],
            out_specs=[blk, blk, blk, blk]),
        out_shape=[jax.ShapeDtypeStruct((rows, cols), f32)] * 4, name=name,
        compiler_params=_cparams("parallel"))(scalar, *[a for a, _ in flat], w, m, v)


FLAT_COLS = 1024


def _to_flat(vec):
    n = vec.shape[0]
    total = -(-n // (8 * FLAT_COLS)) * 8 * FLAT_COLS
    return jnp.concatenate([vec, jnp.zeros((total - n,), vec.dtype)]).reshape(-1, FLAT_COLS)


def _pack(tree, names):
    return jnp.concatenate([tree[n].reshape(-1) for n in names])


def _unpack(vec, like, names):
    out, off = {}, 0
    for n in names:
        size = like[n].size
        out[n] = vec[off:off + size].reshape(like[n].shape)
        off += size
    return out


def kernel(x, c, ctx, c_ctx, w_mod, b_mod, norm1, norm2, w_in, a_sink, ssm_conv_w, ssm_conv_b, ssm_A_log, ssm_dt_bias, ssm_D, ssm_norm, c_q_norm, c_k_norm, w_oa, w_ob, w_oc, w_out, ffn_w_up, ffn_w_gate, ffn_conv_w, ffn_conv_b, ffn_w_down, final_norm, loss_target, m_c_ctx, m_w_mod, m_b_mod, m_norm1, m_norm2, m_w_in, m_a_sink, m_ssm_conv_w, m_ssm_conv_b, m_ssm_A_log, m_ssm_dt_bias, m_ssm_D, m_ssm_norm, m_c_q_norm, m_c_k_norm, m_w_oa, m_w_ob, m_w_oc, m_w_out, m_ffn_w_up, m_ffn_w_gate, m_ffn_conv_w, m_ffn_conv_b, m_ffn_w_down, m_final_norm, v_c_ctx, v_w_mod, v_b_mod, v_norm1, v_norm2, v_w_in, v_a_sink, v_ssm_conv_w, v_ssm_conv_b, v_ssm_A_log, v_ssm_dt_bias, v_ssm_D, v_ssm_norm, v_c_q_norm, v_c_k_norm, v_w_oa, v_w_ob, v_w_oc, v_w_out, v_ffn_w_up, v_ffn_w_gate, v_ffn_conv_w, v_ffn_conv_b, v_ffn_w_down, v_final_norm):
    args = (x, c, ctx, c_ctx, w_mod, b_mod, norm1, norm2, w_in, a_sink, ssm_conv_w, ssm_conv_b, ssm_A_log, ssm_dt_bias, ssm_D, ssm_norm, c_q_norm, c_k_norm, w_oa, w_ob, w_oc, w_out, ffn_w_up, ffn_w_gate, ffn_conv_w, ffn_conv_b, ffn_w_down, final_norm, loss_target)
    moms = (m_c_ctx, m_w_mod, m_b_mod, m_norm1, m_norm2, m_w_in, m_a_sink, m_ssm_conv_w, m_ssm_conv_b, m_ssm_A_log, m_ssm_dt_bias, m_ssm_D, m_ssm_norm, m_c_q_norm, m_c_k_norm, m_w_oa, m_w_ob, m_w_oc, m_w_out, m_ffn_w_up, m_ffn_w_gate, m_ffn_conv_w, m_ffn_conv_b, m_ffn_w_down, m_final_norm)
    vars_ = (v_c_ctx, v_w_mod, v_b_mod, v_norm1, v_norm2, v_w_in, v_a_sink, v_ssm_conv_w, v_ssm_conv_b, v_ssm_A_log, v_ssm_dt_bias, v_ssm_D, v_ssm_norm, v_c_q_norm, v_c_k_norm, v_w_oa, v_w_ob, v_w_oc, v_w_out, v_ffn_w_up, v_ffn_w_gate, v_ffn_conv_w, v_ffn_conv_b, v_ffn_w_down, v_final_norm)
    p = dict(zip(IN_NAMES, args))
    mom = dict(zip(WEIGHTS, moms))
    var = dict(zip(WEIGHTS, vars_))
    depth = w_in.shape[0]
    n_ctx = ctx.shape[1]
    xi, yi, ci = lax.axis_index("x"), lax.axis_index("y"), lax.axis_index("c")
    dev = 4 * xi + 2 * yi + ci
    big_names = list(BIG)

    def as2d(a):
        return a.reshape(-1, a.shape[-1])

    g_big = all_gather([as2d(p[n]).astype(bf16) for n in big_names], "gather_big")
    g_conv = all_gather([_to_flat(_pack(p, CONV_W))], "gather_conv")[0].reshape(8, -1)

    def full_big(n, gathered):
        shp = p[n].shape
        seg = gathered.reshape(8, *shp)
        if BIG[n] == 1:
            return jnp.concatenate([seg[j] for j in range(8)], axis=-1)
        return jnp.concatenate([seg[j] for j in range(8)], axis=1)

    big_full = {n: full_big(n, g) for n, g in zip(big_names, g_big)}
    conv_full, off = {}, 0
    for n in CONV_W:
        shp = p[n].shape
        seg = g_conv[:, off:off + p[n].size].reshape(8, *shp)
        conv_full[n] = jnp.moveaxis(seg, 0, -2).reshape(*shp[:-1], 8 * shp[-1])
        off += p[n].size
    big_layers = [{n: big_full[n][li] for n in big_names} for li in range(depth)]
    small = {n: p[n] for n in REPL}
    small.update(conv_full)

    loss, (g_layers, g_small, g_x) = jax.value_and_grad(_loss_fn, argnums=(0, 1, 2))(
        big_layers, small, x[0], ctx[0], c, loss_target[0], n_ctx)
    loss = lax.psum(loss, AXES)

    def send_rows(n):
        shp = p[n].shape
        b = shp[-1] if BIG[n] == 1 else shp[1]
        cut = (lambda g, j: g[:, b * j:b * (j + 1)]) if BIG[n] == 1 else (lambda g, j: g[b * j:b * (j + 1), :])
        return jnp.stack([jnp.concatenate([cut(g_layers[li][n], j) for li in range(depth)], axis=0) for j in range(8)])

    send = [send_rows(n) for n in big_names]
    from_sibling = rs_to_sibling(send)
    my_c = ci.reshape(1).astype(jnp.int32)
    side_sum = [pair_sum(s, r, my_c, "rs_pair_sum_" + n) for n, s, r in zip(big_names, send, from_sibling)]
    from_chips = rs_to_chips(side_sum)
    chip = (2 * xi + yi).reshape(1).astype(jnp.int32)
    big_out = [{}, {}, {}, {}]
    for n, s, r in zip(big_names, side_sum, from_chips):
        outs = sum_adam([(s, None, chip), (r, (0, 1, 2), None)], as2d(p[n]), as2d(mom[n]), as2d(var[n]), "adam_" + n)
        for k in range(4):
            big_out[k][n] = outs[k].reshape(p[n].shape)

    sm_names = REPL + list(CONV_W)
    g_vec = _to_flat(_pack(g_small, sm_names))
    gathered = all_gather([g_vec], "gather_small_grads")[0]
    n_repl = sum(p[n].size for n in REPL)

    def repl_flat(tree):
        return _to_flat(jnp.concatenate([_pack(tree, REPL), jnp.zeros((g_vec.size - n_repl,), f32)]))

    outs_small = sum_adam([(gathered, tuple(range(8)), None)], repl_flat(p), repl_flat(mom), repl_flat(var), "adam_small")
    g_sum = outs_small[0].reshape(-1)
    small_out = [_unpack(o.reshape(-1), p, REPL) for o in outs_small]
    conv_g_full = _unpack(g_sum[n_repl:], conv_full, CONV_W)
    conv_g = {n: lax.dynamic_slice_in_dim(conv_g_full[n], dev * p[n].shape[-1], p[n].shape[-1], axis=2) for n in CONV_W}
    conv_gv = _to_flat(_pack(conv_g, CONV_W))
    outs_conv = sum_adam([(conv_gv[None], (0,), None)], _to_flat(_pack(p, CONV_W)), _to_flat(_pack(mom, CONV_W)),
                         _to_flat(_pack(var, CONV_W)), "adam_conv")
    conv_out = [_unpack(o.reshape(-1), p, CONV_W) for o in outs_conv]

    res = []
    for k in range(4):
        tree = {**big_out[k], **small_out[k], **conv_out[k]}
        res.append([tree[n] for n in WEIGHTS])
    return (loss, g_x[None], *res[0], *res[1], *res[2], *res[3])
```

```python
import functools

import jax
import jax.numpy as jnp
from jax import lax
from jax.experimental import pallas as pl
from jax.experimental.pallas import tpu as pltpu

f32 = jnp.float32
bf16 = jnp.bfloat16
MESH = pl.DeviceIdType.MESH
AXES = ("x", "y", "c")

GRID_W = 64
HEAD_DIM = 64
ROPE_BASE = 10000.0
EPS = 1e-6
WINDOW = 128
N_HEADS = 8
N_KV = 2
SSM_HEADS = 16
SSM_P = 64
SSM_G = 2
SSM_N = 128
SSM_INNER = SSM_HEADS * SSM_P
SSM_BC = SSM_G * SSM_N
SSM_Q = 128
Q_W = N_HEADS * HEAD_DIM
KV_W = N_KV * HEAD_DIM
DT_W = 2 * SSM_HEADS
DT_PAD = 128
ADAM_LR, ADAM_B1, ADAM_B2, ADAM_EPS, ADAM_WD, ADAM_STEP = 0.001, 0.9, 0.999, 1e-08, 0.01, 10

LANES = 128
ROW_TILE = 256
VMEM_BLOCK_BUDGET = 6 * 1024 * 1024
ATTN_SLAB = 128
MM_ROW_CAP = 1088
MM_TILE_CAP = 1536
NEG = -1e30

IN_NAMES = ['x', 'c', 'ctx', 'c_ctx', 'w_mod', 'b_mod', 'norm1', 'norm2', 'w_in', 'a_sink', 'ssm_conv_w', 'ssm_conv_b', 'ssm_A_log', 'ssm_dt_bias', 'ssm_D', 'ssm_norm', 'c_q_norm', 'c_k_norm', 'w_oa', 'w_ob', 'w_oc', 'w_out', 'ffn_w_up', 'ffn_w_gate', 'ffn_conv_w', 'ffn_conv_b', 'ffn_w_down', 'final_norm', 'loss_target']
WEIGHTS = IN_NAMES[3:28]
BIG = {'w_mod': 1, 'w_in': 1, 'w_oa': 1, 'w_ob': 0, 'w_oc': 1, 'w_out': 0, 'ffn_w_up': 1, 'ffn_w_gate': 1, 'ffn_w_down': 0}
CONV_W = ('ssm_conv_w', 'ffn_conv_w')
REPL = [n for n in WEIGHTS if n not in BIG and n not in CONV_W]

NT = (((1,), (1,)), ((), ()))
TN = (((0,), (0,)), ((), ()))
NN = (((1,), (0,)), ((), ()))


def _cparams(*sem):
    return pltpu.CompilerParams(dimension_semantics=sem)


def _div_tile(n, unit, cap):
    for d in range(min(n, int(cap)), 0, -1):
        if n % d == 0 and d % unit == 0:
            return d
    return n


def _row_tile(m, row_bytes):
    return _div_tile(m, 16, max(16, VMEM_BLOCK_BUDGET // row_bytes))


def _mm_call(a, b, mode, out_dtype, name):
    if mode == "nn":
        (m, k), n = a.shape, b.shape[1]
    elif mode == "nt":
        (m, k), n = a.shape, b.shape[0]
    else:
        (k, m), n = a.shape, b.shape[1]
    dims = {"nn": NN, "nt": NT, "tn": TN}[mode]
    ia, ib = a.dtype.itemsize, b.dtype.itemsize
    tm = _div_tile(m, LANES, MM_TILE_CAP) if mode == "tn" else _div_tile(m, 16, MM_ROW_CAP)
    tn = _div_tile(n, LANES, min(MM_TILE_CAP, VMEM_BLOCK_BUDGET // (4 * tm)))
    tk = _div_tile(k, 16 if mode == "tn" else LANES,
                   min(MM_ROW_CAP if mode == "tn" else MM_TILE_CAP, VMEM_BLOCK_BUDGET // (tm * ia), VMEM_BLOCK_BUDGET // (tn * ib)))
    nk = k // tk

    def body(a_ref, b_ref, o_ref, *acc):
        part = lax.dot_general(a_ref[...].astype(bf16), b_ref[...].astype(bf16), dims, preferred_element_type=f32)
        if nk == 1:
            o_ref[...] = part.astype(o_ref.dtype)
            return
        kk = pl.program_id(2)

        @pl.when(kk == 0)
        def _():
            acc[0][...] = part

        @pl.when(kk > 0)
        def _():
            acc[0][...] += part

        @pl.when(kk == nk - 1)
        def _():
            o_ref[...] = acc[0][...].astype(o_ref.dtype)

    a_spec = pl.BlockSpec((tk, tm), lambda i, j, kk: (kk, i)) if mode == "tn" else pl.BlockSpec((tm, tk), lambda i, j, kk: (i, kk))
    b_spec = pl.BlockSpec((tn, tk), lambda i, j, kk: (j, kk)) if mode == "nt" else pl.BlockSpec((tk, tn), lambda i, j, kk: (kk, j))
    return pl.pallas_call(
        body, grid=(m // tm, n // tn, nk), in_specs=[a_spec, b_spec],
        out_specs=pl.BlockSpec((tm, tn), lambda i, j, kk: (i, j)),
        out_shape=jax.ShapeDtypeStruct((m, n), out_dtype),
        scratch_shapes=[pltpu.VMEM((tm, tn), f32)] if nk > 1 else [], name=name,
        compiler_params=_cparams("parallel", "parallel", "arbitrary"))(a, b)


def mm(a, b, name, out_dtype=None):
    @jax.custom_vjp
    def op(a, b):
        return _mm_call(a, b, "nn", out_dtype or bf16, name)

    def fwd(a, b):
        return op(a, b), (a, b)

    def bwd(res, g):
        a, b = res
        return _mm_call(g, b, "nt", a.dtype, name + "_da"), _mm_call(a, g, "tn", b.dtype, name + "_db")

    op.defvjp(fwd, bwd)
    return op(a, b)


def split_cols(u, widths):
    offs = [0]
    for w in widths:
        offs.append(offs[-1] + w)

    @jax.custom_vjp
    def op(u):
        return tuple(u[:, offs[i]:offs[i + 1]] for i in range(len(widths)))

    def fwd(u):
        return op(u), None

    def bwd(_, cts):
        return (jnp.concatenate(cts, axis=1),)

    op.defvjp(fwd, bwd)
    return op(u)


def rowwise(fn, rows, consts, pars, out_widths, out_dtypes, name):
    t = rows[0].shape[0]
    tm = ROW_TILE
    nb = t // tm
    nr, nc, npar = len(rows), len(consts), len(pars)

    def rspec(a):
        return pl.BlockSpec((tm, a.shape[1]), lambda i: (i, 0))

    def pspec(a):
        return pl.BlockSpec(a.shape, lambda i: (0,) * a.ndim)

    def call_fwd(rows, consts, pars):
        def body(*refs):
            blk = pl.program_id(0)
            ins = [r[...].astype(f32) for r in refs[:nr + nc]]
            ps = [r[...] for r in refs[nr + nc:nr + nc + npar]]
            outs = fn(blk, *ins, *ps)
            for o_ref, o in zip(refs[nr + nc + npar:], outs):
                o_ref[...] = o.astype(o_ref.dtype)

        return pl.pallas_call(
            body, grid=(nb,),
            in_specs=[rspec(a) for a in rows + consts] + [pspec(a) for a in pars],
            out_specs=[pl.BlockSpec((tm, w), lambda i: (i, 0)) for w in out_widths],
            out_shape=[jax.ShapeDtypeStruct((t, w), d) for w, d in zip(out_widths, out_dtypes)],
            name=name, compiler_params=_cparams("parallel"))(*rows, *consts, *pars)

    def call_bwd(rows, consts, pars, cts):
        nout = len(cts)

        def body(*refs):
            blk = pl.program_id(0)
            ins = [r[...].astype(f32) for r in refs[:nr]]
            cs = [r[...].astype(f32) for r in refs[nr:nr + nc]]
            ps = [r[...] for r in refs[nr + nc:nr + nc + npar]]
            dys = [r[...].astype(f32) for r in refs[nr + nc + npar:nr + nc + npar + nout]]
            d_refs = refs[nr + nc + npar + nout:]
            _, vjp = jax.vjp(lambda *a: tuple(fn(blk, *a[:nr], *cs, *a[nr:])), *ins, *ps)
            grads = vjp(tuple(dys))
            for d_ref, g in zip(d_refs[:nr], grads[:nr]):
                d_ref[...] = g.astype(d_ref.dtype)
            if npar:
                @pl.when(blk == 0)
                def _():
                    for d_ref in d_refs[nr:]:
                        d_ref[...] = jnp.zeros_like(d_ref)

                for d_ref, g in zip(d_refs[nr:], grads[nr:]):
                    d_ref[...] += g

        return pl.pallas_call(
            body, grid=(nb,),
            in_specs=[rspec(a) for a in rows + consts] + [pspec(a) for a in pars] + [rspec(a) for a in cts],
            out_specs=[rspec(a) for a in rows] + [pspec(a) for a in pars],
            out_shape=[jax.ShapeDtypeStruct(a.shape, a.dtype) for a in rows + pars],
            name=name + "_bwd", compiler_params=_cparams("arbitrary"))(*rows, *consts, *pars, *cts)

    @jax.custom_vjp
    def op(rows, consts, pars):
        return tuple(call_fwd(list(rows), list(consts), list(pars)))

    def fwd(rows, consts, pars):
        return op(rows, consts, pars), (rows, consts, pars)

    def bwd(res, cts):
        rows, consts, pars = res
        g = call_bwd(list(rows), list(consts), list(pars), list(cts))
        return tuple(g[:nr]), tuple(jnp.zeros_like(a) for a in consts), tuple(g[nr:])

    op.defvjp(fwd, bwd)
    return op(tuple(rows), tuple(consts), tuple(pars))


def colwise(fn, cols, pars, out_dtype, name):
    t, w = cols[0].shape
    tc = LANES
    nb = w // tc
    ncol, npar = len(cols), len(pars)

    def cspec(a):
        return pl.BlockSpec((a.shape[0], tc), lambda j: (0, j))

    def call_fwd(cols, pars):
        def body(*refs):
            ins = [r[...].astype(f32) for r in refs[:ncol]]
            ps = [r[...] for r in refs[ncol:ncol + npar]]
            refs[-1][...] = fn(*ins, *ps).astype(refs[-1].dtype)

        return pl.pallas_call(
            body, grid=(nb,), in_specs=[cspec(a) for a in cols + pars], out_specs=cspec(cols[0]),
            out_shape=jax.ShapeDtypeStruct((t, w), out_dtype), name=name, compiler_params=_cparams("parallel"))(*cols, *pars)

    def call_bwd(cols, pars, ct):
        def body(*refs):
            ins = [r[...].astype(f32) for r in refs[:ncol]]
            ps = [r[...] for r in refs[ncol:ncol + npar]]
            dy = refs[ncol + npar][...].astype(f32)
            d_refs = refs[ncol + npar + 1:]
            _, vjp = jax.vjp(fn, *ins, *ps)
            grads = vjp(dy)
            for d_ref, g in zip(d_refs, grads):
                d_ref[...] = g.astype(d_ref.dtype)

        return pl.pallas_call(
            body, grid=(nb,), in_specs=[cspec(a) for a in cols + pars + [ct]],
            out_specs=[cspec(a) for a in cols + pars],
            out_shape=[jax.ShapeDtypeStruct(a.shape, a.dtype) for a in cols + pars],
            name=name + "_bwd", compiler_params=_cparams("parallel"))(*cols, *pars, ct)

    @jax.custom_vjp
    def op(cols, pars):
        return call_fwd(list(cols), list(pars))

    def fwd(cols, pars):
        return op(cols, pars), (cols, pars)

    def bwd(res, ct):
        cols, pars = res
        g = call_bwd(list(cols), list(pars), ct)
        return tuple(g[:ncol]), tuple(g[ncol:])

    op.defvjp(fwd, bwd)
    return op(tuple(cols), tuple(pars))


def _sigmoid(x):
    return 1.0 / (1.0 + jnp.exp(-x))


def _silu(x):
    return x * _sigmoid(x)


def _rms(x, g):
    return x * lax.rsqrt(jnp.mean(x * x, axis=-1, keepdims=True) + EPS) * g


def _shift_rows(u, k, n_ctx):
    @jax.custom_vjp
    def op(u):
        t = u.shape[0]
        row = lax.broadcasted_iota(jnp.int32, u.shape, 0)
        edge = ((row == 0) | (row == n_ctx)) if k == 1 else ((row == n_ctx - 1) | (row == t - 1))
        return jnp.where(edge, 0.0, pltpu.roll(u, k % t, 0))

    op.defvjp(lambda u: (op(u), None), lambda _, g: (_shift_rows(g, -k, n_ctx),))
    return op(u)


def _dwconv(u, w0, w1, w2, b, n_ctx):
    return w0 * _shift_rows(u, 1, n_ctx) + w1 * u + w2 * _shift_rows(u, -1, n_ctx) + b


@jax.custom_vjp
def _rot_half(x):
    w = x.shape[1]
    lane = lax.broadcasted_iota(jnp.int32, x.shape, 1)
    return jnp.where((lane % HEAD_DIM) < HEAD_DIM // 2, pltpu.roll(x, w - HEAD_DIM // 2, 1), pltpu.roll(x, HEAD_DIM // 2, 1))


_rot_half.defvjp(lambda x: (_rot_half(x), None), lambda _, g: (_rot_half(g),))


def _head_rms(x, g):
    w = x.shape[1]
    same = (lax.broadcasted_iota(jnp.int32, (w, w), 0) // HEAD_DIM) == (lax.broadcasted_iota(jnp.int32, (w, w), 1) // HEAD_DIM)
    ms = jnp.dot(x * x, same.astype(f32), precision=lax.Precision.HIGHEST, preferred_element_type=f32) * (1.0 / HEAD_DIM)
    return x * lax.rsqrt(ms + EPS) * g


def _band_ok(i, j, c0, shape, tq, tk):
    kpos = j * tk + lax.broadcasted_iota(jnp.int32, shape, 0)
    qpos = i * tq + (c0 + lax.broadcasted_iota(jnp.int32, shape, 1)) % tq
    return jnp.abs(qpos - kpos) <= WINDOW


def _kv_range(i, nb, window):
    is_ctx = i == 0
    if window:
        return jnp.where(is_ctx, 1, jnp.maximum(i - 1, 1)), jnp.where(is_ctx, 1, jnp.minimum(i + 2, nb))
    return 1, jnp.where(is_ctx, 1, nb)


def _sink_row(sink_ref, g, r, tq):
    return jnp.concatenate([jnp.full((1, tq), sink_ref[g * r + h], f32) for h in range(r)], axis=1)


def _attn_fwd_call(q, k, v, sink, window, name, shards=()):
    h, t, dh = q.shape
    nkv = k.shape[0]
    r = h // nkv
    tq = tk = ROW_TILE
    nb = t // tq
    rows = r * tq
    ns = len(shards)

    assert window or nb % 2 == 1, "the dense schedule takes the kv chunks after the context chunk in pairs"

    def body(sink_ref, q_ref, k_ref, v_ref, *rest):
        x_refs, (o_ref, lse_ref), gathered_refs = rest[:ns], rest[ns:ns + 2], rest[ns + 2:2 * ns + 2]
        m_scr, l_scr, acc_scr, s_a, s_b, p_a, p_b, a_a, a_b = rest[2 * ns + 2:2 * ns + 11]
        comm_sems = rest[2 * ns + 11:]
        g, i = pl.program_id(0), pl.program_id(1)
        if ns:
            @pl.when((g == 0) & (i == 0))
            def _():
                _gather_steps(x_refs, gathered_refs, *comm_sems)[0]()

        qv = q_ref[...].reshape(rows, dh)
        m_scr[...] = jnp.full_like(m_scr, NEG)
        l_scr[...] = jnp.zeros_like(l_scr)
        acc_scr[...] = jnp.zeros_like(acc_scr)

        def kv_rows(j):
            return pl.ds(pl.multiple_of(jnp.minimum(j, nb - 1) * tk, tk), tk)

        def scores(j, s_scr):
            s_scr[...] = lax.dot_general(k_ref[0, kv_rows(j), :], qv, NT, preferred_element_type=f32)

        def softmax(j, s_scr, p_scr, a_scr, masked):
            for cb in range(rows // ATTN_SLAB):
                cs = slice(cb * ATTN_SLAB, (cb + 1) * ATTN_SLAB)
                s = s_scr[:, cs]
                if masked:
                    s = jnp.where(_band_ok(i, j, cb * ATTN_SLAB, s.shape, tq, tk), s, NEG)
                m = m_scr[:, cs]
                m2 = jnp.maximum(m, jnp.max(s, axis=0, keepdims=True))
                p = jnp.exp(s - m2)
                a = jnp.exp(m - m2)
                l_scr[:, cs] = a * l_scr[:, cs] + jnp.sum(p, axis=0, keepdims=True)
                m_scr[:, cs] = m2
                a_scr[:, cs] = a
                p_scr[:, cs] = p.astype(bf16)

        def weighted_v(j, p_scr, a_scr):
            acc_scr[...] = a_scr[...] * acc_scr[...] + lax.dot_general(v_ref[0, kv_rows(j), :], p_scr[...], TN, preferred_element_type=f32)

        scores(0, s_a)
        softmax(0, s_a, p_a, a_a, False)
        if window:
            weighted_v(0, p_a, a_a)
            lo, hi = _kv_range(i, nb, window)

            def chunk(j, c):
                scores(j, s_a)
                softmax(j, s_a, p_a, a_a, True)
                weighted_v(j, p_a, a_a)
                return c

            lax.fori_loop(lo, hi, chunk, 0)
        else:
            scores(1, s_b)

            def pair(tt, c):
                j0 = 2 * tt + 1
                scores(j0 + 1, s_a)
                weighted_v(j0 - 1, p_a, a_a)
                softmax(j0, s_b, p_b, a_b, False)
                scores(j0 + 2, s_b)
                weighted_v(j0, p_b, a_b)
                softmax(j0 + 1, s_a, p_a, a_a, False)
                return c

            lax.fori_loop(0, jnp.where(i == 0, 0, (nb - 1) // 2), pair, 0)
            weighted_v(jnp.where(i == 0, 0, nb - 1), p_a, a_a)
        m, l, acc = m_scr[...], l_scr[...], acc_scr[...]
        if window:
            sk = _sink_row(sink_ref, g, r, tq)
            m2 = jnp.maximum(m, sk)
            a = jnp.exp(m - m2)
            l = a * l + jnp.exp(sk - m2)
            acc = a * acc
            m = m2
        o_ref[...] = (acc / l).T.reshape(r, tq, dh).astype(o_ref.dtype)
        lse_ref[0] = m + jnp.log(l)
        if ns:
            @pl.when((g == nkv - 1) & (i == nb - 1))
            def _():
                _gather_steps(x_refs, gathered_refs, *comm_sems)[1]()

    qspec = pl.BlockSpec((r, tq, dh), lambda g, i: (g, i, 0))
    kspec = pl.BlockSpec((1, t, dh), lambda g, i: (g, 0, 0))
    hbm = pl.BlockSpec(memory_space=pl.ANY)
    sem = ("arbitrary", "arbitrary") if ns else ("parallel", "parallel")
    return pl.pallas_call(
        body, grid=(nkv, nb),
        in_specs=[pl.BlockSpec(memory_space=pltpu.SMEM), qspec, kspec, kspec] + [hbm] * ns,
        out_specs=[qspec, pl.BlockSpec((1, 1, rows), lambda g, i: (g * nb + i, 0, 0))] + [hbm] * ns,
        out_shape=[jax.ShapeDtypeStruct((h, t, dh), bf16), jax.ShapeDtypeStruct((nkv * nb, 1, rows), f32)]
        + [jax.ShapeDtypeStruct((8,) + s.shape, s.dtype) for s in shards],
        scratch_shapes=[pltpu.VMEM((1, rows), f32), pltpu.VMEM((1, rows), f32), pltpu.VMEM((dh, rows), f32),
                        pltpu.VMEM((tk, rows), f32), pltpu.VMEM((tk, rows), f32), pltpu.VMEM((tk, rows), bf16),
                        pltpu.VMEM((tk, rows), bf16), pltpu.VMEM((1, rows), f32), pltpu.VMEM((1, rows), f32)]
        + (_gather_scratch(ns) if ns else []),
        name=name, compiler_params=_cparams(*sem))(sink, q, k, v, *shards)


def _attn_bwd_call(q, k, v, sink, o, lse, do, window, name):
    h, t, dh = q.shape
    nkv = k.shape[0]
    r = h // nkv
    tq = tk = ROW_TILE
    nb = t // tq
    rows = r * tq

    def body(sink_ref, q_ref, k_ref, v_ref, o_ref, lse_ref, do_ref, dq_ref, dk_ref, dv_ref, dsink_ref,
             s_scr, dp_scr, p_scr, ds_scr, dq_scr):
        g, i = pl.program_id(0), pl.program_id(1)

        @pl.when(i == 0)
        def _():
            dk_ref[...] = jnp.zeros_like(dk_ref)
            dv_ref[...] = jnp.zeros_like(dv_ref)

        qv = q_ref[...].reshape(rows, dh)
        dov = do_ref[...].reshape(rows, dh)
        lse_t = lse_ref[0]
        delta_t = jnp.sum((dov.astype(f32) * o_ref[...].reshape(rows, dh).astype(f32)).T, axis=0, keepdims=True)
        dq_scr[...] = jnp.zeros_like(dq_scr)

        def chunk(j, masked):
            kv_rows = pl.ds(pl.multiple_of(j * tk, tk), tk)
            kj = k_ref[0, kv_rows, :]
            vj = v_ref[0, kv_rows, :]
            s_scr[...] = lax.dot_general(kj, qv, NT, preferred_element_type=f32)
            dp_scr[...] = lax.dot_general(vj, dov, NT, preferred_element_type=f32)
            for cb in range(rows // ATTN_SLAB):
                cs = slice(cb * ATTN_SLAB, (cb + 1) * ATTN_SLAB)
                s = s_scr[:, cs]
                if masked:
                    s = jnp.where(_band_ok(i, j, cb * ATTN_SLAB, s.shape, tq, tk), s, NEG)
                p = jnp.exp(s - lse_t[:, cs])
                p_scr[:, cs] = p.astype(bf16)
                ds_scr[:, cs] = (p * (dp_scr[:, cs] - delta_t[:, cs])).astype(bf16)
            dv_ref[0, kv_rows, :] += jnp.dot(p_scr[...], dov, preferred_element_type=f32)
            dk_ref[0, kv_rows, :] += jnp.dot(ds_scr[...], qv, preferred_element_type=f32)
            dq_scr[...] += lax.dot_general(kj, ds_scr[...], TN, preferred_element_type=f32)

        chunk(0, False)
        lo, hi = _kv_range(i, nb, window)
        lax.fori_loop(lo, hi, lambda j, c: (chunk(j, window), c)[1], 0)
        dq_ref[...] = dq_scr[...].T.reshape(r, tq, dh).astype(dq_ref.dtype)
        if window:
            dsink_ref[0] = -jnp.exp(_sink_row(sink_ref, g, r, tq) - lse_t) * delta_t
        else:
            dsink_ref[...] = jnp.zeros_like(dsink_ref)

    qspec = pl.BlockSpec((r, tq, dh), lambda g, i: (g, i, 0))
    cspec = pl.BlockSpec((1, 1, rows), lambda g, i: (g * nb + i, 0, 0))
    kspec = pl.BlockSpec((1, t, dh), lambda g, i: (g, 0, 0))
    return pl.pallas_call(
        body, grid=(nkv, nb),
        in_specs=[pl.BlockSpec(memory_space=pltpu.SMEM), qspec, kspec, kspec, qspec, cspec, qspec],
        out_specs=[qspec, kspec, kspec, cspec],
        out_shape=[jax.ShapeDtypeStruct((h, t, dh), bf16), jax.ShapeDtypeStruct(k.shape, f32), jax.ShapeDtypeStruct(v.shape, f32),
                   jax.ShapeDtypeStruct((nkv * nb, 1, rows), f32)],
        scratch_shapes=[pltpu.VMEM((tk, rows), f32), pltpu.VMEM((tk, rows), f32), pltpu.VMEM((tk, rows), bf16),
                        pltpu.VMEM((tk, rows), bf16), pltpu.VMEM((dh, rows), f32)],
        name=name, compiler_params=_cparams("parallel", "arbitrary"))(sink, q, k, v, o, lse, do)


def attention(q, k, v, sink, window, name, shards=(), stand_ins=()):
    @jax.custom_vjp
    def op(q, k, v, sink, shards, stand_ins):
        o, _, *gathered = _attn_fwd_call(q, k, v, sink, window, name, shards)
        return o, tuple(gathered)

    def fwd(q, k, v, sink, shards, stand_ins):
        o, lse, *gathered = _attn_fwd_call(q, k, v, sink, window, name, shards)
        return (o, tuple(gathered)), (q, k, v, sink, o, lse, shards)

    def bwd(res, cts):
        q, k, v, sink, o, lse, shards = res
        do, d_gathered = cts
        dq, dk, dv, dsink_rows = _attn_bwd_call(q, k, v, sink, o, lse, do, window, name + "_bwd")
        nkv, r = k.shape[0], q.shape[0] // k.shape[0]
        dsink = jnp.sum(dsink_rows.reshape(nkv, -1, r, ROW_TILE), axis=(1, 3)).reshape(nkv * r)
        return dq, dk.astype(k.dtype), dv.astype(v.dtype), dsink, tuple(jnp.zeros_like(s) for s in shards), tuple(d_gathered)

    op.defvjp(fwd, bwd)
    return op(q, k, v, sink, tuple(shards), tuple(stand_ins))


def _ssd_pair(xs_p, dtx_p, dtr1, dtr2, ac1, ac2, bg, cg, hin_p, rev):
    q = xs_p.shape[0]
    ii = lax.broadcasted_iota(jnp.int32, (q, q), 0)
    jj = lax.broadcasted_iota(jnp.int32, (q, q), 1)
    tri = (ii <= jj) if rev else (ii >= jj)
    lo = lax.broadcasted_iota(jnp.int32, (q, LANES), 1) < SSM_P
    lo_row = lax.broadcasted_iota(jnp.int32, (1, LANES), 1) < SSM_P

    def cums(dtr, ac):
        a = dtr * ac
        c = jnp.sum(jnp.where(tri, jnp.broadcast_to(a, (q, q)), 0.0), axis=1, keepdims=True)
        return c, jnp.sum(a, axis=1, keepdims=True)

    def lmat(c):
        cf = jnp.broadcast_to(c, (q, q))
        return jnp.where(tri, jnp.exp(jnp.minimum(cf - cf.T, 0.0)), 0.0)

    c1, t1 = cums(dtr1, ac1)
    c2, t2 = cums(dtr2, ac2)
    cb = lax.dot_general(cg.astype(bf16), bg.astype(bf16), NT, preferred_element_type=f32)
    m = jnp.concatenate([cb * lmat(c1), cb * lmat(c2)], axis=1)
    xdt = xs_p * dtx_p
    x2 = jnp.concatenate([jnp.where(lo, xdt, 0.0), jnp.where(lo, 0.0, xdt)], axis=0)
    y_diag = jnp.dot(m.astype(bf16), x2.astype(bf16), preferred_element_type=f32)
    csel = jnp.where(lo, jnp.broadcast_to(c1, (q, LANES)), jnp.broadcast_to(c2, (q, LANES)))
    tsel = jnp.where(lo_row, jnp.broadcast_to(t1, (1, LANES)), jnp.broadcast_to(t2, (1, LANES)))
    st = lax.dot_general(bg.astype(bf16), (xdt * jnp.exp(tsel - csel)).astype(bf16), TN, preferred_element_type=f32)
    y_off = jnp.dot(cg.astype(bf16), hin_p.astype(bf16), preferred_element_type=f32) * jnp.exp(csel)
    return y_diag + y_off, hin_p * jnp.exp(tsel) + st


def _ssd_order(s, nc, ncc, rev):
    if not rev:
        return s
    return jnp.where(s < ncc, ncc - 1 - s, nc - 1 - (s - ncc))


def _ssd_pair_slices(j):
    return slice(LANES * j, LANES * (j + 1)), 2 * j, 2 * j + 1, (2 * j) // (SSM_HEADS // SSM_G)


def _head_lanes(w, transpose=False):
    shape = (w, SSM_HEADS) if transpose else (SSM_HEADS, w)
    head = lax.broadcasted_iota(jnp.int32, shape, 1 if transpose else 0)
    lane = lax.broadcasted_iota(jnp.int32, shape, 0 if transpose else 1)
    return (lane // SSM_P == head).astype(f32)


def _ssd_fwd_call(xs, dt, dtr, bm, cm, acol, rev, n_ctx, name):
    t, w = xs.shape
    q = SSM_Q
    nc, ncc = t // q, n_ctx // q

    def body(xs_ref, dt_ref, dtr_ref, b_ref, c_ref, a_ref, y_ref, hin_ref, h_scr):
        @pl.when(pl.program_id(0) == 0)
        def _():
            h_scr[...] = jnp.zeros_like(h_scr)

        hin_ref[0] = h_scr[...]
        dtx = jnp.dot(dt_ref[...], _head_lanes(w), precision=lax.Precision.HIGHEST, preferred_element_type=f32)
        for j in range(SSM_HEADS // 2):
            sl, h1, h2, g = _ssd_pair_slices(j)
            gs = slice(SSM_N * g, SSM_N * (g + 1))
            y, hout = _ssd_pair(xs_ref[:, sl], dtx[:, sl], dtr_ref[h1:h1 + 1, :], dtr_ref[h2:h2 + 1, :],
                                a_ref[h1:h1 + 1, :], a_ref[h2:h2 + 1, :], b_ref[:, gs], c_ref[:, gs], h_scr[:, sl], rev)
            y_ref[:, sl] = y.astype(y_ref.dtype)
            h_scr[:, sl] = hout

    def at(s):
        return _ssd_order(s, nc, ncc, rev)

    return pl.pallas_call(
        body, grid=(nc,),
        in_specs=[pl.BlockSpec((q, w), lambda s: (at(s), 0)), pl.BlockSpec((q, SSM_HEADS), lambda s: (at(s), 0)),
                  pl.BlockSpec((SSM_HEADS, q), lambda s: (0, at(s))),
                  pl.BlockSpec((q, SSM_BC), lambda s: (at(s), 0)), pl.BlockSpec((q, SSM_BC), lambda s: (at(s), 0)),
                  pl.BlockSpec((SSM_HEADS, 1), lambda s: (0, 0))],
        out_specs=[pl.BlockSpec((q, w), lambda s: (at(s), 0)), pl.BlockSpec((1, SSM_N, w), lambda s: (s, 0, 0))],
        out_shape=[jax.ShapeDtypeStruct((t, w), xs.dtype), jax.ShapeDtypeStruct((nc, SSM_N, w), f32)],
        scratch_shapes=[pltpu.VMEM((SSM_N, w), f32)],
        name=name, compiler_params=_cparams("arbitrary"))(xs, dt, dtr, bm, cm, acol)


def _ssd_bwd_call(xs, dt, dtr, bm, cm, acol, hin, dy, rev, n_ctx, name):
    t, w = xs.shape
    q = SSM_Q
    nc, ncc = t // q, n_ctx // q

    def body(xs_ref, dt_ref, dtr_ref, b_ref, c_ref, a_ref, hin_ref, dy_ref,
             dxs_ref, ddt_ref, ddtr_ref, db_ref, dc_ref, da_ref, dh_scr):
        @pl.when(pl.program_id(0) == 0)
        def _():
            dh_scr[...] = jnp.zeros_like(dh_scr)
            da_ref[...] = jnp.zeros_like(da_ref)

        dtx = jnp.dot(dt_ref[...], _head_lanes(w), precision=lax.Precision.HIGHEST, preferred_element_type=f32)
        ddtx = []
        db = [None] * SSM_G
        dc = [None] * SSM_G
        for j in range(SSM_HEADS // 2):
            sl, h1, h2, g = _ssd_pair_slices(j)
            gs = slice(SSM_N * g, SSM_N * (g + 1))
            _, vjp = jax.vjp(
                functools.partial(_ssd_pair, rev=rev),
                xs_ref[:, sl].astype(f32), dtx[:, sl], dtr_ref[h1:h1 + 1, :], dtr_ref[h2:h2 + 1, :],
                a_ref[h1:h1 + 1, :], a_ref[h2:h2 + 1, :], b_ref[:, gs].astype(f32), c_ref[:, gs].astype(f32), hin_ref[0, :, sl])
            gr = vjp((dy_ref[:, sl].astype(f32), dh_scr[:, sl]))
            dxs_ref[:, sl] = gr[0].astype(dxs_ref.dtype)
            ddtx.append(gr[1])
            ddtr_ref[h1:h1 + 1, :] = gr[2]
            ddtr_ref[h2:h2 + 1, :] = gr[3]
            da_ref[h1:h1 + 1, :] += gr[4]
            da_ref[h2:h2 + 1, :] += gr[5]
            db[g] = gr[6] if db[g] is None else db[g] + gr[6]
            dc[g] = gr[7] if dc[g] is None else dc[g] + gr[7]
            dh_scr[:, sl] = gr[8]
        for g in range(SSM_G):
            gs = slice(SSM_N * g, SSM_N * (g + 1))
            db_ref[:, gs] = db[g].astype(db_ref.dtype)
            dc_ref[:, gs] = dc[g].astype(dc_ref.dtype)
        ddt_ref[...] = jnp.dot(jnp.concatenate(ddtx, axis=1), _head_lanes(w, transpose=True),
                               precision=lax.Precision.HIGHEST, preferred_element_type=f32)

    def step(s):
        return nc - 1 - s

    def at(s):
        return _ssd_order(step(s), nc, ncc, rev)

    row = lambda wd: pl.BlockSpec((q, wd), lambda s: (at(s), 0))
    dtr_spec = pl.BlockSpec((SSM_HEADS, q), lambda s: (0, at(s)))
    a_spec = pl.BlockSpec((SSM_HEADS, 1), lambda s: (0, 0))
    return pl.pallas_call(
        body, grid=(nc,),
        in_specs=[row(w), row(SSM_HEADS), dtr_spec, row(SSM_BC), row(SSM_BC), a_spec,
                  pl.BlockSpec((1, SSM_N, w), lambda s: (step(s), 0, 0)), row(w)],
        out_specs=[row(w), row(SSM_HEADS), dtr_spec, row(SSM_BC), row(SSM_BC), a_spec],
        out_shape=[jax.ShapeDtypeStruct((t, w), xs.dtype), jax.ShapeDtypeStruct(dt.shape, f32), jax.ShapeDtypeStruct(dtr.shape, f32),
                   jax.ShapeDtypeStruct(bm.shape, bm.dtype), jax.ShapeDtypeStruct(cm.shape, cm.dtype), jax.ShapeDtypeStruct(acol.shape, f32)],
        scratch_shapes=[pltpu.VMEM((SSM_N, w), f32)],
        name=name, compiler_params=_cparams("arbitrary"))(xs, dt, dtr, bm, cm, acol, hin, dy)


def ssd_scan(xs, dt, dtr, bm, cm, acol, rev, n_ctx, name):
    @jax.custom_vjp
    def op(xs, dt, dtr, bm, cm, acol):
        return _ssd_fwd_call(xs, dt, dtr, bm, cm, acol, rev, n_ctx, name)[0]

    def fwd(xs, dt, dtr, bm, cm, acol):
        y, hin = _ssd_fwd_call(xs, dt, dtr, bm, cm, acol, rev, n_ctx, name)
        return y, (xs, dt, dtr, bm, cm, acol, hin)

    def bwd(res, dy):
        return tuple(_ssd_bwd_call(*res, dy, rev, n_ctx, name + "_bwd"))

    op.defvjp(fwd, bwd)
    return op(xs, dt, dtr, bm, cm, acol)


def _deinterleave(w, n_heads):
    lead = w.shape[:-1]
    return w.reshape(*lead, n_heads, HEAD_DIM // 2, 2).swapaxes(-1, -2).reshape(*lead, n_heads * HEAD_DIM)


def _interleave(w, n_heads):
    lead = w.shape[:-1]
    return w.reshape(*lead, n_heads, 2, HEAD_DIM // 2).swapaxes(-1, -2).reshape(*lead, n_heads * HEAD_DIM)


def _in_layout(d):
    sizes = [('a_q', Q_W, N_HEADS), ('a_k', KV_W, N_KV), ('a_v', KV_W, 0), ('b_z', SSM_INNER, 0),
             ('b_xbc', SSM_INNER + 2 * SSM_BC, 0), ('b_dt', DT_W, 0), ('c_q', Q_W, N_HEADS), ('c_k', KV_W, N_KV),
             ('c_v', KV_W, 0), ('g_a', d, 0), ('g_b', d, 0), ('g_c', d, 0)]
    out, start = [], 0
    for name, n, heads in sizes:
        out.append((name, start, n, heads))
        start += n
    return out


@jax.custom_vjp
def _w_in_split(w):
    d = w.shape[0]
    parts, dt = [], None
    for name, s, n, heads in _in_layout(d):
        p = w[:, s:s + n]
        if heads:
            p = _deinterleave(p, heads)
        if name == 'b_dt':
            dt = jnp.concatenate([p, jnp.zeros((d, DT_PAD - n), w.dtype)], axis=1)
        else:
            parts.append(p)
    return jnp.concatenate(parts, axis=1), dt


def _w_in_join(g_main, g_dt):
    d = g_main.shape[0]
    parts, start = [], 0
    for name, _, n, heads in _in_layout(d):
        if name == 'b_dt':
            parts.append(g_dt[:, :n])
            continue
        p = g_main[:, start:start + n]
        parts.append(_interleave(p, heads) if heads else p)
        start += n
    return jnp.concatenate(parts, axis=1)


_w_in_split.defvjp(lambda w: (_w_in_split(w), None), lambda _, g: (_w_in_join(*g),))


def _rope_tables(n_ctx, n_lat):
    rows = n_lat // GRID_W
    t_row = jnp.repeat(jnp.arange(rows), GRID_W).astype(f32)
    t_col = jnp.tile(jnp.arange(GRID_W), rows).astype(f32)
    n = HEAD_DIM // 4
    inv = ROPE_BASE ** (-jnp.arange(n, dtype=f32) / n)
    ang = jnp.concatenate([t_row[:, None] * inv, t_col[:, None] * inv], axis=-1)
    cos = jnp.concatenate([jnp.ones((n_ctx, HEAD_DIM // 2), f32), jnp.cos(ang)], axis=0)
    sin = jnp.concatenate([jnp.zeros((n_ctx, HEAD_DIM // 2), f32), jnp.sin(ang)], axis=0)
    return jnp.concatenate([cos, cos], axis=1), jnp.concatenate([-sin, sin], axis=1)


def _heads_major(a, n_heads):
    return a.reshape(a.shape[0], n_heads, HEAD_DIM).transpose(1, 0, 2)


def _heads_minor(a):
    return a.transpose(1, 0, 2).reshape(a.shape[1], a.shape[0] * HEAD_DIM)


def _layer(xall, w, s, cm, tabs, n_ctx, li, gather=((), ())):
    t, d = xall.shape
    ncb = n_ctx // ROW_TILE
    nm = f"l{li}_"
    ctq, stq, ctk, stk = tabs
    mod = [(cm[0:1, i * d:(i + 1) * d], cm[1:2, i * d:(i + 1) * d]) for i in range(6)]

    def pick(blk, pair_c, pair_l):
        return jnp.where(blk < ncb, pair_c, pair_l)

    def norm_mod(blk, x, g, sh_c, sh_l, sc_c, sc_l):
        return (_rms(x, g) * (1.0 + pick(blk, sc_c, sc_l)) + pick(blk, sh_c, sh_l),)

    (h,) = rowwise(norm_mod, [xall], [], [s['norm1'][None], *mod[0], *mod[1]], [d], [bf16], nm + "norm1")
    w_main, w_dt = _w_in_split(w['w_in'])
    u = mm(h, w_main, nm + "in")
    b_dt = mm(h, w_dt, nm + "in_dt", f32)
    a_q, a_k, a_v, b_z, b_xbc, c_q, c_k, c_v, g_a, g_b, g_c = split_cols(u, [n for name, _, n, _ in _in_layout(d) if name != 'b_dt'])

    def rope(blk, q, k, v, ct_q, st_q, ct_k, st_k):
        return q * ct_q + _rot_half(q) * st_q, k * ct_k + _rot_half(k) * st_k, v

    def norm_rope(blk, q, k, v, ct_q, st_q, ct_k, st_k, gq, gk):
        return rope(blk, _head_rms(q, gq), _head_rms(k, gk), v, ct_q, st_q, ct_k, st_k)

    qkv_w, qkv_t = [Q_W, KV_W, KV_W], [bf16, bf16, bf16]
    qa, ka, va = rowwise(rope, [a_q, a_k, a_v], [ctq, stq, ctk, stk], [], qkv_w, qkv_t, nm + "ropeA")
    gq = jnp.tile(_deinterleave(s['c_q_norm'], 1), N_HEADS)[None]
    gk = jnp.tile(_deinterleave(s['c_k_norm'], 1), N_KV)[None]
    qc, kc, vc = rowwise(norm_rope, [c_q, c_k, c_v], [ctq, stq, ctk, stk], [gq, gk], qkv_w, qkv_t, nm + "ropeC")
    ya = _heads_minor(attention(_heads_major(qa, N_HEADS), _heads_major(ka, N_KV), _heads_major(va, N_KV),
                                s['a_sink'], True, nm + "attnA")[0])
    yc, gathered = attention(_heads_major(qc, N_HEADS), _heads_major(kc, N_KV), _heads_major(vc, N_KV),
                             jnp.zeros((N_HEADS,), f32), False, nm + "attnC", *gather)
    yc = _heads_minor(yc)

    cw, cb = s['ssm_conv_w'], s['ssm_conv_b']
    conv_silu = lambda uu, w0, w1, w2, b: _silu(_dwconv(uu, w0, w1, w2, b, n_ctx))
    xbc = colwise(conv_silu, [b_xbc], [cw[0:1], cw[1:2], cw[2:3], cb[None]], bf16, nm + "ssmconv")
    xs, bm, cmat = split_cols(xbc, [SSM_INNER, SSM_BC, SSM_BC])
    bias = jnp.concatenate([s['ssm_dt_bias'].reshape(1, DT_W), jnp.zeros((1, DT_PAD - DT_W), f32)], axis=1)

    def softplus(blk, r, b):
        z = r + b
        return (jnp.maximum(z, 0.0) + jnp.log(1.0 + jnp.exp(-jnp.abs(z))),)

    (dt_all,) = rowwise(softplus, [b_dt], [], [bias], [DT_PAD], [f32], nm + "dt")
    a_coef = -jnp.exp(s['ssm_A_log'])
    ys_dir = []
    for di, rev in enumerate((False, True)):
        dt = dt_all[:, di * SSM_HEADS:(di + 1) * SSM_HEADS]
        ys_dir.append(ssd_scan(xs, dt, dt.T, bm, cmat, a_coef[di][:, None], rev, n_ctx,
                               nm + ("ssd_r" if rev else "ssd_f")))

    def ssm_out(blk, yf, yb, x, z, dskip, g):
        return (_rms((yf + yb + x * dskip) * _silu(z), g),)

    (ysn,) = rowwise(ssm_out, [ys_dir[0], ys_dir[1], xs, b_z], [], [jnp.repeat(s['ssm_D'], SSM_P)[None], s['ssm_norm'][None]],
                     [SSM_INNER], [bf16], nm + "ssmout")

    pa, pb, pc = mm(ya, w['w_oa'], nm + "oa"), mm(ysn, w['w_ob'], nm + "ob"), mm(yc, w['w_oc'], nm + "oc")

    def merge(blk, ga, gb, gc, a, b, c):
        return (_sigmoid(ga) * a + _sigmoid(gb) * b + _sigmoid(gc) * c,)

    (mrg,) = rowwise(merge, [g_a, g_b, g_c, pa, pb, pc], [], [], [d], [bf16], nm + "merge")
    o = mm(mrg, w['w_out'], nm + "out")

    def resid_norm_mod(blk, x, oo, g1_c, g1_l, g, sh_c, sh_l, sc_c, sc_l):
        x1 = x + pick(blk, g1_c, g1_l) * oo
        return x1, _rms(x1, g) * (1.0 + pick(blk, sc_c, sc_l)) + pick(blk, sh_c, sh_l)

    x1, h2 = rowwise(resid_norm_mod, [xall, o], [], [*mod[2], s['norm2'][None], *mod[3], *mod[4]], [d, d], [f32, bf16], nm + "norm2")
    up, gt = mm(h2, w['ffn_w_up'], nm + "up"), mm(h2, w['ffn_w_gate'], nm + "gate")
    fw, fb = s['ffn_conv_w'], s['ffn_conv_b']
    ffn_act = lambda g_, u_, w0, w1, w2, b: _silu(_dwconv(g_, w0, w1, w2, b, n_ctx)) * u_
    act = colwise(ffn_act, [gt, up], [fw[0:1], fw[1:2], fw[2:3], fb[None]], bf16, nm + "ffnact")
    f = mm(act, w['ffn_w_down'], nm + "down")

    def resid(blk, x, ff, g2_c, g2_l):
        return (x + pick(blk, g2_c, g2_l) * ff,)

    (x2,) = rowwise(resid, [x1, f], [], [*mod[5]], [d], [f32], nm + "resid")
    return x2, gathered


def _assemble(name, gathered):
    return jnp.concatenate([gathered[j] for j in range(8)], axis=1 if BIG[name] == 1 else 0)


def _loss_fn(big0, shards1, stand_ins1, small, x, ctx, c, target, n_ctx):
    n_lat, d = x.shape
    xall = jnp.concatenate([ctx, x], axis=0)
    ct, st = _rope_tables(n_ctx, n_lat)
    tabs = (jnp.tile(ct, (1, N_HEADS)) * HEAD_DIM ** -0.5, jnp.tile(st, (1, N_HEADS)) * HEAD_DIM ** -0.5,
            jnp.tile(ct, (1, N_KV)), jnp.tile(st, (1, N_KV)))
    srows = jnp.concatenate([_silu(small['c_ctx'])[None], _silu(c), jnp.zeros((14, d), f32)], axis=0)
    names = list(BIG)
    big, gather = big0, ([shards1[n] for n in names], [stand_ins1[n] for n in names])
    for li in range(2):
        cm = mm(srows, big['w_mod'], f"l{li}_mod", f32)[0:2] + small['b_mod'][li][None]
        sl = {k: v[li] for k, v in small.items() if k not in ('c_ctx', 'final_norm')}
        xall, gathered = _layer(xall, big, sl, cm, tabs, n_ctx, li, gather)
        if li == 0:
            big, gather = {n: _assemble(n, g) for n, g in zip(names, gathered)}, ((), ())
    ncb = n_ctx // ROW_TILE
    tgt = jnp.concatenate([jnp.zeros((n_ctx, d), f32), target], axis=0)

    def loss_rows(blk, xx, tg, g):
        e = _rms(xx, g) - tg
        return (jnp.where(blk < ncb, 0.0, 0.5) * jnp.mean(e * e, axis=-1, keepdims=True),)

    (rows,) = rowwise(loss_rows, [xall], [tgt], [small['final_norm'][None]], [1], [f32], "loss")
    return jnp.sum(rows)


def _hbm_call(body, ins, out_shapes, n_sems, name):
    any_spec = pl.BlockSpec(memory_space=pl.ANY)
    return pl.pallas_call(
        body, out_shape=out_shapes, in_specs=[any_spec] * len(ins), out_specs=[any_spec] * len(out_shapes),
        scratch_shapes=[pltpu.SemaphoreType.DMA((n_sems,)), pltpu.SemaphoreType.DMA((n_sems,)), pltpu.SemaphoreType.DMA((len(ins),))],
        name=name)(*ins)


def _gather_steps(x_refs, out_refs, send_sems, recv_sems, local_sems):
    n = len(x_refs)
    x, y, c = lax.axis_index("x"), lax.axis_index("y"), lax.axis_index("c")
    me, sibling = (x, y, c), (x, y, 1 - c)
    chips = [(1 - x, y), (x, 1 - y), (1 - x, 1 - y)]

    def copy(a, k, block, to, src=None):
        px, py, pc = block
        slot = out_refs[a].at[4 * px + 2 * py + pc]
        return pltpu.make_async_remote_copy(
            src_ref=slot if src is None else src, dst_ref=slot,
            send_sem=send_sems.at[7 * a + k], recv_sem=recv_sems.at[7 * a + k], device_id=to, device_id_type=MESH)

    mine = [pltpu.make_async_copy(x_refs[a], out_refs[a].at[4 * x + 2 * y + c], local_sems.at[a]) for a in range(n)]
    first = []
    for a in range(n):
        first += [copy(a, 1 + j, me, (*chip, c), src=x_refs[a]) for j, chip in enumerate(chips)]
        first.append(copy(a, 0, me, sibling, src=x_refs[a]))

    def start():
        for cp in mine + first:
            cp.start()

    def finish():
        passed = []
        for a in range(n):
            for j, chip in enumerate(chips):
                copy(a, 1 + j, (*chip, c), me).wait_recv()
                passed.append(copy(a, 4 + j, (*chip, c), sibling))
                passed[-1].start()
        for a in range(n):
            copy(a, 0, sibling, me).wait_recv()
            for j, chip in enumerate(chips):
                copy(a, 4 + j, (*chip, 1 - c), me).wait_recv()
        for cp in first + passed:
            cp.wait_send()
        for cp in mine:
            cp.wait()

    return start, finish


def _gather_scratch(n):
    return [pltpu.SemaphoreType.DMA((7 * n,)), pltpu.SemaphoreType.DMA((7 * n,)), pltpu.SemaphoreType.DMA((n,))]


def all_gather(shards, name):
    n = len(shards)

    def body(*refs):
        start, finish = _gather_steps(refs[:n], refs[n:2 * n], *refs[2 * n:])
        start()
        finish()

    return _hbm_call(body, shards, [jax.ShapeDtypeStruct((8,) + s.shape, s.dtype) for s in shards], 7 * n, name)


def rs_to_sibling(gs):
    n = len(gs)

    def body(*refs):
        g_refs, out_refs, (send_sems, recv_sems, _) = refs[:n], refs[n:2 * n], refs[2 * n:]
        x, y, c = lax.axis_index("x"), lax.axis_index("y"), lax.axis_index("c")
        copies = [pltpu.make_async_remote_copy(
            src_ref=g_refs[a].at[2 * k + (1 - c)], dst_ref=out_refs[a].at[k], send_sem=send_sems.at[4 * a + k],
            recv_sem=recv_sems.at[4 * a + k], device_id=(x, y, 1 - c), device_id_type=MESH) for a in range(n) for k in range(4)]
        for cp in copies:
            cp.start()
        for cp in copies:
            cp.wait()

    return _hbm_call(body, gs, [jax.ShapeDtypeStruct((4,) + g.shape[1:], g.dtype) for g in gs], 4 * n, "rs_sibling")


def rs_to_chips(ss):
    n = len(ss)
    flips = [(1, 0), (0, 1), (1, 1)]

    def body(*refs):
        s_refs, out_refs, (send_sems, recv_sems, _) = refs[:n], refs[n:2 * n], refs[2 * n:]
        x, y, c = lax.axis_index("x"), lax.axis_index("y"), lax.axis_index("c")
        copies = []
        for a in range(n):
            for k, (fx, fy) in enumerate(flips):
                px, py = (1 - x) if fx else x, (1 - y) if fy else y
                copies.append(pltpu.make_async_remote_copy(
                    src_ref=s_refs[a].at[2 * px + py], dst_ref=out_refs[a].at[k], send_sem=send_sems.at[3 * a + k],
                    recv_sem=recv_sems.at[3 * a + k], device_id=(px, py, c), device_id_type=MESH))
        for cp in copies:
            cp.start()
        for cp in copies:
            cp.wait()

    return _hbm_call(body, ss, [jax.ShapeDtypeStruct((3,) + s.shape[1:], s.dtype) for s in ss], 3 * n, "rs_chips")


def _flat_tile(rows, cols):
    return _row_tile(rows, 4 * 4 * cols)


def pair_sum(g, r1, my_c, name):
    _, rows, cols = g.shape
    tm = _flat_tile(rows, cols)

    def body(c_ref, g_ref, r_ref, o_ref):
        o_ref[...] = (g_ref[...].astype(f32) + r_ref[...].astype(f32)).astype(o_ref.dtype)

    return pl.pallas_call(
        body, grid_spec=pltpu.PrefetchScalarGridSpec(
            num_scalar_prefetch=1, grid=(4, rows // tm),
            in_specs=[pl.BlockSpec((1, tm, cols), lambda k, i, c: (2 * k + c[0], i, 0)),
                      pl.BlockSpec((1, tm, cols), lambda k, i, c: (k, i, 0))],
            out_specs=pl.BlockSpec((1, tm, cols), lambda k, i, c: (k, i, 0))),
        out_shape=jax.ShapeDtypeStruct((4, rows, cols), g.dtype), name=name,
        compiler_params=_cparams("parallel", "parallel"))(my_c, g, r1)


def _adam_math(w, g, m, v):
    m2 = ADAM_B1 * m + (1.0 - ADAM_B1) * g
    v2 = ADAM_B2 * v + (1.0 - ADAM_B2) * (g * g)
    m_hat = m2 / (1.0 - ADAM_B1 ** ADAM_STEP)
    v_hat = v2 / (1.0 - ADAM_B2 ** ADAM_STEP)
    return -ADAM_LR * (m_hat / (jnp.sqrt(v_hat) + ADAM_EPS) + ADAM_WD * w), m2, v2


def sum_adam(parts, w, m, v, name):
    groups, rows, cols = w.shape
    tm = _flat_tile(rows, cols)
    nblk = rows // tm
    flat = []
    scalars = [p[2] for ps in parts for p in ps if p[2] is not None]
    for gi, ps in enumerate(parts):
        flat.append([])
        for arr, static_rows, dyn in ps:
            if dyn is not None:
                flat[gi].append((arr, functools.partial(lambda l, i, s, gi: (s[0], jnp.where(l == gi, i, nblk - 1), 0), gi=gi)))
            else:
                for k in static_rows:
                    flat[gi].append((arr, functools.partial(lambda l, i, s, gi, k: (k, jnp.where(l == gi, i, nblk - 1), 0), gi=gi, k=k)))
    counts = [len(f) for f in flat]
    na = sum(counts)

    def body(s_ref, *refs):
        sums, at = [], 0
        for cnt in counts:
            g = refs[at][0].astype(f32)
            for r in refs[at + 1:at + cnt]:
                g = g + r[0].astype(f32)
            sums.append(g)
            at += cnt
        g = sums[0]
        for gi in range(1, groups):
            g = jnp.where(pl.program_id(0) == gi, sums[gi], g)
        w_ref, m_ref, v_ref = refs[na:na + 3]
        g_out, d_out, m_out, v_out = refs[na + 3:]
        d, m2, v2 = _adam_math(w_ref[0], g, m_ref[0], v_ref[0])
        g_out[0] = g
        d_out[0] = d
        m_out[0] = m2
        v_out[0] = v2

    blk = pl.BlockSpec((1, tm, cols), lambda l, i, s: (l, i, 0))
    scalar = scalars[0] if scalars else jnp.zeros((1,), jnp.int32)
    return pl.pallas_call(
        body, grid_spec=pltpu.PrefetchScalarGridSpec(
            num_scalar_prefetch=1, grid=(groups, nblk),
            in_specs=[pl.BlockSpec((1, tm, cols), im) for f in flat for _, im in f] + [blk, blk, blk],
            out_specs=[blk, blk, blk, blk]),
        out_shape=[jax.ShapeDtypeStruct((groups, rows, cols), f32)] * 4, name=name,
        compiler_params=_cparams("arbitrary", "arbitrary"))(scalar, *[a for f in flat for a, _ in f], w, m, v)


FLAT_COLS = 1024


def _to_flat(vec):
    n = vec.shape[0]
    total = -(-n // (8 * FLAT_COLS)) * 8 * FLAT_COLS
    return jnp.concatenate([vec, jnp.zeros((total - n,), vec.dtype)]).reshape(-1, FLAT_COLS)


def _pack(tree, names):
    return jnp.concatenate([tree[n].reshape(-1) for n in names])


def _unpack(vec, like, names):
    out, off = {}, 0
    for n in names:
        size = like[n].size
        out[n] = vec[off:off + size].reshape(like[n].shape)
        off += size
    return out


def kernel(x, c, ctx, c_ctx, w_mod, b_mod, norm1, norm2, w_in, a_sink, ssm_conv_w, ssm_conv_b, ssm_A_log, ssm_dt_bias, ssm_D, ssm_norm, c_q_norm, c_k_norm, w_oa, w_ob, w_oc, w_out, ffn_w_up, ffn_w_gate, ffn_conv_w, ffn_conv_b, ffn_w_down, final_norm, loss_target, m_c_ctx, m_w_mod, m_b_mod, m_norm1, m_norm2, m_w_in, m_a_sink, m_ssm_conv_w, m_ssm_conv_b, m_ssm_A_log, m_ssm_dt_bias, m_ssm_D, m_ssm_norm, m_c_q_norm, m_c_k_norm, m_w_oa, m_w_ob, m_w_oc, m_w_out, m_ffn_w_up, m_ffn_w_gate, m_ffn_conv_w, m_ffn_conv_b, m_ffn_w_down, m_final_norm, v_c_ctx, v_w_mod, v_b_mod, v_norm1, v_norm2, v_w_in, v_a_sink, v_ssm_conv_w, v_ssm_conv_b, v_ssm_A_log, v_ssm_dt_bias, v_ssm_D, v_ssm_norm, v_c_q_norm, v_c_k_norm, v_w_oa, v_w_ob, v_w_oc, v_w_out, v_ffn_w_up, v_ffn_w_gate, v_ffn_conv_w, v_ffn_conv_b, v_ffn_w_down, v_final_norm):
    args = (x, c, ctx, c_ctx, w_mod, b_mod, norm1, norm2, w_in, a_sink, ssm_conv_w, ssm_conv_b, ssm_A_log, ssm_dt_bias, ssm_D, ssm_norm, c_q_norm, c_k_norm, w_oa, w_ob, w_oc, w_out, ffn_w_up, ffn_w_gate, ffn_conv_w, ffn_conv_b, ffn_w_down, final_norm, loss_target)
    moms = (m_c_ctx, m_w_mod, m_b_mod, m_norm1, m_norm2, m_w_in, m_a_sink, m_ssm_conv_w, m_ssm_conv_b, m_ssm_A_log, m_ssm_dt_bias, m_ssm_D, m_ssm_norm, m_c_q_norm, m_c_k_norm, m_w_oa, m_w_ob, m_w_oc, m_w_out, m_ffn_w_up, m_ffn_w_gate, m_ffn_conv_w, m_ffn_conv_b, m_ffn_w_down, m_final_norm)
    vars_ = (v_c_ctx, v_w_mod, v_b_mod, v_norm1, v_norm2, v_w_in, v_a_sink, v_ssm_conv_w, v_ssm_conv_b, v_ssm_A_log, v_ssm_dt_bias, v_ssm_D, v_ssm_norm, v_c_q_norm, v_c_k_norm, v_w_oa, v_w_ob, v_w_oc, v_w_out, v_ffn_w_up, v_ffn_w_gate, v_ffn_conv_w, v_ffn_conv_b, v_ffn_w_down, v_final_norm)
    p = dict(zip(IN_NAMES, args))
    mom = dict(zip(WEIGHTS, moms))
    var = dict(zip(WEIGHTS, vars_))
    depth = w_in.shape[0]
    n_ctx = ctx.shape[1]
    xi, yi, ci = lax.axis_index("x"), lax.axis_index("y"), lax.axis_index("c")
    dev = 4 * xi + 2 * yi + ci
    big_names = list(BIG)

    assert depth == 2 and n_ctx == ROW_TILE
    shards = [{n: p[n][li].astype(bf16) for n in big_names} for li in range(depth)]
    g_big0 = all_gather([shards[0][n] for n in big_names], "gather_l0")
    g_conv = all_gather([_to_flat(_pack(p, CONV_W))], "gather_conv")[0].reshape(8, -1)
    big0 = {n: _assemble(n, g) for n, g in zip(big_names, g_big0)}
    stand_ins = {n: jnp.zeros((8,) + shards[1][n].shape, bf16) for n in big_names}
    conv_full, off = {}, 0
    for n in CONV_W:
        shp = p[n].shape
        seg = g_conv[:, off:off + p[n].size].reshape(8, *shp)
        conv_full[n] = jnp.moveaxis(seg, 0, -2).reshape(*shp[:-1], 8 * shp[-1])
        off += p[n].size
    small = {n: p[n] for n in REPL}
    small.update(conv_full)

    loss, (g_big0, g_big1, g_small, g_x) = jax.value_and_grad(_loss_fn, argnums=(0, 2, 3, 4))(
        big0, shards[1], stand_ins, small, x[0], ctx[0], c, loss_target[0], n_ctx)
    loss = lax.psum(loss, AXES)

    def send_rows(n):
        b = p[n].shape[-1] if BIG[n] == 1 else p[n].shape[1]
        cut = (lambda g, j: g[:, b * j:b * (j + 1)]) if BIG[n] == 1 else (lambda g, j: g[b * j:b * (j + 1), :])
        return jnp.stack([cut(g_big0[n], j) for j in range(8)])

    send = [send_rows(n) for n in big_names] + [g_big1[n] for n in big_names]
    tags = [f"{n}_l{li}" for li in range(depth) for n in big_names]
    from_sibling = rs_to_sibling(send)
    my_c = ci.reshape(1).astype(jnp.int32)
    side_sum = [pair_sum(s, r, my_c, "rs_pair_sum_" + tg) for tg, s, r in zip(tags, send, from_sibling)]
    from_chips = rs_to_chips(side_sum)
    chip = (2 * xi + yi).reshape(1).astype(jnp.int32)
    big_out = [{}, {}, {}, {}]
    nbig = len(big_names)
    for a, n in enumerate(big_names):
        parts = [[(side_sum[a + li * nbig], None, chip), (from_chips[a + li * nbig], (0, 1, 2), None)] for li in range(depth)]
        outs = sum_adam(parts, p[n], mom[n], var[n], "adam_" + n)
        for k in range(4):
            big_out[k][n] = outs[k]

    sm_names = REPL + list(CONV_W)
    g_vec = _to_flat(_pack(g_small, sm_names))
    gathered = all_gather([g_vec], "gather_small_grads")[0]
    n_repl = sum(p[n].size for n in REPL)

    def repl_flat(tree):
        return _to_flat(jnp.concatenate([_pack(tree, REPL), jnp.zeros((g_vec.size - n_repl,), f32)]))

    outs_small = sum_adam([[(gathered, tuple(range(8)), None)]], repl_flat(p)[None], repl_flat(mom)[None], repl_flat(var)[None],
                          "adam_small")
    g_sum = outs_small[0].reshape(-1)
    small_out = [_unpack(o.reshape(-1), p, REPL) for o in outs_small]
    conv_g_full = _unpack(g_sum[n_repl:], conv_full, CONV_W)
    conv_g = {n: lax.dynamic_slice_in_dim(conv_g_full[n], dev * p[n].shape[-1], p[n].shape[-1], axis=2) for n in CONV_W}
    conv_gv = _to_flat(_pack(conv_g, CONV_W))
    outs_conv = sum_adam([[(conv_gv[None], (0,), None)]], _to_flat(_pack(p, CONV_W))[None], _to_flat(_pack(mom, CONV_W))[None],
                         _to_flat(_pack(var, CONV_W))[None], "adam_conv")
    conv_out = [_unpack(o.reshape(-1), p, CONV_W) for o in outs_conv]

    res = []
    for k in range(4):
        tree = {**big_out[k], **small_out[k], **conv_out[k]}
        res.append([tree[n] for n in WEIGHTS])
    return (loss, g_x[None], *res[0], *res[1], *res[2], *res[3])
```

```python
import functools

import jax
import jax.numpy as jnp
from jax import lax
from jax.experimental import pallas as pl
from jax.experimental.pallas import tpu as pltpu

f32 = jnp.float32
bf16 = jnp.bfloat16
MESH = pl.DeviceIdType.MESH
AXES = ("x", "y", "c")

GRID_W = 64
HEAD_DIM = 64
ROPE_BASE = 10000.0
EPS = 1e-6
WINDOW = 128
N_HEADS = 8
N_KV = 2
SSM_HEADS = 16
SSM_P = 64
SSM_G = 2
SSM_N = 128
SSM_INNER = SSM_HEADS * SSM_P
SSM_BC = SSM_G * SSM_N
SSM_Q = 128
Q_W = N_HEADS * HEAD_DIM
KV_W = N_KV * HEAD_DIM
DT_W = 2 * SSM_HEADS
DT_PAD = 128
ADAM_LR, ADAM_B1, ADAM_B2, ADAM_EPS, ADAM_WD, ADAM_STEP = 0.001, 0.9, 0.999, 1e-08, 0.01, 10

LANES = 128
ROW_TILE = 256
VMEM_BLOCK_BUDGET = 6 * 1024 * 1024
ATTN_SLAB = 128
MM_ROW_CAP = 1088
MM_TILE_CAP = 1536
NEG = -1e30

IN_NAMES = ['x', 'c', 'ctx', 'c_ctx', 'w_mod', 'b_mod', 'norm1', 'norm2', 'w_in', 'a_sink', 'ssm_conv_w', 'ssm_conv_b', 'ssm_A_log', 'ssm_dt_bias', 'ssm_D', 'ssm_norm', 'c_q_norm', 'c_k_norm', 'w_oa', 'w_ob', 'w_oc', 'w_out', 'ffn_w_up', 'ffn_w_gate', 'ffn_conv_w', 'ffn_conv_b', 'ffn_w_down', 'final_norm', 'loss_target']
WEIGHTS = IN_NAMES[3:28]
BIG = {'w_mod': 1, 'w_in': 1, 'w_oa': 1, 'w_ob': 0, 'w_oc': 1, 'w_out': 0, 'ffn_w_up': 1, 'ffn_w_gate': 1, 'ffn_w_down': 0}
CONV_W = ('ssm_conv_w', 'ffn_conv_w')
REPL = [n for n in WEIGHTS if n not in BIG and n not in CONV_W]

NT = (((1,), (1,)), ((), ()))
TN = (((0,), (0,)), ((), ()))
NN = (((1,), (0,)), ((), ()))


def _cparams(*sem):
    return pltpu.CompilerParams(dimension_semantics=sem)


def _div_tile(n, unit, cap):
    for d in range(min(n, int(cap)), 0, -1):
        if n % d == 0 and d % unit == 0:
            return d
    return n


def _row_tile(m, row_bytes):
    return _div_tile(m, 16, max(16, VMEM_BLOCK_BUDGET // row_bytes))


def _mm_call(a, b, mode, out_dtype, name):
    if mode == "nn":
        (m, k), n = a.shape, b.shape[1]
    elif mode == "nt":
        (m, k), n = a.shape, b.shape[0]
    else:
        (k, m), n = a.shape, b.shape[1]
    dims = {"nn": NN, "nt": NT, "tn": TN}[mode]
    ia, ib = a.dtype.itemsize, b.dtype.itemsize
    tm = _div_tile(m, LANES, MM_TILE_CAP) if mode == "tn" else _div_tile(m, 16, MM_ROW_CAP)
    tn = _div_tile(n, LANES, min(MM_TILE_CAP, VMEM_BLOCK_BUDGET // (4 * tm)))
    tk = _div_tile(k, 16 if mode == "tn" else LANES,
                   min(MM_ROW_CAP if mode == "tn" else MM_TILE_CAP, VMEM_BLOCK_BUDGET // (tm * ia), VMEM_BLOCK_BUDGET // (tn * ib)))
    nk = k // tk

    def body(a_ref, b_ref, o_ref, *acc):
        part = lax.dot_general(a_ref[...].astype(bf16), b_ref[...].astype(bf16), dims, preferred_element_type=f32)
        if nk == 1:
            o_ref[...] = part.astype(o_ref.dtype)
            return
        kk = pl.program_id(2)

        @pl.when(kk == 0)
        def _():
            acc[0][...] = part

        @pl.when(kk > 0)
        def _():
            acc[0][...] += part

        @pl.when(kk == nk - 1)
        def _():
            o_ref[...] = acc[0][...].astype(o_ref.dtype)

    a_spec = pl.BlockSpec((tk, tm), lambda i, j, kk: (kk, i)) if mode == "tn" else pl.BlockSpec((tm, tk), lambda i, j, kk: (i, kk))
    b_spec = pl.BlockSpec((tn, tk), lambda i, j, kk: (j, kk)) if mode == "nt" else pl.BlockSpec((tk, tn), lambda i, j, kk: (kk, j))
    return pl.pallas_call(
        body, grid=(m // tm, n // tn, nk), in_specs=[a_spec, b_spec],
        out_specs=pl.BlockSpec((tm, tn), lambda i, j, kk: (i, j)),
        out_shape=jax.ShapeDtypeStruct((m, n), out_dtype),
        scratch_shapes=[pltpu.VMEM((tm, tn), f32)] if nk > 1 else [], name=name,
        compiler_params=_cparams("parallel", "parallel", "arbitrary"))(a, b)


def mm(a, b, name, out_dtype=None):
    @jax.custom_vjp
    def op(a, b):
        return _mm_call(a, b, "nn", out_dtype or bf16, name)

    def fwd(a, b):
        return op(a, b), (a, b)

    def bwd(res, g):
        a, b = res
        return _mm_call(g, b, "nt", a.dtype, name + "_da"), _mm_call(a, g, "tn", b.dtype, name + "_db")

    op.defvjp(fwd, bwd)
    return op(a, b)


def split_cols(u, widths):
    offs = [0]
    for w in widths:
        offs.append(offs[-1] + w)

    @jax.custom_vjp
    def op(u):
        return tuple(u[:, offs[i]:offs[i + 1]] for i in range(len(widths)))

    def fwd(u):
        return op(u), None

    def bwd(_, cts):
        return (jnp.concatenate(cts, axis=1),)

    op.defvjp(fwd, bwd)
    return op(u)


def rowwise(fn, rows, consts, pars, out_widths, out_dtypes, name):
    t = rows[0].shape[0]
    tm = ROW_TILE
    nb = t // tm
    nr, nc, npar = len(rows), len(consts), len(pars)

    def rspec(a):
        return pl.BlockSpec((tm, a.shape[1]), lambda i: (i, 0))

    def pspec(a):
        return pl.BlockSpec(a.shape, lambda i: (0,) * a.ndim)

    def call_fwd(rows, consts, pars):
        def body(*refs):
            blk = pl.program_id(0)
            ins = [r[...].astype(f32) for r in refs[:nr + nc]]
            ps = [r[...] for r in refs[nr + nc:nr + nc + npar]]
            outs = fn(blk, *ins, *ps)
            for o_ref, o in zip(refs[nr + nc + npar:], outs):
                o_ref[...] = o.astype(o_ref.dtype)

        return pl.pallas_call(
            body, grid=(nb,),
            in_specs=[rspec(a) for a in rows + consts] + [pspec(a) for a in pars],
            out_specs=[pl.BlockSpec((tm, w), lambda i: (i, 0)) for w in out_widths],
            out_shape=[jax.ShapeDtypeStruct((t, w), d) for w, d in zip(out_widths, out_dtypes)],
            name=name, compiler_params=_cparams("parallel"))(*rows, *consts, *pars)

    def call_bwd(rows, consts, pars, cts):
        nout = len(cts)

        def body(*refs):
            blk = pl.program_id(0)
            ins = [r[...].astype(f32) for r in refs[:nr]]
            cs = [r[...].astype(f32) for r in refs[nr:nr + nc]]
            ps = [r[...] for r in refs[nr + nc:nr + nc + npar]]
            dys = [r[...].astype(f32) for r in refs[nr + nc + npar:nr + nc + npar + nout]]
            d_refs = refs[nr + nc + npar + nout:]
            _, vjp = jax.vjp(lambda *a: tuple(fn(blk, *a[:nr], *cs, *a[nr:])), *ins, *ps)
            grads = vjp(tuple(dys))
            for d_ref, g in zip(d_refs[:nr], grads[:nr]):
                d_ref[...] = g.astype(d_ref.dtype)
            if npar:
                @pl.when(blk == 0)
                def _():
                    for d_ref in d_refs[nr:]:
                        d_ref[...] = jnp.zeros_like(d_ref)

                for d_ref, g in zip(d_refs[nr:], grads[nr:]):
                    d_ref[...] += g

        return pl.pallas_call(
            body, grid=(nb,),
            in_specs=[rspec(a) for a in rows + consts] + [pspec(a) for a in pars] + [rspec(a) for a in cts],
            out_specs=[rspec(a) for a in rows] + [pspec(a) for a in pars],
            out_shape=[jax.ShapeDtypeStruct(a.shape, a.dtype) for a in rows + pars],
            name=name + "_bwd", compiler_params=_cparams("arbitrary"))(*rows, *consts, *pars, *cts)

    @jax.custom_vjp
    def op(rows, consts, pars):
        return tuple(call_fwd(list(rows), list(consts), list(pars)))

    def fwd(rows, consts, pars):
        return op(rows, consts, pars), (rows, consts, pars)

    def bwd(res, cts):
        rows, consts, pars = res
        g = call_bwd(list(rows), list(consts), list(pars), list(cts))
        return tuple(g[:nr]), tuple(jnp.zeros_like(a) for a in consts), tuple(g[nr:])

    op.defvjp(fwd, bwd)
    return op(tuple(rows), tuple(consts), tuple(pars))


def colwise(fn, cols, pars, out_dtype, name):
    t, w = cols[0].shape
    tc = LANES
    nb = w // tc
    ncol, npar = len(cols), len(pars)

    def cspec(a):
        return pl.BlockSpec((a.shape[0], tc), lambda j: (0, j))

    def call_fwd(cols, pars):
        def body(*refs):
            ins = [r[...].astype(f32) for r in refs[:ncol]]
            ps = [r[...] for r in refs[ncol:ncol + npar]]
            refs[-1][...] = fn(*ins, *ps).astype(refs[-1].dtype)

        return pl.pallas_call(
            body, grid=(nb,), in_specs=[cspec(a) for a in cols + pars], out_specs=cspec(cols[0]),
            out_shape=jax.ShapeDtypeStruct((t, w), out_dtype), name=name, compiler_params=_cparams("parallel"))(*cols, *pars)

    def call_bwd(cols, pars, ct):
        def body(*refs):
            ins = [r[...].astype(f32) for r in refs[:ncol]]
            ps = [r[...] for r in refs[ncol:ncol + npar]]
            dy = refs[ncol + npar][...].astype(f32)
            d_refs = refs[ncol + npar + 1:]
            _, vjp = jax.vjp(fn, *ins, *ps)
            grads = vjp(dy)
            for d_ref, g in zip(d_refs, grads):
                d_ref[...] = g.astype(d_ref.dtype)

        return pl.pallas_call(
            body, grid=(nb,), in_specs=[cspec(a) for a in cols + pars + [ct]],
            out_specs=[cspec(a) for a in cols + pars],
            out_shape=[jax.ShapeDtypeStruct(a.shape, a.dtype) for a in cols + pars],
            name=name + "_bwd", compiler_params=_cparams("parallel"))(*cols, *pars, ct)

    @jax.custom_vjp
    def op(cols, pars):
        return call_fwd(list(cols), list(pars))

    def fwd(cols, pars):
        return op(cols, pars), (cols, pars)

    def bwd(res, ct):
        cols, pars = res
        g = call_bwd(list(cols), list(pars), ct)
        return tuple(g[:ncol]), tuple(g[ncol:])

    op.defvjp(fwd, bwd)
    return op(tuple(cols), tuple(pars))


def _sigmoid(x):
    return 1.0 / (1.0 + jnp.exp(-x))


def _silu(x):
    return x * _sigmoid(x)


def _rms(x, g):
    return x * lax.rsqrt(jnp.mean(x * x, axis=-1, keepdims=True) + EPS) * g


def _shift_rows(u, k, n_ctx):
    @jax.custom_vjp
    def op(u):
        t = u.shape[0]
        row = lax.broadcasted_iota(jnp.int32, u.shape, 0)
        edge = ((row == 0) | (row == n_ctx)) if k == 1 else ((row == n_ctx - 1) | (row == t - 1))
        return jnp.where(edge, 0.0, pltpu.roll(u, k % t, 0))

    op.defvjp(lambda u: (op(u), None), lambda _, g: (_shift_rows(g, -k, n_ctx),))
    return op(u)


def _dwconv(u, w0, w1, w2, b, n_ctx):
    return w0 * _shift_rows(u, 1, n_ctx) + w1 * u + w2 * _shift_rows(u, -1, n_ctx) + b


@jax.custom_vjp
def _rot_half(x):
    w = x.shape[1]
    lane = lax.broadcasted_iota(jnp.int32, x.shape, 1)
    return jnp.where((lane % HEAD_DIM) < HEAD_DIM // 2, pltpu.roll(x, w - HEAD_DIM // 2, 1), pltpu.roll(x, HEAD_DIM // 2, 1))


_rot_half.defvjp(lambda x: (_rot_half(x), None), lambda _, g: (_rot_half(g),))


def _head_rms(x, g):
    w = x.shape[1]
    same = (lax.broadcasted_iota(jnp.int32, (w, w), 0) // HEAD_DIM) == (lax.broadcasted_iota(jnp.int32, (w, w), 1) // HEAD_DIM)
    ms = jnp.dot(x * x, same.astype(f32), precision=lax.Precision.HIGHEST, preferred_element_type=f32) * (1.0 / HEAD_DIM)
    return x * lax.rsqrt(ms + EPS) * g


def _band_ok(i, j, c0, shape, tq, tk):
    kpos = j * tk + lax.broadcasted_iota(jnp.int32, shape, 0)
    qpos = i * tq + (c0 + lax.broadcasted_iota(jnp.int32, shape, 1)) % tq
    return jnp.abs(qpos - kpos) <= WINDOW


def _kv_range(i, nb, window):
    is_ctx = i == 0
    if window:
        return jnp.where(is_ctx, 1, jnp.maximum(i - 1, 1)), jnp.where(is_ctx, 1, jnp.minimum(i + 2, nb))
    return 1, jnp.where(is_ctx, 1, nb)


def _sink_row(sink_ref, g, r, tq):
    return jnp.concatenate([jnp.full((1, tq), sink_ref[g * r + h], f32) for h in range(r)], axis=1)


def _attn_fwd_call(q, k, v, sink, window, name, shards=()):
    h, t, dh = q.shape
    nkv = k.shape[0]
    r = h // nkv
    tq = tk = ROW_TILE
    nb = t // tq
    rows = r * tq
    ns = len(shards)

    assert window or nb % 2 == 1, "the dense schedule takes the kv chunks after the context chunk in pairs"

    def body(sink_ref, q_ref, k_ref, v_ref, *rest):
        x_refs, (o_ref, lse_ref), gathered_refs = rest[:ns], rest[ns:ns + 2], rest[ns + 2:2 * ns + 2]
        m_scr, l_scr, acc_scr, s_a, s_b, p_a, p_b, a_a, a_b = rest[2 * ns + 2:2 * ns + 11]
        comm_sems = rest[2 * ns + 11:]
        g, i = pl.program_id(0), pl.program_id(1)
        if ns:
            @pl.when((g == 0) & (i == 0))
            def _():
                _gather_steps(x_refs, gathered_refs, *comm_sems)[0]()

        qv = q_ref[...].reshape(rows, dh)
        m_scr[...] = jnp.full_like(m_scr, NEG)
        l_scr[...] = jnp.zeros_like(l_scr)
        acc_scr[...] = jnp.zeros_like(acc_scr)

        def kv_rows(j):
            return pl.ds(pl.multiple_of(jnp.minimum(j, nb - 1) * tk, tk), tk)

        def scores(j, s_scr):
            s_scr[...] = lax.dot_general(k_ref[0, kv_rows(j), :], qv, NT, preferred_element_type=f32)

        def softmax(j, s_scr, p_scr, a_scr, masked):
            for cb in range(rows // ATTN_SLAB):
                cs = slice(cb * ATTN_SLAB, (cb + 1) * ATTN_SLAB)
                s = s_scr[:, cs]
                if masked:
                    s = jnp.where(_band_ok(i, j, cb * ATTN_SLAB, s.shape, tq, tk), s, NEG)
                m = m_scr[:, cs]
                m2 = jnp.maximum(m, jnp.max(s, axis=0, keepdims=True))
                p = jnp.exp(s - m2)
                a = jnp.exp(m - m2)
                l_scr[:, cs] = a * l_scr[:, cs] + jnp.sum(p, axis=0, keepdims=True)
                m_scr[:, cs] = m2
                a_scr[:, cs] = a
                p_scr[:, cs] = p.astype(bf16)

        def weighted_v(j, p_scr, a_scr):
            acc_scr[...] = a_scr[...] * acc_scr[...] + lax.dot_general(v_ref[0, kv_rows(j), :], p_scr[...], TN, preferred_element_type=f32)

        scores(0, s_a)
        softmax(0, s_a, p_a, a_a, False)
        if window:
            weighted_v(0, p_a, a_a)
            lo, hi = _kv_range(i, nb, window)

            def chunk(j, c):
                scores(j, s_a)
                softmax(j, s_a, p_a, a_a, True)
                weighted_v(j, p_a, a_a)
                return c

            lax.fori_loop(lo, hi, chunk, 0)
        else:
            scores(1, s_b)

            def pair(tt, c):
                j0 = 2 * tt + 1
                scores(j0 + 1, s_a)
                weighted_v(j0 - 1, p_a, a_a)
                softmax(j0, s_b, p_b, a_b, False)
                scores(j0 + 2, s_b)
                weighted_v(j0, p_b, a_b)
                softmax(j0 + 1, s_a, p_a, a_a, False)
                return c

            lax.fori_loop(0, jnp.where(i == 0, 0, (nb - 1) // 2), pair, 0)
            weighted_v(jnp.where(i == 0, 0, nb - 1), p_a, a_a)
        m, l, acc = m_scr[...], l_scr[...], acc_scr[...]
        if window:
            sk = _sink_row(sink_ref, g, r, tq)
            m2 = jnp.maximum(m, sk)
            a = jnp.exp(m - m2)
            l = a * l + jnp.exp(sk - m2)
            acc = a * acc
            m = m2
        o_ref[...] = (acc / l).T.reshape(r, tq, dh).astype(o_ref.dtype)
        lse_ref[0] = m + jnp.log(l)
        if ns:
            @pl.when((g == nkv - 1) & (i == nb - 1))
            def _():
                _gather_steps(x_refs, gathered_refs, *comm_sems)[1]()

    qspec = pl.BlockSpec((r, tq, dh), lambda g, i: (g, i, 0))
    kspec = pl.BlockSpec((1, t, dh), lambda g, i: (g, 0, 0))
    hbm = pl.BlockSpec(memory_space=pl.ANY)
    sem = ("arbitrary", "arbitrary") if ns else ("parallel", "parallel")
    return pl.pallas_call(
        body, grid=(nkv, nb),
        in_specs=[pl.BlockSpec(memory_space=pltpu.SMEM), qspec, kspec, kspec] + [hbm] * ns,
        out_specs=[qspec, pl.BlockSpec((1, 1, rows), lambda g, i: (g * nb + i, 0, 0))] + [hbm] * ns,
        out_shape=[jax.ShapeDtypeStruct((h, t, dh), bf16), jax.ShapeDtypeStruct((nkv * nb, 1, rows), f32)]
        + [jax.ShapeDtypeStruct((8,) + s.shape, s.dtype) for s in shards],
        scratch_shapes=[pltpu.VMEM((1, rows), f32), pltpu.VMEM((1, rows), f32), pltpu.VMEM((dh, rows), f32),
                        pltpu.VMEM((tk, rows), f32), pltpu.VMEM((tk, rows), f32), pltpu.VMEM((tk, rows), bf16),
                        pltpu.VMEM((tk, rows), bf16), pltpu.VMEM((1, rows), f32), pltpu.VMEM((1, rows), f32)]
        + (_gather_scratch(ns) if ns else []),
        name=name, compiler_params=_cparams(*sem))(sink, q, k, v, *shards)


def _attn_bwd_call(q, k, v, sink, o, lse, do, window, name, side_sums=()):
    h, t, dh = q.shape
    nkv = k.shape[0]
    r = h // nkv
    tq = tk = ROW_TILE
    nb = t // tq
    rows = r * tq
    ns = len(side_sums)

    def body(sink_ref, q_ref, k_ref, v_ref, o_ref, lse_ref, do_ref, *rest):
        ss_refs, (dq_ref, dk_ref, dv_ref, dsink_ref), got_refs = rest[:ns], rest[ns:ns + 4], rest[ns + 4:2 * ns + 4]
        s_scr, dp_scr, p_scr, ds_scr, dq_scr = rest[2 * ns + 4:2 * ns + 9]
        comm_sems = rest[2 * ns + 9:]
        g, i = pl.program_id(0), pl.program_id(1)
        if ns:
            @pl.when((g == 0) & (i == 0))
            def _():
                for cp in _chips_copies(ss_refs, got_refs, *comm_sems):
                    cp.start()

        @pl.when(i == 0)
        def _():
            dk_ref[...] = jnp.zeros_like(dk_ref)
            dv_ref[...] = jnp.zeros_like(dv_ref)

        qv = q_ref[...].reshape(rows, dh)
        dov = do_ref[...].reshape(rows, dh)
        lse_t = lse_ref[0]
        delta_t = jnp.sum((dov.astype(f32) * o_ref[...].reshape(rows, dh).astype(f32)).T, axis=0, keepdims=True)
        dq_scr[...] = jnp.zeros_like(dq_scr)

        def chunk(j, masked):
            kv_rows = pl.ds(pl.multiple_of(j * tk, tk), tk)
            kj = k_ref[0, kv_rows, :]
            vj = v_ref[0, kv_rows, :]
            s_scr[...] = lax.dot_general(kj, qv, NT, preferred_element_type=f32)
            dp_scr[...] = lax.dot_general(vj, dov, NT, preferred_element_type=f32)
            for cb in range(rows // ATTN_SLAB):
                cs = slice(cb * ATTN_SLAB, (cb + 1) * ATTN_SLAB)
                s = s_scr[:, cs]
                if masked:
                    s = jnp.where(_band_ok(i, j, cb * ATTN_SLAB, s.shape, tq, tk), s, NEG)
                p = jnp.exp(s - lse_t[:, cs])
                p_scr[:, cs] = p.astype(bf16)
                ds_scr[:, cs] = (p * (dp_scr[:, cs] - delta_t[:, cs])).astype(bf16)
            dv_ref[0, kv_rows, :] += jnp.dot(p_scr[...], dov, preferred_element_type=f32)
            dk_ref[0, kv_rows, :] += jnp.dot(ds_scr[...], qv, preferred_element_type=f32)
            dq_scr[...] += lax.dot_general(kj, ds_scr[...], TN, preferred_element_type=f32)

        chunk(0, False)
        lo, hi = _kv_range(i, nb, window)
        lax.fori_loop(lo, hi, lambda j, c: (chunk(j, window), c)[1], 0)
        dq_ref[...] = dq_scr[...].T.reshape(r, tq, dh).astype(dq_ref.dtype)
        if window:
            dsink_ref[0] = -jnp.exp(_sink_row(sink_ref, g, r, tq) - lse_t) * delta_t
        else:
            dsink_ref[...] = jnp.zeros_like(dsink_ref)
        if ns:
            @pl.when((g == nkv - 1) & (i == nb - 1))
            def _():
                for cp in _chips_copies(ss_refs, got_refs, *comm_sems):
                    cp.wait()

    qspec = pl.BlockSpec((r, tq, dh), lambda g, i: (g, i, 0))
    cspec = pl.BlockSpec((1, 1, rows), lambda g, i: (g * nb + i, 0, 0))
    kspec = pl.BlockSpec((1, t, dh), lambda g, i: (g, 0, 0))
    hbm = pl.BlockSpec(memory_space=pl.ANY)
    return pl.pallas_call(
        body, grid=(nkv, nb),
        in_specs=[pl.BlockSpec(memory_space=pltpu.SMEM), qspec, kspec, kspec, qspec, cspec, qspec] + [hbm] * ns,
        out_specs=[qspec, kspec, kspec, cspec] + [hbm] * ns,
        out_shape=[jax.ShapeDtypeStruct((h, t, dh), bf16), jax.ShapeDtypeStruct(k.shape, f32), jax.ShapeDtypeStruct(v.shape, f32),
                   jax.ShapeDtypeStruct((nkv * nb, 1, rows), f32)] + [jax.ShapeDtypeStruct((3,) + s.shape[1:], s.dtype) for s in side_sums],
        scratch_shapes=[pltpu.VMEM((tk, rows), f32), pltpu.VMEM((tk, rows), f32), pltpu.VMEM((tk, rows), bf16),
                        pltpu.VMEM((tk, rows), bf16), pltpu.VMEM((dh, rows), f32)]
        + ([pltpu.SemaphoreType.DMA((3 * ns,)), pltpu.SemaphoreType.DMA((3 * ns,))] if ns else []),
        name=name, compiler_params=_cparams("arbitrary" if ns else "parallel", "arbitrary"))(sink, q, k, v, o, lse, do, *side_sums)


def attention(q, k, v, sink, window, name, shards=(), stand_ins=()):
    @jax.custom_vjp
    def op(q, k, v, sink, shards, stand_ins):
        o, _, *gathered = _attn_fwd_call(q, k, v, sink, window, name, shards)
        return o, tuple(gathered)

    def fwd(q, k, v, sink, shards, stand_ins):
        o, lse, *gathered = _attn_fwd_call(q, k, v, sink, window, name, shards)
        return (o, tuple(gathered)), (q, k, v, sink, o, lse, shards)

    def bwd(res, cts):
        q, k, v, sink, o, lse, shards = res
        do, d_gathered = cts
        side_sums = []
        if shards:
            my_c = lax.axis_index("c").reshape(1).astype(jnp.int32)
            from_sibling = rs_to_sibling(list(d_gathered), name + "_rs_sibling")
            side_sums = [pair_sum(s, rr, my_c, f"{name}_pair_sum{a}") for a, (s, rr) in enumerate(zip(d_gathered, from_sibling))]
        dq, dk, dv, dsink_rows, *from_chips = _attn_bwd_call(q, k, v, sink, o, lse, do, window, name + "_bwd", side_sums)
        nkv, r = k.shape[0], q.shape[0] // k.shape[0]
        dsink = jnp.sum(dsink_rows.reshape(nkv, -1, r, ROW_TILE), axis=(1, 3)).reshape(nkv * r)
        reduced = tuple(jnp.concatenate([s, fc, jnp.zeros_like(s[:1])], axis=0) for s, fc in zip(side_sums, from_chips))
        return dq, dk.astype(k.dtype), dv.astype(v.dtype), dsink, tuple(jnp.zeros_like(s) for s in shards), reduced

    op.defvjp(fwd, bwd)
    return op(q, k, v, sink, tuple(shards), tuple(stand_ins))


def _ssd_pair(xs_p, dtx_p, dtr1, dtr2, ac1, ac2, bg, cg, hin_p, rev):
    q = xs_p.shape[0]
    ii = lax.broadcasted_iota(jnp.int32, (q, q), 0)
    jj = lax.broadcasted_iota(jnp.int32, (q, q), 1)
    tri = (ii <= jj) if rev else (ii >= jj)
    lo = lax.broadcasted_iota(jnp.int32, (q, LANES), 1) < SSM_P
    lo_row = lax.broadcasted_iota(jnp.int32, (1, LANES), 1) < SSM_P

    def cums(dtr, ac):
        a = dtr * ac
        c = jnp.sum(jnp.where(tri, jnp.broadcast_to(a, (q, q)), 0.0), axis=1, keepdims=True)
        return c, jnp.sum(a, axis=1, keepdims=True)

    def lmat(c):
        cf = jnp.broadcast_to(c, (q, q))
        return jnp.where(tri, jnp.exp(jnp.minimum(cf - cf.T, 0.0)), 0.0)

    c1, t1 = cums(dtr1, ac1)
    c2, t2 = cums(dtr2, ac2)
    cb = lax.dot_general(cg.astype(bf16), bg.astype(bf16), NT, preferred_element_type=f32)
    m = jnp.concatenate([cb * lmat(c1), cb * lmat(c2)], axis=1)
    xdt = xs_p * dtx_p
    x2 = jnp.concatenate([jnp.where(lo, xdt, 0.0), jnp.where(lo, 0.0, xdt)], axis=0)
    y_diag = jnp.dot(m.astype(bf16), x2.astype(bf16), preferred_element_type=f32)
    csel = jnp.where(lo, jnp.broadcast_to(c1, (q, LANES)), jnp.broadcast_to(c2, (q, LANES)))
    tsel = jnp.where(lo_row, jnp.broadcast_to(t1, (1, LANES)), jnp.broadcast_to(t2, (1, LANES)))
    st = lax.dot_general(bg.astype(bf16), (xdt * jnp.exp(tsel - csel)).astype(bf16), TN, preferred_element_type=f32)
    y_off = jnp.dot(cg.astype(bf16), hin_p.astype(bf16), preferred_element_type=f32) * jnp.exp(csel)
    return y_diag + y_off, hin_p * jnp.exp(tsel) + st


def _ssd_order(s, nc, ncc, rev):
    if not rev:
        return s
    return jnp.where(s < ncc, ncc - 1 - s, nc - 1 - (s - ncc))


def _ssd_pair_slices(j):
    return slice(LANES * j, LANES * (j + 1)), 2 * j, 2 * j + 1, (2 * j) // (SSM_HEADS // SSM_G)


def _head_lanes(w, transpose=False):
    shape = (w, SSM_HEADS) if transpose else (SSM_HEADS, w)
    head = lax.broadcasted_iota(jnp.int32, shape, 1 if transpose else 0)
    lane = lax.broadcasted_iota(jnp.int32, shape, 0 if transpose else 1)
    return (lane // SSM_P == head).astype(f32)


def _ssd_fwd_call(xs, dt, dtr, bm, cm, acol, rev, n_ctx, name):
    t, w = xs.shape
    q = SSM_Q
    nc, ncc = t // q, n_ctx // q

    def body(xs_ref, dt_ref, dtr_ref, b_ref, c_ref, a_ref, y_ref, hin_ref, h_scr):
        @pl.when(pl.program_id(0) == 0)
        def _():
            h_scr[...] = jnp.zeros_like(h_scr)

        hin_ref[0] = h_scr[...]
        dtx = jnp.dot(dt_ref[...], _head_lanes(w), precision=lax.Precision.HIGHEST, preferred_element_type=f32)
        for j in range(SSM_HEADS // 2):
            sl, h1, h2, g = _ssd_pair_slices(j)
            gs = slice(SSM_N * g, SSM_N * (g + 1))
            y, hout = _ssd_pair(xs_ref[:, sl], dtx[:, sl], dtr_ref[h1:h1 + 1, :], dtr_ref[h2:h2 + 1, :],
                                a_ref[h1:h1 + 1, :], a_ref[h2:h2 + 1, :], b_ref[:, gs], c_ref[:, gs], h_scr[:, sl], rev)
            y_ref[:, sl] = y.astype(y_ref.dtype)
            h_scr[:, sl] = hout

    def at(s):
        return _ssd_order(s, nc, ncc, rev)

    return pl.pallas_call(
        body, grid=(nc,),
        in_specs=[pl.BlockSpec((q, w), lambda s: (at(s), 0)), pl.BlockSpec((q, SSM_HEADS), lambda s: (at(s), 0)),
                  pl.BlockSpec((SSM_HEADS, q), lambda s: (0, at(s))),
                  pl.BlockSpec((q, SSM_BC), lambda s: (at(s), 0)), pl.BlockSpec((q, SSM_BC), lambda s: (at(s), 0)),
                  pl.BlockSpec((SSM_HEADS, 1), lambda s: (0, 0))],
        out_specs=[pl.BlockSpec((q, w), lambda s: (at(s), 0)), pl.BlockSpec((1, SSM_N, w), lambda s: (s, 0, 0))],
        out_shape=[jax.ShapeDtypeStruct((t, w), xs.dtype), jax.ShapeDtypeStruct((nc, SSM_N, w), f32)],
        scratch_shapes=[pltpu.VMEM((SSM_N, w), f32)],
        name=name, compiler_params=_cparams("arbitrary"))(xs, dt, dtr, bm, cm, acol)


def _ssd_bwd_call(xs, dt, dtr, bm, cm, acol, hin, dy, rev, n_ctx, name):
    t, w = xs.shape
    q = SSM_Q
    nc, ncc = t // q, n_ctx // q

    def body(xs_ref, dt_ref, dtr_ref, b_ref, c_ref, a_ref, hin_ref, dy_ref,
             dxs_ref, ddt_ref, ddtr_ref, db_ref, dc_ref, da_ref, dh_scr):
        @pl.when(pl.program_id(0) == 0)
        def _():
            dh_scr[...] = jnp.zeros_like(dh_scr)
            da_ref[...] = jnp.zeros_like(da_ref)

        dtx = jnp.dot(dt_ref[...], _head_lanes(w), precision=lax.Precision.HIGHEST, preferred_element_type=f32)
        ddtx = []
        db = [None] * SSM_G
        dc = [None] * SSM_G
        for j in range(SSM_HEADS // 2):
            sl, h1, h2, g = _ssd_pair_slices(j)
            gs = slice(SSM_N * g, SSM_N * (g + 1))
            _, vjp = jax.vjp(
                functools.partial(_ssd_pair, rev=rev),
                xs_ref[:, sl].astype(f32), dtx[:, sl], dtr_ref[h1:h1 + 1, :], dtr_ref[h2:h2 + 1, :],
                a_ref[h1:h1 + 1, :], a_ref[h2:h2 + 1, :], b_ref[:, gs].astype(f32), c_ref[:, gs].astype(f32), hin_ref[0, :, sl])
            gr = vjp((dy_ref[:, sl].astype(f32), dh_scr[:, sl]))
            dxs_ref[:, sl] = gr[0].astype(dxs_ref.dtype)
            ddtx.append(gr[1])
            ddtr_ref[h1:h1 + 1, :] = gr[2]
            ddtr_ref[h2:h2 + 1, :] = gr[3]
            da_ref[h1:h1 + 1, :] += gr[4]
            da_ref[h2:h2 + 1, :] += gr[5]
            db[g] = gr[6] if db[g] is None else db[g] + gr[6]
            dc[g] = gr[7] if dc[g] is None else dc[g] + gr[7]
            dh_scr[:, sl] = gr[8]
        for g in range(SSM_G):
            gs = slice(SSM_N * g, SSM_N * (g + 1))
            db_ref[:, gs] = db[g].astype(db_ref.dtype)
            dc_ref[:, gs] = dc[g].astype(dc_ref.dtype)
        ddt_ref[...] = jnp.dot(jnp.concatenate(ddtx, axis=1), _head_lanes(w, transpose=True),
                               precision=lax.Precision.HIGHEST, preferred_element_type=f32)

    def step(s):
        return nc - 1 - s

    def at(s):
        return _ssd_order(step(s), nc, ncc, rev)

    row = lambda wd: pl.BlockSpec((q, wd), lambda s: (at(s), 0))
    dtr_spec = pl.BlockSpec((SSM_HEADS, q), lambda s: (0, at(s)))
    a_spec = pl.BlockSpec((SSM_HEADS, 1), lambda s: (0, 0))
    return pl.pallas_call(
        body, grid=(nc,),
        in_specs=[row(w), row(SSM_HEADS), dtr_spec, row(SSM_BC), row(SSM_BC), a_spec,
                  pl.BlockSpec((1, SSM_N, w), lambda s: (step(s), 0, 0)), row(w)],
        out_specs=[row(w), row(SSM_HEADS), dtr_spec, row(SSM_BC), row(SSM_BC), a_spec],
        out_shape=[jax.ShapeDtypeStruct((t, w), xs.dtype), jax.ShapeDtypeStruct(dt.shape, f32), jax.ShapeDtypeStruct(dtr.shape, f32),
                   jax.ShapeDtypeStruct(bm.shape, bm.dtype), jax.ShapeDtypeStruct(cm.shape, cm.dtype), jax.ShapeDtypeStruct(acol.shape, f32)],
        scratch_shapes=[pltpu.VMEM((SSM_N, w), f32)],
        name=name, compiler_params=_cparams("arbitrary"))(xs, dt, dtr, bm, cm, acol, hin, dy)


def ssd_scan(xs, dt, dtr, bm, cm, acol, rev, n_ctx, name):
    @jax.custom_vjp
    def op(xs, dt, dtr, bm, cm, acol):
        return _ssd_fwd_call(xs, dt, dtr, bm, cm, acol, rev, n_ctx, name)[0]

    def fwd(xs, dt, dtr, bm, cm, acol):
        y, hin = _ssd_fwd_call(xs, dt, dtr, bm, cm, acol, rev, n_ctx, name)
        return y, (xs, dt, dtr, bm, cm, acol, hin)

    def bwd(res, dy):
        return tuple(_ssd_bwd_call(*res, dy, rev, n_ctx, name + "_bwd"))

    op.defvjp(fwd, bwd)
    return op(xs, dt, dtr, bm, cm, acol)


def _deinterleave(w, n_heads):
    lead = w.shape[:-1]
    return w.reshape(*lead, n_heads, HEAD_DIM // 2, 2).swapaxes(-1, -2).reshape(*lead, n_heads * HEAD_DIM)


def _interleave(w, n_heads):
    lead = w.shape[:-1]
    return w.reshape(*lead, n_heads, 2, HEAD_DIM // 2).swapaxes(-1, -2).reshape(*lead, n_heads * HEAD_DIM)


def _in_layout(d):
    sizes = [('a_q', Q_W, N_HEADS), ('a_k', KV_W, N_KV), ('a_v', KV_W, 0), ('b_z', SSM_INNER, 0),
             ('b_xbc', SSM_INNER + 2 * SSM_BC, 0), ('b_dt', DT_W, 0), ('c_q', Q_W, N_HEADS), ('c_k', KV_W, N_KV),
             ('c_v', KV_W, 0), ('g_a', d, 0), ('g_b', d, 0), ('g_c', d, 0)]
    out, start = [], 0
    for name, n, heads in sizes:
        out.append((name, start, n, heads))
        start += n
    return out


@jax.custom_vjp
def _w_in_split(w):
    d = w.shape[0]
    parts, dt = [], None
    for name, s, n, heads in _in_layout(d):
        p = w[:, s:s + n]
        if heads:
            p = _deinterleave(p, heads)
        if name == 'b_dt':
            dt = jnp.concatenate([p, jnp.zeros((d, DT_PAD - n), w.dtype)], axis=1)
        else:
            parts.append(p)
    return jnp.concatenate(parts, axis=1), dt


def _w_in_join(g_main, g_dt):
    d = g_main.shape[0]
    parts, start = [], 0
    for name, _, n, heads in _in_layout(d):
        if name == 'b_dt':
            parts.append(g_dt[:, :n])
            continue
        p = g_main[:, start:start + n]
        parts.append(_interleave(p, heads) if heads else p)
        start += n
    return jnp.concatenate(parts, axis=1)


_w_in_split.defvjp(lambda w: (_w_in_split(w), None), lambda _, g: (_w_in_join(*g),))


def _rope_tables(n_ctx, n_lat):
    rows = n_lat // GRID_W
    t_row = jnp.repeat(jnp.arange(rows), GRID_W).astype(f32)
    t_col = jnp.tile(jnp.arange(GRID_W), rows).astype(f32)
    n = HEAD_DIM // 4
    inv = ROPE_BASE ** (-jnp.arange(n, dtype=f32) / n)
    ang = jnp.concatenate([t_row[:, None] * inv, t_col[:, None] * inv], axis=-1)
    cos = jnp.concatenate([jnp.ones((n_ctx, HEAD_DIM // 2), f32), jnp.cos(ang)], axis=0)
    sin = jnp.concatenate([jnp.zeros((n_ctx, HEAD_DIM // 2), f32), jnp.sin(ang)], axis=0)
    return jnp.concatenate([cos, cos], axis=1), jnp.concatenate([-sin, sin], axis=1)


def _heads_major(a, n_heads):
    return a.reshape(a.shape[0], n_heads, HEAD_DIM).transpose(1, 0, 2)


def _heads_minor(a):
    return a.transpose(1, 0, 2).reshape(a.shape[1], a.shape[0] * HEAD_DIM)


def _layer(xall, w, s, cm, tabs, n_ctx, li, gather=((), ())):
    t, d = xall.shape
    ncb = n_ctx // ROW_TILE
    nm = f"l{li}_"
    ctq, stq, ctk, stk = tabs
    mod = [(cm[0:1, i * d:(i + 1) * d], cm[1:2, i * d:(i + 1) * d]) for i in range(6)]

    def pick(blk, pair_c, pair_l):
        return jnp.where(blk < ncb, pair_c, pair_l)

    def norm_mod(blk, x, g, sh_c, sh_l, sc_c, sc_l):
        return (_rms(x, g) * (1.0 + pick(blk, sc_c, sc_l)) + pick(blk, sh_c, sh_l),)

    (h,) = rowwise(norm_mod, [xall], [], [s['norm1'][None], *mod[0], *mod[1]], [d], [bf16], nm + "norm1")
    w_main, w_dt = _w_in_split(w['w_in'])
    u = mm(h, w_main, nm + "in")
    b_dt = mm(h, w_dt, nm + "in_dt", f32)
    a_q, a_k, a_v, b_z, b_xbc, c_q, c_k, c_v, g_a, g_b, g_c = split_cols(u, [n for name, _, n, _ in _in_layout(d) if name != 'b_dt'])

    def rope(blk, q, k, v, ct_q, st_q, ct_k, st_k):
        return q * ct_q + _rot_half(q) * st_q, k * ct_k + _rot_half(k) * st_k, v

    def norm_rope(blk, q, k, v, ct_q, st_q, ct_k, st_k, gq, gk):
        return rope(blk, _head_rms(q, gq), _head_rms(k, gk), v, ct_q, st_q, ct_k, st_k)

    qkv_w, qkv_t = [Q_W, KV_W, KV_W], [bf16, bf16, bf16]
    qa, ka, va = rowwise(rope, [a_q, a_k, a_v], [ctq, stq, ctk, stk], [], qkv_w, qkv_t, nm + "ropeA")
    gq = jnp.tile(_deinterleave(s['c_q_norm'], 1), N_HEADS)[None]
    gk = jnp.tile(_deinterleave(s['c_k_norm'], 1), N_KV)[None]
    qc, kc, vc = rowwise(norm_rope, [c_q, c_k, c_v], [ctq, stq, ctk, stk], [gq, gk], qkv_w, qkv_t, nm + "ropeC")
    ya = _heads_minor(attention(_heads_major(qa, N_HEADS), _heads_major(ka, N_KV), _heads_major(va, N_KV),
                                s['a_sink'], True, nm + "attnA")[0])
    yc, gathered = attention(_heads_major(qc, N_HEADS), _heads_major(kc, N_KV), _heads_major(vc, N_KV),
                             jnp.zeros((N_HEADS,), f32), False, nm + "attnC", *gather)
    yc = _heads_minor(yc)

    cw, cb = s['ssm_conv_w'], s['ssm_conv_b']
    conv_silu = lambda uu, w0, w1, w2, b: _silu(_dwconv(uu, w0, w1, w2, b, n_ctx))
    xbc = colwise(conv_silu, [b_xbc], [cw[0:1], cw[1:2], cw[2:3], cb[None]], bf16, nm + "ssmconv")
    xs, bm, cmat = split_cols(xbc, [SSM_INNER, SSM_BC, SSM_BC])
    bias = jnp.concatenate([s['ssm_dt_bias'].reshape(1, DT_W), jnp.zeros((1, DT_PAD - DT_W), f32)], axis=1)

    def softplus(blk, r, b):
        z = r + b
        return (jnp.maximum(z, 0.0) + jnp.log(1.0 + jnp.exp(-jnp.abs(z))),)

    (dt_all,) = rowwise(softplus, [b_dt], [], [bias], [DT_PAD], [f32], nm + "dt")
    a_coef = -jnp.exp(s['ssm_A_log'])
    ys_dir = []
    for di, rev in enumerate((False, True)):
        dt = dt_all[:, di * SSM_HEADS:(di + 1) * SSM_HEADS]
        ys_dir.append(ssd_scan(xs, dt, dt.T, bm, cmat, a_coef[di][:, None], rev, n_ctx,
                               nm + ("ssd_r" if rev else "ssd_f")))

    def ssm_out(blk, yf, yb, x, z, dskip, g):
        return (_rms((yf + yb + x * dskip) * _silu(z), g),)

    (ysn,) = rowwise(ssm_out, [ys_dir[0], ys_dir[1], xs, b_z], [], [jnp.repeat(s['ssm_D'], SSM_P)[None], s['ssm_norm'][None]],
                     [SSM_INNER], [bf16], nm + "ssmout")

    pa, pb, pc = mm(ya, w['w_oa'], nm + "oa"), mm(ysn, w['w_ob'], nm + "ob"), mm(yc, w['w_oc'], nm + "oc")

    def merge(blk, ga, gb, gc, a, b, c):
        return (_sigmoid(ga) * a + _sigmoid(gb) * b + _sigmoid(gc) * c,)

    (mrg,) = rowwise(merge, [g_a, g_b, g_c, pa, pb, pc], [], [], [d], [bf16], nm + "merge")
    o = mm(mrg, w['w_out'], nm + "out")

    def resid_norm_mod(blk, x, oo, g1_c, g1_l, g, sh_c, sh_l, sc_c, sc_l):
        x1 = x + pick(blk, g1_c, g1_l) * oo
        return x1, _rms(x1, g) * (1.0 + pick(blk, sc_c, sc_l)) + pick(blk, sh_c, sh_l)

    x1, h2 = rowwise(resid_norm_mod, [xall, o], [], [*mod[2], s['norm2'][None], *mod[3], *mod[4]], [d, d], [f32, bf16], nm + "norm2")
    up, gt = mm(h2, w['ffn_w_up'], nm + "up"), mm(h2, w['ffn_w_gate'], nm + "gate")
    fw, fb = s['ffn_conv_w'], s['ffn_conv_b']
    ffn_act = lambda g_, u_, w0, w1, w2, b: _silu(_dwconv(g_, w0, w1, w2, b, n_ctx)) * u_
    act = colwise(ffn_act, [gt, up], [fw[0:1], fw[1:2], fw[2:3], fb[None]], bf16, nm + "ffnact")
    f = mm(act, w['ffn_w_down'], nm + "down")

    def resid(blk, x, ff, g2_c, g2_l):
        return (x + pick(blk, g2_c, g2_l) * ff,)

    (x2,) = rowwise(resid, [x1, f], [], [*mod[5]], [d], [f32], nm + "resid")
    return x2, gathered


def _assemble(name, gathered):
    return jnp.concatenate([gathered[j] for j in range(8)], axis=1 if BIG[name] == 1 else 0)


def _loss_fn(big0, shards1, stand_ins1, small, x, ctx, c, target, n_ctx):
    n_lat, d = x.shape
    xall = jnp.concatenate([ctx, x], axis=0)
    ct, st = _rope_tables(n_ctx, n_lat)
    tabs = (jnp.tile(ct, (1, N_HEADS)) * HEAD_DIM ** -0.5, jnp.tile(st, (1, N_HEADS)) * HEAD_DIM ** -0.5,
            jnp.tile(ct, (1, N_KV)), jnp.tile(st, (1, N_KV)))
    srows = jnp.concatenate([_silu(small['c_ctx'])[None], _silu(c), jnp.zeros((14, d), f32)], axis=0)
    names = list(BIG)
    big, gather = big0, ([shards1[n] for n in names], [stand_ins1[n] for n in names])
    for li in range(2):
        cm = mm(srows, big['w_mod'], f"l{li}_mod", f32)[0:2] + small['b_mod'][li][None]
        sl = {k: v[li] for k, v in small.items() if k not in ('c_ctx', 'final_norm')}
        xall, gathered = _layer(xall, big, sl, cm, tabs, n_ctx, li, gather)
        if li == 0:
            big, gather = {n: _assemble(n, g) for n, g in zip(names, gathered)}, ((), ())
    ncb = n_ctx // ROW_TILE
    tgt = jnp.concatenate([jnp.zeros((n_ctx, d), f32), target], axis=0)

    def loss_rows(blk, xx, tg, g):
        e = _rms(xx, g) - tg
        return (jnp.where(blk < ncb, 0.0, 0.5) * jnp.mean(e * e, axis=-1, keepdims=True),)

    (rows,) = rowwise(loss_rows, [xall], [tgt], [small['final_norm'][None]], [1], [f32], "loss")
    return jnp.sum(rows)


def _hbm_call(body, ins, out_shapes, n_sems, name):
    any_spec = pl.BlockSpec(memory_space=pl.ANY)
    return pl.pallas_call(
        body, out_shape=out_shapes, in_specs=[any_spec] * len(ins), out_specs=[any_spec] * len(out_shapes),
        scratch_shapes=[pltpu.SemaphoreType.DMA((n_sems,)), pltpu.SemaphoreType.DMA((n_sems,)), pltpu.SemaphoreType.DMA((len(ins),))],
        name=name)(*ins)


def _gather_steps(x_refs, out_refs, send_sems, recv_sems, local_sems):
    n = len(x_refs)
    x, y, c = lax.axis_index("x"), lax.axis_index("y"), lax.axis_index("c")
    me, sibling = (x, y, c), (x, y, 1 - c)
    chips = [(1 - x, y), (x, 1 - y), (1 - x, 1 - y)]

    def copy(a, k, block, to, src=None):
        px, py, pc = block
        slot = out_refs[a].at[4 * px + 2 * py + pc]
        return pltpu.make_async_remote_copy(
            src_ref=slot if src is None else src, dst_ref=slot,
            send_sem=send_sems.at[7 * a + k], recv_sem=recv_sems.at[7 * a + k], device_id=to, device_id_type=MESH)

    mine = [pltpu.make_async_copy(x_refs[a], out_refs[a].at[4 * x + 2 * y + c], local_sems.at[a]) for a in range(n)]
    first = []
    for a in range(n):
        first += [copy(a, 1 + j, me, (*chip, c), src=x_refs[a]) for j, chip in enumerate(chips)]
        first.append(copy(a, 0, me, sibling, src=x_refs[a]))

    def start():
        for cp in mine + first:
            cp.start()

    def finish():
        passed = []
        for a in range(n):
            for j, chip in enumerate(chips):
                copy(a, 1 + j, (*chip, c), me).wait_recv()
                passed.append(copy(a, 4 + j, (*chip, c), sibling))
                passed[-1].start()
        for a in range(n):
            copy(a, 0, sibling, me).wait_recv()
            for j, chip in enumerate(chips):
                copy(a, 4 + j, (*chip, 1 - c), me).wait_recv()
        for cp in first + passed:
            cp.wait_send()
        for cp in mine:
            cp.wait()

    return start, finish


def _gather_scratch(n):
    return [pltpu.SemaphoreType.DMA((7 * n,)), pltpu.SemaphoreType.DMA((7 * n,)), pltpu.SemaphoreType.DMA((n,))]


def all_gather(shards, name):
    n = len(shards)

    def body(*refs):
        start, finish = _gather_steps(refs[:n], refs[n:2 * n], *refs[2 * n:])
        start()
        finish()

    return _hbm_call(body, shards, [jax.ShapeDtypeStruct((8,) + s.shape, s.dtype) for s in shards], 7 * n, name)


def rs_to_sibling(gs, name="rs_sibling"):
    n = len(gs)

    def body(*refs):
        g_refs, out_refs, (send_sems, recv_sems, _) = refs[:n], refs[n:2 * n], refs[2 * n:]
        x, y, c = lax.axis_index("x"), lax.axis_index("y"), lax.axis_index("c")
        copies = [pltpu.make_async_remote_copy(
            src_ref=g_refs[a].at[2 * k + (1 - c)], dst_ref=out_refs[a].at[k], send_sem=send_sems.at[4 * a + k],
            recv_sem=recv_sems.at[4 * a + k], device_id=(x, y, 1 - c), device_id_type=MESH) for a in range(n) for k in range(4)]
        for cp in copies:
            cp.start()
        for cp in copies:
            cp.wait()

    return _hbm_call(body, gs, [jax.ShapeDtypeStruct((4,) + g.shape[1:], g.dtype) for g in gs], 4 * n, name)


def rs_to_chips(ss):
    n = len(ss)

    def body(*refs):
        copies = _chips_copies(refs[:n], refs[n:2 * n], refs[2 * n], refs[2 * n + 1])
        for cp in copies:
            cp.start()
        for cp in copies:
            cp.wait()

    return _hbm_call(body, ss, [jax.ShapeDtypeStruct((3,) + s.shape[1:], s.dtype) for s in ss], 3 * n, "rs_chips")


def _chips_copies(s_refs, out_refs, send_sems, recv_sems):
    x, y, c = lax.axis_index("x"), lax.axis_index("y"), lax.axis_index("c")
    copies = []
    for a in range(len(s_refs)):
        for k, (fx, fy) in enumerate([(1, 0), (0, 1), (1, 1)]):
            px, py = (1 - x) if fx else x, (1 - y) if fy else y
            copies.append(pltpu.make_async_remote_copy(
                src_ref=s_refs[a].at[2 * px + py], dst_ref=out_refs[a].at[k], send_sem=send_sems.at[3 * a + k],
                recv_sem=recv_sems.at[3 * a + k], device_id=(px, py, c), device_id_type=MESH))
    return copies


def _flat_tile(rows, cols):
    return _row_tile(rows, 4 * 4 * cols)


def pair_sum(g, r1, my_c, name):
    _, rows, cols = g.shape
    tm = _flat_tile(rows, cols)

    def body(c_ref, g_ref, r_ref, o_ref):
        o_ref[...] = (g_ref[...].astype(f32) + r_ref[...].astype(f32)).astype(o_ref.dtype)

    return pl.pallas_call(
        body, grid_spec=pltpu.PrefetchScalarGridSpec(
            num_scalar_prefetch=1, grid=(4, rows // tm),
            in_specs=[pl.BlockSpec((1, tm, cols), lambda k, i, c: (2 * k + c[0], i, 0)),
                      pl.BlockSpec((1, tm, cols), lambda k, i, c: (k, i, 0))],
            out_specs=pl.BlockSpec((1, tm, cols), lambda k, i, c: (k, i, 0))),
        out_shape=jax.ShapeDtypeStruct((4, rows, cols), g.dtype), name=name,
        compiler_params=_cparams("parallel", "parallel"))(my_c, g, r1)


def _adam_math(w, g, m, v):
    m2 = ADAM_B1 * m + (1.0 - ADAM_B1) * g
    v2 = ADAM_B2 * v + (1.0 - ADAM_B2) * (g * g)
    m_hat = m2 / (1.0 - ADAM_B1 ** ADAM_STEP)
    v_hat = v2 / (1.0 - ADAM_B2 ** ADAM_STEP)
    return -ADAM_LR * (m_hat / (jnp.sqrt(v_hat) + ADAM_EPS) + ADAM_WD * w), m2, v2


def sum_adam(parts, w, m, v, name):
    groups, rows, cols = w.shape
    tm = _flat_tile(rows, cols)
    nblk = rows // tm
    flat = []
    scalars = [p[2] for ps in parts for p in ps if p[2] is not None]
    for gi, ps in enumerate(parts):
        flat.append([])
        for arr, static_rows, dyn in ps:
            if dyn is not None:
                flat[gi].append((arr, functools.partial(lambda l, i, s, gi: (s[0], jnp.where(l == gi, i, nblk - 1), 0), gi=gi)))
            else:
                for k in static_rows:
                    flat[gi].append((arr, functools.partial(lambda l, i, s, gi, k: (k, jnp.where(l == gi, i, nblk - 1), 0), gi=gi, k=k)))
    counts = [len(f) for f in flat]
    na = sum(counts)

    def body(s_ref, *refs):
        sums, at = [], 0
        for cnt in counts:
            g = refs[at][0].astype(f32)
            for r in refs[at + 1:at + cnt]:
                g = g + r[0].astype(f32)
            sums.append(g)
            at += cnt
        g = sums[0]
        for gi in range(1, groups):
            g = jnp.where(pl.program_id(0) == gi, sums[gi], g)
        w_ref, m_ref, v_ref = refs[na:na + 3]
        g_out, d_out, m_out, v_out = refs[na + 3:]
        d, m2, v2 = _adam_math(w_ref[0], g, m_ref[0], v_ref[0])
        g_out[0] = g
        d_out[0] = d
        m_out[0] = m2
        v_out[0] = v2

    blk = pl.BlockSpec((1, tm, cols), lambda l, i, s: (l, i, 0))
    scalar = scalars[0] if scalars else jnp.zeros((1,), jnp.int32)
    return pl.pallas_call(
        body, grid_spec=pltpu.PrefetchScalarGridSpec(
            num_scalar_prefetch=1, grid=(groups, nblk),
            in_specs=[pl.BlockSpec((1, tm, cols), im) for f in flat for _, im in f] + [blk, blk, blk],
            out_specs=[blk, blk, blk, blk]),
        out_shape=[jax.ShapeDtypeStruct((groups, rows, cols), f32)] * 4, name=name,
        compiler_params=_cparams("arbitrary", "arbitrary"))(scalar, *[a for f in flat for a, _ in f], w, m, v)


FLAT_COLS = 1024


def _to_flat(vec):
    n = vec.shape[0]
    total = -(-n // (8 * FLAT_COLS)) * 8 * FLAT_COLS
    return jnp.concatenate([vec, jnp.zeros((total - n,), vec.dtype)]).reshape(-1, FLAT_COLS)


def _pack(tree, names):
    return jnp.concatenate([tree[n].reshape(-1) for n in names])


def _unpack(vec, like, names):
    out, off = {}, 0
    for n in names:
        size = like[n].size
        out[n] = vec[off:off + size].reshape(like[n].shape)
        off += size
    return out


def kernel(x, c, ctx, c_ctx, w_mod, b_mod, norm1, norm2, w_in, a_sink, ssm_conv_w, ssm_conv_b, ssm_A_log, ssm_dt_bias, ssm_D, ssm_norm, c_q_norm, c_k_norm, w_oa, w_ob, w_oc, w_out, ffn_w_up, ffn_w_gate, ffn_conv_w, ffn_conv_b, ffn_w_down, final_norm, loss_target, m_c_ctx, m_w_mod, m_b_mod, m_norm1, m_norm2, m_w_in, m_a_sink, m_ssm_conv_w, m_ssm_conv_b, m_ssm_A_log, m_ssm_dt_bias, m_ssm_D, m_ssm_norm, m_c_q_norm, m_c_k_norm, m_w_oa, m_w_ob, m_w_oc, m_w_out, m_ffn_w_up, m_ffn_w_gate, m_ffn_conv_w, m_ffn_conv_b, m_ffn_w_down, m_final_norm, v_c_ctx, v_w_mod, v_b_mod, v_norm1, v_norm2, v_w_in, v_a_sink, v_ssm_conv_w, v_ssm_conv_b, v_ssm_A_log, v_ssm_dt_bias, v_ssm_D, v_ssm_norm, v_c_q_norm, v_c_k_norm, v_w_oa, v_w_ob, v_w_oc, v_w_out, v_ffn_w_up, v_ffn_w_gate, v_ffn_conv_w, v_ffn_conv_b, v_ffn_w_down, v_final_norm):
    args = (x, c, ctx, c_ctx, w_mod, b_mod, norm1, norm2, w_in, a_sink, ssm_conv_w, ssm_conv_b, ssm_A_log, ssm_dt_bias, ssm_D, ssm_norm, c_q_norm, c_k_norm, w_oa, w_ob, w_oc, w_out, ffn_w_up, ffn_w_gate, ffn_conv_w, ffn_conv_b, ffn_w_down, final_norm, loss_target)
    moms = (m_c_ctx, m_w_mod, m_b_mod, m_norm1, m_norm2, m_w_in, m_a_sink, m_ssm_conv_w, m_ssm_conv_b, m_ssm_A_log, m_ssm_dt_bias, m_ssm_D, m_ssm_norm, m_c_q_norm, m_c_k_norm, m_w_oa, m_w_ob, m_w_oc, m_w_out, m_ffn_w_up, m_ffn_w_gate, m_ffn_conv_w, m_ffn_conv_b, m_ffn_w_down, m_final_norm)
    vars_ = (v_c_ctx, v_w_mod, v_b_mod, v_norm1, v_norm2, v_w_in, v_a_sink, v_ssm_conv_w, v_ssm_conv_b, v_ssm_A_log, v_ssm_dt_bias, v_ssm_D, v_ssm_norm, v_c_q_norm, v_c_k_norm, v_w_oa, v_w_ob, v_w_oc, v_w_out, v_ffn_w_up, v_ffn_w_gate, v_ffn_conv_w, v_ffn_conv_b, v_ffn_w_down, v_final_norm)
    p = dict(zip(IN_NAMES, args))
    mom = dict(zip(WEIGHTS, moms))
    var = dict(zip(WEIGHTS, vars_))
    depth = w_in.shape[0]
    n_ctx = ctx.shape[1]
    xi, yi, ci = lax.axis_index("x"), lax.axis_index("y"), lax.axis_index("c")
    dev = 4 * xi + 2 * yi + ci
    big_names = list(BIG)

    assert depth == 2 and n_ctx == ROW_TILE
    shards = [{n: p[n][li].astype(bf16) for n in big_names} for li in range(depth)]
    g_big0 = all_gather([shards[0][n] for n in big_names], "gather_l0")
    g_conv = all_gather([_to_flat(_pack(p, CONV_W))], "gather_conv")[0].reshape(8, -1)
    big0 = {n: _assemble(n, g) for n, g in zip(big_names, g_big0)}
    stand_ins = {n: jnp.zeros((8,) + shards[1][n].shape, bf16) for n in big_names}
    conv_full, off = {}, 0
    for n in CONV_W:
        shp = p[n].shape
        seg = g_conv[:, off:off + p[n].size].reshape(8, *shp)
        conv_full[n] = jnp.moveaxis(seg, 0, -2).reshape(*shp[:-1], 8 * shp[-1])
        off += p[n].size
    small = {n: p[n] for n in REPL}
    small.update(conv_full)

    loss, (g_big0, g_big1, g_small, g_x) = jax.value_and_grad(_loss_fn, argnums=(0, 2, 3, 4))(
        big0, shards[1], stand_ins, small, x[0], ctx[0], c, loss_target[0], n_ctx)
    loss = lax.psum(loss, AXES)

    def send_rows(n):
        b = p[n].shape[-1] if BIG[n] == 1 else p[n].shape[1]
        cut = (lambda g, j: g[:, b * j:b * (j + 1)]) if BIG[n] == 1 else (lambda g, j: g[b * j:b * (j + 1), :])
        return jnp.stack([cut(g_big0[n], j) for j in range(8)])

    send = [send_rows(n) for n in big_names]
    from_sibling = rs_to_sibling(send)
    my_c = ci.reshape(1).astype(jnp.int32)
    side_sum = [pair_sum(s, r, my_c, "rs_pair_sum_" + n) for n, s, r in zip(big_names, send, from_sibling)]
    from_chips = rs_to_chips(side_sum)
    chip = (2 * xi + yi).reshape(1).astype(jnp.int32)
    big_out = [{}, {}, {}, {}]
    for a, n in enumerate(big_names):
        parts = [[(side_sum[a], None, chip), (from_chips[a], (0, 1, 2), None)],
                 [(g_big1[n], None, chip), (g_big1[n], (4, 5, 6), None)]]
        outs = sum_adam(parts, p[n], mom[n], var[n], "adam_" + n)
        for k in range(4):
            big_out[k][n] = outs[k]

    sm_names = REPL + list(CONV_W)
    g_vec = _to_flat(_pack(g_small, sm_names))
    gathered = all_gather([g_vec], "gather_small_grads")[0]
    n_repl = sum(p[n].size for n in REPL)

    def repl_flat(tree):
        return _to_flat(jnp.concatenate([_pack(tree, REPL), jnp.zeros((g_vec.size - n_repl,), f32)]))

    outs_small = sum_adam([[(gathered, tuple(range(8)), None)]], repl_flat(p)[None], repl_flat(mom)[None], repl_flat(var)[None],
                          "adam_small")
    g_sum = outs_small[0].reshape(-1)
    small_out = [_unpack(o.reshape(-1), p, REPL) for o in outs_small]
    conv_g_full = _unpack(g_sum[n_repl:], conv_full, CONV_W)
    conv_g = {n: lax.dynamic_slice_in_dim(conv_g_full[n], dev * p[n].shape[-1], p[n].shape[-1], axis=2) for n in CONV_W}
    conv_gv = _to_flat(_pack(conv_g, CONV_W))
    outs_conv = sum_adam([[(conv_gv[None], (0,), None)]], _to_flat(_pack(p, CONV_W))[None], _to_flat(_pack(mom, CONV_W))[None],
                         _to_flat(_pack(var, CONV_W))[None], "adam_conv")
    conv_out = [_unpack(o.reshape(-1), p, CONV_W) for o in outs_conv]

    res = []
    for k in range(4):
        tree = {**big_out[k], **small_out[k], **conv_out[k]}
        res.append([tree[n] for n in WEIGHTS])
    return (loss, g_x[None], *res[0], *res[1], *res[2], *res[3])
```

```python
import functools

import jax
import jax.numpy as jnp
from jax import lax
from jax.experimental import pallas as pl
from jax.experimental.pallas import tpu as pltpu

f32 = jnp.float32
bf16 = jnp.bfloat16
MESH = pl.DeviceIdType.MESH
AXES = ("x", "y", "c")

GRID_W = 64
HEAD_DIM = 64
ROPE_BASE = 10000.0
EPS = 1e-6
WINDOW = 128
N_HEADS = 8
N_KV = 2
SSM_HEADS = 16
SSM_P = 64
SSM_G = 2
SSM_N = 128
SSM_INNER = SSM_HEADS * SSM_P
SSM_BC = SSM_G * SSM_N
SSM_Q = 128
Q_W = N_HEADS * HEAD_DIM
KV_W = N_KV * HEAD_DIM
DT_W = 2 * SSM_HEADS
DT_PAD = 128
ADAM_LR, ADAM_B1, ADAM_B2, ADAM_EPS, ADAM_WD, ADAM_STEP = 0.001, 0.9, 0.999, 1e-08, 0.01, 10

LANES = 128
ROW_TILE = 256
VMEM_BLOCK_BUDGET = 6 * 1024 * 1024
ATTN_SLAB = 128
MM_ROW_CAP = 1088
MM_TILE_CAP = 1536
NEG = -1e30

IN_NAMES = ['x', 'c', 'ctx', 'c_ctx', 'w_mod', 'b_mod', 'norm1', 'norm2', 'w_in', 'a_sink', 'ssm_conv_w', 'ssm_conv_b', 'ssm_A_log', 'ssm_dt_bias', 'ssm_D', 'ssm_norm', 'c_q_norm', 'c_k_norm', 'w_oa', 'w_ob', 'w_oc', 'w_out', 'ffn_w_up', 'ffn_w_gate', 'ffn_conv_w', 'ffn_conv_b', 'ffn_w_down', 'final_norm', 'loss_target']
WEIGHTS = IN_NAMES[3:28]
BIG = {'w_mod': 1, 'w_in': 1, 'w_oa': 1, 'w_ob': 0, 'w_oc': 1, 'w_out': 0, 'ffn_w_up': 1, 'ffn_w_gate': 1, 'ffn_w_down': 0}
CONV_W = ('ssm_conv_w', 'ffn_conv_w')
REPL = [n for n in WEIGHTS if n not in BIG and n not in CONV_W]

NT = (((1,), (1,)), ((), ()))
TN = (((0,), (0,)), ((), ()))
NN = (((1,), (0,)), ((), ()))


def _cparams(*sem):
    return pltpu.CompilerParams(dimension_semantics=sem)


def _div_tile(n, unit, cap):
    for d in range(min(n, int(cap)), 0, -1):
        if n % d == 0 and d % unit == 0:
            return d
    return n


def _row_tile(m, row_bytes):
    return _div_tile(m, 16, max(16, VMEM_BLOCK_BUDGET // row_bytes))


def _mm_call(a, b, mode, out_dtype, name):
    if mode == "nn":
        (m, k), n = a.shape, b.shape[1]
    elif mode == "nt":
        (m, k), n = a.shape, b.shape[0]
    else:
        (k, m), n = a.shape, b.shape[1]
    dims = {"nn": NN, "nt": NT, "tn": TN}[mode]
    ia, ib = a.dtype.itemsize, b.dtype.itemsize
    tm = _div_tile(m, LANES, MM_TILE_CAP) if mode == "tn" else _div_tile(m, 16, MM_ROW_CAP)
    tn = _div_tile(n, LANES, min(MM_TILE_CAP, VMEM_BLOCK_BUDGET // (4 * tm)))
    tk = _div_tile(k, 16 if mode == "tn" else LANES,
                   min(MM_ROW_CAP if mode == "tn" else MM_TILE_CAP, VMEM_BLOCK_BUDGET // (tm * ia), VMEM_BLOCK_BUDGET // (tn * ib)))
    nk = k // tk

    def body(a_ref, b_ref, o_ref, *acc):
        part = lax.dot_general(a_ref[...].astype(bf16), b_ref[...].astype(bf16), dims, preferred_element_type=f32)
        if nk == 1:
            o_ref[...] = part.astype(o_ref.dtype)
            return
        kk = pl.program_id(2)

        @pl.when(kk == 0)
        def _():
            acc[0][...] = part

        @pl.when(kk > 0)
        def _():
            acc[0][...] += part

        @pl.when(kk == nk - 1)
        def _():
            o_ref[...] = acc[0][...].astype(o_ref.dtype)

    a_spec = pl.BlockSpec((tk, tm), lambda i, j, kk: (kk, i)) if mode == "tn" else pl.BlockSpec((tm, tk), lambda i, j, kk: (i, kk))
    b_spec = pl.BlockSpec((tn, tk), lambda i, j, kk: (j, kk)) if mode == "nt" else pl.BlockSpec((tk, tn), lambda i, j, kk: (kk, j))
    return pl.pallas_call(
        body, grid=(m // tm, n // tn, nk), in_specs=[a_spec, b_spec],
        out_specs=pl.BlockSpec((tm, tn), lambda i, j, kk: (i, j)),
        out_shape=jax.ShapeDtypeStruct((m, n), out_dtype),
        scratch_shapes=[pltpu.VMEM((tm, tn), f32)] if nk > 1 else [], name=name,
        compiler_params=_cparams("parallel", "parallel", "arbitrary"))(a, b)


def mm(a, b, name, out_dtype=None):
    @jax.custom_vjp
    def op(a, b):
        return _mm_call(a, b, "nn", out_dtype or bf16, name)

    def fwd(a, b):
        return op(a, b), (a, b)

    def bwd(res, g):
        a, b = res
        return _mm_call(g, b, "nt", a.dtype, name + "_da"), _mm_call(a, g, "tn", b.dtype, name + "_db")

    op.defvjp(fwd, bwd)
    return op(a, b)


def split_cols(u, widths):
    offs = [0]
    for w in widths:
        offs.append(offs[-1] + w)

    @jax.custom_vjp
    def op(u):
        return tuple(u[:, offs[i]:offs[i + 1]] for i in range(len(widths)))

    def fwd(u):
        return op(u), None

    def bwd(_, cts):
        return (jnp.concatenate(cts, axis=1),)

    op.defvjp(fwd, bwd)
    return op(u)


def rowwise(fn, rows, consts, pars, out_widths, out_dtypes, name):
    t = rows[0].shape[0]
    tm = ROW_TILE
    nb = t // tm
    nr, nc, npar = len(rows), len(consts), len(pars)

    def rspec(a):
        return pl.BlockSpec((tm, a.shape[1]), lambda i: (i, 0))

    def pspec(a):
        return pl.BlockSpec(a.shape, lambda i: (0,) * a.ndim)

    def call_fwd(rows, consts, pars):
        def body(*refs):
            blk = pl.program_id(0)
            ins = [r[...].astype(f32) for r in refs[:nr + nc]]
            ps = [r[...] for r in refs[nr + nc:nr + nc + npar]]
            outs = fn(blk, *ins, *ps)
            for o_ref, o in zip(refs[nr + nc + npar:], outs):
                o_ref[...] = o.astype(o_ref.dtype)

        return pl.pallas_call(
            body, grid=(nb,),
            in_specs=[rspec(a) for a in rows + consts] + [pspec(a) for a in pars],
            out_specs=[pl.BlockSpec((tm, w), lambda i: (i, 0)) for w in out_widths],
            out_shape=[jax.ShapeDtypeStruct((t, w), d) for w, d in zip(out_widths, out_dtypes)],
            name=name, compiler_params=_cparams("parallel"))(*rows, *consts, *pars)

    def call_bwd(rows, consts, pars, cts):
        nout = len(cts)

        def body(*refs):
            blk = pl.program_id(0)
            ins = [r[...].astype(f32) for r in refs[:nr]]
            cs = [r[...].astype(f32) for r in refs[nr:nr + nc]]
            ps = [r[...] for r in refs[nr + nc:nr + nc + npar]]
            dys = [r[...].astype(f32) for r in refs[nr + nc + npar:nr + nc + npar + nout]]
            d_refs = refs[nr + nc + npar + nout:]
            _, vjp = jax.vjp(lambda *a: tuple(fn(blk, *a[:nr], *cs, *a[nr:])), *ins, *ps)
            grads = vjp(tuple(dys))
            for d_ref, g in zip(d_refs[:nr], grads[:nr]):
                d_ref[...] = g.astype(d_ref.dtype)
            if npar:
                @pl.when(blk == 0)
                def _():
                    for d_ref in d_refs[nr:]:
                        d_ref[...] = jnp.zeros_like(d_ref)

                for d_ref, g in zip(d_refs[nr:], grads[nr:]):
                    d_ref[...] += g

        return pl.pallas_call(
            body, grid=(nb,),
            in_specs=[rspec(a) for a in rows + consts] + [pspec(a) for a in pars] + [rspec(a) for a in cts],
            out_specs=[rspec(a) for a in rows] + [pspec(a) for a in pars],
            out_shape=[jax.ShapeDtypeStruct(a.shape, a.dtype) for a in rows + pars],
            name=name + "_bwd", compiler_params=_cparams("arbitrary"))(*rows, *consts, *pars, *cts)

    @jax.custom_vjp
    def op(rows, consts, pars):
        return tuple(call_fwd(list(rows), list(consts), list(pars)))

    def fwd(rows, consts, pars):
        return op(rows, consts, pars), (rows, consts, pars)

    def bwd(res, cts):
        rows, consts, pars = res
        g = call_bwd(list(rows), list(consts), list(pars), list(cts))
        return tuple(g[:nr]), tuple(jnp.zeros_like(a) for a in consts), tuple(g[nr:])

    op.defvjp(fwd, bwd)
    return op(tuple(rows), tuple(consts), tuple(pars))


def colwise(fn, cols, pars, out_dtype, name):
    t, w = cols[0].shape
    tc = LANES
    nb = w // tc
    ncol, npar = len(cols), len(pars)

    def cspec(a):
        return pl.BlockSpec((a.shape[0], tc), lambda j: (0, j))

    def call_fwd(cols, pars):
        def body(*refs):
            ins = [r[...].astype(f32) for r in refs[:ncol]]
            ps = [r[...] for r in refs[ncol:ncol + npar]]
            refs[-1][...] = fn(*ins, *ps).astype(refs[-1].dtype)

        return pl.pallas_call(
            body, grid=(nb,), in_specs=[cspec(a) for a in cols + pars], out_specs=cspec(cols[0]),
            out_shape=jax.ShapeDtypeStruct((t, w), out_dtype), name=name, compiler_params=_cparams("parallel"))(*cols, *pars)

    def call_bwd(cols, pars, ct):
        def body(*refs):
            ins = [r[...].astype(f32) for r in refs[:ncol]]
            ps = [r[...] for r in refs[ncol:ncol + npar]]
            dy = refs[ncol + npar][...].astype(f32)
            d_refs = refs[ncol + npar + 1:]
            _, vjp = jax.vjp(fn, *ins, *ps)
            grads = vjp(dy)
            for d_ref, g in zip(d_refs, grads):
                d_ref[...] = g.astype(d_ref.dtype)

        return pl.pallas_call(
            body, grid=(nb,), in_specs=[cspec(a) for a in cols + pars + [ct]],
            out_specs=[cspec(a) for a in cols + pars],
            out_shape=[jax.ShapeDtypeStruct(a.shape, a.dtype) for a in cols + pars],
            name=name + "_bwd", compiler_params=_cparams("parallel"))(*cols, *pars, ct)

    @jax.custom_vjp
    def op(cols, pars):
        return call_fwd(list(cols), list(pars))

    def fwd(cols, pars):
        return op(cols, pars), (cols, pars)

    def bwd(res, ct):
        cols, pars = res
        g = call_bwd(list(cols), list(pars), ct)
        return tuple(g[:ncol]), tuple(g[ncol:])

    op.defvjp(fwd, bwd)
    return op(tuple(cols), tuple(pars))


def _sigmoid(x):
    return 1.0 / (1.0 + jnp.exp(-x))


def _silu(x):
    return x * _sigmoid(x)


def _rms(x, g):
    return x * lax.rsqrt(jnp.mean(x * x, axis=-1, keepdims=True) + EPS) * g


def _shift_rows(u, k, n_ctx):
    @jax.custom_vjp
    def op(u):
        t = u.shape[0]
        row = lax.broadcasted_iota(jnp.int32, u.shape, 0)
        edge = ((row == 0) | (row == n_ctx)) if k == 1 else ((row == n_ctx - 1) | (row == t - 1))
        return jnp.where(edge, 0.0, pltpu.roll(u, k % t, 0))

    op.defvjp(lambda u: (op(u), None), lambda _, g: (_shift_rows(g, -k, n_ctx),))
    return op(u)


def _dwconv(u, w0, w1, w2, b, n_ctx):
    return w0 * _shift_rows(u, 1, n_ctx) + w1 * u + w2 * _shift_rows(u, -1, n_ctx) + b


@jax.custom_vjp
def _rot_half(x):
    w = x.shape[1]
    lane = lax.broadcasted_iota(jnp.int32, x.shape, 1)
    return jnp.where((lane % HEAD_DIM) < HEAD_DIM // 2, pltpu.roll(x, w - HEAD_DIM // 2, 1), pltpu.roll(x, HEAD_DIM // 2, 1))


_rot_half.defvjp(lambda x: (_rot_half(x), None), lambda _, g: (_rot_half(g),))


def _head_rms(x, g):
    w = x.shape[1]
    same = (lax.broadcasted_iota(jnp.int32, (w, w), 0) // HEAD_DIM) == (lax.broadcasted_iota(jnp.int32, (w, w), 1) // HEAD_DIM)
    ms = jnp.dot(x * x, same.astype(f32), precision=lax.Precision.HIGHEST, preferred_element_type=f32) * (1.0 / HEAD_DIM)
    return x * lax.rsqrt(ms + EPS) * g


def _band_ok(i, j, c0, shape, tq, tk):
    kpos = j * tk + lax.broadcasted_iota(jnp.int32, shape, 0)
    qpos = i * tq + (c0 + lax.broadcasted_iota(jnp.int32, shape, 1)) % tq
    return jnp.abs(qpos - kpos) <= WINDOW


def _kv_range(i, nb, window):
    is_ctx = i == 0
    if window:
        return jnp.where(is_ctx, 1, jnp.maximum(i - 1, 1)), jnp.where(is_ctx, 1, jnp.minimum(i + 2, nb))
    return 1, jnp.where(is_ctx, 1, nb)


def _sink_row(sink_ref, g, r, tq):
    return jnp.concatenate([jnp.full((1, tq), sink_ref[g * r + h], f32) for h in range(r)], axis=1)


def _attn_fwd_call(q, k, v, sink, window, name, shards=()):
    h, t, dh = q.shape
    nkv = k.shape[0]
    r = h // nkv
    tq = tk = ROW_TILE
    nb = t // tq
    rows = r * tq
    ns = len(shards)

    assert window or nb % 2 == 1, "the dense schedule takes the kv chunks after the context chunk in pairs"

    def body(sink_ref, q_ref, k_ref, v_ref, *rest):
        x_refs, (o_ref, lse_ref), gathered_refs = rest[:ns], rest[ns:ns + 2], rest[ns + 2:2 * ns + 2]
        m_scr, l_scr, acc_scr, s_a, s_b, p_a, p_b, a_a, a_b = rest[2 * ns + 2:2 * ns + 11]
        comm_sems = rest[2 * ns + 11:]
        g, i = pl.program_id(0), pl.program_id(1)
        if ns:
            @pl.when((g == 0) & (i == 0))
            def _():
                _gather_steps(x_refs, gathered_refs, *comm_sems)[0]()

        qv = q_ref[...].reshape(rows, dh)
        m_scr[...] = jnp.full_like(m_scr, NEG)
        l_scr[...] = jnp.zeros_like(l_scr)
        acc_scr[...] = jnp.zeros_like(acc_scr)

        def kv_rows(j):
            return pl.ds(pl.multiple_of(jnp.minimum(j, nb - 1) * tk, tk), tk)

        def scores(j, s_scr):
            s_scr[...] = lax.dot_general(k_ref[0, kv_rows(j), :], qv, NT, preferred_element_type=f32)

        def softmax(j, s_scr, p_scr, a_scr, masked):
            for cb in range(rows // ATTN_SLAB):
                cs = slice(cb * ATTN_SLAB, (cb + 1) * ATTN_SLAB)
                s = s_scr[:, cs]
                if masked:
                    s = jnp.where(_band_ok(i, j, cb * ATTN_SLAB, s.shape, tq, tk), s, NEG)
                m = m_scr[:, cs]
                m2 = jnp.maximum(m, jnp.max(s, axis=0, keepdims=True))
                p = jnp.exp(s - m2)
                a = jnp.exp(m - m2)
                l_scr[:, cs] = a * l_scr[:, cs] + jnp.sum(p, axis=0, keepdims=True)
                m_scr[:, cs] = m2
                a_scr[:, cs] = a
                p_scr[:, cs] = p.astype(bf16)

        def weighted_v(j, p_scr, a_scr):
            acc_scr[...] = a_scr[...] * acc_scr[...] + lax.dot_general(v_ref[0, kv_rows(j), :], p_scr[...], TN, preferred_element_type=f32)

        scores(0, s_a)
        softmax(0, s_a, p_a, a_a, False)
        if window:
            weighted_v(0, p_a, a_a)
            lo, hi = _kv_range(i, nb, window)

            def chunk(j, c):
                scores(j, s_a)
                softmax(j, s_a, p_a, a_a, True)
                weighted_v(j, p_a, a_a)
                return c

            lax.fori_loop(lo, hi, chunk, 0)
        else:
            scores(1, s_b)

            def pair(tt, c):
                j0 = 2 * tt + 1
                scores(j0 + 1, s_a)
                weighted_v(j0 - 1, p_a, a_a)
                softmax(j0, s_b, p_b, a_b, False)
                scores(j0 + 2, s_b)
                weighted_v(j0, p_b, a_b)
                softmax(j0 + 1, s_a, p_a, a_a, False)
                return c

            lax.fori_loop(0, jnp.where(i == 0, 0, (nb - 1) // 2), pair, 0)
            weighted_v(jnp.where(i == 0, 0, nb - 1), p_a, a_a)
        m, l, acc = m_scr[...], l_scr[...], acc_scr[...]
        if window:
            sk = _sink_row(sink_ref, g, r, tq)
            m2 = jnp.maximum(m, sk)
            a = jnp.exp(m - m2)
            l = a * l + jnp.exp(sk - m2)
            acc = a * acc
            m = m2
        o_ref[...] = (acc / l).T.reshape(r, tq, dh).astype(o_ref.dtype)
        lse_ref[0] = m + jnp.log(l)
        if ns:
            @pl.when((g == nkv - 1) & (i == nb - 1))
            def _():
                _gather_steps(x_refs, gathered_refs, *comm_sems)[1]()

    qspec = pl.BlockSpec((r, tq, dh), lambda g, i: (g, i, 0))
    kspec = pl.BlockSpec((1, t, dh), lambda g, i: (g, 0, 0))
    hbm = pl.BlockSpec(memory_space=pl.ANY)
    sem = ("arbitrary", "arbitrary") if ns else ("parallel", "parallel")
    return pl.pallas_call(
        body, grid=(nkv, nb),
        in_specs=[pl.BlockSpec(memory_space=pltpu.SMEM), qspec, kspec, kspec] + [hbm] * ns,
        out_specs=[qspec, pl.BlockSpec((1, 1, rows), lambda g, i: (g * nb + i, 0, 0))] + [hbm] * ns,
        out_shape=[jax.ShapeDtypeStruct((h, t, dh), bf16), jax.ShapeDtypeStruct((nkv * nb, 1, rows), f32)]
        + [jax.ShapeDtypeStruct((8,) + s.shape, s.dtype) for s in shards],
        scratch_shapes=[pltpu.VMEM((1, rows), f32), pltpu.VMEM((1, rows), f32), pltpu.VMEM((dh, rows), f32),
                        pltpu.VMEM((tk, rows), f32), pltpu.VMEM((tk, rows), f32), pltpu.VMEM((tk, rows), bf16),
                        pltpu.VMEM((tk, rows), bf16), pltpu.VMEM((1, rows), f32), pltpu.VMEM((1, rows), f32)]
        + (_gather_scratch(ns) if ns else []),
        name=name, compiler_params=_cparams(*sem))(sink, q, k, v, *shards)


def _attn_bwd_call(q, k, v, sink, o, lse, do, window, name, side_sums=()):
    h, t, dh = q.shape
    nkv = k.shape[0]
    r = h // nkv
    tq = tk = ROW_TILE
    nb = t // tq
    rows = r * tq
    ns = len(side_sums)

    def body(sink_ref, q_ref, k_ref, v_ref, o_ref, lse_ref, do_ref, *rest):
        ss_refs, (dq_ref, dk_ref, dv_ref, dsink_ref), got_refs = rest[:ns], rest[ns:ns + 4], rest[ns + 4:2 * ns + 4]
        s_scr, dp_scr, p_scr, ds_scr, dq_scr = rest[2 * ns + 4:2 * ns + 9]
        comm_sems = rest[2 * ns + 9:]
        g, i = pl.program_id(0), pl.program_id(1)
        if ns:
            @pl.when((g == 0) & (i == 0))
            def _():
                for cp in _chips_copies(ss_refs, got_refs, *comm_sems):
                    cp.start()

        @pl.when(i == 0)
        def _():
            dk_ref[...] = jnp.zeros_like(dk_ref)
            dv_ref[...] = jnp.zeros_like(dv_ref)

        qv = q_ref[...].reshape(rows, dh)
        dov = do_ref[...].reshape(rows, dh)
        lse_t = lse_ref[0]
        delta_t = jnp.sum((dov.astype(f32) * o_ref[...].reshape(rows, dh).astype(f32)).T, axis=0, keepdims=True)
        dq_scr[...] = jnp.zeros_like(dq_scr)

        def chunk(j, masked):
            kv_rows = pl.ds(pl.multiple_of(j * tk, tk), tk)
            kj = k_ref[0, kv_rows, :]
            vj = v_ref[0, kv_rows, :]
            s_scr[...] = lax.dot_general(kj, qv, NT, preferred_element_type=f32)
            dp_scr[...] = lax.dot_general(vj, dov, NT, preferred_element_type=f32)
            for cb in range(rows // ATTN_SLAB):
                cs = slice(cb * ATTN_SLAB, (cb + 1) * ATTN_SLAB)
                s = s_scr[:, cs]
                if masked:
                    s = jnp.where(_band_ok(i, j, cb * ATTN_SLAB, s.shape, tq, tk), s, NEG)
                p = jnp.exp(s - lse_t[:, cs])
                p_scr[:, cs] = p.astype(bf16)
                ds_scr[:, cs] = (p * (dp_scr[:, cs] - delta_t[:, cs])).astype(bf16)
            dv_ref[0, kv_rows, :] += jnp.dot(p_scr[...], dov, preferred_element_type=f32)
            dk_ref[0, kv_rows, :] += jnp.dot(ds_scr[...], qv, preferred_element_type=f32)
            dq_scr[...] += lax.dot_general(kj, ds_scr[...], TN, preferred_element_type=f32)

        chunk(0, False)
        lo, hi = _kv_range(i, nb, window)
        lax.fori_loop(lo, hi, lambda j, c: (chunk(j, window), c)[1], 0)
        dq_ref[...] = dq_scr[...].T.reshape(r, tq, dh).astype(dq_ref.dtype)
        if window:
            dsink_ref[0] = -jnp.exp(_sink_row(sink_ref, g, r, tq) - lse_t) * delta_t
        else:
            dsink_ref[...] = jnp.zeros_like(dsink_ref)
        if ns:
            @pl.when((g == nkv - 1) & (i == nb - 1))
            def _():
                for cp in _chips_copies(ss_refs, got_refs, *comm_sems):
                    cp.wait()

    qspec = pl.BlockSpec((r, tq, dh), lambda g, i: (g, i, 0))
    cspec = pl.BlockSpec((1, 1, rows), lambda g, i: (g * nb + i, 0, 0))
    kspec = pl.BlockSpec((1, t, dh), lambda g, i: (g, 0, 0))
    hbm = pl.BlockSpec(memory_space=pl.ANY)
    return pl.pallas_call(
        body, grid=(nkv, nb),
        in_specs=[pl.BlockSpec(memory_space=pltpu.SMEM), qspec, kspec, kspec, qspec, cspec, qspec] + [hbm] * ns,
        out_specs=[qspec, kspec, kspec, cspec] + [hbm] * ns,
        out_shape=[jax.ShapeDtypeStruct((h, t, dh), bf16), jax.ShapeDtypeStruct(k.shape, f32), jax.ShapeDtypeStruct(v.shape, f32),
                   jax.ShapeDtypeStruct((nkv * nb, 1, rows), f32)] + [jax.ShapeDtypeStruct((3,) + s.shape[1:], s.dtype) for s in side_sums],
        scratch_shapes=[pltpu.VMEM((tk, rows), f32), pltpu.VMEM((tk, rows), f32), pltpu.VMEM((tk, rows), bf16),
                        pltpu.VMEM((tk, rows), bf16), pltpu.VMEM((dh, rows), f32)]
        + ([pltpu.SemaphoreType.DMA((3 * ns,)), pltpu.SemaphoreType.DMA((3 * ns,))] if ns else []),
        name=name, compiler_params=_cparams("arbitrary" if ns else "parallel", "arbitrary"))(sink, q, k, v, o, lse, do, *side_sums)


def attention(q, k, v, sink, window, name, shards=(), stand_ins=()):
    @jax.custom_vjp
    def op(q, k, v, sink, shards, stand_ins):
        o, _, *gathered = _attn_fwd_call(q, k, v, sink, window, name, shards)
        return o, tuple(gathered)

    def fwd(q, k, v, sink, shards, stand_ins):
        o, lse, *gathered = _attn_fwd_call(q, k, v, sink, window, name, shards)
        return (o, tuple(gathered)), (q, k, v, sink, o, lse, shards)

    def bwd(res, cts):
        q, k, v, sink, o, lse, shards = res
        do, d_gathered = cts
        side_sums = []
        if shards:
            my_c = lax.axis_index("c").reshape(1).astype(jnp.int32)
            from_sibling = rs_to_sibling(list(d_gathered), name + "_rs_sibling")
            side_sums = [pair_sum(s, rr, my_c, f"{name}_pair_sum{a}") for a, (s, rr) in enumerate(zip(d_gathered, from_sibling))]
        dq, dk, dv, dsink_rows, *from_chips = _attn_bwd_call(q, k, v, sink, o, lse, do, window, name + "_bwd", side_sums)
        nkv, r = k.shape[0], q.shape[0] // k.shape[0]
        dsink = jnp.sum(dsink_rows.reshape(nkv, -1, r, ROW_TILE), axis=(1, 3)).reshape(nkv * r)
        reduced = tuple(jnp.concatenate([s, fc, jnp.zeros_like(s[:1])], axis=0) for s, fc in zip(side_sums, from_chips))
        return dq, dk.astype(k.dtype), dv.astype(v.dtype), dsink, tuple(jnp.zeros_like(s) for s in shards), reduced

    op.defvjp(fwd, bwd)
    return op(q, k, v, sink, tuple(shards), tuple(stand_ins))


def _ssd_chunk(xs, dtx, dtr, ac, bs, cs, hin, rev):
    q = xs[0].shape[0]
    ii = lax.broadcasted_iota(jnp.int32, (q, q), 0)
    jj = lax.broadcasted_iota(jnp.int32, (q, q), 1)
    tri = (ii <= jj) if rev else (ii >= jj)
    lo = lax.broadcasted_iota(jnp.int32, (q, LANES), 1) < SSM_P
    lo_row = lax.broadcasted_iota(jnp.int32, (1, LANES), 1) < SSM_P
    heads, slabs, per_group = range(SSM_HEADS), range(SSM_HEADS // 2), SSM_HEADS // 2 // SSM_G

    a = [dtr[h] * ac[h] for h in heads]
    c = [jnp.sum(jnp.where(tri, jnp.broadcast_to(a[h], (q, q)), 0.0), axis=1, keepdims=True) for h in heads]
    tot = [jnp.sum(a[h], axis=1, keepdims=True) for h in heads]
    cf = [jnp.broadcast_to(c[h], (q, q)) for h in heads]
    seg = [jnp.minimum(cf[h] - cf[h].T, 0.0) for h in heads]
    decay = [jnp.where(tri, jnp.exp(seg[h]), 0.0) for h in heads]
    cb = [lax.dot_general(cs[g].astype(bf16), bs[g].astype(bf16), NT, preferred_element_type=f32) for g in range(SSM_G)]
    m = [jnp.concatenate([cb[j // per_group] * decay[2 * j], cb[j // per_group] * decay[2 * j + 1]], axis=1).astype(bf16) for j in slabs]
    xdt = [xs[j] * dtx[j] for j in slabs]
    x2 = [jnp.concatenate([jnp.where(lo, xdt[j], 0.0), jnp.where(lo, 0.0, xdt[j])], axis=0).astype(bf16) for j in slabs]
    y_diag = [jnp.dot(m[j], x2[j], preferred_element_type=f32) for j in slabs]
    csel = [jnp.where(lo, cf[2 * j], cf[2 * j + 1]) for j in slabs]
    tsel = [jnp.where(lo_row, jnp.broadcast_to(tot[2 * j], (1, LANES)), jnp.broadcast_to(tot[2 * j + 1], (1, LANES))) for j in slabs]
    xend = [(xdt[j] * jnp.exp(tsel[j] - csel[j])).astype(bf16) for j in slabs]
    st = [lax.dot_general(bs[j // per_group].astype(bf16), xend[j], TN, preferred_element_type=f32) for j in slabs]
    y_off = [jnp.dot(cs[j // per_group].astype(bf16), hin[j].astype(bf16), preferred_element_type=f32) * jnp.exp(csel[j]) for j in slabs]
    return [y_diag[j] + y_off[j] for j in slabs], [hin[j] * jnp.exp(tsel[j]) + st[j] for j in slabs]


def _ssd_order(s, nc, ncc, rev):
    if not rev:
        return s
    return jnp.where(s < ncc, ncc - 1 - s, nc - 1 - (s - ncc))


SSD_SLABS = [slice(LANES * j, LANES * (j + 1)) for j in range(SSM_HEADS // 2)]
SSD_GROUPS = [slice(SSM_N * g, SSM_N * (g + 1)) for g in range(SSM_G)]


def _head_lanes(w, transpose=False):
    shape = (w, SSM_HEADS) if transpose else (SSM_HEADS, w)
    head = lax.broadcasted_iota(jnp.int32, shape, 1 if transpose else 0)
    lane = lax.broadcasted_iota(jnp.int32, shape, 0 if transpose else 1)
    return (lane // SSM_P == head).astype(f32)


def _ssd_fwd_call(xs, dt, dtr, bm, cm, acol, rev, n_ctx, name):
    t, w = xs.shape
    q = SSM_Q
    nc, ncc = t // q, n_ctx // q

    def body(xs_ref, dt_ref, dtr_ref, b_ref, c_ref, a_ref, y_ref, hin_ref, h_scr):
        @pl.when(pl.program_id(0) == 0)
        def _():
            h_scr[...] = jnp.zeros_like(h_scr)

        hin_ref[0] = h_scr[...]
        dtx = jnp.dot(dt_ref[...], _head_lanes(w), precision=lax.Precision.HIGHEST, preferred_element_type=f32)
        ys, houts = _ssd_chunk([xs_ref[:, sl] for sl in SSD_SLABS], [dtx[:, sl] for sl in SSD_SLABS],
                               [dtr_ref[h:h + 1, :] for h in range(SSM_HEADS)], [a_ref[h:h + 1, :] for h in range(SSM_HEADS)],
                               [b_ref[:, gs] for gs in SSD_GROUPS], [c_ref[:, gs] for gs in SSD_GROUPS],
                               [h_scr[:, sl] for sl in SSD_SLABS], rev)
        for sl, y, hout in zip(SSD_SLABS, ys, houts):
            y_ref[:, sl] = y.astype(y_ref.dtype)
            h_scr[:, sl] = hout

    def at(s):
        return _ssd_order(s, nc, ncc, rev)

    return pl.pallas_call(
        body, grid=(nc,),
        in_specs=[pl.BlockSpec((q, w), lambda s: (at(s), 0)), pl.BlockSpec((q, SSM_HEADS), lambda s: (at(s), 0)),
                  pl.BlockSpec((SSM_HEADS, q), lambda s: (0, at(s))),
                  pl.BlockSpec((q, SSM_BC), lambda s: (at(s), 0)), pl.BlockSpec((q, SSM_BC), lambda s: (at(s), 0)),
                  pl.BlockSpec((SSM_HEADS, 1), lambda s: (0, 0))],
        out_specs=[pl.BlockSpec((q, w), lambda s: (at(s), 0)), pl.BlockSpec((1, SSM_N, w), lambda s: (s, 0, 0))],
        out_shape=[jax.ShapeDtypeStruct((t, w), xs.dtype), jax.ShapeDtypeStruct((nc, SSM_N, w), f32)],
        scratch_shapes=[pltpu.VMEM((SSM_N, w), f32)],
        name=name, compiler_params=_cparams("arbitrary"))(xs, dt, dtr, bm, cm, acol)


def _ssd_bwd_call(xs, dt, dtr, bm, cm, acol, hin, dy, rev, n_ctx, name):
    t, w = xs.shape
    q = SSM_Q
    nc, ncc = t // q, n_ctx // q

    def body(xs_ref, dt_ref, dtr_ref, b_ref, c_ref, a_ref, hin_ref, dy_ref,
             dxs_ref, ddt_ref, ddtr_ref, db_ref, dc_ref, da_ref, dh_scr):
        @pl.when(pl.program_id(0) == 0)
        def _():
            dh_scr[...] = jnp.zeros_like(dh_scr)
            da_ref[...] = jnp.zeros_like(da_ref)

        dtx = jnp.dot(dt_ref[...], _head_lanes(w), precision=lax.Precision.HIGHEST, preferred_element_type=f32)
        _, vjp = jax.vjp(
            functools.partial(_ssd_chunk, rev=rev),
            [xs_ref[:, sl].astype(f32) for sl in SSD_SLABS], [dtx[:, sl] for sl in SSD_SLABS],
            [dtr_ref[h:h + 1, :] for h in range(SSM_HEADS)], [a_ref[h:h + 1, :] for h in range(SSM_HEADS)],
            [b_ref[:, gs].astype(f32) for gs in SSD_GROUPS], [c_ref[:, gs].astype(f32) for gs in SSD_GROUPS],
            [hin_ref[0, :, sl] for sl in SSD_SLABS])
        dxs, ddtx, ddtr, dac, dbs, dcs, dhin = vjp(([dy_ref[:, sl].astype(f32) for sl in SSD_SLABS], [dh_scr[:, sl] for sl in SSD_SLABS]))
        for j, sl in enumerate(SSD_SLABS):
            dxs_ref[:, sl] = dxs[j].astype(dxs_ref.dtype)
            dh_scr[:, sl] = dhin[j]
        for h in range(SSM_HEADS):
            ddtr_ref[h:h + 1, :] = ddtr[h]
            da_ref[h:h + 1, :] += dac[h]
        for g, gs in enumerate(SSD_GROUPS):
            db_ref[:, gs] = dbs[g].astype(db_ref.dtype)
            dc_ref[:, gs] = dcs[g].astype(dc_ref.dtype)
        ddt_ref[...] = jnp.dot(jnp.concatenate(ddtx, axis=1), _head_lanes(w, transpose=True),
                               precision=lax.Precision.HIGHEST, preferred_element_type=f32)

    def step(s):
        return nc - 1 - s

    def at(s):
        return _ssd_order(step(s), nc, ncc, rev)

    row = lambda wd: pl.BlockSpec((q, wd), lambda s: (at(s), 0))
    dtr_spec = pl.BlockSpec((SSM_HEADS, q), lambda s: (0, at(s)))
    a_spec = pl.BlockSpec((SSM_HEADS, 1), lambda s: (0, 0))
    return pl.pallas_call(
        body, grid=(nc,),
        in_specs=[row(w), row(SSM_HEADS), dtr_spec, row(SSM_BC), row(SSM_BC), a_spec,
                  pl.BlockSpec((1, SSM_N, w), lambda s: (step(s), 0, 0)), row(w)],
        out_specs=[row(w), row(SSM_HEADS), dtr_spec, row(SSM_BC), row(SSM_BC), a_spec],
        out_shape=[jax.ShapeDtypeStruct((t, w), xs.dtype), jax.ShapeDtypeStruct(dt.shape, f32), jax.ShapeDtypeStruct(dtr.shape, f32),
                   jax.ShapeDtypeStruct(bm.shape, bm.dtype), jax.ShapeDtypeStruct(cm.shape, cm.dtype), jax.ShapeDtypeStruct(acol.shape, f32)],
        scratch_shapes=[pltpu.VMEM((SSM_N, w), f32)],
        name=name, compiler_params=_cparams("arbitrary"))(xs, dt, dtr, bm, cm, acol, hin, dy)


def ssd_scan(xs, dt, dtr, bm, cm, acol, rev, n_ctx, name):
    @jax.custom_vjp
    def op(xs, dt, dtr, bm, cm, acol):
        return _ssd_fwd_call(xs, dt, dtr, bm, cm, acol, rev, n_ctx, name)[0]

    def fwd(xs, dt, dtr, bm, cm, acol):
        y, hin = _ssd_fwd_call(xs, dt, dtr, bm, cm, acol, rev, n_ctx, name)
        return y, (xs, dt, dtr, bm, cm, acol, hin)

    def bwd(res, dy):
        return tuple(_ssd_bwd_call(*res, dy, rev, n_ctx, name + "_bwd"))

    op.defvjp(fwd, bwd)
    return op(xs, dt, dtr, bm, cm, acol)


def _deinterleave(w, n_heads):
    lead = w.shape[:-1]
    return w.reshape(*lead, n_heads, HEAD_DIM // 2, 2).swapaxes(-1, -2).reshape(*lead, n_heads * HEAD_DIM)


def _interleave(w, n_heads):
    lead = w.shape[:-1]
    return w.reshape(*lead, n_heads, 2, HEAD_DIM // 2).swapaxes(-1, -2).reshape(*lead, n_heads * HEAD_DIM)


def _in_layout(d):
    sizes = [('a_q', Q_W, N_HEADS), ('a_k', KV_W, N_KV), ('a_v', KV_W, 0), ('b_z', SSM_INNER, 0),
             ('b_xbc', SSM_INNER + 2 * SSM_BC, 0), ('b_dt', DT_W, 0), ('c_q', Q_W, N_HEADS), ('c_k', KV_W, N_KV),
             ('c_v', KV_W, 0), ('g_a', d, 0), ('g_b', d, 0), ('g_c', d, 0)]
    out, start = [], 0
    for name, n, heads in sizes:
        out.append((name, start, n, heads))
        start += n
    return out


@jax.custom_vjp
def _w_in_split(w):
    d = w.shape[0]
    parts, dt = [], None
    for name, s, n, heads in _in_layout(d):
        p = w[:, s:s + n]
        if heads:
            p = _deinterleave(p, heads)
        if name == 'b_dt':
            dt = jnp.concatenate([p, jnp.zeros((d, DT_PAD - n), w.dtype)], axis=1)
        else:
            parts.append(p)
    return jnp.concatenate(parts, axis=1), dt


def _w_in_join(g_main, g_dt):
    d = g_main.shape[0]
    parts, start = [], 0
    for name, _, n, heads in _in_layout(d):
        if name == 'b_dt':
            parts.append(g_dt[:, :n])
            continue
        p = g_main[:, start:start + n]
        parts.append(_interleave(p, heads) if heads else p)
        start += n
    return jnp.concatenate(parts, axis=1)


_w_in_split.defvjp(lambda w: (_w_in_split(w), None), lambda _, g: (_w_in_join(*g),))


def _rope_tables(n_ctx, n_lat):
    rows = n_lat // GRID_W
    t_row = jnp.repeat(jnp.arange(rows), GRID_W).astype(f32)
    t_col = jnp.tile(jnp.arange(GRID_W), rows).astype(f32)
    n = HEAD_DIM // 4
    inv = ROPE_BASE ** (-jnp.arange(n, dtype=f32) / n)
    ang = jnp.concatenate([t_row[:, None] * inv, t_col[:, None] * inv], axis=-1)
    cos = jnp.concatenate([jnp.ones((n_ctx, HEAD_DIM // 2), f32), jnp.cos(ang)], axis=0)
    sin = jnp.concatenate([jnp.zeros((n_ctx, HEAD_DIM // 2), f32), jnp.sin(ang)], axis=0)
    return jnp.concatenate([cos, cos], axis=1), jnp.concatenate([-sin, sin], axis=1)


def _heads_major(a, n_heads):
    return a.reshape(a.shape[0], n_heads, HEAD_DIM).transpose(1, 0, 2)


def _heads_minor(a):
    return a.transpose(1, 0, 2).reshape(a.shape[1], a.shape[0] * HEAD_DIM)


def _layer(xall, w, s, cm, tabs, n_ctx, li, gather=((), ())):
    t, d = xall.shape
    ncb = n_ctx // ROW_TILE
    nm = f"l{li}_"
    ctq, stq, ctk, stk = tabs
    mod = [(cm[0:1, i * d:(i + 1) * d], cm[1:2, i * d:(i + 1) * d]) for i in range(6)]

    def pick(blk, pair_c, pair_l):
        return jnp.where(blk < ncb, pair_c, pair_l)

    def norm_mod(blk, x, g, sh_c, sh_l, sc_c, sc_l):
        return (_rms(x, g) * (1.0 + pick(blk, sc_c, sc_l)) + pick(blk, sh_c, sh_l),)

    (h,) = rowwise(norm_mod, [xall], [], [s['norm1'][None], *mod[0], *mod[1]], [d], [bf16], nm + "norm1")
    w_main, w_dt = _w_in_split(w['w_in'])
    u = mm(h, w_main, nm + "in")
    b_dt = mm(h, w_dt, nm + "in_dt", f32)
    a_q, a_k, a_v, b_z, b_xbc, c_q, c_k, c_v, g_a, g_b, g_c = split_cols(u, [n for name, _, n, _ in _in_layout(d) if name != 'b_dt'])

    def rope(blk, q, k, v, ct_q, st_q, ct_k, st_k):
        return q * ct_q + _rot_half(q) * st_q, k * ct_k + _rot_half(k) * st_k, v

    def norm_rope(blk, q, k, v, ct_q, st_q, ct_k, st_k, gq, gk):
        return rope(blk, _head_rms(q, gq), _head_rms(k, gk), v, ct_q, st_q, ct_k, st_k)

    qkv_w, qkv_t = [Q_W, KV_W, KV_W], [bf16, bf16, bf16]
    qa, ka, va = rowwise(rope, [a_q, a_k, a_v], [ctq, stq, ctk, stk], [], qkv_w, qkv_t, nm + "ropeA")
    gq = jnp.tile(_deinterleave(s['c_q_norm'], 1), N_HEADS)[None]
    gk = jnp.tile(_deinterleave(s['c_k_norm'], 1), N_KV)[None]
    qc, kc, vc = rowwise(norm_rope, [c_q, c_k, c_v], [ctq, stq, ctk, stk], [gq, gk], qkv_w, qkv_t, nm + "ropeC")
    ya = _heads_minor(attention(_heads_major(qa, N_HEADS), _heads_major(ka, N_KV), _heads_major(va, N_KV),
                                s['a_sink'], True, nm + "attnA")[0])
    yc, gathered = attention(_heads_major(qc, N_HEADS), _heads_major(kc, N_KV), _heads_major(vc, N_KV),
                             jnp.zeros((N_HEADS,), f32), False, nm + "attnC", *gather)
    yc = _heads_minor(yc)

    cw, cb = s['ssm_conv_w'], s['ssm_conv_b']
    conv_silu = lambda uu, w0, w1, w2, b: _silu(_dwconv(uu, w0, w1, w2, b, n_ctx))
    xbc = colwise(conv_silu, [b_xbc], [cw[0:1], cw[1:2], cw[2:3], cb[None]], bf16, nm + "ssmconv")
    xs, bm, cmat = split_cols(xbc, [SSM_INNER, SSM_BC, SSM_BC])
    bias = jnp.concatenate([s['ssm_dt_bias'].reshape(1, DT_W), jnp.zeros((1, DT_PAD - DT_W), f32)], axis=1)

    def softplus(blk, r, b):
        z = r + b
        return (jnp.maximum(z, 0.0) + jnp.log(1.0 + jnp.exp(-jnp.abs(z))),)

    (dt_all,) = rowwise(softplus, [b_dt], [], [bias], [DT_PAD], [f32], nm + "dt")
    a_coef = -jnp.exp(s['ssm_A_log'])
    ys_dir = []
    for di, rev in enumerate((False, True)):
        dt = dt_all[:, di * SSM_HEADS:(di + 1) * SSM_HEADS]
        ys_dir.append(ssd_scan(xs, dt, dt.T, bm, cmat, a_coef[di][:, None], rev, n_ctx,
                               nm + ("ssd_r" if rev else "ssd_f")))

    def ssm_out(blk, yf, yb, x, z, dskip, g):
        return (_rms((yf + yb + x * dskip) * _silu(z), g),)

    (ysn,) = rowwise(ssm_out, [ys_dir[0], ys_dir[1], xs, b_z], [], [jnp.repeat(s['ssm_D'], SSM_P)[None], s['ssm_norm'][None]],
                     [SSM_INNER], [bf16], nm + "ssmout")

    pa, pb, pc = mm(ya, w['w_oa'], nm + "oa"), mm(ysn, w['w_ob'], nm + "ob"), mm(yc, w['w_oc'], nm + "oc")

    def merge(blk, ga, gb, gc, a, b, c):
        return (_sigmoid(ga) * a + _sigmoid(gb) * b + _sigmoid(gc) * c,)

    (mrg,) = rowwise(merge, [g_a, g_b, g_c, pa, pb, pc], [], [], [d], [bf16], nm + "merge")
    o = mm(mrg, w['w_out'], nm + "out")

    def resid_norm_mod(blk, x, oo, g1_c, g1_l, g, sh_c, sh_l, sc_c, sc_l):
        x1 = x + pick(blk, g1_c, g1_l) * oo
        return x1, _rms(x1, g) * (1.0 + pick(blk, sc_c, sc_l)) + pick(blk, sh_c, sh_l)

    x1, h2 = rowwise(resid_norm_mod, [xall, o], [], [*mod[2], s['norm2'][None], *mod[3], *mod[4]], [d, d], [f32, bf16], nm + "norm2")
    up, gt = mm(h2, w['ffn_w_up'], nm + "up"), mm(h2, w['ffn_w_gate'], nm + "gate")
    fw, fb = s['ffn_conv_w'], s['ffn_conv_b']
    ffn_act = lambda g_, u_, w0, w1, w2, b: _silu(_dwconv(g_, w0, w1, w2, b, n_ctx)) * u_
    act = colwise(ffn_act, [gt, up], [fw[0:1], fw[1:2], fw[2:3], fb[None]], bf16, nm + "ffnact")
    f = mm(act, w['ffn_w_down'], nm + "down")

    def resid(blk, x, ff, g2_c, g2_l):
        return (x + pick(blk, g2_c, g2_l) * ff,)

    (x2,) = rowwise(resid, [x1, f], [], [*mod[5]], [d], [f32], nm + "resid")
    return x2, gathered


def _assemble(name, gathered):
    return jnp.concatenate([gathered[j] for j in range(8)], axis=1 if BIG[name] == 1 else 0)


def _loss_fn(big0, shards1, stand_ins1, small, x, ctx, c, target, n_ctx):
    n_lat, d = x.shape
    xall = jnp.concatenate([ctx, x], axis=0)
    ct, st = _rope_tables(n_ctx, n_lat)
    tabs = (jnp.tile(ct, (1, N_HEADS)) * HEAD_DIM ** -0.5, jnp.tile(st, (1, N_HEADS)) * HEAD_DIM ** -0.5,
            jnp.tile(ct, (1, N_KV)), jnp.tile(st, (1, N_KV)))
    srows = jnp.concatenate([_silu(small['c_ctx'])[None], _silu(c), jnp.zeros((14, d), f32)], axis=0)
    names = list(BIG)
    big, gather = big0, ([shards1[n] for n in names], [stand_ins1[n] for n in names])
    for li in range(2):
        cm = mm(srows, big['w_mod'], f"l{li}_mod", f32)[0:2] + small['b_mod'][li][None]
        sl = {k: v[li] for k, v in small.items() if k not in ('c_ctx', 'final_norm')}
        xall, gathered = _layer(xall, big, sl, cm, tabs, n_ctx, li, gather)
        if li == 0:
            big, gather = {n: _assemble(n, g) for n, g in zip(names, gathered)}, ((), ())
    ncb = n_ctx // ROW_TILE
    tgt = jnp.concatenate([jnp.zeros((n_ctx, d), f32), target], axis=0)

    def loss_rows(blk, xx, tg, g):
        e = _rms(xx, g) - tg
        return (jnp.where(blk < ncb, 0.0, 0.5) * jnp.mean(e * e, axis=-1, keepdims=True),)

    (rows,) = rowwise(loss_rows, [xall], [tgt], [small['final_norm'][None]], [1], [f32], "loss")
    return jnp.sum(rows)


def _hbm_call(body, ins, out_shapes, n_sems, name):
    any_spec = pl.BlockSpec(memory_space=pl.ANY)
    return pl.pallas_call(
        body, out_shape=out_shapes, in_specs=[any_spec] * len(ins), out_specs=[any_spec] * len(out_shapes),
        scratch_shapes=[pltpu.SemaphoreType.DMA((n_sems,)), pltpu.SemaphoreType.DMA((n_sems,)), pltpu.SemaphoreType.DMA((len(ins),))],
        name=name)(*ins)


def _gather_steps(x_refs, out_refs, send_sems, recv_sems, local_sems):
    n = len(x_refs)
    x, y, c = lax.axis_index("x"), lax.axis_index("y"), lax.axis_index("c")
    me, sibling = (x, y, c), (x, y, 1 - c)
    chips = [(1 - x, y), (x, 1 - y), (1 - x, 1 - y)]

    def copy(a, k, block, to, src=None):
        px, py, pc = block
        slot = out_refs[a].at[4 * px + 2 * py + pc]
        return pltpu.make_async_remote_copy(
            src_ref=slot if src is None else src, dst_ref=slot,
            send_sem=send_sems.at[7 * a + k], recv_sem=recv_sems.at[7 * a + k], device_id=to, device_id_type=MESH)

    mine = [pltpu.make_async_copy(x_refs[a], out_refs[a].at[4 * x + 2 * y + c], local_sems.at[a]) for a in range(n)]
    first = []
    for a in range(n):
        first += [copy(a, 1 + j, me, (*chip, c), src=x_refs[a]) for j, chip in enumerate(chips)]
        first.append(copy(a, 0, me, sibling, src=x_refs[a]))

    def start():
        for cp in mine + first:
            cp.start()

    def finish():
        passed = []
        for a in range(n):
            for j, chip in enumerate(chips):
                copy(a, 1 + j, (*chip, c), me).wait_recv()
                passed.append(copy(a, 4 + j, (*chip, c), sibling))
                passed[-1].start()
        for a in range(n):
            copy(a, 0, sibling, me).wait_recv()
            for j, chip in enumerate(chips):
                copy(a, 4 + j, (*chip, 1 - c), me).wait_recv()
        for cp in first + passed:
            cp.wait_send()
        for cp in mine:
            cp.wait()

    return start, finish


def _gather_scratch(n):
    return [pltpu.SemaphoreType.DMA((7 * n,)), pltpu.SemaphoreType.DMA((7 * n,)), pltpu.SemaphoreType.DMA((n,))]


def all_gather(shards, name):
    n = len(shards)

    def body(*refs):
        start, finish = _gather_steps(refs[:n], refs[n:2 * n], *refs[2 * n:])
        start()
        finish()

    return _hbm_call(body, shards, [jax.ShapeDtypeStruct((8,) + s.shape, s.dtype) for s in shards], 7 * n, name)


def rs_to_sibling(gs, name="rs_sibling"):
    n = len(gs)

    def body(*refs):
        g_refs, out_refs, (send_sems, recv_sems, _) = refs[:n], refs[n:2 * n], refs[2 * n:]
        x, y, c = lax.axis_index("x"), lax.axis_index("y"), lax.axis_index("c")
        copies = [pltpu.make_async_remote_copy(
            src_ref=g_refs[a].at[2 * k + (1 - c)], dst_ref=out_refs[a].at[k], send_sem=send_sems.at[4 * a + k],
            recv_sem=recv_sems.at[4 * a + k], device_id=(x, y, 1 - c), device_id_type=MESH) for a in range(n) for k in range(4)]
        for cp in copies:
            cp.start()
        for cp in copies:
            cp.wait()

    return _hbm_call(body, gs, [jax.ShapeDtypeStruct((4,) + g.shape[1:], g.dtype) for g in gs], 4 * n, name)


def rs_to_chips(ss):
    n = len(ss)

    def body(*refs):
        copies = _chips_copies(refs[:n], refs[n:2 * n], refs[2 * n], refs[2 * n + 1])
        for cp in copies:
            cp.start()
        for cp in copies:
            cp.wait()

    return _hbm_call(body, ss, [jax.ShapeDtypeStruct((3,) + s.shape[1:], s.dtype) for s in ss], 3 * n, "rs_chips")


def _chips_copies(s_refs, out_refs, send_sems, recv_sems):
    x, y, c = lax.axis_index("x"), lax.axis_index("y"), lax.axis_index("c")
    copies = []
    for a in range(len(s_refs)):
        for k, (fx, fy) in enumerate([(1, 0), (0, 1), (1, 1)]):
            px, py = (1 - x) if fx else x, (1 - y) if fy else y
            copies.append(pltpu.make_async_remote_copy(
                src_ref=s_refs[a].at[2 * px + py], dst_ref=out_refs[a].at[k], send_sem=send_sems.at[3 * a + k],
                recv_sem=recv_sems.at[3 * a + k], device_id=(px, py, c), device_id_type=MESH))
    return copies


def _flat_tile(rows, cols):
    return _row_tile(rows, 4 * 4 * cols)


def pair_sum(g, r1, my_c, name):
    _, rows, cols = g.shape
    tm = _flat_tile(rows, cols)

    def body(c_ref, g_ref, r_ref, o_ref):
        o_ref[...] = (g_ref[...].astype(f32) + r_ref[...].astype(f32)).astype(o_ref.dtype)

    return pl.pallas_call(
        body, grid_spec=pltpu.PrefetchScalarGridSpec(
            num_scalar_prefetch=1, grid=(4, rows // tm),
            in_specs=[pl.BlockSpec((1, tm, cols), lambda k, i, c: (2 * k + c[0], i, 0)),
                      pl.BlockSpec((1, tm, cols), lambda k, i, c: (k, i, 0))],
            out_specs=pl.BlockSpec((1, tm, cols), lambda k, i, c: (k, i, 0))),
        out_shape=jax.ShapeDtypeStruct((4, rows, cols), g.dtype), name=name,
        compiler_params=_cparams("parallel", "parallel"))(my_c, g, r1)


def _adam_math(w, g, m, v):
    m2 = ADAM_B1 * m + (1.0 - ADAM_B1) * g
    v2 = ADAM_B2 * v + (1.0 - ADAM_B2) * (g * g)
    m_hat = m2 / (1.0 - ADAM_B1 ** ADAM_STEP)
    v_hat = v2 / (1.0 - ADAM_B2 ** ADAM_STEP)
    return -ADAM_LR * (m_hat / (jnp.sqrt(v_hat) + ADAM_EPS) + ADAM_WD * w), m2, v2


def sum_adam(parts, w, m, v, name):
    groups, rows, cols = w.shape
    tm = _flat_tile(rows, cols)
    nblk = rows // tm
    flat = []
    scalars = [p[2] for ps in parts for p in ps if p[2] is not None]
    for gi, ps in enumerate(parts):
        flat.append([])
        for arr, static_rows, dyn in ps:
            if dyn is not None:
                flat[gi].append((arr, functools.partial(lambda l, i, s, gi: (s[0], jnp.where(l == gi, i, nblk - 1), 0), gi=gi)))
            else:
                for k in static_rows:
                    flat[gi].append((arr, functools.partial(lambda l, i, s, gi, k: (k, jnp.where(l == gi, i, nblk - 1), 0), gi=gi, k=k)))
    counts = [len(f) for f in flat]
    na = sum(counts)

    def body(s_ref, *refs):
        sums, at = [], 0
        for cnt in counts:
            g = refs[at][0].astype(f32)
            for r in refs[at + 1:at + cnt]:
                g = g + r[0].astype(f32)
            sums.append(g)
            at += cnt
        g = sums[0]
        for gi in range(1, groups):
            g = jnp.where(pl.program_id(0) == gi, sums[gi], g)
        w_ref, m_ref, v_ref = refs[na:na + 3]
        g_out, d_out, m_out, v_out = refs[na + 3:]
        d, m2, v2 = _adam_math(w_ref[0], g, m_ref[0], v_ref[0])
        g_out[0] = g
        d_out[0] = d
        m_out[0] = m2
        v_out[0] = v2

    blk = pl.BlockSpec((1, tm, cols), lambda l, i, s: (l, i, 0))
    scalar = scalars[0] if scalars else jnp.zeros((1,), jnp.int32)
    return pl.pallas_call(
        body, grid_spec=pltpu.PrefetchScalarGridSpec(
            num_scalar_prefetch=1, grid=(groups, nblk),
            in_specs=[pl.BlockSpec((1, tm, cols), im) for f in flat for _, im in f] + [blk, blk, blk],
            out_specs=[blk, blk, blk, blk]),
        out_shape=[jax.ShapeDtypeStruct((groups, rows, cols), f32)] * 4, name=name,
        compiler_params=_cparams("arbitrary", "arbitrary"))(scalar, *[a for f in flat for a, _ in f], w, m, v)


FLAT_COLS = 1024


def _to_flat(vec):
    n = vec.shape[0]
    total = -(-n // (8 * FLAT_COLS)) * 8 * FLAT_COLS
    return jnp.concatenate([vec, jnp.zeros((total - n,), vec.dtype)]).reshape(-1, FLAT_COLS)


def _pack(tree, names):
    return jnp.concatenate([tree[n].reshape(-1) for n in names])


def _unpack(vec, like, names):
    out, off = {}, 0
    for n in names:
        size = like[n].size
        out[n] = vec[off:off + size].reshape(like[n].shape)
        off += size
    return out


def kernel(x, c, ctx, c_ctx, w_mod, b_mod, norm1, norm2, w_in, a_sink, ssm_conv_w, ssm_conv_b, ssm_A_log, ssm_dt_bias, ssm_D, ssm_norm, c_q_norm, c_k_norm, w_oa, w_ob, w_oc, w_out, ffn_w_up, ffn_w_gate, ffn_conv_w, ffn_conv_b, ffn_w_down, final_norm, loss_target, m_c_ctx, m_w_mod, m_b_mod, m_norm1, m_norm2, m_w_in, m_a_sink, m_ssm_conv_w, m_ssm_conv_b, m_ssm_A_log, m_ssm_dt_bias, m_ssm_D, m_ssm_norm, m_c_q_norm, m_c_k_norm, m_w_oa, m_w_ob, m_w_oc, m_w_out, m_ffn_w_up, m_ffn_w_gate, m_ffn_conv_w, m_ffn_conv_b, m_ffn_w_down, m_final_norm, v_c_ctx, v_w_mod, v_b_mod, v_norm1, v_norm2, v_w_in, v_a_sink, v_ssm_conv_w, v_ssm_conv_b, v_ssm_A_log, v_ssm_dt_bias, v_ssm_D, v_ssm_norm, v_c_q_norm, v_c_k_norm, v_w_oa, v_w_ob, v_w_oc, v_w_out, v_ffn_w_up, v_ffn_w_gate, v_ffn_conv_w, v_ffn_conv_b, v_ffn_w_down, v_final_norm):
    args = (x, c, ctx, c_ctx, w_mod, b_mod, norm1, norm2, w_in, a_sink, ssm_conv_w, ssm_conv_b, ssm_A_log, ssm_dt_bias, ssm_D, ssm_norm, c_q_norm, c_k_norm, w_oa, w_ob, w_oc, w_out, ffn_w_up, ffn_w_gate, ffn_conv_w, ffn_conv_b, ffn_w_down, final_norm, loss_target)
    moms = (m_c_ctx, m_w_mod, m_b_mod, m_norm1, m_norm2, m_w_in, m_a_sink, m_ssm_conv_w, m_ssm_conv_b, m_ssm_A_log, m_ssm_dt_bias, m_ssm_D, m_ssm_norm, m_c_q_norm, m_c_k_norm, m_w_oa, m_w_ob, m_w_oc, m_w_out, m_ffn_w_up, m_ffn_w_gate, m_ffn_conv_w, m_ffn_conv_b, m_ffn_w_down, m_final_norm)
    vars_ = (v_c_ctx, v_w_mod, v_b_mod, v_norm1, v_norm2, v_w_in, v_a_sink, v_ssm_conv_w, v_ssm_conv_b, v_ssm_A_log, v_ssm_dt_bias, v_ssm_D, v_ssm_norm, v_c_q_norm, v_c_k_norm, v_w_oa, v_w_ob, v_w_oc, v_w_out, v_ffn_w_up, v_ffn_w_gate, v_ffn_conv_w, v_ffn_conv_b, v_ffn_w_down, v_final_norm)
    p = dict(zip(IN_NAMES, args))
    mom = dict(zip(WEIGHTS, moms))
    var = dict(zip(WEIGHTS, vars_))
    depth = w_in.shape[0]
    n_ctx = ctx.shape[1]
    xi, yi, ci = lax.axis_index("x"), lax.axis_index("y"), lax.axis_index("c")
    dev = 4 * xi + 2 * yi + ci
    big_names = list(BIG)

    assert depth == 2 and n_ctx == ROW_TILE
    shards = [{n: p[n][li].astype(bf16) for n in big_names} for li in range(depth)]
    g_big0 = all_gather([shards[0][n] for n in big_names], "gather_l0")
    g_conv = all_gather([_to_flat(_pack(p, CONV_W))], "gather_conv")[0].reshape(8, -1)
    big0 = {n: _assemble(n, g) for n, g in zip(big_names, g_big0)}
    stand_ins = {n: jnp.zeros((8,) + shards[1][n].shape, bf16) for n in big_names}
    conv_full, off = {}, 0
    for n in CONV_W:
        shp = p[n].shape
        seg = g_conv[:, off:off + p[n].size].reshape(8, *shp)
        conv_full[n] = jnp.moveaxis(seg, 0, -2).reshape(*shp[:-1], 8 * shp[-1])
        off += p[n].size
    small = {n: p[n] for n in REPL}
    small.update(conv_full)

    loss, (g_big0, g_big1, g_small, g_x) = jax.value_and_grad(_loss_fn, argnums=(0, 2, 3, 4))(
        big0, shards[1], stand_ins, small, x[0], ctx[0], c, loss_target[0], n_ctx)
    loss = lax.psum(loss, AXES)

    def send_rows(n):
        b = p[n].shape[-1] if BIG[n] == 1 else p[n].shape[1]
        cut = (lambda g, j: g[:, b * j:b * (j + 1)]) if BIG[n] == 1 else (lambda g, j: g[b * j:b * (j + 1), :])
        return jnp.stack([cut(g_big0[n], j) for j in range(8)])

    send = [send_rows(n) for n in big_names]
    from_sibling = rs_to_sibling(send)
    my_c = ci.reshape(1).astype(jnp.int32)
    side_sum = [pair_sum(s, r, my_c, "rs_pair_sum_" + n) for n, s, r in zip(big_names, send, from_sibling)]
    from_chips = rs_to_chips(side_sum)
    chip = (2 * xi + yi).reshape(1).astype(jnp.int32)
    big_out = [{}, {}, {}, {}]
    for a, n in enumerate(big_names):
        parts = [[(side_sum[a], None, chip), (from_chips[a], (0, 1, 2), None)],
                 [(g_big1[n], None, chip), (g_big1[n], (4, 5, 6), None)]]
        outs = sum_adam(parts, p[n], mom[n], var[n], "adam_" + n)
        for k in range(4):
            big_out[k][n] = outs[k]

    sm_names = REPL + list(CONV_W)
    g_vec = _to_flat(_pack(g_small, sm_names))
    gathered = all_gather([g_vec], "gather_small_grads")[0]
    n_repl = sum(p[n].size for n in REPL)

    def repl_flat(tree):
        return _to_flat(jnp.concatenate([_pack(tree, REPL), jnp.zeros((g_vec.size - n_repl,), f32)]))

    outs_small = sum_adam([[(gathered, tuple(range(8)), None)]], repl_flat(p)[None], repl_flat(mom)[None], repl_flat(var)[None],
                          "adam_small")
    g_sum = outs_small[0].reshape(-1)
    small_out = [_unpack(o.reshape(-1), p, REPL) for o in outs_small]
    conv_g_full = _unpack(g_sum[n_repl:], conv_full, CONV_W)
    conv_g = {n: lax.dynamic_slice_in_dim(conv_g_full[n], dev * p[n].shape[-1], p[n].shape[-1], axis=2) for n in CONV_W}
    conv_gv = _to_flat(_pack(conv_g, CONV_W))
    outs_conv = sum_adam([[(conv_gv[None], (0,), None)]], _to_flat(_pack(p, CONV_W))[None], _to_flat(_pack(mom, CONV_W))[None],
                         _to_flat(_pack(var, CONV_W))[None], "adam_conv")
    conv_out = [_unpack(o.reshape(-1), p, CONV_W) for o in outs_conv]

    res = []
    for k in range(4):
        tree = {**big_out[k], **small_out[k], **conv_out[k]}
        res.append([tree[n] for n in WEIGHTS])
    return (loss, g_x[None], *res[0], *res[1], *res[2], *res[3])
```

```python
import functools

import jax
import jax.numpy as jnp
from jax import lax
from jax.experimental import pallas as pl
from jax.experimental.pallas import tpu as pltpu

f32 = jnp.float32
bf16 = jnp.bfloat16
MESH = pl.DeviceIdType.MESH
AXES = ("x", "y", "c")

GRID_W = 64
HEAD_DIM = 64
ROPE_BASE = 10000.0
EPS = 1e-6
WINDOW = 128
N_HEADS = 8
N_KV = 2
SSM_HEADS = 16
SSM_P = 64
SSM_G = 2
SSM_N = 128
SSM_INNER = SSM_HEADS * SSM_P
SSM_BC = SSM_G * SSM_N
SSM_Q = 128
Q_W = N_HEADS * HEAD_DIM
KV_W = N_KV * HEAD_DIM
DT_W = 2 * SSM_HEADS
DT_PAD = 128
ADAM_LR, ADAM_B1, ADAM_B2, ADAM_EPS, ADAM_WD, ADAM_STEP = 0.001, 0.9, 0.999, 1e-08, 0.01, 10

LANES = 128
ROW_TILE = 256
VMEM_BLOCK_BUDGET = 6 * 1024 * 1024
ATTN_SLAB = 128
MM_ROW_CAP = 1088
MM_TILE_CAP = 1536
NEG = -1e30

IN_NAMES = ['x', 'c', 'ctx', 'c_ctx', 'w_mod', 'b_mod', 'norm1', 'norm2', 'w_in', 'a_sink', 'ssm_conv_w', 'ssm_conv_b', 'ssm_A_log', 'ssm_dt_bias', 'ssm_D', 'ssm_norm', 'c_q_norm', 'c_k_norm', 'w_oa', 'w_ob', 'w_oc', 'w_out', 'ffn_w_up', 'ffn_w_gate', 'ffn_conv_w', 'ffn_conv_b', 'ffn_w_down', 'final_norm', 'loss_target']
WEIGHTS = IN_NAMES[3:28]
BIG = {'w_mod': 1, 'w_in': 1, 'w_oa': 1, 'w_ob': 0, 'w_oc': 1, 'w_out': 0, 'ffn_w_up': 1, 'ffn_w_gate': 1, 'ffn_w_down': 0}
CONV_W = ('ssm_conv_w', 'ffn_conv_w')
REPL = [n for n in WEIGHTS if n not in BIG and n not in CONV_W]

NT = (((1,), (1,)), ((), ()))
TN = (((0,), (0,)), ((), ()))
NN = (((1,), (0,)), ((), ()))


def _cparams(*sem):
    return pltpu.CompilerParams(dimension_semantics=sem)


def _div_tile(n, unit, cap):
    for d in range(min(n, int(cap)), 0, -1):
        if n % d == 0 and d % unit == 0:
            return d
    return n


def _row_tile(m, row_bytes):
    return _div_tile(m, 16, max(16, VMEM_BLOCK_BUDGET // row_bytes))


def _mm_call(a, b, mode, out_dtype, name):
    if mode == "nn":
        (m, k), n = a.shape, b.shape[1]
    elif mode == "nt":
        (m, k), n = a.shape, b.shape[0]
    else:
        (k, m), n = a.shape, b.shape[1]
    dims = {"nn": NN, "nt": NT, "tn": TN}[mode]
    ia, ib = a.dtype.itemsize, b.dtype.itemsize
    tm = _div_tile(m, LANES, MM_TILE_CAP) if mode == "tn" else _div_tile(m, 16, MM_ROW_CAP)
    tn = _div_tile(n, LANES, min(MM_TILE_CAP, VMEM_BLOCK_BUDGET // (4 * tm)))
    tk = _div_tile(k, 16 if mode == "tn" else LANES,
                   min(MM_ROW_CAP if mode == "tn" else MM_TILE_CAP, VMEM_BLOCK_BUDGET // (tm * ia), VMEM_BLOCK_BUDGET // (tn * ib)))
    nk = k // tk

    def body(a_ref, b_ref, o_ref, *acc):
        part = lax.dot_general(a_ref[...].astype(bf16), b_ref[...].astype(bf16), dims, preferred_element_type=f32)
        if nk == 1:
            o_ref[...] = part.astype(o_ref.dtype)
            return
        kk = pl.program_id(2)

        @pl.when(kk == 0)
        def _():
            acc[0][...] = part

        @pl.when(kk > 0)
        def _():
            acc[0][...] += part

        @pl.when(kk == nk - 1)
        def _():
            o_ref[...] = acc[0][...].astype(o_ref.dtype)

    a_spec = pl.BlockSpec((tk, tm), lambda i, j, kk: (kk, i)) if mode == "tn" else pl.BlockSpec((tm, tk), lambda i, j, kk: (i, kk))
    b_spec = pl.BlockSpec((tn, tk), lambda i, j, kk: (j, kk)) if mode == "nt" else pl.BlockSpec((tk, tn), lambda i, j, kk: (kk, j))
    return pl.pallas_call(
        body, grid=(m // tm, n // tn, nk), in_specs=[a_spec, b_spec],
        out_specs=pl.BlockSpec((tm, tn), lambda i, j, kk: (i, j)),
        out_shape=jax.ShapeDtypeStruct((m, n), out_dtype),
        scratch_shapes=[pltpu.VMEM((tm, tn), f32)] if nk > 1 else [], name=name,
        compiler_params=_cparams("parallel", "parallel", "arbitrary"))(a, b)


def mm(a, b, name, out_dtype=None):
    @jax.custom_vjp
    def op(a, b):
        return _mm_call(a, b, "nn", out_dtype or bf16, name)

    def fwd(a, b):
        return op(a, b), (a, b)

    def bwd(res, g):
        a, b = res
        return _mm_call(g, b, "nt", a.dtype, name + "_da"), _mm_call(a, g, "tn", b.dtype, name + "_db")

    op.defvjp(fwd, bwd)
    return op(a, b)


def split_cols(u, widths):
    offs = [0]
    for w in widths:
        offs.append(offs[-1] + w)

    @jax.custom_vjp
    def op(u):
        return tuple(u[:, offs[i]:offs[i + 1]] for i in range(len(widths)))

    def fwd(u):
        return op(u), None

    def bwd(_, cts):
        return (jnp.concatenate(cts, axis=1),)

    op.defvjp(fwd, bwd)
    return op(u)


def rowwise(fn, rows, consts, pars, out_widths, out_dtypes, name):
    t = rows[0].shape[0]
    tm = ROW_TILE
    nb = t // tm
    nr, nc, npar = len(rows), len(consts), len(pars)

    def rspec(a):
        return pl.BlockSpec((tm, a.shape[1]), lambda i: (i, 0))

    def pspec(a):
        return pl.BlockSpec(a.shape, lambda i: (0,) * a.ndim)

    def call_fwd(rows, consts, pars):
        def body(*refs):
            blk = pl.program_id(0)
            ins = [r[...].astype(f32) for r in refs[:nr + nc]]
            ps = [r[...] for r in refs[nr + nc:nr + nc + npar]]
            outs = fn(blk, *ins, *ps)
            for o_ref, o in zip(refs[nr + nc + npar:], outs):
                o_ref[...] = o.astype(o_ref.dtype)

        return pl.pallas_call(
            body, grid=(nb,),
            in_specs=[rspec(a) for a in rows + consts] + [pspec(a) for a in pars],
            out_specs=[pl.BlockSpec((tm, w), lambda i: (i, 0)) for w in out_widths],
            out_shape=[jax.ShapeDtypeStruct((t, w), d) for w, d in zip(out_widths, out_dtypes)],
            name=name, compiler_params=_cparams("parallel"))(*rows, *consts, *pars)

    def call_bwd(rows, consts, pars, cts):
        nout = len(cts)

        def body(*refs):
            blk = pl.program_id(0)
            ins = [r[...].astype(f32) for r in refs[:nr]]
            cs = [r[...].astype(f32) for r in refs[nr:nr + nc]]
            ps = [r[...] for r in refs[nr + nc:nr + nc + npar]]
            dys = [r[...].astype(f32) for r in refs[nr + nc + npar:nr + nc + npar + nout]]
            d_refs = refs[nr + nc + npar + nout:]
            _, vjp = jax.vjp(lambda *a: tuple(fn(blk, *a[:nr], *cs, *a[nr:])), *ins, *ps)
            grads = vjp(tuple(dys))
            for d_ref, g in zip(d_refs[:nr], grads[:nr]):
                d_ref[...] = g.astype(d_ref.dtype)
            if npar:
                @pl.when(blk == 0)
                def _():
                    for d_ref in d_refs[nr:]:
                        d_ref[...] = jnp.zeros_like(d_ref)

                for d_ref, g in zip(d_refs[nr:], grads[nr:]):
                    d_ref[...] += g

        return pl.pallas_call(
            body, grid=(nb,),
            in_specs=[rspec(a) for a in rows + consts] + [pspec(a) for a in pars] + [rspec(a) for a in cts],
            out_specs=[rspec(a) for a in rows] + [pspec(a) for a in pars],
            out_shape=[jax.ShapeDtypeStruct(a.shape, a.dtype) for a in rows + pars],
            name=name + "_bwd", compiler_params=_cparams("arbitrary"))(*rows, *consts, *pars, *cts)

    @jax.custom_vjp
    def op(rows, consts, pars):
        return tuple(call_fwd(list(rows), list(consts), list(pars)))

    def fwd(rows, consts, pars):
        return op(rows, consts, pars), (rows, consts, pars)

    def bwd(res, cts):
        rows, consts, pars = res
        g = call_bwd(list(rows), list(consts), list(pars), list(cts))
        return tuple(g[:nr]), tuple(jnp.zeros_like(a) for a in consts), tuple(g[nr:])

    op.defvjp(fwd, bwd)
    return op(tuple(rows), tuple(consts), tuple(pars))


def colwise(fn, cols, pars, out_dtype, name):
    t, w = cols[0].shape
    tc = LANES
    nb = w // tc
    ncol, npar = len(cols), len(pars)

    def cspec(a):
        return pl.BlockSpec((a.shape[0], tc), lambda j: (0, j))

    def call_fwd(cols, pars):
        def body(*refs):
            ins = [r[...].astype(f32) for r in refs[:ncol]]
            ps = [r[...] for r in refs[ncol:ncol + npar]]
            refs[-1][...] = fn(*ins, *ps).astype(refs[-1].dtype)

        return pl.pallas_call(
            body, grid=(nb,), in_specs=[cspec(a) for a in cols + pars], out_specs=cspec(cols[0]),
            out_shape=jax.ShapeDtypeStruct((t, w), out_dtype), name=name, compiler_params=_cparams("parallel"))(*cols, *pars)

    def call_bwd(cols, pars, ct):
        def body(*refs):
            ins = [r[...].astype(f32) for r in refs[:ncol]]
            ps = [r[...] for r in refs[ncol:ncol + npar]]
            dy = refs[ncol + npar][...].astype(f32)
            d_refs = refs[ncol + npar + 1:]
            _, vjp = jax.vjp(fn, *ins, *ps)
            grads = vjp(dy)
            for d_ref, g in zip(d_refs, grads):
                d_ref[...] = g.astype(d_ref.dtype)

        return pl.pallas_call(
            body, grid=(nb,), in_specs=[cspec(a) for a in cols + pars + [ct]],
            out_specs=[cspec(a) for a in cols + pars],
            out_shape=[jax.ShapeDtypeStruct(a.shape, a.dtype) for a in cols + pars],
            name=name + "_bwd", compiler_params=_cparams("parallel"))(*cols, *pars, ct)

    @jax.custom_vjp
    def op(cols, pars):
        return call_fwd(list(cols), list(pars))

    def fwd(cols, pars):
        return op(cols, pars), (cols, pars)

    def bwd(res, ct):
        cols, pars = res
        g = call_bwd(list(cols), list(pars), ct)
        return tuple(g[:ncol]), tuple(g[ncol:])

    op.defvjp(fwd, bwd)
    return op(tuple(cols), tuple(pars))


def _sigmoid(x):
    return 1.0 / (1.0 + jnp.exp(-x))


def _silu(x):
    return x * _sigmoid(x)


def _rms(x, g):
    return x * lax.rsqrt(jnp.mean(x * x, axis=-1, keepdims=True) + EPS) * g


def _shift_rows(u, k, n_ctx):
    @jax.custom_vjp
    def op(u):
        t = u.shape[0]
        row = lax.broadcasted_iota(jnp.int32, u.shape, 0)
        edge = ((row == 0) | (row == n_ctx)) if k == 1 else ((row == n_ctx - 1) | (row == t - 1))
        return jnp.where(edge, 0.0, pltpu.roll(u, k % t, 0))

    op.defvjp(lambda u: (op(u), None), lambda _, g: (_shift_rows(g, -k, n_ctx),))
    return op(u)


def _dwconv(u, w0, w1, w2, b, n_ctx):
    return w0 * _shift_rows(u, 1, n_ctx) + w1 * u + w2 * _shift_rows(u, -1, n_ctx) + b


@jax.custom_vjp
def _rot_half(x):
    w = x.shape[1]
    lane = lax.broadcasted_iota(jnp.int32, x.shape, 1)
    return jnp.where((lane % HEAD_DIM) < HEAD_DIM // 2, pltpu.roll(x, w - HEAD_DIM // 2, 1), pltpu.roll(x, HEAD_DIM // 2, 1))


_rot_half.defvjp(lambda x: (_rot_half(x), None), lambda _, g: (_rot_half(g),))


def _head_rms(x, g):
    w = x.shape[1]
    same = (lax.broadcasted_iota(jnp.int32, (w, w), 0) // HEAD_DIM) == (lax.broadcasted_iota(jnp.int32, (w, w), 1) // HEAD_DIM)
    ms = jnp.dot(x * x, same.astype(f32), precision=lax.Precision.HIGHEST, preferred_element_type=f32) * (1.0 / HEAD_DIM)
    return x * lax.rsqrt(ms + EPS) * g


def _band_ok(i, j, c0, shape, tq, tk):
    kpos = j * tk + lax.broadcasted_iota(jnp.int32, shape, 0)
    qpos = i * tq + (c0 + lax.broadcasted_iota(jnp.int32, shape, 1)) % tq
    return jnp.abs(qpos - kpos) <= WINDOW


def _kv_range(i, nb, window):
    is_ctx = i == 0
    if window:
        return jnp.where(is_ctx, 1, jnp.maximum(i - 1, 1)), jnp.where(is_ctx, 1, jnp.minimum(i + 2, nb))
    return 1, jnp.where(is_ctx, 1, nb)


def _sink_row(sink_ref, g, r, tq):
    return jnp.concatenate([jnp.full((1, tq), sink_ref[g * r + h], f32) for h in range(r)], axis=1)


def _attn_fwd_call(q, k, v, sink, window, name, shards=()):
    h, t, dh = q.shape
    nkv = k.shape[0]
    r = h // nkv
    tq = tk = ROW_TILE
    nb = t // tq
    rows = r * tq
    ns = len(shards)

    assert window or nb % 2 == 1, "the dense schedule takes the kv chunks after the context chunk in pairs"

    def body(sink_ref, q_ref, k_ref, v_ref, *rest):
        x_refs, (o_ref, lse_ref), gathered_refs = rest[:ns], rest[ns:ns + 2], rest[ns + 2:2 * ns + 2]
        m_scr, l_scr, acc_scr, s_a, s_b, p_a, p_b, a_a, a_b = rest[2 * ns + 2:2 * ns + 11]
        comm_sems = rest[2 * ns + 11:]
        g, i = pl.program_id(0), pl.program_id(1)
        if ns:
            @pl.when((g == 0) & (i == 0))
            def _():
                _gather_steps(x_refs, gathered_refs, *comm_sems)[0]()

        qv = q_ref[...].reshape(rows, dh)
        m_scr[...] = jnp.full_like(m_scr, NEG)
        l_scr[...] = jnp.zeros_like(l_scr)
        acc_scr[...] = jnp.zeros_like(acc_scr)

        def kv_rows(j):
            return pl.ds(pl.multiple_of(jnp.minimum(j, nb - 1) * tk, tk), tk)

        def scores(j, s_scr):
            s_scr[...] = lax.dot_general(k_ref[0, kv_rows(j), :], qv, NT, preferred_element_type=f32)

        def softmax(j, s_scr, p_scr, a_scr, masked):
            for cb in range(rows // ATTN_SLAB):
                cs = slice(cb * ATTN_SLAB, (cb + 1) * ATTN_SLAB)
                s = s_scr[:, cs]
                if masked:
                    s = jnp.where(_band_ok(i, j, cb * ATTN_SLAB, s.shape, tq, tk), s, NEG)
                m = m_scr[:, cs]
                m2 = jnp.maximum(m, jnp.max(s, axis=0, keepdims=True))
                p = jnp.exp(s - m2)
                a = jnp.exp(m - m2)
                l_scr[:, cs] = a * l_scr[:, cs] + jnp.sum(p, axis=0, keepdims=True)
                m_scr[:, cs] = m2
                a_scr[:, cs] = a
                p_scr[:, cs] = p.astype(bf16)

        def weighted_v(j, p_scr, a_scr):
            acc_scr[...] = a_scr[...] * acc_scr[...] + lax.dot_general(v_ref[0, kv_rows(j), :], p_scr[...], TN, preferred_element_type=f32)

        scores(0, s_a)
        softmax(0, s_a, p_a, a_a, False)
        if window:
            weighted_v(0, p_a, a_a)
            lo, hi = _kv_range(i, nb, window)

            def chunk(j, c):
                scores(j, s_a)
                softmax(j, s_a, p_a, a_a, True)
                weighted_v(j, p_a, a_a)
                return c

            lax.fori_loop(lo, hi, chunk, 0)
        else:
            scores(1, s_b)

            def pair(tt, c):
                j0 = 2 * tt + 1
                scores(j0 + 1, s_a)
                weighted_v(j0 - 1, p_a, a_a)
                softmax(j0, s_b, p_b, a_b, False)
                scores(j0 + 2, s_b)
                weighted_v(j0, p_b, a_b)
                softmax(j0 + 1, s_a, p_a, a_a, False)
                return c

            lax.fori_loop(0, jnp.where(i == 0, 0, (nb - 1) // 2), pair, 0)
            weighted_v(jnp.where(i == 0, 0, nb - 1), p_a, a_a)
        m, l, acc = m_scr[...], l_scr[...], acc_scr[...]
        if window:
            sk = _sink_row(sink_ref, g, r, tq)
            m2 = jnp.maximum(m, sk)
            a = jnp.exp(m - m2)
            l = a * l + jnp.exp(sk - m2)
            acc = a * acc
            m = m2
        o_ref[...] = (acc / l).T.reshape(r, tq, dh).astype(o_ref.dtype)
        lse_ref[0] = m + jnp.log(l)
        if ns:
            @pl.when((g == nkv - 1) & (i == nb - 1))
            def _():
                _gather_steps(x_refs, gathered_refs, *comm_sems)[1]()

    qspec = pl.BlockSpec((r, tq, dh), lambda g, i: (g, i, 0))
    kspec = pl.BlockSpec((1, t, dh), lambda g, i: (g, 0, 0))
    hbm = pl.BlockSpec(memory_space=pl.ANY)
    sem = ("arbitrary", "arbitrary") if ns else ("parallel", "parallel")
    return pl.pallas_call(
        body, grid=(nkv, nb),
        in_specs=[pl.BlockSpec(memory_space=pltpu.SMEM), qspec, kspec, kspec] + [hbm] * ns,
        out_specs=[qspec, pl.BlockSpec((1, 1, rows), lambda g, i: (g * nb + i, 0, 0))] + [hbm] * ns,
        out_shape=[jax.ShapeDtypeStruct((h, t, dh), bf16), jax.ShapeDtypeStruct((nkv * nb, 1, rows), f32)]
        + [jax.ShapeDtypeStruct((8,) + s.shape, s.dtype) for s in shards],
        scratch_shapes=[pltpu.VMEM((1, rows), f32), pltpu.VMEM((1, rows), f32), pltpu.VMEM((dh, rows), f32),
                        pltpu.VMEM((tk, rows), f32), pltpu.VMEM((tk, rows), f32), pltpu.VMEM((tk, rows), bf16),
                        pltpu.VMEM((tk, rows), bf16), pltpu.VMEM((1, rows), f32), pltpu.VMEM((1, rows), f32)]
        + (_gather_scratch(ns) if ns else []),
        name=name, compiler_params=_cparams(*sem))(sink, q, k, v, *shards)


def _attn_bwd_call(q, k, v, sink, o, lse, do, window, name, side_sums=()):
    h, t, dh = q.shape
    nkv = k.shape[0]
    r = h // nkv
    tq = tk = ROW_TILE
    nb = t // tq
    rows = r * tq
    ns = len(side_sums)

    def body(sink_ref, q_ref, k_ref, v_ref, o_ref, lse_ref, do_ref, *rest):
        ss_refs, (dq_ref, dk_ref, dv_ref, dsink_ref), got_refs = rest[:ns], rest[ns:ns + 4], rest[ns + 4:2 * ns + 4]
        s_a, s_b, dp_a, dp_b, p_a, p_b, ds_a, ds_b, dq_scr = rest[2 * ns + 4:2 * ns + 13]
        comm_sems = rest[2 * ns + 13:]
        g, i = pl.program_id(0), pl.program_id(1)
        if ns:
            @pl.when((g == 0) & (i == 0))
            def _():
                for cp in _chips_copies(ss_refs, got_refs, *comm_sems):
                    cp.start()

        @pl.when(i == 0)
        def _():
            dk_ref[...] = jnp.zeros_like(dk_ref)
            dv_ref[...] = jnp.zeros_like(dv_ref)

        qv = q_ref[...].reshape(rows, dh)
        dov = do_ref[...].reshape(rows, dh)
        lse_t = lse_ref[0]
        delta_t = jnp.sum((dov.astype(f32) * o_ref[...].reshape(rows, dh).astype(f32)).T, axis=0, keepdims=True)
        dq_scr[...] = jnp.zeros_like(dq_scr)
        q_t, do_t = qv.T, dov.T

        def kv_rows(j):
            return pl.ds(pl.multiple_of(jnp.minimum(j, nb - 1) * tk, tk), tk)

        def scores(j, s_scr, dp_scr):
            s_scr[...] = lax.dot_general(k_ref[0, kv_rows(j), :], qv, NT, preferred_element_type=f32)
            dp_scr[...] = lax.dot_general(v_ref[0, kv_rows(j), :], dov, NT, preferred_element_type=f32)

        def probs(j, s_scr, dp_scr, p_scr, ds_scr, masked):
            for cb in range(rows // ATTN_SLAB):
                cs = slice(cb * ATTN_SLAB, (cb + 1) * ATTN_SLAB)
                s = s_scr[:, cs]
                if masked:
                    s = jnp.where(_band_ok(i, j, cb * ATTN_SLAB, s.shape, tq, tk), s, NEG)
                p = jnp.exp(s - lse_t[:, cs])
                p_scr[:, cs] = p.astype(bf16)
                ds_scr[:, cs] = (p * (dp_scr[:, cs] - delta_t[:, cs])).astype(bf16)

        def grads(j, p_scr, ds_scr):
            dv_ref[0, :, kv_rows(j)] += lax.dot_general(do_t, p_scr[...], NT, preferred_element_type=f32)
            dk_ref[0, :, kv_rows(j)] += lax.dot_general(q_t, ds_scr[...], NT, preferred_element_type=f32)
            dq_scr[...] += lax.dot_general(k_ref[0, kv_rows(j), :], ds_scr[...], TN, preferred_element_type=f32)

        scores(0, s_a, dp_a)
        probs(0, s_a, dp_a, p_a, ds_a, False)
        if window:
            grads(0, p_a, ds_a)
            lo, hi = _kv_range(i, nb, window)

            def chunk(j, c):
                scores(j, s_a, dp_a)
                probs(j, s_a, dp_a, p_a, ds_a, True)
                grads(j, p_a, ds_a)
                return c

            lax.fori_loop(lo, hi, chunk, 0)
        else:
            scores(1, s_b, dp_b)

            def pair(tt, c):
                j0 = 2 * tt + 1
                scores(j0 + 1, s_a, dp_a)
                grads(j0 - 1, p_a, ds_a)
                probs(j0, s_b, dp_b, p_b, ds_b, False)
                scores(j0 + 2, s_b, dp_b)
                grads(j0, p_b, ds_b)
                probs(j0 + 1, s_a, dp_a, p_a, ds_a, False)
                return c

            lax.fori_loop(0, jnp.where(i == 0, 0, (nb - 1) // 2), pair, 0)
            grads(jnp.where(i == 0, 0, nb - 1), p_a, ds_a)
        dq_ref[...] = dq_scr[...].T.reshape(r, tq, dh).astype(dq_ref.dtype)
        if window:
            dsink_ref[0] = -jnp.exp(_sink_row(sink_ref, g, r, tq) - lse_t) * delta_t
        else:
            dsink_ref[...] = jnp.zeros_like(dsink_ref)
        if ns:
            @pl.when((g == nkv - 1) & (i == nb - 1))
            def _():
                for cp in _chips_copies(ss_refs, got_refs, *comm_sems):
                    cp.wait()

    qspec = pl.BlockSpec((r, tq, dh), lambda g, i: (g, i, 0))
    cspec = pl.BlockSpec((1, 1, rows), lambda g, i: (g * nb + i, 0, 0))
    kspec = pl.BlockSpec((1, t, dh), lambda g, i: (g, 0, 0))
    ktspec = pl.BlockSpec((1, dh, t), lambda g, i: (g, 0, 0))
    hbm = pl.BlockSpec(memory_space=pl.ANY)
    return pl.pallas_call(
        body, grid=(nkv, nb),
        in_specs=[pl.BlockSpec(memory_space=pltpu.SMEM), qspec, kspec, kspec, qspec, cspec, qspec] + [hbm] * ns,
        out_specs=[qspec, ktspec, ktspec, cspec] + [hbm] * ns,
        out_shape=[jax.ShapeDtypeStruct((h, t, dh), bf16), jax.ShapeDtypeStruct((nkv, dh, t), f32), jax.ShapeDtypeStruct((nkv, dh, t), f32),
                   jax.ShapeDtypeStruct((nkv * nb, 1, rows), f32)] + [jax.ShapeDtypeStruct((3,) + s.shape[1:], s.dtype) for s in side_sums],
        scratch_shapes=[pltpu.VMEM((tk, rows), f32)] * 4 + [pltpu.VMEM((tk, rows), bf16)] * 4 + [pltpu.VMEM((dh, rows), f32)]
        + ([pltpu.SemaphoreType.DMA((3 * ns,)), pltpu.SemaphoreType.DMA((3 * ns,))] if ns else []),
        name=name, compiler_params=_cparams("arbitrary" if ns else "parallel", "arbitrary"))(sink, q, k, v, o, lse, do, *side_sums)


def attention(q, k, v, sink, window, name, shards=(), stand_ins=()):
    @jax.custom_vjp
    def op(q, k, v, sink, shards, stand_ins):
        o, _, *gathered = _attn_fwd_call(q, k, v, sink, window, name, shards)
        return o, tuple(gathered)

    def fwd(q, k, v, sink, shards, stand_ins):
        o, lse, *gathered = _attn_fwd_call(q, k, v, sink, window, name, shards)
        return (o, tuple(gathered)), (q, k, v, sink, o, lse, shards)

    def bwd(res, cts):
        q, k, v, sink, o, lse, shards = res
        do, d_gathered = cts
        side_sums = []
        if shards:
            my_c = lax.axis_index("c").reshape(1).astype(jnp.int32)
            from_sibling = rs_to_sibling(list(d_gathered), name + "_rs_sibling")
            side_sums = [pair_sum(s, rr, my_c, f"{name}_pair_sum{a}") for a, (s, rr) in enumerate(zip(d_gathered, from_sibling))]
        dq, dk, dv, dsink_rows, *from_chips = _attn_bwd_call(q, k, v, sink, o, lse, do, window, name + "_bwd", side_sums)
        nkv, r = k.shape[0], q.shape[0] // k.shape[0]
        dsink = jnp.sum(dsink_rows.reshape(nkv, -1, r, ROW_TILE), axis=(1, 3)).reshape(nkv * r)
        reduced = tuple(jnp.concatenate([s, fc, jnp.zeros_like(s[:1])], axis=0) for s, fc in zip(side_sums, from_chips))
        dk, dv = dk.transpose(0, 2, 1).astype(k.dtype), dv.transpose(0, 2, 1).astype(v.dtype)
        return dq, dk, dv, dsink, tuple(jnp.zeros_like(s) for s in shards), reduced

    op.defvjp(fwd, bwd)
    return op(q, k, v, sink, tuple(shards), tuple(stand_ins))


def _ssd_chunk(xs, dtx, dtr, ac, bs, cs, hin, rev):
    q = xs[0].shape[0]
    ii = lax.broadcasted_iota(jnp.int32, (q, q), 0)
    jj = lax.broadcasted_iota(jnp.int32, (q, q), 1)
    tri = (ii <= jj) if rev else (ii >= jj)
    lo = lax.broadcasted_iota(jnp.int32, (q, LANES), 1) < SSM_P
    lo_row = lax.broadcasted_iota(jnp.int32, (1, LANES), 1) < SSM_P
    heads, slabs, per_group = range(SSM_HEADS), range(SSM_HEADS // 2), SSM_HEADS // 2 // SSM_G

    a = [dtr[h] * ac[h] for h in heads]
    c = [jnp.sum(jnp.where(tri, jnp.broadcast_to(a[h], (q, q)), 0.0), axis=1, keepdims=True) for h in heads]
    tot = [jnp.sum(a[h], axis=1, keepdims=True) for h in heads]
    cf = [jnp.broadcast_to(c[h], (q, q)) for h in heads]
    seg = [jnp.minimum(cf[h] - cf[h].T, 0.0) for h in heads]
    decay = [jnp.where(tri, jnp.exp(seg[h]), 0.0) for h in heads]
    cb = [lax.dot_general(cs[g].astype(bf16), bs[g].astype(bf16), NT, preferred_element_type=f32) for g in range(SSM_G)]
    m = [jnp.concatenate([cb[j // per_group] * decay[2 * j], cb[j // per_group] * decay[2 * j + 1]], axis=1).astype(bf16) for j in slabs]
    xdt = [xs[j] * dtx[j] for j in slabs]
    x2 = [jnp.concatenate([jnp.where(lo, xdt[j], 0.0), jnp.where(lo, 0.0, xdt[j])], axis=0).astype(bf16) for j in slabs]
    y_diag = [jnp.dot(m[j], x2[j], preferred_element_type=f32) for j in slabs]
    csel = [jnp.where(lo, cf[2 * j], cf[2 * j + 1]) for j in slabs]
    tsel = [jnp.where(lo_row, jnp.broadcast_to(tot[2 * j], (1, LANES)), jnp.broadcast_to(tot[2 * j + 1], (1, LANES))) for j in slabs]
    xend = [(xdt[j] * jnp.exp(tsel[j] - csel[j])).astype(bf16) for j in slabs]
    st = [lax.dot_general(bs[j // per_group].astype(bf16), xend[j], TN, preferred_element_type=f32) for j in slabs]
    y_off = [jnp.dot(cs[j // per_group].astype(bf16), hin[j].astype(bf16), preferred_element_type=f32) * jnp.exp(csel[j]) for j in slabs]
    return [y_diag[j] + y_off[j] for j in slabs], [hin[j] * jnp.exp(tsel[j]) + st[j] for j in slabs]


def _ssd_order(s, nc, ncc, rev):
    if not rev:
        return s
    return jnp.where(s < ncc, ncc - 1 - s, nc - 1 - (s - ncc))


SSD_SLABS = [slice(LANES * j, LANES * (j + 1)) for j in range(SSM_HEADS // 2)]
SSD_GROUPS = [slice(SSM_N * g, SSM_N * (g + 1)) for g in range(SSM_G)]


def _head_lanes(w, transpose=False):
    shape = (w, SSM_HEADS) if transpose else (SSM_HEADS, w)
    head = lax.broadcasted_iota(jnp.int32, shape, 1 if transpose else 0)
    lane = lax.broadcasted_iota(jnp.int32, shape, 0 if transpose else 1)
    return (lane // SSM_P == head).astype(f32)


def _ssd_fwd_call(xs, dt, dtr, bm, cm, acol, rev, n_ctx, name):
    t, w = xs.shape
    q = SSM_Q
    nc, ncc = t // q, n_ctx // q

    def body(xs_ref, dt_ref, dtr_ref, b_ref, c_ref, a_ref, y_ref, hin_ref, h_scr):
        @pl.when(pl.program_id(0) == 0)
        def _():
            h_scr[...] = jnp.zeros_like(h_scr)

        hin_ref[0] = h_scr[...]
        dtx = jnp.dot(dt_ref[...], _head_lanes(w), precision=lax.Precision.HIGHEST, preferred_element_type=f32)
        ys, houts = _ssd_chunk([xs_ref[:, sl] for sl in SSD_SLABS], [dtx[:, sl] for sl in SSD_SLABS],
                               [dtr_ref[h:h + 1, :] for h in range(SSM_HEADS)], [a_ref[h:h + 1, :] for h in range(SSM_HEADS)],
                               [b_ref[:, gs] for gs in SSD_GROUPS], [c_ref[:, gs] for gs in SSD_GROUPS],
                               [h_scr[:, sl] for sl in SSD_SLABS], rev)
        for sl, y, hout in zip(SSD_SLABS, ys, houts):
            y_ref[:, sl] = y.astype(y_ref.dtype)
            h_scr[:, sl] = hout

    def at(s):
        return _ssd_order(s, nc, ncc, rev)

    return pl.pallas_call(
        body, grid=(nc,),
        in_specs=[pl.BlockSpec((q, w), lambda s: (at(s), 0)), pl.BlockSpec((q, SSM_HEADS), lambda s: (at(s), 0)),
                  pl.BlockSpec((SSM_HEADS, q), lambda s: (0, at(s))),
                  pl.BlockSpec((q, SSM_BC), lambda s: (at(s), 0)), pl.BlockSpec((q, SSM_BC), lambda s: (at(s), 0)),
                  pl.BlockSpec((SSM_HEADS, 1), lambda s: (0, 0))],
        out_specs=[pl.BlockSpec((q, w), lambda s: (at(s), 0)), pl.BlockSpec((1, SSM_N, w), lambda s: (s, 0, 0))],
        out_shape=[jax.ShapeDtypeStruct((t, w), xs.dtype), jax.ShapeDtypeStruct((nc, SSM_N, w), f32)],
        scratch_shapes=[pltpu.VMEM((SSM_N, w), f32)],
        name=name, compiler_params=_cparams("arbitrary"))(xs, dt, dtr, bm, cm, acol)


def _ssd_bwd_call(xs, dt, dtr, bm, cm, acol, hin, dy, rev, n_ctx, name):
    t, w = xs.shape
    q = SSM_Q
    nc, ncc = t // q, n_ctx // q

    def body(xs_ref, dt_ref, dtr_ref, b_ref, c_ref, a_ref, hin_ref, dy_ref,
             dxs_ref, ddt_ref, ddtr_ref, db_ref, dc_ref, da_ref, dh_scr):
        @pl.when(pl.program_id(0) == 0)
        def _():
            dh_scr[...] = jnp.zeros_like(dh_scr)
            da_ref[...] = jnp.zeros_like(da_ref)

        dtx = jnp.dot(dt_ref[...], _head_lanes(w), precision=lax.Precision.HIGHEST, preferred_element_type=f32)
        _, vjp = jax.vjp(
            functools.partial(_ssd_chunk, rev=rev),
            [xs_ref[:, sl].astype(f32) for sl in SSD_SLABS], [dtx[:, sl] for sl in SSD_SLABS],
            [dtr_ref[h:h + 1, :] for h in range(SSM_HEADS)], [a_ref[h:h + 1, :] for h in range(SSM_HEADS)],
            [b_ref[:, gs].astype(f32) for gs in SSD_GROUPS], [c_ref[:, gs].astype(f32) for gs in SSD_GROUPS],
            [hin_ref[0, :, sl] for sl in SSD_SLABS])
        dxs, ddtx, ddtr, dac, dbs, dcs, dhin = vjp(([dy_ref[:, sl].astype(f32) for sl in SSD_SLABS], [dh_scr[:, sl] for sl in SSD_SLABS]))
        for j, sl in enumerate(SSD_SLABS):
            dxs_ref[:, sl] = dxs[j].astype(dxs_ref.dtype)
            dh_scr[:, sl] = dhin[j]
        for h in range(SSM_HEADS):
            ddtr_ref[h:h + 1, :] = ddtr[h]
            da_ref[h:h + 1, :] += dac[h]
        for g, gs in enumerate(SSD_GROUPS):
            db_ref[:, gs] = dbs[g].astype(db_ref.dtype)
            dc_ref[:, gs] = dcs[g].astype(dc_ref.dtype)
        ddt_ref[...] = jnp.dot(jnp.concatenate(ddtx, axis=1), _head_lanes(w, transpose=True),
                               precision=lax.Precision.HIGHEST, preferred_element_type=f32)

    def step(s):
        return nc - 1 - s

    def at(s):
        return _ssd_order(step(s), nc, ncc, rev)

    row = lambda wd: pl.BlockSpec((q, wd), lambda s: (at(s), 0))
    dtr_spec = pl.BlockSpec((SSM_HEADS, q), lambda s: (0, at(s)))
    a_spec = pl.BlockSpec((SSM_HEADS, 1), lambda s: (0, 0))
    return pl.pallas_call(
        body, grid=(nc,),
        in_specs=[row(w), row(SSM_HEADS), dtr_spec, row(SSM_BC), row(SSM_BC), a_spec,
                  pl.BlockSpec((1, SSM_N, w), lambda s: (step(s), 0, 0)), row(w)],
        out_specs=[row(w), row(SSM_HEADS), dtr_spec, row(SSM_BC), row(SSM_BC), a_spec],
        out_shape=[jax.ShapeDtypeStruct((t, w), xs.dtype), jax.ShapeDtypeStruct(dt.shape, f32), jax.ShapeDtypeStruct(dtr.shape, f32),
                   jax.ShapeDtypeStruct(bm.shape, bm.dtype), jax.ShapeDtypeStruct(cm.shape, cm.dtype), jax.ShapeDtypeStruct(acol.shape, f32)],
        scratch_shapes=[pltpu.VMEM((SSM_N, w), f32)],
        name=name, compiler_params=_cparams("arbitrary"))(xs, dt, dtr, bm, cm, acol, hin, dy)


def ssd_scan(xs, dt, dtr, bm, cm, acol, rev, n_ctx, name):
    @jax.custom_vjp
    def op(xs, dt, dtr, bm, cm, acol):
        return _ssd_fwd_call(xs, dt, dtr, bm, cm, acol, rev, n_ctx, name)[0]

    def fwd(xs, dt, dtr, bm, cm, acol):
        y, hin = _ssd_fwd_call(xs, dt, dtr, bm, cm, acol, rev, n_ctx, name)
        return y, (xs, dt, dtr, bm, cm, acol, hin)

    def bwd(res, dy):
        return tuple(_ssd_bwd_call(*res, dy, rev, n_ctx, name + "_bwd"))

    op.defvjp(fwd, bwd)
    return op(xs, dt, dtr, bm, cm, acol)


def _deinterleave(w, n_heads):
    lead = w.shape[:-1]
    return w.reshape(*lead, n_heads, HEAD_DIM // 2, 2).swapaxes(-1, -2).reshape(*lead, n_heads * HEAD_DIM)


def _interleave(w, n_heads):
    lead = w.shape[:-1]
    return w.reshape(*lead, n_heads, 2, HEAD_DIM // 2).swapaxes(-1, -2).reshape(*lead, n_heads * HEAD_DIM)


def _in_layout(d):
    sizes = [('a_q', Q_W, N_HEADS), ('a_k', KV_W, N_KV), ('a_v', KV_W, 0), ('b_z', SSM_INNER, 0),
             ('b_xbc', SSM_INNER + 2 * SSM_BC, 0), ('b_dt', DT_W, 0), ('c_q', Q_W, N_HEADS), ('c_k', KV_W, N_KV),
             ('c_v', KV_W, 0), ('g_a', d, 0), ('g_b', d, 0), ('g_c', d, 0)]
    out, start = [], 0
    for name, n, heads in sizes:
        out.append((name, start, n, heads))
        start += n
    return out


@jax.custom_vjp
def _w_in_split(w):
    d = w.shape[0]
    parts, dt = [], None
    for name, s, n, heads in _in_layout(d):
        p = w[:, s:s + n]
        if heads:
            p = _deinterleave(p, heads)
        if name == 'b_dt':
            dt = jnp.concatenate([p, jnp.zeros((d, DT_PAD - n), w.dtype)], axis=1)
        else:
            parts.append(p)
    return jnp.concatenate(parts, axis=1), dt


def _w_in_join(g_main, g_dt):
    d = g_main.shape[0]
    parts, start = [], 0
    for name, _, n, heads in _in_layout(d):
        if name == 'b_dt':
            parts.append(g_dt[:, :n])
            continue
        p = g_main[:, start:start + n]
        parts.append(_interleave(p, heads) if heads else p)
        start += n
    return jnp.concatenate(parts, axis=1)


_w_in_split.defvjp(lambda w: (_w_in_split(w), None), lambda _, g: (_w_in_join(*g),))


def _rope_tables(n_ctx, n_lat):
    rows = n_lat // GRID_W
    t_row = jnp.repeat(jnp.arange(rows), GRID_W).astype(f32)
    t_col = jnp.tile(jnp.arange(GRID_W), rows).astype(f32)
    n = HEAD_DIM // 4
    inv = ROPE_BASE ** (-jnp.arange(n, dtype=f32) / n)
    ang = jnp.concatenate([t_row[:, None] * inv, t_col[:, None] * inv], axis=-1)
    cos = jnp.concatenate([jnp.ones((n_ctx, HEAD_DIM // 2), f32), jnp.cos(ang)], axis=0)
    sin = jnp.concatenate([jnp.zeros((n_ctx, HEAD_DIM // 2), f32), jnp.sin(ang)], axis=0)
    return jnp.concatenate([cos, cos], axis=1), jnp.concatenate([-sin, sin], axis=1)


def _heads_major(a, n_heads):
    return a.reshape(a.shape[0], n_heads, HEAD_DIM).transpose(1, 0, 2)


def _heads_minor(a):
    return a.transpose(1, 0, 2).reshape(a.shape[1], a.shape[0] * HEAD_DIM)


def _layer(xall, w, s, cm, tabs, n_ctx, li, gather=((), ())):
    t, d = xall.shape
    ncb = n_ctx // ROW_TILE
    nm = f"l{li}_"
    ctq, stq, ctk, stk = tabs
    def mod(blk, cmv, i):
        return jnp.where(blk < ncb, cmv[0:1, i * d:(i + 1) * d], cmv[1:2, i * d:(i + 1) * d])

    def norm_mod(blk, x, g, cmv):
        return (_rms(x, g) * (1.0 + mod(blk, cmv, 1)) + mod(blk, cmv, 0),)

    (h,) = rowwise(norm_mod, [xall], [], [s['norm1'][None], cm], [d], [bf16], nm + "norm1")
    w_main, w_dt = _w_in_split(w['w_in'])
    u = mm(h, w_main, nm + "in")
    b_dt = mm(h, w_dt, nm + "in_dt", f32)
    a_q, a_k, a_v, b_z, b_xbc, c_q, c_k, c_v, g_a, g_b, g_c = split_cols(u, [n for name, _, n, _ in _in_layout(d) if name != 'b_dt'])

    def rope(blk, q, k, v, ct_q, st_q, ct_k, st_k):
        return q * ct_q + _rot_half(q) * st_q, k * ct_k + _rot_half(k) * st_k, v

    def norm_rope(blk, q, k, v, ct_q, st_q, ct_k, st_k, gq, gk):
        return rope(blk, _head_rms(q, gq), _head_rms(k, gk), v, ct_q, st_q, ct_k, st_k)

    qkv_w, qkv_t = [Q_W, KV_W, KV_W], [bf16, bf16, bf16]
    qa, ka, va = rowwise(rope, [a_q, a_k, a_v], [ctq, stq, ctk, stk], [], qkv_w, qkv_t, nm + "ropeA")
    gq = jnp.tile(_deinterleave(s['c_q_norm'], 1), N_HEADS)[None]
    gk = jnp.tile(_deinterleave(s['c_k_norm'], 1), N_KV)[None]
    qc, kc, vc = rowwise(norm_rope, [c_q, c_k, c_v], [ctq, stq, ctk, stk], [gq, gk], qkv_w, qkv_t, nm + "ropeC")
    ya = _heads_minor(attention(_heads_major(qa, N_HEADS), _heads_major(ka, N_KV), _heads_major(va, N_KV),
                                s['a_sink'], True, nm + "attnA")[0])
    yc, gathered = attention(_heads_major(qc, N_HEADS), _heads_major(kc, N_KV), _heads_major(vc, N_KV),
                             jnp.zeros((N_HEADS,), f32), False, nm + "attnC", *gather)
    yc = _heads_minor(yc)

    cw, cb = s['ssm_conv_w'], s['ssm_conv_b']
    conv_silu = lambda uu, w3, b: _silu(_dwconv(uu, w3[0:1], w3[1:2], w3[2:3], b, n_ctx))
    xbc = colwise(conv_silu, [b_xbc], [cw, cb[None]], bf16, nm + "ssmconv")
    xs, bm, cmat = split_cols(xbc, [SSM_INNER, SSM_BC, SSM_BC])
    bias = jnp.concatenate([s['ssm_dt_bias'].reshape(1, DT_W), jnp.zeros((1, DT_PAD - DT_W), f32)], axis=1)

    def softplus(blk, r, b):
        z = r + b
        return (jnp.maximum(z, 0.0) + jnp.log(1.0 + jnp.exp(-jnp.abs(z))),)

    (dt_all,) = rowwise(softplus, [b_dt], [], [bias], [DT_PAD], [f32], nm + "dt")
    a_coef = -jnp.exp(s['ssm_A_log'])
    ys_dir = []
    for di, rev in enumerate((False, True)):
        dt = dt_all[:, di * SSM_HEADS:(di + 1) * SSM_HEADS]
        ys_dir.append(ssd_scan(xs, dt, dt.T, bm, cmat, a_coef[di][:, None], rev, n_ctx,
                               nm + ("ssd_r" if rev else "ssd_f")))

    def ssm_out(blk, yf, yb, x, z, dskip, g):
        return (_rms((yf + yb + x * dskip) * _silu(z), g),)

    (ysn,) = rowwise(ssm_out, [ys_dir[0], ys_dir[1], xs, b_z], [], [jnp.repeat(s['ssm_D'], SSM_P)[None], s['ssm_norm'][None]],
                     [SSM_INNER], [bf16], nm + "ssmout")

    pa, pb, pc = mm(ya, w['w_oa'], nm + "oa"), mm(ysn, w['w_ob'], nm + "ob"), mm(yc, w['w_oc'], nm + "oc")

    def merge(blk, ga, gb, gc, a, b, c):
        return (_sigmoid(ga) * a + _sigmoid(gb) * b + _sigmoid(gc) * c,)

    (mrg,) = rowwise(merge, [g_a, g_b, g_c, pa, pb, pc], [], [], [d], [bf16], nm + "merge")
    o = mm(mrg, w['w_out'], nm + "out")

    def resid_norm_mod(blk, x, oo, g, cmv):
        x1 = x + mod(blk, cmv, 2) * oo
        return x1, _rms(x1, g) * (1.0 + mod(blk, cmv, 4)) + mod(blk, cmv, 3)

    x1, h2 = rowwise(resid_norm_mod, [xall, o], [], [s['norm2'][None], cm], [d, d], [f32, bf16], nm + "norm2")
    up, gt = mm(h2, w['ffn_w_up'], nm + "up"), mm(h2, w['ffn_w_gate'], nm + "gate")
    fw, fb = s['ffn_conv_w'], s['ffn_conv_b']
    ffn_act = lambda g_, u_, w3, b: _silu(_dwconv(g_, w3[0:1], w3[1:2], w3[2:3], b, n_ctx)) * u_
    act = colwise(ffn_act, [gt, up], [fw, fb[None]], bf16, nm + "ffnact")
    f = mm(act, w['ffn_w_down'], nm + "down")

    def resid(blk, x, ff, cmv):
        return (x + mod(blk, cmv, 5) * ff,)

    (x2,) = rowwise(resid, [x1, f], [], [cm], [d], [f32], nm + "resid")
    return x2, gathered


def _assemble(name, gathered):
    return jnp.concatenate([gathered[j] for j in range(8)], axis=1 if BIG[name] == 1 else 0)


def _loss_fn(big0, shards1, stand_ins1, small, x, ctx, c, target, n_ctx):
    n_lat, d = x.shape
    xall = jnp.concatenate([ctx, x], axis=0)
    ct, st = _rope_tables(n_ctx, n_lat)
    tabs = (jnp.tile(ct, (1, N_HEADS)) * HEAD_DIM ** -0.5, jnp.tile(st, (1, N_HEADS)) * HEAD_DIM ** -0.5,
            jnp.tile(ct, (1, N_KV)), jnp.tile(st, (1, N_KV)))
    srows = jnp.concatenate([_silu(small['c_ctx'])[None], _silu(c), jnp.zeros((14, d), f32)], axis=0)
    names = list(BIG)
    big, gather = big0, ([shards1[n] for n in names], [stand_ins1[n] for n in names])
    for li in range(2):
        cm = mm(srows, big['w_mod'], f"l{li}_mod", f32)[0:2] + small['b_mod'][li][None]
        sl = {k: v[li] for k, v in small.items() if k not in ('c_ctx', 'final_norm')}
        xall, gathered = _layer(xall, big, sl, cm, tabs, n_ctx, li, gather)
        if li == 0:
            big, gather = {n: _assemble(n, g) for n, g in zip(names, gathered)}, ((), ())
    ncb = n_ctx // ROW_TILE
    tgt = jnp.concatenate([jnp.zeros((n_ctx, d), f32), target], axis=0)

    def loss_rows(blk, xx, tg, g):
        e = _rms(xx, g) - tg
        return (jnp.where(blk < ncb, 0.0, 0.5) * jnp.mean(e * e, axis=-1, keepdims=True),)

    (rows,) = rowwise(loss_rows, [xall], [tgt], [small['final_norm'][None]], [1], [f32], "loss")
    return jnp.sum(rows)


def _hbm_call(body, ins, out_shapes, n_sems, name):
    any_spec = pl.BlockSpec(memory_space=pl.ANY)
    return pl.pallas_call(
        body, out_shape=out_shapes, in_specs=[any_spec] * len(ins), out_specs=[any_spec] * len(out_shapes),
        scratch_shapes=[pltpu.SemaphoreType.DMA((n_sems,)), pltpu.SemaphoreType.DMA((n_sems,)), pltpu.SemaphoreType.DMA((len(ins),))],
        name=name)(*ins)


def _gather_steps(x_refs, out_refs, send_sems, recv_sems, local_sems):
    n = len(x_refs)
    x, y, c = lax.axis_index("x"), lax.axis_index("y"), lax.axis_index("c")
    me, sibling = (x, y, c), (x, y, 1 - c)
    chips = [(1 - x, y), (x, 1 - y), (1 - x, 1 - y)]

    def copy(a, k, block, to, src=None):
        px, py, pc = block
        slot = out_refs[a].at[4 * px + 2 * py + pc]
        return pltpu.make_async_remote_copy(
            src_ref=slot if src is None else src, dst_ref=slot,
            send_sem=send_sems.at[7 * a + k], recv_sem=recv_sems.at[7 * a + k], device_id=to, device_id_type=MESH)

    mine = [pltpu.make_async_copy(x_refs[a], out_refs[a].at[4 * x + 2 * y + c], local_sems.at[a]) for a in range(n)]
    first = []
    for a in range(n):
        first += [copy(a, 1 + j, me, (*chip, c), src=x_refs[a]) for j, chip in enumerate(chips)]
        first.append(copy(a, 0, me, sibling, src=x_refs[a]))

    def start():
        for cp in mine + first:
            cp.start()

    def finish():
        passed = []
        for a in range(n):
            for j, chip in enumerate(chips):
                copy(a, 1 + j, (*chip, c), me).wait_recv()
                passed.append(copy(a, 4 + j, (*chip, c), sibling))
                passed[-1].start()
        for a in range(n):
            copy(a, 0, sibling, me).wait_recv()
            for j, chip in enumerate(chips):
                copy(a, 4 + j, (*chip, 1 - c), me).wait_recv()
        for cp in first + passed:
            cp.wait_send()
        for cp in mine:
            cp.wait()

    return start, finish


def _gather_scratch(n):
    return [pltpu.SemaphoreType.DMA((7 * n,)), pltpu.SemaphoreType.DMA((7 * n,)), pltpu.SemaphoreType.DMA((n,))]


def all_gather(shards, name):
    n = len(shards)

    def body(*refs):
        start, finish = _gather_steps(refs[:n], refs[n:2 * n], *refs[2 * n:])
        start()
        finish()

    return _hbm_call(body, shards, [jax.ShapeDtypeStruct((8,) + s.shape, s.dtype) for s in shards], 7 * n, name)


def rs_to_sibling(gs, name="rs_sibling"):
    n = len(gs)

    def body(*refs):
        g_refs, out_refs, (send_sems, recv_sems, _) = refs[:n], refs[n:2 * n], refs[2 * n:]
        x, y, c = lax.axis_index("x"), lax.axis_index("y"), lax.axis_index("c")
        copies = [pltpu.make_async_remote_copy(
            src_ref=g_refs[a].at[2 * k + (1 - c)], dst_ref=out_refs[a].at[k], send_sem=send_sems.at[4 * a + k],
            recv_sem=recv_sems.at[4 * a + k], device_id=(x, y, 1 - c), device_id_type=MESH) for a in range(n) for k in range(4)]
        for cp in copies:
            cp.start()
        for cp in copies:
            cp.wait()

    return _hbm_call(body, gs, [jax.ShapeDtypeStruct((4,) + g.shape[1:], g.dtype) for g in gs], 4 * n, name)


def rs_to_chips(ss):
    n = len(ss)

    def body(*refs):
        copies = _chips_copies(refs[:n], refs[n:2 * n], refs[2 * n], refs[2 * n + 1])
        for cp in copies:
            cp.start()
        for cp in copies:
            cp.wait()

    return _hbm_call(body, ss, [jax.ShapeDtypeStruct((3,) + s.shape[1:], s.dtype) for s in ss], 3 * n, "rs_chips")


def _chips_copies(s_refs, out_refs, send_sems, recv_sems):
    x, y, c = lax.axis_index("x"), lax.axis_index("y"), lax.axis_index("c")
    copies = []
    for a in range(len(s_refs)):
        for k, (fx, fy) in enumerate([(1, 0), (0, 1), (1, 1)]):
            px, py = (1 - x) if fx else x, (1 - y) if fy else y
            copies.append(pltpu.make_async_remote_copy(
                src_ref=s_refs[a].at[2 * px + py], dst_ref=out_refs[a].at[k], send_sem=send_sems.at[3 * a + k],
                recv_sem=recv_sems.at[3 * a + k], device_id=(px, py, c), device_id_type=MESH))
    return copies


def _flat_tile(rows, cols):
    return _row_tile(rows, 4 * 4 * cols)


def pair_sum(g, r1, my_c, name):
    _, rows, cols = g.shape
    tm = _flat_tile(rows, cols)

    def body(c_ref, g_ref, r_ref, o_ref):
        o_ref[...] = (g_ref[...].astype(f32) + r_ref[...].astype(f32)).astype(o_ref.dtype)

    return pl.pallas_call(
        body, grid_spec=pltpu.PrefetchScalarGridSpec(
            num_scalar_prefetch=1, grid=(4, rows // tm),
            in_specs=[pl.BlockSpec((1, tm, cols), lambda k, i, c: (2 * k + c[0], i, 0)),
                      pl.BlockSpec((1, tm, cols), lambda k, i, c: (k, i, 0))],
            out_specs=pl.BlockSpec((1, tm, cols), lambda k, i, c: (k, i, 0))),
        out_shape=jax.ShapeDtypeStruct((4, rows, cols), g.dtype), name=name,
        compiler_params=_cparams("parallel", "parallel"))(my_c, g, r1)


def _adam_math(w, g, m, v):
    m2 = ADAM_B1 * m + (1.0 - ADAM_B1) * g
    v2 = ADAM_B2 * v + (1.0 - ADAM_B2) * (g * g)
    m_hat = m2 / (1.0 - ADAM_B1 ** ADAM_STEP)
    v_hat = v2 / (1.0 - ADAM_B2 ** ADAM_STEP)
    return -ADAM_LR * (m_hat / (jnp.sqrt(v_hat) + ADAM_EPS) + ADAM_WD * w), m2, v2


def sum_adam(parts, w, m, v, name):
    groups, rows, cols = w.shape
    tm = _flat_tile(rows, cols)
    nblk = rows // tm
    flat = []
    scalars = [p[2] for ps in parts for p in ps if p[2] is not None]
    for gi, ps in enumerate(parts):
        flat.append([])
        for arr, static_rows, dyn in ps:
            if dyn is not None:
                flat[gi].append((arr, functools.partial(lambda l, i, s, gi: (s[0], jnp.where(l == gi, i, nblk - 1), 0), gi=gi)))
            else:
                for k in static_rows:
                    flat[gi].append((arr, functools.partial(lambda l, i, s, gi, k: (k, jnp.where(l == gi, i, nblk - 1), 0), gi=gi, k=k)))
    counts = [len(f) for f in flat]
    na = sum(counts)

    def body(s_ref, *refs):
        sums, at = [], 0
        for cnt in counts:
            g = refs[at][0].astype(f32)
            for r in refs[at + 1:at + cnt]:
                g = g + r[0].astype(f32)
            sums.append(g)
            at += cnt
        g = sums[0]
        for gi in range(1, groups):
            g = jnp.where(pl.program_id(0) == gi, sums[gi], g)
        w_ref, m_ref, v_ref = refs[na:na + 3]
        g_out, d_out, m_out, v_out = refs[na + 3:]
        d, m2, v2 = _adam_math(w_ref[0], g, m_ref[0], v_ref[0])
        g_out[0] = g
        d_out[0] = d
        m_out[0] = m2
        v_out[0] = v2

    blk = pl.BlockSpec((1, tm, cols), lambda l, i, s: (l, i, 0))
    scalar = scalars[0] if scalars else jnp.zeros((1,), jnp.int32)
    return pl.pallas_call(
        body, grid_spec=pltpu.PrefetchScalarGridSpec(
            num_scalar_prefetch=1, grid=(groups, nblk),
            in_specs=[pl.BlockSpec((1, tm, cols), im) for f in flat for _, im in f] + [blk, blk, blk],
            out_specs=[blk, blk, blk, blk]),
        out_shape=[jax.ShapeDtypeStruct((groups, rows, cols), f32)] * 4, name=name,
        compiler_params=_cparams("arbitrary", "arbitrary"))(scalar, *[a for f in flat for a, _ in f], w, m, v)


FLAT_COLS = 1024


def _to_flat(vec):
    n = vec.shape[0]
    total = -(-n // (8 * FLAT_COLS)) * 8 * FLAT_COLS
    return jnp.concatenate([vec, jnp.zeros((total - n,), vec.dtype)]).reshape(-1, FLAT_COLS)


def _pack(tree, names):
    return jnp.concatenate([tree[n].reshape(-1) for n in names])


def _unpack(vec, like, names):
    out, off = {}, 0
    for n in names:
        size = like[n].size
        out[n] = vec[off:off + size].reshape(like[n].shape)
        off += size
    return out


def kernel(x, c, ctx, c_ctx, w_mod, b_mod, norm1, norm2, w_in, a_sink, ssm_conv_w, ssm_conv_b, ssm_A_log, ssm_dt_bias, ssm_D, ssm_norm, c_q_norm, c_k_norm, w_oa, w_ob, w_oc, w_out, ffn_w_up, ffn_w_gate, ffn_conv_w, ffn_conv_b, ffn_w_down, final_norm, loss_target, m_c_ctx, m_w_mod, m_b_mod, m_norm1, m_norm2, m_w_in, m_a_sink, m_ssm_conv_w, m_ssm_conv_b, m_ssm_A_log, m_ssm_dt_bias, m_ssm_D, m_ssm_norm, m_c_q_norm, m_c_k_norm, m_w_oa, m_w_ob, m_w_oc, m_w_out, m_ffn_w_up, m_ffn_w_gate, m_ffn_conv_w, m_ffn_conv_b, m_ffn_w_down, m_final_norm, v_c_ctx, v_w_mod, v_b_mod, v_norm1, v_norm2, v_w_in, v_a_sink, v_ssm_conv_w, v_ssm_conv_b, v_ssm_A_log, v_ssm_dt_bias, v_ssm_D, v_ssm_norm, v_c_q_norm, v_c_k_norm, v_w_oa, v_w_ob, v_w_oc, v_w_out, v_ffn_w_up, v_ffn_w_gate, v_ffn_conv_w, v_ffn_conv_b, v_ffn_w_down, v_final_norm):
    args = (x, c, ctx, c_ctx, w_mod, b_mod, norm1, norm2, w_in, a_sink, ssm_conv_w, ssm_conv_b, ssm_A_log, ssm_dt_bias, ssm_D, ssm_norm, c_q_norm, c_k_norm, w_oa, w_ob, w_oc, w_out, ffn_w_up, ffn_w_gate, ffn_conv_w, ffn_conv_b, ffn_w_down, final_norm, loss_target)
    moms = (m_c_ctx, m_w_mod, m_b_mod, m_norm1, m_norm2, m_w_in, m_a_sink, m_ssm_conv_w, m_ssm_conv_b, m_ssm_A_log, m_ssm_dt_bias, m_ssm_D, m_ssm_norm, m_c_q_norm, m_c_k_norm, m_w_oa, m_w_ob, m_w_oc, m_w_out, m_ffn_w_up, m_ffn_w_gate, m_ffn_conv_w, m_ffn_conv_b, m_ffn_w_down, m_final_norm)
    vars_ = (v_c_ctx, v_w_mod, v_b_mod, v_norm1, v_norm2, v_w_in, v_a_sink, v_ssm_conv_w, v_ssm_conv_b, v_ssm_A_log, v_ssm_dt_bias, v_ssm_D, v_ssm_norm, v_c_q_norm, v_c_k_norm, v_w_oa, v_w_ob, v_w_oc, v_w_out, v_ffn_w_up, v_ffn_w_gate, v_ffn_conv_w, v_ffn_conv_b, v_ffn_w_down, v_final_norm)
    p = dict(zip(IN_NAMES, args))
    mom = dict(zip(WEIGHTS, moms))
    var = dict(zip(WEIGHTS, vars_))
    depth = w_in.shape[0]
    n_ctx = ctx.shape[1]
    xi, yi, ci = lax.axis_index("x"), lax.axis_index("y"), lax.axis_index("c")
    dev = 4 * xi + 2 * yi + ci
    big_names = list(BIG)

    assert depth == 2 and n_ctx == ROW_TILE
    shards = [{n: p[n][li].astype(bf16) for n in big_names} for li in range(depth)]
    g_big0 = all_gather([shards[0][n] for n in big_names], "gather_l0")
    g_conv = all_gather([_to_flat(_pack(p, CONV_W))], "gather_conv")[0].reshape(8, -1)
    big0 = {n: _assemble(n, g) for n, g in zip(big_names, g_big0)}
    stand_ins = {n: jnp.zeros((8,) + shards[1][n].shape, bf16) for n in big_names}
    conv_full, off = {}, 0
    for n in CONV_W:
        shp = p[n].shape
        seg = g_conv[:, off:off + p[n].size].reshape(8, *shp)
        conv_full[n] = jnp.moveaxis(seg, 0, -2).reshape(*shp[:-1], 8 * shp[-1])
        off += p[n].size
    small = {n: p[n] for n in REPL}
    small.update(conv_full)

    loss, (g_big0, g_big1, g_small, g_x) = jax.value_and_grad(_loss_fn, argnums=(0, 2, 3, 4))(
        big0, shards[1], stand_ins, small, x[0], ctx[0], c, loss_target[0], n_ctx)
    loss = lax.psum(loss, AXES)

    def send_rows(n):
        b = p[n].shape[-1] if BIG[n] == 1 else p[n].shape[1]
        cut = (lambda g, j: g[:, b * j:b * (j + 1)]) if BIG[n] == 1 else (lambda g, j: g[b * j:b * (j + 1), :])
        return jnp.stack([cut(g_big0[n], j) for j in range(8)])

    send = [send_rows(n) for n in big_names]
    from_sibling = rs_to_sibling(send)
    my_c = ci.reshape(1).astype(jnp.int32)
    side_sum = [pair_sum(s, r, my_c, "rs_pair_sum_" + n) for n, s, r in zip(big_names, send, from_sibling)]
    from_chips = rs_to_chips(side_sum)
    chip = (2 * xi + yi).reshape(1).astype(jnp.int32)
    big_out = [{}, {}, {}, {}]
    for a, n in enumerate(big_names):
        parts = [[(side_sum[a], None, chip), (from_chips[a], (0, 1, 2), None)],
                 [(g_big1[n], None, chip), (g_big1[n], (4, 5, 6), None)]]
        outs = sum_adam(parts, p[n], mom[n], var[n], "adam_" + n)
        for k in range(4):
            big_out[k][n] = outs[k]

    sm_names = REPL + list(CONV_W)
    g_vec = _to_flat(_pack(g_small, sm_names))
    gathered = all_gather([g_vec], "gather_small_grads")[0]
    n_repl = sum(p[n].size for n in REPL)

    def repl_flat(tree):
        return _to_flat(jnp.concatenate([_pack(tree, REPL), jnp.zeros((g_vec.size - n_repl,), f32)]))

    outs_small = sum_adam([[(gathered, tuple(range(8)), None)]], repl_flat(p)[None], repl_flat(mom)[None], repl_flat(var)[None],
                          "adam_small")
    g_sum = outs_small[0].reshape(-1)
    small_out = [_unpack(o.reshape(-1), p, REPL) for o in outs_small]
    conv_g_full = _unpack(g_sum[n_repl:], conv_full, CONV_W)
    conv_g = {n: lax.dynamic_slice_in_dim(conv_g_full[n], dev * p[n].shape[-1], p[n].shape[-1], axis=2) for n in CONV_W}
    conv_gv = _to_flat(_pack(conv_g, CONV_W))
    outs_conv = sum_adam([[(conv_gv[None], (0,), None)]], _to_flat(_pack(p, CONV_W))[None], _to_flat(_pack(mom, CONV_W))[None],
                         _to_flat(_pack(var, CONV_W))[None], "adam_conv")
    conv_out = [_unpack(o.reshape(-1), p, CONV_W) for o in outs_conv]

    res = []
    for k in range(4):
        tree = {**big_out[k], **small_out[k], **conv_out[k]}
        res.append([tree[n] for n in WEIGHTS])
    return (loss, g_x[None], *res[0], *res[1], *res[2], *res[3])
```

```python
import functools

import jax
import jax.numpy as jnp
from jax import lax
from jax.experimental import pallas as pl
from jax.experimental.pallas import tpu as pltpu

f32 = jnp.float32
bf16 = jnp.bfloat16
MESH = pl.DeviceIdType.MESH
AXES = ("x", "y", "c")

GRID_W = 64
HEAD_DIM = 64
ROPE_BASE = 10000.0
EPS = 1e-6
WINDOW = 128
N_HEADS = 8
N_KV = 2
SSM_HEADS = 16
SSM_P = 64
SSM_G = 2
SSM_N = 128
SSM_INNER = SSM_HEADS * SSM_P
SSM_BC = SSM_G * SSM_N
SSM_Q = 128
Q_W = N_HEADS * HEAD_DIM
KV_W = N_KV * HEAD_DIM
DT_W = 2 * SSM_HEADS
DT_PAD = 128
ADAM_LR, ADAM_B1, ADAM_B2, ADAM_EPS, ADAM_WD, ADAM_STEP = 0.001, 0.9, 0.999, 1e-08, 0.01, 10

LANES = 128
ROW_TILE = 256
VMEM_BLOCK_BUDGET = 6 * 1024 * 1024
ATTN_SLAB = 128
MM_ROW_CAP = 1088
MM_TILE_CAP = 1536
NEG = -1e30

IN_NAMES = ['x', 'c', 'ctx', 'c_ctx', 'w_mod', 'b_mod', 'norm1', 'norm2', 'w_in', 'a_sink', 'ssm_conv_w', 'ssm_conv_b', 'ssm_A_log', 'ssm_dt_bias', 'ssm_D', 'ssm_norm', 'c_q_norm', 'c_k_norm', 'w_oa', 'w_ob', 'w_oc', 'w_out', 'ffn_w_up', 'ffn_w_gate', 'ffn_conv_w', 'ffn_conv_b', 'ffn_w_down', 'final_norm', 'loss_target']
WEIGHTS = IN_NAMES[3:28]
BIG = {'w_mod': 1, 'w_in': 1, 'w_oa': 1, 'w_ob': 0, 'w_oc': 1, 'w_out': 0, 'ffn_w_up': 1, 'ffn_w_gate': 1, 'ffn_w_down': 0}
CONV_W = ('ssm_conv_w', 'ffn_conv_w')
REPL = [n for n in WEIGHTS if n not in BIG and n not in CONV_W]

NT = (((1,), (1,)), ((), ()))
TN = (((0,), (0,)), ((), ()))
NN = (((1,), (0,)), ((), ()))


def _cparams(*sem):
    return pltpu.CompilerParams(dimension_semantics=sem)


def _div_tile(n, unit, cap):
    for d in range(min(n, int(cap)), 0, -1):
        if n % d == 0 and d % unit == 0:
            return d
    return n


def _row_tile(m, row_bytes):
    return _div_tile(m, 16, max(16, VMEM_BLOCK_BUDGET // row_bytes))


def _mm_call(a, b, mode, out_dtype, name):
    if mode == "nn":
        (m, k), n = a.shape, b.shape[1]
    elif mode == "nt":
        (m, k), n = a.shape, b.shape[0]
    else:
        (k, m), n = a.shape, b.shape[1]
    dims = {"nn": NN, "nt": NT, "tn": TN}[mode]
    ia, ib = a.dtype.itemsize, b.dtype.itemsize
    tm = _div_tile(m, LANES, MM_TILE_CAP) if mode == "tn" else _div_tile(m, 16, MM_ROW_CAP)
    tn = _div_tile(n, LANES, min(MM_TILE_CAP, VMEM_BLOCK_BUDGET // (4 * tm)))
    tk = _div_tile(k, 16 if mode == "tn" else LANES,
                   min(MM_ROW_CAP if mode == "tn" else MM_TILE_CAP, VMEM_BLOCK_BUDGET // (tm * ia), VMEM_BLOCK_BUDGET // (tn * ib)))
    nk = k // tk

    def body(a_ref, b_ref, o_ref, *acc):
        part = lax.dot_general(a_ref[...].astype(bf16), b_ref[...].astype(bf16), dims, preferred_element_type=f32)
        if nk == 1:
            o_ref[...] = part.astype(o_ref.dtype)
            return
        kk = pl.program_id(2)

        @pl.when(kk == 0)
        def _():
            acc[0][...] = part

        @pl.when(kk > 0)
        def _():
            acc[0][...] += part

        @pl.when(kk == nk - 1)
        def _():
            o_ref[...] = acc[0][...].astype(o_ref.dtype)

    a_spec = pl.BlockSpec((tk, tm), lambda i, j, kk: (kk, i)) if mode == "tn" else pl.BlockSpec((tm, tk), lambda i, j, kk: (i, kk))
    b_spec = pl.BlockSpec((tn, tk), lambda i, j, kk: (j, kk)) if mode == "nt" else pl.BlockSpec((tk, tn), lambda i, j, kk: (kk, j))
    return pl.pallas_call(
        body, grid=(m // tm, n // tn, nk), in_specs=[a_spec, b_spec],
        out_specs=pl.BlockSpec((tm, tn), lambda i, j, kk: (i, j)),
        out_shape=jax.ShapeDtypeStruct((m, n), out_dtype),
        scratch_shapes=[pltpu.VMEM((tm, tn), f32)] if nk > 1 else [], name=name,
        compiler_params=_cparams("parallel", "parallel", "arbitrary"))(a, b)


def mm(a, b, name, out_dtype=None):
    @jax.custom_vjp
    def op(a, b):
        return _mm_call(a, b, "nn", out_dtype or bf16, name)

    def fwd(a, b):
        return op(a, b), (a, b)

    def bwd(res, g):
        a, b = res
        return _mm_call(g, b, "nt", a.dtype, name + "_da"), _mm_call(a, g, "tn", b.dtype, name + "_db")

    op.defvjp(fwd, bwd)
    return op(a, b)


def split_cols(u, widths):
    offs = [0]
    for w in widths:
        offs.append(offs[-1] + w)

    @jax.custom_vjp
    def op(u):
        return tuple(u[:, offs[i]:offs[i + 1]] for i in range(len(widths)))

    def fwd(u):
        return op(u), None

    def bwd(_, cts):
        return (jnp.concatenate(cts, axis=1),)

    op.defvjp(fwd, bwd)
    return op(u)


def rowwise(fn, rows, consts, pars, out_widths, out_dtypes, name):
    t = rows[0].shape[0]
    tm = ROW_TILE
    nb = t // tm
    nr, nc, npar = len(rows), len(consts), len(pars)

    def rspec(a):
        return pl.BlockSpec((tm, a.shape[1]), lambda i: (i, 0))

    def pspec(a):
        return pl.BlockSpec(a.shape, lambda i: (0,) * a.ndim)

    def call_fwd(rows, consts, pars):
        def body(*refs):
            blk = pl.program_id(0)
            ins = [r[...].astype(f32) for r in refs[:nr + nc]]
            ps = [r[...] for r in refs[nr + nc:nr + nc + npar]]
            outs = fn(blk, *ins, *ps)
            for o_ref, o in zip(refs[nr + nc + npar:], outs):
                o_ref[...] = o.astype(o_ref.dtype)

        return pl.pallas_call(
            body, grid=(nb,),
            in_specs=[rspec(a) for a in rows + consts] + [pspec(a) for a in pars],
            out_specs=[pl.BlockSpec((tm, w), lambda i: (i, 0)) for w in out_widths],
            out_shape=[jax.ShapeDtypeStruct((t, w), d) for w, d in zip(out_widths, out_dtypes)],
            name=name, compiler_params=_cparams("parallel"))(*rows, *consts, *pars)

    def call_bwd(rows, consts, pars, cts):
        nout = len(cts)

        def body(*refs):
            blk = pl.program_id(0)
            ins = [r[...].astype(f32) for r in refs[:nr]]
            cs = [r[...].astype(f32) for r in refs[nr:nr + nc]]
            ps = [r[...] for r in refs[nr + nc:nr + nc + npar]]
            dys = [r[...].astype(f32) for r in refs[nr + nc + npar:nr + nc + npar + nout]]
            d_refs = refs[nr + nc + npar + nout:]
            _, vjp = jax.vjp(lambda *a: tuple(fn(blk, *a[:nr], *cs, *a[nr:])), *ins, *ps)
            grads = vjp(tuple(dys))
            for d_ref, g in zip(d_refs[:nr], grads[:nr]):
                d_ref[...] = g.astype(d_ref.dtype)
            if npar:
                @pl.when(blk == 0)
                def _():
                    for d_ref in d_refs[nr:]:
                        d_ref[...] = jnp.zeros_like(d_ref)

                for d_ref, g in zip(d_refs[nr:], grads[nr:]):
                    d_ref[...] += g

        return pl.pallas_call(
            body, grid=(nb,),
            in_specs=[rspec(a) for a in rows + consts] + [pspec(a) for a in pars] + [rspec(a) for a in cts],
            out_specs=[rspec(a) for a in rows] + [pspec(a) for a in pars],
            out_shape=[jax.ShapeDtypeStruct(a.shape, a.dtype) for a in rows + pars],
            name=name + "_bwd", compiler_params=_cparams("arbitrary"))(*rows, *consts, *pars, *cts)

    @jax.custom_vjp
    def op(rows, consts, pars):
        return tuple(call_fwd(list(rows), list(consts), list(pars)))

    def fwd(rows, consts, pars):
        return op(rows, consts, pars), (rows, consts, pars)

    def bwd(res, cts):
        rows, consts, pars = res
        g = call_bwd(list(rows), list(consts), list(pars), list(cts))
        return tuple(g[:nr]), tuple(jnp.zeros_like(a) for a in consts), tuple(g[nr:])

    op.defvjp(fwd, bwd)
    return op(tuple(rows), tuple(consts), tuple(pars))


def colwise(fn, cols, pars, out_dtype, name):
    t, w = cols[0].shape
    tc = LANES
    nb = w // tc
    ncol, npar = len(cols), len(pars)

    def cspec(a):
        return pl.BlockSpec((a.shape[0], tc), lambda j: (0, j))

    def call_fwd(cols, pars):
        def body(*refs):
            ins = [r[...].astype(f32) for r in refs[:ncol]]
            ps = [r[...] for r in refs[ncol:ncol + npar]]
            refs[-1][...] = fn(*ins, *ps).astype(refs[-1].dtype)

        return pl.pallas_call(
            body, grid=(nb,), in_specs=[cspec(a) for a in cols + pars], out_specs=cspec(cols[0]),
            out_shape=jax.ShapeDtypeStruct((t, w), out_dtype), name=name, compiler_params=_cparams("parallel"))(*cols, *pars)

    def call_bwd(cols, pars, ct):
        def body(*refs):
            ins = [r[...].astype(f32) for r in refs[:ncol]]
            ps = [r[...] for r in refs[ncol:ncol + npar]]
            dy = refs[ncol + npar][...].astype(f32)
            d_refs = refs[ncol + npar + 1:]
            _, vjp = jax.vjp(fn, *ins, *ps)
            grads = vjp(dy)
            for d_ref, g in zip(d_refs, grads):
                d_ref[...] = g.astype(d_ref.dtype)

        return pl.pallas_call(
            body, grid=(nb,), in_specs=[cspec(a) for a in cols + pars + [ct]],
            out_specs=[cspec(a) for a in cols + pars],
            out_shape=[jax.ShapeDtypeStruct(a.shape, a.dtype) for a in cols + pars],
            name=name + "_bwd", compiler_params=_cparams("parallel"))(*cols, *pars, ct)

    @jax.custom_vjp
    def op(cols, pars):
        return call_fwd(list(cols), list(pars))

    def fwd(cols, pars):
        return op(cols, pars), (cols, pars)

    def bwd(res, ct):
        cols, pars = res
        g = call_bwd(list(cols), list(pars), ct)
        return tuple(g[:ncol]), tuple(g[ncol:])

    op.defvjp(fwd, bwd)
    return op(tuple(cols), tuple(pars))


def _sigmoid(x):
    return 1.0 / (1.0 + jnp.exp(-x))


def _silu(x):
    return x * _sigmoid(x)


def _rms(x, g):
    return x * lax.rsqrt(jnp.mean(x * x, axis=-1, keepdims=True) + EPS) * g


def _shift_rows(u, k, n_ctx):
    @jax.custom_vjp
    def op(u):
        t = u.shape[0]
        row = lax.broadcasted_iota(jnp.int32, u.shape, 0)
        edge = ((row == 0) | (row == n_ctx)) if k == 1 else ((row == n_ctx - 1) | (row == t - 1))
        return jnp.where(edge, 0.0, pltpu.roll(u, k % t, 0))

    op.defvjp(lambda u: (op(u), None), lambda _, g: (_shift_rows(g, -k, n_ctx),))
    return op(u)


def _dwconv(u, w0, w1, w2, b, n_ctx):
    return w0 * _shift_rows(u, 1, n_ctx) + w1 * u + w2 * _shift_rows(u, -1, n_ctx) + b


@jax.custom_vjp
def _swap_pairs(x):
    w = x.shape[1]
    lane = lax.broadcasted_iota(jnp.int32, x.shape, 1)
    return jnp.where(lane % 2 == 0, pltpu.roll(x, w - 1, 1), pltpu.roll(x, 1, 1))


_swap_pairs.defvjp(lambda x: (_swap_pairs(x), None), lambda _, g: (_swap_pairs(g),))


def _head_rms(x, g):
    w = x.shape[1]
    same = (lax.broadcasted_iota(jnp.int32, (w, w), 0) // HEAD_DIM) == (lax.broadcasted_iota(jnp.int32, (w, w), 1) // HEAD_DIM)
    ms = jnp.dot(x * x, same.astype(f32), precision=lax.Precision.HIGHEST, preferred_element_type=f32) * (1.0 / HEAD_DIM)
    return x * lax.rsqrt(ms + EPS) * g


def _band_ok(i, j, c0, shape, tq, tk):
    kpos = j * tk + lax.broadcasted_iota(jnp.int32, shape, 0)
    qpos = i * tq + (c0 + lax.broadcasted_iota(jnp.int32, shape, 1)) % tq
    return jnp.abs(qpos - kpos) <= WINDOW


def _kv_range(i, nb, window):
    is_ctx = i == 0
    if window:
        return jnp.where(is_ctx, 1, jnp.maximum(i - 1, 1)), jnp.where(is_ctx, 1, jnp.minimum(i + 2, nb))
    return 1, jnp.where(is_ctx, 1, nb)


def _sink_row(sink_ref, g, r, tq):
    return jnp.concatenate([jnp.full((1, tq), sink_ref[g * r + h], f32) for h in range(r)], axis=1)


def _attn_fwd_call(q, k, v, sink, window, name, shards=()):
    h, t, dh = q.shape
    nkv = k.shape[0]
    r = h // nkv
    tq = tk = ROW_TILE
    nb = t // tq
    rows = r * tq
    ns = len(shards)

    assert window or nb % 2 == 1, "the dense schedule takes the kv chunks after the context chunk in pairs"

    def body(sink_ref, q_ref, k_ref, v_ref, *rest):
        x_refs, (o_ref, lse_ref), gathered_refs = rest[:ns], rest[ns:ns + 2], rest[ns + 2:2 * ns + 2]
        m_scr, l_scr, acc_scr, s_a, s_b, p_a, p_b, a_a, a_b = rest[2 * ns + 2:2 * ns + 11]
        comm_sems = rest[2 * ns + 11:]
        g, i = pl.program_id(0), pl.program_id(1)
        if ns:
            @pl.when((g == 0) & (i == 0))
            def _():
                _gather_steps(x_refs, gathered_refs, *comm_sems)[0]()

        qv = q_ref[...].reshape(rows, dh)
        m_scr[...] = jnp.full_like(m_scr, NEG)
        l_scr[...] = jnp.zeros_like(l_scr)
        acc_scr[...] = jnp.zeros_like(acc_scr)

        def kv_rows(j):
            return pl.ds(pl.multiple_of(jnp.minimum(j, nb - 1) * tk, tk), tk)

        def scores(j, s_scr):
            s_scr[...] = lax.dot_general(k_ref[0, kv_rows(j), :], qv, NT, preferred_element_type=f32)

        def softmax(j, s_scr, p_scr, a_scr, masked):
            for cb in range(rows // ATTN_SLAB):
                cs = slice(cb * ATTN_SLAB, (cb + 1) * ATTN_SLAB)
                s = s_scr[:, cs]
                if masked:
                    s = jnp.where(_band_ok(i, j, cb * ATTN_SLAB, s.shape, tq, tk), s, NEG)
                m = m_scr[:, cs]
                m2 = jnp.maximum(m, jnp.max(s, axis=0, keepdims=True))
                p = jnp.exp(s - m2)
                a = jnp.exp(m - m2)
                l_scr[:, cs] = a * l_scr[:, cs] + jnp.sum(p, axis=0, keepdims=True)
                m_scr[:, cs] = m2
                a_scr[:, cs] = a
                p_scr[:, cs] = p.astype(bf16)

        def weighted_v(j, p_scr, a_scr):
            acc_scr[...] = a_scr[...] * acc_scr[...] + lax.dot_general(v_ref[0, kv_rows(j), :], p_scr[...], TN, preferred_element_type=f32)

        scores(0, s_a)
        softmax(0, s_a, p_a, a_a, False)
        if window:
            weighted_v(0, p_a, a_a)
            lo, hi = _kv_range(i, nb, window)

            def chunk(j, c):
                scores(j, s_a)
                softmax(j, s_a, p_a, a_a, True)
                weighted_v(j, p_a, a_a)
                return c

            lax.fori_loop(lo, hi, chunk, 0)
        else:
            scores(1, s_b)

            def pair(tt, c):
                j0 = 2 * tt + 1
                scores(j0 + 1, s_a)
                weighted_v(j0 - 1, p_a, a_a)
                softmax(j0, s_b, p_b, a_b, False)
                scores(j0 + 2, s_b)
                weighted_v(j0, p_b, a_b)
                softmax(j0 + 1, s_a, p_a, a_a, False)
                return c

            lax.fori_loop(0, jnp.where(i == 0, 0, (nb - 1) // 2), pair, 0)
            weighted_v(jnp.where(i == 0, 0, nb - 1), p_a, a_a)
        m, l, acc = m_scr[...], l_scr[...], acc_scr[...]
        if window:
            sk = _sink_row(sink_ref, g, r, tq)
            m2 = jnp.maximum(m, sk)
            a = jnp.exp(m - m2)
            l = a * l + jnp.exp(sk - m2)
            acc = a * acc
            m = m2
        o_ref[...] = (acc / l).T.reshape(r, tq, dh).astype(o_ref.dtype)
        lse_ref[0] = m + jnp.log(l)
        if ns:
            @pl.when((g == nkv - 1) & (i == nb - 1))
            def _():
                _gather_steps(x_refs, gathered_refs, *comm_sems)[1]()

    qspec = pl.BlockSpec((r, tq, dh), lambda g, i: (g, i, 0))
    kspec = pl.BlockSpec((1, t, dh), lambda g, i: (g, 0, 0))
    hbm = pl.BlockSpec(memory_space=pl.ANY)
    sem = ("arbitrary", "arbitrary") if ns else ("parallel", "parallel")
    return pl.pallas_call(
        body, grid=(nkv, nb),
        in_specs=[pl.BlockSpec(memory_space=pltpu.SMEM), qspec, kspec, kspec] + [hbm] * ns,
        out_specs=[qspec, pl.BlockSpec((1, 1, rows), lambda g, i: (g * nb + i, 0, 0))] + [hbm] * ns,
        out_shape=[jax.ShapeDtypeStruct((h, t, dh), bf16), jax.ShapeDtypeStruct((nkv * nb, 1, rows), f32)]
        + [jax.ShapeDtypeStruct((8,) + s.shape, s.dtype) for s in shards],
        scratch_shapes=[pltpu.VMEM((1, rows), f32), pltpu.VMEM((1, rows), f32), pltpu.VMEM((dh, rows), f32),
                        pltpu.VMEM((tk, rows), f32), pltpu.VMEM((tk, rows), f32), pltpu.VMEM((tk, rows), bf16),
                        pltpu.VMEM((tk, rows), bf16), pltpu.VMEM((1, rows), f32), pltpu.VMEM((1, rows), f32)]
        + (_gather_scratch(ns) if ns else []),
        name=name, compiler_params=_cparams(*sem))(sink, q, k, v, *shards)


def _attn_bwd_call(q, k, v, sink, o, lse, do, window, name, side_sums=()):
    h, t, dh = q.shape
    nkv = k.shape[0]
    r = h // nkv
    tq = tk = ROW_TILE
    nb = t // tq
    rows = r * tq
    ns = len(side_sums)

    def body(sink_ref, q_ref, k_ref, v_ref, o_ref, lse_ref, do_ref, *rest):
        ss_refs, (dq_ref, dk_ref, dv_ref, dsink_ref), got_refs = rest[:ns], rest[ns:ns + 4], rest[ns + 4:2 * ns + 4]
        s_a, s_b, dp_a, dp_b, p_a, p_b, ds_a, ds_b, dq_scr = rest[2 * ns + 4:2 * ns + 13]
        comm_sems = rest[2 * ns + 13:]
        g, i = pl.program_id(0), pl.program_id(1)
        if ns:
            @pl.when((g == 0) & (i == 0))
            def _():
                for cp in _chips_copies(ss_refs, got_refs, *comm_sems):
                    cp.start()

        @pl.when(i == 0)
        def _():
            dk_ref[...] = jnp.zeros_like(dk_ref)
            dv_ref[...] = jnp.zeros_like(dv_ref)

        qv = q_ref[...].reshape(rows, dh)
        dov = do_ref[...].reshape(rows, dh)
        lse_t = lse_ref[0]
        delta_t = jnp.sum((dov.astype(f32) * o_ref[...].reshape(rows, dh).astype(f32)).T, axis=0, keepdims=True)
        dq_scr[...] = jnp.zeros_like(dq_scr)
        q_t, do_t = qv.T, dov.T

        def kv_rows(j):
            return pl.ds(pl.multiple_of(jnp.minimum(j, nb - 1) * tk, tk), tk)

        def scores(j, s_scr, dp_scr):
            s_scr[...] = lax.dot_general(k_ref[0, kv_rows(j), :], qv, NT, preferred_element_type=f32)
            dp_scr[...] = lax.dot_general(v_ref[0, kv_rows(j), :], dov, NT, preferred_element_type=f32)

        def probs(j, s_scr, dp_scr, p_scr, ds_scr, masked):
            for cb in range(rows // ATTN_SLAB):
                cs = slice(cb * ATTN_SLAB, (cb + 1) * ATTN_SLAB)
                s = s_scr[:, cs]
                if masked:
                    s = jnp.where(_band_ok(i, j, cb * ATTN_SLAB, s.shape, tq, tk), s, NEG)
                p = jnp.exp(s - lse_t[:, cs])
                p_scr[:, cs] = p.astype(bf16)
                ds_scr[:, cs] = (p * (dp_scr[:, cs] - delta_t[:, cs])).astype(bf16)

        def grads(j, p_scr, ds_scr):
            dv_ref[0, :, kv_rows(j)] += lax.dot_general(do_t, p_scr[...], NT, preferred_element_type=f32)
            dk_ref[0, :, kv_rows(j)] += lax.dot_general(q_t, ds_scr[...], NT, preferred_element_type=f32)
            dq_scr[...] += lax.dot_general(k_ref[0, kv_rows(j), :], ds_scr[...], TN, preferred_element_type=f32)

        scores(0, s_a, dp_a)
        probs(0, s_a, dp_a, p_a, ds_a, False)
        if window:
            grads(0, p_a, ds_a)
            lo, hi = _kv_range(i, nb, window)

            def chunk(j, c):
                scores(j, s_a, dp_a)
                probs(j, s_a, dp_a, p_a, ds_a, True)
                grads(j, p_a, ds_a)
                return c

            lax.fori_loop(lo, hi, chunk, 0)
        else:
            scores(1, s_b, dp_b)

            def pair(tt, c):
                j0 = 2 * tt + 1
                scores(j0 + 1, s_a, dp_a)
                grads(j0 - 1, p_a, ds_a)
                probs(j0, s_b, dp_b, p_b, ds_b, False)
                scores(j0 + 2, s_b, dp_b)
                grads(j0, p_b, ds_b)
                probs(j0 + 1, s_a, dp_a, p_a, ds_a, False)
                return c

            lax.fori_loop(0, jnp.where(i == 0, 0, (nb - 1) // 2), pair, 0)
            grads(jnp.where(i == 0, 0, nb - 1), p_a, ds_a)
        dq_ref[...] = dq_scr[...].T.reshape(r, tq, dh).astype(dq_ref.dtype)
        if window:
            dsink_ref[0] = -jnp.exp(_sink_row(sink_ref, g, r, tq) - lse_t) * delta_t
        else:
            dsink_ref[...] = jnp.zeros_like(dsink_ref)
        if ns:
            @pl.when((g == nkv - 1) & (i == nb - 1))
            def _():
                for cp in _chips_copies(ss_refs, got_refs, *comm_sems):
                    cp.wait()

    qspec = pl.BlockSpec((r, tq, dh), lambda g, i: (g, i, 0))
    cspec = pl.BlockSpec((1, 1, rows), lambda g, i: (g * nb + i, 0, 0))
    kspec = pl.BlockSpec((1, t, dh), lambda g, i: (g, 0, 0))
    ktspec = pl.BlockSpec((1, dh, t), lambda g, i: (g, 0, 0))
    hbm = pl.BlockSpec(memory_space=pl.ANY)
    return pl.pallas_call(
        body, grid=(nkv, nb),
        in_specs=[pl.BlockSpec(memory_space=pltpu.SMEM), qspec, kspec, kspec, qspec, cspec, qspec] + [hbm] * ns,
        out_specs=[qspec, ktspec, ktspec, cspec] + [hbm] * ns,
        out_shape=[jax.ShapeDtypeStruct((h, t, dh), bf16), jax.ShapeDtypeStruct((nkv, dh, t), f32), jax.ShapeDtypeStruct((nkv, dh, t), f32),
                   jax.ShapeDtypeStruct((nkv * nb, 1, rows), f32)] + [jax.ShapeDtypeStruct((3,) + s.shape[1:], s.dtype) for s in side_sums],
        scratch_shapes=[pltpu.VMEM((tk, rows), f32)] * 4 + [pltpu.VMEM((tk, rows), bf16)] * 4 + [pltpu.VMEM((dh, rows), f32)]
        + ([pltpu.SemaphoreType.DMA((3 * ns,)), pltpu.SemaphoreType.DMA((3 * ns,))] if ns else []),
        name=name, compiler_params=_cparams("arbitrary" if ns else "parallel", "arbitrary"))(sink, q, k, v, o, lse, do, *side_sums)


def attention(q, k, v, sink, window, name, shards=(), stand_ins=()):
    @jax.custom_vjp
    def op(q, k, v, sink, shards, stand_ins):
        o, _, *gathered = _attn_fwd_call(q, k, v, sink, window, name, shards)
        return o, tuple(gathered)

    def fwd(q, k, v, sink, shards, stand_ins):
        o, lse, *gathered = _attn_fwd_call(q, k, v, sink, window, name, shards)
        return (o, tuple(gathered)), (q, k, v, sink, o, lse, shards)

    def bwd(res, cts):
        q, k, v, sink, o, lse, shards = res
        do, d_gathered = cts
        side_sums = []
        if shards:
            my_c = lax.axis_index("c").reshape(1).astype(jnp.int32)
            from_sibling = rs_to_sibling(list(d_gathered), name + "_rs_sibling")
            side_sums = [pair_sum(s, rr, my_c, f"{name}_pair_sum{a}") for a, (s, rr) in enumerate(zip(d_gathered, from_sibling))]
        dq, dk, dv, dsink_rows, *from_chips = _attn_bwd_call(q, k, v, sink, o, lse, do, window, name + "_bwd", side_sums)
        nkv, r = k.shape[0], q.shape[0] // k.shape[0]
        dsink = jnp.sum(dsink_rows.reshape(nkv, -1, r, ROW_TILE), axis=(1, 3)).reshape(nkv * r)
        reduced = tuple(jnp.concatenate([s, fc, jnp.zeros_like(s[:1])], axis=0) for s, fc in zip(side_sums, from_chips))
        dk, dv = dk.transpose(0, 2, 1).astype(k.dtype), dv.transpose(0, 2, 1).astype(v.dtype)
        return dq, dk, dv, dsink, tuple(jnp.zeros_like(s) for s in shards), reduced

    op.defvjp(fwd, bwd)
    return op(q, k, v, sink, tuple(shards), tuple(stand_ins))


def _ssd_chunk(xs, dtx, dtr, ac, bs, cs, hin, rev):
    q = xs[0].shape[0]
    ii = lax.broadcasted_iota(jnp.int32, (q, q), 0)
    jj = lax.broadcasted_iota(jnp.int32, (q, q), 1)
    tri = (ii <= jj) if rev else (ii >= jj)
    lo = lax.broadcasted_iota(jnp.int32, (q, LANES), 1) < SSM_P
    lo_row = lax.broadcasted_iota(jnp.int32, (1, LANES), 1) < SSM_P
    heads, slabs, per_group = range(SSM_HEADS), range(SSM_HEADS // 2), SSM_HEADS // 2 // SSM_G

    a = [dtr[h] * ac[h] for h in heads]
    c = [jnp.sum(jnp.where(tri, jnp.broadcast_to(a[h], (q, q)), 0.0), axis=1, keepdims=True) for h in heads]
    tot = [jnp.sum(a[h], axis=1, keepdims=True) for h in heads]
    cf = [jnp.broadcast_to(c[h], (q, q)) for h in heads]
    seg = [jnp.minimum(cf[h] - cf[h].T, 0.0) for h in heads]
    decay = [jnp.where(tri, jnp.exp(seg[h]), 0.0) for h in heads]
    cb = [lax.dot_general(cs[g].astype(bf16), bs[g].astype(bf16), NT, preferred_element_type=f32) for g in range(SSM_G)]
    m = [jnp.concatenate([cb[j // per_group] * decay[2 * j], cb[j // per_group] * decay[2 * j + 1]], axis=1).astype(bf16) for j in slabs]
    xdt = [xs[j] * dtx[j] for j in slabs]
    x2 = [jnp.concatenate([jnp.where(lo, xdt[j], 0.0), jnp.where(lo, 0.0, xdt[j])], axis=0).astype(bf16) for j in slabs]
    y_diag = [jnp.dot(m[j], x2[j], preferred_element_type=f32) for j in slabs]
    csel = [jnp.where(lo, cf[2 * j], cf[2 * j + 1]) for j in slabs]
    tsel = [jnp.where(lo_row, jnp.broadcast_to(tot[2 * j], (1, LANES)), jnp.broadcast_to(tot[2 * j + 1], (1, LANES))) for j in slabs]
    xend = [(xdt[j] * jnp.exp(tsel[j] - csel[j])).astype(bf16) for j in slabs]
    st = [lax.dot_general(bs[j // per_group].astype(bf16), xend[j], TN, preferred_element_type=f32) for j in slabs]
    y_off = [jnp.dot(cs[j // per_group].astype(bf16), hin[j].astype(bf16), preferred_element_type=f32) * jnp.exp(csel[j]) for j in slabs]
    return [y_diag[j] + y_off[j] for j in slabs], [hin[j] * jnp.exp(tsel[j]) + st[j] for j in slabs]


def _ssd_order(s, nc, ncc, rev):
    if not rev:
        return s
    return jnp.where(s < ncc, ncc - 1 - s, nc - 1 - (s - ncc))


SSD_SLABS = [slice(LANES * j, LANES * (j + 1)) for j in range(SSM_HEADS // 2)]
SSD_GROUPS = [slice(SSM_N * g, SSM_N * (g + 1)) for g in range(SSM_G)]


def _head_lanes(w, transpose=False):
    shape = (w, SSM_HEADS) if transpose else (SSM_HEADS, w)
    head = lax.broadcasted_iota(jnp.int32, shape, 1 if transpose else 0)
    lane = lax.broadcasted_iota(jnp.int32, shape, 0 if transpose else 1)
    return (lane // SSM_P == head).astype(f32)


def _ssd_fwd_call(xs, dt, dtr, bm, cm, acol, rev, n_ctx, name):
    t, w = xs.shape
    q = SSM_Q
    nc, ncc = t // q, n_ctx // q

    def body(xs_ref, dt_ref, dtr_ref, b_ref, c_ref, a_ref, y_ref, hin_ref, h_scr):
        @pl.when(pl.program_id(0) == 0)
        def _():
            h_scr[...] = jnp.zeros_like(h_scr)

        hin_ref[0] = h_scr[...]
        dtx = jnp.dot(dt_ref[...], _head_lanes(w), precision=lax.Precision.HIGHEST, preferred_element_type=f32)
        ys, houts = _ssd_chunk([xs_ref[:, sl] for sl in SSD_SLABS], [dtx[:, sl] for sl in SSD_SLABS],
                               [dtr_ref[h:h + 1, :] for h in range(SSM_HEADS)], [a_ref[h:h + 1, :] for h in range(SSM_HEADS)],
                               [b_ref[:, gs] for gs in SSD_GROUPS], [c_ref[:, gs] for gs in SSD_GROUPS],
                               [h_scr[:, sl] for sl in SSD_SLABS], rev)
        for sl, y, hout in zip(SSD_SLABS, ys, houts):
            y_ref[:, sl] = y.astype(y_ref.dtype)
            h_scr[:, sl] = hout

    def at(s):
        return _ssd_order(s, nc, ncc, rev)

    return pl.pallas_call(
        body, grid=(nc,),
        in_specs=[pl.BlockSpec((q, w), lambda s: (at(s), 0)), pl.BlockSpec((q, SSM_HEADS), lambda s: (at(s), 0)),
                  pl.BlockSpec((SSM_HEADS, q), lambda s: (0, at(s))),
                  pl.BlockSpec((q, SSM_BC), lambda s: (at(s), 0)), pl.BlockSpec((q, SSM_BC), lambda s: (at(s), 0)),
                  pl.BlockSpec((SSM_HEADS, 1), lambda s: (0, 0))],
        out_specs=[pl.BlockSpec((q, w), lambda s: (at(s), 0)), pl.BlockSpec((1, SSM_N, w), lambda s: (s, 0, 0))],
        out_shape=[jax.ShapeDtypeStruct((t, w), xs.dtype), jax.ShapeDtypeStruct((nc, SSM_N, w), f32)],
        scratch_shapes=[pltpu.VMEM((SSM_N, w), f32)],
        name=name, compiler_params=_cparams("arbitrary"))(xs, dt, dtr, bm, cm, acol)


def _ssd_bwd_call(xs, dt, dtr, bm, cm, acol, hin, dy, rev, n_ctx, name):
    t, w = xs.shape
    q = SSM_Q
    nc, ncc = t // q, n_ctx // q

    def body(xs_ref, dt_ref, dtr_ref, b_ref, c_ref, a_ref, hin_ref, dy_ref,
             dxs_ref, ddt_ref, ddtr_ref, db_ref, dc_ref, da_ref, dh_scr):
        @pl.when(pl.program_id(0) == 0)
        def _():
            dh_scr[...] = jnp.zeros_like(dh_scr)
            da_ref[...] = jnp.zeros_like(da_ref)

        dtx = jnp.dot(dt_ref[...], _head_lanes(w), precision=lax.Precision.HIGHEST, preferred_element_type=f32)
        _, vjp = jax.vjp(
            functools.partial(_ssd_chunk, rev=rev),
            [xs_ref[:, sl].astype(f32) for sl in SSD_SLABS], [dtx[:, sl] for sl in SSD_SLABS],
            [dtr_ref[h:h + 1, :] for h in range(SSM_HEADS)], [a_ref[h:h + 1, :] for h in range(SSM_HEADS)],
            [b_ref[:, gs].astype(f32) for gs in SSD_GROUPS], [c_ref[:, gs].astype(f32) for gs in SSD_GROUPS],
            [hin_ref[0, :, sl] for sl in SSD_SLABS])
        dxs, ddtx, ddtr, dac, dbs, dcs, dhin = vjp(([dy_ref[:, sl].astype(f32) for sl in SSD_SLABS], [dh_scr[:, sl] for sl in SSD_SLABS]))
        for j, sl in enumerate(SSD_SLABS):
            dxs_ref[:, sl] = dxs[j].astype(dxs_ref.dtype)
            dh_scr[:, sl] = dhin[j]
        for h in range(SSM_HEADS):
            ddtr_ref[h:h + 1, :] = ddtr[h]
            da_ref[h:h + 1, :] += dac[h]
        for g, gs in enumerate(SSD_GROUPS):
            db_ref[:, gs] = dbs[g].astype(db_ref.dtype)
            dc_ref[:, gs] = dcs[g].astype(dc_ref.dtype)
        ddt_ref[...] = jnp.dot(jnp.concatenate(ddtx, axis=1), _head_lanes(w, transpose=True),
                               precision=lax.Precision.HIGHEST, preferred_element_type=f32)

    def step(s):
        return nc - 1 - s

    def at(s):
        return _ssd_order(step(s), nc, ncc, rev)

    row = lambda wd: pl.BlockSpec((q, wd), lambda s: (at(s), 0))
    dtr_spec = pl.BlockSpec((SSM_HEADS, q), lambda s: (0, at(s)))
    a_spec = pl.BlockSpec((SSM_HEADS, 1), lambda s: (0, 0))
    return pl.pallas_call(
        body, grid=(nc,),
        in_specs=[row(w), row(SSM_HEADS), dtr_spec, row(SSM_BC), row(SSM_BC), a_spec,
                  pl.BlockSpec((1, SSM_N, w), lambda s: (step(s), 0, 0)), row(w)],
        out_specs=[row(w), row(SSM_HEADS), dtr_spec, row(SSM_BC), row(SSM_BC), a_spec],
        out_shape=[jax.ShapeDtypeStruct((t, w), xs.dtype), jax.ShapeDtypeStruct(dt.shape, f32), jax.ShapeDtypeStruct(dtr.shape, f32),
                   jax.ShapeDtypeStruct(bm.shape, bm.dtype), jax.ShapeDtypeStruct(cm.shape, cm.dtype), jax.ShapeDtypeStruct(acol.shape, f32)],
        scratch_shapes=[pltpu.VMEM((SSM_N, w), f32)],
        name=name, compiler_params=_cparams("arbitrary"))(xs, dt, dtr, bm, cm, acol, hin, dy)


def ssd_scan(xs, dt, dtr, bm, cm, acol, rev, n_ctx, name):
    @jax.custom_vjp
    def op(xs, dt, dtr, bm, cm, acol):
        return _ssd_fwd_call(xs, dt, dtr, bm, cm, acol, rev, n_ctx, name)[0]

    def fwd(xs, dt, dtr, bm, cm, acol):
        y, hin = _ssd_fwd_call(xs, dt, dtr, bm, cm, acol, rev, n_ctx, name)
        return y, (xs, dt, dtr, bm, cm, acol, hin)

    def bwd(res, dy):
        return tuple(_ssd_bwd_call(*res, dy, rev, n_ctx, name + "_bwd"))

    op.defvjp(fwd, bwd)
    return op(xs, dt, dtr, bm, cm, acol)


def _in_layout(d):
    return [('a_q', Q_W), ('a_k', KV_W), ('a_v', KV_W), ('b_z', SSM_INNER), ('b_xbc', SSM_INNER + 2 * SSM_BC), ('b_dt', DT_W),
            ('c_q', Q_W), ('c_k', KV_W), ('c_v', KV_W), ('g_a', d), ('g_b', d), ('g_c', d)]


def _dt_span(d):
    start = 0
    for name, n in _in_layout(d):
        if name == 'b_dt':
            return start, start + n
        start += n


@jax.custom_vjp
def _w_in_split(w):
    lo, hi = _dt_span(w.shape[0])
    dt = jnp.concatenate([w[:, lo:hi], jnp.zeros((w.shape[0], DT_PAD - (hi - lo)), w.dtype)], axis=1)
    return jnp.concatenate([w[:, :lo], w[:, hi:]], axis=1), dt


def _w_in_join(g_main, g_dt):
    lo, hi = _dt_span(g_main.shape[0])
    return jnp.concatenate([g_main[:, :lo], g_dt[:, :hi - lo], g_main[:, lo:]], axis=1)


_w_in_split.defvjp(lambda w: (_w_in_split(w), None), lambda _, g: (_w_in_join(*g),))


def _rope_tables(n_ctx, n_lat):
    rows = n_lat // GRID_W
    t_row = jnp.repeat(jnp.arange(rows), GRID_W).astype(f32)
    t_col = jnp.tile(jnp.arange(GRID_W), rows).astype(f32)
    n = HEAD_DIM // 4
    inv = ROPE_BASE ** (-jnp.arange(n, dtype=f32) / n)
    ang = jnp.concatenate([t_row[:, None] * inv, t_col[:, None] * inv], axis=-1)
    cos = jnp.concatenate([jnp.ones((n_ctx, HEAD_DIM // 2), f32), jnp.cos(ang)], axis=0)
    sin = jnp.concatenate([jnp.zeros((n_ctx, HEAD_DIM // 2), f32), jnp.sin(ang)], axis=0)
    return jnp.repeat(cos, 2, axis=1), jnp.stack([-sin, sin], axis=-1).reshape(sin.shape[0], HEAD_DIM)


def _heads_major(a, n_heads):
    return a.reshape(a.shape[0], n_heads, HEAD_DIM).transpose(1, 0, 2)


def _heads_minor(a):
    return a.transpose(1, 0, 2).reshape(a.shape[1], a.shape[0] * HEAD_DIM)


def _layer(xall, w, s, cm, tabs, n_ctx, li, gather=((), ())):
    t, d = xall.shape
    ncb = n_ctx // ROW_TILE
    nm = f"l{li}_"
    ctq, stq, ctk, stk = tabs
    def mod(blk, cmv, i):
        return jnp.where(blk < ncb, cmv[0:1, i * d:(i + 1) * d], cmv[1:2, i * d:(i + 1) * d])

    def norm_mod(blk, x, g, cmv):
        return (_rms(x, g) * (1.0 + mod(blk, cmv, 1)) + mod(blk, cmv, 0),)

    (h,) = rowwise(norm_mod, [xall], [], [s['norm1'][None], cm], [d], [bf16], nm + "norm1")
    w_main, w_dt = _w_in_split(w['w_in'])
    u = mm(h, w_main, nm + "in")
    b_dt = mm(h, w_dt, nm + "in_dt", f32)
    a_q, a_k, a_v, b_z, b_xbc, c_q, c_k, c_v, g_a, g_b, g_c = split_cols(u, [n for name, n in _in_layout(d) if name != 'b_dt'])

    def rope(blk, q, k, v, ct_q, st_q, ct_k, st_k):
        return q * ct_q + _swap_pairs(q) * st_q, k * ct_k + _swap_pairs(k) * st_k, v

    def norm_rope(blk, q, k, v, ct_q, st_q, ct_k, st_k, gq, gk):
        return rope(blk, _head_rms(q, gq), _head_rms(k, gk), v, ct_q, st_q, ct_k, st_k)

    qkv_w, qkv_t = [Q_W, KV_W, KV_W], [bf16, bf16, bf16]
    qa, ka, va = rowwise(rope, [a_q, a_k, a_v], [ctq, stq, ctk, stk], [], qkv_w, qkv_t, nm + "ropeA")
    gq = jnp.tile(s['c_q_norm'], N_HEADS)[None]
    gk = jnp.tile(s['c_k_norm'], N_KV)[None]
    qc, kc, vc = rowwise(norm_rope, [c_q, c_k, c_v], [ctq, stq, ctk, stk], [gq, gk], qkv_w, qkv_t, nm + "ropeC")
    ya = _heads_minor(attention(_heads_major(qa, N_HEADS), _heads_major(ka, N_KV), _heads_major(va, N_KV),
                                s['a_sink'], True, nm + "attnA")[0])
    yc, gathered = attention(_heads_major(qc, N_HEADS), _heads_major(kc, N_KV), _heads_major(vc, N_KV),
                             jnp.zeros((N_HEADS,), f32), False, nm + "attnC", *gather)
    yc = _heads_minor(yc)

    cw, cb = s['ssm_conv_w'], s['ssm_conv_b']
    conv_silu = lambda uu, w3, b: _silu(_dwconv(uu, w3[0:1], w3[1:2], w3[2:3], b, n_ctx))
    xbc = colwise(conv_silu, [b_xbc], [cw, cb[None]], bf16, nm + "ssmconv")
    xs, bm, cmat = split_cols(xbc, [SSM_INNER, SSM_BC, SSM_BC])
    bias = jnp.concatenate([s['ssm_dt_bias'].reshape(1, DT_W), jnp.zeros((1, DT_PAD - DT_W), f32)], axis=1)

    def softplus(blk, r, b):
        z = r + b
        return (jnp.maximum(z, 0.0) + jnp.log(1.0 + jnp.exp(-jnp.abs(z))),)

    (dt_all,) = rowwise(softplus, [b_dt], [], [bias], [DT_PAD], [f32], nm + "dt")
    a_coef = -jnp.exp(s['ssm_A_log'])
    ys_dir = []
    for di, rev in enumerate((False, True)):
        dt = dt_all[:, di * SSM_HEADS:(di + 1) * SSM_HEADS]
        ys_dir.append(ssd_scan(xs, dt, dt.T, bm, cmat, a_coef[di][:, None], rev, n_ctx,
                               nm + ("ssd_r" if rev else "ssd_f")))

    def ssm_out(blk, yf, yb, x, z, dskip, g):
        return (_rms((yf + yb + x * dskip) * _silu(z), g),)

    (ysn,) = rowwise(ssm_out, [ys_dir[0], ys_dir[1], xs, b_z], [], [jnp.repeat(s['ssm_D'], SSM_P)[None], s['ssm_norm'][None]],
                     [SSM_INNER], [bf16], nm + "ssmout")

    pa, pb, pc = mm(ya, w['w_oa'], nm + "oa"), mm(ysn, w['w_ob'], nm + "ob"), mm(yc, w['w_oc'], nm + "oc")

    def merge(blk, ga, gb, gc, a, b, c):
        return (_sigmoid(ga) * a + _sigmoid(gb) * b + _sigmoid(gc) * c,)

    (mrg,) = rowwise(merge, [g_a, g_b, g_c, pa, pb, pc], [], [], [d], [bf16], nm + "merge")
    o = mm(mrg, w['w_out'], nm + "out")

    def resid_norm_mod(blk, x, oo, g, cmv):
        x1 = x + mod(blk, cmv, 2) * oo
        return x1, _rms(x1, g) * (1.0 + mod(blk, cmv, 4)) + mod(blk, cmv, 3)

    x1, h2 = rowwise(resid_norm_mod, [xall, o], [], [s['norm2'][None], cm], [d, d], [f32, bf16], nm + "norm2")
    up, gt = mm(h2, w['ffn_w_up'], nm + "up"), mm(h2, w['ffn_w_gate'], nm + "gate")
    fw, fb = s['ffn_conv_w'], s['ffn_conv_b']
    ffn_act = lambda g_, u_, w3, b: _silu(_dwconv(g_, w3[0:1], w3[1:2], w3[2:3], b, n_ctx)) * u_
    act = colwise(ffn_act, [gt, up], [fw, fb[None]], bf16, nm + "ffnact")
    f = mm(act, w['ffn_w_down'], nm + "down")

    def resid(blk, x, ff, cmv):
        return (x + mod(blk, cmv, 5) * ff,)

    (x2,) = rowwise(resid, [x1, f], [], [cm], [d], [f32], nm + "resid")
    return x2, gathered


def _assemble(name, gathered):
    if BIG[name] == 0:
        return gathered.reshape(-1, gathered.shape[-1])
    return jnp.concatenate([gathered[j] for j in range(8)], axis=1)


def _loss_fn(big0, shards1, stand_ins1, small, x, ctx, c, target, n_ctx):
    n_lat, d = x.shape
    xall = jnp.concatenate([ctx, x], axis=0)
    ct, st = _rope_tables(n_ctx, n_lat)
    tabs = (jnp.tile(ct, (1, N_HEADS)) * HEAD_DIM ** -0.5, jnp.tile(st, (1, N_HEADS)) * HEAD_DIM ** -0.5,
            jnp.tile(ct, (1, N_KV)), jnp.tile(st, (1, N_KV)))
    srows = jnp.concatenate([_silu(small['c_ctx'])[None], _silu(c), jnp.zeros((14, d), f32)], axis=0)
    names = list(BIG)
    big, gather = big0, ([shards1[n] for n in names], [stand_ins1[n] for n in names])
    for li in range(2):
        cm = mm(srows, big['w_mod'], f"l{li}_mod", f32)[0:2] + small['b_mod'][li][None]
        sl = {k: v[li] for k, v in small.items() if k not in ('c_ctx', 'final_norm')}
        xall, gathered = _layer(xall, big, sl, cm, tabs, n_ctx, li, gather)
        if li == 0:
            big, gather = {n: _assemble(n, g) for n, g in zip(names, gathered)}, ((), ())
    ncb = n_ctx // ROW_TILE
    tgt = jnp.concatenate([jnp.zeros((n_ctx, d), f32), target], axis=0)

    def loss_rows(blk, xx, tg, g):
        e = _rms(xx, g) - tg
        return (jnp.where(blk < ncb, 0.0, 0.5) * jnp.mean(e * e, axis=-1, keepdims=True),)

    (rows,) = rowwise(loss_rows, [xall], [tgt], [small['final_norm'][None]], [1], [f32], "loss")
    return jnp.sum(rows)


def _hbm_call(body, ins, out_shapes, n_sems, name):
    any_spec = pl.BlockSpec(memory_space=pl.ANY)
    return pl.pallas_call(
        body, out_shape=out_shapes, in_specs=[any_spec] * len(ins), out_specs=[any_spec] * len(out_shapes),
        scratch_shapes=[pltpu.SemaphoreType.DMA((n_sems,)), pltpu.SemaphoreType.DMA((n_sems,)), pltpu.SemaphoreType.DMA((len(ins),))],
        name=name)(*ins)


def _gather_steps(x_refs, out_refs, send_sems, recv_sems, local_sems):
    n = len(x_refs)
    x, y, c = lax.axis_index("x"), lax.axis_index("y"), lax.axis_index("c")
    me, sibling = (x, y, c), (x, y, 1 - c)
    chips = [(1 - x, y), (x, 1 - y), (1 - x, 1 - y)]

    def copy(a, k, block, to, src=None):
        px, py, pc = block
        slot = out_refs[a].at[4 * px + 2 * py + pc]
        return pltpu.make_async_remote_copy(
            src_ref=slot if src is None else src, dst_ref=slot,
            send_sem=send_sems.at[7 * a + k], recv_sem=recv_sems.at[7 * a + k], device_id=to, device_id_type=MESH)

    mine = [pltpu.make_async_copy(x_refs[a], out_refs[a].at[4 * x + 2 * y + c], local_sems.at[a]) for a in range(n)]
    first = []
    for a in range(n):
        first += [copy(a, 1 + j, me, (*chip, c), src=x_refs[a]) for j, chip in enumerate(chips)]
        first.append(copy(a, 0, me, sibling, src=x_refs[a]))

    def start():
        for cp in mine + first:
            cp.start()

    def finish():
        passed = []
        for a in range(n):
            for j, chip in enumerate(chips):
                copy(a, 1 + j, (*chip, c), me).wait_recv()
                passed.append(copy(a, 4 + j, (*chip, c), sibling))
                passed[-1].start()
        for a in range(n):
            copy(a, 0, sibling, me).wait_recv()
            for j, chip in enumerate(chips):
                copy(a, 4 + j, (*chip, 1 - c), me).wait_recv()
        for cp in first + passed:
            cp.wait_send()
        for cp in mine:
            cp.wait()

    return start, finish


def _gather_scratch(n):
    return [pltpu.SemaphoreType.DMA((7 * n,)), pltpu.SemaphoreType.DMA((7 * n,)), pltpu.SemaphoreType.DMA((n,))]


def all_gather(shards, name):
    n = len(shards)

    def body(*refs):
        start, finish = _gather_steps(refs[:n], refs[n:2 * n], *refs[2 * n:])
        start()
        finish()

    return _hbm_call(body, shards, [jax.ShapeDtypeStruct((8,) + s.shape, s.dtype) for s in shards], 7 * n, name)


def rs_to_sibling(gs, name="rs_sibling"):
    n = len(gs)

    def body(*refs):
        g_refs, out_refs, (send_sems, recv_sems, _) = refs[:n], refs[n:2 * n], refs[2 * n:]
        x, y, c = lax.axis_index("x"), lax.axis_index("y"), lax.axis_index("c")
        copies = [pltpu.make_async_remote_copy(
            src_ref=g_refs[a].at[2 * k + (1 - c)], dst_ref=out_refs[a].at[k], send_sem=send_sems.at[4 * a + k],
            recv_sem=recv_sems.at[4 * a + k], device_id=(x, y, 1 - c), device_id_type=MESH) for a in range(n) for k in range(4)]
        for cp in copies:
            cp.start()
        for cp in copies:
            cp.wait()

    return _hbm_call(body, gs, [jax.ShapeDtypeStruct((4,) + g.shape[1:], g.dtype) for g in gs], 4 * n, name)


def rs_to_chips(ss):
    n = len(ss)

    def body(*refs):
        copies = _chips_copies(refs[:n], refs[n:2 * n], refs[2 * n], refs[2 * n + 1])
        for cp in copies:
            cp.start()
        for cp in copies:
            cp.wait()

    return _hbm_call(body, ss, [jax.ShapeDtypeStruct((3,) + s.shape[1:], s.dtype) for s in ss], 3 * n, "rs_chips")


def _chips_copies(s_refs, out_refs, send_sems, recv_sems):
    x, y, c = lax.axis_index("x"), lax.axis_index("y"), lax.axis_index("c")
    copies = []
    for a in range(len(s_refs)):
        for k, (fx, fy) in enumerate([(1, 0), (0, 1), (1, 1)]):
            px, py = (1 - x) if fx else x, (1 - y) if fy else y
            copies.append(pltpu.make_async_remote_copy(
                src_ref=s_refs[a].at[2 * px + py], dst_ref=out_refs[a].at[k], send_sem=send_sems.at[3 * a + k],
                recv_sem=recv_sems.at[3 * a + k], device_id=(px, py, c), device_id_type=MESH))
    return copies


def _flat_tile(rows, cols):
    return _row_tile(rows, 4 * 4 * cols)


def pair_sum(g, r1, my_c, name):
    _, rows, cols = g.shape
    tm = _flat_tile(rows, cols)

    def body(c_ref, g_ref, r_ref, o_ref):
        o_ref[...] = (g_ref[...].astype(f32) + r_ref[...].astype(f32)).astype(o_ref.dtype)

    return pl.pallas_call(
        body, grid_spec=pltpu.PrefetchScalarGridSpec(
            num_scalar_prefetch=1, grid=(4, rows // tm),
            in_specs=[pl.BlockSpec((1, tm, cols), lambda k, i, c: (2 * k + c[0], i, 0)),
                      pl.BlockSpec((1, tm, cols), lambda k, i, c: (k, i, 0))],
            out_specs=pl.BlockSpec((1, tm, cols), lambda k, i, c: (k, i, 0))),
        out_shape=jax.ShapeDtypeStruct((4, rows, cols), g.dtype), name=name,
        compiler_params=_cparams("parallel", "parallel"))(my_c, g, r1)


def _adam_math(w, g, m, v):
    m2 = ADAM_B1 * m + (1.0 - ADAM_B1) * g
    v2 = ADAM_B2 * v + (1.0 - ADAM_B2) * (g * g)
    m_hat = m2 / (1.0 - ADAM_B1 ** ADAM_STEP)
    v_hat = v2 / (1.0 - ADAM_B2 ** ADAM_STEP)
    return -ADAM_LR * (m_hat / (jnp.sqrt(v_hat) + ADAM_EPS) + ADAM_WD * w), m2, v2


def sum_adam(parts, w, m, v, name):
    groups, rows, cols = w.shape
    tm = _flat_tile(rows, cols)
    nblk = rows // tm
    flat = []
    scalars = [p[2] for ps in parts for p in ps if p[2] is not None]
    for gi, ps in enumerate(parts):
        flat.append([])
        for arr, static_rows, dyn in ps:
            if dyn is not None:
                flat[gi].append((arr, functools.partial(lambda l, i, s, gi: (s[0], jnp.where(l == gi, i, nblk - 1), 0), gi=gi)))
            else:
                for k in static_rows:
                    flat[gi].append((arr, functools.partial(lambda l, i, s, gi, k: (k, jnp.where(l == gi, i, nblk - 1), 0), gi=gi, k=k)))
    counts = [len(f) for f in flat]
    na = sum(counts)

    def body(s_ref, *refs):
        sums, at = [], 0
        for cnt in counts:
            g = refs[at][0].astype(f32)
            for r in refs[at + 1:at + cnt]:
                g = g + r[0].astype(f32)
            sums.append(g)
            at += cnt
        g = sums[0]
        for gi in range(1, groups):
            g = jnp.where(pl.program_id(0) == gi, sums[gi], g)
        w_ref, m_ref, v_ref = refs[na:na + 3]
        g_out, d_out, m_out, v_out = refs[na + 3:]
        d, m2, v2 = _adam_math(w_ref[0], g, m_ref[0], v_ref[0])
        g_out[0] = g
        d_out[0] = d
        m_out[0] = m2
        v_out[0] = v2

    blk = pl.BlockSpec((1, tm, cols), lambda l, i, s: (l, i, 0))
    scalar = scalars[0] if scalars else jnp.zeros((1,), jnp.int32)
    return pl.pallas_call(
        body, grid_spec=pltpu.PrefetchScalarGridSpec(
            num_scalar_prefetch=1, grid=(groups, nblk),
            in_specs=[pl.BlockSpec((1, tm, cols), im) for f in flat for _, im in f] + [blk, blk, blk],
            out_specs=[blk, blk, blk, blk]),
        out_shape=[jax.ShapeDtypeStruct((groups, rows, cols), f32)] * 4, name=name,
        compiler_params=_cparams("arbitrary", "arbitrary"))(scalar, *[a for f in flat for a, _ in f], w, m, v)


FLAT_COLS = 1024


def _to_flat(vec):
    n = vec.shape[0]
    total = -(-n // (8 * FLAT_COLS)) * 8 * FLAT_COLS
    return jnp.concatenate([vec, jnp.zeros((total - n,), vec.dtype)]).reshape(-1, FLAT_COLS)


def _pack(tree, names):
    return jnp.concatenate([tree[n].reshape(-1) for n in names])


def _unpack(vec, like, names):
    out, off = {}, 0
    for n in names:
        size = like[n].size
        out[n] = vec[off:off + size].reshape(like[n].shape)
        off += size
    return out


def kernel(x, c, ctx, c_ctx, w_mod, b_mod, norm1, norm2, w_in, a_sink, ssm_conv_w, ssm_conv_b, ssm_A_log, ssm_dt_bias, ssm_D, ssm_norm, c_q_norm, c_k_norm, w_oa, w_ob, w_oc, w_out, ffn_w_up, ffn_w_gate, ffn_conv_w, ffn_conv_b, ffn_w_down, final_norm, loss_target, m_c_ctx, m_w_mod, m_b_mod, m_norm1, m_norm2, m_w_in, m_a_sink, m_ssm_conv_w, m_ssm_conv_b, m_ssm_A_log, m_ssm_dt_bias, m_ssm_D, m_ssm_norm, m_c_q_norm, m_c_k_norm, m_w_oa, m_w_ob, m_w_oc, m_w_out, m_ffn_w_up, m_ffn_w_gate, m_ffn_conv_w, m_ffn_conv_b, m_ffn_w_down, m_final_norm, v_c_ctx, v_w_mod, v_b_mod, v_norm1, v_norm2, v_w_in, v_a_sink, v_ssm_conv_w, v_ssm_conv_b, v_ssm_A_log, v_ssm_dt_bias, v_ssm_D, v_ssm_norm, v_c_q_norm, v_c_k_norm, v_w_oa, v_w_ob, v_w_oc, v_w_out, v_ffn_w_up, v_ffn_w_gate, v_ffn_conv_w, v_ffn_conv_b, v_ffn_w_down, v_final_norm):
    args = (x, c, ctx, c_ctx, w_mod, b_mod, norm1, norm2, w_in, a_sink, ssm_conv_w, ssm_conv_b, ssm_A_log, ssm_dt_bias, ssm_D, ssm_norm, c_q_norm, c_k_norm, w_oa, w_ob, w_oc, w_out, ffn_w_up, ffn_w_gate, ffn_conv_w, ffn_conv_b, ffn_w_down, final_norm, loss_target)
    moms = (m_c_ctx, m_w_mod, m_b_mod, m_norm1, m_norm2, m_w_in, m_a_sink, m_ssm_conv_w, m_ssm_conv_b, m_ssm_A_log, m_ssm_dt_bias, m_ssm_D, m_ssm_norm, m_c_q_norm, m_c_k_norm, m_w_oa, m_w_ob, m_w_oc, m_w_out, m_ffn_w_up, m_ffn_w_gate, m_ffn_conv_w, m_ffn_conv_b, m_ffn_w_down, m_final_norm)
    vars_ = (v_c_ctx, v_w_mod, v_b_mod, v_norm1, v_norm2, v_w_in, v_a_sink, v_ssm_conv_w, v_ssm_conv_b, v_ssm_A_log, v_ssm_dt_bias, v_ssm_D, v_ssm_norm, v_c_q_norm, v_c_k_norm, v_w_oa, v_w_ob, v_w_oc, v_w_out, v_ffn_w_up, v_ffn_w_gate, v_ffn_conv_w, v_ffn_conv_b, v_ffn_w_down, v_final_norm)
    p = dict(zip(IN_NAMES, args))
    mom = dict(zip(WEIGHTS, moms))
    var = dict(zip(WEIGHTS, vars_))
    depth = w_in.shape[0]
    n_ctx = ctx.shape[1]
    xi, yi, ci = lax.axis_index("x"), lax.axis_index("y"), lax.axis_index("c")
    dev = 4 * xi + 2 * yi + ci
    big_names = list(BIG)

    assert depth == 2 and n_ctx == ROW_TILE
    shards = [{n: p[n][li].astype(bf16) for n in big_names} for li in range(depth)]
    g_big0 = all_gather([shards[0][n] for n in big_names], "gather_l0")
    g_conv = all_gather([_to_flat(_pack(p, CONV_W))], "gather_conv")[0].reshape(8, -1)
    big0 = {n: _assemble(n, g) for n, g in zip(big_names, g_big0)}
    stand_ins = {n: jnp.zeros((8,) + shards[1][n].shape, bf16) for n in big_names}
    conv_full, off = {}, 0
    for n in CONV_W:
        shp = p[n].shape
        seg = g_conv[:, off:off + p[n].size].reshape(8, *shp)
        conv_full[n] = jnp.moveaxis(seg, 0, -2).reshape(*shp[:-1], 8 * shp[-1])
        off += p[n].size
    small = {n: p[n] for n in REPL}
    small.update(conv_full)

    loss, (g_big0, g_big1, g_small, g_x) = jax.value_and_grad(_loss_fn, argnums=(0, 2, 3, 4))(
        big0, shards[1], stand_ins, small, x[0], ctx[0], c, loss_target[0], n_ctx)
    loss = lax.psum(loss, AXES)

    def send_rows(n):
        if BIG[n] == 0:
            return g_big0[n].reshape(8, -1, g_big0[n].shape[-1])
        b = p[n].shape[-1]
        return jnp.stack([g_big0[n][:, b * j:b * (j + 1)] for j in range(8)])

    send = [send_rows(n) for n in big_names]
    from_sibling = rs_to_sibling(send)
    my_c = ci.reshape(1).astype(jnp.int32)
    side_sum = [pair_sum(s, r, my_c, "rs_pair_sum_" + n) for n, s, r in zip(big_names, send, from_sibling)]
    from_chips = rs_to_chips(side_sum)
    chip = (2 * xi + yi).reshape(1).astype(jnp.int32)
    big_out = [{}, {}, {}, {}]
    for a, n in enumerate(big_names):
        parts = [[(side_sum[a], None, chip), (from_chips[a], (0, 1, 2), None)],
                 [(g_big1[n], None, chip), (g_big1[n], (4, 5, 6), None)]]
        outs = sum_adam(parts, p[n], mom[n], var[n], "adam_" + n)
        for k in range(4):
            big_out[k][n] = outs[k]

    sm_names = REPL + list(CONV_W)
    g_vec = _to_flat(_pack(g_small, sm_names))
    gathered = all_gather([g_vec], "gather_small_grads")[0]
    n_repl = sum(p[n].size for n in REPL)

    def repl_flat(tree):
        return _to_flat(jnp.concatenate([_pack(tree, REPL), jnp.zeros((g_vec.size - n_repl,), f32)]))

    outs_small = sum_adam([[(gathered, tuple(range(8)), None)]], repl_flat(p)[None], repl_flat(mom)[None], repl_flat(var)[None],
                          "adam_small")
    g_sum = outs_small[0].reshape(-1)
    small_out = [_unpack(o.reshape(-1), p, REPL) for o in outs_small]
    conv_g_full = _unpack(g_sum[n_repl:], conv_full, CONV_W)
    conv_g = {n: lax.dynamic_slice_in_dim(conv_g_full[n], dev * p[n].shape[-1], p[n].shape[-1], axis=2) for n in CONV_W}
    conv_gv = _to_flat(_pack(conv_g, CONV_W))
    outs_conv = sum_adam([[(conv_gv[None], (0,), None)]], _to_flat(_pack(p, CONV_W))[None], _to_flat(_pack(mom, CONV_W))[None],
                         _to_flat(_pack(var, CONV_W))[None], "adam_conv")
    conv_out = [_unpack(o.reshape(-1), p, CONV_W) for o in outs_conv]

    res = []
    for k in range(4):
        tree = {**big_out[k], **small_out[k], **conv_out[k]}
        res.append([tree[n] for n in WEIGHTS])
    return (loss, g_x[None], *res[0], *res[1], *res[2], *res[3])
```

```python
import functools

import jax
import jax.numpy as jnp
from jax import lax
from jax.experimental import pallas as pl
from jax.experimental.pallas import tpu as pltpu

f32 = jnp.float32
bf16 = jnp.bfloat16
MESH = pl.DeviceIdType.MESH
AXES = ("x", "y", "c")

GRID_W = 64
HEAD_DIM = 64
ROPE_BASE = 10000.0
EPS = 1e-6
WINDOW = 128
N_HEADS = 8
N_KV = 2
SSM_HEADS = 16
SSM_P = 64
SSM_G = 2
SSM_N = 128
SSM_INNER = SSM_HEADS * SSM_P
SSM_BC = SSM_G * SSM_N
SSM_Q = 128
Q_W = N_HEADS * HEAD_DIM
KV_W = N_KV * HEAD_DIM
DT_W = 2 * SSM_HEADS
DT_PAD = 128
ADAM_LR, ADAM_B1, ADAM_B2, ADAM_EPS, ADAM_WD, ADAM_STEP = 0.001, 0.9, 0.999, 1e-08, 0.01, 10

LANES = 128
ROW_TILE = 256
VMEM_BLOCK_BUDGET = 6 * 1024 * 1024
ATTN_SLAB = 128
MM_ROW_CAP = 1088
MM_TILE_CAP = 1536
NEG = -1e30

IN_NAMES = ['x', 'c', 'ctx', 'c_ctx', 'w_mod', 'b_mod', 'norm1', 'norm2', 'w_in', 'a_sink', 'ssm_conv_w', 'ssm_conv_b', 'ssm_A_log', 'ssm_dt_bias', 'ssm_D', 'ssm_norm', 'c_q_norm', 'c_k_norm', 'w_oa', 'w_ob', 'w_oc', 'w_out', 'ffn_w_up', 'ffn_w_gate', 'ffn_conv_w', 'ffn_conv_b', 'ffn_w_down', 'final_norm', 'loss_target']
WEIGHTS = IN_NAMES[3:28]
BIG = {'w_mod': 1, 'w_in': 1, 'w_oa': 1, 'w_ob': 0, 'w_oc': 1, 'w_out': 0, 'ffn_w_up': 1, 'ffn_w_gate': 1, 'ffn_w_down': 0}
EARLY = ['w_mod', 'w_in']
LATE = [n for n in BIG if n not in EARLY]
CONV_W = ('ssm_conv_w', 'ffn_conv_w')
REPL = [n for n in WEIGHTS if n not in BIG and n not in CONV_W]

NT = (((1,), (1,)), ((), ()))
TN = (((0,), (0,)), ((), ()))
NN = (((1,), (0,)), ((), ()))


def _cparams(*sem):
    return pltpu.CompilerParams(dimension_semantics=sem)


def _div_tile(n, unit, cap):
    for d in range(min(n, int(cap)), 0, -1):
        if n % d == 0 and d % unit == 0:
            return d
    return n


def _row_tile(m, row_bytes):
    return _div_tile(m, 16, max(16, VMEM_BLOCK_BUDGET // row_bytes))


def _mm_call(a, b, mode, out_dtype, name):
    if mode == "nn":
        (m, k), n = a.shape, b.shape[1]
    elif mode == "nt":
        (m, k), n = a.shape, b.shape[0]
    else:
        (k, m), n = a.shape, b.shape[1]
    dims = {"nn": NN, "nt": NT, "tn": TN}[mode]
    ia, ib = a.dtype.itemsize, b.dtype.itemsize
    tm = _div_tile(m, LANES, MM_TILE_CAP) if mode == "tn" else _div_tile(m, 16, MM_ROW_CAP)
    tn = _div_tile(n, LANES, min(MM_TILE_CAP, VMEM_BLOCK_BUDGET // (4 * tm)))
    tk = _div_tile(k, 16 if mode == "tn" else LANES,
                   min(MM_ROW_CAP if mode == "tn" else MM_TILE_CAP, VMEM_BLOCK_BUDGET // (tm * ia), VMEM_BLOCK_BUDGET // (tn * ib)))
    nk = k // tk

    def body(a_ref, b_ref, o_ref, *acc):
        part = lax.dot_general(a_ref[...].astype(bf16), b_ref[...].astype(bf16), dims, preferred_element_type=f32)
        if nk == 1:
            o_ref[...] = part.astype(o_ref.dtype)
            return
        kk = pl.program_id(2)

        @pl.when(kk == 0)
        def _():
            acc[0][...] = part

        @pl.when(kk > 0)
        def _():
            acc[0][...] += part

        @pl.when(kk == nk - 1)
        def _():
            o_ref[...] = acc[0][...].astype(o_ref.dtype)

    a_spec = pl.BlockSpec((tk, tm), lambda i, j, kk: (kk, i)) if mode == "tn" else pl.BlockSpec((tm, tk), lambda i, j, kk: (i, kk))
    b_spec = pl.BlockSpec((tn, tk), lambda i, j, kk: (j, kk)) if mode == "nt" else pl.BlockSpec((tk, tn), lambda i, j, kk: (kk, j))
    return pl.pallas_call(
        body, grid=(m // tm, n // tn, nk), in_specs=[a_spec, b_spec],
        out_specs=pl.BlockSpec((tm, tn), lambda i, j, kk: (i, j)),
        out_shape=jax.ShapeDtypeStruct((m, n), out_dtype),
        scratch_shapes=[pltpu.VMEM((tm, tn), f32)] if nk > 1 else [], name=name,
        compiler_params=_cparams("parallel", "parallel", "arbitrary"))(a, b)


def mm(a, b, name, out_dtype=None):
    @jax.custom_vjp
    def op(a, b):
        return _mm_call(a, b, "nn", out_dtype or bf16, name)

    def fwd(a, b):
        return op(a, b), (a, b)

    def bwd(res, g):
        a, b = res
        return _mm_call(g, b, "nt", a.dtype, name + "_da"), _mm_call(a, g, "tn", b.dtype, name + "_db")

    op.defvjp(fwd, bwd)
    return op(a, b)


def split_cols(u, widths):
    offs = [0]
    for w in widths:
        offs.append(offs[-1] + w)

    @jax.custom_vjp
    def op(u):
        return tuple(u[:, offs[i]:offs[i + 1]] for i in range(len(widths)))

    def fwd(u):
        return op(u), None

    def bwd(_, cts):
        return (jnp.concatenate(cts, axis=1),)

    op.defvjp(fwd, bwd)
    return op(u)


def rowwise(fn, rows, consts, pars, out_widths, out_dtypes, name):
    t = rows[0].shape[0]
    tm = ROW_TILE
    nb = t // tm
    nr, nc, npar = len(rows), len(consts), len(pars)

    def rspec(a):
        return pl.BlockSpec((tm, a.shape[1]), lambda i: (i, 0))

    def pspec(a):
        return pl.BlockSpec(a.shape, lambda i: (0,) * a.ndim)

    def call_fwd(rows, consts, pars):
        def body(*refs):
            blk = pl.program_id(0)
            ins = [r[...].astype(f32) for r in refs[:nr + nc]]
            ps = [r[...] for r in refs[nr + nc:nr + nc + npar]]
            outs = fn(blk, *ins, *ps)
            for o_ref, o in zip(refs[nr + nc + npar:], outs):
                o_ref[...] = o.astype(o_ref.dtype)

        return pl.pallas_call(
            body, grid=(nb,),
            in_specs=[rspec(a) for a in rows + consts] + [pspec(a) for a in pars],
            out_specs=[pl.BlockSpec((tm, w), lambda i: (i, 0)) for w in out_widths],
            out_shape=[jax.ShapeDtypeStruct((t, w), d) for w, d in zip(out_widths, out_dtypes)],
            name=name, compiler_params=_cparams("parallel"))(*rows, *consts, *pars)

    def call_bwd(rows, consts, pars, cts):
        nout = len(cts)

        def body(*refs):
            blk = pl.program_id(0)
            ins = [r[...].astype(f32) for r in refs[:nr]]
            cs = [r[...].astype(f32) for r in refs[nr:nr + nc]]
            ps = [r[...] for r in refs[nr + nc:nr + nc + npar]]
            dys = [r[...].astype(f32) for r in refs[nr + nc + npar:nr + nc + npar + nout]]
            d_refs = refs[nr + nc + npar + nout:]
            _, vjp = jax.vjp(lambda *a: tuple(fn(blk, *a[:nr], *cs, *a[nr:])), *ins, *ps)
            grads = vjp(tuple(dys))
            for d_ref, g in zip(d_refs[:nr], grads[:nr]):
                d_ref[...] = g.astype(d_ref.dtype)
            if npar:
                @pl.when(blk == 0)
                def _():
                    for d_ref in d_refs[nr:]:
                        d_ref[...] = jnp.zeros_like(d_ref)

                for d_ref, g in zip(d_refs[nr:], grads[nr:]):
                    d_ref[...] += g

        return pl.pallas_call(
            body, grid=(nb,),
            in_specs=[rspec(a) for a in rows + consts] + [pspec(a) for a in pars] + [rspec(a) for a in cts],
            out_specs=[rspec(a) for a in rows] + [pspec(a) for a in pars],
            out_shape=[jax.ShapeDtypeStruct(a.shape, a.dtype) for a in rows + pars],
            name=name + "_bwd", compiler_params=_cparams("arbitrary"))(*rows, *consts, *pars, *cts)

    @jax.custom_vjp
    def op(rows, consts, pars):
        return tuple(call_fwd(list(rows), list(consts), list(pars)))

    def fwd(rows, consts, pars):
        return op(rows, consts, pars), (rows, consts, pars)

    def bwd(res, cts):
        rows, consts, pars = res
        g = call_bwd(list(rows), list(consts), list(pars), list(cts))
        return tuple(g[:nr]), tuple(jnp.zeros_like(a) for a in consts), tuple(g[nr:])

    op.defvjp(fwd, bwd)
    return op(tuple(rows), tuple(consts), tuple(pars))


def colwise(fn, cols, pars, out_dtype, name):
    t, w = cols[0].shape
    tc = LANES
    nb = w // tc
    ncol, npar = len(cols), len(pars)

    def cspec(a):
        return pl.BlockSpec((a.shape[0], tc), lambda j: (0, j))

    def call_fwd(cols, pars):
        def body(*refs):
            ins = [r[...].astype(f32) for r in refs[:ncol]]
            ps = [r[...] for r in refs[ncol:ncol + npar]]
            refs[-1][...] = fn(*ins, *ps).astype(refs[-1].dtype)

        return pl.pallas_call(
            body, grid=(nb,), in_specs=[cspec(a) for a in cols + pars], out_specs=cspec(cols[0]),
            out_shape=jax.ShapeDtypeStruct((t, w), out_dtype), name=name, compiler_params=_cparams("parallel"))(*cols, *pars)

    def call_bwd(cols, pars, ct):
        def body(*refs):
            ins = [r[...].astype(f32) for r in refs[:ncol]]
            ps = [r[...] for r in refs[ncol:ncol + npar]]
            dy = refs[ncol + npar][...].astype(f32)
            d_refs = refs[ncol + npar + 1:]
            _, vjp = jax.vjp(fn, *ins, *ps)
            grads = vjp(dy)
            for d_ref, g in zip(d_refs, grads):
                d_ref[...] = g.astype(d_ref.dtype)

        return pl.pallas_call(
            body, grid=(nb,), in_specs=[cspec(a) for a in cols + pars + [ct]],
            out_specs=[cspec(a) for a in cols + pars],
            out_shape=[jax.ShapeDtypeStruct(a.shape, a.dtype) for a in cols + pars],
            name=name + "_bwd", compiler_params=_cparams("parallel"))(*cols, *pars, ct)

    @jax.custom_vjp
    def op(cols, pars):
        return call_fwd(list(cols), list(pars))

    def fwd(cols, pars):
        return op(cols, pars), (cols, pars)

    def bwd(res, ct):
        cols, pars = res
        g = call_bwd(list(cols), list(pars), ct)
        return tuple(g[:ncol]), tuple(g[ncol:])

    op.defvjp(fwd, bwd)
    return op(tuple(cols), tuple(pars))


def _sigmoid(x):
    return 1.0 / (1.0 + jnp.exp(-x))


def _silu(x):
    return x * _sigmoid(x)


def _rms(x, g):
    return x * lax.rsqrt(jnp.mean(x * x, axis=-1, keepdims=True) + EPS) * g


def _shift_rows(u, k, n_ctx):
    @jax.custom_vjp
    def op(u):
        t = u.shape[0]
        row = lax.broadcasted_iota(jnp.int32, u.shape, 0)
        edge = ((row == 0) | (row == n_ctx)) if k == 1 else ((row == n_ctx - 1) | (row == t - 1))
        return jnp.where(edge, 0.0, pltpu.roll(u, k % t, 0))

    op.defvjp(lambda u: (op(u), None), lambda _, g: (_shift_rows(g, -k, n_ctx),))
    return op(u)


def _dwconv(u, w0, w1, w2, b, n_ctx):
    return w0 * _shift_rows(u, 1, n_ctx) + w1 * u + w2 * _shift_rows(u, -1, n_ctx) + b


@jax.custom_vjp
def _swap_pairs(x):
    w = x.shape[1]
    lane = lax.broadcasted_iota(jnp.int32, x.shape, 1)
    return jnp.where(lane % 2 == 0, pltpu.roll(x, w - 1, 1), pltpu.roll(x, 1, 1))


_swap_pairs.defvjp(lambda x: (_swap_pairs(x), None), lambda _, g: (_swap_pairs(g),))


def _head_rms(x, g):
    w = x.shape[1]
    same = (lax.broadcasted_iota(jnp.int32, (w, w), 0) // HEAD_DIM) == (lax.broadcasted_iota(jnp.int32, (w, w), 1) // HEAD_DIM)
    ms = jnp.dot(x * x, same.astype(f32), precision=lax.Precision.HIGHEST, preferred_element_type=f32) * (1.0 / HEAD_DIM)
    return x * lax.rsqrt(ms + EPS) * g


def _band_ok(i, j, c0, shape, tq, tk):
    kpos = j * tk + lax.broadcasted_iota(jnp.int32, shape, 0)
    qpos = i * tq + (c0 + lax.broadcasted_iota(jnp.int32, shape, 1)) % tq
    return jnp.abs(qpos - kpos) <= WINDOW


def _kv_range(i, nb, window):
    is_ctx = i == 0
    if window:
        return jnp.where(is_ctx, 1, jnp.maximum(i - 1, 1)), jnp.where(is_ctx, 1, jnp.minimum(i + 2, nb))
    return 1, jnp.where(is_ctx, 1, nb)


def _sink_row(sink_ref, g, r, tq):
    return jnp.concatenate([jnp.full((1, tq), sink_ref[g * r + h], f32) for h in range(r)], axis=1)


def _attn_fwd_call(q, k, v, sink, window, name, shards=()):
    h, t, dh = q.shape
    nkv = k.shape[0]
    r = h // nkv
    tq = tk = ROW_TILE
    nb = t // tq
    rows = r * tq
    ns = len(shards)

    assert window or nb % 2 == 1, "the dense schedule takes the kv chunks after the context chunk in pairs"

    def body(sink_ref, q_ref, k_ref, v_ref, *rest):
        x_refs, (o_ref, lse_ref), gathered_refs = rest[:ns], rest[ns:ns + 2], rest[ns + 2:2 * ns + 2]
        m_scr, l_scr, acc_scr, s_a, s_b, p_a, p_b, a_a, a_b = rest[2 * ns + 2:2 * ns + 11]
        comm_sems = rest[2 * ns + 11:]
        g, i = pl.program_id(0), pl.program_id(1)
        if ns:
            @pl.when((g == 0) & (i == 0))
            def _():
                _gather_steps(x_refs, gathered_refs, *comm_sems)[0]()

        qv = q_ref[...].reshape(rows, dh)
        m_scr[...] = jnp.full_like(m_scr, NEG)
        l_scr[...] = jnp.zeros_like(l_scr)
        acc_scr[...] = jnp.zeros_like(acc_scr)

        def kv_rows(j):
            return pl.ds(pl.multiple_of(jnp.minimum(j, nb - 1) * tk, tk), tk)

        def scores(j, s_scr):
            s_scr[...] = lax.dot_general(k_ref[0, kv_rows(j), :], qv, NT, preferred_element_type=f32)

        def softmax(j, s_scr, p_scr, a_scr, masked):
            for cb in range(rows // ATTN_SLAB):
                cs = slice(cb * ATTN_SLAB, (cb + 1) * ATTN_SLAB)
                s = s_scr[:, cs]
                if masked:
                    s = jnp.where(_band_ok(i, j, cb * ATTN_SLAB, s.shape, tq, tk), s, NEG)
                m = m_scr[:, cs]
                m2 = jnp.maximum(m, jnp.max(s, axis=0, keepdims=True))
                p = jnp.exp(s - m2)
                a = jnp.exp(m - m2)
                l_scr[:, cs] = a * l_scr[:, cs] + jnp.sum(p, axis=0, keepdims=True)
                m_scr[:, cs] = m2
                a_scr[:, cs] = a
                p_scr[:, cs] = p.astype(bf16)

        def weighted_v(j, p_scr, a_scr):
            acc_scr[...] = a_scr[...] * acc_scr[...] + lax.dot_general(v_ref[0, kv_rows(j), :], p_scr[...], TN, preferred_element_type=f32)

        scores(0, s_a)
        softmax(0, s_a, p_a, a_a, False)
        if window:
            weighted_v(0, p_a, a_a)
            lo, hi = _kv_range(i, nb, window)

            def chunk(j, c):
                scores(j, s_a)
                softmax(j, s_a, p_a, a_a, True)
                weighted_v(j, p_a, a_a)
                return c

            lax.fori_loop(lo, hi, chunk, 0)
        else:
            scores(1, s_b)

            def pair(tt, c):
                j0 = 2 * tt + 1
                scores(j0 + 1, s_a)
                weighted_v(j0 - 1, p_a, a_a)
                softmax(j0, s_b, p_b, a_b, False)
                scores(j0 + 2, s_b)
                weighted_v(j0, p_b, a_b)
                softmax(j0 + 1, s_a, p_a, a_a, False)
                return c

            lax.fori_loop(0, jnp.where(i == 0, 0, (nb - 1) // 2), pair, 0)
            weighted_v(jnp.where(i == 0, 0, nb - 1), p_a, a_a)
        m, l, acc = m_scr[...], l_scr[...], acc_scr[...]
        if window:
            sk = _sink_row(sink_ref, g, r, tq)
            m2 = jnp.maximum(m, sk)
            a = jnp.exp(m - m2)
            l = a * l + jnp.exp(sk - m2)
            acc = a * acc
            m = m2
        o_ref[...] = (acc / l).T.reshape(r, tq, dh).astype(o_ref.dtype)
        lse_ref[0] = m + jnp.log(l)
        if ns:
            @pl.when((g == nkv - 1) & (i == nb - 1))
            def _():
                _gather_steps(x_refs, gathered_refs, *comm_sems)[1]()

    qspec = pl.BlockSpec((r, tq, dh), lambda g, i: (g, i, 0))
    kspec = pl.BlockSpec((1, t, dh), lambda g, i: (g, 0, 0))
    hbm = pl.BlockSpec(memory_space=pl.ANY)
    sem = ("arbitrary", "arbitrary") if ns else ("parallel", "parallel")
    return pl.pallas_call(
        body, grid=(nkv, nb),
        in_specs=[pl.BlockSpec(memory_space=pltpu.SMEM), qspec, kspec, kspec] + [hbm] * ns,
        out_specs=[qspec, pl.BlockSpec((1, 1, rows), lambda g, i: (g * nb + i, 0, 0))] + [hbm] * ns,
        out_shape=[jax.ShapeDtypeStruct((h, t, dh), bf16), jax.ShapeDtypeStruct((nkv * nb, 1, rows), f32)]
        + [jax.ShapeDtypeStruct((8,) + s.shape, s.dtype) for s in shards],
        scratch_shapes=[pltpu.VMEM((1, rows), f32), pltpu.VMEM((1, rows), f32), pltpu.VMEM((dh, rows), f32),
                        pltpu.VMEM((tk, rows), f32), pltpu.VMEM((tk, rows), f32), pltpu.VMEM((tk, rows), bf16),
                        pltpu.VMEM((tk, rows), bf16), pltpu.VMEM((1, rows), f32), pltpu.VMEM((1, rows), f32)]
        + (_gather_scratch(ns) if ns else []),
        name=name, compiler_params=_cparams(*sem))(sink, q, k, v, *shards)


def _attn_bwd_call(q, k, v, sink, o, lse, do, window, name, side_sums=()):
    h, t, dh = q.shape
    nkv = k.shape[0]
    r = h // nkv
    tq = tk = ROW_TILE
    nb = t // tq
    rows = r * tq
    ns = len(side_sums)

    def body(sink_ref, q_ref, k_ref, v_ref, o_ref, lse_ref, do_ref, *rest):
        ss_refs, (dq_ref, dk_ref, dv_ref, dsink_ref), got_refs = rest[:ns], rest[ns:ns + 4], rest[ns + 4:2 * ns + 4]
        s_a, s_b, dp_a, dp_b, p_a, p_b, ds_a, ds_b, dq_scr = rest[2 * ns + 4:2 * ns + 13]
        comm_sems = rest[2 * ns + 13:]
        g, i = pl.program_id(0), pl.program_id(1)
        if ns:
            @pl.when((g == 0) & (i == 0))
            def _():
                for cp in _chips_copies(ss_refs, got_refs, *comm_sems):
                    cp.start()

        @pl.when(i == 0)
        def _():
            dk_ref[...] = jnp.zeros_like(dk_ref)
            dv_ref[...] = jnp.zeros_like(dv_ref)

        qv = q_ref[...].reshape(rows, dh)
        dov = do_ref[...].reshape(rows, dh)
        lse_t = lse_ref[0]
        delta_t = jnp.sum((dov.astype(f32) * o_ref[...].reshape(rows, dh).astype(f32)).T, axis=0, keepdims=True)
        dq_scr[...] = jnp.zeros_like(dq_scr)
        q_t, do_t = qv.T, dov.T

        def kv_rows(j):
            return pl.ds(pl.multiple_of(jnp.minimum(j, nb - 1) * tk, tk), tk)

        def scores(j, s_scr, dp_scr):
            s_scr[...] = lax.dot_general(k_ref[0, kv_rows(j), :], qv, NT, preferred_element_type=f32)
            dp_scr[...] = lax.dot_general(v_ref[0, kv_rows(j), :], dov, NT, preferred_element_type=f32)

        def probs(j, s_scr, dp_scr, p_scr, ds_scr, masked):
            for cb in range(rows // ATTN_SLAB):
                cs = slice(cb * ATTN_SLAB, (cb + 1) * ATTN_SLAB)
                s = s_scr[:, cs]
                if masked:
                    s = jnp.where(_band_ok(i, j, cb * ATTN_SLAB, s.shape, tq, tk), s, NEG)
                p = jnp.exp(s - lse_t[:, cs])
                p_scr[:, cs] = p.astype(bf16)
                ds_scr[:, cs] = (p * (dp_scr[:, cs] - delta_t[:, cs])).astype(bf16)

        def grads(j, p_scr, ds_scr):
            dv_ref[0, :, kv_rows(j)] += lax.dot_general(do_t, p_scr[...], NT, preferred_element_type=f32)
            dk_ref[0, :, kv_rows(j)] += lax.dot_general(q_t, ds_scr[...], NT, preferred_element_type=f32)
            dq_scr[...] += lax.dot_general(k_ref[0, kv_rows(j), :], ds_scr[...], TN, preferred_element_type=f32)

        scores(0, s_a, dp_a)
        probs(0, s_a, dp_a, p_a, ds_a, False)
        if window:
            grads(0, p_a, ds_a)
            lo, hi = _kv_range(i, nb, window)

            def chunk(j, c):
                scores(j, s_a, dp_a)
                probs(j, s_a, dp_a, p_a, ds_a, True)
                grads(j, p_a, ds_a)
                return c

            lax.fori_loop(lo, hi, chunk, 0)
        else:
            scores(1, s_b, dp_b)

            def pair(tt, c):
                j0 = 2 * tt + 1
                scores(j0 + 1, s_a, dp_a)
                grads(j0 - 1, p_a, ds_a)
                probs(j0, s_b, dp_b, p_b, ds_b, False)
                scores(j0 + 2, s_b, dp_b)
                grads(j0, p_b, ds_b)
                probs(j0 + 1, s_a, dp_a, p_a, ds_a, False)
                return c

            lax.fori_loop(0, jnp.where(i == 0, 0, (nb - 1) // 2), pair, 0)
            grads(jnp.where(i == 0, 0, nb - 1), p_a, ds_a)
        dq_ref[...] = dq_scr[...].T.reshape(r, tq, dh).astype(dq_ref.dtype)
        if window:
            dsink_ref[0] = -jnp.exp(_sink_row(sink_ref, g, r, tq) - lse_t) * delta_t
        else:
            dsink_ref[...] = jnp.zeros_like(dsink_ref)
        if ns:
            @pl.when((g == nkv - 1) & (i == nb - 1))
            def _():
                for cp in _chips_copies(ss_refs, got_refs, *comm_sems):
                    cp.wait()

    qspec = pl.BlockSpec((r, tq, dh), lambda g, i: (g, i, 0))
    cspec = pl.BlockSpec((1, 1, rows), lambda g, i: (g * nb + i, 0, 0))
    kspec = pl.BlockSpec((1, t, dh), lambda g, i: (g, 0, 0))
    ktspec = pl.BlockSpec((1, dh, t), lambda g, i: (g, 0, 0))
    hbm = pl.BlockSpec(memory_space=pl.ANY)
    return pl.pallas_call(
        body, grid=(nkv, nb),
        in_specs=[pl.BlockSpec(memory_space=pltpu.SMEM), qspec, kspec, kspec, qspec, cspec, qspec] + [hbm] * ns,
        out_specs=[qspec, ktspec, ktspec, cspec] + [hbm] * ns,
        out_shape=[jax.ShapeDtypeStruct((h, t, dh), bf16), jax.ShapeDtypeStruct((nkv, dh, t), f32), jax.ShapeDtypeStruct((nkv, dh, t), f32),
                   jax.ShapeDtypeStruct((nkv * nb, 1, rows), f32)] + [jax.ShapeDtypeStruct((3,) + s.shape[1:], s.dtype) for s in side_sums],
        scratch_shapes=[pltpu.VMEM((tk, rows), f32)] * 4 + [pltpu.VMEM((tk, rows), bf16)] * 4 + [pltpu.VMEM((dh, rows), f32)]
        + ([pltpu.SemaphoreType.DMA((3 * ns,)), pltpu.SemaphoreType.DMA((3 * ns,))] if ns else []),
        name=name, compiler_params=_cparams("arbitrary" if ns else "parallel", "arbitrary"))(sink, q, k, v, o, lse, do, *side_sums)


def attention(q, k, v, sink, window, name, shards=(), stand_ins=()):
    @jax.custom_vjp
    def op(q, k, v, sink, shards, stand_ins):
        o, _, *gathered = _attn_fwd_call(q, k, v, sink, window, name, shards)
        return o, tuple(gathered)

    def fwd(q, k, v, sink, shards, stand_ins):
        o, lse, *gathered = _attn_fwd_call(q, k, v, sink, window, name, shards)
        return (o, tuple(gathered)), (q, k, v, sink, o, lse, shards)

    def bwd(res, cts):
        q, k, v, sink, o, lse, shards = res
        do, d_gathered = cts
        side_sums = []
        if shards:
            my_c = lax.axis_index("c").reshape(1).astype(jnp.int32)
            from_sibling = rs_to_sibling(list(d_gathered), name + "_rs_sibling")
            side_sums = [pair_sum(s, rr, my_c, f"{name}_pair_sum{a}") for a, (s, rr) in enumerate(zip(d_gathered, from_sibling))]
        dq, dk, dv, dsink_rows, *from_chips = _attn_bwd_call(q, k, v, sink, o, lse, do, window, name + "_bwd", side_sums)
        nkv, r = k.shape[0], q.shape[0] // k.shape[0]
        dsink = jnp.sum(dsink_rows.reshape(nkv, -1, r, ROW_TILE), axis=(1, 3)).reshape(nkv * r)
        reduced = tuple(jnp.concatenate([s, fc, jnp.zeros_like(s[:1])], axis=0) for s, fc in zip(side_sums, from_chips))
        dk, dv = dk.transpose(0, 2, 1).astype(k.dtype), dv.transpose(0, 2, 1).astype(v.dtype)
        return dq, dk, dv, dsink, tuple(jnp.zeros_like(s) for s in shards), reduced

    op.defvjp(fwd, bwd)
    return op(q, k, v, sink, tuple(shards), tuple(stand_ins))


def _ssd_chunk(xs, dtx, dtr, ac, bs, cs, hin, rev):
    q = xs[0].shape[0]
    ii = lax.broadcasted_iota(jnp.int32, (q, q), 0)
    jj = lax.broadcasted_iota(jnp.int32, (q, q), 1)
    tri = (ii <= jj) if rev else (ii >= jj)
    lo = lax.broadcasted_iota(jnp.int32, (q, LANES), 1) < SSM_P
    lo_row = lax.broadcasted_iota(jnp.int32, (1, LANES), 1) < SSM_P
    heads, slabs, per_group = range(SSM_HEADS), range(SSM_HEADS // 2), SSM_HEADS // 2 // SSM_G

    a = [dtr[h] * ac[h] for h in heads]
    c = [jnp.sum(jnp.where(tri, jnp.broadcast_to(a[h], (q, q)), 0.0), axis=1, keepdims=True) for h in heads]
    tot = [jnp.sum(a[h], axis=1, keepdims=True) for h in heads]
    cf = [jnp.broadcast_to(c[h], (q, q)) for h in heads]
    seg = [jnp.minimum(cf[h] - cf[h].T, 0.0) for h in heads]
    decay = [jnp.where(tri, jnp.exp(seg[h]), 0.0) for h in heads]
    cb = [lax.dot_general(cs[g].astype(bf16), bs[g].astype(bf16), NT, preferred_element_type=f32) for g in range(SSM_G)]
    m = [jnp.concatenate([cb[j // per_group] * decay[2 * j], cb[j // per_group] * decay[2 * j + 1]], axis=1).astype(bf16) for j in slabs]
    xdt = [xs[j] * dtx[j] for j in slabs]
    x2 = [jnp.concatenate([jnp.where(lo, xdt[j], 0.0), jnp.where(lo, 0.0, xdt[j])], axis=0).astype(bf16) for j in slabs]
    y_diag = [jnp.dot(m[j], x2[j], preferred_element_type=f32) for j in slabs]
    csel = [jnp.where(lo, cf[2 * j], cf[2 * j + 1]) for j in slabs]
    tsel = [jnp.where(lo_row, jnp.broadcast_to(tot[2 * j], (1, LANES)), jnp.broadcast_to(tot[2 * j + 1], (1, LANES))) for j in slabs]
    xend = [(xdt[j] * jnp.exp(tsel[j] - csel[j])).astype(bf16) for j in slabs]
    st = [lax.dot_general(bs[j // per_group].astype(bf16), xend[j], TN, preferred_element_type=f32) for j in slabs]
    y_off = [jnp.dot(cs[j // per_group].astype(bf16), hin[j].astype(bf16), preferred_element_type=f32) * jnp.exp(csel[j]) for j in slabs]
    return [y_diag[j] + y_off[j] for j in slabs], [hin[j] * jnp.exp(tsel[j]) + st[j] for j in slabs]


def _ssd_order(s, nc, ncc, rev):
    if not rev:
        return s
    return jnp.where(s < ncc, ncc - 1 - s, nc - 1 - (s - ncc))


SSD_SLABS = [slice(LANES * j, LANES * (j + 1)) for j in range(SSM_HEADS // 2)]
SSD_GROUPS = [slice(SSM_N * g, SSM_N * (g + 1)) for g in range(SSM_G)]


def _head_lanes(w, transpose=False):
    shape = (w, SSM_HEADS) if transpose else (SSM_HEADS, w)
    head = lax.broadcasted_iota(jnp.int32, shape, 1 if transpose else 0)
    lane = lax.broadcasted_iota(jnp.int32, shape, 0 if transpose else 1)
    return (lane // SSM_P == head).astype(f32)


def _ssd_fwd_call(xs, dt, dtr, bm, cm, acol, rev, n_ctx, name):
    t, w = xs.shape
    q = SSM_Q
    nc, ncc = t // q, n_ctx // q

    def body(xs_ref, dt_ref, dtr_ref, b_ref, c_ref, a_ref, y_ref, hin_ref, h_scr):
        @pl.when(pl.program_id(0) == 0)
        def _():
            h_scr[...] = jnp.zeros_like(h_scr)

        hin_ref[0] = h_scr[...]
        dtx = jnp.dot(dt_ref[...], _head_lanes(w), precision=lax.Precision.HIGHEST, preferred_element_type=f32)
        ys, houts = _ssd_chunk([xs_ref[:, sl] for sl in SSD_SLABS], [dtx[:, sl] for sl in SSD_SLABS],
                               [dtr_ref[h:h + 1, :] for h in range(SSM_HEADS)], [a_ref[h:h + 1, :] for h in range(SSM_HEADS)],
                               [b_ref[:, gs] for gs in SSD_GROUPS], [c_ref[:, gs] for gs in SSD_GROUPS],
                               [h_scr[:, sl] for sl in SSD_SLABS], rev)
        for sl, y, hout in zip(SSD_SLABS, ys, houts):
            y_ref[:, sl] = y.astype(y_ref.dtype)
            h_scr[:, sl] = hout

    def at(s):
        return _ssd_order(s, nc, ncc, rev)

    return pl.pallas_call(
        body, grid=(nc,),
        in_specs=[pl.BlockSpec((q, w), lambda s: (at(s), 0)), pl.BlockSpec((q, SSM_HEADS), lambda s: (at(s), 0)),
                  pl.BlockSpec((SSM_HEADS, q), lambda s: (0, at(s))),
                  pl.BlockSpec((q, SSM_BC), lambda s: (at(s), 0)), pl.BlockSpec((q, SSM_BC), lambda s: (at(s), 0)),
                  pl.BlockSpec((SSM_HEADS, 1), lambda s: (0, 0))],
        out_specs=[pl.BlockSpec((q, w), lambda s: (at(s), 0)), pl.BlockSpec((1, SSM_N, w), lambda s: (s, 0, 0))],
        out_shape=[jax.ShapeDtypeStruct((t, w), xs.dtype), jax.ShapeDtypeStruct((nc, SSM_N, w), f32)],
        scratch_shapes=[pltpu.VMEM((SSM_N, w), f32)],
        name=name, compiler_params=_cparams("arbitrary"))(xs, dt, dtr, bm, cm, acol)


def _ssd_bwd_call(xs, dt, dtr, bm, cm, acol, hin, dy, rev, n_ctx, name):
    t, w = xs.shape
    q = SSM_Q
    nc, ncc = t // q, n_ctx // q

    def body(xs_ref, dt_ref, dtr_ref, b_ref, c_ref, a_ref, hin_ref, dy_ref,
             dxs_ref, ddt_ref, ddtr_ref, db_ref, dc_ref, da_ref, dh_scr):
        @pl.when(pl.program_id(0) == 0)
        def _():
            dh_scr[...] = jnp.zeros_like(dh_scr)
            da_ref[...] = jnp.zeros_like(da_ref)

        dtx = jnp.dot(dt_ref[...], _head_lanes(w), precision=lax.Precision.HIGHEST, preferred_element_type=f32)
        _, vjp = jax.vjp(
            functools.partial(_ssd_chunk, rev=rev),
            [xs_ref[:, sl].astype(f32) for sl in SSD_SLABS], [dtx[:, sl] for sl in SSD_SLABS],
            [dtr_ref[h:h + 1, :] for h in range(SSM_HEADS)], [a_ref[h:h + 1, :] for h in range(SSM_HEADS)],
            [b_ref[:, gs].astype(f32) for gs in SSD_GROUPS], [c_ref[:, gs].astype(f32) for gs in SSD_GROUPS],
            [hin_ref[0, :, sl] for sl in SSD_SLABS])
        dxs, ddtx, ddtr, dac, dbs, dcs, dhin = vjp(([dy_ref[:, sl].astype(f32) for sl in SSD_SLABS], [dh_scr[:, sl] for sl in SSD_SLABS]))
        for j, sl in enumerate(SSD_SLABS):
            dxs_ref[:, sl] = dxs[j].astype(dxs_ref.dtype)
            dh_scr[:, sl] = dhin[j]
        for h in range(SSM_HEADS):
            ddtr_ref[h:h + 1, :] = ddtr[h]
            da_ref[h:h + 1, :] += dac[h]
        for g, gs in enumerate(SSD_GROUPS):
            db_ref[:, gs] = dbs[g].astype(db_ref.dtype)
            dc_ref[:, gs] = dcs[g].astype(dc_ref.dtype)
        ddt_ref[...] = jnp.dot(jnp.concatenate(ddtx, axis=1), _head_lanes(w, transpose=True),
                               precision=lax.Precision.HIGHEST, preferred_element_type=f32)

    def step(s):
        return nc - 1 - s

    def at(s):
        return _ssd_order(step(s), nc, ncc, rev)

    row = lambda wd: pl.BlockSpec((q, wd), lambda s: (at(s), 0))
    dtr_spec = pl.BlockSpec((SSM_HEADS, q), lambda s: (0, at(s)))
    a_spec = pl.BlockSpec((SSM_HEADS, 1), lambda s: (0, 0))
    return pl.pallas_call(
        body, grid=(nc,),
        in_specs=[row(w), row(SSM_HEADS), dtr_spec, row(SSM_BC), row(SSM_BC), a_spec,
                  pl.BlockSpec((1, SSM_N, w), lambda s: (step(s), 0, 0)), row(w)],
        out_specs=[row(w), row(SSM_HEADS), dtr_spec, row(SSM_BC), row(SSM_BC), a_spec],
        out_shape=[jax.ShapeDtypeStruct((t, w), xs.dtype), jax.ShapeDtypeStruct(dt.shape, f32), jax.ShapeDtypeStruct(dtr.shape, f32),
                   jax.ShapeDtypeStruct(bm.shape, bm.dtype), jax.ShapeDtypeStruct(cm.shape, cm.dtype), jax.ShapeDtypeStruct(acol.shape, f32)],
        scratch_shapes=[pltpu.VMEM((SSM_N, w), f32)],
        name=name, compiler_params=_cparams("arbitrary"))(xs, dt, dtr, bm, cm, acol, hin, dy)


def ssd_scan(xs, dt, dtr, bm, cm, acol, rev, n_ctx, name):
    @jax.custom_vjp
    def op(xs, dt, dtr, bm, cm, acol):
        return _ssd_fwd_call(xs, dt, dtr, bm, cm, acol, rev, n_ctx, name)[0]

    def fwd(xs, dt, dtr, bm, cm, acol):
        y, hin = _ssd_fwd_call(xs, dt, dtr, bm, cm, acol, rev, n_ctx, name)
        return y, (xs, dt, dtr, bm, cm, acol, hin)

    def bwd(res, dy):
        return tuple(_ssd_bwd_call(*res, dy, rev, n_ctx, name + "_bwd"))

    op.defvjp(fwd, bwd)
    return op(xs, dt, dtr, bm, cm, acol)


def _in_layout(d):
    return [('a_q', Q_W), ('a_k', KV_W), ('a_v', KV_W), ('b_z', SSM_INNER), ('b_xbc', SSM_INNER + 2 * SSM_BC), ('b_dt', DT_W),
            ('c_q', Q_W), ('c_k', KV_W), ('c_v', KV_W), ('g_a', d), ('g_b', d), ('g_c', d)]


def _dt_span(d):
    start = 0
    for name, n in _in_layout(d):
        if name == 'b_dt':
            return start, start + n
        start += n


@jax.custom_vjp
def _w_in_split(w):
    lo, hi = _dt_span(w.shape[0])
    dt = jnp.concatenate([w[:, lo:hi], jnp.zeros((w.shape[0], DT_PAD - (hi - lo)), w.dtype)], axis=1)
    return jnp.concatenate([w[:, :lo], w[:, hi:]], axis=1), dt


def _w_in_join(g_main, g_dt):
    lo, hi = _dt_span(g_main.shape[0])
    return jnp.concatenate([g_main[:, :lo], g_dt[:, :hi - lo], g_main[:, lo:]], axis=1)


_w_in_split.defvjp(lambda w: (_w_in_split(w), None), lambda _, g: (_w_in_join(*g),))


def _rope_tables(n_ctx, n_lat):
    rows = n_lat // GRID_W
    t_row = jnp.repeat(jnp.arange(rows), GRID_W).astype(f32)
    t_col = jnp.tile(jnp.arange(GRID_W), rows).astype(f32)
    n = HEAD_DIM // 4
    inv = ROPE_BASE ** (-jnp.arange(n, dtype=f32) / n)
    ang = jnp.concatenate([t_row[:, None] * inv, t_col[:, None] * inv], axis=-1)
    cos = jnp.concatenate([jnp.ones((n_ctx, HEAD_DIM // 2), f32), jnp.cos(ang)], axis=0)
    sin = jnp.concatenate([jnp.zeros((n_ctx, HEAD_DIM // 2), f32), jnp.sin(ang)], axis=0)
    return jnp.repeat(cos, 2, axis=1), jnp.stack([-sin, sin], axis=-1).reshape(sin.shape[0], HEAD_DIM)


def _heads_major(a, n_heads):
    return a.reshape(a.shape[0], n_heads, HEAD_DIM).transpose(1, 0, 2)


def _heads_minor(a):
    return a.transpose(1, 0, 2).reshape(a.shape[1], a.shape[0] * HEAD_DIM)


def _layer(xall, w, s, cm, tabs, n_ctx, li, gather_a, gather_c):
    t, d = xall.shape
    ncb = n_ctx // ROW_TILE
    nm = f"l{li}_"
    ctq, stq, ctk, stk = tabs
    def mod(blk, cmv, i):
        return jnp.where(blk < ncb, cmv[0:1, i * d:(i + 1) * d], cmv[1:2, i * d:(i + 1) * d])

    def norm_mod(blk, x, g, cmv):
        return (_rms(x, g) * (1.0 + mod(blk, cmv, 1)) + mod(blk, cmv, 0),)

    (h,) = rowwise(norm_mod, [xall], [], [s['norm1'][None], cm], [d], [bf16], nm + "norm1")
    w_main, w_dt = _w_in_split(w['w_in'])
    u = mm(h, w_main, nm + "in")
    b_dt = mm(h, w_dt, nm + "in_dt", f32)
    a_q, a_k, a_v, b_z, b_xbc, c_q, c_k, c_v, g_a, g_b, g_c = split_cols(u, [n for name, n in _in_layout(d) if name != 'b_dt'])

    def rope(blk, q, k, v, ct_q, st_q, ct_k, st_k):
        return q * ct_q + _swap_pairs(q) * st_q, k * ct_k + _swap_pairs(k) * st_k, v

    def norm_rope(blk, q, k, v, ct_q, st_q, ct_k, st_k, gq, gk):
        return rope(blk, _head_rms(q, gq), _head_rms(k, gk), v, ct_q, st_q, ct_k, st_k)

    qkv_w, qkv_t = [Q_W, KV_W, KV_W], [bf16, bf16, bf16]
    qa, ka, va = rowwise(rope, [a_q, a_k, a_v], [ctq, stq, ctk, stk], [], qkv_w, qkv_t, nm + "ropeA")
    gq = jnp.tile(s['c_q_norm'], N_HEADS)[None]
    gk = jnp.tile(s['c_k_norm'], N_KV)[None]
    qc, kc, vc = rowwise(norm_rope, [c_q, c_k, c_v], [ctq, stq, ctk, stk], [gq, gk], qkv_w, qkv_t, nm + "ropeC")
    ya, mine = attention(_heads_major(qa, N_HEADS), _heads_major(ka, N_KV), _heads_major(va, N_KV),
                         s['a_sink'], True, nm + "attnA", *gather_a[1:])
    yc, nxt = attention(_heads_major(qc, N_HEADS), _heads_major(kc, N_KV), _heads_major(vc, N_KV),
                        jnp.zeros((N_HEADS,), f32), False, nm + "attnC", *gather_c[1:])
    ya, yc = _heads_minor(ya), _heads_minor(yc)
    w = {**w, **{n: _assemble(n, g) for n, g in zip(gather_a[0], mine)}}
    w_next = {n: _assemble(n, g) for n, g in zip(gather_c[0], nxt)}

    cw, cb = s['ssm_conv_w'], s['ssm_conv_b']
    conv_silu = lambda uu, w3, b: _silu(_dwconv(uu, w3[0:1], w3[1:2], w3[2:3], b, n_ctx))
    xbc = colwise(conv_silu, [b_xbc], [cw, cb[None]], bf16, nm + "ssmconv")
    xs, bm, cmat = split_cols(xbc, [SSM_INNER, SSM_BC, SSM_BC])
    bias = jnp.concatenate([s['ssm_dt_bias'].reshape(1, DT_W), jnp.zeros((1, DT_PAD - DT_W), f32)], axis=1)

    def softplus(blk, r, b):
        z = r + b
        return (jnp.maximum(z, 0.0) + jnp.log(1.0 + jnp.exp(-jnp.abs(z))),)

    (dt_all,) = rowwise(softplus, [b_dt], [], [bias], [DT_PAD], [f32], nm + "dt")
    a_coef = -jnp.exp(s['ssm_A_log'])
    ys_dir = []
    for di, rev in enumerate((False, True)):
        dt = dt_all[:, di * SSM_HEADS:(di + 1) * SSM_HEADS]
        ys_dir.append(ssd_scan(xs, dt, dt.T, bm, cmat, a_coef[di][:, None], rev, n_ctx,
                               nm + ("ssd_r" if rev else "ssd_f")))

    def ssm_out(blk, yf, yb, x, z, dskip, g):
        return (_rms((yf + yb + x * dskip) * _silu(z), g),)

    (ysn,) = rowwise(ssm_out, [ys_dir[0], ys_dir[1], xs, b_z], [], [jnp.repeat(s['ssm_D'], SSM_P)[None], s['ssm_norm'][None]],
                     [SSM_INNER], [bf16], nm + "ssmout")

    pa, pb, pc = mm(ya, w['w_oa'], nm + "oa"), mm(ysn, w['w_ob'], nm + "ob"), mm(yc, w['w_oc'], nm + "oc")

    def merge(blk, ga, gb, gc, a, b, c):
        return (_sigmoid(ga) * a + _sigmoid(gb) * b + _sigmoid(gc) * c,)

    (mrg,) = rowwise(merge, [g_a, g_b, g_c, pa, pb, pc], [], [], [d], [bf16], nm + "merge")
    o = mm(mrg, w['w_out'], nm + "out")

    def resid_norm_mod(blk, x, oo, g, cmv):
        x1 = x + mod(blk, cmv, 2) * oo
        return x1, _rms(x1, g) * (1.0 + mod(blk, cmv, 4)) + mod(blk, cmv, 3)

    x1, h2 = rowwise(resid_norm_mod, [xall, o], [], [s['norm2'][None], cm], [d, d], [f32, bf16], nm + "norm2")
    up, gt = mm(h2, w['ffn_w_up'], nm + "up"), mm(h2, w['ffn_w_gate'], nm + "gate")
    fw, fb = s['ffn_conv_w'], s['ffn_conv_b']
    ffn_act = lambda g_, u_, w3, b: _silu(_dwconv(g_, w3[0:1], w3[1:2], w3[2:3], b, n_ctx)) * u_
    act = colwise(ffn_act, [gt, up], [fw, fb[None]], bf16, nm + "ffnact")
    f = mm(act, w['ffn_w_down'], nm + "down")

    def resid(blk, x, ff, cmv):
        return (x + mod(blk, cmv, 5) * ff,)

    (x2,) = rowwise(resid, [x1, f], [], [cm], [d], [f32], nm + "resid")
    return x2, w_next


def _assemble(name, gathered):
    if BIG[name] == 0:
        return gathered.reshape(-1, gathered.shape[-1])
    return jnp.concatenate([gathered[j] for j in range(8)], axis=1)


def _loss_fn(big0, shards0_late, stand_ins0, shards1, stand_ins1, small, x, ctx, c, target, n_ctx):
    n_lat, d = x.shape
    xall = jnp.concatenate([ctx, x], axis=0)
    ct, st = _rope_tables(n_ctx, n_lat)
    tabs = (jnp.tile(ct, (1, N_HEADS)) * HEAD_DIM ** -0.5, jnp.tile(st, (1, N_HEADS)) * HEAD_DIM ** -0.5,
            jnp.tile(ct, (1, N_KV)), jnp.tile(st, (1, N_KV)))
    srows = jnp.concatenate([_silu(small['c_ctx'])[None], _silu(c), jnp.zeros((14, d), f32)], axis=0)
    names = list(BIG)
    big = big0
    gather_a = (LATE, [shards0_late[n] for n in LATE], [stand_ins0[n] for n in LATE])
    gather_c = (names, [shards1[n] for n in names], [stand_ins1[n] for n in names])
    for li in range(2):
        cm = mm(srows, big['w_mod'], f"l{li}_mod", f32)[0:2] + small['b_mod'][li][None]
        sl = {k: v[li] for k, v in small.items() if k not in ('c_ctx', 'final_norm')}
        xall, big = _layer(xall, big, sl, cm, tabs, n_ctx, li, gather_a, gather_c)
        gather_a = gather_c = ((), (), ())
    ncb = n_ctx // ROW_TILE
    tgt = jnp.concatenate([jnp.zeros((n_ctx, d), f32), target], axis=0)

    def loss_rows(blk, xx, tg, g):
        e = _rms(xx, g) - tg
        return (jnp.where(blk < ncb, 0.0, 0.5) * jnp.mean(e * e, axis=-1, keepdims=True),)

    (rows,) = rowwise(loss_rows, [xall], [tgt], [small['final_norm'][None]], [1], [f32], "loss")
    return jnp.sum(rows)


def _hbm_call(body, ins, out_shapes, n_sems, name):
    any_spec = pl.BlockSpec(memory_space=pl.ANY)
    return pl.pallas_call(
        body, out_shape=out_shapes, in_specs=[any_spec] * len(ins), out_specs=[any_spec] * len(out_shapes),
        scratch_shapes=[pltpu.SemaphoreType.DMA((n_sems,)), pltpu.SemaphoreType.DMA((n_sems,)), pltpu.SemaphoreType.DMA((len(ins),))],
        name=name)(*ins)


def _gather_steps(x_refs, out_refs, send_sems, recv_sems, local_sems):
    n = len(x_refs)
    x, y, c = lax.axis_index("x"), lax.axis_index("y"), lax.axis_index("c")
    me, sibling = (x, y, c), (x, y, 1 - c)
    chips = [(1 - x, y), (x, 1 - y), (1 - x, 1 - y)]

    def copy(a, k, block, to, src=None):
        px, py, pc = block
        slot = out_refs[a].at[4 * px + 2 * py + pc]
        return pltpu.make_async_remote_copy(
            src_ref=slot if src is None else src, dst_ref=slot,
            send_sem=send_sems.at[7 * a + k], recv_sem=recv_sems.at[7 * a + k], device_id=to, device_id_type=MESH)

    mine = [pltpu.make_async_copy(x_refs[a], out_refs[a].at[4 * x + 2 * y + c], local_sems.at[a]) for a in range(n)]
    first = []
    for a in range(n):
        first += [copy(a, 1 + j, me, (*chip, c), src=x_refs[a]) for j, chip in enumerate(chips)]
        first.append(copy(a, 0, me, sibling, src=x_refs[a]))

    def start():
        for cp in mine + first:
            cp.start()

    def finish():
        passed = []
        for a in range(n):
            for j, chip in enumerate(chips):
                copy(a, 1 + j, (*chip, c), me).wait_recv()
                passed.append(copy(a, 4 + j, (*chip, c), sibling))
                passed[-1].start()
        for a in range(n):
            copy(a, 0, sibling, me).wait_recv()
            for j, chip in enumerate(chips):
                copy(a, 4 + j, (*chip, 1 - c), me).wait_recv()
        for cp in first + passed:
            cp.wait_send()
        for cp in mine:
            cp.wait()

    return start, finish


def _gather_scratch(n):
    return [pltpu.SemaphoreType.DMA((7 * n,)), pltpu.SemaphoreType.DMA((7 * n,)), pltpu.SemaphoreType.DMA((n,))]


def all_gather(shards, name):
    n = len(shards)

    def body(*refs):
        start, finish = _gather_steps(refs[:n], refs[n:2 * n], *refs[2 * n:])
        start()
        finish()

    return _hbm_call(body, shards, [jax.ShapeDtypeStruct((8,) + s.shape, s.dtype) for s in shards], 7 * n, name)


def rs_to_sibling(gs, name="rs_sibling"):
    n = len(gs)

    def body(*refs):
        g_refs, out_refs, (send_sems, recv_sems, _) = refs[:n], refs[n:2 * n], refs[2 * n:]
        x, y, c = lax.axis_index("x"), lax.axis_index("y"), lax.axis_index("c")
        copies = [pltpu.make_async_remote_copy(
            src_ref=g_refs[a].at[2 * k + (1 - c)], dst_ref=out_refs[a].at[k], send_sem=send_sems.at[4 * a + k],
            recv_sem=recv_sems.at[4 * a + k], device_id=(x, y, 1 - c), device_id_type=MESH) for a in range(n) for k in range(4)]
        for cp in copies:
            cp.start()
        for cp in copies:
            cp.wait()

    return _hbm_call(body, gs, [jax.ShapeDtypeStruct((4,) + g.shape[1:], g.dtype) for g in gs], 4 * n, name)


def rs_to_chips(ss):
    n = len(ss)

    def body(*refs):
        copies = _chips_copies(refs[:n], refs[n:2 * n], refs[2 * n], refs[2 * n + 1])
        for cp in copies:
            cp.start()
        for cp in copies:
            cp.wait()

    return _hbm_call(body, ss, [jax.ShapeDtypeStruct((3,) + s.shape[1:], s.dtype) for s in ss], 3 * n, "rs_chips")


def _chips_copies(s_refs, out_refs, send_sems, recv_sems):
    x, y, c = lax.axis_index("x"), lax.axis_index("y"), lax.axis_index("c")
    copies = []
    for a in range(len(s_refs)):
        for k, (fx, fy) in enumerate([(1, 0), (0, 1), (1, 1)]):
            px, py = (1 - x) if fx else x, (1 - y) if fy else y
            copies.append(pltpu.make_async_remote_copy(
                src_ref=s_refs[a].at[2 * px + py], dst_ref=out_refs[a].at[k], send_sem=send_sems.at[3 * a + k],
                recv_sem=recv_sems.at[3 * a + k], device_id=(px, py, c), device_id_type=MESH))
    return copies


def _flat_tile(rows, cols):
    return _row_tile(rows, 4 * 4 * cols)


def pair_sum(g, r1, my_c, name):
    _, rows, cols = g.shape
    tm = _flat_tile(rows, cols)

    def body(c_ref, g_ref, r_ref, o_ref):
        o_ref[...] = (g_ref[...].astype(f32) + r_ref[...].astype(f32)).astype(o_ref.dtype)

    return pl.pallas_call(
        body, grid_spec=pltpu.PrefetchScalarGridSpec(
            num_scalar_prefetch=1, grid=(4, rows // tm),
            in_specs=[pl.BlockSpec((1, tm, cols), lambda k, i, c: (2 * k + c[0], i, 0)),
                      pl.BlockSpec((1, tm, cols), lambda k, i, c: (k, i, 0))],
            out_specs=pl.BlockSpec((1, tm, cols), lambda k, i, c: (k, i, 0))),
        out_shape=jax.ShapeDtypeStruct((4, rows, cols), g.dtype), name=name,
        compiler_params=_cparams("parallel", "parallel"))(my_c, g, r1)


def _adam_math(w, g, m, v):
    m2 = ADAM_B1 * m + (1.0 - ADAM_B1) * g
    v2 = ADAM_B2 * v + (1.0 - ADAM_B2) * (g * g)
    m_hat = m2 / (1.0 - ADAM_B1 ** ADAM_STEP)
    v_hat = v2 / (1.0 - ADAM_B2 ** ADAM_STEP)
    return -ADAM_LR * (m_hat / (jnp.sqrt(v_hat) + ADAM_EPS) + ADAM_WD * w), m2, v2


def sum_adam(parts, w, m, v, name):
    groups, rows, cols = w.shape
    tm = _flat_tile(rows, cols)
    nblk = rows // tm
    flat = []
    scalars = [p[2] for ps in parts for p in ps if p[2] is not None]
    for gi, ps in enumerate(parts):
        flat.append([])
        for arr, static_rows, dyn in ps:
            if dyn is not None:
                flat[gi].append((arr, functools.partial(lambda l, i, s, gi: (s[0], jnp.where(l == gi, i, nblk - 1), 0), gi=gi)))
            else:
                for k in static_rows:
                    flat[gi].append((arr, functools.partial(lambda l, i, s, gi, k: (k, jnp.where(l == gi, i, nblk - 1), 0), gi=gi, k=k)))
    counts = [len(f) for f in flat]
    na = sum(counts)

    def body(s_ref, *refs):
        sums, at = [], 0
        for cnt in counts:
            g = refs[at][0].astype(f32)
            for r in refs[at + 1:at + cnt]:
                g = g + r[0].astype(f32)
            sums.append(g)
            at += cnt
        g = sums[0]
        for gi in range(1, groups):
            g = jnp.where(pl.program_id(0) == gi, sums[gi], g)
        w_ref, m_ref, v_ref = refs[na:na + 3]
        g_out, d_out, m_out, v_out = refs[na + 3:]
        d, m2, v2 = _adam_math(w_ref[0], g, m_ref[0], v_ref[0])
        g_out[0] = g
        d_out[0] = d
        m_out[0] = m2
        v_out[0] = v2

    blk = pl.BlockSpec((1, tm, cols), lambda l, i, s: (l, i, 0))
    scalar = scalars[0] if scalars else jnp.zeros((1,), jnp.int32)
    return pl.pallas_call(
        body, grid_spec=pltpu.PrefetchScalarGridSpec(
            num_scalar_prefetch=1, grid=(groups, nblk),
            in_specs=[pl.BlockSpec((1, tm, cols), im) for f in flat for _, im in f] + [blk, blk, blk],
            out_specs=[blk, blk, blk, blk]),
        out_shape=[jax.ShapeDtypeStruct((groups, rows, cols), f32)] * 4, name=name,
        compiler_params=_cparams("arbitrary", "arbitrary"))(scalar, *[a for f in flat for a, _ in f], w, m, v)


FLAT_COLS = 1024


def _to_flat(vec):
    n = vec.shape[0]
    total = -(-n // (8 * FLAT_COLS)) * 8 * FLAT_COLS
    return jnp.concatenate([vec, jnp.zeros((total - n,), vec.dtype)]).reshape(-1, FLAT_COLS)


def _pack(tree, names):
    return jnp.concatenate([tree[n].reshape(-1) for n in names])


def _unpack(vec, like, names):
    out, off = {}, 0
    for n in names:
        size = like[n].size
        out[n] = vec[off:off + size].reshape(like[n].shape)
        off += size
    return out


def kernel(x, c, ctx, c_ctx, w_mod, b_mod, norm1, norm2, w_in, a_sink, ssm_conv_w, ssm_conv_b, ssm_A_log, ssm_dt_bias, ssm_D, ssm_norm, c_q_norm, c_k_norm, w_oa, w_ob, w_oc, w_out, ffn_w_up, ffn_w_gate, ffn_conv_w, ffn_conv_b, ffn_w_down, final_norm, loss_target, m_c_ctx, m_w_mod, m_b_mod, m_norm1, m_norm2, m_w_in, m_a_sink, m_ssm_conv_w, m_ssm_conv_b, m_ssm_A_log, m_ssm_dt_bias, m_ssm_D, m_ssm_norm, m_c_q_norm, m_c_k_norm, m_w_oa, m_w_ob, m_w_oc, m_w_out, m_ffn_w_up, m_ffn_w_gate, m_ffn_conv_w, m_ffn_conv_b, m_ffn_w_down, m_final_norm, v_c_ctx, v_w_mod, v_b_mod, v_norm1, v_norm2, v_w_in, v_a_sink, v_ssm_conv_w, v_ssm_conv_b, v_ssm_A_log, v_ssm_dt_bias, v_ssm_D, v_ssm_norm, v_c_q_norm, v_c_k_norm, v_w_oa, v_w_ob, v_w_oc, v_w_out, v_ffn_w_up, v_ffn_w_gate, v_ffn_conv_w, v_ffn_conv_b, v_ffn_w_down, v_final_norm):
    args = (x, c, ctx, c_ctx, w_mod, b_mod, norm1, norm2, w_in, a_sink, ssm_conv_w, ssm_conv_b, ssm_A_log, ssm_dt_bias, ssm_D, ssm_norm, c_q_norm, c_k_norm, w_oa, w_ob, w_oc, w_out, ffn_w_up, ffn_w_gate, ffn_conv_w, ffn_conv_b, ffn_w_down, final_norm, loss_target)
    moms = (m_c_ctx, m_w_mod, m_b_mod, m_norm1, m_norm2, m_w_in, m_a_sink, m_ssm_conv_w, m_ssm_conv_b, m_ssm_A_log, m_ssm_dt_bias, m_ssm_D, m_ssm_norm, m_c_q_norm, m_c_k_norm, m_w_oa, m_w_ob, m_w_oc, m_w_out, m_ffn_w_up, m_ffn_w_gate, m_ffn_conv_w, m_ffn_conv_b, m_ffn_w_down, m_final_norm)
    vars_ = (v_c_ctx, v_w_mod, v_b_mod, v_norm1, v_norm2, v_w_in, v_a_sink, v_ssm_conv_w, v_ssm_conv_b, v_ssm_A_log, v_ssm_dt_bias, v_ssm_D, v_ssm_norm, v_c_q_norm, v_c_k_norm, v_w_oa, v_w_ob, v_w_oc, v_w_out, v_ffn_w_up, v_ffn_w_gate, v_ffn_conv_w, v_ffn_conv_b, v_ffn_w_down, v_final_norm)
    p = dict(zip(IN_NAMES, args))
    mom = dict(zip(WEIGHTS, moms))
    var = dict(zip(WEIGHTS, vars_))
    depth = w_in.shape[0]
    n_ctx = ctx.shape[1]
    xi, yi, ci = lax.axis_index("x"), lax.axis_index("y"), lax.axis_index("c")
    dev = 4 * xi + 2 * yi + ci
    big_names = list(BIG)

    assert depth == 2 and n_ctx == ROW_TILE
    shards = [{n: p[n][li].astype(bf16) for n in big_names} for li in range(depth)]
    g_early = all_gather([shards[0][n] for n in EARLY], "gather_l0")
    g_conv = all_gather([_to_flat(_pack(p, CONV_W))], "gather_conv")[0].reshape(8, -1)
    big0 = {n: _assemble(n, g) for n, g in zip(EARLY, g_early)}
    stand_ins = [{n: jnp.zeros((8,) + shards[li][n].shape, bf16) for n in (LATE, big_names)[li]} for li in range(depth)]
    conv_full, off = {}, 0
    for n in CONV_W:
        shp = p[n].shape
        seg = g_conv[:, off:off + p[n].size].reshape(8, *shp)
        conv_full[n] = jnp.moveaxis(seg, 0, -2).reshape(*shp[:-1], 8 * shp[-1])
        off += p[n].size
    small = {n: p[n] for n in REPL}
    small.update(conv_full)

    loss, (g_early, g_late0, g_big1, g_small, g_x) = jax.value_and_grad(_loss_fn, argnums=(0, 2, 4, 5, 6))(
        big0, {n: shards[0][n] for n in LATE}, stand_ins[0], shards[1], stand_ins[1], small, x[0], ctx[0], c, loss_target[0], n_ctx)
    loss = lax.psum(loss, AXES)

    def send_rows(n):
        b = p[n].shape[-1]
        return jnp.stack([g_early[n][:, b * j:b * (j + 1)] for j in range(8)])

    send = [send_rows(n) for n in EARLY]
    from_sibling = rs_to_sibling(send)
    my_c = ci.reshape(1).astype(jnp.int32)
    side_sum = [pair_sum(s, r, my_c, "rs_pair_sum_" + n) for n, s, r in zip(EARLY, send, from_sibling)]
    from_chips = rs_to_chips(side_sum)
    chip = (2 * xi + yi).reshape(1).astype(jnp.int32)
    big_out = [{}, {}, {}, {}]
    for n in big_names:
        if n in EARLY:
            a = EARLY.index(n)
            parts = [[(side_sum[a], None, chip), (from_chips[a], (0, 1, 2), None)]]
        else:
            parts = [[(g_late0[n], None, chip), (g_late0[n], (4, 5, 6), None)]]
        parts.append([(g_big1[n], None, chip), (g_big1[n], (4, 5, 6), None)])
        outs = sum_adam(parts, p[n], mom[n], var[n], "adam_" + n)
        for k in range(4):
            big_out[k][n] = outs[k]

    sm_names = REPL + list(CONV_W)
    g_vec = _to_flat(_pack(g_small, sm_names))
    gathered = all_gather([g_vec], "gather_small_grads")[0]
    n_repl = sum(p[n].size for n in REPL)

    def repl_flat(tree):
        return _to_flat(jnp.concatenate([_pack(tree, REPL), jnp.zeros((g_vec.size - n_repl,), f32)]))

    outs_small = sum_adam([[(gathered, tuple(range(8)), None)]], repl_flat(p)[None], repl_flat(mom)[None], repl_flat(var)[None],
                          "adam_small")
    g_sum = outs_small[0].reshape(-1)
    small_out = [_unpack(o.reshape(-1), p, REPL) for o in outs_small]
    conv_g_full = _unpack(g_sum[n_repl:], conv_full, CONV_W)
    conv_g = {n: lax.dynamic_slice_in_dim(conv_g_full[n], dev * p[n].shape[-1], p[n].shape[-1], axis=2) for n in CONV_W}
    conv_gv = _to_flat(_pack(conv_g, CONV_W))
    outs_conv = sum_adam([[(conv_gv[None], (0,), None)]], _to_flat(_pack(p, CONV_W))[None], _to_flat(_pack(mom, CONV_W))[None],
                         _to_flat(_pack(var, CONV_W))[None], "adam_conv")
    conv_out = [_unpack(o.reshape(-1), p, CONV_W) for o in outs_conv]

    res = []
    for k in range(4):
        tree = {**big_out[k], **small_out[k], **conv_out[k]}
        res.append([tree[n] for n in WEIGHTS])
    return (loss, g_x[None], *res[0], *res[1], *res[2], *res[3])
```

```python
import functools

import jax
import jax.numpy as jnp
from jax import lax
from jax.experimental import pallas as pl
from jax.experimental.pallas import tpu as pltpu

f32 = jnp.float32
bf16 = jnp.bfloat16
MESH = pl.DeviceIdType.MESH
AXES = ("x", "y", "c")

GRID_W = 64
HEAD_DIM = 64
ROPE_BASE = 10000.0
EPS = 1e-6
WINDOW = 128
N_HEADS = 8
N_KV = 2
SSM_HEADS = 16
SSM_P = 64
SSM_G = 2
SSM_N = 128
SSM_INNER = SSM_HEADS * SSM_P
SSM_BC = SSM_G * SSM_N
SSM_Q = 128
Q_W = N_HEADS * HEAD_DIM
KV_W = N_KV * HEAD_DIM
DT_W = 2 * SSM_HEADS
DT_PAD = 128
ADAM_LR, ADAM_B1, ADAM_B2, ADAM_EPS, ADAM_WD, ADAM_STEP = 0.001, 0.9, 0.999, 1e-08, 0.01, 10

LANES = 128
ROW_TILE = 256
VMEM_BLOCK_BUDGET = 6 * 1024 * 1024
ATTN_SLAB = 128
MM_ROW_CAP = 1088
MM_TILE_CAP = 1536
NEG = -1e30

IN_NAMES = ['x', 'c', 'ctx', 'c_ctx', 'w_mod', 'b_mod', 'norm1', 'norm2', 'w_in', 'a_sink', 'ssm_conv_w', 'ssm_conv_b', 'ssm_A_log', 'ssm_dt_bias', 'ssm_D', 'ssm_norm', 'c_q_norm', 'c_k_norm', 'w_oa', 'w_ob', 'w_oc', 'w_out', 'ffn_w_up', 'ffn_w_gate', 'ffn_conv_w', 'ffn_conv_b', 'ffn_w_down', 'final_norm', 'loss_target']
WEIGHTS = IN_NAMES[3:28]
BIG = {'w_mod': 1, 'w_in': 1, 'w_oa': 1, 'w_ob': 0, 'w_oc': 1, 'w_out': 0, 'ffn_w_up': 1, 'ffn_w_gate': 1, 'ffn_w_down': 0}
EARLY = ['w_mod', 'w_in']
LATE = [n for n in BIG if n not in EARLY]
CONV_W = ('ssm_conv_w', 'ffn_conv_w')
REPL = [n for n in WEIGHTS if n not in BIG and n not in CONV_W]

NT = (((1,), (1,)), ((), ()))
TN = (((0,), (0,)), ((), ()))
NN = (((1,), (0,)), ((), ()))


def _cparams(*sem):
    return pltpu.CompilerParams(dimension_semantics=sem)


def _div_tile(n, unit, cap):
    for d in range(min(n, int(cap)), 0, -1):
        if n % d == 0 and d % unit == 0:
            return d
    return n


def _row_tile(m, row_bytes):
    return _div_tile(m, 16, max(16, VMEM_BLOCK_BUDGET // row_bytes))


def _mm_call(a, b, mode, out_dtype, name):
    if mode == "nn":
        (m, k), n = a.shape, b.shape[1]
    elif mode == "nt":
        (m, k), n = a.shape, b.shape[0]
    else:
        (k, m), n = a.shape, b.shape[1]
    dims = {"nn": NN, "nt": NT, "tn": TN}[mode]
    ia, ib = a.dtype.itemsize, b.dtype.itemsize
    tm = _div_tile(m, LANES, MM_TILE_CAP) if mode == "tn" else _div_tile(m, 16, MM_ROW_CAP)
    tn = _div_tile(n, LANES, min(MM_TILE_CAP, VMEM_BLOCK_BUDGET // (4 * tm)))
    tk = _div_tile(k, 16 if mode == "tn" else LANES,
                   min(MM_ROW_CAP if mode == "tn" else MM_TILE_CAP, VMEM_BLOCK_BUDGET // (tm * ia), VMEM_BLOCK_BUDGET // (tn * ib)))
    nk = k // tk

    def body(a_ref, b_ref, o_ref, *acc):
        part = lax.dot_general(a_ref[...].astype(bf16), b_ref[...].astype(bf16), dims, preferred_element_type=f32)
        if nk == 1:
            o_ref[...] = part.astype(o_ref.dtype)
            return
        kk = pl.program_id(2)

        @pl.when(kk == 0)
        def _():
            acc[0][...] = part

        @pl.when(kk > 0)
        def _():
            acc[0][...] += part

        @pl.when(kk == nk - 1)
        def _():
            o_ref[...] = acc[0][...].astype(o_ref.dtype)

    a_spec = pl.BlockSpec((tk, tm), lambda i, j, kk: (kk, i)) if mode == "tn" else pl.BlockSpec((tm, tk), lambda i, j, kk: (i, kk))
    b_spec = pl.BlockSpec((tn, tk), lambda i, j, kk: (j, kk)) if mode == "nt" else pl.BlockSpec((tk, tn), lambda i, j, kk: (kk, j))
    return pl.pallas_call(
        body, grid=(m // tm, n // tn, nk), in_specs=[a_spec, b_spec],
        out_specs=pl.BlockSpec((tm, tn), lambda i, j, kk: (i, j)),
        out_shape=jax.ShapeDtypeStruct((m, n), out_dtype),
        scratch_shapes=[pltpu.VMEM((tm, tn), f32)] if nk > 1 else [], name=name,
        compiler_params=_cparams("parallel", "parallel", "arbitrary"))(a, b)


def mm(a, b, name, out_dtype=None):
    @jax.custom_vjp
    def op(a, b):
        return _mm_call(a, b, "nn", out_dtype or bf16, name)

    def fwd(a, b):
        return op(a, b), (a, b)

    def bwd(res, g):
        a, b = res
        return _mm_call(g, b, "nt", a.dtype, name + "_da"), _mm_call(a, g, "tn", b.dtype, name + "_db")

    op.defvjp(fwd, bwd)
    return op(a, b)


def split_cols(u, widths):
    offs = [0]
    for w in widths:
        offs.append(offs[-1] + w)

    @jax.custom_vjp
    def op(u):
        return tuple(u[:, offs[i]:offs[i + 1]] for i in range(len(widths)))

    def fwd(u):
        return op(u), None

    def bwd(_, cts):
        return (jnp.concatenate(cts, axis=1),)

    op.defvjp(fwd, bwd)
    return op(u)


def rowwise(fn, rows, consts, pars, out_widths, out_dtypes, name):
    t = rows[0].shape[0]
    tm = ROW_TILE
    nb = t // tm
    nr, nc, npar = len(rows), len(consts), len(pars)

    def rspec(a):
        return pl.BlockSpec((tm, a.shape[1]), lambda i: (i, 0))

    def pspec(a):
        return pl.BlockSpec(a.shape, lambda i: (0,) * a.ndim)

    def call_fwd(rows, consts, pars):
        def body(*refs):
            blk = pl.program_id(0)
            ins = [r[...].astype(f32) for r in refs[:nr + nc]]
            ps = [r[...] for r in refs[nr + nc:nr + nc + npar]]
            outs = fn(blk, *ins, *ps)
            for o_ref, o in zip(refs[nr + nc + npar:], outs):
                o_ref[...] = o.astype(o_ref.dtype)

        return pl.pallas_call(
            body, grid=(nb,),
            in_specs=[rspec(a) for a in rows + consts] + [pspec(a) for a in pars],
            out_specs=[pl.BlockSpec((tm, w), lambda i: (i, 0)) for w in out_widths],
            out_shape=[jax.ShapeDtypeStruct((t, w), d) for w, d in zip(out_widths, out_dtypes)],
            name=name, compiler_params=_cparams("parallel"))(*rows, *consts, *pars)

    def call_bwd(rows, consts, pars, cts):
        nout = len(cts)

        def body(*refs):
            blk = pl.program_id(0)
            ins = [r[...].astype(f32) for r in refs[:nr]]
            cs = [r[...].astype(f32) for r in refs[nr:nr + nc]]
            ps = [r[...] for r in refs[nr + nc:nr + nc + npar]]
            dys = [r[...].astype(f32) for r in refs[nr + nc + npar:nr + nc + npar + nout]]
            d_refs = refs[nr + nc + npar + nout:]
            _, vjp = jax.vjp(lambda *a: tuple(fn(blk, *a[:nr], *cs, *a[nr:])), *ins, *ps)
            grads = vjp(tuple(dys))
            for d_ref, g in zip(d_refs[:nr], grads[:nr]):
                d_ref[...] = g.astype(d_ref.dtype)
            if npar:
                @pl.when(blk == 0)
                def _():
                    for d_ref in d_refs[nr:]:
                        d_ref[...] = jnp.zeros_like(d_ref)

                for d_ref, g in zip(d_refs[nr:], grads[nr:]):
                    d_ref[...] += g

        return pl.pallas_call(
            body, grid=(nb,),
            in_specs=[rspec(a) for a in rows + consts] + [pspec(a) for a in pars] + [rspec(a) for a in cts],
            out_specs=[rspec(a) for a in rows] + [pspec(a) for a in pars],
            out_shape=[jax.ShapeDtypeStruct(a.shape, a.dtype) for a in rows + pars],
            name=name + "_bwd", compiler_params=_cparams("arbitrary"))(*rows, *consts, *pars, *cts)

    @jax.custom_vjp
    def op(rows, consts, pars):
        return tuple(call_fwd(list(rows), list(consts), list(pars)))

    def fwd(rows, consts, pars):
        return op(rows, consts, pars), (rows, consts, pars)

    def bwd(res, cts):
        rows, consts, pars = res
        g = call_bwd(list(rows), list(consts), list(pars), list(cts))
        return tuple(g[:nr]), tuple(jnp.zeros_like(a) for a in consts), tuple(g[nr:])

    op.defvjp(fwd, bwd)
    return op(tuple(rows), tuple(consts), tuple(pars))


def colwise(fn, cols, pars, out_dtype, name):
    t, w = cols[0].shape
    tc = LANES
    nb = w // tc
    ncol, npar = len(cols), len(pars)

    def cspec(a):
        return pl.BlockSpec((a.shape[0], tc), lambda j: (0, j))

    def call_fwd(cols, pars):
        def body(*refs):
            ins = [r[...].astype(f32) for r in refs[:ncol]]
            ps = [r[...] for r in refs[ncol:ncol + npar]]
            refs[-1][...] = fn(*ins, *ps).astype(refs[-1].dtype)

        return pl.pallas_call(
            body, grid=(nb,), in_specs=[cspec(a) for a in cols + pars], out_specs=cspec(cols[0]),
            out_shape=jax.ShapeDtypeStruct((t, w), out_dtype), name=name, compiler_params=_cparams("parallel"))(*cols, *pars)

    def call_bwd(cols, pars, ct):
        def body(*refs):
            ins = [r[...].astype(f32) for r in refs[:ncol]]
            ps = [r[...] for r in refs[ncol:ncol + npar]]
            dy = refs[ncol + npar][...].astype(f32)
            d_refs = refs[ncol + npar + 1:]
            _, vjp = jax.vjp(fn, *ins, *ps)
            grads = vjp(dy)
            for d_ref, g in zip(d_refs, grads):
                d_ref[...] = g.astype(d_ref.dtype)

        return pl.pallas_call(
            body, grid=(nb,), in_specs=[cspec(a) for a in cols + pars + [ct]],
            out_specs=[cspec(a) for a in cols + pars],
            out_shape=[jax.ShapeDtypeStruct(a.shape, a.dtype) for a in cols + pars],
            name=name + "_bwd", compiler_params=_cparams("parallel"))(*cols, *pars, ct)

    @jax.custom_vjp
    def op(cols, pars):
        return call_fwd(list(cols), list(pars))

    def fwd(cols, pars):
        return op(cols, pars), (cols, pars)

    def bwd(res, ct):
        cols, pars = res
        g = call_bwd(list(cols), list(pars), ct)
        return tuple(g[:ncol]), tuple(g[ncol:])

    op.defvjp(fwd, bwd)
    return op(tuple(cols), tuple(pars))


def _sigmoid(x):
    return 1.0 / (1.0 + jnp.exp(-x))


def _silu(x):
    return x * _sigmoid(x)


def _rms(x, g):
    return x * lax.rsqrt(jnp.mean(x * x, axis=-1, keepdims=True) + EPS) * g


def _shift_rows(u, k, n_ctx):
    @jax.custom_vjp
    def op(u):
        t = u.shape[0]
        row = lax.broadcasted_iota(jnp.int32, u.shape, 0)
        edge = ((row == 0) | (row == n_ctx)) if k == 1 else ((row == n_ctx - 1) | (row == t - 1))
        return jnp.where(edge, 0.0, pltpu.roll(u, k % t, 0))

    op.defvjp(lambda u: (op(u), None), lambda _, g: (_shift_rows(g, -k, n_ctx),))
    return op(u)


def _dwconv(u, w0, w1, w2, b, n_ctx):
    return w0 * _shift_rows(u, 1, n_ctx) + w1 * u + w2 * _shift_rows(u, -1, n_ctx) + b


@jax.custom_vjp
def _swap_pairs(x):
    w = x.shape[1]
    lane = lax.broadcasted_iota(jnp.int32, x.shape, 1)
    return jnp.where(lane % 2 == 0, pltpu.roll(x, w - 1, 1), pltpu.roll(x, 1, 1))


_swap_pairs.defvjp(lambda x: (_swap_pairs(x), None), lambda _, g: (_swap_pairs(g),))


def _head_rms(x, g):
    w = x.shape[1]
    same = (lax.broadcasted_iota(jnp.int32, (w, w), 0) // HEAD_DIM) == (lax.broadcasted_iota(jnp.int32, (w, w), 1) // HEAD_DIM)
    ms = jnp.dot(x * x, same.astype(f32), precision=lax.Precision.HIGHEST, preferred_element_type=f32) * (1.0 / HEAD_DIM)
    return x * lax.rsqrt(ms + EPS) * g


def _band_ok(i, j, c0, shape, tq, tk):
    kpos = j * tk + lax.broadcasted_iota(jnp.int32, shape, 0)
    qpos = i * tq + (c0 + lax.broadcasted_iota(jnp.int32, shape, 1)) % tq
    return jnp.abs(qpos - kpos) <= WINDOW


def _kv_range(i, nb, window):
    is_ctx = i == 0
    if window:
        return jnp.where(is_ctx, 1, jnp.maximum(i - 1, 1)), jnp.where(is_ctx, 1, jnp.minimum(i + 2, nb))
    return 1, jnp.where(is_ctx, 1, nb)


def _sink_row(sink_ref, g, r, tq):
    return jnp.concatenate([jnp.full((1, tq), sink_ref[g * r + h], f32) for h in range(r)], axis=1)


def _heads_to_rows(x, r):
    return jnp.concatenate([x[:, HEAD_DIM * h:HEAD_DIM * (h + 1)] for h in range(r)], axis=0)


def _cols_to_heads(xt, r, tq):
    return jnp.concatenate([xt[:, tq * h:tq * (h + 1)].T for h in range(r)], axis=1)


def _attn_fwd_call(q, k, v, sink, window, name, shards=()):
    nkv, t, dh = k.shape
    h = q.shape[1] // dh
    r = h // nkv
    tq = tk = ROW_TILE
    nb = t // tq
    rows = r * tq
    ns = len(shards)

    assert window or nb % 2 == 1, "the dense schedule takes the kv chunks after the context chunk in pairs"

    def body(sink_ref, q_ref, k_ref, v_ref, *rest):
        x_refs, (o_ref, lse_ref), gathered_refs = rest[:ns], rest[ns:ns + 2], rest[ns + 2:2 * ns + 2]
        m_scr, l_scr, acc_scr, s_a, s_b, p_a, p_b, a_a, a_b = rest[2 * ns + 2:2 * ns + 11]
        comm_sems = rest[2 * ns + 11:]
        g, i = pl.program_id(0), pl.program_id(1)
        if ns:
            @pl.when((g == 0) & (i == 0))
            def _():
                _gather_steps(x_refs, gathered_refs, *comm_sems)[0]()

        qv = _heads_to_rows(q_ref[...], r)
        m_scr[...] = jnp.full_like(m_scr, NEG)
        l_scr[...] = jnp.zeros_like(l_scr)
        acc_scr[...] = jnp.zeros_like(acc_scr)

        def kv_rows(j):
            return pl.ds(pl.multiple_of(jnp.minimum(j, nb - 1) * tk, tk), tk)

        def scores(j, s_scr):
            s_scr[...] = lax.dot_general(k_ref[0, kv_rows(j), :], qv, NT, preferred_element_type=f32)

        def softmax(j, s_scr, p_scr, a_scr, masked):
            for cb in range(rows // ATTN_SLAB):
                cs = slice(cb * ATTN_SLAB, (cb + 1) * ATTN_SLAB)
                s = s_scr[:, cs]
                if masked:
                    s = jnp.where(_band_ok(i, j, cb * ATTN_SLAB, s.shape, tq, tk), s, NEG)
                m = m_scr[:, cs]
                m2 = jnp.maximum(m, jnp.max(s, axis=0, keepdims=True))
                p = jnp.exp(s - m2)
                a = jnp.exp(m - m2)
                l_scr[:, cs] = a * l_scr[:, cs] + jnp.sum(p, axis=0, keepdims=True)
                m_scr[:, cs] = m2
                a_scr[:, cs] = a
                p_scr[:, cs] = p.astype(bf16)

        def weighted_v(j, p_scr, a_scr):
            acc_scr[...] = a_scr[...] * acc_scr[...] + lax.dot_general(v_ref[0, kv_rows(j), :], p_scr[...], TN, preferred_element_type=f32)

        scores(0, s_a)
        softmax(0, s_a, p_a, a_a, False)
        if window:
            weighted_v(0, p_a, a_a)
            lo, hi = _kv_range(i, nb, window)

            def chunk(j, c):
                scores(j, s_a)
                softmax(j, s_a, p_a, a_a, True)
                weighted_v(j, p_a, a_a)
                return c

            lax.fori_loop(lo, hi, chunk, 0)
        else:
            scores(1, s_b)

            def pair(tt, c):
                j0 = 2 * tt + 1
                scores(j0 + 1, s_a)
                weighted_v(j0 - 1, p_a, a_a)
                softmax(j0, s_b, p_b, a_b, False)
                scores(j0 + 2, s_b)
                weighted_v(j0, p_b, a_b)
                softmax(j0 + 1, s_a, p_a, a_a, False)
                return c

            lax.fori_loop(0, jnp.where(i == 0, 0, (nb - 1) // 2), pair, 0)
            weighted_v(jnp.where(i == 0, 0, nb - 1), p_a, a_a)
        m, l, acc = m_scr[...], l_scr[...], acc_scr[...]
        if window:
            sk = _sink_row(sink_ref, g, r, tq)
            m2 = jnp.maximum(m, sk)
            a = jnp.exp(m - m2)
            l = a * l + jnp.exp(sk - m2)
            acc = a * acc
            m = m2
        o_ref[...] = _cols_to_heads(acc / l, r, tq).astype(o_ref.dtype)
        lse_ref[0] = m + jnp.log(l)
        if ns:
            @pl.when((g == nkv - 1) & (i == nb - 1))
            def _():
                _gather_steps(x_refs, gathered_refs, *comm_sems)[1]()

    qspec = pl.BlockSpec((tq, r * dh), lambda g, i: (i, g))
    kspec = pl.BlockSpec((1, t, dh), lambda g, i: (g, 0, 0))
    hbm = pl.BlockSpec(memory_space=pl.ANY)
    sem = ("arbitrary", "arbitrary") if ns else ("parallel", "parallel")
    return pl.pallas_call(
        body, grid=(nkv, nb),
        in_specs=[pl.BlockSpec(memory_space=pltpu.SMEM), qspec, kspec, kspec] + [hbm] * ns,
        out_specs=[qspec, pl.BlockSpec((1, 1, rows), lambda g, i: (g * nb + i, 0, 0))] + [hbm] * ns,
        out_shape=[jax.ShapeDtypeStruct((t, h * dh), bf16), jax.ShapeDtypeStruct((nkv * nb, 1, rows), f32)]
        + [jax.ShapeDtypeStruct((8,) + s.shape, s.dtype) for s in shards],
        scratch_shapes=[pltpu.VMEM((1, rows), f32), pltpu.VMEM((1, rows), f32), pltpu.VMEM((dh, rows), f32),
                        pltpu.VMEM((tk, rows), f32), pltpu.VMEM((tk, rows), f32), pltpu.VMEM((tk, rows), bf16),
                        pltpu.VMEM((tk, rows), bf16), pltpu.VMEM((1, rows), f32), pltpu.VMEM((1, rows), f32)]
        + (_gather_scratch(ns) if ns else []),
        name=name, compiler_params=_cparams(*sem))(sink, q, k, v, *shards)


def _attn_bwd_call(q, k, v, sink, o, lse, do, window, name, side_sums=()):
    nkv, t, dh = k.shape
    h = q.shape[1] // dh
    r = h // nkv
    tq = tk = ROW_TILE
    nb = t // tq
    rows = r * tq
    ns = len(side_sums)

    def body(sink_ref, q_ref, k_ref, v_ref, o_ref, lse_ref, do_ref, *rest):
        ss_refs, (dq_ref, dk_ref, dv_ref, dsink_ref), got_refs = rest[:ns], rest[ns:ns + 4], rest[ns + 4:2 * ns + 4]
        s_a, s_b, dp_a, dp_b, p_a, p_b, ds_a, ds_b, dq_scr = rest[2 * ns + 4:2 * ns + 13]
        comm_sems = rest[2 * ns + 13:]
        g, i = pl.program_id(0), pl.program_id(1)
        if ns:
            @pl.when((g == 0) & (i == 0))
            def _():
                for cp in _chips_copies(ss_refs, got_refs, *comm_sems):
                    cp.start()

        @pl.when(i == 0)
        def _():
            dk_ref[...] = jnp.zeros_like(dk_ref)
            dv_ref[...] = jnp.zeros_like(dv_ref)

        qv = _heads_to_rows(q_ref[...], r)
        dov = _heads_to_rows(do_ref[...], r)
        lse_t = lse_ref[0]
        delta_t = jnp.sum((dov.astype(f32) * _heads_to_rows(o_ref[...], r).astype(f32)).T, axis=0, keepdims=True)
        dq_scr[...] = jnp.zeros_like(dq_scr)
        q_t, do_t = qv.T, dov.T

        def kv_rows(j):
            return pl.ds(pl.multiple_of(jnp.minimum(j, nb - 1) * tk, tk), tk)

        def scores(j, s_scr, dp_scr):
            s_scr[...] = lax.dot_general(k_ref[0, kv_rows(j), :], qv, NT, preferred_element_type=f32)
            dp_scr[...] = lax.dot_general(v_ref[0, kv_rows(j), :], dov, NT, preferred_element_type=f32)

        def probs(j, s_scr, dp_scr, p_scr, ds_scr, masked):
            for cb in range(rows // ATTN_SLAB):
                cs = slice(cb * ATTN_SLAB, (cb + 1) * ATTN_SLAB)
                s = s_scr[:, cs]
                if masked:
                    s = jnp.where(_band_ok(i, j, cb * ATTN_SLAB, s.shape, tq, tk), s, NEG)
                p = jnp.exp(s - lse_t[:, cs])
                p_scr[:, cs] = p.astype(bf16)
                ds_scr[:, cs] = (p * (dp_scr[:, cs] - delta_t[:, cs])).astype(bf16)

        def grads(j, p_scr, ds_scr):
            dv_ref[0, :, kv_rows(j)] += lax.dot_general(do_t, p_scr[...], NT, preferred_element_type=f32)
            dk_ref[0, :, kv_rows(j)] += lax.dot_general(q_t, ds_scr[...], NT, preferred_element_type=f32)
            dq_scr[...] += lax.dot_general(k_ref[0, kv_rows(j), :], ds_scr[...], TN, preferred_element_type=f32)

        scores(0, s_a, dp_a)
        probs(0, s_a, dp_a, p_a, ds_a, False)
        if window:
            grads(0, p_a, ds_a)
            lo, hi = _kv_range(i, nb, window)

            def chunk(j, c):
                scores(j, s_a, dp_a)
                probs(j, s_a, dp_a, p_a, ds_a, True)
                grads(j, p_a, ds_a)
                return c

            lax.fori_loop(lo, hi, chunk, 0)
        else:
            scores(1, s_b, dp_b)

            def pair(tt, c):
                j0 = 2 * tt + 1
                scores(j0 + 1, s_a, dp_a)
                grads(j0 - 1, p_a, ds_a)
                probs(j0, s_b, dp_b, p_b, ds_b, False)
                scores(j0 + 2, s_b, dp_b)
                grads(j0, p_b, ds_b)
                probs(j0 + 1, s_a, dp_a, p_a, ds_a, False)
                return c

            lax.fori_loop(0, jnp.where(i == 0, 0, (nb - 1) // 2), pair, 0)
            grads(jnp.where(i == 0, 0, nb - 1), p_a, ds_a)
        dq_ref[...] = _cols_to_heads(dq_scr[...], r, tq).astype(dq_ref.dtype)
        if window:
            dsink_ref[0] = -jnp.exp(_sink_row(sink_ref, g, r, tq) - lse_t) * delta_t
        else:
            dsink_ref[...] = jnp.zeros_like(dsink_ref)
        if ns:
            @pl.when((g == nkv - 1) & (i == nb - 1))
            def _():
                for cp in _chips_copies(ss_refs, got_refs, *comm_sems):
                    cp.wait()

    qspec = pl.BlockSpec((tq, r * dh), lambda g, i: (i, g))
    cspec = pl.BlockSpec((1, 1, rows), lambda g, i: (g * nb + i, 0, 0))
    kspec = pl.BlockSpec((1, t, dh), lambda g, i: (g, 0, 0))
    ktspec = pl.BlockSpec((1, dh, t), lambda g, i: (g, 0, 0))
    hbm = pl.BlockSpec(memory_space=pl.ANY)
    return pl.pallas_call(
        body, grid=(nkv, nb),
        in_specs=[pl.BlockSpec(memory_space=pltpu.SMEM), qspec, kspec, kspec, qspec, cspec, qspec] + [hbm] * ns,
        out_specs=[qspec, ktspec, ktspec, cspec] + [hbm] * ns,
        out_shape=[jax.ShapeDtypeStruct((t, h * dh), bf16), jax.ShapeDtypeStruct((nkv, dh, t), f32), jax.ShapeDtypeStruct((nkv, dh, t), f32),
                   jax.ShapeDtypeStruct((nkv * nb, 1, rows), f32)] + [jax.ShapeDtypeStruct((3,) + s.shape[1:], s.dtype) for s in side_sums],
        scratch_shapes=[pltpu.VMEM((tk, rows), f32)] * 4 + [pltpu.VMEM((tk, rows), bf16)] * 4 + [pltpu.VMEM((dh, rows), f32)]
        + ([pltpu.SemaphoreType.DMA((3 * ns,)), pltpu.SemaphoreType.DMA((3 * ns,))] if ns else []),
        name=name, compiler_params=_cparams("arbitrary" if ns else "parallel", "arbitrary"))(sink, q, k, v, o, lse, do, *side_sums)


def attention(q, k, v, sink, window, name, shards=(), stand_ins=()):
    @jax.custom_vjp
    def op(q, k, v, sink, shards, stand_ins):
        o, _, *gathered = _attn_fwd_call(q, k, v, sink, window, name, shards)
        return o, tuple(gathered)

    def fwd(q, k, v, sink, shards, stand_ins):
        o, lse, *gathered = _attn_fwd_call(q, k, v, sink, window, name, shards)
        return (o, tuple(gathered)), (q, k, v, sink, o, lse, shards)

    def bwd(res, cts):
        q, k, v, sink, o, lse, shards = res
        do, d_gathered = cts
        side_sums = []
        if shards:
            my_c = lax.axis_index("c").reshape(1).astype(jnp.int32)
            from_sibling = rs_to_sibling(list(d_gathered), name + "_rs_sibling")
            side_sums = [pair_sum(s, rr, my_c, f"{name}_pair_sum{a}") for a, (s, rr) in enumerate(zip(d_gathered, from_sibling))]
        dq, dk, dv, dsink_rows, *from_chips = _attn_bwd_call(q, k, v, sink, o, lse, do, window, name + "_bwd", side_sums)
        nkv, r = k.shape[0], q.shape[1] // (k.shape[0] * k.shape[2])
        dsink = jnp.sum(dsink_rows.reshape(nkv, -1, r, ROW_TILE), axis=(1, 3)).reshape(nkv * r)
        reduced = tuple(jnp.concatenate([s, fc, jnp.zeros_like(s[:1])], axis=0) for s, fc in zip(side_sums, from_chips))
        dk, dv = dk.transpose(0, 2, 1).astype(k.dtype), dv.transpose(0, 2, 1).astype(v.dtype)
        return dq, dk, dv, dsink, tuple(jnp.zeros_like(s) for s in shards), reduced

    op.defvjp(fwd, bwd)
    return op(q, k, v, sink, tuple(shards), tuple(stand_ins))


def _ssd_chunk(xs, dtx, dtr, ac, bs, cs, hin, rev):
    q = xs[0].shape[0]
    ii = lax.broadcasted_iota(jnp.int32, (q, q), 0)
    jj = lax.broadcasted_iota(jnp.int32, (q, q), 1)
    tri = (ii <= jj) if rev else (ii >= jj)
    lo = lax.broadcasted_iota(jnp.int32, (q, LANES), 1) < SSM_P
    lo_row = lax.broadcasted_iota(jnp.int32, (1, LANES), 1) < SSM_P
    heads, slabs, per_group = range(SSM_HEADS), range(SSM_HEADS // 2), SSM_HEADS // 2 // SSM_G

    a = [dtr[h] * ac[h] for h in heads]
    c = [jnp.sum(jnp.where(tri, jnp.broadcast_to(a[h], (q, q)), 0.0), axis=1, keepdims=True) for h in heads]
    tot = [jnp.sum(a[h], axis=1, keepdims=True) for h in heads]
    cf = [jnp.broadcast_to(c[h], (q, q)) for h in heads]
    seg = [jnp.minimum(cf[h] - cf[h].T, 0.0) for h in heads]
    decay = [jnp.where(tri, jnp.exp(seg[h]), 0.0) for h in heads]
    cb = [lax.dot_general(cs[g].astype(bf16), bs[g].astype(bf16), NT, preferred_element_type=f32) for g in range(SSM_G)]
    m = [jnp.concatenate([cb[j // per_group] * decay[2 * j], cb[j // per_group] * decay[2 * j + 1]], axis=1).astype(bf16) for j in slabs]
    xdt = [xs[j] * dtx[j] for j in slabs]
    x2 = [jnp.concatenate([jnp.where(lo, xdt[j], 0.0), jnp.where(lo, 0.0, xdt[j])], axis=0).astype(bf16) for j in slabs]
    y_diag = [jnp.dot(m[j], x2[j], preferred_element_type=f32) for j in slabs]
    csel = [jnp.where(lo, cf[2 * j], cf[2 * j + 1]) for j in slabs]
    tsel = [jnp.where(lo_row, jnp.broadcast_to(tot[2 * j], (1, LANES)), jnp.broadcast_to(tot[2 * j + 1], (1, LANES))) for j in slabs]
    xend = [(xdt[j] * jnp.exp(tsel[j] - csel[j])).astype(bf16) for j in slabs]
    st = [lax.dot_general(bs[j // per_group].astype(bf16), xend[j], TN, preferred_element_type=f32) for j in slabs]
    y_off = [jnp.dot(cs[j // per_group].astype(bf16), hin[j].astype(bf16), preferred_element_type=f32) * jnp.exp(csel[j]) for j in slabs]
    return [y_diag[j] + y_off[j] for j in slabs], [hin[j] * jnp.exp(tsel[j]) + st[j] for j in slabs]


def _ssd_order(s, nc, ncc, rev):
    if not rev:
        return s
    return jnp.where(s < ncc, ncc - 1 - s, nc - 1 - (s - ncc))


SSD_SLABS = [slice(LANES * j, LANES * (j + 1)) for j in range(SSM_HEADS // 2)]
SSD_GROUPS = [slice(SSM_N * g, SSM_N * (g + 1)) for g in range(SSM_G)]


def _head_lanes(w, transpose=False):
    shape = (w, SSM_HEADS) if transpose else (SSM_HEADS, w)
    head = lax.broadcasted_iota(jnp.int32, shape, 1 if transpose else 0)
    lane = lax.broadcasted_iota(jnp.int32, shape, 0 if transpose else 1)
    return (lane // SSM_P == head).astype(f32)


def _ssd_fwd_call(xs, dt, dtr, bm, cm, acol, rev, n_ctx, name):
    t, w = xs.shape
    q = SSM_Q
    nc, ncc = t // q, n_ctx // q

    def body(xs_ref, dt_ref, dtr_ref, b_ref, c_ref, a_ref, y_ref, hin_ref, h_scr):
        @pl.when(pl.program_id(0) == 0)
        def _():
            h_scr[...] = jnp.zeros_like(h_scr)

        hin_ref[0] = h_scr[...]
        dtx = jnp.dot(dt_ref[...], _head_lanes(w), precision=lax.Precision.HIGHEST, preferred_element_type=f32)
        ys, houts = _ssd_chunk([xs_ref[:, sl] for sl in SSD_SLABS], [dtx[:, sl] for sl in SSD_SLABS],
                               [dtr_ref[h:h + 1, :] for h in range(SSM_HEADS)], [a_ref[h:h + 1, :] for h in range(SSM_HEADS)],
                               [b_ref[:, gs] for gs in SSD_GROUPS], [c_ref[:, gs] for gs in SSD_GROUPS],
                               [h_scr[:, sl] for sl in SSD_SLABS], rev)
        for sl, y, hout in zip(SSD_SLABS, ys, houts):
            y_ref[:, sl] = y.astype(y_ref.dtype)
            h_scr[:, sl] = hout

    def at(s):
        return _ssd_order(s, nc, ncc, rev)

    return pl.pallas_call(
        body, grid=(nc,),
        in_specs=[pl.BlockSpec((q, w), lambda s: (at(s), 0)), pl.BlockSpec((q, SSM_HEADS), lambda s: (at(s), 0)),
                  pl.BlockSpec((SSM_HEADS, q), lambda s: (0, at(s))),
                  pl.BlockSpec((q, SSM_BC), lambda s: (at(s), 0)), pl.BlockSpec((q, SSM_BC), lambda s: (at(s), 0)),
                  pl.BlockSpec((SSM_HEADS, 1), lambda s: (0, 0))],
        out_specs=[pl.BlockSpec((q, w), lambda s: (at(s), 0)), pl.BlockSpec((1, SSM_N, w), lambda s: (s, 0, 0))],
        out_shape=[jax.ShapeDtypeStruct((t, w), xs.dtype), jax.ShapeDtypeStruct((nc, SSM_N, w), f32)],
        scratch_shapes=[pltpu.VMEM((SSM_N, w), f32)],
        name=name, compiler_params=_cparams("arbitrary"))(xs, dt, dtr, bm, cm, acol)


def _ssd_bwd_call(xs, dt, dtr, bm, cm, acol, hin, dy, rev, n_ctx, name):
    t, w = xs.shape
    q = SSM_Q
    nc, ncc = t // q, n_ctx // q

    def body(xs_ref, dt_ref, dtr_ref, b_ref, c_ref, a_ref, hin_ref, dy_ref,
             dxs_ref, ddt_ref, ddtr_ref, db_ref, dc_ref, da_ref, dh_scr):
        @pl.when(pl.program_id(0) == 0)
        def _():
            dh_scr[...] = jnp.zeros_like(dh_scr)
            da_ref[...] = jnp.zeros_like(da_ref)

        dtx = jnp.dot(dt_ref[...], _head_lanes(w), precision=lax.Precision.HIGHEST, preferred_element_type=f32)
        _, vjp = jax.vjp(
            functools.partial(_ssd_chunk, rev=rev),
            [xs_ref[:, sl].astype(f32) for sl in SSD_SLABS], [dtx[:, sl] for sl in SSD_SLABS],
            [dtr_ref[h:h + 1, :] for h in range(SSM_HEADS)], [a_ref[h:h + 1, :] for h in range(SSM_HEADS)],
            [b_ref[:, gs].astype(f32) for gs in SSD_GROUPS], [c_ref[:, gs].astype(f32) for gs in SSD_GROUPS],
            [hin_ref[0, :, sl] for sl in SSD_SLABS])
        dxs, ddtx, ddtr, dac, dbs, dcs, dhin = vjp(([dy_ref[:, sl].astype(f32) for sl in SSD_SLABS], [dh_scr[:, sl] for sl in SSD_SLABS]))
        for j, sl in enumerate(SSD_SLABS):
            dxs_ref[:, sl] = dxs[j].astype(dxs_ref.dtype)
            dh_scr[:, sl] = dhin[j]
        for h in range(SSM_HEADS):
            ddtr_ref[h:h + 1, :] = ddtr[h]
            da_ref[h:h + 1, :] += dac[h]
        for g, gs in enumerate(SSD_GROUPS):
            db_ref[:, gs] = dbs[g].astype(db_ref.dtype)
            dc_ref[:, gs] = dcs[g].astype(dc_ref.dtype)
        ddt_ref[...] = jnp.dot(jnp.concatenate(ddtx, axis=1), _head_lanes(w, transpose=True),
                               precision=lax.Precision.HIGHEST, preferred_element_type=f32)

    def step(s):
        return nc - 1 - s

    def at(s):
        return _ssd_order(step(s), nc, ncc, rev)

    row = lambda wd: pl.BlockSpec((q, wd), lambda s: (at(s), 0))
    dtr_spec = pl.BlockSpec((SSM_HEADS, q), lambda s: (0, at(s)))
    a_spec = pl.BlockSpec((SSM_HEADS, 1), lambda s: (0, 0))
    return pl.pallas_call(
        body, grid=(nc,),
        in_specs=[row(w), row(SSM_HEADS), dtr_spec, row(SSM_BC), row(SSM_BC), a_spec,
                  pl.BlockSpec((1, SSM_N, w), lambda s: (step(s), 0, 0)), row(w)],
        out_specs=[row(w), row(SSM_HEADS), dtr_spec, row(SSM_BC), row(SSM_BC), a_spec],
        out_shape=[jax.ShapeDtypeStruct((t, w), xs.dtype), jax.ShapeDtypeStruct(dt.shape, f32), jax.ShapeDtypeStruct(dtr.shape, f32),
                   jax.ShapeDtypeStruct(bm.shape, bm.dtype), jax.ShapeDtypeStruct(cm.shape, cm.dtype), jax.ShapeDtypeStruct(acol.shape, f32)],
        scratch_shapes=[pltpu.VMEM((SSM_N, w), f32)],
        name=name, compiler_params=_cparams("arbitrary"))(xs, dt, dtr, bm, cm, acol, hin, dy)


def ssd_scan(xs, dt, dtr, bm, cm, acol, rev, n_ctx, name):
    @jax.custom_vjp
    def op(xs, dt, dtr, bm, cm, acol):
        return _ssd_fwd_call(xs, dt, dtr, bm, cm, acol, rev, n_ctx, name)[0]

    def fwd(xs, dt, dtr, bm, cm, acol):
        y, hin = _ssd_fwd_call(xs, dt, dtr, bm, cm, acol, rev, n_ctx, name)
        return y, (xs, dt, dtr, bm, cm, acol, hin)

    def bwd(res, dy):
        return tuple(_ssd_bwd_call(*res, dy, rev, n_ctx, name + "_bwd"))

    op.defvjp(fwd, bwd)
    return op(xs, dt, dtr, bm, cm, acol)


def _in_layout(d):
    return [('a_q', Q_W), ('a_k', KV_W), ('a_v', KV_W), ('b_z', SSM_INNER), ('b_xbc', SSM_INNER + 2 * SSM_BC), ('b_dt', DT_W),
            ('c_q', Q_W), ('c_k', KV_W), ('c_v', KV_W), ('g_a', d), ('g_b', d), ('g_c', d)]


def _dt_span(d):
    start = 0
    for name, n in _in_layout(d):
        if name == 'b_dt':
            return start, start + n
        start += n


@jax.custom_vjp
def _w_in_split(w):
    lo, hi = _dt_span(w.shape[0])
    dt = jnp.concatenate([w[:, lo:hi], jnp.zeros((w.shape[0], DT_PAD - (hi - lo)), w.dtype)], axis=1)
    return jnp.concatenate([w[:, :lo], w[:, hi:]], axis=1), dt


def _w_in_join(g_main, g_dt):
    lo, hi = _dt_span(g_main.shape[0])
    return jnp.concatenate([g_main[:, :lo], g_dt[:, :hi - lo], g_main[:, lo:]], axis=1)


_w_in_split.defvjp(lambda w: (_w_in_split(w), None), lambda _, g: (_w_in_join(*g),))


def _rope_tables(n_ctx, n_lat):
    rows = n_lat // GRID_W
    t_row = jnp.repeat(jnp.arange(rows), GRID_W).astype(f32)
    t_col = jnp.tile(jnp.arange(GRID_W), rows).astype(f32)
    n = HEAD_DIM // 4
    inv = ROPE_BASE ** (-jnp.arange(n, dtype=f32) / n)
    ang = jnp.concatenate([t_row[:, None] * inv, t_col[:, None] * inv], axis=-1)
    cos = jnp.concatenate([jnp.ones((n_ctx, HEAD_DIM // 2), f32), jnp.cos(ang)], axis=0)
    sin = jnp.concatenate([jnp.zeros((n_ctx, HEAD_DIM // 2), f32), jnp.sin(ang)], axis=0)
    return jnp.repeat(cos, 2, axis=1), jnp.stack([-sin, sin], axis=-1).reshape(sin.shape[0], HEAD_DIM)


def _heads_major(a, n_heads):
    return a.reshape(a.shape[0], n_heads, HEAD_DIM).transpose(1, 0, 2)


def _heads_minor(a):
    return a.transpose(1, 0, 2).reshape(a.shape[1], a.shape[0] * HEAD_DIM)


def _layer(xall, w, s, cm, tabs, n_ctx, li, gather_a, gather_c):
    t, d = xall.shape
    ncb = n_ctx // ROW_TILE
    nm = f"l{li}_"
    ctq, stq, ctk, stk = tabs
    def mod(blk, cmv, i):
        return jnp.where(blk < ncb, cmv[0:1, i * d:(i + 1) * d], cmv[1:2, i * d:(i + 1) * d])

    def norm_mod(blk, x, g, cmv):
        return (_rms(x, g) * (1.0 + mod(blk, cmv, 1)) + mod(blk, cmv, 0),)

    (h,) = rowwise(norm_mod, [xall], [], [s['norm1'][None], cm], [d], [bf16], nm + "norm1")
    w_main, w_dt = _w_in_split(w['w_in'])
    u = mm(h, w_main, nm + "in")
    b_dt = mm(h, w_dt, nm + "in_dt", f32)
    a_q, a_k, a_v, b_z, b_xbc, c_q, c_k, c_v, g_a, g_b, g_c = split_cols(u, [n for name, n in _in_layout(d) if name != 'b_dt'])

    def rope(blk, q, k, v, ct_q, st_q, ct_k, st_k):
        return q * ct_q + _swap_pairs(q) * st_q, k * ct_k + _swap_pairs(k) * st_k, v

    def norm_rope(blk, q, k, v, ct_q, st_q, ct_k, st_k, gq, gk):
        return rope(blk, _head_rms(q, gq), _head_rms(k, gk), v, ct_q, st_q, ct_k, st_k)

    qkv_w, qkv_t = [Q_W, KV_W, KV_W], [bf16, bf16, bf16]
    qa, ka, va = rowwise(rope, [a_q, a_k, a_v], [ctq, stq, ctk, stk], [], qkv_w, qkv_t, nm + "ropeA")
    gq = jnp.tile(s['c_q_norm'], N_HEADS)[None]
    gk = jnp.tile(s['c_k_norm'], N_KV)[None]
    qc, kc, vc = rowwise(norm_rope, [c_q, c_k, c_v], [ctq, stq, ctk, stk], [gq, gk], qkv_w, qkv_t, nm + "ropeC")
    ya, mine = attention(qa, _heads_major(ka, N_KV), _heads_major(va, N_KV), s['a_sink'], True, nm + "attnA", *gather_a[1:])
    yc, nxt = attention(qc, _heads_major(kc, N_KV), _heads_major(vc, N_KV), jnp.zeros((N_HEADS,), f32), False,
                        nm + "attnC", *gather_c[1:])
    w = {**w, **{n: _assemble(n, g) for n, g in zip(gather_a[0], mine)}}
    w_next = {n: _assemble(n, g) for n, g in zip(gather_c[0], nxt)}

    cw, cb = s['ssm_conv_w'], s['ssm_conv_b']
    conv_silu = lambda uu, w3, b: _silu(_dwconv(uu, w3[0:1], w3[1:2], w3[2:3], b, n_ctx))
    xbc = colwise(conv_silu, [b_xbc], [cw, cb[None]], bf16, nm + "ssmconv")
    xs, bm, cmat = split_cols(xbc, [SSM_INNER, SSM_BC, SSM_BC])
    bias = jnp.concatenate([s['ssm_dt_bias'].reshape(1, DT_W), jnp.zeros((1, DT_PAD - DT_W), f32)], axis=1)

    def softplus(blk, r, b):
        z = r + b
        return (jnp.maximum(z, 0.0) + jnp.log(1.0 + jnp.exp(-jnp.abs(z))),)

    (dt_all,) = rowwise(softplus, [b_dt], [], [bias], [DT_PAD], [f32], nm + "dt")
    a_coef = -jnp.exp(s['ssm_A_log'])
    ys_dir = []
    for di, rev in enumerate((False, True)):
        dt = dt_all[:, di * SSM_HEADS:(di + 1) * SSM_HEADS]
        ys_dir.append(ssd_scan(xs, dt, dt.T, bm, cmat, a_coef[di][:, None], rev, n_ctx,
                               nm + ("ssd_r" if rev else "ssd_f")))

    def ssm_out(blk, yf, yb, x, z, dskip, g):
        return (_rms((yf + yb + x * dskip) * _silu(z), g),)

    (ysn,) = rowwise(ssm_out, [ys_dir[0], ys_dir[1], xs, b_z], [], [jnp.repeat(s['ssm_D'], SSM_P)[None], s['ssm_norm'][None]],
                     [SSM_INNER], [bf16], nm + "ssmout")

    pa, pb, pc = mm(ya, w['w_oa'], nm + "oa"), mm(ysn, w['w_ob'], nm + "ob"), mm(yc, w['w_oc'], nm + "oc")

    def merge(blk, ga, gb, gc, a, b, c):
        return (_sigmoid(ga) * a + _sigmoid(gb) * b + _sigmoid(gc) * c,)

    (mrg,) = rowwise(merge, [g_a, g_b, g_c, pa, pb, pc], [], [], [d], [bf16], nm + "merge")
    o = mm(mrg, w['w_out'], nm + "out")

    def resid_norm_mod(blk, x, oo, g, cmv):
        x1 = x + mod(blk, cmv, 2) * oo
        return x1, _rms(x1, g) * (1.0 + mod(blk, cmv, 4)) + mod(blk, cmv, 3)

    x1, h2 = rowwise(resid_norm_mod, [xall, o], [], [s['norm2'][None], cm], [d, d], [f32, bf16], nm + "norm2")
    up, gt = mm(h2, w['ffn_w_up'], nm + "up"), mm(h2, w['ffn_w_gate'], nm + "gate")
    fw, fb = s['ffn_conv_w'], s['ffn_conv_b']
    ffn_act = lambda g_, u_, w3, b: _silu(_dwconv(g_, w3[0:1], w3[1:2], w3[2:3], b, n_ctx)) * u_
    act = colwise(ffn_act, [gt, up], [fw, fb[None]], bf16, nm + "ffnact")
    f = mm(act, w['ffn_w_down'], nm + "down")

    def resid(blk, x, ff, cmv):
        return (x + mod(blk, cmv, 5) * ff,)

    (x2,) = rowwise(resid, [x1, f], [], [cm], [d], [f32], nm + "resid")
    return x2, w_next


def _assemble(name, gathered):
    if BIG[name] == 0:
        return gathered.reshape(-1, gathered.shape[-1])
    return jnp.concatenate([gathered[j] for j in range(8)], axis=1)


def _loss_fn(big0, shards0_late, stand_ins0, shards1, stand_ins1, small, x, ctx, c, target, n_ctx):
    n_lat, d = x.shape
    xall = jnp.concatenate([ctx, x], axis=0)
    ct, st = _rope_tables(n_ctx, n_lat)
    tabs = (jnp.tile(ct, (1, N_HEADS)) * HEAD_DIM ** -0.5, jnp.tile(st, (1, N_HEADS)) * HEAD_DIM ** -0.5,
            jnp.tile(ct, (1, N_KV)), jnp.tile(st, (1, N_KV)))
    srows = jnp.concatenate([_silu(small['c_ctx'])[None], _silu(c), jnp.zeros((14, d), f32)], axis=0)
    names = list(BIG)
    big = big0
    gather_a = (LATE, [shards0_late[n] for n in LATE], [stand_ins0[n] for n in LATE])
    gather_c = (names, [shards1[n] for n in names], [stand_ins1[n] for n in names])
    for li in range(2):
        cm = mm(srows, big['w_mod'], f"l{li}_mod", f32)[0:2] + small['b_mod'][li][None]
        sl = {k: v[li] for k, v in small.items() if k not in ('c_ctx', 'final_norm')}
        xall, big = _layer(xall, big, sl, cm, tabs, n_ctx, li, gather_a, gather_c)
        gather_a = gather_c = ((), (), ())
    ncb = n_ctx // ROW_TILE
    tgt = jnp.concatenate([jnp.zeros((n_ctx, d), f32), target], axis=0)

    def loss_rows(blk, xx, tg, g):
        e = _rms(xx, g) - tg
        return (jnp.where(blk < ncb, 0.0, 0.5) * jnp.mean(e * e, axis=-1, keepdims=True),)

    (rows,) = rowwise(loss_rows, [xall], [tgt], [small['final_norm'][None]], [1], [f32], "loss")
    return jnp.sum(rows)


def _hbm_call(body, ins, out_shapes, n_sems, name):
    any_spec = pl.BlockSpec(memory_space=pl.ANY)
    return pl.pallas_call(
        body, out_shape=out_shapes, in_specs=[any_spec] * len(ins), out_specs=[any_spec] * len(out_shapes),
        scratch_shapes=[pltpu.SemaphoreType.DMA((n_sems,)), pltpu.SemaphoreType.DMA((n_sems,)), pltpu.SemaphoreType.DMA((len(ins),))],
        name=name)(*ins)


def _gather_steps(x_refs, out_refs, send_sems, recv_sems, local_sems):
    n = len(x_refs)
    x, y, c = lax.axis_index("x"), lax.axis_index("y"), lax.axis_index("c")
    me, sibling = (x, y, c), (x, y, 1 - c)
    chips = [(1 - x, y), (x, 1 - y), (1 - x, 1 - y)]

    def copy(a, k, block, to, src=None):
        px, py, pc = block
        slot = out_refs[a].at[4 * px + 2 * py + pc]
        return pltpu.make_async_remote_copy(
            src_ref=slot if src is None else src, dst_ref=slot,
            send_sem=send_sems.at[7 * a + k], recv_sem=recv_sems.at[7 * a + k], device_id=to, device_id_type=MESH)

    mine = [pltpu.make_async_copy(x_refs[a], out_refs[a].at[4 * x + 2 * y + c], local_sems.at[a]) for a in range(n)]
    first = []
    for a in range(n):
        first += [copy(a, 1 + j, me, (*chip, c), src=x_refs[a]) for j, chip in enumerate(chips)]
        first.append(copy(a, 0, me, sibling, src=x_refs[a]))

    def start():
        for cp in mine + first:
            cp.start()

    def finish():
        passed = []
        for a in range(n):
            for j, chip in enumerate(chips):
                copy(a, 1 + j, (*chip, c), me).wait_recv()
                passed.append(copy(a, 4 + j, (*chip, c), sibling))
                passed[-1].start()
        for a in range(n):
            copy(a, 0, sibling, me).wait_recv()
            for j, chip in enumerate(chips):
                copy(a, 4 + j, (*chip, 1 - c), me).wait_recv()
        for cp in first + passed:
            cp.wait_send()
        for cp in mine:
            cp.wait()

    return start, finish


def _gather_scratch(n):
    return [pltpu.SemaphoreType.DMA((7 * n,)), pltpu.SemaphoreType.DMA((7 * n,)), pltpu.SemaphoreType.DMA((n,))]


def all_gather(shards, name):
    n = len(shards)

    def body(*refs):
        start, finish = _gather_steps(refs[:n], refs[n:2 * n], *refs[2 * n:])
        start()
        finish()

    return _hbm_call(body, shards, [jax.ShapeDtypeStruct((8,) + s.shape, s.dtype) for s in shards], 7 * n, name)


def rs_to_sibling(gs, name="rs_sibling"):
    n = len(gs)

    def body(*refs):
        g_refs, out_refs, (send_sems, recv_sems, _) = refs[:n], refs[n:2 * n], refs[2 * n:]
        x, y, c = lax.axis_index("x"), lax.axis_index("y"), lax.axis_index("c")
        copies = [pltpu.make_async_remote_copy(
            src_ref=g_refs[a].at[2 * k + (1 - c)], dst_ref=out_refs[a].at[k], send_sem=send_sems.at[4 * a + k],
            recv_sem=recv_sems.at[4 * a + k], device_id=(x, y, 1 - c), device_id_type=MESH) for a in range(n) for k in range(4)]
        for cp in copies:
            cp.start()
        for cp in copies:
            cp.wait()

    return _hbm_call(body, gs, [jax.ShapeDtypeStruct((4,) + g.shape[1:], g.dtype) for g in gs], 4 * n, name)


def rs_to_chips(ss):
    n = len(ss)

    def body(*refs):
        copies = _chips_copies(refs[:n], refs[n:2 * n], refs[2 * n], refs[2 * n + 1])
        for cp in copies:
            cp.start()
        for cp in copies:
            cp.wait()

    return _hbm_call(body, ss, [jax.ShapeDtypeStruct((3,) + s.shape[1:], s.dtype) for s in ss], 3 * n, "rs_chips")


def _chips_copies(s_refs, out_refs, send_sems, recv_sems):
    x, y, c = lax.axis_index("x"), lax.axis_index("y"), lax.axis_index("c")
    copies = []
    for a in range(len(s_refs)):
        for k, (fx, fy) in enumerate([(1, 0), (0, 1), (1, 1)]):
            px, py = (1 - x) if fx else x, (1 - y) if fy else y
            copies.append(pltpu.make_async_remote_copy(
                src_ref=s_refs[a].at[2 * px + py], dst_ref=out_refs[a].at[k], send_sem=send_sems.at[3 * a + k],
                recv_sem=recv_sems.at[3 * a + k], device_id=(px, py, c), device_id_type=MESH))
    return copies


def _flat_tile(rows, cols):
    return _row_tile(rows, 4 * 4 * cols)


def pair_sum(g, r1, my_c, name):
    _, rows, cols = g.shape
    tm = _flat_tile(rows, cols)

    def body(c_ref, g_ref, r_ref, o_ref):
        o_ref[...] = (g_ref[...].astype(f32) + r_ref[...].astype(f32)).astype(o_ref.dtype)

    return pl.pallas_call(
        body, grid_spec=pltpu.PrefetchScalarGridSpec(
            num_scalar_prefetch=1, grid=(4, rows // tm),
            in_specs=[pl.BlockSpec((1, tm, cols), lambda k, i, c: (2 * k + c[0], i, 0)),
                      pl.BlockSpec((1, tm, cols), lambda k, i, c: (k, i, 0))],
            out_specs=pl.BlockSpec((1, tm, cols), lambda k, i, c: (k, i, 0))),
        out_shape=jax.ShapeDtypeStruct((4, rows, cols), g.dtype), name=name,
        compiler_params=_cparams("parallel", "parallel"))(my_c, g, r1)


def _adam_math(w, g, m, v):
    m2 = ADAM_B1 * m + (1.0 - ADAM_B1) * g
    v2 = ADAM_B2 * v + (1.0 - ADAM_B2) * (g * g)
    m_hat = m2 / (1.0 - ADAM_B1 ** ADAM_STEP)
    v_hat = v2 / (1.0 - ADAM_B2 ** ADAM_STEP)
    return -ADAM_LR * (m_hat / (jnp.sqrt(v_hat) + ADAM_EPS) + ADAM_WD * w), m2, v2


def sum_adam(parts, w, m, v, name):
    groups, rows, cols = w.shape
    tm = _flat_tile(rows, cols)
    nblk = rows // tm
    flat = []
    scalars = [p[2] for ps in parts for p in ps if p[2] is not None]
    for gi, ps in enumerate(parts):
        flat.append([])
        for arr, static_rows, dyn in ps:
            if dyn is not None:
                flat[gi].append((arr, functools.partial(lambda l, i, s, gi: (s[0], jnp.where(l == gi, i, nblk - 1), 0), gi=gi)))
            else:
                for k in static_rows:
                    flat[gi].append((arr, functools.partial(lambda l, i, s, gi, k: (k, jnp.where(l == gi, i, nblk - 1), 0), gi=gi, k=k)))
    counts = [len(f) for f in flat]
    na = sum(counts)

    def body(s_ref, *refs):
        sums, at = [], 0
        for cnt in counts:
            g = refs[at][0].astype(f32)
            for r in refs[at + 1:at + cnt]:
                g = g + r[0].astype(f32)
            sums.append(g)
            at += cnt
        g = sums[0]
        for gi in range(1, groups):
            g = jnp.where(pl.program_id(0) == gi, sums[gi], g)
        w_ref, m_ref, v_ref = refs[na:na + 3]
        g_out, d_out, m_out, v_out = refs[na + 3:]
        d, m2, v2 = _adam_math(w_ref[0], g, m_ref[0], v_ref[0])
        g_out[0] = g
        d_out[0] = d
        m_out[0] = m2
        v_out[0] = v2

    blk = pl.BlockSpec((1, tm, cols), lambda l, i, s: (l, i, 0))
    scalar = scalars[0] if scalars else jnp.zeros((1,), jnp.int32)
    return pl.pallas_call(
        body, grid_spec=pltpu.PrefetchScalarGridSpec(
            num_scalar_prefetch=1, grid=(groups, nblk),
            in_specs=[pl.BlockSpec((1, tm, cols), im) for f in flat for _, im in f] + [blk, blk, blk],
            out_specs=[blk, blk, blk, blk]),
        out_shape=[jax.ShapeDtypeStruct((groups, rows, cols), f32)] * 4, name=name,
        compiler_params=_cparams("arbitrary", "arbitrary"))(scalar, *[a for f in flat for a, _ in f], w, m, v)


FLAT_COLS = 1024


def _to_flat(vec):
    n = vec.shape[0]
    total = -(-n // (8 * FLAT_COLS)) * 8 * FLAT_COLS
    return jnp.concatenate([vec, jnp.zeros((total - n,), vec.dtype)]).reshape(-1, FLAT_COLS)


def _pack(tree, names):
    return jnp.concatenate([tree[n].reshape(-1) for n in names])


def _unpack(vec, like, names):
    out, off = {}, 0
    for n in names:
        size = like[n].size
        out[n] = vec[off:off + size].reshape(like[n].shape)
        off += size
    return out


def kernel(x, c, ctx, c_ctx, w_mod, b_mod, norm1, norm2, w_in, a_sink, ssm_conv_w, ssm_conv_b, ssm_A_log, ssm_dt_bias, ssm_D, ssm_norm, c_q_norm, c_k_norm, w_oa, w_ob, w_oc, w_out, ffn_w_up, ffn_w_gate, ffn_conv_w, ffn_conv_b, ffn_w_down, final_norm, loss_target, m_c_ctx, m_w_mod, m_b_mod, m_norm1, m_norm2, m_w_in, m_a_sink, m_ssm_conv_w, m_ssm_conv_b, m_ssm_A_log, m_ssm_dt_bias, m_ssm_D, m_ssm_norm, m_c_q_norm, m_c_k_norm, m_w_oa, m_w_ob, m_w_oc, m_w_out, m_ffn_w_up, m_ffn_w_gate, m_ffn_conv_w, m_ffn_conv_b, m_ffn_w_down, m_final_norm, v_c_ctx, v_w_mod, v_b_mod, v_norm1, v_norm2, v_w_in, v_a_sink, v_ssm_conv_w, v_ssm_conv_b, v_ssm_A_log, v_ssm_dt_bias, v_ssm_D, v_ssm_norm, v_c_q_norm, v_c_k_norm, v_w_oa, v_w_ob, v_w_oc, v_w_out, v_ffn_w_up, v_ffn_w_gate, v_ffn_conv_w, v_ffn_conv_b, v_ffn_w_down, v_final_norm):
    args = (x, c, ctx, c_ctx, w_mod, b_mod, norm1, norm2, w_in, a_sink, ssm_conv_w, ssm_conv_b, ssm_A_log, ssm_dt_bias, ssm_D, ssm_norm, c_q_norm, c_k_norm, w_oa, w_ob, w_oc, w_out, ffn_w_up, ffn_w_gate, ffn_conv_w, ffn_conv_b, ffn_w_down, final_norm, loss_target)
    moms = (m_c_ctx, m_w_mod, m_b_mod, m_norm1, m_norm2, m_w_in, m_a_sink, m_ssm_conv_w, m_ssm_conv_b, m_ssm_A_log, m_ssm_dt_bias, m_ssm_D, m_ssm_norm, m_c_q_norm, m_c_k_norm, m_w_oa, m_w_ob, m_w_oc, m_w_out, m_ffn_w_up, m_ffn_w_gate, m_ffn_conv_w, m_ffn_conv_b, m_ffn_w_down, m_final_norm)
    vars_ = (v_c_ctx, v_w_mod, v_b_mod, v_norm1, v_norm2, v_w_in, v_a_sink, v_ssm_conv_w, v_ssm_conv_b, v_ssm_A_log, v_ssm_dt_bias, v_ssm_D, v_ssm_norm, v_c_q_norm, v_c_k_norm, v_w_oa, v_w_ob, v_w_oc, v_w_out, v_ffn_w_up, v_ffn_w_gate, v_ffn_conv_w, v_ffn_conv_b, v_ffn_w_down, v_final_norm)
    p = dict(zip(IN_NAMES, args))
    mom = dict(zip(WEIGHTS, moms))
    var = dict(zip(WEIGHTS, vars_))
    depth = w_in.shape[0]
    n_ctx = ctx.shape[1]
    xi, yi, ci = lax.axis_index("x"), lax.axis_index("y"), lax.axis_index("c")
    dev = 4 * xi + 2 * yi + ci
    big_names = list(BIG)

    assert depth == 2 and n_ctx == ROW_TILE
    shards = [{n: p[n][li].astype(bf16) for n in big_names} for li in range(depth)]
    g_early = all_gather([shards[0][n] for n in EARLY], "gather_l0")
    g_conv = all_gather([_to_flat(_pack(p, CONV_W))], "gather_conv")[0].reshape(8, -1)
    big0 = {n: _assemble(n, g) for n, g in zip(EARLY, g_early)}
    stand_ins = [{n: jnp.zeros((8,) + shards[li][n].shape, bf16) for n in (LATE, big_names)[li]} for li in range(depth)]
    conv_full, off = {}, 0
    for n in CONV_W:
        shp = p[n].shape
        seg = g_conv[:, off:off + p[n].size].reshape(8, *shp)
        conv_full[n] = jnp.moveaxis(seg, 0, -2).reshape(*shp[:-1], 8 * shp[-1])
        off += p[n].size
    small = {n: p[n] for n in REPL}
    small.update(conv_full)

    loss, (g_early, g_late0, g_big1, g_small, g_x) = jax.value_and_grad(_loss_fn, argnums=(0, 2, 4, 5, 6))(
        big0, {n: shards[0][n] for n in LATE}, stand_ins[0], shards[1], stand_ins[1], small, x[0], ctx[0], c, loss_target[0], n_ctx)
    loss = lax.psum(loss, AXES)

    def send_rows(n):
        b = p[n].shape[-1]
        return jnp.stack([g_early[n][:, b * j:b * (j + 1)] for j in range(8)])

    send = [send_rows(n) for n in EARLY]
    from_sibling = rs_to_sibling(send)
    my_c = ci.reshape(1).astype(jnp.int32)
    side_sum = [pair_sum(s, r, my_c, "rs_pair_sum_" + n) for n, s, r in zip(EARLY, send, from_sibling)]
    from_chips = rs_to_chips(side_sum)
    chip = (2 * xi + yi).reshape(1).astype(jnp.int32)
    big_out = [{}, {}, {}, {}]
    for n in big_names:
        if n in EARLY:
            a = EARLY.index(n)
            parts = [[(side_sum[a], None, chip), (from_chips[a], (0, 1, 2), None)]]
        else:
            parts = [[(g_late0[n], None, chip), (g_late0[n], (4, 5, 6), None)]]
        parts.append([(g_big1[n], None, chip), (g_big1[n], (4, 5, 6), None)])
        outs = sum_adam(parts, p[n], mom[n], var[n], "adam_" + n)
        for k in range(4):
            big_out[k][n] = outs[k]

    sm_names = REPL + list(CONV_W)
    g_vec = _to_flat(_pack(g_small, sm_names))
    gathered = all_gather([g_vec], "gather_small_grads")[0]
    n_repl = sum(p[n].size for n in REPL)

    def repl_flat(tree):
        return _to_flat(jnp.concatenate([_pack(tree, REPL), jnp.zeros((g_vec.size - n_repl,), f32)]))

    outs_small = sum_adam([[(gathered, tuple(range(8)), None)]], repl_flat(p)[None], repl_flat(mom)[None], repl_flat(var)[None],
                          "adam_small")
    g_sum = outs_small[0].reshape(-1)
    small_out = [_unpack(o.reshape(-1), p, REPL) for o in outs_small]
    conv_g_full = _unpack(g_sum[n_repl:], conv_full, CONV_W)
    conv_g = {n: lax.dynamic_slice_in_dim(conv_g_full[n], dev * p[n].shape[-1], p[n].shape[-1], axis=2) for n in CONV_W}
    conv_gv = _to_flat(_pack(conv_g, CONV_W))
    outs_conv = sum_adam([[(conv_gv[None], (0,), None)]], _to_flat(_pack(p, CONV_W))[None], _to_flat(_pack(mom, CONV_W))[None],
                         _to_flat(_pack(var, CONV_W))[None], "adam_conv")
    conv_out = [_unpack(o.reshape(-1), p, CONV_W) for o in outs_conv]

    res = []
    for k in range(4):
        tree = {**big_out[k], **small_out[k], **conv_out[k]}
        res.append([tree[n] for n in WEIGHTS])
    return (loss, g_x[None], *res[0], *res[1], *res[2], *res[3])
```

```python
import functools

import jax
import jax.numpy as jnp
from jax import lax
from jax.experimental import pallas as pl
from jax.experimental.pallas import tpu as pltpu

f32 = jnp.float32
bf16 = jnp.bfloat16
MESH = pl.DeviceIdType.MESH
AXES = ("x", "y", "c")

GRID_W = 64
HEAD_DIM = 64
ROPE_BASE = 10000.0
EPS = 1e-6
WINDOW = 128
N_HEADS = 8
N_KV = 2
SSM_HEADS = 16
SSM_P = 64
SSM_G = 2
SSM_N = 128
SSM_INNER = SSM_HEADS * SSM_P
SSM_BC = SSM_G * SSM_N
SSM_Q = 128
Q_W = N_HEADS * HEAD_DIM
KV_W = N_KV * HEAD_DIM
DT_W = 2 * SSM_HEADS
DT_PAD = 128
ADAM_LR, ADAM_B1, ADAM_B2, ADAM_EPS, ADAM_WD, ADAM_STEP = 0.001, 0.9, 0.999, 1e-08, 0.01, 10

LANES = 128
ROW_TILE = 256
VMEM_BLOCK_BUDGET = 6 * 1024 * 1024
ATTN_SLAB = 128
MM_ROW_CAP = 1088
MM_TILE_CAP = 1536
NEG = -1e30

IN_NAMES = ['x', 'c', 'ctx', 'c_ctx', 'w_mod', 'b_mod', 'norm1', 'norm2', 'w_in', 'a_sink', 'ssm_conv_w', 'ssm_conv_b', 'ssm_A_log', 'ssm_dt_bias', 'ssm_D', 'ssm_norm', 'c_q_norm', 'c_k_norm', 'w_oa', 'w_ob', 'w_oc', 'w_out', 'ffn_w_up', 'ffn_w_gate', 'ffn_conv_w', 'ffn_conv_b', 'ffn_w_down', 'final_norm', 'loss_target']
WEIGHTS = IN_NAMES[3:28]
BIG = {'w_mod': 1, 'w_in': 1, 'w_oa': 1, 'w_ob': 0, 'w_oc': 1, 'w_out': 0, 'ffn_w_up': 1, 'ffn_w_gate': 1, 'ffn_w_down': 0}
EARLY = ['w_mod', 'w_in']
LATE = [n for n in BIG if n not in EARLY]
CONV_W = ('ssm_conv_w', 'ffn_conv_w')
REPL = [n for n in WEIGHTS if n not in BIG and n not in CONV_W]

NT = (((1,), (1,)), ((), ()))
TN = (((0,), (0,)), ((), ()))
NN = (((1,), (0,)), ((), ()))


def _cparams(*sem):
    return pltpu.CompilerParams(dimension_semantics=sem)


def _div_tile(n, unit, cap):
    for d in range(min(n, int(cap)), 0, -1):
        if n % d == 0 and d % unit == 0:
            return d
    return n


def _row_tile(m, row_bytes):
    return _div_tile(m, 16, max(16, VMEM_BLOCK_BUDGET // row_bytes))


def _mm_call(a, b, mode, out_dtype, name):
    if mode == "nn":
        (m, k), n = a.shape, b.shape[1]
    elif mode == "nt":
        (m, k), n = a.shape, b.shape[0]
    else:
        (k, m), n = a.shape, b.shape[1]
    dims = {"nn": NN, "nt": NT, "tn": TN}[mode]
    ia, ib = a.dtype.itemsize, b.dtype.itemsize
    tm = _div_tile(m, LANES, MM_TILE_CAP) if mode == "tn" else _div_tile(m, 16, MM_ROW_CAP)
    tn = _div_tile(n, LANES, min(MM_TILE_CAP, VMEM_BLOCK_BUDGET // (4 * tm)))
    tk = _div_tile(k, 16 if mode == "tn" else LANES,
                   min(MM_ROW_CAP if mode == "tn" else MM_TILE_CAP, VMEM_BLOCK_BUDGET // (tm * ia), VMEM_BLOCK_BUDGET // (tn * ib)))
    nk = k // tk

    def body(a_ref, b_ref, o_ref, *acc):
        part = lax.dot_general(a_ref[...].astype(bf16), b_ref[...].astype(bf16), dims, preferred_element_type=f32)
        if nk == 1:
            o_ref[...] = part.astype(o_ref.dtype)
            return
        kk = pl.program_id(2)

        @pl.when(kk == 0)
        def _():
            acc[0][...] = part

        @pl.when(kk > 0)
        def _():
            acc[0][...] += part

        @pl.when(kk == nk - 1)
        def _():
            o_ref[...] = acc[0][...].astype(o_ref.dtype)

    a_spec = pl.BlockSpec((tk, tm), lambda i, j, kk: (kk, i)) if mode == "tn" else pl.BlockSpec((tm, tk), lambda i, j, kk: (i, kk))
    b_spec = pl.BlockSpec((tn, tk), lambda i, j, kk: (j, kk)) if mode == "nt" else pl.BlockSpec((tk, tn), lambda i, j, kk: (kk, j))
    return pl.pallas_call(
        body, grid=(m // tm, n // tn, nk), in_specs=[a_spec, b_spec],
        out_specs=pl.BlockSpec((tm, tn), lambda i, j, kk: (i, j)),
        out_shape=jax.ShapeDtypeStruct((m, n), out_dtype),
        scratch_shapes=[pltpu.VMEM((tm, tn), f32)] if nk > 1 else [], name=name,
        compiler_params=_cparams("parallel", "parallel", "arbitrary"))(a, b)


def mm(a, b, name, out_dtype=None):
    @jax.custom_vjp
    def op(a, b):
        return _mm_call(a, b, "nn", out_dtype or bf16, name)

    def fwd(a, b):
        return op(a, b), (a, b)

    def bwd(res, g):
        a, b = res
        return _mm_call(g, b, "nt", a.dtype, name + "_da"), _mm_call(a, g, "tn", b.dtype, name + "_db")

    op.defvjp(fwd, bwd)
    return op(a, b)


def split_cols(u, widths):
    offs = [0]
    for w in widths:
        offs.append(offs[-1] + w)

    @jax.custom_vjp
    def op(u):
        return tuple(u[:, offs[i]:offs[i + 1]] for i in range(len(widths)))

    def fwd(u):
        return op(u), None

    def bwd(_, cts):
        return (jnp.concatenate(cts, axis=1),)

    op.defvjp(fwd, bwd)
    return op(u)


def rowwise(fn, rows, consts, pars, out_widths, out_dtypes, name):
    t = rows[0].shape[0]
    tm = ROW_TILE
    nb = t // tm
    nr, nc, npar = len(rows), len(consts), len(pars)

    def rspec(a):
        return pl.BlockSpec((tm, a.shape[1]), lambda i: (i, 0))

    def pspec(a):
        return pl.BlockSpec(a.shape, lambda i: (0,) * a.ndim)

    def call_fwd(rows, consts, pars):
        def body(*refs):
            blk = pl.program_id(0)
            ins = [r[...].astype(f32) for r in refs[:nr + nc]]
            ps = [r[...] for r in refs[nr + nc:nr + nc + npar]]
            outs = fn(blk, *ins, *ps)
            for o_ref, o in zip(refs[nr + nc + npar:], outs):
                o_ref[...] = o.astype(o_ref.dtype)

        return pl.pallas_call(
            body, grid=(nb,),
            in_specs=[rspec(a) for a in rows + consts] + [pspec(a) for a in pars],
            out_specs=[pl.BlockSpec((tm, w), lambda i: (i, 0)) for w in out_widths],
            out_shape=[jax.ShapeDtypeStruct((t, w), d) for w, d in zip(out_widths, out_dtypes)],
            name=name, compiler_params=_cparams("parallel"))(*rows, *consts, *pars)

    def call_bwd(rows, consts, pars, cts):
        nout = len(cts)

        def body(*refs):
            blk = pl.program_id(0)
            ins = [r[...].astype(f32) for r in refs[:nr]]
            cs = [r[...].astype(f32) for r in refs[nr:nr + nc]]
            ps = [r[...] for r in refs[nr + nc:nr + nc + npar]]
            dys = [r[...].astype(f32) for r in refs[nr + nc + npar:nr + nc + npar + nout]]
            d_refs = refs[nr + nc + npar + nout:]
            _, vjp = jax.vjp(lambda *a: tuple(fn(blk, *a[:nr], *cs, *a[nr:])), *ins, *ps)
            grads = vjp(tuple(dys))
            for d_ref, g in zip(d_refs[:nr], grads[:nr]):
                d_ref[...] = g.astype(d_ref.dtype)
            if npar:
                @pl.when(blk == 0)
                def _():
                    for d_ref in d_refs[nr:]:
                        d_ref[...] = jnp.zeros_like(d_ref)

                for d_ref, g in zip(d_refs[nr:], grads[nr:]):
                    d_ref[...] += g

        return pl.pallas_call(
            body, grid=(nb,),
            in_specs=[rspec(a) for a in rows + consts] + [pspec(a) for a in pars] + [rspec(a) for a in cts],
            out_specs=[rspec(a) for a in rows] + [pspec(a) for a in pars],
            out_shape=[jax.ShapeDtypeStruct(a.shape, a.dtype) for a in rows + pars],
            name=name + "_bwd", compiler_params=_cparams("arbitrary"))(*rows, *consts, *pars, *cts)

    @jax.custom_vjp
    def op(rows, consts, pars):
        return tuple(call_fwd(list(rows), list(consts), list(pars)))

    def fwd(rows, consts, pars):
        return op(rows, consts, pars), (rows, consts, pars)

    def bwd(res, cts):
        rows, consts, pars = res
        g = call_bwd(list(rows), list(consts), list(pars), list(cts))
        return tuple(g[:nr]), tuple(jnp.zeros_like(a) for a in consts), tuple(g[nr:])

    op.defvjp(fwd, bwd)
    return op(tuple(rows), tuple(consts), tuple(pars))


def colwise(fn, cols, pars, out_dtype, name):
    t, w = cols[0].shape
    tc = LANES
    nb = w // tc
    ncol, npar = len(cols), len(pars)

    def cspec(a):
        return pl.BlockSpec((a.shape[0], tc), lambda j: (0, j))

    def call_fwd(cols, pars):
        def body(*refs):
            ins = [r[...].astype(f32) for r in refs[:ncol]]
            ps = [r[...] for r in refs[ncol:ncol + npar]]
            refs[-1][...] = fn(*ins, *ps).astype(refs[-1].dtype)

        return pl.pallas_call(
            body, grid=(nb,), in_specs=[cspec(a) for a in cols + pars], out_specs=cspec(cols[0]),
            out_shape=jax.ShapeDtypeStruct((t, w), out_dtype), name=name, compiler_params=_cparams("parallel"))(*cols, *pars)

    def call_bwd(cols, pars, ct):
        def body(*refs):
            ins = [r[...].astype(f32) for r in refs[:ncol]]
            ps = [r[...] for r in refs[ncol:ncol + npar]]
            dy = refs[ncol + npar][...].astype(f32)
            d_refs = refs[ncol + npar + 1:]
            _, vjp = jax.vjp(fn, *ins, *ps)
            grads = vjp(dy)
            for d_ref, g in zip(d_refs, grads):
                d_ref[...] = g.astype(d_ref.dtype)

        return pl.pallas_call(
            body, grid=(nb,), in_specs=[cspec(a) for a in cols + pars + [ct]],
            out_specs=[cspec(a) for a in cols + pars],
            out_shape=[jax.ShapeDtypeStruct(a.shape, a.dtype) for a in cols + pars],
            name=name + "_bwd", compiler_params=_cparams("parallel"))(*cols, *pars, ct)

    @jax.custom_vjp
    def op(cols, pars):
        return call_fwd(list(cols), list(pars))

    def fwd(cols, pars):
        return op(cols, pars), (cols, pars)

    def bwd(res, ct):
        cols, pars = res
        g = call_bwd(list(cols), list(pars), ct)
        return tuple(g[:ncol]), tuple(g[ncol:])

    op.defvjp(fwd, bwd)
    return op(tuple(cols), tuple(pars))


def _sigmoid(x):
    return 1.0 / (1.0 + jnp.exp(-x))


def _silu(x):
    return x * _sigmoid(x)


def _rms(x, g):
    return x * lax.rsqrt(jnp.mean(x * x, axis=-1, keepdims=True) + EPS) * g


def _shift_rows(u, k, n_ctx):
    @jax.custom_vjp
    def op(u):
        t = u.shape[0]
        row = lax.broadcasted_iota(jnp.int32, u.shape, 0)
        edge = ((row == 0) | (row == n_ctx)) if k == 1 else ((row == n_ctx - 1) | (row == t - 1))
        return jnp.where(edge, 0.0, pltpu.roll(u, k % t, 0))

    op.defvjp(lambda u: (op(u), None), lambda _, g: (_shift_rows(g, -k, n_ctx),))
    return op(u)


def _dwconv(u, w0, w1, w2, b, n_ctx):
    return w0 * _shift_rows(u, 1, n_ctx) + w1 * u + w2 * _shift_rows(u, -1, n_ctx) + b


@jax.custom_vjp
def _swap_pairs(x):
    w = x.shape[1]
    lane = lax.broadcasted_iota(jnp.int32, x.shape, 1)
    return jnp.where(lane % 2 == 0, pltpu.roll(x, w - 1, 1), pltpu.roll(x, 1, 1))


_swap_pairs.defvjp(lambda x: (_swap_pairs(x), None), lambda _, g: (_swap_pairs(g),))


def _head_rms(x, g):
    w = x.shape[1]
    same = (lax.broadcasted_iota(jnp.int32, (w, w), 0) // HEAD_DIM) == (lax.broadcasted_iota(jnp.int32, (w, w), 1) // HEAD_DIM)
    ms = jnp.dot(x * x, same.astype(f32), precision=lax.Precision.HIGHEST, preferred_element_type=f32) * (1.0 / HEAD_DIM)
    return x * lax.rsqrt(ms + EPS) * g


def _band_ok(i, j, c0, shape, tq, tk):
    kpos = j * tk + lax.broadcasted_iota(jnp.int32, shape, 0)
    qpos = i * tq + (c0 + lax.broadcasted_iota(jnp.int32, shape, 1)) % tq
    return jnp.abs(qpos - kpos) <= WINDOW


def _kv_range(i, nb, window):
    is_ctx = i == 0
    if window:
        return jnp.where(is_ctx, 1, jnp.maximum(i - 1, 1)), jnp.where(is_ctx, 1, jnp.minimum(i + 2, nb))
    return 1, jnp.where(is_ctx, 1, nb)


def _sink_row(sink_ref, g, r, tq):
    return jnp.concatenate([jnp.full((1, tq), sink_ref[g * r + h], f32) for h in range(r)], axis=1)


def _heads_to_rows(x, r):
    return jnp.concatenate([x[:, HEAD_DIM * h:HEAD_DIM * (h + 1)] for h in range(r)], axis=0)


def _cols_to_heads(xt, r, tq):
    return jnp.concatenate([xt[:, tq * h:tq * (h + 1)].T for h in range(r)], axis=1)


def _attn_fwd_call(q, k, v, sink, window, name, shards=()):
    nkv, t, dh = k.shape
    h = q.shape[1] // dh
    r = h // nkv
    tq = tk = ROW_TILE
    nb = t // tq
    rows = r * tq
    ns = len(shards)

    assert window or nb % 2 == 1, "the dense schedule takes the kv chunks after the context chunk in pairs"

    def body(sink_ref, q_ref, k_ref, v_ref, *rest):
        x_refs, (o_ref, lse_ref), gathered_refs = rest[:ns], rest[ns:ns + 2], rest[ns + 2:2 * ns + 2]
        m_scr, l_scr, acc_scr, s_a, s_b, p_a, p_b, a_a, a_b = rest[2 * ns + 2:2 * ns + 11]
        comm_sems = rest[2 * ns + 11:]
        g, i = pl.program_id(0), pl.program_id(1)
        if ns:
            @pl.when((g == 0) & (i == 0))
            def _():
                _gather_steps(x_refs, gathered_refs, *comm_sems)[0]()

        qv = _heads_to_rows(q_ref[...], r)
        m_scr[...] = jnp.full_like(m_scr, NEG)
        l_scr[...] = jnp.zeros_like(l_scr)
        acc_scr[...] = jnp.zeros_like(acc_scr)

        def kv_rows(j):
            return pl.ds(pl.multiple_of(jnp.minimum(j, nb - 1) * tk, tk), tk)

        def scores(j, s_scr):
            s_scr[...] = lax.dot_general(k_ref[0, kv_rows(j), :], qv, NT, preferred_element_type=f32)

        def softmax(j, s_scr, p_scr, a_scr, masked):
            for cb in range(rows // ATTN_SLAB):
                cs = slice(cb * ATTN_SLAB, (cb + 1) * ATTN_SLAB)
                s = s_scr[:, cs]
                if masked:
                    s = jnp.where(_band_ok(i, j, cb * ATTN_SLAB, s.shape, tq, tk), s, NEG)
                m = m_scr[:, cs]
                m2 = jnp.maximum(m, jnp.max(s, axis=0, keepdims=True))
                p = jnp.exp(s - m2)
                a = jnp.exp(m - m2)
                l_scr[:, cs] = a * l_scr[:, cs] + jnp.sum(p, axis=0, keepdims=True)
                m_scr[:, cs] = m2
                a_scr[:, cs] = a
                p_scr[:, cs] = p.astype(bf16)

        def weighted_v(j, p_scr, a_scr):
            acc_scr[...] = a_scr[...] * acc_scr[...] + lax.dot_general(v_ref[0, kv_rows(j), :], p_scr[...], TN, preferred_element_type=f32)

        scores(0, s_a)
        softmax(0, s_a, p_a, a_a, False)
        if window:
            weighted_v(0, p_a, a_a)
            lo, hi = _kv_range(i, nb, window)

            def chunk(j, c):
                scores(j, s_a)
                softmax(j, s_a, p_a, a_a, True)
                weighted_v(j, p_a, a_a)
                return c

            lax.fori_loop(lo, hi, chunk, 0)
        else:
            scores(1, s_b)

            def pair(tt, c):
                j0 = 2 * tt + 1
                scores(j0 + 1, s_a)
                weighted_v(j0 - 1, p_a, a_a)
                softmax(j0, s_b, p_b, a_b, False)
                scores(j0 + 2, s_b)
                weighted_v(j0, p_b, a_b)
                softmax(j0 + 1, s_a, p_a, a_a, False)
                return c

            lax.fori_loop(0, jnp.where(i == 0, 0, (nb - 1) // 2), pair, 0)
            weighted_v(jnp.where(i == 0, 0, nb - 1), p_a, a_a)
        m, l, acc = m_scr[...], l_scr[...], acc_scr[...]
        if window:
            sk = _sink_row(sink_ref, g, r, tq)
            m2 = jnp.maximum(m, sk)
            a = jnp.exp(m - m2)
            l = a * l + jnp.exp(sk - m2)
            acc = a * acc
            m = m2
        o_ref[...] = _cols_to_heads(acc / l, r, tq).astype(o_ref.dtype)
        lse_ref[0] = m + jnp.log(l)
        if ns:
            @pl.when((g == nkv - 1) & (i == nb - 1))
            def _():
                _gather_steps(x_refs, gathered_refs, *comm_sems)[1]()

    qspec = pl.BlockSpec((tq, r * dh), lambda g, i: (i, g))
    kspec = pl.BlockSpec((1, t, dh), lambda g, i: (g, 0, 0))
    hbm = pl.BlockSpec(memory_space=pl.ANY)
    sem = ("arbitrary", "arbitrary") if ns else ("parallel", "parallel")
    return pl.pallas_call(
        body, grid=(nkv, nb),
        in_specs=[pl.BlockSpec(memory_space=pltpu.SMEM), qspec, kspec, kspec] + [hbm] * ns,
        out_specs=[qspec, pl.BlockSpec((1, 1, rows), lambda g, i: (g * nb + i, 0, 0))] + [hbm] * ns,
        out_shape=[jax.ShapeDtypeStruct((t, h * dh), bf16), jax.ShapeDtypeStruct((nkv * nb, 1, rows), f32)]
        + [jax.ShapeDtypeStruct((8,) + s.shape, s.dtype) for s in shards],
        scratch_shapes=[pltpu.VMEM((1, rows), f32), pltpu.VMEM((1, rows), f32), pltpu.VMEM((dh, rows), f32),
                        pltpu.VMEM((tk, rows), f32), pltpu.VMEM((tk, rows), f32), pltpu.VMEM((tk, rows), bf16),
                        pltpu.VMEM((tk, rows), bf16), pltpu.VMEM((1, rows), f32), pltpu.VMEM((1, rows), f32)]
        + (_gather_scratch(ns) if ns else []),
        name=name, compiler_params=_cparams(*sem))(sink, q, k, v, *shards)


def _attn_bwd_call(q, k, v, sink, o, lse, do, window, name, side_sums=()):
    nkv, t, dh = k.shape
    h = q.shape[1] // dh
    r = h // nkv
    tq = tk = ROW_TILE
    nb = t // tq
    rows = r * tq
    ns = len(side_sums)

    def body(sink_ref, q_ref, k_ref, v_ref, o_ref, lse_ref, do_ref, *rest):
        ss_refs, (dq_ref, dk_ref, dv_ref, dsink_ref), got_refs = rest[:ns], rest[ns:ns + 4], rest[ns + 4:2 * ns + 4]
        p_a, p_b, ds_a, ds_b, dq_scr = rest[2 * ns + 4:2 * ns + 9]
        comm_sems = rest[2 * ns + 9:]
        g, i = pl.program_id(0), pl.program_id(1)
        if ns:
            @pl.when((g == 0) & (i == 0))
            def _():
                for cp in _chips_copies(ss_refs, got_refs, *comm_sems):
                    cp.start()

        @pl.when(i == 0)
        def _():
            dk_ref[...] = jnp.zeros_like(dk_ref)
            dv_ref[...] = jnp.zeros_like(dv_ref)

        qv = _heads_to_rows(q_ref[...], r)
        dov = _heads_to_rows(do_ref[...], r)
        lse_t = lse_ref[0]
        delta_t = jnp.sum((dov.astype(f32) * _heads_to_rows(o_ref[...], r).astype(f32)).T, axis=0, keepdims=True)
        dq_scr[...] = jnp.zeros_like(dq_scr)
        q_t, do_t = qv.T, dov.T

        def kv_rows(j):
            return pl.ds(pl.multiple_of(jnp.minimum(j, nb - 1) * tk, tk), tk)

        def grads(j, p_scr, ds_scr):
            dv_ref[0, :, kv_rows(j)] += lax.dot_general(do_t, p_scr[...], NT, preferred_element_type=f32)
            dk_ref[0, :, kv_rows(j)] += lax.dot_general(q_t, ds_scr[...], NT, preferred_element_type=f32)
            dq_scr[...] += lax.dot_general(k_ref[0, kv_rows(j), :], ds_scr[...], TN, preferred_element_type=f32)

        def probs(j, p_scr, ds_scr, masked, before=None):
            s_all = lax.dot_general(k_ref[0, kv_rows(j), :], qv, NT, preferred_element_type=f32)
            dp_all = lax.dot_general(v_ref[0, kv_rows(j), :], dov, NT, preferred_element_type=f32)
            if before is not None:
                grads(*before)
            for cb in range(rows // ATTN_SLAB):
                cs = slice(cb * ATTN_SLAB, (cb + 1) * ATTN_SLAB)
                s = s_all[:, cs]
                if masked:
                    s = jnp.where(_band_ok(i, j, cb * ATTN_SLAB, s.shape, tq, tk), s, NEG)
                p = jnp.exp(s - lse_t[:, cs])
                p_scr[:, cs] = p.astype(bf16)
                ds_scr[:, cs] = (p * (dp_all[:, cs] - delta_t[:, cs])).astype(bf16)

        probs(0, p_a, ds_a, False)
        if window:
            grads(0, p_a, ds_a)
            lo, hi = _kv_range(i, nb, window)

            def chunk(j, c):
                probs(j, p_a, ds_a, True)
                grads(j, p_a, ds_a)
                return c

            lax.fori_loop(lo, hi, chunk, 0)
        else:
            def pair(tt, c):
                j0 = 2 * tt + 1
                probs(j0, p_b, ds_b, False, before=(j0 - 1, p_a, ds_a))
                probs(j0 + 1, p_a, ds_a, False, before=(j0, p_b, ds_b))
                return c

            lax.fori_loop(0, jnp.where(i == 0, 0, (nb - 1) // 2), pair, 0)
            grads(jnp.where(i == 0, 0, nb - 1), p_a, ds_a)
        dq_ref[...] = _cols_to_heads(dq_scr[...], r, tq).astype(dq_ref.dtype)
        if window:
            dsink_ref[0] = -jnp.exp(_sink_row(sink_ref, g, r, tq) - lse_t) * delta_t
        else:
            dsink_ref[...] = jnp.zeros_like(dsink_ref)
        if ns:
            @pl.when((g == nkv - 1) & (i == nb - 1))
            def _():
                for cp in _chips_copies(ss_refs, got_refs, *comm_sems):
                    cp.wait()

    qspec = pl.BlockSpec((tq, r * dh), lambda g, i: (i, g))
    cspec = pl.BlockSpec((1, 1, rows), lambda g, i: (g * nb + i, 0, 0))
    kspec = pl.BlockSpec((1, t, dh), lambda g, i: (g, 0, 0))
    ktspec = pl.BlockSpec((1, dh, t), lambda g, i: (g, 0, 0))
    hbm = pl.BlockSpec(memory_space=pl.ANY)
    return pl.pallas_call(
        body, grid=(nkv, nb),
        in_specs=[pl.BlockSpec(memory_space=pltpu.SMEM), qspec, kspec, kspec, qspec, cspec, qspec] + [hbm] * ns,
        out_specs=[qspec, ktspec, ktspec, cspec] + [hbm] * ns,
        out_shape=[jax.ShapeDtypeStruct((t, h * dh), bf16), jax.ShapeDtypeStruct((nkv, dh, t), f32), jax.ShapeDtypeStruct((nkv, dh, t), f32),
                   jax.ShapeDtypeStruct((nkv * nb, 1, rows), f32)] + [jax.ShapeDtypeStruct((3,) + s.shape[1:], s.dtype) for s in side_sums],
        scratch_shapes=[pltpu.VMEM((tk, rows), bf16)] * 4 + [pltpu.VMEM((dh, rows), f32)]
        + ([pltpu.SemaphoreType.DMA((3 * ns,)), pltpu.SemaphoreType.DMA((3 * ns,))] if ns else []),
        name=name, compiler_params=_cparams("arbitrary" if ns else "parallel", "arbitrary"))(sink, q, k, v, o, lse, do, *side_sums)


def attention(q, k, v, sink, window, name, shards=(), stand_ins=()):
    @jax.custom_vjp
    def op(q, k, v, sink, shards, stand_ins):
        o, _, *gathered = _attn_fwd_call(q, k, v, sink, window, name, shards)
        return o, tuple(gathered)

    def fwd(q, k, v, sink, shards, stand_ins):
        o, lse, *gathered = _attn_fwd_call(q, k, v, sink, window, name, shards)
        return (o, tuple(gathered)), (q, k, v, sink, o, lse, shards)

    def bwd(res, cts):
        q, k, v, sink, o, lse, shards = res
        do, d_gathered = cts
        side_sums = []
        if shards:
            my_c = lax.axis_index("c").reshape(1).astype(jnp.int32)
            from_sibling = rs_to_sibling(list(d_gathered), name + "_rs_sibling")
            side_sums = [pair_sum(s, rr, my_c, f"{name}_pair_sum{a}") for a, (s, rr) in enumerate(zip(d_gathered, from_sibling))]
        dq, dk, dv, dsink_rows, *from_chips = _attn_bwd_call(q, k, v, sink, o, lse, do, window, name + "_bwd", side_sums)
        nkv, r = k.shape[0], q.shape[1] // (k.shape[0] * k.shape[2])
        dsink = jnp.sum(dsink_rows.reshape(nkv, -1, r, ROW_TILE), axis=(1, 3)).reshape(nkv * r)
        reduced = tuple(jnp.concatenate([s, fc, jnp.zeros_like(s[:1])], axis=0) for s, fc in zip(side_sums, from_chips))
        dk, dv = dk.transpose(0, 2, 1).astype(k.dtype), dv.transpose(0, 2, 1).astype(v.dtype)
        return dq, dk, dv, dsink, tuple(jnp.zeros_like(s) for s in shards), reduced

    op.defvjp(fwd, bwd)
    return op(q, k, v, sink, tuple(shards), tuple(stand_ins))


def _ssd_chunk(xs, dtx, dtr, ac, bs, cs, hin, rev):
    q = xs[0].shape[0]
    ii = lax.broadcasted_iota(jnp.int32, (q, q), 0)
    jj = lax.broadcasted_iota(jnp.int32, (q, q), 1)
    tri = (ii <= jj) if rev else (ii >= jj)
    lo = lax.broadcasted_iota(jnp.int32, (q, LANES), 1) < SSM_P
    lo_row = lax.broadcasted_iota(jnp.int32, (1, LANES), 1) < SSM_P
    heads, slabs, per_group = range(SSM_HEADS), range(SSM_HEADS // 2), SSM_HEADS // 2 // SSM_G

    a = [dtr[h] * ac[h] for h in heads]
    c = [jnp.sum(jnp.where(tri, jnp.broadcast_to(a[h], (q, q)), 0.0), axis=1, keepdims=True) for h in heads]
    tot = [jnp.sum(a[h], axis=1, keepdims=True) for h in heads]
    cf = [jnp.broadcast_to(c[h], (q, q)) for h in heads]
    seg = [jnp.minimum(cf[h] - cf[h].T, 0.0) for h in heads]
    decay = [jnp.where(tri, jnp.exp(seg[h]), 0.0) for h in heads]
    cb = [lax.dot_general(cs[g].astype(bf16), bs[g].astype(bf16), NT, preferred_element_type=f32) for g in range(SSM_G)]
    m = [jnp.concatenate([cb[j // per_group] * decay[2 * j], cb[j // per_group] * decay[2 * j + 1]], axis=1).astype(bf16) for j in slabs]
    xdt = [xs[j] * dtx[j] for j in slabs]
    x2 = [jnp.concatenate([jnp.where(lo, xdt[j], 0.0), jnp.where(lo, 0.0, xdt[j])], axis=0).astype(bf16) for j in slabs]
    y_diag = [jnp.dot(m[j], x2[j], preferred_element_type=f32) for j in slabs]
    csel = [jnp.where(lo, cf[2 * j], cf[2 * j + 1]) for j in slabs]
    tsel = [jnp.where(lo_row, jnp.broadcast_to(tot[2 * j], (1, LANES)), jnp.broadcast_to(tot[2 * j + 1], (1, LANES))) for j in slabs]
    xend = [(xdt[j] * jnp.exp(tsel[j] - csel[j])).astype(bf16) for j in slabs]
    st = [lax.dot_general(bs[j // per_group].astype(bf16), xend[j], TN, preferred_element_type=f32) for j in slabs]
    y_off = [jnp.dot(cs[j // per_group].astype(bf16), hin[j].astype(bf16), preferred_element_type=f32) * jnp.exp(csel[j]) for j in slabs]
    return [y_diag[j] + y_off[j] for j in slabs], [hin[j] * jnp.exp(tsel[j]) + st[j] for j in slabs]


def _ssd_order(s, nc, ncc, rev):
    if not rev:
        return s
    return jnp.where(s < ncc, ncc - 1 - s, nc - 1 - (s - ncc))


SSD_SLABS = [slice(LANES * j, LANES * (j + 1)) for j in range(SSM_HEADS // 2)]
SSD_GROUPS = [slice(SSM_N * g, SSM_N * (g + 1)) for g in range(SSM_G)]


def _head_lanes(w, transpose=False):
    shape = (w, SSM_HEADS) if transpose else (SSM_HEADS, w)
    head = lax.broadcasted_iota(jnp.int32, shape, 1 if transpose else 0)
    lane = lax.broadcasted_iota(jnp.int32, shape, 0 if transpose else 1)
    return (lane // SSM_P == head).astype(f32)


def _ssd_fwd_call(xs, dt, dtr, bm, cm, acol, rev, n_ctx, name):
    t, w = xs.shape
    q = SSM_Q
    nc, ncc = t // q, n_ctx // q

    def body(xs_ref, dt_ref, dtr_ref, b_ref, c_ref, a_ref, y_ref, hin_ref, h_scr):
        @pl.when(pl.program_id(0) == 0)
        def _():
            h_scr[...] = jnp.zeros_like(h_scr)

        hin_ref[0] = h_scr[...]
        dtx = jnp.dot(dt_ref[...], _head_lanes(w), precision=lax.Precision.HIGHEST, preferred_element_type=f32)
        ys, houts = _ssd_chunk([xs_ref[:, sl] for sl in SSD_SLABS], [dtx[:, sl] for sl in SSD_SLABS],
                               [dtr_ref[h:h + 1, :] for h in range(SSM_HEADS)], [a_ref[h:h + 1, :] for h in range(SSM_HEADS)],
                               [b_ref[:, gs] for gs in SSD_GROUPS], [c_ref[:, gs] for gs in SSD_GROUPS],
                               [h_scr[:, sl] for sl in SSD_SLABS], rev)
        for sl, y, hout in zip(SSD_SLABS, ys, houts):
            y_ref[:, sl] = y.astype(y_ref.dtype)
            h_scr[:, sl] = hout

    def at(s):
        return _ssd_order(s, nc, ncc, rev)

    return pl.pallas_call(
        body, grid=(nc,),
        in_specs=[pl.BlockSpec((q, w), lambda s: (at(s), 0)), pl.BlockSpec((q, SSM_HEADS), lambda s: (at(s), 0)),
                  pl.BlockSpec((SSM_HEADS, q), lambda s: (0, at(s))),
                  pl.BlockSpec((q, SSM_BC), lambda s: (at(s), 0)), pl.BlockSpec((q, SSM_BC), lambda s: (at(s), 0)),
                  pl.BlockSpec((SSM_HEADS, 1), lambda s: (0, 0))],
        out_specs=[pl.BlockSpec((q, w), lambda s: (at(s), 0)), pl.BlockSpec((1, SSM_N, w), lambda s: (s, 0, 0))],
        out_shape=[jax.ShapeDtypeStruct((t, w), xs.dtype), jax.ShapeDtypeStruct((nc, SSM_N, w), f32)],
        scratch_shapes=[pltpu.VMEM((SSM_N, w), f32)],
        name=name, compiler_params=_cparams("arbitrary"))(xs, dt, dtr, bm, cm, acol)


def _ssd_bwd_call(xs, dt, dtr, bm, cm, acol, hin, dy, rev, n_ctx, name):
    t, w = xs.shape
    q = SSM_Q
    nc, ncc = t // q, n_ctx // q

    def body(xs_ref, dt_ref, dtr_ref, b_ref, c_ref, a_ref, hin_ref, dy_ref,
             dxs_ref, ddt_ref, ddtr_ref, db_ref, dc_ref, da_ref, dh_scr):
        @pl.when(pl.program_id(0) == 0)
        def _():
            dh_scr[...] = jnp.zeros_like(dh_scr)
            da_ref[...] = jnp.zeros_like(da_ref)

        dtx = jnp.dot(dt_ref[...], _head_lanes(w), precision=lax.Precision.HIGHEST, preferred_element_type=f32)
        _, vjp = jax.vjp(
            functools.partial(_ssd_chunk, rev=rev),
            [xs_ref[:, sl].astype(f32) for sl in SSD_SLABS], [dtx[:, sl] for sl in SSD_SLABS],
            [dtr_ref[h:h + 1, :] for h in range(SSM_HEADS)], [a_ref[h:h + 1, :] for h in range(SSM_HEADS)],
            [b_ref[:, gs].astype(f32) for gs in SSD_GROUPS], [c_ref[:, gs].astype(f32) for gs in SSD_GROUPS],
            [hin_ref[0, :, sl] for sl in SSD_SLABS])
        dxs, ddtx, ddtr, dac, dbs, dcs, dhin = vjp(([dy_ref[:, sl].astype(f32) for sl in SSD_SLABS], [dh_scr[:, sl] for sl in SSD_SLABS]))
        for j, sl in enumerate(SSD_SLABS):
            dxs_ref[:, sl] = dxs[j].astype(dxs_ref.dtype)
            dh_scr[:, sl] = dhin[j]
        for h in range(SSM_HEADS):
            ddtr_ref[h:h + 1, :] = ddtr[h]
            da_ref[h:h + 1, :] += dac[h]
        for g, gs in enumerate(SSD_GROUPS):
            db_ref[:, gs] = dbs[g].astype(db_ref.dtype)
            dc_ref[:, gs] = dcs[g].astype(dc_ref.dtype)
        ddt_ref[...] = jnp.dot(jnp.concatenate(ddtx, axis=1), _head_lanes(w, transpose=True),
                               precision=lax.Precision.HIGHEST, preferred_element_type=f32)

    def step(s):
        return nc - 1 - s

    def at(s):
        return _ssd_order(step(s), nc, ncc, rev)

    row = lambda wd: pl.BlockSpec((q, wd), lambda s: (at(s), 0))
    dtr_spec = pl.BlockSpec((SSM_HEADS, q), lambda s: (0, at(s)))
    a_spec = pl.BlockSpec((SSM_HEADS, 1), lambda s: (0, 0))
    return pl.pallas_call(
        body, grid=(nc,),
        in_specs=[row(w), row(SSM_HEADS), dtr_spec, row(SSM_BC), row(SSM_BC), a_spec,
                  pl.BlockSpec((1, SSM_N, w), lambda s: (step(s), 0, 0)), row(w)],
        out_specs=[row(w), row(SSM_HEADS), dtr_spec, row(SSM_BC), row(SSM_BC), a_spec],
        out_shape=[jax.ShapeDtypeStruct((t, w), xs.dtype), jax.ShapeDtypeStruct(dt.shape, f32), jax.ShapeDtypeStruct(dtr.shape, f32),
                   jax.ShapeDtypeStruct(bm.shape, bm.dtype), jax.ShapeDtypeStruct(cm.shape, cm.dtype), jax.ShapeDtypeStruct(acol.shape, f32)],
        scratch_shapes=[pltpu.VMEM((SSM_N, w), f32)],
        name=name, compiler_params=_cparams("arbitrary"))(xs, dt, dtr, bm, cm, acol, hin, dy)


def ssd_scan(xs, dt, dtr, bm, cm, acol, rev, n_ctx, name):
    @jax.custom_vjp
    def op(xs, dt, dtr, bm, cm, acol):
        return _ssd_fwd_call(xs, dt, dtr, bm, cm, acol, rev, n_ctx, name)[0]

    def fwd(xs, dt, dtr, bm, cm, acol):
        y, hin = _ssd_fwd_call(xs, dt, dtr, bm, cm, acol, rev, n_ctx, name)
        return y, (xs, dt, dtr, bm, cm, acol, hin)

    def bwd(res, dy):
        return tuple(_ssd_bwd_call(*res, dy, rev, n_ctx, name + "_bwd"))

    op.defvjp(fwd, bwd)
    return op(xs, dt, dtr, bm, cm, acol)


def _in_layout(d):
    return [('a_q', Q_W), ('a_k', KV_W), ('a_v', KV_W), ('b_z', SSM_INNER), ('b_xbc', SSM_INNER + 2 * SSM_BC), ('b_dt', DT_W),
            ('c_q', Q_W), ('c_k', KV_W), ('c_v', KV_W), ('g_a', d), ('g_b', d), ('g_c', d)]


def _dt_span(d):
    start = 0
    for name, n in _in_layout(d):
        if name == 'b_dt':
            return start, start + n
        start += n


@jax.custom_vjp
def _w_in_split(w):
    lo, hi = _dt_span(w.shape[0])
    dt = jnp.concatenate([w[:, lo:hi], jnp.zeros((w.shape[0], DT_PAD - (hi - lo)), w.dtype)], axis=1)
    return jnp.concatenate([w[:, :lo], w[:, hi:]], axis=1), dt


def _w_in_join(g_main, g_dt):
    lo, hi = _dt_span(g_main.shape[0])
    return jnp.concatenate([g_main[:, :lo], g_dt[:, :hi - lo], g_main[:, lo:]], axis=1)


_w_in_split.defvjp(lambda w: (_w_in_split(w), None), lambda _, g: (_w_in_join(*g),))


def _rope_tables(n_ctx, n_lat):
    rows = n_lat // GRID_W
    t_row = jnp.repeat(jnp.arange(rows), GRID_W).astype(f32)
    t_col = jnp.tile(jnp.arange(GRID_W), rows).astype(f32)
    n = HEAD_DIM // 4
    inv = ROPE_BASE ** (-jnp.arange(n, dtype=f32) / n)
    ang = jnp.concatenate([t_row[:, None] * inv, t_col[:, None] * inv], axis=-1)
    cos = jnp.concatenate([jnp.ones((n_ctx, HEAD_DIM // 2), f32), jnp.cos(ang)], axis=0)
    sin = jnp.concatenate([jnp.zeros((n_ctx, HEAD_DIM // 2), f32), jnp.sin(ang)], axis=0)
    return jnp.repeat(cos, 2, axis=1), jnp.stack([-sin, sin], axis=-1).reshape(sin.shape[0], HEAD_DIM)


def _heads_major(a, n_heads):
    return a.reshape(a.shape[0], n_heads, HEAD_DIM).transpose(1, 0, 2)


def _heads_minor(a):
    return a.transpose(1, 0, 2).reshape(a.shape[1], a.shape[0] * HEAD_DIM)


def _layer(xall, w, s, cm, tabs, n_ctx, li, gather_a, gather_c):
    t, d = xall.shape
    ncb = n_ctx // ROW_TILE
    nm = f"l{li}_"
    ctq, stq, ctk, stk = tabs
    def mod(blk, cmv, i):
        return jnp.where(blk < ncb, cmv[0:1, i * d:(i + 1) * d], cmv[1:2, i * d:(i + 1) * d])

    def norm_mod(blk, x, g, cmv):
        return (_rms(x, g) * (1.0 + mod(blk, cmv, 1)) + mod(blk, cmv, 0),)

    (h,) = rowwise(norm_mod, [xall], [], [s['norm1'][None], cm], [d], [bf16], nm + "norm1")
    w_main, w_dt = _w_in_split(w['w_in'])
    u = mm(h, w_main, nm + "in")
    b_dt = mm(h, w_dt, nm + "in_dt", f32)
    a_q, a_k, a_v, b_z, b_xbc, c_q, c_k, c_v, g_a, g_b, g_c = split_cols(u, [n for name, n in _in_layout(d) if name != 'b_dt'])

    def rope(blk, q, k, v, ct_q, st_q, ct_k, st_k):
        return q * ct_q + _swap_pairs(q) * st_q, k * ct_k + _swap_pairs(k) * st_k, v

    def norm_rope(blk, q, k, v, ct_q, st_q, ct_k, st_k, gq, gk):
        return rope(blk, _head_rms(q, gq), _head_rms(k, gk), v, ct_q, st_q, ct_k, st_k)

    qkv_w, qkv_t = [Q_W, KV_W, KV_W], [bf16, bf16, bf16]
    qa, ka, va = rowwise(rope, [a_q, a_k, a_v], [ctq, stq, ctk, stk], [], qkv_w, qkv_t, nm + "ropeA")
    gq = jnp.tile(s['c_q_norm'], N_HEADS)[None]
    gk = jnp.tile(s['c_k_norm'], N_KV)[None]
    qc, kc, vc = rowwise(norm_rope, [c_q, c_k, c_v], [ctq, stq, ctk, stk], [gq, gk], qkv_w, qkv_t, nm + "ropeC")
    ya, mine = attention(qa, _heads_major(ka, N_KV), _heads_major(va, N_KV), s['a_sink'], True, nm + "attnA", *gather_a[1:])
    yc, nxt = attention(qc, _heads_major(kc, N_KV), _heads_major(vc, N_KV), jnp.zeros((N_HEADS,), f32), False,
                        nm + "attnC", *gather_c[1:])
    w = {**w, **{n: _assemble(n, g) for n, g in zip(gather_a[0], mine)}}
    w_next = {n: _assemble(n, g) for n, g in zip(gather_c[0], nxt)}

    cw, cb = s['ssm_conv_w'], s['ssm_conv_b']
    conv_silu = lambda uu, w3, b: _silu(_dwconv(uu, w3[0:1], w3[1:2], w3[2:3], b, n_ctx))
    xbc = colwise(conv_silu, [b_xbc], [cw, cb[None]], bf16, nm + "ssmconv")
    xs, bm, cmat = split_cols(xbc, [SSM_INNER, SSM_BC, SSM_BC])
    bias = jnp.concatenate([s['ssm_dt_bias'].reshape(1, DT_W), jnp.zeros((1, DT_PAD - DT_W), f32)], axis=1)

    def softplus(blk, r, b):
        z = r + b
        return (jnp.maximum(z, 0.0) + jnp.log(1.0 + jnp.exp(-jnp.abs(z))),)

    (dt_all,) = rowwise(softplus, [b_dt], [], [bias], [DT_PAD], [f32], nm + "dt")
    a_coef = -jnp.exp(s['ssm_A_log'])
    ys_dir = []
    for di, rev in enumerate((False, True)):
        dt = dt_all[:, di * SSM_HEADS:(di + 1) * SSM_HEADS]
        ys_dir.append(ssd_scan(xs, dt, dt.T, bm, cmat, a_coef[di][:, None], rev, n_ctx,
                               nm + ("ssd_r" if rev else "ssd_f")))

    def ssm_out(blk, yf, yb, x, z, dskip, g):
        return (_rms((yf + yb + x * dskip) * _silu(z), g),)

    (ysn,) = rowwise(ssm_out, [ys_dir[0], ys_dir[1], xs, b_z], [], [jnp.repeat(s['ssm_D'], SSM_P)[None], s['ssm_norm'][None]],
                     [SSM_INNER], [bf16], nm + "ssmout")

    pa, pb, pc = mm(ya, w['w_oa'], nm + "oa"), mm(ysn, w['w_ob'], nm + "ob"), mm(yc, w['w_oc'], nm + "oc")

    def merge(blk, ga, gb, gc, a, b, c):
        return (_sigmoid(ga) * a + _sigmoid(gb) * b + _sigmoid(gc) * c,)

    (mrg,) = rowwise(merge, [g_a, g_b, g_c, pa, pb, pc], [], [], [d], [bf16], nm + "merge")
    o = mm(mrg, w['w_out'], nm + "out")

    def resid_norm_mod(blk, x, oo, g, cmv):
        x1 = x + mod(blk, cmv, 2) * oo
        return x1, _rms(x1, g) * (1.0 + mod(blk, cmv, 4)) + mod(blk, cmv, 3)

    x1, h2 = rowwise(resid_norm_mod, [xall, o], [], [s['norm2'][None], cm], [d, d], [f32, bf16], nm + "norm2")
    up, gt = mm(h2, w['ffn_w_up'], nm + "up"), mm(h2, w['ffn_w_gate'], nm + "gate")
    fw, fb = s['ffn_conv_w'], s['ffn_conv_b']
    ffn_act = lambda g_, u_, w3, b: _silu(_dwconv(g_, w3[0:1], w3[1:2], w3[2:3], b, n_ctx)) * u_
    act = colwise(ffn_act, [gt, up], [fw, fb[None]], bf16, nm + "ffnact")
    f = mm(act, w['ffn_w_down'], nm + "down")

    def resid(blk, x, ff, cmv):
        return (x + mod(blk, cmv, 5) * ff,)

    (x2,) = rowwise(resid, [x1, f], [], [cm], [d], [f32], nm + "resid")
    return x2, w_next


def _assemble(name, gathered):
    if BIG[name] == 0:
        return gathered.reshape(-1, gathered.shape[-1])
    return jnp.concatenate([gathered[j] for j in range(8)], axis=1)


def _loss_fn(big0, shards0_late, stand_ins0, shards1, stand_ins1, small, x, ctx, c, target, n_ctx):
    n_lat, d = x.shape
    xall = jnp.concatenate([ctx, x], axis=0)
    ct, st = _rope_tables(n_ctx, n_lat)
    tabs = (jnp.tile(ct, (1, N_HEADS)) * HEAD_DIM ** -0.5, jnp.tile(st, (1, N_HEADS)) * HEAD_DIM ** -0.5,
            jnp.tile(ct, (1, N_KV)), jnp.tile(st, (1, N_KV)))
    srows = jnp.concatenate([_silu(small['c_ctx'])[None], _silu(c), jnp.zeros((14, d), f32)], axis=0)
    names = list(BIG)
    big = big0
    gather_a = (LATE, [shards0_late[n] for n in LATE], [stand_ins0[n] for n in LATE])
    gather_c = (names, [shards1[n] for n in names], [stand_ins1[n] for n in names])
    for li in range(2):
        cm = mm(srows, big['w_mod'], f"l{li}_mod", f32)[0:2] + small['b_mod'][li][None]
        sl = {k: v[li] for k, v in small.items() if k not in ('c_ctx', 'final_norm')}
        xall, big = _layer(xall, big, sl, cm, tabs, n_ctx, li, gather_a, gather_c)
        gather_a = gather_c = ((), (), ())
    ncb = n_ctx // ROW_TILE
    tgt = jnp.concatenate([jnp.zeros((n_ctx, d), f32), target], axis=0)

    def loss_rows(blk, xx, tg, g):
        e = _rms(xx, g) - tg
        return (jnp.where(blk < ncb, 0.0, 0.5) * jnp.mean(e * e, axis=-1, keepdims=True),)

    (rows,) = rowwise(loss_rows, [xall], [tgt], [small['final_norm'][None]], [1], [f32], "loss")
    return jnp.sum(rows)


def _hbm_call(body, ins, out_shapes, n_sems, name):
    any_spec = pl.BlockSpec(memory_space=pl.ANY)
    return pl.pallas_call(
        body, out_shape=out_shapes, in_specs=[any_spec] * len(ins), out_specs=[any_spec] * len(out_shapes),
        scratch_shapes=[pltpu.SemaphoreType.DMA((n_sems,)), pltpu.SemaphoreType.DMA((n_sems,)), pltpu.SemaphoreType.DMA((len(ins),))],
        name=name)(*ins)


def _gather_steps(x_refs, out_refs, send_sems, recv_sems, local_sems):
    n = len(x_refs)
    x, y, c = lax.axis_index("x"), lax.axis_index("y"), lax.axis_index("c")
    me, sibling = (x, y, c), (x, y, 1 - c)
    chips = [(1 - x, y), (x, 1 - y), (1 - x, 1 - y)]

    def copy(a, k, block, to, src=None):
        px, py, pc = block
        slot = out_refs[a].at[4 * px + 2 * py + pc]
        return pltpu.make_async_remote_copy(
            src_ref=slot if src is None else src, dst_ref=slot,
            send_sem=send_sems.at[7 * a + k], recv_sem=recv_sems.at[7 * a + k], device_id=to, device_id_type=MESH)

    mine = [pltpu.make_async_copy(x_refs[a], out_refs[a].at[4 * x + 2 * y + c], local_sems.at[a]) for a in range(n)]
    first = []
    for a in range(n):
        first += [copy(a, 1 + j, me, (*chip, c), src=x_refs[a]) for j, chip in enumerate(chips)]
        first.append(copy(a, 0, me, sibling, src=x_refs[a]))

    def start():
        for cp in mine + first:
            cp.start()

    def finish():
        passed = []
        for a in range(n):
            for j, chip in enumerate(chips):
                copy(a, 1 + j, (*chip, c), me).wait_recv()
                passed.append(copy(a, 4 + j, (*chip, c), sibling))
                passed[-1].start()
        for a in range(n):
            copy(a, 0, sibling, me).wait_recv()
            for j, chip in enumerate(chips):
                copy(a, 4 + j, (*chip, 1 - c), me).wait_recv()
        for cp in first + passed:
            cp.wait_send()
        for cp in mine:
            cp.wait()

    return start, finish


def _gather_scratch(n):
    return [pltpu.SemaphoreType.DMA((7 * n,)), pltpu.SemaphoreType.DMA((7 * n,)), pltpu.SemaphoreType.DMA((n,))]


def all_gather(shards, name):
    n = len(shards)

    def body(*refs):
        start, finish = _gather_steps(refs[:n], refs[n:2 * n], *refs[2 * n:])
        start()
        finish()

    return _hbm_call(body, shards, [jax.ShapeDtypeStruct((8,) + s.shape, s.dtype) for s in shards], 7 * n, name)


def rs_to_sibling(gs, name="rs_sibling"):
    n = len(gs)

    def body(*refs):
        g_refs, out_refs, (send_sems, recv_sems, _) = refs[:n], refs[n:2 * n], refs[2 * n:]
        x, y, c = lax.axis_index("x"), lax.axis_index("y"), lax.axis_index("c")
        copies = [pltpu.make_async_remote_copy(
            src_ref=g_refs[a].at[2 * k + (1 - c)], dst_ref=out_refs[a].at[k], send_sem=send_sems.at[4 * a + k],
            recv_sem=recv_sems.at[4 * a + k], device_id=(x, y, 1 - c), device_id_type=MESH) for a in range(n) for k in range(4)]
        for cp in copies:
            cp.start()
        for cp in copies:
            cp.wait()

    return _hbm_call(body, gs, [jax.ShapeDtypeStruct((4,) + g.shape[1:], g.dtype) for g in gs], 4 * n, name)


def rs_to_chips(ss):
    n = len(ss)

    def body(*refs):
        copies = _chips_copies(refs[:n], refs[n:2 * n], refs[2 * n], refs[2 * n + 1])
        for cp in copies:
            cp.start()
        for cp in copies:
            cp.wait()

    return _hbm_call(body, ss, [jax.ShapeDtypeStruct((3,) + s.shape[1:], s.dtype) for s in ss], 3 * n, "rs_chips")


def _chips_copies(s_refs, out_refs, send_sems, recv_sems):
    x, y, c = lax.axis_index("x"), lax.axis_index("y"), lax.axis_index("c")
    copies = []
    for a in range(len(s_refs)):
        for k, (fx, fy) in enumerate([(1, 0), (0, 1), (1, 1)]):
            px, py = (1 - x) if fx else x, (1 - y) if fy else y
            copies.append(pltpu.make_async_remote_copy(
                src_ref=s_refs[a].at[2 * px + py], dst_ref=out_refs[a].at[k], send_sem=send_sems.at[3 * a + k],
                recv_sem=recv_sems.at[3 * a + k], device_id=(px, py, c), device_id_type=MESH))
    return copies


def _flat_tile(rows, cols):
    return _row_tile(rows, 4 * 4 * cols)


def pair_sum(g, r1, my_c, name):
    _, rows, cols = g.shape
    tm = _flat_tile(rows, cols)

    def body(c_ref, g_ref, r_ref, o_ref):
        o_ref[...] = (g_ref[...].astype(f32) + r_ref[...].astype(f32)).astype(o_ref.dtype)

    return pl.pallas_call(
        body, grid_spec=pltpu.PrefetchScalarGridSpec(
            num_scalar_prefetch=1, grid=(4, rows // tm),
            in_specs=[pl.BlockSpec((1, tm, cols), lambda k, i, c: (2 * k + c[0], i, 0)),
                      pl.BlockSpec((1, tm, cols), lambda k, i, c: (k, i, 0))],
            out_specs=pl.BlockSpec((1, tm, cols), lambda k, i, c: (k, i, 0))),
        out_shape=jax.ShapeDtypeStruct((4, rows, cols), g.dtype), name=name,
        compiler_params=_cparams("parallel", "parallel"))(my_c, g, r1)


def _adam_math(w, g, m, v):
    m2 = ADAM_B1 * m + (1.0 - ADAM_B1) * g
    v2 = ADAM_B2 * v + (1.0 - ADAM_B2) * (g * g)
    m_hat = m2 / (1.0 - ADAM_B1 ** ADAM_STEP)
    v_hat = v2 / (1.0 - ADAM_B2 ** ADAM_STEP)
    return -ADAM_LR * (m_hat / (jnp.sqrt(v_hat) + ADAM_EPS) + ADAM_WD * w), m2, v2


def sum_adam(parts, w, m, v, name):
    groups, rows, cols = w.shape
    tm = _flat_tile(rows, cols)
    nblk = rows // tm
    flat = []
    scalars = [p[2] for ps in parts for p in ps if p[2] is not None]
    for gi, ps in enumerate(parts):
        flat.append([])
        for arr, static_rows, dyn in ps:
            if dyn is not None:
                flat[gi].append((arr, functools.partial(lambda l, i, s, gi: (s[0], jnp.where(l == gi, i, nblk - 1), 0), gi=gi)))
            else:
                for k in static_rows:
                    flat[gi].append((arr, functools.partial(lambda l, i, s, gi, k: (k, jnp.where(l == gi, i, nblk - 1), 0), gi=gi, k=k)))
    counts = [len(f) for f in flat]
    na = sum(counts)

    def body(s_ref, *refs):
        sums, at = [], 0
        for cnt in counts:
            g = refs[at][0].astype(f32)
            for r in refs[at + 1:at + cnt]:
                g = g + r[0].astype(f32)
            sums.append(g)
            at += cnt
        g = sums[0]
        for gi in range(1, groups):
            g = jnp.where(pl.program_id(0) == gi, sums[gi], g)
        w_ref, m_ref, v_ref = refs[na:na + 3]
        g_out, d_out, m_out, v_out = refs[na + 3:]
        d, m2, v2 = _adam_math(w_ref[0], g, m_ref[0], v_ref[0])
        g_out[0] = g
        d_out[0] = d
        m_out[0] = m2
        v_out[0] = v2

    blk = pl.BlockSpec((1, tm, cols), lambda l, i, s: (l, i, 0))
    scalar = scalars[0] if scalars else jnp.zeros((1,), jnp.int32)
    return pl.pallas_call(
        body, grid_spec=pltpu.PrefetchScalarGridSpec(
            num_scalar_prefetch=1, grid=(groups, nblk),
            in_specs=[pl.BlockSpec((1, tm, cols), im) for f in flat for _, im in f] + [blk, blk, blk],
            out_specs=[blk, blk, blk, blk]),
        out_shape=[jax.ShapeDtypeStruct((groups, rows, cols), f32)] * 4, name=name,
        compiler_params=_cparams("arbitrary", "arbitrary"))(scalar, *[a for f in flat for a, _ in f], w, m, v)


FLAT_COLS = 1024


def _to_flat(vec):
    n = vec.shape[0]
    total = -(-n // (8 * FLAT_COLS)) * 8 * FLAT_COLS
    return jnp.concatenate([vec, jnp.zeros((total - n,), vec.dtype)]).reshape(-1, FLAT_COLS)


def _pack(tree, names):
    return jnp.concatenate([tree[n].reshape(-1) for n in names])


def _unpack(vec, like, names):
    out, off = {}, 0
    for n in names:
        size = like[n].size
        out[n] = vec[off:off + size].reshape(like[n].shape)
        off += size
    return out


def kernel(x, c, ctx, c_ctx, w_mod, b_mod, norm1, norm2, w_in, a_sink, ssm_conv_w, ssm_conv_b, ssm_A_log, ssm_dt_bias, ssm_D, ssm_norm, c_q_norm, c_k_norm, w_oa, w_ob, w_oc, w_out, ffn_w_up, ffn_w_gate, ffn_conv_w, ffn_conv_b, ffn_w_down, final_norm, loss_target, m_c_ctx, m_w_mod, m_b_mod, m_norm1, m_norm2, m_w_in, m_a_sink, m_ssm_conv_w, m_ssm_conv_b, m_ssm_A_log, m_ssm_dt_bias, m_ssm_D, m_ssm_norm, m_c_q_norm, m_c_k_norm, m_w_oa, m_w_ob, m_w_oc, m_w_out, m_ffn_w_up, m_ffn_w_gate, m_ffn_conv_w, m_ffn_conv_b, m_ffn_w_down, m_final_norm, v_c_ctx, v_w_mod, v_b_mod, v_norm1, v_norm2, v_w_in, v_a_sink, v_ssm_conv_w, v_ssm_conv_b, v_ssm_A_log, v_ssm_dt_bias, v_ssm_D, v_ssm_norm, v_c_q_norm, v_c_k_norm, v_w_oa, v_w_ob, v_w_oc, v_w_out, v_ffn_w_up, v_ffn_w_gate, v_ffn_conv_w, v_ffn_conv_b, v_ffn_w_down, v_final_norm):
    args = (x, c, ctx, c_ctx, w_mod, b_mod, norm1, norm2, w_in, a_sink, ssm_conv_w, ssm_conv_b, ssm_A_log, ssm_dt_bias, ssm_D, ssm_norm, c_q_norm, c_k_norm, w_oa, w_ob, w_oc, w_out, ffn_w_up, ffn_w_gate, ffn_conv_w, ffn_conv_b, ffn_w_down, final_norm, loss_target)
    moms = (m_c_ctx, m_w_mod, m_b_mod, m_norm1, m_norm2, m_w_in, m_a_sink, m_ssm_conv_w, m_ssm_conv_b, m_ssm_A_log, m_ssm_dt_bias, m_ssm_D, m_ssm_norm, m_c_q_norm, m_c_k_norm, m_w_oa, m_w_ob, m_w_oc, m_w_out, m_ffn_w_up, m_ffn_w_gate, m_ffn_conv_w, m_ffn_conv_b, m_ffn_w_down, m_final_norm)
    vars_ = (v_c_ctx, v_w_mod, v_b_mod, v_norm1, v_norm2, v_w_in, v_a_sink, v_ssm_conv_w, v_ssm_conv_b, v_ssm_A_log, v_ssm_dt_bias, v_ssm_D, v_ssm_norm, v_c_q_norm, v_c_k_norm, v_w_oa, v_w_ob, v_w_oc, v_w_out, v_ffn_w_up, v_ffn_w_gate, v_ffn_conv_w, v_ffn_conv_b, v_ffn_w_down, v_final_norm)
    p = dict(zip(IN_NAMES, args))
    mom = dict(zip(WEIGHTS, moms))
    var = dict(zip(WEIGHTS, vars_))
    depth = w_in.shape[0]
    n_ctx = ctx.shape[1]
    xi, yi, ci = lax.axis_index("x"), lax.axis_index("y"), lax.axis_index("c")
    dev = 4 * xi + 2 * yi + ci
    big_names = list(BIG)

    assert depth == 2 and n_ctx == ROW_TILE
    shards = [{n: p[n][li].astype(bf16) for n in big_names} for li in range(depth)]
    g_early = all_gather([shards[0][n] for n in EARLY], "gather_l0")
    g_conv = all_gather([_to_flat(_pack(p, CONV_W))], "gather_conv")[0].reshape(8, -1)
    big0 = {n: _assemble(n, g) for n, g in zip(EARLY, g_early)}
    stand_ins = [{n: jnp.zeros((8,) + shards[li][n].shape, bf16) for n in (LATE, big_names)[li]} for li in range(depth)]
    conv_full, off = {}, 0
    for n in CONV_W:
        shp = p[n].shape
        seg = g_conv[:, off:off + p[n].size].reshape(8, *shp)
        conv_full[n] = jnp.moveaxis(seg, 0, -2).reshape(*shp[:-1], 8 * shp[-1])
        off += p[n].size
    small = {n: p[n] for n in REPL}
    small.update(conv_full)

    loss, (g_early, g_late0, g_big1, g_small, g_x) = jax.value_and_grad(_loss_fn, argnums=(0, 2, 4, 5, 6))(
        big0, {n: shards[0][n] for n in LATE}, stand_ins[0], shards[1], stand_ins[1], small, x[0], ctx[0], c, loss_target[0], n_ctx)
    loss = lax.psum(loss, AXES)

    def send_rows(n):
        b = p[n].shape[-1]
        return jnp.stack([g_early[n][:, b * j:b * (j + 1)] for j in range(8)])

    send = [send_rows(n) for n in EARLY]
    from_sibling = rs_to_sibling(send)
    my_c = ci.reshape(1).astype(jnp.int32)
    side_sum = [pair_sum(s, r, my_c, "rs_pair_sum_" + n) for n, s, r in zip(EARLY, send, from_sibling)]
    from_chips = rs_to_chips(side_sum)
    chip = (2 * xi + yi).reshape(1).astype(jnp.int32)
    big_out = [{}, {}, {}, {}]
    for n in big_names:
        if n in EARLY:
            a = EARLY.index(n)
            parts = [[(side_sum[a], None, chip), (from_chips[a], (0, 1, 2), None)]]
        else:
            parts = [[(g_late0[n], None, chip), (g_late0[n], (4, 5, 6), None)]]
        parts.append([(g_big1[n], None, chip), (g_big1[n], (4, 5, 6), None)])
        outs = sum_adam(parts, p[n], mom[n], var[n], "adam_" + n)
        for k in range(4):
            big_out[k][n] = outs[k]

    sm_names = REPL + list(CONV_W)
    g_vec = _to_flat(_pack(g_small, sm_names))
    gathered = all_gather([g_vec], "gather_small_grads")[0]
    n_repl = sum(p[n].size for n in REPL)

    def repl_flat(tree):
        return _to_flat(jnp.concatenate([_pack(tree, REPL), jnp.zeros((g_vec.size - n_repl,), f32)]))

    outs_small = sum_adam([[(gathered, tuple(range(8)), None)]], repl_flat(p)[None], repl_flat(mom)[None], repl_flat(var)[None],
                          "adam_small")
    g_sum = outs_small[0].reshape(-1)
    small_out = [_unpack(o.reshape(-1), p, REPL) for o in outs_small]
    conv_g_full = _unpack(g_sum[n_repl:], conv_full, CONV_W)
    conv_g = {n: lax.dynamic_slice_in_dim(conv_g_full[n], dev * p[n].shape[-1], p[n].shape[-1], axis=2) for n in CONV_W}
    conv_gv = _to_flat(_pack(conv_g, CONV_W))
    outs_conv = sum_adam([[(conv_gv[None], (0,), None)]], _to_flat(_pack(p, CONV_W))[None], _to_flat(_pack(mom, CONV_W))[None],
                         _to_flat(_pack(var, CONV_W))[None], "adam_conv")
    conv_out = [_unpack(o.reshape(-1), p, CONV_W) for o in outs_conv]

    res = []
    for k in range(4):
        tree = {**big_out[k], **small_out[k], **conv_out[k]}
        res.append([tree[n] for n in WEIGHTS])
    return (loss, g_x[None], *res[0], *res[1], *res[2], *res[3])
```

```python
import functools

import jax
import jax.numpy as jnp
from jax import lax
from jax.experimental import pallas as pl
from jax.experimental.pallas import tpu as pltpu

f32 = jnp.float32
bf16 = jnp.bfloat16
MESH = pl.DeviceIdType.MESH
AXES = ("x", "y", "c")

GRID_W = 64
HEAD_DIM = 64
ROPE_BASE = 10000.0
EPS = 1e-6
WINDOW = 128
N_HEADS = 8
N_KV = 2
SSM_HEADS = 16
SSM_P = 64
SSM_G = 2
SSM_N = 128
SSM_INNER = SSM_HEADS * SSM_P
SSM_BC = SSM_G * SSM_N
SSM_Q = 128
Q_W = N_HEADS * HEAD_DIM
KV_W = N_KV * HEAD_DIM
DT_W = 2 * SSM_HEADS
DT_PAD = 128
ADAM_LR, ADAM_B1, ADAM_B2, ADAM_EPS, ADAM_WD, ADAM_STEP = 0.001, 0.9, 0.999, 1e-08, 0.01, 10

LANES = 128
ROW_TILE = 256
VMEM_BLOCK_BUDGET = 6 * 1024 * 1024
ATTN_SLAB = 128
MM_ROW_CAP = 1088
MM_TILE_CAP = 1536
NEG = -1e30

IN_NAMES = ['x', 'c', 'ctx', 'c_ctx', 'w_mod', 'b_mod', 'norm1', 'norm2', 'w_in', 'a_sink', 'ssm_conv_w', 'ssm_conv_b', 'ssm_A_log', 'ssm_dt_bias', 'ssm_D', 'ssm_norm', 'c_q_norm', 'c_k_norm', 'w_oa', 'w_ob', 'w_oc', 'w_out', 'ffn_w_up', 'ffn_w_gate', 'ffn_conv_w', 'ffn_conv_b', 'ffn_w_down', 'final_norm', 'loss_target']
WEIGHTS = IN_NAMES[3:28]
BIG = {'w_mod': 1, 'w_in': 1, 'w_oa': 1, 'w_ob': 0, 'w_oc': 1, 'w_out': 0, 'ffn_w_up': 1, 'ffn_w_gate': 1, 'ffn_w_down': 0}
EARLY = ['w_mod', 'w_in']
LATE = [n for n in BIG if n not in EARLY]
CONV_W = ('ssm_conv_w', 'ffn_conv_w')
REPL = [n for n in WEIGHTS if n not in BIG and n not in CONV_W]

NT = (((1,), (1,)), ((), ()))
TN = (((0,), (0,)), ((), ()))
NN = (((1,), (0,)), ((), ()))


def _cparams(*sem):
    return pltpu.CompilerParams(dimension_semantics=sem)


def _div_tile(n, unit, cap):
    for d in range(min(n, int(cap)), 0, -1):
        if n % d == 0 and d % unit == 0:
            return d
    return n


def _row_tile(m, row_bytes):
    return _div_tile(m, 16, max(16, VMEM_BLOCK_BUDGET // row_bytes))


def _mm_call(a, b, mode, out_dtype, name):
    if mode == "nn":
        (m, k), n = a.shape, b.shape[1]
    elif mode == "nt":
        (m, k), n = a.shape, b.shape[0]
    else:
        (k, m), n = a.shape, b.shape[1]
    dims = {"nn": NN, "nt": NT, "tn": TN}[mode]
    ia, ib = a.dtype.itemsize, b.dtype.itemsize
    tm = _div_tile(m, LANES, MM_TILE_CAP) if mode == "tn" else _div_tile(m, 16, MM_ROW_CAP)
    tn = _div_tile(n, LANES, min(MM_TILE_CAP, VMEM_BLOCK_BUDGET // (4 * tm)))
    tk = _div_tile(k, 16 if mode == "tn" else LANES,
                   min(MM_ROW_CAP if mode == "tn" else MM_TILE_CAP, VMEM_BLOCK_BUDGET // (tm * ia), VMEM_BLOCK_BUDGET // (tn * ib)))
    nk = k // tk

    def body(a_ref, b_ref, o_ref, *acc):
        part = lax.dot_general(a_ref[...].astype(bf16), b_ref[...].astype(bf16), dims, preferred_element_type=f32)
        if nk == 1:
            o_ref[...] = part.astype(o_ref.dtype)
            return
        kk = pl.program_id(2)

        @pl.when(kk == 0)
        def _():
            acc[0][...] = part

        @pl.when(kk > 0)
        def _():
            acc[0][...] += part

        @pl.when(kk == nk - 1)
        def _():
            o_ref[...] = acc[0][...].astype(o_ref.dtype)

    a_spec = pl.BlockSpec((tk, tm), lambda i, j, kk: (kk, i)) if mode == "tn" else pl.BlockSpec((tm, tk), lambda i, j, kk: (i, kk))
    b_spec = pl.BlockSpec((tn, tk), lambda i, j, kk: (j, kk)) if mode == "nt" else pl.BlockSpec((tk, tn), lambda i, j, kk: (kk, j))
    return pl.pallas_call(
        body, grid=(m // tm, n // tn, nk), in_specs=[a_spec, b_spec],
        out_specs=pl.BlockSpec((tm, tn), lambda i, j, kk: (i, j)),
        out_shape=jax.ShapeDtypeStruct((m, n), out_dtype),
        scratch_shapes=[pltpu.VMEM((tm, tn), f32)] if nk > 1 else [], name=name,
        compiler_params=_cparams("parallel", "parallel", "arbitrary"))(a, b)


def mm(a, b, name, out_dtype=None):
    @jax.custom_vjp
    def op(a, b):
        return _mm_call(a, b, "nn", out_dtype or bf16, name)

    def fwd(a, b):
        return op(a, b), (a, b)

    def bwd(res, g):
        a, b = res
        return _mm_call(g, b, "nt", a.dtype, name + "_da"), _mm_call(a, g, "tn", b.dtype, name + "_db")

    op.defvjp(fwd, bwd)
    return op(a, b)


def split_cols(u, widths):
    offs = [0]
    for w in widths:
        offs.append(offs[-1] + w)

    @jax.custom_vjp
    def op(u):
        return tuple(u[:, offs[i]:offs[i + 1]] for i in range(len(widths)))

    def fwd(u):
        return op(u), None

    def bwd(_, cts):
        return (jnp.concatenate(cts, axis=1),)

    op.defvjp(fwd, bwd)
    return op(u)


def rowwise(fn, rows, consts, pars, out_widths, out_dtypes, name):
    t = rows[0].shape[0]
    tm = ROW_TILE
    nb = t // tm
    nr, nc, npar = len(rows), len(consts), len(pars)

    def rspec(a):
        return pl.BlockSpec((tm, a.shape[1]), lambda i: (i, 0))

    def pspec(a):
        return pl.BlockSpec(a.shape, lambda i: (0,) * a.ndim)

    def call_fwd(rows, consts, pars):
        def body(*refs):
            blk = pl.program_id(0)
            ins = [r[...].astype(f32) for r in refs[:nr + nc]]
            ps = [r[...] for r in refs[nr + nc:nr + nc + npar]]
            outs = fn(blk, *ins, *ps)
            for o_ref, o in zip(refs[nr + nc + npar:], outs):
                o_ref[...] = o.astype(o_ref.dtype)

        return pl.pallas_call(
            body, grid=(nb,),
            in_specs=[rspec(a) for a in rows + consts] + [pspec(a) for a in pars],
            out_specs=[pl.BlockSpec((tm, w), lambda i: (i, 0)) for w in out_widths],
            out_shape=[jax.ShapeDtypeStruct((t, w), d) for w, d in zip(out_widths, out_dtypes)],
            name=name, compiler_params=_cparams("parallel"))(*rows, *consts, *pars)

    def call_bwd(rows, consts, pars, cts):
        nout = len(cts)

        def body(*refs):
            blk = pl.program_id(0)
            ins = [r[...].astype(f32) for r in refs[:nr]]
            cs = [r[...].astype(f32) for r in refs[nr:nr + nc]]
            ps = [r[...] for r in refs[nr + nc:nr + nc + npar]]
            dys = [r[...].astype(f32) for r in refs[nr + nc + npar:nr + nc + npar + nout]]
            d_refs = refs[nr + nc + npar + nout:]
            _, vjp = jax.vjp(lambda *a: tuple(fn(blk, *a[:nr], *cs, *a[nr:])), *ins, *ps)
            grads = vjp(tuple(dys))
            for d_ref, g in zip(d_refs[:nr], grads[:nr]):
                d_ref[...] = g.astype(d_ref.dtype)
            if npar:
                @pl.when(blk == 0)
                def _():
                    for d_ref in d_refs[nr:]:
                        d_ref[...] = jnp.zeros_like(d_ref)

                for d_ref, g in zip(d_refs[nr:], grads[nr:]):
                    d_ref[...] += g

        return pl.pallas_call(
            body, grid=(nb,),
            in_specs=[rspec(a) for a in rows + consts] + [pspec(a) for a in pars] + [rspec(a) for a in cts],
            out_specs=[rspec(a) for a in rows] + [pspec(a) for a in pars],
            out_shape=[jax.ShapeDtypeStruct(a.shape, a.dtype) for a in rows + pars],
            name=name + "_bwd", compiler_params=_cparams("arbitrary"))(*rows, *consts, *pars, *cts)

    @jax.custom_vjp
    def op(rows, consts, pars):
        return tuple(call_fwd(list(rows), list(consts), list(pars)))

    def fwd(rows, consts, pars):
        return op(rows, consts, pars), (rows, consts, pars)

    def bwd(res, cts):
        rows, consts, pars = res
        g = call_bwd(list(rows), list(consts), list(pars), list(cts))
        return tuple(g[:nr]), tuple(jnp.zeros_like(a) for a in consts), tuple(g[nr:])

    op.defvjp(fwd, bwd)
    return op(tuple(rows), tuple(consts), tuple(pars))


def colwise(fn, cols, pars, out_dtype, name):
    t, w = cols[0].shape
    tc = LANES
    nb = w // tc
    ncol, npar = len(cols), len(pars)

    def cspec(a):
        return pl.BlockSpec((a.shape[0], tc), lambda j: (0, j))

    def call_fwd(cols, pars):
        def body(*refs):
            ins = [r[...].astype(f32) for r in refs[:ncol]]
            ps = [r[...] for r in refs[ncol:ncol + npar]]
            refs[-1][...] = fn(*ins, *ps).astype(refs[-1].dtype)

        return pl.pallas_call(
            body, grid=(nb,), in_specs=[cspec(a) for a in cols + pars], out_specs=cspec(cols[0]),
            out_shape=jax.ShapeDtypeStruct((t, w), out_dtype), name=name, compiler_params=_cparams("parallel"))(*cols, *pars)

    def call_bwd(cols, pars, ct):
        def body(*refs):
            ins = [r[...].astype(f32) for r in refs[:ncol]]
            ps = [r[...] for r in refs[ncol:ncol + npar]]
            dy = refs[ncol + npar][...].astype(f32)
            d_refs = refs[ncol + npar + 1:]
            _, vjp = jax.vjp(fn, *ins, *ps)
            grads = vjp(dy)
            for d_ref, g in zip(d_refs, grads):
                d_ref[...] = g.astype(d_ref.dtype)

        return pl.pallas_call(
            body, grid=(nb,), in_specs=[cspec(a) for a in cols + pars + [ct]],
            out_specs=[cspec(a) for a in cols + pars],
            out_shape=[jax.ShapeDtypeStruct(a.shape, a.dtype) for a in cols + pars],
            name=name + "_bwd", compiler_params=_cparams("parallel"))(*cols, *pars, ct)

    @jax.custom_vjp
    def op(cols, pars):
        return call_fwd(list(cols), list(pars))

    def fwd(cols, pars):
        return op(cols, pars), (cols, pars)

    def bwd(res, ct):
        cols, pars = res
        g = call_bwd(list(cols), list(pars), ct)
        return tuple(g[:ncol]), tuple(g[ncol:])

    op.defvjp(fwd, bwd)
    return op(tuple(cols), tuple(pars))


def _sigmoid(x):
    return 1.0 / (1.0 + jnp.exp(-x))


def _silu(x):
    return x * _sigmoid(x)


def _rms(x, g):
    return x * lax.rsqrt(jnp.mean(x * x, axis=-1, keepdims=True) + EPS) * g


def _shift_rows(u, k, n_ctx):
    @jax.custom_vjp
    def op(u):
        t = u.shape[0]
        row = lax.broadcasted_iota(jnp.int32, u.shape, 0)
        edge = ((row == 0) | (row == n_ctx)) if k == 1 else ((row == n_ctx - 1) | (row == t - 1))
        return jnp.where(edge, 0.0, pltpu.roll(u, k % t, 0))

    op.defvjp(lambda u: (op(u), None), lambda _, g: (_shift_rows(g, -k, n_ctx),))
    return op(u)


def _dwconv(u, w0, w1, w2, b, n_ctx):
    return w0 * _shift_rows(u, 1, n_ctx) + w1 * u + w2 * _shift_rows(u, -1, n_ctx) + b


@jax.custom_vjp
def _swap_pairs(x):
    w = x.shape[1]
    lane = lax.broadcasted_iota(jnp.int32, x.shape, 1)
    return jnp.where(lane % 2 == 0, pltpu.roll(x, w - 1, 1), pltpu.roll(x, 1, 1))


_swap_pairs.defvjp(lambda x: (_swap_pairs(x), None), lambda _, g: (_swap_pairs(g),))


def _head_rms(x, g):
    w = x.shape[1]
    same = (lax.broadcasted_iota(jnp.int32, (w, w), 0) // HEAD_DIM) == (lax.broadcasted_iota(jnp.int32, (w, w), 1) // HEAD_DIM)
    ms = jnp.dot(x * x, same.astype(f32), precision=lax.Precision.HIGHEST, preferred_element_type=f32) * (1.0 / HEAD_DIM)
    return x * lax.rsqrt(ms + EPS) * g


def _band_ok(i, j, c0, shape, tq, tk):
    kpos = j * tk + lax.broadcasted_iota(jnp.int32, shape, 0)
    qpos = i * tq + (c0 + lax.broadcasted_iota(jnp.int32, shape, 1)) % tq
    return jnp.abs(qpos - kpos) <= WINDOW


def _sink_row(sink_ref, g, r, tq):
    return jnp.concatenate([jnp.full((1, tq), sink_ref[g * r + h], f32) for h in range(r)], axis=1)


def _heads_to_rows(x, r):
    return jnp.concatenate([x[:, HEAD_DIM * h:HEAD_DIM * (h + 1)] for h in range(r)], axis=0)


def _cols_to_heads(xt, r, tq):
    return jnp.concatenate([xt[:, tq * h:tq * (h + 1)].T for h in range(r)], axis=1)


def _attn_fwd_call(q, k, v, sink, window, name, shards=()):
    nkv, t, dh = k.shape
    h = q.shape[1] // dh
    r = h // nkv
    tq = tk = ROW_TILE
    nb = t // tq
    rows = r * tq
    ns = len(shards)

    assert window or nb % 2 == 1, "the dense schedule takes the kv chunks after the context chunk in pairs"

    def body(sink_ref, q_ref, k_ref, v_ref, *rest):
        x_refs, (o_ref, lse_ref), gathered_refs = rest[:ns], rest[ns:ns + 2], rest[ns + 2:2 * ns + 2]
        m_scr, l_scr, acc_scr, s_a, s_b, p_a, p_b, a_a, a_b = rest[2 * ns + 2:2 * ns + 11]
        comm_sems = rest[2 * ns + 11:]
        g, i = pl.program_id(0), pl.program_id(1)
        if ns:
            @pl.when((g == 0) & (i == 0))
            def _():
                _gather_steps(x_refs, gathered_refs, *comm_sems)[0]()

        qv = _heads_to_rows(q_ref[...], r)
        m_scr[...] = jnp.full_like(m_scr, NEG)
        l_scr[...] = jnp.zeros_like(l_scr)
        acc_scr[...] = jnp.zeros_like(acc_scr)

        def kv_rows(j):
            return pl.ds(pl.multiple_of(jnp.minimum(j, nb - 1) * tk, tk), tk)

        def scores(j, s_scr):
            s_scr[...] = lax.dot_general(k_ref[0, kv_rows(j), :], qv, NT, preferred_element_type=f32)

        def softmax(j, s_scr, p_scr, a_scr, masked):
            for cb in range(rows // ATTN_SLAB):
                cs = slice(cb * ATTN_SLAB, (cb + 1) * ATTN_SLAB)
                s = s_scr[:, cs]
                if masked is not False:
                    s = jnp.where(_band_ok(i, j, cb * ATTN_SLAB, s.shape, tq, tk) & masked, s, NEG)
                m = m_scr[:, cs]
                m2 = jnp.maximum(m, jnp.max(s, axis=0, keepdims=True))
                p = jnp.exp(s - m2)
                a = jnp.exp(m - m2)
                l_scr[:, cs] = a * l_scr[:, cs] + jnp.sum(p, axis=0, keepdims=True)
                m_scr[:, cs] = m2
                a_scr[:, cs] = a
                p_scr[:, cs] = p.astype(bf16)

        def weighted_v(j, p_scr, a_scr):
            acc_scr[...] = a_scr[...] * acc_scr[...] + lax.dot_general(v_ref[0, kv_rows(j), :], p_scr[...], TN, preferred_element_type=f32)

        scores(0, s_a)
        softmax(0, s_a, p_a, a_a, False)
        if window:
            cj = [jnp.clip(i + dj, 1, nb - 1) for dj in (-1, 0, 1)]
            on = [i >= 2, i >= 1, (i >= 1) & (i + 1 <= nb - 1)]
            scores(cj[0], s_b)
            weighted_v(0, p_a, a_a)
            softmax(cj[0], s_b, p_b, a_b, on[0])
            scores(cj[1], s_a)
            weighted_v(cj[0], p_b, a_b)
            softmax(cj[1], s_a, p_a, a_a, on[1])
            scores(cj[2], s_b)
            weighted_v(cj[1], p_a, a_a)
            softmax(cj[2], s_b, p_b, a_b, on[2])
            weighted_v(cj[2], p_b, a_b)
        else:
            scores(1, s_b)

            def pair(tt, c):
                j0 = 2 * tt + 1
                scores(j0 + 1, s_a)
                weighted_v(j0 - 1, p_a, a_a)
                softmax(j0, s_b, p_b, a_b, False)
                scores(j0 + 2, s_b)
                weighted_v(j0, p_b, a_b)
                softmax(j0 + 1, s_a, p_a, a_a, False)
                return c

            lax.fori_loop(0, jnp.where(i == 0, 0, (nb - 1) // 2), pair, 0)
            weighted_v(jnp.where(i == 0, 0, nb - 1), p_a, a_a)
        m, l, acc = m_scr[...], l_scr[...], acc_scr[...]
        if window:
            sk = _sink_row(sink_ref, g, r, tq)
            m2 = jnp.maximum(m, sk)
            a = jnp.exp(m - m2)
            l = a * l + jnp.exp(sk - m2)
            acc = a * acc
            m = m2
        o_ref[...] = _cols_to_heads(acc / l, r, tq).astype(o_ref.dtype)
        lse_ref[0] = m + jnp.log(l)
        if ns:
            @pl.when((g == nkv - 1) & (i == nb - 1))
            def _():
                _gather_steps(x_refs, gathered_refs, *comm_sems)[1]()

    qspec = pl.BlockSpec((tq, r * dh), lambda g, i: (i, g))
    kspec = pl.BlockSpec((1, t, dh), lambda g, i: (g, 0, 0))
    hbm = pl.BlockSpec(memory_space=pl.ANY)
    sem = ("arbitrary", "arbitrary") if ns else ("parallel", "parallel")
    return pl.pallas_call(
        body, grid=(nkv, nb),
        in_specs=[pl.BlockSpec(memory_space=pltpu.SMEM), qspec, kspec, kspec] + [hbm] * ns,
        out_specs=[qspec, pl.BlockSpec((1, 1, rows), lambda g, i: (g * nb + i, 0, 0))] + [hbm] * ns,
        out_shape=[jax.ShapeDtypeStruct((t, h * dh), bf16), jax.ShapeDtypeStruct((nkv * nb, 1, rows), f32)]
        + [jax.ShapeDtypeStruct((8,) + s.shape, s.dtype) for s in shards],
        scratch_shapes=[pltpu.VMEM((1, rows), f32), pltpu.VMEM((1, rows), f32), pltpu.VMEM((dh, rows), f32),
                        pltpu.VMEM((tk, rows), f32), pltpu.VMEM((tk, rows), f32), pltpu.VMEM((tk, rows), bf16),
                        pltpu.VMEM((tk, rows), bf16), pltpu.VMEM((1, rows), f32), pltpu.VMEM((1, rows), f32)]
        + (_gather_scratch(ns) if ns else []),
        name=name, compiler_params=_cparams(*sem))(sink, q, k, v, *shards)


def _attn_bwd_call(q, k, v, sink, o, lse, do, window, name, side_sums=()):
    nkv, t, dh = k.shape
    h = q.shape[1] // dh
    r = h // nkv
    tq = tk = ROW_TILE
    nb = t // tq
    rows = r * tq
    ns = len(side_sums)

    def body(sink_ref, q_ref, k_ref, v_ref, o_ref, lse_ref, do_ref, *rest):
        ss_refs, (dq_ref, dk_ref, dv_ref, dsink_ref), got_refs = rest[:ns], rest[ns:ns + 4], rest[ns + 4:2 * ns + 4]
        p_a, p_b, ds_a, ds_b, dq_scr = rest[2 * ns + 4:2 * ns + 9]
        comm_sems = rest[2 * ns + 9:]
        g, i = pl.program_id(0), pl.program_id(1)
        if ns:
            @pl.when((g == 0) & (i == 0))
            def _():
                for cp in _chips_copies(ss_refs, got_refs, *comm_sems):
                    cp.start()

        @pl.when(i == 0)
        def _():
            dk_ref[...] = jnp.zeros_like(dk_ref)
            dv_ref[...] = jnp.zeros_like(dv_ref)

        qv = _heads_to_rows(q_ref[...], r)
        dov = _heads_to_rows(do_ref[...], r)
        lse_t = lse_ref[0]
        delta_t = jnp.sum((dov.astype(f32) * _heads_to_rows(o_ref[...], r).astype(f32)).T, axis=0, keepdims=True)
        dq_scr[...] = jnp.zeros_like(dq_scr)
        q_t, do_t = qv.T, dov.T

        def kv_rows(j):
            return pl.ds(pl.multiple_of(jnp.minimum(j, nb - 1) * tk, tk), tk)

        def grads(j, p_scr, ds_scr):
            dv_ref[0, :, kv_rows(j)] += lax.dot_general(do_t, p_scr[...], NT, preferred_element_type=f32)
            dk_ref[0, :, kv_rows(j)] += lax.dot_general(q_t, ds_scr[...], NT, preferred_element_type=f32)
            dq_scr[...] += lax.dot_general(k_ref[0, kv_rows(j), :], ds_scr[...], TN, preferred_element_type=f32)

        def probs(j, p_scr, ds_scr, masked, before=None):
            s_all = lax.dot_general(k_ref[0, kv_rows(j), :], qv, NT, preferred_element_type=f32)
            dp_all = lax.dot_general(v_ref[0, kv_rows(j), :], dov, NT, preferred_element_type=f32)
            if before is not None:
                grads(*before)
            for cb in range(rows // ATTN_SLAB):
                cs = slice(cb * ATTN_SLAB, (cb + 1) * ATTN_SLAB)
                s = s_all[:, cs]
                if masked is not False:
                    s = jnp.where(_band_ok(i, j, cb * ATTN_SLAB, s.shape, tq, tk) & masked, s, NEG)
                p = jnp.exp(s - lse_t[:, cs])
                p_scr[:, cs] = p.astype(bf16)
                ds_scr[:, cs] = (p * (dp_all[:, cs] - delta_t[:, cs])).astype(bf16)

        probs(0, p_a, ds_a, False)
        if window:
            cj = [jnp.clip(i + dj, 1, nb - 1) for dj in (-1, 0, 1)]
            on = [i >= 2, i >= 1, (i >= 1) & (i + 1 <= nb - 1)]
            probs(cj[0], p_b, ds_b, on[0], before=(0, p_a, ds_a))
            probs(cj[1], p_a, ds_a, on[1], before=(cj[0], p_b, ds_b))
            probs(cj[2], p_b, ds_b, on[2], before=(cj[1], p_a, ds_a))
            grads(cj[2], p_b, ds_b)
        else:
            def pair(tt, c):
                j0 = 2 * tt + 1
                probs(j0, p_b, ds_b, False, before=(j0 - 1, p_a, ds_a))
                probs(j0 + 1, p_a, ds_a, False, before=(j0, p_b, ds_b))
                return c

            lax.fori_loop(0, jnp.where(i == 0, 0, (nb - 1) // 2), pair, 0)
            grads(jnp.where(i == 0, 0, nb - 1), p_a, ds_a)
        dq_ref[...] = _cols_to_heads(dq_scr[...], r, tq).astype(dq_ref.dtype)
        if window:
            dsink_ref[0] = -jnp.exp(_sink_row(sink_ref, g, r, tq) - lse_t) * delta_t
        else:
            dsink_ref[...] = jnp.zeros_like(dsink_ref)
        if ns:
            @pl.when((g == nkv - 1) & (i == nb - 1))
            def _():
                for cp in _chips_copies(ss_refs, got_refs, *comm_sems):
                    cp.wait()

    qspec = pl.BlockSpec((tq, r * dh), lambda g, i: (i, g))
    cspec = pl.BlockSpec((1, 1, rows), lambda g, i: (g * nb + i, 0, 0))
    kspec = pl.BlockSpec((1, t, dh), lambda g, i: (g, 0, 0))
    ktspec = pl.BlockSpec((1, dh, t), lambda g, i: (g, 0, 0))
    hbm = pl.BlockSpec(memory_space=pl.ANY)
    return pl.pallas_call(
        body, grid=(nkv, nb),
        in_specs=[pl.BlockSpec(memory_space=pltpu.SMEM), qspec, kspec, kspec, qspec, cspec, qspec] + [hbm] * ns,
        out_specs=[qspec, ktspec, ktspec, cspec] + [hbm] * ns,
        out_shape=[jax.ShapeDtypeStruct((t, h * dh), bf16), jax.ShapeDtypeStruct((nkv, dh, t), f32), jax.ShapeDtypeStruct((nkv, dh, t), f32),
                   jax.ShapeDtypeStruct((nkv * nb, 1, rows), f32)] + [jax.ShapeDtypeStruct((3,) + s.shape[1:], s.dtype) for s in side_sums],
        scratch_shapes=[pltpu.VMEM((tk, rows), bf16)] * 4 + [pltpu.VMEM((dh, rows), f32)]
        + ([pltpu.SemaphoreType.DMA((3 * ns,)), pltpu.SemaphoreType.DMA((3 * ns,))] if ns else []),
        name=name, compiler_params=_cparams("arbitrary" if ns else "parallel", "arbitrary"))(sink, q, k, v, o, lse, do, *side_sums)


def attention(q, k, v, sink, window, name, shards=(), stand_ins=()):
    @jax.custom_vjp
    def op(q, k, v, sink, shards, stand_ins):
        o, _, *gathered = _attn_fwd_call(q, k, v, sink, window, name, shards)
        return o, tuple(gathered)

    def fwd(q, k, v, sink, shards, stand_ins):
        o, lse, *gathered = _attn_fwd_call(q, k, v, sink, window, name, shards)
        return (o, tuple(gathered)), (q, k, v, sink, o, lse, shards)

    def bwd(res, cts):
        q, k, v, sink, o, lse, shards = res
        do, d_gathered = cts
        side_sums = []
        if shards:
            my_c = lax.axis_index("c").reshape(1).astype(jnp.int32)
            from_sibling = rs_to_sibling(list(d_gathered), name + "_rs_sibling")
            side_sums = [pair_sum(s, rr, my_c, f"{name}_pair_sum{a}") for a, (s, rr) in enumerate(zip(d_gathered, from_sibling))]
        dq, dk, dv, dsink_rows, *from_chips = _attn_bwd_call(q, k, v, sink, o, lse, do, window, name + "_bwd", side_sums)
        nkv, r = k.shape[0], q.shape[1] // (k.shape[0] * k.shape[2])
        dsink = jnp.sum(dsink_rows.reshape(nkv, -1, r, ROW_TILE), axis=(1, 3)).reshape(nkv * r)
        reduced = tuple(jnp.concatenate([s, fc, jnp.zeros_like(s[:1])], axis=0) for s, fc in zip(side_sums, from_chips))
        dk, dv = dk.transpose(0, 2, 1).astype(k.dtype), dv.transpose(0, 2, 1).astype(v.dtype)
        return dq, dk, dv, dsink, tuple(jnp.zeros_like(s) for s in shards), reduced

    op.defvjp(fwd, bwd)
    return op(q, k, v, sink, tuple(shards), tuple(stand_ins))


def _ssd_chunk(xs, dtx, dtr, ac, bs, cs, hin, rev):
    q = xs[0].shape[0]
    ii = lax.broadcasted_iota(jnp.int32, (q, q), 0)
    jj = lax.broadcasted_iota(jnp.int32, (q, q), 1)
    tri = (ii <= jj) if rev else (ii >= jj)
    lo = lax.broadcasted_iota(jnp.int32, (q, LANES), 1) < SSM_P
    lo_row = lax.broadcasted_iota(jnp.int32, (1, LANES), 1) < SSM_P
    heads, slabs, per_group = range(SSM_HEADS), range(SSM_HEADS // 2), SSM_HEADS // 2 // SSM_G

    a = [dtr[h] * ac[h] for h in heads]
    c = [jnp.sum(jnp.where(tri, jnp.broadcast_to(a[h], (q, q)), 0.0), axis=1, keepdims=True) for h in heads]
    tot = [jnp.sum(a[h], axis=1, keepdims=True) for h in heads]
    cf = [jnp.broadcast_to(c[h], (q, q)) for h in heads]
    seg = [jnp.minimum(cf[h] - cf[h].T, 0.0) for h in heads]
    decay = [jnp.where(tri, jnp.exp(seg[h]), 0.0) for h in heads]
    cb = [lax.dot_general(cs[g].astype(bf16), bs[g].astype(bf16), NT, preferred_element_type=f32) for g in range(SSM_G)]
    m = [jnp.concatenate([cb[j // per_group] * decay[2 * j], cb[j // per_group] * decay[2 * j + 1]], axis=1).astype(bf16) for j in slabs]
    xdt = [xs[j] * dtx[j] for j in slabs]
    x2 = [jnp.concatenate([jnp.where(lo, xdt[j], 0.0), jnp.where(lo, 0.0, xdt[j])], axis=0).astype(bf16) for j in slabs]
    y_diag = [jnp.dot(m[j], x2[j], preferred_element_type=f32) for j in slabs]
    csel = [jnp.where(lo, cf[2 * j], cf[2 * j + 1]) for j in slabs]
    tsel = [jnp.where(lo_row, jnp.broadcast_to(tot[2 * j], (1, LANES)), jnp.broadcast_to(tot[2 * j + 1], (1, LANES))) for j in slabs]
    xend = [(xdt[j] * jnp.exp(tsel[j] - csel[j])).astype(bf16) for j in slabs]
    st = [lax.dot_general(bs[j // per_group].astype(bf16), xend[j], TN, preferred_element_type=f32) for j in slabs]
    y_off = [jnp.dot(cs[j // per_group].astype(bf16), hin[j].astype(bf16), preferred_element_type=f32) * jnp.exp(csel[j]) for j in slabs]
    return [y_diag[j] + y_off[j] for j in slabs], [hin[j] * jnp.exp(tsel[j]) + st[j] for j in slabs]


def _ssd_order(s, nc, ncc, rev):
    if not rev:
        return s
    return jnp.where(s < ncc, ncc - 1 - s, nc - 1 - (s - ncc))


SSD_SLABS = [slice(LANES * j, LANES * (j + 1)) for j in range(SSM_HEADS // 2)]
SSD_GROUPS = [slice(SSM_N * g, SSM_N * (g + 1)) for g in range(SSM_G)]


def _head_lanes(w, transpose=False):
    shape = (w, SSM_HEADS) if transpose else (SSM_HEADS, w)
    head = lax.broadcasted_iota(jnp.int32, shape, 1 if transpose else 0)
    lane = lax.broadcasted_iota(jnp.int32, shape, 0 if transpose else 1)
    return (lane // SSM_P == head).astype(f32)


def _ssd_fwd_call(xs, dt, dtr, bm, cm, acol, rev, n_ctx, name):
    t, w = xs.shape
    q = SSM_Q
    nc, ncc = t // q, n_ctx // q

    def body(xs_ref, dt_ref, dtr_ref, b_ref, c_ref, a_ref, y_ref, hin_ref, h_scr):
        @pl.when(pl.program_id(0) == 0)
        def _():
            h_scr[...] = jnp.zeros_like(h_scr)

        hin_ref[0] = h_scr[...]
        dtx = jnp.dot(dt_ref[...], _head_lanes(w), precision=lax.Precision.HIGHEST, preferred_element_type=f32)
        ys, houts = _ssd_chunk([xs_ref[:, sl] for sl in SSD_SLABS], [dtx[:, sl] for sl in SSD_SLABS],
                               [dtr_ref[h:h + 1, :] for h in range(SSM_HEADS)], [a_ref[h:h + 1, :] for h in range(SSM_HEADS)],
                               [b_ref[:, gs] for gs in SSD_GROUPS], [c_ref[:, gs] for gs in SSD_GROUPS],
                               [h_scr[:, sl] for sl in SSD_SLABS], rev)
        for sl, y, hout in zip(SSD_SLABS, ys, houts):
            y_ref[:, sl] = y.astype(y_ref.dtype)
            h_scr[:, sl] = hout

    def at(s):
        return _ssd_order(s, nc, ncc, rev)

    return pl.pallas_call(
        body, grid=(nc,),
        in_specs=[pl.BlockSpec((q, w), lambda s: (at(s), 0)), pl.BlockSpec((q, SSM_HEADS), lambda s: (at(s), 0)),
                  pl.BlockSpec((SSM_HEADS, q), lambda s: (0, at(s))),
                  pl.BlockSpec((q, SSM_BC), lambda s: (at(s), 0)), pl.BlockSpec((q, SSM_BC), lambda s: (at(s), 0)),
                  pl.BlockSpec((SSM_HEADS, 1), lambda s: (0, 0))],
        out_specs=[pl.BlockSpec((q, w), lambda s: (at(s), 0)), pl.BlockSpec((1, SSM_N, w), lambda s: (s, 0, 0))],
        out_shape=[jax.ShapeDtypeStruct((t, w), xs.dtype), jax.ShapeDtypeStruct((nc, SSM_N, w), f32)],
        scratch_shapes=[pltpu.VMEM((SSM_N, w), f32)],
        name=name, compiler_params=_cparams("arbitrary"))(xs, dt, dtr, bm, cm, acol)


def _ssd_bwd_call(xs, dt, dtr, bm, cm, acol, hin, dy, rev, n_ctx, name):
    t, w = xs.shape
    q = SSM_Q
    nc, ncc = t // q, n_ctx // q

    def body(xs_ref, dt_ref, dtr_ref, b_ref, c_ref, a_ref, hin_ref, dy_ref,
             dxs_ref, ddt_ref, ddtr_ref, db_ref, dc_ref, da_ref, dh_scr):
        @pl.when(pl.program_id(0) == 0)
        def _():
            dh_scr[...] = jnp.zeros_like(dh_scr)
            da_ref[...] = jnp.zeros_like(da_ref)

        dtx = jnp.dot(dt_ref[...], _head_lanes(w), precision=lax.Precision.HIGHEST, preferred_element_type=f32)
        _, vjp = jax.vjp(
            functools.partial(_ssd_chunk, rev=rev),
            [xs_ref[:, sl].astype(f32) for sl in SSD_SLABS], [dtx[:, sl] for sl in SSD_SLABS],
            [dtr_ref[h:h + 1, :] for h in range(SSM_HEADS)], [a_ref[h:h + 1, :] for h in range(SSM_HEADS)],
            [b_ref[:, gs].astype(f32) for gs in SSD_GROUPS], [c_ref[:, gs].astype(f32) for gs in SSD_GROUPS],
            [hin_ref[0, :, sl] for sl in SSD_SLABS])
        dxs, ddtx, ddtr, dac, dbs, dcs, dhin = vjp(([dy_ref[:, sl].astype(f32) for sl in SSD_SLABS], [dh_scr[:, sl] for sl in SSD_SLABS]))
        for j, sl in enumerate(SSD_SLABS):
            dxs_ref[:, sl] = dxs[j].astype(dxs_ref.dtype)
            dh_scr[:, sl] = dhin[j]
        for h in range(SSM_HEADS):
            ddtr_ref[h:h + 1, :] = ddtr[h]
            da_ref[h:h + 1, :] += dac[h]
        for g, gs in enumerate(SSD_GROUPS):
            db_ref[:, gs] = dbs[g].astype(db_ref.dtype)
            dc_ref[:, gs] = dcs[g].astype(dc_ref.dtype)
        ddt_ref[...] = jnp.dot(jnp.concatenate(ddtx, axis=1), _head_lanes(w, transpose=True),
                               precision=lax.Precision.HIGHEST, preferred_element_type=f32)

    def step(s):
        return nc - 1 - s

    def at(s):
        return _ssd_order(step(s), nc, ncc, rev)

    row = lambda wd: pl.BlockSpec((q, wd), lambda s: (at(s), 0))
    dtr_spec = pl.BlockSpec((SSM_HEADS, q), lambda s: (0, at(s)))
    a_spec = pl.BlockSpec((SSM_HEADS, 1), lambda s: (0, 0))
    return pl.pallas_call(
        body, grid=(nc,),
        in_specs=[row(w), row(SSM_HEADS), dtr_spec, row(SSM_BC), row(SSM_BC), a_spec,
                  pl.BlockSpec((1, SSM_N, w), lambda s: (step(s), 0, 0)), row(w)],
        out_specs=[row(w), row(SSM_HEADS), dtr_spec, row(SSM_BC), row(SSM_BC), a_spec],
        out_shape=[jax.ShapeDtypeStruct((t, w), xs.dtype), jax.ShapeDtypeStruct(dt.shape, f32), jax.ShapeDtypeStruct(dtr.shape, f32),
                   jax.ShapeDtypeStruct(bm.shape, bm.dtype), jax.ShapeDtypeStruct(cm.shape, cm.dtype), jax.ShapeDtypeStruct(acol.shape, f32)],
        scratch_shapes=[pltpu.VMEM((SSM_N, w), f32)],
        name=name, compiler_params=_cparams("arbitrary"))(xs, dt, dtr, bm, cm, acol, hin, dy)


def ssd_scan(xs, dt, dtr, bm, cm, acol, rev, n_ctx, name):
    @jax.custom_vjp
    def op(xs, dt, dtr, bm, cm, acol):
        return _ssd_fwd_call(xs, dt, dtr, bm, cm, acol, rev, n_ctx, name)[0]

    def fwd(xs, dt, dtr, bm, cm, acol):
        y, hin = _ssd_fwd_call(xs, dt, dtr, bm, cm, acol, rev, n_ctx, name)
        return y, (xs, dt, dtr, bm, cm, acol, hin)

    def bwd(res, dy):
        return tuple(_ssd_bwd_call(*res, dy, rev, n_ctx, name + "_bwd"))

    op.defvjp(fwd, bwd)
    return op(xs, dt, dtr, bm, cm, acol)


def _in_layout(d):
    return [('a_q', Q_W), ('a_k', KV_W), ('a_v', KV_W), ('b_z', SSM_INNER), ('b_xbc', SSM_INNER + 2 * SSM_BC), ('b_dt', DT_W),
            ('c_q', Q_W), ('c_k', KV_W), ('c_v', KV_W), ('g_a', d), ('g_b', d), ('g_c', d)]


def _dt_span(d):
    start = 0
    for name, n in _in_layout(d):
        if name == 'b_dt':
            return start, start + n
        start += n


@jax.custom_vjp
def _w_in_split(w):
    lo, hi = _dt_span(w.shape[0])
    dt = jnp.concatenate([w[:, lo:hi], jnp.zeros((w.shape[0], DT_PAD - (hi - lo)), w.dtype)], axis=1)
    return jnp.concatenate([w[:, :lo], w[:, hi:]], axis=1), dt


def _w_in_join(g_main, g_dt):
    lo, hi = _dt_span(g_main.shape[0])
    return jnp.concatenate([g_main[:, :lo], g_dt[:, :hi - lo], g_main[:, lo:]], axis=1)


_w_in_split.defvjp(lambda w: (_w_in_split(w), None), lambda _, g: (_w_in_join(*g),))


def _rope_tables(n_ctx, n_lat):
    rows = n_lat // GRID_W
    t_row = jnp.repeat(jnp.arange(rows), GRID_W).astype(f32)
    t_col = jnp.tile(jnp.arange(GRID_W), rows).astype(f32)
    n = HEAD_DIM // 4
    inv = ROPE_BASE ** (-jnp.arange(n, dtype=f32) / n)
    ang = jnp.concatenate([t_row[:, None] * inv, t_col[:, None] * inv], axis=-1)
    cos = jnp.concatenate([jnp.ones((n_ctx, HEAD_DIM // 2), f32), jnp.cos(ang)], axis=0)
    sin = jnp.concatenate([jnp.zeros((n_ctx, HEAD_DIM // 2), f32), jnp.sin(ang)], axis=0)
    return jnp.repeat(cos, 2, axis=1), jnp.stack([-sin, sin], axis=-1).reshape(sin.shape[0], HEAD_DIM)


def _heads_major(a, n_heads):
    return a.reshape(a.shape[0], n_heads, HEAD_DIM).transpose(1, 0, 2)


def _layer(xall, w, s, cm, tabs, n_ctx, li, gather_a, gather_c):
    t, d = xall.shape
    ncb = n_ctx // ROW_TILE
    nm = f"l{li}_"
    ctq, stq, ctk, stk = tabs
    def mod(blk, cmv, i):
        return jnp.where(blk < ncb, cmv[0:1, i * d:(i + 1) * d], cmv[1:2, i * d:(i + 1) * d])

    def norm_mod(blk, x, g, cmv):
        return (_rms(x, g) * (1.0 + mod(blk, cmv, 1)) + mod(blk, cmv, 0),)

    (h,) = rowwise(norm_mod, [xall], [], [s['norm1'][None], cm], [d], [bf16], nm + "norm1")
    w_main, w_dt = _w_in_split(w['w_in'])
    u = mm(h, w_main, nm + "in")
    b_dt = mm(h, w_dt, nm + "in_dt", f32)
    a_q, a_k, a_v, b_z, b_xbc, c_q, c_k, c_v, g_a, g_b, g_c = split_cols(u, [n for name, n in _in_layout(d) if name != 'b_dt'])

    def rope(blk, q, k, v, ct_q, st_q, ct_k, st_k):
        return q * ct_q + _swap_pairs(q) * st_q, k * ct_k + _swap_pairs(k) * st_k, v

    def norm_rope(blk, q, k, v, ct_q, st_q, ct_k, st_k, gq, gk):
        return rope(blk, _head_rms(q, gq), _head_rms(k, gk), v, ct_q, st_q, ct_k, st_k)

    qkv_w, qkv_t = [Q_W, KV_W, KV_W], [bf16, bf16, bf16]
    qa, ka, va = rowwise(rope, [a_q, a_k, a_v], [ctq, stq, ctk, stk], [], qkv_w, qkv_t, nm + "ropeA")
    gq = jnp.tile(s['c_q_norm'], N_HEADS)[None]
    gk = jnp.tile(s['c_k_norm'], N_KV)[None]
    qc, kc, vc = rowwise(norm_rope, [c_q, c_k, c_v], [ctq, stq, ctk, stk], [gq, gk], qkv_w, qkv_t, nm + "ropeC")
    ya, mine = attention(qa, _heads_major(ka, N_KV), _heads_major(va, N_KV), s['a_sink'], True, nm + "attnA", *gather_a[1:])
    yc, nxt = attention(qc, _heads_major(kc, N_KV), _heads_major(vc, N_KV), jnp.zeros((N_HEADS,), f32), False,
                        nm + "attnC", *gather_c[1:])
    w = {**w, **{n: _assemble(n, g) for n, g in zip(gather_a[0], mine)}}
    w_next = {n: _assemble(n, g) for n, g in zip(gather_c[0], nxt)}

    cw, cb = s['ssm_conv_w'], s['ssm_conv_b']
    conv_silu = lambda uu, w3, b: _silu(_dwconv(uu, w3[0:1], w3[1:2], w3[2:3], b, n_ctx))
    xbc = colwise(conv_silu, [b_xbc], [cw, cb[None]], bf16, nm + "ssmconv")
    xs, bm, cmat = split_cols(xbc, [SSM_INNER, SSM_BC, SSM_BC])
    bias = jnp.concatenate([s['ssm_dt_bias'].reshape(1, DT_W), jnp.zeros((1, DT_PAD - DT_W), f32)], axis=1)

    def softplus(blk, r, b):
        z = r + b
        return (jnp.maximum(z, 0.0) + jnp.log(1.0 + jnp.exp(-jnp.abs(z))),)

    (dt_all,) = rowwise(softplus, [b_dt], [], [bias], [DT_PAD], [f32], nm + "dt")
    a_coef = -jnp.exp(s['ssm_A_log'])
    ys_dir = []
    for di, rev in enumerate((False, True)):
        dt = dt_all[:, di * SSM_HEADS:(di + 1) * SSM_HEADS]
        ys_dir.append(ssd_scan(xs, dt, dt.T, bm, cmat, a_coef[di][:, None], rev, n_ctx,
                               nm + ("ssd_r" if rev else "ssd_f")))

    def ssm_out(blk, yf, yb, x, z, dskip, g):
        return (_rms((yf + yb + x * dskip) * _silu(z), g),)

    (ysn,) = rowwise(ssm_out, [ys_dir[0], ys_dir[1], xs, b_z], [], [jnp.repeat(s['ssm_D'], SSM_P)[None], s['ssm_norm'][None]],
                     [SSM_INNER], [bf16], nm + "ssmout")

    pa, pb, pc = mm(ya, w['w_oa'], nm + "oa"), mm(ysn, w['w_ob'], nm + "ob"), mm(yc, w['w_oc'], nm + "oc")

    def merge(blk, ga, gb, gc, a, b, c):
        return (_sigmoid(ga) * a + _sigmoid(gb) * b + _sigmoid(gc) * c,)

    (mrg,) = rowwise(merge, [g_a, g_b, g_c, pa, pb, pc], [], [], [d], [bf16], nm + "merge")
    o = mm(mrg, w['w_out'], nm + "out")

    def resid_norm_mod(blk, x, oo, g, cmv):
        x1 = x + mod(blk, cmv, 2) * oo
        return x1, _rms(x1, g) * (1.0 + mod(blk, cmv, 4)) + mod(blk, cmv, 3)

    x1, h2 = rowwise(resid_norm_mod, [xall, o], [], [s['norm2'][None], cm], [d, d], [f32, bf16], nm + "norm2")
    up, gt = mm(h2, w['ffn_w_up'], nm + "up"), mm(h2, w['ffn_w_gate'], nm + "gate")
    fw, fb = s['ffn_conv_w'], s['ffn_conv_b']
    ffn_act = lambda g_, u_, w3, b: _silu(_dwconv(g_, w3[0:1], w3[1:2], w3[2:3], b, n_ctx)) * u_
    act = colwise(ffn_act, [gt, up], [fw, fb[None]], bf16, nm + "ffnact")
    f = mm(act, w['ffn_w_down'], nm + "down")

    def resid(blk, x, ff, cmv):
        return (x + mod(blk, cmv, 5) * ff,)

    (x2,) = rowwise(resid, [x1, f], [], [cm], [d], [f32], nm + "resid")
    return x2, w_next


def _assemble(name, gathered):
    if BIG[name] == 0:
        return gathered.reshape(-1, gathered.shape[-1])
    return jnp.concatenate([gathered[j] for j in range(8)], axis=1)


def _loss_fn(big0, shards0_late, stand_ins0, shards1, stand_ins1, small, x, ctx, c, target, n_ctx):
    n_lat, d = x.shape
    xall = jnp.concatenate([ctx, x], axis=0)
    ct, st = _rope_tables(n_ctx, n_lat)
    tabs = (jnp.tile(ct, (1, N_HEADS)) * HEAD_DIM ** -0.5, jnp.tile(st, (1, N_HEADS)) * HEAD_DIM ** -0.5,
            jnp.tile(ct, (1, N_KV)), jnp.tile(st, (1, N_KV)))
    srows = jnp.concatenate([_silu(small['c_ctx'])[None], _silu(c), jnp.zeros((14, d), f32)], axis=0)
    names = list(BIG)
    big = big0
    gather_a = (LATE, [shards0_late[n] for n in LATE], [stand_ins0[n] for n in LATE])
    gather_c = (names, [shards1[n] for n in names], [stand_ins1[n] for n in names])
    for li in range(2):
        cm = mm(srows, big['w_mod'], f"l{li}_mod", f32)[0:2] + small['b_mod'][li][None]
        sl = {k: v[li] for k, v in small.items() if k not in ('c_ctx', 'final_norm')}
        xall, big = _layer(xall, big, sl, cm, tabs, n_ctx, li, gather_a, gather_c)
        gather_a = gather_c = ((), (), ())
    ncb = n_ctx // ROW_TILE
    tgt = jnp.concatenate([jnp.zeros((n_ctx, d), f32), target], axis=0)

    def loss_rows(blk, xx, tg, g):
        e = _rms(xx, g) - tg
        return (jnp.where(blk < ncb, 0.0, 0.5) * jnp.mean(e * e, axis=-1, keepdims=True),)

    (rows,) = rowwise(loss_rows, [xall], [tgt], [small['final_norm'][None]], [1], [f32], "loss")
    return jnp.sum(rows)


def _hbm_call(body, ins, out_shapes, n_sems, name):
    any_spec = pl.BlockSpec(memory_space=pl.ANY)
    return pl.pallas_call(
        body, out_shape=out_shapes, in_specs=[any_spec] * len(ins), out_specs=[any_spec] * len(out_shapes),
        scratch_shapes=[pltpu.SemaphoreType.DMA((n_sems,)), pltpu.SemaphoreType.DMA((n_sems,)), pltpu.SemaphoreType.DMA((len(ins),))],
        name=name)(*ins)


def _gather_steps(x_refs, out_refs, send_sems, recv_sems, local_sems):
    n = len(x_refs)
    x, y, c = lax.axis_index("x"), lax.axis_index("y"), lax.axis_index("c")
    me, sibling = (x, y, c), (x, y, 1 - c)
    chips = [(1 - x, y), (x, 1 - y), (1 - x, 1 - y)]

    def copy(a, k, block, to, src=None):
        px, py, pc = block
        slot = out_refs[a].at[4 * px + 2 * py + pc]
        return pltpu.make_async_remote_copy(
            src_ref=slot if src is None else src, dst_ref=slot,
            send_sem=send_sems.at[7 * a + k], recv_sem=recv_sems.at[7 * a + k], device_id=to, device_id_type=MESH)

    mine = [pltpu.make_async_copy(x_refs[a], out_refs[a].at[4 * x + 2 * y + c], local_sems.at[a]) for a in range(n)]
    first = []
    for a in range(n):
        first += [copy(a, 1 + j, me, (*chip, c), src=x_refs[a]) for j, chip in enumerate(chips)]
        first.append(copy(a, 0, me, sibling, src=x_refs[a]))

    def start():
        for cp in mine + first:
            cp.start()

    def finish():
        passed = []
        for a in range(n):
            for j, chip in enumerate(chips):
                copy(a, 1 + j, (*chip, c), me).wait_recv()
                passed.append(copy(a, 4 + j, (*chip, c), sibling))
                passed[-1].start()
        for a in range(n):
            copy(a, 0, sibling, me).wait_recv()
            for j, chip in enumerate(chips):
                copy(a, 4 + j, (*chip, 1 - c), me).wait_recv()
        for cp in first + passed:
            cp.wait_send()
        for cp in mine:
            cp.wait()

    return start, finish


def _gather_scratch(n):
    return [pltpu.SemaphoreType.DMA((7 * n,)), pltpu.SemaphoreType.DMA((7 * n,)), pltpu.SemaphoreType.DMA((n,))]


def all_gather(shards, name):
    n = len(shards)

    def body(*refs):
        start, finish = _gather_steps(refs[:n], refs[n:2 * n], *refs[2 * n:])
        start()
        finish()

    return _hbm_call(body, shards, [jax.ShapeDtypeStruct((8,) + s.shape, s.dtype) for s in shards], 7 * n, name)


def rs_to_sibling(gs, name="rs_sibling"):
    n = len(gs)

    def body(*refs):
        g_refs, out_refs, (send_sems, recv_sems, _) = refs[:n], refs[n:2 * n], refs[2 * n:]
        x, y, c = lax.axis_index("x"), lax.axis_index("y"), lax.axis_index("c")
        copies = [pltpu.make_async_remote_copy(
            src_ref=g_refs[a].at[2 * k + (1 - c)], dst_ref=out_refs[a].at[k], send_sem=send_sems.at[4 * a + k],
            recv_sem=recv_sems.at[4 * a + k], device_id=(x, y, 1 - c), device_id_type=MESH) for a in range(n) for k in range(4)]
        for cp in copies:
            cp.start()
        for cp in copies:
            cp.wait()

    return _hbm_call(body, gs, [jax.ShapeDtypeStruct((4,) + g.shape[1:], g.dtype) for g in gs], 4 * n, name)


def rs_to_chips(ss):
    n = len(ss)

    def body(*refs):
        copies = _chips_copies(refs[:n], refs[n:2 * n], refs[2 * n], refs[2 * n + 1])
        for cp in copies:
            cp.start()
        for cp in copies:
            cp.wait()

    return _hbm_call(body, ss, [jax.ShapeDtypeStruct((3,) + s.shape[1:], s.dtype) for s in ss], 3 * n, "rs_chips")


def _chips_copies(s_refs, out_refs, send_sems, recv_sems):
    x, y, c = lax.axis_index("x"), lax.axis_index("y"), lax.axis_index("c")
    copies = []
    for a in range(len(s_refs)):
        for k, (fx, fy) in enumerate([(1, 0), (0, 1), (1, 1)]):
            px, py = (1 - x) if fx else x, (1 - y) if fy else y
            copies.append(pltpu.make_async_remote_copy(
                src_ref=s_refs[a].at[2 * px + py], dst_ref=out_refs[a].at[k], send_sem=send_sems.at[3 * a + k],
                recv_sem=recv_sems.at[3 * a + k], device_id=(px, py, c), device_id_type=MESH))
    return copies


def _flat_tile(rows, cols):
    return _row_tile(rows, 4 * 4 * cols)


def pair_sum(g, r1, my_c, name):
    _, rows, cols = g.shape
    tm = _flat_tile(rows, cols)

    def body(c_ref, g_ref, r_ref, o_ref):
        o_ref[...] = (g_ref[...].astype(f32) + r_ref[...].astype(f32)).astype(o_ref.dtype)

    return pl.pallas_call(
        body, grid_spec=pltpu.PrefetchScalarGridSpec(
            num_scalar_prefetch=1, grid=(4, rows // tm),
            in_specs=[pl.BlockSpec((1, tm, cols), lambda k, i, c: (2 * k + c[0], i, 0)),
                      pl.BlockSpec((1, tm, cols), lambda k, i, c: (k, i, 0))],
            out_specs=pl.BlockSpec((1, tm, cols), lambda k, i, c: (k, i, 0))),
        out_shape=jax.ShapeDtypeStruct((4, rows, cols), g.dtype), name=name,
        compiler_params=_cparams("parallel", "parallel"))(my_c, g, r1)


def _adam_math(w, g, m, v):
    m2 = ADAM_B1 * m + (1.0 - ADAM_B1) * g
    v2 = ADAM_B2 * v + (1.0 - ADAM_B2) * (g * g)
    m_hat = m2 / (1.0 - ADAM_B1 ** ADAM_STEP)
    v_hat = v2 / (1.0 - ADAM_B2 ** ADAM_STEP)
    return -ADAM_LR * (m_hat / (jnp.sqrt(v_hat) + ADAM_EPS) + ADAM_WD * w), m2, v2


def sum_adam(parts, w, m, v, name):
    groups, rows, cols = w.shape
    tm = _flat_tile(rows, cols)
    nblk = rows // tm
    flat = []
    scalars = [p[2] for ps in parts for p in ps if p[2] is not None]
    for gi, ps in enumerate(parts):
        flat.append([])
        for arr, static_rows, dyn in ps:
            if dyn is not None:
                flat[gi].append((arr, functools.partial(lambda l, i, s, gi: (s[0], jnp.where(l == gi, i, nblk - 1), 0), gi=gi)))
            else:
                for k in static_rows:
                    flat[gi].append((arr, functools.partial(lambda l, i, s, gi, k: (k, jnp.where(l == gi, i, nblk - 1), 0), gi=gi, k=k)))
    counts = [len(f) for f in flat]
    na = sum(counts)

    def body(s_ref, *refs):
        sums, at = [], 0
        for cnt in counts:
            g = refs[at][0].astype(f32)
            for r in refs[at + 1:at + cnt]:
                g = g + r[0].astype(f32)
            sums.append(g)
            at += cnt
        g = sums[0]
        for gi in range(1, groups):
            g = jnp.where(pl.program_id(0) == gi, sums[gi], g)
        w_ref, m_ref, v_ref = refs[na:na + 3]
        g_out, d_out, m_out, v_out = refs[na + 3:]
        d, m2, v2 = _adam_math(w_ref[0], g, m_ref[0], v_ref[0])
        g_out[0] = g
        d_out[0] = d
        m_out[0] = m2
        v_out[0] = v2

    blk = pl.BlockSpec((1, tm, cols), lambda l, i, s: (l, i, 0))
    scalar = scalars[0] if scalars else jnp.zeros((1,), jnp.int32)
    return pl.pallas_call(
        body, grid_spec=pltpu.PrefetchScalarGridSpec(
            num_scalar_prefetch=1, grid=(groups, nblk),
            in_specs=[pl.BlockSpec((1, tm, cols), im) for f in flat for _, im in f] + [blk, blk, blk],
            out_specs=[blk, blk, blk, blk]),
        out_shape=[jax.ShapeDtypeStruct((groups, rows, cols), f32)] * 4, name=name,
        compiler_params=_cparams("arbitrary", "arbitrary"))(scalar, *[a for f in flat for a, _ in f], w, m, v)


FLAT_COLS = 1024


def _to_flat(vec):
    n = vec.shape[0]
    total = -(-n // (8 * FLAT_COLS)) * 8 * FLAT_COLS
    return jnp.concatenate([vec, jnp.zeros((total - n,), vec.dtype)]).reshape(-1, FLAT_COLS)


def _pack(tree, names):
    return jnp.concatenate([tree[n].reshape(-1) for n in names])


def _unpack(vec, like, names):
    out, off = {}, 0
    for n in names:
        size = like[n].size
        out[n] = vec[off:off + size].reshape(like[n].shape)
        off += size
    return out


def kernel(x, c, ctx, c_ctx, w_mod, b_mod, norm1, norm2, w_in, a_sink, ssm_conv_w, ssm_conv_b, ssm_A_log, ssm_dt_bias, ssm_D, ssm_norm, c_q_norm, c_k_norm, w_oa, w_ob, w_oc, w_out, ffn_w_up, ffn_w_gate, ffn_conv_w, ffn_conv_b, ffn_w_down, final_norm, loss_target, m_c_ctx, m_w_mod, m_b_mod, m_norm1, m_norm2, m_w_in, m_a_sink, m_ssm_conv_w, m_ssm_conv_b, m_ssm_A_log, m_ssm_dt_bias, m_ssm_D, m_ssm_norm, m_c_q_norm, m_c_k_norm, m_w_oa, m_w_ob, m_w_oc, m_w_out, m_ffn_w_up, m_ffn_w_gate, m_ffn_conv_w, m_ffn_conv_b, m_ffn_w_down, m_final_norm, v_c_ctx, v_w_mod, v_b_mod, v_norm1, v_norm2, v_w_in, v_a_sink, v_ssm_conv_w, v_ssm_conv_b, v_ssm_A_log, v_ssm_dt_bias, v_ssm_D, v_ssm_norm, v_c_q_norm, v_c_k_norm, v_w_oa, v_w_ob, v_w_oc, v_w_out, v_ffn_w_up, v_ffn_w_gate, v_ffn_conv_w, v_ffn_conv_b, v_ffn_w_down, v_final_norm):
    args = (x, c, ctx, c_ctx, w_mod, b_mod, norm1, norm2, w_in, a_sink, ssm_conv_w, ssm_conv_b, ssm_A_log, ssm_dt_bias, ssm_D, ssm_norm, c_q_norm, c_k_norm, w_oa, w_ob, w_oc, w_out, ffn_w_up, ffn_w_gate, ffn_conv_w, ffn_conv_b, ffn_w_down, final_norm, loss_target)
    moms = (m_c_ctx, m_w_mod, m_b_mod, m_norm1, m_norm2, m_w_in, m_a_sink, m_ssm_conv_w, m_ssm_conv_b, m_ssm_A_log, m_ssm_dt_bias, m_ssm_D, m_ssm_norm, m_c_q_norm, m_c_k_norm, m_w_oa, m_w_ob, m_w_oc, m_w_out, m_ffn_w_up, m_ffn_w_gate, m_ffn_conv_w, m_ffn_conv_b, m_ffn_w_down, m_final_norm)
    vars_ = (v_c_ctx, v_w_mod, v_b_mod, v_norm1, v_norm2, v_w_in, v_a_sink, v_ssm_conv_w, v_ssm_conv_b, v_ssm_A_log, v_ssm_dt_bias, v_ssm_D, v_ssm_norm, v_c_q_norm, v_c_k_norm, v_w_oa, v_w_ob, v_w_oc, v_w_out, v_ffn_w_up, v_ffn_w_gate, v_ffn_conv_w, v_ffn_conv_b, v_ffn_w_down, v_final_norm)
    p = dict(zip(IN_NAMES, args))
    mom = dict(zip(WEIGHTS, moms))
    var = dict(zip(WEIGHTS, vars_))
    depth = w_in.shape[0]
    n_ctx = ctx.shape[1]
    xi, yi, ci = lax.axis_index("x"), lax.axis_index("y"), lax.axis_index("c")
    dev = 4 * xi + 2 * yi + ci
    big_names = list(BIG)

    assert depth == 2 and n_ctx == ROW_TILE
    shards = [{n: p[n][li].astype(bf16) for n in big_names} for li in range(depth)]
    g_early = all_gather([shards[0][n] for n in EARLY], "gather_l0")
    g_conv = all_gather([_to_flat(_pack(p, CONV_W))], "gather_conv")[0].reshape(8, -1)
    big0 = {n: _assemble(n, g) for n, g in zip(EARLY, g_early)}
    stand_ins = [{n: jnp.zeros((8,) + shards[li][n].shape, bf16) for n in (LATE, big_names)[li]} for li in range(depth)]
    conv_full, off = {}, 0
    for n in CONV_W:
        shp = p[n].shape
        seg = g_conv[:, off:off + p[n].size].reshape(8, *shp)
        conv_full[n] = jnp.moveaxis(seg, 0, -2).reshape(*shp[:-1], 8 * shp[-1])
        off += p[n].size
    small = {n: p[n] for n in REPL}
    small.update(conv_full)

    loss, (g_early, g_late0, g_big1, g_small, g_x) = jax.value_and_grad(_loss_fn, argnums=(0, 2, 4, 5, 6))(
        big0, {n: shards[0][n] for n in LATE}, stand_ins[0], shards[1], stand_ins[1], small, x[0], ctx[0], c, loss_target[0], n_ctx)
    loss = lax.psum(loss, AXES)

    def send_rows(n):
        b = p[n].shape[-1]
        return jnp.stack([g_early[n][:, b * j:b * (j + 1)] for j in range(8)])

    send = [send_rows(n) for n in EARLY]
    from_sibling = rs_to_sibling(send)
    my_c = ci.reshape(1).astype(jnp.int32)
    side_sum = [pair_sum(s, r, my_c, "rs_pair_sum_" + n) for n, s, r in zip(EARLY, send, from_sibling)]
    from_chips = rs_to_chips(side_sum)
    chip = (2 * xi + yi).reshape(1).astype(jnp.int32)
    big_out = [{}, {}, {}, {}]
    for n in big_names:
        if n in EARLY:
            a = EARLY.index(n)
            parts = [[(side_sum[a], None, chip), (from_chips[a], (0, 1, 2), None)]]
        else:
            parts = [[(g_late0[n], None, chip), (g_late0[n], (4, 5, 6), None)]]
        parts.append([(g_big1[n], None, chip), (g_big1[n], (4, 5, 6), None)])
        outs = sum_adam(parts, p[n], mom[n], var[n], "adam_" + n)
        for k in range(4):
            big_out[k][n] = outs[k]

    sm_names = REPL + list(CONV_W)
    g_vec = _to_flat(_pack(g_small, sm_names))
    gathered = all_gather([g_vec], "gather_small_grads")[0]
    n_repl = sum(p[n].size for n in REPL)

    def repl_flat(tree):
        return _to_flat(jnp.concatenate([_pack(tree, REPL), jnp.zeros((g_vec.size - n_repl,), f32)]))

    outs_small = sum_adam([[(gathered, tuple(range(8)), None)]], repl_flat(p)[None], repl_flat(mom)[None], repl_flat(var)[None],
                          "adam_small")
    g_sum = outs_small[0].reshape(-1)
    small_out = [_unpack(o.reshape(-1), p, REPL) for o in outs_small]
    conv_g_full = _unpack(g_sum[n_repl:], conv_full, CONV_W)
    conv_g = {n: lax.dynamic_slice_in_dim(conv_g_full[n], dev * p[n].shape[-1], p[n].shape[-1], axis=2) for n in CONV_W}
    conv_gv = _to_flat(_pack(conv_g, CONV_W))
    outs_conv = sum_adam([[(conv_gv[None], (0,), None)]], _to_flat(_pack(p, CONV_W))[None], _to_flat(_pack(mom, CONV_W))[None],
                         _to_flat(_pack(var, CONV_W))[None], "adam_conv")
    conv_out = [_unpack(o.reshape(-1), p, CONV_W) for o in outs_conv]

    res = []
    for k in range(4):
        tree = {**big_out[k], **small_out[k], **conv_out[k]}
        res.append([tree[n] for n in WEIGHTS])
    return (loss, g_x[None], *res[0], *res[1], *res[2], *res[3])
```

```python
import functools

import jax
import jax.numpy as jnp
from jax import lax
from jax.experimental import pallas as pl
from jax.experimental.pallas import tpu as pltpu

f32 = jnp.float32
bf16 = jnp.bfloat16
MESH = pl.DeviceIdType.MESH
AXES = ("x", "y", "c")

GRID_W = 64
HEAD_DIM = 64
ROPE_BASE = 10000.0
EPS = 1e-6
WINDOW = 128
N_HEADS = 8
N_KV = 2
SSM_HEADS = 16
SSM_P = 64
SSM_G = 2
SSM_N = 128
SSM_INNER = SSM_HEADS * SSM_P
SSM_BC = SSM_G * SSM_N
SSM_Q = 128
Q_W = N_HEADS * HEAD_DIM
KV_W = N_KV * HEAD_DIM
DT_W = 2 * SSM_HEADS
DT_PAD = 128
ADAM_LR, ADAM_B1, ADAM_B2, ADAM_EPS, ADAM_WD, ADAM_STEP = 0.001, 0.9, 0.999, 1e-08, 0.01, 10

LANES = 128
ROW_TILE = 256
VMEM_BLOCK_BUDGET = 6 * 1024 * 1024
ATTN_SLAB = 128
MM_ROW_CAP = 1088
MM_TILE_CAP = 1536
NEG = -1e30
LOG2E = 1.4426950408889634
LN2 = 0.6931471805599453

IN_NAMES = ['x', 'c', 'ctx', 'c_ctx', 'w_mod', 'b_mod', 'norm1', 'norm2', 'w_in', 'a_sink', 'ssm_conv_w', 'ssm_conv_b', 'ssm_A_log', 'ssm_dt_bias', 'ssm_D', 'ssm_norm', 'c_q_norm', 'c_k_norm', 'w_oa', 'w_ob', 'w_oc', 'w_out', 'ffn_w_up', 'ffn_w_gate', 'ffn_conv_w', 'ffn_conv_b', 'ffn_w_down', 'final_norm', 'loss_target']
WEIGHTS = IN_NAMES[3:28]
BIG = {'w_mod': 1, 'w_in': 1, 'w_oa': 1, 'w_ob': 0, 'w_oc': 1, 'w_out': 0, 'ffn_w_up': 1, 'ffn_w_gate': 1, 'ffn_w_down': 0}
EARLY = ['w_mod', 'w_in']
LATE = [n for n in BIG if n not in EARLY]
CONV_W = ('ssm_conv_w', 'ffn_conv_w')
REPL = [n for n in WEIGHTS if n not in BIG and n not in CONV_W]

NT = (((1,), (1,)), ((), ()))
TN = (((0,), (0,)), ((), ()))
NN = (((1,), (0,)), ((), ()))


def _cparams(*sem):
    return pltpu.CompilerParams(dimension_semantics=sem)


def _div_tile(n, unit, cap):
    for d in range(min(n, int(cap)), 0, -1):
        if n % d == 0 and d % unit == 0:
            return d
    return n


def _row_tile(m, row_bytes):
    return _div_tile(m, 16, max(16, VMEM_BLOCK_BUDGET // row_bytes))


def _mm_call(a, b, mode, out_dtype, name):
    if mode == "nn":
        (m, k), n = a.shape, b.shape[1]
    elif mode == "nt":
        (m, k), n = a.shape, b.shape[0]
    else:
        (k, m), n = a.shape, b.shape[1]
    dims = {"nn": NN, "nt": NT, "tn": TN}[mode]
    ia, ib = a.dtype.itemsize, b.dtype.itemsize
    tm = _div_tile(m, LANES, MM_TILE_CAP) if mode == "tn" else _div_tile(m, 16, MM_ROW_CAP)
    tn = _div_tile(n, LANES, min(MM_TILE_CAP, VMEM_BLOCK_BUDGET // (4 * tm)))
    tk = _div_tile(k, 16 if mode == "tn" else LANES,
                   min(MM_ROW_CAP if mode == "tn" else MM_TILE_CAP, VMEM_BLOCK_BUDGET // (tm * ia), VMEM_BLOCK_BUDGET // (tn * ib)))
    nk = k // tk

    def body(a_ref, b_ref, o_ref, *acc):
        part = lax.dot_general(a_ref[...].astype(bf16), b_ref[...].astype(bf16), dims, preferred_element_type=f32)
        if nk == 1:
            o_ref[...] = part.astype(o_ref.dtype)
            return
        kk = pl.program_id(2)

        @pl.when(kk == 0)
        def _():
            acc[0][...] = part

        @pl.when(kk > 0)
        def _():
            acc[0][...] += part

        @pl.when(kk == nk - 1)
        def _():
            o_ref[...] = acc[0][...].astype(o_ref.dtype)

    a_spec = pl.BlockSpec((tk, tm), lambda i, j, kk: (kk, i)) if mode == "tn" else pl.BlockSpec((tm, tk), lambda i, j, kk: (i, kk))
    b_spec = pl.BlockSpec((tn, tk), lambda i, j, kk: (j, kk)) if mode == "nt" else pl.BlockSpec((tk, tn), lambda i, j, kk: (kk, j))
    return pl.pallas_call(
        body, grid=(m // tm, n // tn, nk), in_specs=[a_spec, b_spec],
        out_specs=pl.BlockSpec((tm, tn), lambda i, j, kk: (i, j)),
        out_shape=jax.ShapeDtypeStruct((m, n), out_dtype),
        scratch_shapes=[pltpu.VMEM((tm, tn), f32)] if nk > 1 else [], name=name,
        compiler_params=_cparams("parallel", "parallel", "arbitrary"))(a, b)


def mm(a, b, name, out_dtype=None):
    @jax.custom_vjp
    def op(a, b):
        return _mm_call(a, b, "nn", out_dtype or bf16, name)

    def fwd(a, b):
        return op(a, b), (a, b)

    def bwd(res, g):
        a, b = res
        return _mm_call(g, b, "nt", a.dtype, name + "_da"), _mm_call(a, g, "tn", b.dtype, name + "_db")

    op.defvjp(fwd, bwd)
    return op(a, b)


def split_cols(u, widths):
    offs = [0]
    for w in widths:
        offs.append(offs[-1] + w)

    @jax.custom_vjp
    def op(u):
        return tuple(u[:, offs[i]:offs[i + 1]] for i in range(len(widths)))

    def fwd(u):
        return op(u), None

    def bwd(_, cts):
        return (jnp.concatenate(cts, axis=1),)

    op.defvjp(fwd, bwd)
    return op(u)


def rowwise(fn, rows, consts, pars, out_widths, out_dtypes, name):
    t = rows[0].shape[0]
    tm = ROW_TILE
    nb = t // tm
    nr, nc, npar = len(rows), len(consts), len(pars)

    def rspec(a):
        return pl.BlockSpec((tm, a.shape[1]), lambda i: (i, 0))

    def pspec(a):
        return pl.BlockSpec(a.shape, lambda i: (0,) * a.ndim)

    def call_fwd(rows, consts, pars):
        def body(*refs):
            blk = pl.program_id(0)
            ins = [r[...].astype(f32) for r in refs[:nr + nc]]
            ps = [r[...] for r in refs[nr + nc:nr + nc + npar]]
            outs = fn(blk, *ins, *ps)
            for o_ref, o in zip(refs[nr + nc + npar:], outs):
                o_ref[...] = o.astype(o_ref.dtype)

        return pl.pallas_call(
            body, grid=(nb,),
            in_specs=[rspec(a) for a in rows + consts] + [pspec(a) for a in pars],
            out_specs=[pl.BlockSpec((tm, w), lambda i: (i, 0)) for w in out_widths],
            out_shape=[jax.ShapeDtypeStruct((t, w), d) for w, d in zip(out_widths, out_dtypes)],
            name=name, compiler_params=_cparams("parallel"))(*rows, *consts, *pars)

    def call_bwd(rows, consts, pars, cts):
        nout = len(cts)

        def body(*refs):
            blk = pl.program_id(0)
            ins = [r[...].astype(f32) for r in refs[:nr]]
            cs = [r[...].astype(f32) for r in refs[nr:nr + nc]]
            ps = [r[...] for r in refs[nr + nc:nr + nc + npar]]
            dys = [r[...].astype(f32) for r in refs[nr + nc + npar:nr + nc + npar + nout]]
            d_refs = refs[nr + nc + npar + nout:]
            _, vjp = jax.vjp(lambda *a: tuple(fn(blk, *a[:nr], *cs, *a[nr:])), *ins, *ps)
            grads = vjp(tuple(dys))
            for d_ref, g in zip(d_refs[:nr], grads[:nr]):
                d_ref[...] = g.astype(d_ref.dtype)
            if npar:
                @pl.when(blk == 0)
                def _():
                    for d_ref in d_refs[nr:]:
                        d_ref[...] = jnp.zeros_like(d_ref)

                for d_ref, g in zip(d_refs[nr:], grads[nr:]):
                    d_ref[...] += g

        return pl.pallas_call(
            body, grid=(nb,),
            in_specs=[rspec(a) for a in rows + consts] + [pspec(a) for a in pars] + [rspec(a) for a in cts],
            out_specs=[rspec(a) for a in rows] + [pspec(a) for a in pars],
            out_shape=[jax.ShapeDtypeStruct(a.shape, a.dtype) for a in rows + pars],
            name=name + "_bwd", compiler_params=_cparams("arbitrary"))(*rows, *consts, *pars, *cts)

    @jax.custom_vjp
    def op(rows, consts, pars):
        return tuple(call_fwd(list(rows), list(consts), list(pars)))

    def fwd(rows, consts, pars):
        return op(rows, consts, pars), (rows, consts, pars)

    def bwd(res, cts):
        rows, consts, pars = res
        g = call_bwd(list(rows), list(consts), list(pars), list(cts))
        return tuple(g[:nr]), tuple(jnp.zeros_like(a) for a in consts), tuple(g[nr:])

    op.defvjp(fwd, bwd)
    return op(tuple(rows), tuple(consts), tuple(pars))


def colwise(fn, cols, pars, out_dtype, name):
    t, w = cols[0].shape
    tc = LANES
    nb = w // tc
    ncol, npar = len(cols), len(pars)

    def cspec(a):
        return pl.BlockSpec((a.shape[0], tc), lambda j: (0, j))

    def call_fwd(cols, pars):
        def body(*refs):
            ins = [r[...].astype(f32) for r in refs[:ncol]]
            ps = [r[...] for r in refs[ncol:ncol + npar]]
            refs[-1][...] = fn(*ins, *ps).astype(refs[-1].dtype)

        return pl.pallas_call(
            body, grid=(nb,), in_specs=[cspec(a) for a in cols + pars], out_specs=cspec(cols[0]),
            out_shape=jax.ShapeDtypeStruct((t, w), out_dtype), name=name, compiler_params=_cparams("parallel"))(*cols, *pars)

    def call_bwd(cols, pars, ct):
        def body(*refs):
            ins = [r[...].astype(f32) for r in refs[:ncol]]
            ps = [r[...] for r in refs[ncol:ncol + npar]]
            dy = refs[ncol + npar][...].astype(f32)
            d_refs = refs[ncol + npar + 1:]
            _, vjp = jax.vjp(fn, *ins, *ps)
            grads = vjp(dy)
            for d_ref, g in zip(d_refs, grads):
                d_ref[...] = g.astype(d_ref.dtype)

        return pl.pallas_call(
            body, grid=(nb,), in_specs=[cspec(a) for a in cols + pars + [ct]],
            out_specs=[cspec(a) for a in cols + pars],
            out_shape=[jax.ShapeDtypeStruct(a.shape, a.dtype) for a in cols + pars],
            name=name + "_bwd", compiler_params=_cparams("parallel"))(*cols, *pars, ct)

    @jax.custom_vjp
    def op(cols, pars):
        return call_fwd(list(cols), list(pars))

    def fwd(cols, pars):
        return op(cols, pars), (cols, pars)

    def bwd(res, ct):
        cols, pars = res
        g = call_bwd(list(cols), list(pars), ct)
        return tuple(g[:ncol]), tuple(g[ncol:])

    op.defvjp(fwd, bwd)
    return op(tuple(cols), tuple(pars))


def _sigmoid(x):
    return 1.0 / (1.0 + jnp.exp(-x))


def _silu(x):
    return x * _sigmoid(x)


def _rms(x, g):
    return x * lax.rsqrt(jnp.mean(x * x, axis=-1, keepdims=True) + EPS) * g


def _shift_rows(u, k, n_ctx):
    @jax.custom_vjp
    def op(u):
        t = u.shape[0]
        row = lax.broadcasted_iota(jnp.int32, u.shape, 0)
        edge = ((row == 0) | (row == n_ctx)) if k == 1 else ((row == n_ctx - 1) | (row == t - 1))
        return jnp.where(edge, 0.0, pltpu.roll(u, k % t, 0))

    op.defvjp(lambda u: (op(u), None), lambda _, g: (_shift_rows(g, -k, n_ctx),))
    return op(u)


def _dwconv(u, w0, w1, w2, b, n_ctx):
    return w0 * _shift_rows(u, 1, n_ctx) + w1 * u + w2 * _shift_rows(u, -1, n_ctx) + b


@jax.custom_vjp
def _swap_pairs(x):
    w = x.shape[1]
    lane = lax.broadcasted_iota(jnp.int32, x.shape, 1)
    return jnp.where(lane % 2 == 0, pltpu.roll(x, w - 1, 1), pltpu.roll(x, 1, 1))


_swap_pairs.defvjp(lambda x: (_swap_pairs(x), None), lambda _, g: (_swap_pairs(g),))


def _head_rms(x, g):
    w = x.shape[1]
    same = (lax.broadcasted_iota(jnp.int32, (w, w), 0) // HEAD_DIM) == (lax.broadcasted_iota(jnp.int32, (w, w), 1) // HEAD_DIM)
    ms = jnp.dot(x * x, same.astype(f32), precision=lax.Precision.HIGHEST, preferred_element_type=f32) * (1.0 / HEAD_DIM)
    return x * lax.rsqrt(ms + EPS) * g


def _band_ok(i, j, c0, shape, tq, tk):
    kpos = j * tk + lax.broadcasted_iota(jnp.int32, shape, 0)
    qpos = i * tq + (c0 + lax.broadcasted_iota(jnp.int32, shape, 1)) % tq
    return jnp.abs(qpos - kpos) <= WINDOW


def _sink_row(sink_ref, g, r, tq):
    return jnp.concatenate([jnp.full((1, tq), sink_ref[g * r + h], f32) for h in range(r)], axis=1)


def _heads_to_rows(x, r):
    return jnp.concatenate([x[:, HEAD_DIM * h:HEAD_DIM * (h + 1)] for h in range(r)], axis=0)


def _cols_to_heads(xt, r, tq):
    return jnp.concatenate([xt[:, tq * h:tq * (h + 1)].T for h in range(r)], axis=1)


def _attn_fwd_call(q, k, v, sink, window, name, shards=()):
    nkv, t, dh = k.shape
    h = q.shape[1] // dh
    r = h // nkv
    tq = tk = ROW_TILE
    nb = t // tq
    rows = r * tq
    ns = len(shards)

    assert window or nb % 2 == 1, "the dense schedule takes the kv chunks after the context chunk in pairs"

    def body(sink_ref, q_ref, k_ref, v_ref, *rest):
        x_refs, (o_ref, lse_ref), gathered_refs = rest[:ns], rest[ns:ns + 2], rest[ns + 2:2 * ns + 2]
        m_scr, l_scr, acc_scr, s_a, s_b, p_a, p_b, a_a, a_b = rest[2 * ns + 2:2 * ns + 11]
        comm_sems = rest[2 * ns + 11:]
        g, i = pl.program_id(0), pl.program_id(1)
        if ns:
            @pl.when((g == 0) & (i == 0))
            def _():
                _gather_steps(x_refs, gathered_refs, *comm_sems)[0]()

        qv = _heads_to_rows(q_ref[...], r)
        m_scr[...] = jnp.full_like(m_scr, NEG)
        l_scr[...] = jnp.zeros_like(l_scr)
        acc_scr[...] = jnp.zeros_like(acc_scr)

        def kv_rows(j):
            return pl.ds(pl.multiple_of(jnp.minimum(j, nb - 1) * tk, tk), tk)

        def scores(j, s_scr):
            s_scr[...] = lax.dot_general(k_ref[0, kv_rows(j), :], qv, NT, preferred_element_type=f32)

        def softmax(j, s_scr, p_scr, a_scr, masked):
            for cb in range(rows // ATTN_SLAB):
                cs = slice(cb * ATTN_SLAB, (cb + 1) * ATTN_SLAB)
                s = s_scr[:, cs]
                if masked is not False:
                    s = jnp.where(_band_ok(i, j, cb * ATTN_SLAB, s.shape, tq, tk) & masked, s, NEG)
                m = m_scr[:, cs]
                m2 = jnp.maximum(m, jnp.max(s, axis=0, keepdims=True))
                p = jnp.exp2(s - m2)
                a = jnp.exp2(m - m2)
                l_scr[:, cs] = a * l_scr[:, cs] + jnp.sum(p, axis=0, keepdims=True)
                m_scr[:, cs] = m2
                a_scr[:, cs] = a
                p_scr[:, cs] = p.astype(bf16)

        def weighted_v(j, p_scr, a_scr):
            acc_scr[...] = a_scr[...] * acc_scr[...] + lax.dot_general(v_ref[0, kv_rows(j), :], p_scr[...], TN, preferred_element_type=f32)

        scores(0, s_a)
        softmax(0, s_a, p_a, a_a, False)
        if window:
            cj = [jnp.clip(i + dj, 1, nb - 1) for dj in (-1, 0, 1)]
            on = [i >= 2, i >= 1, (i >= 1) & (i + 1 <= nb - 1)]
            scores(cj[0], s_b)
            weighted_v(0, p_a, a_a)
            softmax(cj[0], s_b, p_b, a_b, on[0])
            scores(cj[1], s_a)
            weighted_v(cj[0], p_b, a_b)
            softmax(cj[1], s_a, p_a, a_a, on[1])
            scores(cj[2], s_b)
            weighted_v(cj[1], p_a, a_a)
            softmax(cj[2], s_b, p_b, a_b, on[2])
            weighted_v(cj[2], p_b, a_b)
        else:
            scores(1, s_b)

            def pair(tt, c):
                j0 = 2 * tt + 1
                scores(j0 + 1, s_a)
                weighted_v(j0 - 1, p_a, a_a)
                softmax(j0, s_b, p_b, a_b, False)
                scores(j0 + 2, s_b)
                weighted_v(j0, p_b, a_b)
                softmax(j0 + 1, s_a, p_a, a_a, False)
                return c

            lax.fori_loop(0, jnp.where(i == 0, 0, (nb - 1) // 2), pair, 0)
            weighted_v(jnp.where(i == 0, 0, nb - 1), p_a, a_a)
        m, l, acc = m_scr[...], l_scr[...], acc_scr[...]
        if window:
            sk = _sink_row(sink_ref, g, r, tq) * LOG2E
            m2 = jnp.maximum(m, sk)
            a = jnp.exp2(m - m2)
            l = a * l + jnp.exp2(sk - m2)
            acc = a * acc
            m = m2
        o_ref[...] = _cols_to_heads(acc / l, r, tq).astype(o_ref.dtype)
        lse_ref[0] = m + jnp.log2(l)
        if ns:
            @pl.when((g == nkv - 1) & (i == nb - 1))
            def _():
                _gather_steps(x_refs, gathered_refs, *comm_sems)[1]()

    qspec = pl.BlockSpec((tq, r * dh), lambda g, i: (i, g))
    kspec = pl.BlockSpec((1, t, dh), lambda g, i: (g, 0, 0))
    hbm = pl.BlockSpec(memory_space=pl.ANY)
    sem = ("arbitrary", "arbitrary") if ns else ("parallel", "parallel")
    return pl.pallas_call(
        body, grid=(nkv, nb),
        in_specs=[pl.BlockSpec(memory_space=pltpu.SMEM), qspec, kspec, kspec] + [hbm] * ns,
        out_specs=[qspec, pl.BlockSpec((1, 1, rows), lambda g, i: (g * nb + i, 0, 0))] + [hbm] * ns,
        out_shape=[jax.ShapeDtypeStruct((t, h * dh), bf16), jax.ShapeDtypeStruct((nkv * nb, 1, rows), f32)]
        + [jax.ShapeDtypeStruct((8,) + s.shape, s.dtype) for s in shards],
        scratch_shapes=[pltpu.VMEM((1, rows), f32), pltpu.VMEM((1, rows), f32), pltpu.VMEM((dh, rows), f32),
                        pltpu.VMEM((tk, rows), f32), pltpu.VMEM((tk, rows), f32), pltpu.VMEM((tk, rows), bf16),
                        pltpu.VMEM((tk, rows), bf16), pltpu.VMEM((1, rows), f32), pltpu.VMEM((1, rows), f32)]
        + (_gather_scratch(ns) if ns else []),
        name=name, compiler_params=_cparams(*sem))(sink, q, k, v, *shards)


def _attn_bwd_call(q, k, v, sink, o, lse, do, window, name, side_sums=()):
    nkv, t, dh = k.shape
    h = q.shape[1] // dh
    r = h // nkv
    tq = tk = ROW_TILE
    nb = t // tq
    rows = r * tq
    ns = len(side_sums)

    def body(sink_ref, q_ref, k_ref, v_ref, o_ref, lse_ref, do_ref, *rest):
        ss_refs, (dq_ref, dk_ref, dv_ref, dsink_ref), got_refs = rest[:ns], rest[ns:ns + 4], rest[ns + 4:2 * ns + 4]
        p_a, p_b, ds_a, ds_b, dq_scr = rest[2 * ns + 4:2 * ns + 9]
        comm_sems = rest[2 * ns + 9:]
        g, i = pl.program_id(0), pl.program_id(1)
        if ns:
            @pl.when((g == 0) & (i == 0))
            def _():
                for cp in _chips_copies(ss_refs, got_refs, *comm_sems):
                    cp.start()

        @pl.when(i == 0)
        def _():
            dk_ref[...] = jnp.zeros_like(dk_ref)
            dv_ref[...] = jnp.zeros_like(dv_ref)

        qv = _heads_to_rows(q_ref[...], r)
        dov = _heads_to_rows(do_ref[...], r)
        lse_t = lse_ref[0]
        delta_t = jnp.sum((dov.astype(f32) * _heads_to_rows(o_ref[...], r).astype(f32)).T, axis=0, keepdims=True)
        dq_scr[...] = jnp.zeros_like(dq_scr)
        q_t, do_t = qv.T, dov.T

        def kv_rows(j):
            return pl.ds(pl.multiple_of(jnp.minimum(j, nb - 1) * tk, tk), tk)

        def grads(j, p_scr, ds_scr):
            dv_ref[0, :, kv_rows(j)] += lax.dot_general(do_t, p_scr[...], NT, preferred_element_type=f32)
            dk_ref[0, :, kv_rows(j)] += lax.dot_general(q_t, ds_scr[...], NT, preferred_element_type=f32)
            dq_scr[...] += lax.dot_general(k_ref[0, kv_rows(j), :], ds_scr[...], TN, preferred_element_type=f32)

        def probs(j, p_scr, ds_scr, masked, before=None):
            s_all = lax.dot_general(k_ref[0, kv_rows(j), :], qv, NT, preferred_element_type=f32)
            dp_all = lax.dot_general(v_ref[0, kv_rows(j), :], dov, NT, preferred_element_type=f32)
            if before is not None:
                grads(*before)
            for cb in range(rows // ATTN_SLAB):
                cs = slice(cb * ATTN_SLAB, (cb + 1) * ATTN_SLAB)
                s = s_all[:, cs]
                if masked is not False:
                    s = jnp.where(_band_ok(i, j, cb * ATTN_SLAB, s.shape, tq, tk) & masked, s, NEG)
                p = jnp.exp2(s - lse_t[:, cs])
                p_scr[:, cs] = p.astype(bf16)
                ds_scr[:, cs] = ((LN2 * p) * (dp_all[:, cs] - delta_t[:, cs])).astype(bf16)

        probs(0, p_a, ds_a, False)
        if window:
            cj = [jnp.clip(i + dj, 1, nb - 1) for dj in (-1, 0, 1)]
            on = [i >= 2, i >= 1, (i >= 1) & (i + 1 <= nb - 1)]
            probs(cj[0], p_b, ds_b, on[0], before=(0, p_a, ds_a))
            probs(cj[1], p_a, ds_a, on[1], before=(cj[0], p_b, ds_b))
            probs(cj[2], p_b, ds_b, on[2], before=(cj[1], p_a, ds_a))
            grads(cj[2], p_b, ds_b)
        else:
            def pair(tt, c):
                j0 = 2 * tt + 1
                probs(j0, p_b, ds_b, False, before=(j0 - 1, p_a, ds_a))
                probs(j0 + 1, p_a, ds_a, False, before=(j0, p_b, ds_b))
                return c

            lax.fori_loop(0, jnp.where(i == 0, 0, (nb - 1) // 2), pair, 0)
            grads(jnp.where(i == 0, 0, nb - 1), p_a, ds_a)
        dq_ref[...] = _cols_to_heads(dq_scr[...], r, tq).astype(dq_ref.dtype)
        if window:
            dsink_ref[0] = -jnp.exp2(_sink_row(sink_ref, g, r, tq) * LOG2E - lse_t) * delta_t
        else:
            dsink_ref[...] = jnp.zeros_like(dsink_ref)
        if ns:
            @pl.when((g == nkv - 1) & (i == nb - 1))
            def _():
                for cp in _chips_copies(ss_refs, got_refs, *comm_sems):
                    cp.wait()

    qspec = pl.BlockSpec((tq, r * dh), lambda g, i: (i, g))
    cspec = pl.BlockSpec((1, 1, rows), lambda g, i: (g * nb + i, 0, 0))
    kspec = pl.BlockSpec((1, t, dh), lambda g, i: (g, 0, 0))
    ktspec = pl.BlockSpec((1, dh, t), lambda g, i: (g, 0, 0))
    hbm = pl.BlockSpec(memory_space=pl.ANY)
    return pl.pallas_call(
        body, grid=(nkv, nb),
        in_specs=[pl.BlockSpec(memory_space=pltpu.SMEM), qspec, kspec, kspec, qspec, cspec, qspec] + [hbm] * ns,
        out_specs=[qspec, ktspec, ktspec, cspec] + [hbm] * ns,
        out_shape=[jax.ShapeDtypeStruct((t, h * dh), bf16), jax.ShapeDtypeStruct((nkv, dh, t), f32), jax.ShapeDtypeStruct((nkv, dh, t), f32),
                   jax.ShapeDtypeStruct((nkv * nb, 1, rows), f32)] + [jax.ShapeDtypeStruct((3,) + s.shape[1:], s.dtype) for s in side_sums],
        scratch_shapes=[pltpu.VMEM((tk, rows), bf16)] * 4 + [pltpu.VMEM((dh, rows), f32)]
        + ([pltpu.SemaphoreType.DMA((3 * ns,)), pltpu.SemaphoreType.DMA((3 * ns,))] if ns else []),
        name=name, compiler_params=_cparams("arbitrary" if ns else "parallel", "arbitrary"))(sink, q, k, v, o, lse, do, *side_sums)


def attention(q, k, v, sink, window, name, shards=(), stand_ins=()):
    @jax.custom_vjp
    def op(q, k, v, sink, shards, stand_ins):
        o, _, *gathered = _attn_fwd_call(q, k, v, sink, window, name, shards)
        return o, tuple(gathered)

    def fwd(q, k, v, sink, shards, stand_ins):
        o, lse, *gathered = _attn_fwd_call(q, k, v, sink, window, name, shards)
        return (o, tuple(gathered)), (q, k, v, sink, o, lse, shards)

    def bwd(res, cts):
        q, k, v, sink, o, lse, shards = res
        do, d_gathered = cts
        side_sums = []
        if shards:
            my_c = lax.axis_index("c").reshape(1).astype(jnp.int32)
            from_sibling = rs_to_sibling(list(d_gathered), name + "_rs_sibling")
            side_sums = [pair_sum(s, rr, my_c, f"{name}_pair_sum{a}") for a, (s, rr) in enumerate(zip(d_gathered, from_sibling))]
        dq, dk, dv, dsink_rows, *from_chips = _attn_bwd_call(q, k, v, sink, o, lse, do, window, name + "_bwd", side_sums)
        nkv, r = k.shape[0], q.shape[1] // (k.shape[0] * k.shape[2])
        dsink = jnp.sum(dsink_rows.reshape(nkv, -1, r, ROW_TILE), axis=(1, 3)).reshape(nkv * r)
        reduced = tuple(jnp.concatenate([s, fc, jnp.zeros_like(s[:1])], axis=0) for s, fc in zip(side_sums, from_chips))
        dk, dv = dk.transpose(0, 2, 1).astype(k.dtype), dv.transpose(0, 2, 1).astype(v.dtype)
        return dq, dk, dv, dsink, tuple(jnp.zeros_like(s) for s in shards), reduced

    op.defvjp(fwd, bwd)
    return op(q, k, v, sink, tuple(shards), tuple(stand_ins))


def _ssd_chunk(xs, dtx, dtr, ac, bs, cs, hin, rev):
    q = xs[0].shape[0]
    ii = lax.broadcasted_iota(jnp.int32, (q, q), 0)
    jj = lax.broadcasted_iota(jnp.int32, (q, q), 1)
    tri = (ii <= jj) if rev else (ii >= jj)
    lo = lax.broadcasted_iota(jnp.int32, (q, LANES), 1) < SSM_P
    lo_row = lax.broadcasted_iota(jnp.int32, (1, LANES), 1) < SSM_P
    heads, slabs, per_group = range(SSM_HEADS), range(SSM_HEADS // 2), SSM_HEADS // 2 // SSM_G

    a = [dtr[h] * ac[h] for h in heads]
    c = [jnp.sum(jnp.where(tri, jnp.broadcast_to(a[h], (q, q)), 0.0), axis=1, keepdims=True) for h in heads]
    tot = [jnp.sum(a[h], axis=1, keepdims=True) for h in heads]
    cf = [jnp.broadcast_to(c[h], (q, q)) for h in heads]
    seg = [jnp.minimum(cf[h] - cf[h].T, 0.0) for h in heads]
    decay = [jnp.where(tri, jnp.exp(seg[h]), 0.0) for h in heads]
    cb = [lax.dot_general(cs[g].astype(bf16), bs[g].astype(bf16), NT, preferred_element_type=f32) for g in range(SSM_G)]
    m = [jnp.concatenate([cb[j // per_group] * decay[2 * j], cb[j // per_group] * decay[2 * j + 1]], axis=1).astype(bf16) for j in slabs]
    xdt = [xs[j] * dtx[j] for j in slabs]
    x2 = [jnp.concatenate([jnp.where(lo, xdt[j], 0.0), jnp.where(lo, 0.0, xdt[j])], axis=0).astype(bf16) for j in slabs]
    y_diag = [jnp.dot(m[j], x2[j], preferred_element_type=f32) for j in slabs]
    csel = [jnp.where(lo, cf[2 * j], cf[2 * j + 1]) for j in slabs]
    tsel = [jnp.where(lo_row, jnp.broadcast_to(tot[2 * j], (1, LANES)), jnp.broadcast_to(tot[2 * j + 1], (1, LANES))) for j in slabs]
    xend = [(xdt[j] * jnp.exp(tsel[j] - csel[j])).astype(bf16) for j in slabs]
    st = [lax.dot_general(bs[j // per_group].astype(bf16), xend[j], TN, preferred_element_type=f32) for j in slabs]
    y_off = [jnp.dot(cs[j // per_group].astype(bf16), hin[j].astype(bf16), preferred_element_type=f32) * jnp.exp(csel[j]) for j in slabs]
    return [y_diag[j] + y_off[j] for j in slabs], [hin[j] * jnp.exp(tsel[j]) + st[j] for j in slabs]


def _ssd_order(s, nc, ncc, rev):
    if not rev:
        return s
    return jnp.where(s < ncc, ncc - 1 - s, nc - 1 - (s - ncc))


SSD_SLABS = [slice(LANES * j, LANES * (j + 1)) for j in range(SSM_HEADS // 2)]
SSD_GROUPS = [slice(SSM_N * g, SSM_N * (g + 1)) for g in range(SSM_G)]


def _head_lanes(w, transpose=False):
    shape = (w, SSM_HEADS) if transpose else (SSM_HEADS, w)
    head = lax.broadcasted_iota(jnp.int32, shape, 1 if transpose else 0)
    lane = lax.broadcasted_iota(jnp.int32, shape, 0 if transpose else 1)
    return (lane // SSM_P == head).astype(f32)


def _ssd_fwd_call(xs, dt, dtr, bm, cm, acol, rev, n_ctx, name):
    t, w = xs.shape
    q = SSM_Q
    nc, ncc = t // q, n_ctx // q

    def body(xs_ref, dt_ref, dtr_ref, b_ref, c_ref, a_ref, y_ref, hin_ref, h_scr):
        @pl.when(pl.program_id(0) == 0)
        def _():
            h_scr[...] = jnp.zeros_like(h_scr)

        hin_ref[0] = h_scr[...]
        dtx = jnp.dot(dt_ref[...], _head_lanes(w), precision=lax.Precision.HIGHEST, preferred_element_type=f32)
        ys, houts = _ssd_chunk([xs_ref[:, sl] for sl in SSD_SLABS], [dtx[:, sl] for sl in SSD_SLABS],
                               [dtr_ref[h:h + 1, :] for h in range(SSM_HEADS)], [a_ref[h:h + 1, :] for h in range(SSM_HEADS)],
                               [b_ref[:, gs] for gs in SSD_GROUPS], [c_ref[:, gs] for gs in SSD_GROUPS],
                               [h_scr[:, sl] for sl in SSD_SLABS], rev)
        for sl, y, hout in zip(SSD_SLABS, ys, houts):
            y_ref[:, sl] = y.astype(y_ref.dtype)
            h_scr[:, sl] = hout

    def at(s):
        return _ssd_order(s, nc, ncc, rev)

    return pl.pallas_call(
        body, grid=(nc,),
        in_specs=[pl.BlockSpec((q, w), lambda s: (at(s), 0)), pl.BlockSpec((q, SSM_HEADS), lambda s: (at(s), 0)),
                  pl.BlockSpec((SSM_HEADS, q), lambda s: (0, at(s))),
                  pl.BlockSpec((q, SSM_BC), lambda s: (at(s), 0)), pl.BlockSpec((q, SSM_BC), lambda s: (at(s), 0)),
                  pl.BlockSpec((SSM_HEADS, 1), lambda s: (0, 0))],
        out_specs=[pl.BlockSpec((q, w), lambda s: (at(s), 0)), pl.BlockSpec((1, SSM_N, w), lambda s: (s, 0, 0))],
        out_shape=[jax.ShapeDtypeStruct((t, w), xs.dtype), jax.ShapeDtypeStruct((nc, SSM_N, w), f32)],
        scratch_shapes=[pltpu.VMEM((SSM_N, w), f32)],
        name=name, compiler_params=_cparams("arbitrary"))(xs, dt, dtr, bm, cm, acol)


def _ssd_bwd_call(xs, dt, dtr, bm, cm, acol, hin, dy, rev, n_ctx, name):
    t, w = xs.shape
    q = SSM_Q
    nc, ncc = t // q, n_ctx // q

    def body(xs_ref, dt_ref, dtr_ref, b_ref, c_ref, a_ref, hin_ref, dy_ref,
             dxs_ref, ddt_ref, ddtr_ref, db_ref, dc_ref, da_ref, dh_scr):
        @pl.when(pl.program_id(0) == 0)
        def _():
            dh_scr[...] = jnp.zeros_like(dh_scr)
            da_ref[...] = jnp.zeros_like(da_ref)

        dtx = jnp.dot(dt_ref[...], _head_lanes(w), precision=lax.Precision.HIGHEST, preferred_element_type=f32)
        _, vjp = jax.vjp(
            functools.partial(_ssd_chunk, rev=rev),
            [xs_ref[:, sl].astype(f32) for sl in SSD_SLABS], [dtx[:, sl] for sl in SSD_SLABS],
            [dtr_ref[h:h + 1, :] for h in range(SSM_HEADS)], [a_ref[h:h + 1, :] for h in range(SSM_HEADS)],
            [b_ref[:, gs].astype(f32) for gs in SSD_GROUPS], [c_ref[:, gs].astype(f32) for gs in SSD_GROUPS],
            [hin_ref[0, :, sl] for sl in SSD_SLABS])
        dxs, ddtx, ddtr, dac, dbs, dcs, dhin = vjp(([dy_ref[:, sl].astype(f32) for sl in SSD_SLABS], [dh_scr[:, sl] for sl in SSD_SLABS]))
        for j, sl in enumerate(SSD_SLABS):
            dxs_ref[:, sl] = dxs[j].astype(dxs_ref.dtype)
            dh_scr[:, sl] = dhin[j]
        for h in range(SSM_HEADS):
            ddtr_ref[h:h + 1, :] = ddtr[h]
            da_ref[h:h + 1, :] += dac[h]
        for g, gs in enumerate(SSD_GROUPS):
            db_ref[:, gs] = dbs[g].astype(db_ref.dtype)
            dc_ref[:, gs] = dcs[g].astype(dc_ref.dtype)
        ddt_ref[...] = jnp.dot(jnp.concatenate(ddtx, axis=1), _head_lanes(w, transpose=True),
                               precision=lax.Precision.HIGHEST, preferred_element_type=f32)

    def step(s):
        return nc - 1 - s

    def at(s):
        return _ssd_order(step(s), nc, ncc, rev)

    row = lambda wd: pl.BlockSpec((q, wd), lambda s: (at(s), 0))
    dtr_spec = pl.BlockSpec((SSM_HEADS, q), lambda s: (0, at(s)))
    a_spec = pl.BlockSpec((SSM_HEADS, 1), lambda s: (0, 0))
    return pl.pallas_call(
        body, grid=(nc,),
        in_specs=[row(w), row(SSM_HEADS), dtr_spec, row(SSM_BC), row(SSM_BC), a_spec,
                  pl.BlockSpec((1, SSM_N, w), lambda s: (step(s), 0, 0)), row(w)],
        out_specs=[row(w), row(SSM_HEADS), dtr_spec, row(SSM_BC), row(SSM_BC), a_spec],
        out_shape=[jax.ShapeDtypeStruct((t, w), xs.dtype), jax.ShapeDtypeStruct(dt.shape, f32), jax.ShapeDtypeStruct(dtr.shape, f32),
                   jax.ShapeDtypeStruct(bm.shape, bm.dtype), jax.ShapeDtypeStruct(cm.shape, cm.dtype), jax.ShapeDtypeStruct(acol.shape, f32)],
        scratch_shapes=[pltpu.VMEM((SSM_N, w), f32)],
        name=name, compiler_params=_cparams("arbitrary"))(xs, dt, dtr, bm, cm, acol, hin, dy)


def ssd_scan(xs, dt, dtr, bm, cm, acol, rev, n_ctx, name):
    @jax.custom_vjp
    def op(xs, dt, dtr, bm, cm, acol):
        return _ssd_fwd_call(xs, dt, dtr, bm, cm, acol, rev, n_ctx, name)[0]

    def fwd(xs, dt, dtr, bm, cm, acol):
        y, hin = _ssd_fwd_call(xs, dt, dtr, bm, cm, acol, rev, n_ctx, name)
        return y, (xs, dt, dtr, bm, cm, acol, hin)

    def bwd(res, dy):
        return tuple(_ssd_bwd_call(*res, dy, rev, n_ctx, name + "_bwd"))

    op.defvjp(fwd, bwd)
    return op(xs, dt, dtr, bm, cm, acol)


def _in_layout(d):
    return [('a_q', Q_W), ('a_k', KV_W), ('a_v', KV_W), ('b_z', SSM_INNER), ('b_xbc', SSM_INNER + 2 * SSM_BC), ('b_dt', DT_W),
            ('c_q', Q_W), ('c_k', KV_W), ('c_v', KV_W), ('g_a', d), ('g_b', d), ('g_c', d)]


def _dt_span(d):
    start = 0
    for name, n in _in_layout(d):
        if name == 'b_dt':
            return start, start + n
        start += n


@jax.custom_vjp
def _w_in_split(w):
    lo, hi = _dt_span(w.shape[0])
    dt = jnp.concatenate([w[:, lo:hi], jnp.zeros((w.shape[0], DT_PAD - (hi - lo)), w.dtype)], axis=1)
    return jnp.concatenate([w[:, :lo], w[:, hi:]], axis=1), dt


def _w_in_join(g_main, g_dt):
    lo, hi = _dt_span(g_main.shape[0])
    return jnp.concatenate([g_main[:, :lo], g_dt[:, :hi - lo], g_main[:, lo:]], axis=1)


_w_in_split.defvjp(lambda w: (_w_in_split(w), None), lambda _, g: (_w_in_join(*g),))


def _rope_tables(n_ctx, n_lat):
    rows = n_lat // GRID_W
    t_row = jnp.repeat(jnp.arange(rows), GRID_W).astype(f32)
    t_col = jnp.tile(jnp.arange(GRID_W), rows).astype(f32)
    n = HEAD_DIM // 4
    inv = ROPE_BASE ** (-jnp.arange(n, dtype=f32) / n)
    ang = jnp.concatenate([t_row[:, None] * inv, t_col[:, None] * inv], axis=-1)
    cos = jnp.concatenate([jnp.ones((n_ctx, HEAD_DIM // 2), f32), jnp.cos(ang)], axis=0)
    sin = jnp.concatenate([jnp.zeros((n_ctx, HEAD_DIM // 2), f32), jnp.sin(ang)], axis=0)
    return jnp.repeat(cos, 2, axis=1), jnp.stack([-sin, sin], axis=-1).reshape(sin.shape[0], HEAD_DIM)


def _heads_major(a, n_heads):
    return a.reshape(a.shape[0], n_heads, HEAD_DIM).transpose(1, 0, 2)


def _layer(xall, w, s, cm, tabs, n_ctx, li, gather_a, gather_c):
    t, d = xall.shape
    ncb = n_ctx // ROW_TILE
    nm = f"l{li}_"
    ctq, stq, ctk, stk = tabs
    def mod(blk, cmv, i):
        return jnp.where(blk < ncb, cmv[0:1, i * d:(i + 1) * d], cmv[1:2, i * d:(i + 1) * d])

    def norm_mod(blk, x, g, cmv):
        return (_rms(x, g) * (1.0 + mod(blk, cmv, 1)) + mod(blk, cmv, 0),)

    (h,) = rowwise(norm_mod, [xall], [], [s['norm1'][None], cm], [d], [bf16], nm + "norm1")
    w_main, w_dt = _w_in_split(w['w_in'])
    u = mm(h, w_main, nm + "in")
    b_dt = mm(h, w_dt, nm + "in_dt", f32)
    a_q, a_k, a_v, b_z, b_xbc, c_q, c_k, c_v, g_a, g_b, g_c = split_cols(u, [n for name, n in _in_layout(d) if name != 'b_dt'])

    def rope(blk, q, k, v, ct_q, st_q, ct_k, st_k):
        return q * ct_q + _swap_pairs(q) * st_q, k * ct_k + _swap_pairs(k) * st_k, v

    def norm_rope(blk, q, k, v, ct_q, st_q, ct_k, st_k, gq, gk):
        return rope(blk, _head_rms(q, gq), _head_rms(k, gk), v, ct_q, st_q, ct_k, st_k)

    qkv_w, qkv_t = [Q_W, KV_W, KV_W], [bf16, bf16, bf16]
    qa, ka, va = rowwise(rope, [a_q, a_k, a_v], [ctq, stq, ctk, stk], [], qkv_w, qkv_t, nm + "ropeA")
    gq = jnp.tile(s['c_q_norm'], N_HEADS)[None]
    gk = jnp.tile(s['c_k_norm'], N_KV)[None]
    qc, kc, vc = rowwise(norm_rope, [c_q, c_k, c_v], [ctq, stq, ctk, stk], [gq, gk], qkv_w, qkv_t, nm + "ropeC")
    ya, mine = attention(qa, _heads_major(ka, N_KV), _heads_major(va, N_KV), s['a_sink'], True, nm + "attnA", *gather_a[1:])
    yc, nxt = attention(qc, _heads_major(kc, N_KV), _heads_major(vc, N_KV), jnp.zeros((N_HEADS,), f32), False,
                        nm + "attnC", *gather_c[1:])
    w = {**w, **{n: _assemble(n, g) for n, g in zip(gather_a[0], mine)}}
    w_next = {n: _assemble(n, g) for n, g in zip(gather_c[0], nxt)}

    cw, cb = s['ssm_conv_w'], s['ssm_conv_b']
    conv_silu = lambda uu, w3, b: _silu(_dwconv(uu, w3[0:1], w3[1:2], w3[2:3], b, n_ctx))
    xbc = colwise(conv_silu, [b_xbc], [cw, cb[None]], bf16, nm + "ssmconv")
    xs, bm, cmat = split_cols(xbc, [SSM_INNER, SSM_BC, SSM_BC])
    bias = jnp.concatenate([s['ssm_dt_bias'].reshape(1, DT_W), jnp.zeros((1, DT_PAD - DT_W), f32)], axis=1)

    def softplus(blk, r, b):
        z = r + b
        return (jnp.maximum(z, 0.0) + jnp.log(1.0 + jnp.exp(-jnp.abs(z))),)

    (dt_all,) = rowwise(softplus, [b_dt], [], [bias], [DT_PAD], [f32], nm + "dt")
    a_coef = -jnp.exp(s['ssm_A_log'])
    ys_dir = []
    for di, rev in enumerate((False, True)):
        dt = dt_all[:, di * SSM_HEADS:(di + 1) * SSM_HEADS]
        ys_dir.append(ssd_scan(xs, dt, dt.T, bm, cmat, a_coef[di][:, None], rev, n_ctx,
                               nm + ("ssd_r" if rev else "ssd_f")))

    def ssm_out(blk, yf, yb, x, z, dskip, g):
        return (_rms((yf + yb + x * dskip) * _silu(z), g),)

    (ysn,) = rowwise(ssm_out, [ys_dir[0], ys_dir[1], xs, b_z], [], [jnp.repeat(s['ssm_D'], SSM_P)[None], s['ssm_norm'][None]],
                     [SSM_INNER], [bf16], nm + "ssmout")

    pa, pb, pc = mm(ya, w['w_oa'], nm + "oa"), mm(ysn, w['w_ob'], nm + "ob"), mm(yc, w['w_oc'], nm + "oc")

    def merge(blk, ga, gb, gc, a, b, c):
        return (_sigmoid(ga) * a + _sigmoid(gb) * b + _sigmoid(gc) * c,)

    (mrg,) = rowwise(merge, [g_a, g_b, g_c, pa, pb, pc], [], [], [d], [bf16], nm + "merge")
    o = mm(mrg, w['w_out'], nm + "out")

    def resid_norm_mod(blk, x, oo, g, cmv):
        x1 = x + mod(blk, cmv, 2) * oo
        return x1, _rms(x1, g) * (1.0 + mod(blk, cmv, 4)) + mod(blk, cmv, 3)

    x1, h2 = rowwise(resid_norm_mod, [xall, o], [], [s['norm2'][None], cm], [d, d], [f32, bf16], nm + "norm2")
    up, gt = mm(h2, w['ffn_w_up'], nm + "up"), mm(h2, w['ffn_w_gate'], nm + "gate")
    fw, fb = s['ffn_conv_w'], s['ffn_conv_b']
    ffn_act = lambda g_, u_, w3, b: _silu(_dwconv(g_, w3[0:1], w3[1:2], w3[2:3], b, n_ctx)) * u_
    act = colwise(ffn_act, [gt, up], [fw, fb[None]], bf16, nm + "ffnact")
    f = mm(act, w['ffn_w_down'], nm + "down")

    def resid(blk, x, ff, cmv):
        return (x + mod(blk, cmv, 5) * ff,)

    (x2,) = rowwise(resid, [x1, f], [], [cm], [d], [f32], nm + "resid")
    return x2, w_next


def _assemble(name, gathered):
    if BIG[name] == 0:
        return gathered.reshape(-1, gathered.shape[-1])
    return jnp.concatenate([gathered[j] for j in range(8)], axis=1)


def _loss_fn(big0, shards0_late, stand_ins0, shards1, stand_ins1, small, x, ctx, c, target, n_ctx):
    n_lat, d = x.shape
    xall = jnp.concatenate([ctx, x], axis=0)
    ct, st = _rope_tables(n_ctx, n_lat)
    q_scale = HEAD_DIM ** -0.5 * LOG2E
    tabs = (jnp.tile(ct, (1, N_HEADS)) * q_scale, jnp.tile(st, (1, N_HEADS)) * q_scale,
            jnp.tile(ct, (1, N_KV)), jnp.tile(st, (1, N_KV)))
    srows = jnp.concatenate([_silu(small['c_ctx'])[None], _silu(c), jnp.zeros((14, d), f32)], axis=0)
    names = list(BIG)
    big = big0
    gather_a = (LATE, [shards0_late[n] for n in LATE], [stand_ins0[n] for n in LATE])
    gather_c = (names, [shards1[n] for n in names], [stand_ins1[n] for n in names])
    for li in range(2):
        cm = mm(srows, big['w_mod'], f"l{li}_mod", f32)[0:2] + small['b_mod'][li][None]
        sl = {k: v[li] for k, v in small.items() if k not in ('c_ctx', 'final_norm')}
        xall, big = _layer(xall, big, sl, cm, tabs, n_ctx, li, gather_a, gather_c)
        gather_a = gather_c = ((), (), ())
    ncb = n_ctx // ROW_TILE
    tgt = jnp.concatenate([jnp.zeros((n_ctx, d), f32), target], axis=0)

    def loss_rows(blk, xx, tg, g):
        e = _rms(xx, g) - tg
        return (jnp.where(blk < ncb, 0.0, 0.5) * jnp.mean(e * e, axis=-1, keepdims=True),)

    (rows,) = rowwise(loss_rows, [xall], [tgt], [small['final_norm'][None]], [1], [f32], "loss")
    return jnp.sum(rows)


def _hbm_call(body, ins, out_shapes, n_sems, name):
    any_spec = pl.BlockSpec(memory_space=pl.ANY)
    return pl.pallas_call(
        body, out_shape=out_shapes, in_specs=[any_spec] * len(ins), out_specs=[any_spec] * len(out_shapes),
        scratch_shapes=[pltpu.SemaphoreType.DMA((n_sems,)), pltpu.SemaphoreType.DMA((n_sems,)), pltpu.SemaphoreType.DMA((len(ins),))],
        name=name)(*ins)


def _gather_steps(x_refs, out_refs, send_sems, recv_sems, local_sems):
    n = len(x_refs)
    x, y, c = lax.axis_index("x"), lax.axis_index("y"), lax.axis_index("c")
    me, sibling = (x, y, c), (x, y, 1 - c)
    chips = [(1 - x, y), (x, 1 - y), (1 - x, 1 - y)]

    def copy(a, k, block, to, src=None):
        px, py, pc = block
        slot = out_refs[a].at[4 * px + 2 * py + pc]
        return pltpu.make_async_remote_copy(
            src_ref=slot if src is None else src, dst_ref=slot,
            send_sem=send_sems.at[7 * a + k], recv_sem=recv_sems.at[7 * a + k], device_id=to, device_id_type=MESH)

    mine = [pltpu.make_async_copy(x_refs[a], out_refs[a].at[4 * x + 2 * y + c], local_sems.at[a]) for a in range(n)]
    first = []
    for a in range(n):
        first += [copy(a, 1 + j, me, (*chip, c), src=x_refs[a]) for j, chip in enumerate(chips)]
        first.append(copy(a, 0, me, sibling, src=x_refs[a]))

    def start():
        for cp in mine + first:
            cp.start()

    def finish():
        passed = []
        for a in range(n):
            for j, chip in enumerate(chips):
                copy(a, 1 + j, (*chip, c), me).wait_recv()
                passed.append(copy(a, 4 + j, (*chip, c), sibling))
                passed[-1].start()
        for a in range(n):
            copy(a, 0, sibling, me).wait_recv()
            for j, chip in enumerate(chips):
                copy(a, 4 + j, (*chip, 1 - c), me).wait_recv()
        for cp in first + passed:
            cp.wait_send()
        for cp in mine:
            cp.wait()

    return start, finish


def _gather_scratch(n):
    return [pltpu.SemaphoreType.DMA((7 * n,)), pltpu.SemaphoreType.DMA((7 * n,)), pltpu.SemaphoreType.DMA((n,))]


def all_gather(shards, name):
    n = len(shards)

    def body(*refs):
        start, finish = _gather_steps(refs[:n], refs[n:2 * n], *refs[2 * n:])
        start()
        finish()

    return _hbm_call(body, shards, [jax.ShapeDtypeStruct((8,) + s.shape, s.dtype) for s in shards], 7 * n, name)


def rs_to_sibling(gs, name="rs_sibling"):
    n = len(gs)

    def body(*refs):
        g_refs, out_refs, (send_sems, recv_sems, _) = refs[:n], refs[n:2 * n], refs[2 * n:]
        x, y, c = lax.axis_index("x"), lax.axis_index("y"), lax.axis_index("c")
        copies = [pltpu.make_async_remote_copy(
            src_ref=g_refs[a].at[2 * k + (1 - c)], dst_ref=out_refs[a].at[k], send_sem=send_sems.at[4 * a + k],
            recv_sem=recv_sems.at[4 * a + k], device_id=(x, y, 1 - c), device_id_type=MESH) for a in range(n) for k in range(4)]
        for cp in copies:
            cp.start()
        for cp in copies:
            cp.wait()

    return _hbm_call(body, gs, [jax.ShapeDtypeStruct((4,) + g.shape[1:], g.dtype) for g in gs], 4 * n, name)


def rs_to_chips(ss):
    n = len(ss)

    def body(*refs):
        copies = _chips_copies(refs[:n], refs[n:2 * n], refs[2 * n], refs[2 * n + 1])
        for cp in copies:
            cp.start()
        for cp in copies:
            cp.wait()

    return _hbm_call(body, ss, [jax.ShapeDtypeStruct((3,) + s.shape[1:], s.dtype) for s in ss], 3 * n, "rs_chips")


def _chips_copies(s_refs, out_refs, send_sems, recv_sems):
    x, y, c = lax.axis_index("x"), lax.axis_index("y"), lax.axis_index("c")
    copies = []
    for a in range(len(s_refs)):
        for k, (fx, fy) in enumerate([(1, 0), (0, 1), (1, 1)]):
            px, py = (1 - x) if fx else x, (1 - y) if fy else y
            copies.append(pltpu.make_async_remote_copy(
                src_ref=s_refs[a].at[2 * px + py], dst_ref=out_refs[a].at[k], send_sem=send_sems.at[3 * a + k],
                recv_sem=recv_sems.at[3 * a + k], device_id=(px, py, c), device_id_type=MESH))
    return copies


def _flat_tile(rows, cols):
    return _row_tile(rows, 4 * 4 * cols)


def pair_sum(g, r1, my_c, name):
    _, rows, cols = g.shape
    tm = _flat_tile(rows, cols)

    def body(c_ref, g_ref, r_ref, o_ref):
        o_ref[...] = (g_ref[...].astype(f32) + r_ref[...].astype(f32)).astype(o_ref.dtype)

    return pl.pallas_call(
        body, grid_spec=pltpu.PrefetchScalarGridSpec(
            num_scalar_prefetch=1, grid=(4, rows // tm),
            in_specs=[pl.BlockSpec((1, tm, cols), lambda k, i, c: (2 * k + c[0], i, 0)),
                      pl.BlockSpec((1, tm, cols), lambda k, i, c: (k, i, 0))],
            out_specs=pl.BlockSpec((1, tm, cols), lambda k, i, c: (k, i, 0))),
        out_shape=jax.ShapeDtypeStruct((4, rows, cols), g.dtype), name=name,
        compiler_params=_cparams("parallel", "parallel"))(my_c, g, r1)


def _adam_math(w, g, m, v):
    m2 = ADAM_B1 * m + (1.0 - ADAM_B1) * g
    v2 = ADAM_B2 * v + (1.0 - ADAM_B2) * (g * g)
    m_hat = m2 / (1.0 - ADAM_B1 ** ADAM_STEP)
    v_hat = v2 / (1.0 - ADAM_B2 ** ADAM_STEP)
    return -ADAM_LR * (m_hat / (jnp.sqrt(v_hat) + ADAM_EPS) + ADAM_WD * w), m2, v2


def sum_adam(parts, w, m, v, name):
    groups, rows, cols = w.shape
    tm = _flat_tile(rows, cols)
    nblk = rows // tm
    flat = []
    scalars = [p[2] for ps in parts for p in ps if p[2] is not None]
    for gi, ps in enumerate(parts):
        flat.append([])
        for arr, static_rows, dyn in ps:
            if dyn is not None:
                flat[gi].append((arr, functools.partial(lambda l, i, s, gi: (s[0], jnp.where(l == gi, i, nblk - 1), 0), gi=gi)))
            else:
                for k in static_rows:
                    flat[gi].append((arr, functools.partial(lambda l, i, s, gi, k: (k, jnp.where(l == gi, i, nblk - 1), 0), gi=gi, k=k)))
    counts = [len(f) for f in flat]
    na = sum(counts)

    def body(s_ref, *refs):
        sums, at = [], 0
        for cnt in counts:
            g = refs[at][0].astype(f32)
            for r in refs[at + 1:at + cnt]:
                g = g + r[0].astype(f32)
            sums.append(g)
            at += cnt
        g = sums[0]
        for gi in range(1, groups):
            g = jnp.where(pl.program_id(0) == gi, sums[gi], g)
        w_ref, m_ref, v_ref = refs[na:na + 3]
        g_out, d_out, m_out, v_out = refs[na + 3:]
        d, m2, v2 = _adam_math(w_ref[0], g, m_ref[0], v_ref[0])
        g_out[0] = g
        d_out[0] = d
        m_out[0] = m2
        v_out[0] = v2

    blk = pl.BlockSpec((1, tm, cols), lambda l, i, s: (l, i, 0))
    scalar = scalars[0] if scalars else jnp.zeros((1,), jnp.int32)
    return pl.pallas_call(
        body, grid_spec=pltpu.PrefetchScalarGridSpec(
            num_scalar_prefetch=1, grid=(groups, nblk),
            in_specs=[pl.BlockSpec((1, tm, cols), im) for f in flat for _, im in f] + [blk, blk, blk],
            out_specs=[blk, blk, blk, blk]),
        out_shape=[jax.ShapeDtypeStruct((groups, rows, cols), f32)] * 4, name=name,
        compiler_params=_cparams("arbitrary", "arbitrary"))(scalar, *[a for f in flat for a, _ in f], w, m, v)


FLAT_COLS = 1024


def _to_flat(vec):
    n = vec.shape[0]
    total = -(-n // (8 * FLAT_COLS)) * 8 * FLAT_COLS
    return jnp.concatenate([vec, jnp.zeros((total - n,), vec.dtype)]).reshape(-1, FLAT_COLS)


def _pack(tree, names):
    return jnp.concatenate([tree[n].reshape(-1) for n in names])


def _unpack(vec, like, names):
    out, off = {}, 0
    for n in names:
        size = like[n].size
        out[n] = vec[off:off + size].reshape(like[n].shape)
        off += size
    return out


def kernel(x, c, ctx, c_ctx, w_mod, b_mod, norm1, norm2, w_in, a_sink, ssm_conv_w, ssm_conv_b, ssm_A_log, ssm_dt_bias, ssm_D, ssm_norm, c_q_norm, c_k_norm, w_oa, w_ob, w_oc, w_out, ffn_w_up, ffn_w_gate, ffn_conv_w, ffn_conv_b, ffn_w_down, final_norm, loss_target, m_c_ctx, m_w_mod, m_b_mod, m_norm1, m_norm2, m_w_in, m_a_sink, m_ssm_conv_w, m_ssm_conv_b, m_ssm_A_log, m_ssm_dt_bias, m_ssm_D, m_ssm_norm, m_c_q_norm, m_c_k_norm, m_w_oa, m_w_ob, m_w_oc, m_w_out, m_ffn_w_up, m_ffn_w_gate, m_ffn_conv_w, m_ffn_conv_b, m_ffn_w_down, m_final_norm, v_c_ctx, v_w_mod, v_b_mod, v_norm1, v_norm2, v_w_in, v_a_sink, v_ssm_conv_w, v_ssm_conv_b, v_ssm_A_log, v_ssm_dt_bias, v_ssm_D, v_ssm_norm, v_c_q_norm, v_c_k_norm, v_w_oa, v_w_ob, v_w_oc, v_w_out, v_ffn_w_up, v_ffn_w_gate, v_ffn_conv_w, v_ffn_conv_b, v_ffn_w_down, v_final_norm):
    args = (x, c, ctx, c_ctx, w_mod, b_mod, norm1, norm2, w_in, a_sink, ssm_conv_w, ssm_conv_b, ssm_A_log, ssm_dt_bias, ssm_D, ssm_norm, c_q_norm, c_k_norm, w_oa, w_ob, w_oc, w_out, ffn_w_up, ffn_w_gate, ffn_conv_w, ffn_conv_b, ffn_w_down, final_norm, loss_target)
    moms = (m_c_ctx, m_w_mod, m_b_mod, m_norm1, m_norm2, m_w_in, m_a_sink, m_ssm_conv_w, m_ssm_conv_b, m_ssm_A_log, m_ssm_dt_bias, m_ssm_D, m_ssm_norm, m_c_q_norm, m_c_k_norm, m_w_oa, m_w_ob, m_w_oc, m_w_out, m_ffn_w_up, m_ffn_w_gate, m_ffn_conv_w, m_ffn_conv_b, m_ffn_w_down, m_final_norm)
    vars_ = (v_c_ctx, v_w_mod, v_b_mod, v_norm1, v_norm2, v_w_in, v_a_sink, v_ssm_conv_w, v_ssm_conv_b, v_ssm_A_log, v_ssm_dt_bias, v_ssm_D, v_ssm_norm, v_c_q_norm, v_c_k_norm, v_w_oa, v_w_ob, v_w_oc, v_w_out, v_ffn_w_up, v_ffn_w_gate, v_ffn_conv_w, v_ffn_conv_b, v_ffn_w_down, v_final_norm)
    p = dict(zip(IN_NAMES, args))
    mom = dict(zip(WEIGHTS, moms))
    var = dict(zip(WEIGHTS, vars_))
    depth = w_in.shape[0]
    n_ctx = ctx.shape[1]
    xi, yi, ci = lax.axis_index("x"), lax.axis_index("y"), lax.axis_index("c")
    dev = 4 * xi + 2 * yi + ci
    big_names = list(BIG)

    assert depth == 2 and n_ctx == ROW_TILE
    shards = [{n: p[n][li].astype(bf16) for n in big_names} for li in range(depth)]
    g_early = all_gather([shards[0][n] for n in EARLY], "gather_l0")
    g_conv = all_gather([_to_flat(_pack(p, CONV_W))], "gather_conv")[0].reshape(8, -1)
    big0 = {n: _assemble(n, g) for n, g in zip(EARLY, g_early)}
    stand_ins = [{n: jnp.zeros((8,) + shards[li][n].shape, bf16) for n in (LATE, big_names)[li]} for li in range(depth)]
    conv_full, off = {}, 0
    for n in CONV_W:
        shp = p[n].shape
        seg = g_conv[:, off:off + p[n].size].reshape(8, *shp)
        conv_full[n] = jnp.moveaxis(seg, 0, -2).reshape(*shp[:-1], 8 * shp[-1])
        off += p[n].size
    small = {n: p[n] for n in REPL}
    small.update(conv_full)

    loss, (g_early, g_late0, g_big1, g_small, g_x) = jax.value_and_grad(_loss_fn, argnums=(0, 2, 4, 5, 6))(
        big0, {n: shards[0][n] for n in LATE}, stand_ins[0], shards[1], stand_ins[1], small, x[0], ctx[0], c, loss_target[0], n_ctx)
    loss = lax.psum(loss, AXES)

    def send_rows(n):
        b = p[n].shape[-1]
        return jnp.stack([g_early[n][:, b * j:b * (j + 1)] for j in range(8)])

    send = [send_rows(n) for n in EARLY]
    from_sibling = rs_to_sibling(send)
    my_c = ci.reshape(1).astype(jnp.int32)
    side_sum = [pair_sum(s, r, my_c, "rs_pair_sum_" + n) for n, s, r in zip(EARLY, send, from_sibling)]
    from_chips = rs_to_chips(side_sum)
    chip = (2 * xi + yi).reshape(1).astype(jnp.int32)
    big_out = [{}, {}, {}, {}]
    for n in big_names:
        if n in EARLY:
            a = EARLY.index(n)
            parts = [[(side_sum[a], None, chip), (from_chips[a], (0, 1, 2), None)]]
        else:
            parts = [[(g_late0[n], None, chip), (g_late0[n], (4, 5, 6), None)]]
        parts.append([(g_big1[n], None, chip), (g_big1[n], (4, 5, 6), None)])
        outs = sum_adam(parts, p[n], mom[n], var[n], "adam_" + n)
        for k in range(4):
            big_out[k][n] = outs[k]

    sm_names = REPL + list(CONV_W)
    g_vec = _to_flat(_pack(g_small, sm_names))
    gathered = all_gather([g_vec], "gather_small_grads")[0]
    n_repl = sum(p[n].size for n in REPL)

    def repl_flat(tree):
        return _to_flat(jnp.concatenate([_pack(tree, REPL), jnp.zeros((g_vec.size - n_repl,), f32)]))

    outs_small = sum_adam([[(gathered, tuple(range(8)), None)]], repl_flat(p)[None], repl_flat(mom)[None], repl_flat(var)[None],
                          "adam_small")
    g_sum = outs_small[0].reshape(-1)
    small_out = [_unpack(o.reshape(-1), p, REPL) for o in outs_small]
    conv_g_full = _unpack(g_sum[n_repl:], conv_full, CONV_W)
    conv_g = {n: lax.dynamic_slice_in_dim(conv_g_full[n], dev * p[n].shape[-1], p[n].shape[-1], axis=2) for n in CONV_W}
    conv_gv = _to_flat(_pack(conv_g, CONV_W))
    outs_conv = sum_adam([[(conv_gv[None], (0,), None)]], _to_flat(_pack(p, CONV_W))[None], _to_flat(_pack(mom, CONV_W))[None],
                         _to_flat(_pack(var, CONV_W))[None], "adam_conv")
    conv_out = [_unpack(o.reshape(-1), p, CONV_W) for o in outs_conv]

    res = []
    for k in range(4):
        tree = {**big_out[k], **small_out[k], **conv_out[k]}
        res.append([tree[n] for n in WEIGHTS])
    return (loss, g_x[None], *res[0], *res[1], *res[2], *res[3])
```
